```python
import math
import jax, jax.numpy as jnp
from jax import lax
import numpy as np

D_MODEL = 1024
BATCH = 8
SEQ = 2048
DEPTH = 2

ATTN_WIDTH = D_MODEL // 2
SSM_WIDTH = D_MODEL // 2
HEAD_DIM = 64
N_Q_HEADS = ATTN_WIDTH // HEAD_DIM
N_KV_HEADS = 2
Q_PER_KV = N_Q_HEADS // N_KV_HEADS
KV_WIDTH = N_KV_HEADS * HEAD_DIM
WINDOW = 128
ATTN_BLOCK = 128
ROPE_THETA = 10000.0

SSM_GROUP_CH = 16
SSM_GROUPS = SSM_WIDTH // SSM_GROUP_CH
SSM_STATE = 64

IN_WIDTH = ATTN_WIDTH + 2 * KV_WIDTH + SSM_WIDTH

N_EXPERTS = 16
N_EXPERT_GROUPS = 4
EXPERTS_PER_GROUP = N_EXPERTS // N_EXPERT_GROUPS
TOP_K = 2
D_FF_EXPERT = D_MODEL // 2

EPS = 1e-6

kernel_name = "hymba_swa_s5_grouped_moe_adaln"


def rmsnorm(x, g):
    xf = x.astype(jnp.float32)
    y = xf * lax.rsqrt(jnp.mean(xf * xf, axis=-1, keepdims=True) + EPS)
    return (y * g.astype(jnp.float32)).astype(x.dtype)


def rope(x, pos):
    half = HEAD_DIM // 2
    freq = ROPE_THETA ** (-jnp.arange(half, dtype=jnp.float32) / half)
    ang = pos.astype(jnp.float32)[..., None] * freq
    cos = jnp.cos(ang)[:, :, None, :]
    sin = jnp.sin(ang)[:, :, None, :]
    xf = x.astype(jnp.float32)
    x1, x2 = xf[..., :half], xf[..., half:]
    out = jnp.concatenate([x1 * cos - x2 * sin, x2 * cos + x1 * sin], axis=-1)
    return out.astype(x.dtype)


def swa_attention(q, k, v, sink):
    bsz, s = q.shape[0], q.shape[1]
    nb = s // ATTN_BLOCK
    qb = q.reshape(bsz, nb, ATTN_BLOCK, N_KV_HEADS, Q_PER_KV, HEAD_DIM)
    kb = k.reshape(bsz, nb, ATTN_BLOCK, N_KV_HEADS, HEAD_DIM)
    vb = v.reshape(bsz, nb, ATTN_BLOCK, N_KV_HEADS, HEAD_DIM)
    pad = ((0, 0), (1, 0), (0, 0), (0, 0), (0, 0))
    kk = jnp.concatenate([jnp.pad(kb, pad)[:, :-1], kb], axis=2)
    vv = jnp.concatenate([jnp.pad(vb, pad)[:, :-1], vb], axis=2)
    scores = jnp.einsum('bnqkgd,bnskd->bnkgqs', qb, kk).astype(jnp.float32) * (HEAD_DIM ** -0.5)
    qi = jnp.arange(ATTN_BLOCK)[:, None]
    sj = jnp.arange(2 * ATTN_BLOCK)[None, :]
    diff = qi + ATTN_BLOCK - sj
    band = (diff >= 0) & (diff < WINDOW)
    real_key = (jnp.arange(nb)[:, None, None] > 0) | (sj >= ATTN_BLOCK)[None]
    mask = (band[None] & real_key)[None, :, None, None]
    scores = jnp.where(mask, scores, -jnp.inf)
    sink_l = jnp.broadcast_to(
        sink.astype(jnp.float32).reshape(N_KV_HEADS, Q_PER_KV)[None, None, :, :, None, None],
        scores.shape[:-1] + (1,))
    p = jax.nn.softmax(jnp.concatenate([scores, sink_l], axis=-1), axis=-1)[..., :-1]
    out = jnp.einsum('bnkgqs,bnskd->bnqkgd', p.astype(v.dtype), vv)
    return out.reshape(bsz, s, N_Q_HEADS * HEAD_DIM)


def _ssm_combine(e1, e2):
    a1, b1 = e1
    a2, b2 = e2
    return a1 * a2, a2 * b1 + b2


def s5_mixer(u, lam_re, lam_im, b_re, b_im, c_re, c_im, d_skip, log_dt, w_glu):
    bsz, s, _ = u.shape
    f32 = jnp.float32
    uf = u.astype(f32).reshape(bsz, s, SSM_GROUPS, SSM_GROUP_CH)
    lam = lax.complex(lam_re.astype(f32), lam_im.astype(f32))
    dt = jnp.exp(log_dt.astype(f32))[:, None]
    lam_bar = jnp.exp(lam * dt)
    b_c = lax.complex(b_re.astype(f32), b_im.astype(f32))
    b_bar = ((lam_bar - 1.0) / lam)[..., None] * b_c
    bu = jnp.einsum('blgh,gph->blgp', uf.astype(jnp.complex64), b_bar)
    a = jnp.broadcast_to(lam_bar, bu.shape)
    _, states = lax.associative_scan(_ssm_combine, (a, bu), axis=1)
    c_c = lax.complex(c_re.astype(f32), c_im.astype(f32))
    y = jnp.real(jnp.einsum('blgp,ghp->blgh', states, c_c))
    y = y + d_skip.astype(f32).reshape(SSM_GROUPS, SSM_GROUP_CH) * uf
    y = jax.nn.gelu(y.reshape(bsz, s, SSM_WIDTH))
    y = y * jax.nn.sigmoid(y @ w_glu.astype(f32))
    return y.astype(u.dtype)


def moe_ffn(h, w_router, router_bias, w_gate, w_up, w_down):
    bsz, s, d = h.shape
    t = h.reshape(-1, d)
    logits = t.astype(jnp.float32) @ w_router.astype(jnp.float32)
    probs = jax.nn.softmax(logits, axis=-1)
    sel = probs + router_bias.astype(jnp.float32)
    grp = sel.reshape(-1, N_EXPERT_GROUPS, EXPERTS_PER_GROUP)
    gscore = lax.top_k(grp, TOP_K)[0].sum(-1)
    gbest = jnp.argmax(gscore, axis=-1)
    in_group = (jnp.arange(N_EXPERTS) // EXPERTS_PER_GROUP)[None, :] == gbest[:, None]
    _, idx = lax.top_k(jnp.where(in_group, sel, -jnp.inf), TOP_K)
    gates = jnp.take_along_axis(probs, idx, axis=-1)
    gates = gates / jnp.sum(gates, axis=-1, keepdims=True)
    combine = jnp.sum(jax.nn.one_hot(idx, N_EXPERTS, dtype=jnp.float32) * gates[..., None], axis=1)
    g = jnp.einsum('td,edf->tef', t, w_gate)
    up = jnp.einsum('td,edf->tef', t, w_up)
    act = jax.nn.silu(g) * up * combine.astype(t.dtype)[..., None]
    y = jnp.einsum('tef,efd->td', act, w_down)
    return y.reshape(bsz, s, d)


def setup_inputs(seed: int = 0) -> dict:
    key = jax.random.key(seed)
    ks = jax.random.split(key, 32)
    f32 = jnp.float32
    nrm = lambda k, shp, sc: jax.random.normal(k, shp, f32) * sc
    x = nrm(ks[0], (BATCH, SEQ, D_MODEL), 1.0)
    c = nrm(ks[1], (BATCH, D_MODEL), 1.0)
    offs = jax.random.randint(ks[2], (BATCH, 1), 0, 1024, dtype=jnp.int32)
    positions = offs + jnp.arange(SEQ, dtype=jnp.int32)[None, :]
    ada_w = nrm(ks[3], (DEPTH, D_MODEL, 6 * D_MODEL), 0.5 * D_MODEL ** -0.5)
    ada_b = nrm(ks[4], (DEPTH, 6 * D_MODEL), 0.02)
    norm1_g = 1.0 + nrm(ks[5], (DEPTH, D_MODEL), 0.02)
    w_in = nrm(ks[6], (DEPTH, D_MODEL, IN_WIDTH), D_MODEL ** -0.5)
    q_norm_g = 1.0 + nrm(ks[7], (DEPTH, HEAD_DIM), 0.02)
    k_norm_g = 1.0 + nrm(ks[8], (DEPTH, HEAD_DIM), 0.02)
    attn_sink = nrm(ks[9], (DEPTH, N_Q_HEADS), 0.5)
    n = jnp.arange(SSM_STATE, dtype=f32)
    lam_re = -0.5 + nrm(ks[10], (DEPTH, SSM_GROUPS, SSM_STATE), 0.01)
    lam_im = math.pi * n + nrm(ks[11], (DEPTH, SSM_GROUPS, SSM_STATE), 0.01)
    b_sc = (2.0 * SSM_GROUP_CH) ** -0.5
    ssm_b_re = nrm(ks[12], (DEPTH, SSM_GROUPS, SSM_STATE, SSM_GROUP_CH), b_sc)
    ssm_b_im = nrm(ks[13], (DEPTH, SSM_GROUPS, SSM_STATE, SSM_GROUP_CH), b_sc)
    c_sc = (2.0 * SSM_STATE) ** -0.5
    ssm_c_re = nrm(ks[14], (DEPTH, SSM_GROUPS, SSM_GROUP_CH, SSM_STATE), c_sc)
    ssm_c_im = nrm(ks[15], (DEPTH, SSM_GROUPS, SSM_GROUP_CH, SSM_STATE), c_sc)
    ssm_d = nrm(ks[16], (DEPTH, SSM_WIDTH), 1.0)
    ssm_log_dt = jax.random.uniform(ks[17], (DEPTH, SSM_GROUPS), f32, math.log(0.001), math.log(0.1))
    w_glu = nrm(ks[18], (DEPTH, SSM_WIDTH, SSM_WIDTH), SSM_WIDTH ** -0.5)
    attn_out_g = 1.0 + nrm(ks[19], (DEPTH, ATTN_WIDTH), 0.02)
    ssm_out_g = 1.0 + nrm(ks[20], (DEPTH, SSM_WIDTH), 0.02)
    w_out = nrm(ks[21], (DEPTH, ATTN_WIDTH + SSM_WIDTH, D_MODEL), (ATTN_WIDTH + SSM_WIDTH) ** -0.5)
    norm2_g = 1.0 + nrm(ks[22], (DEPTH, D_MODEL), 0.02)
    w_router = nrm(ks[23], (D_MODEL, N_EXPERTS), D_MODEL ** -0.5)
    router_bias = nrm(ks[24], (N_EXPERTS,), 0.01)
    w_exp_gate = nrm(ks[25], (DEPTH, N_EXPERTS, D_MODEL, D_FF_EXPERT), D_MODEL ** -0.5)
    w_exp_up = nrm(ks[26], (DEPTH, N_EXPERTS, D_MODEL, D_FF_EXPERT), D_MODEL ** -0.5)
    w_exp_down = nrm(ks[27], (DEPTH, N_EXPERTS, D_FF_EXPERT, D_MODEL), D_FF_EXPERT ** -0.5)
    return {"x": x, "c": c, "positions": positions, "ada_w": ada_w, "ada_b": ada_b,
            "norm1_g": norm1_g, "w_in": w_in, "q_norm_g": q_norm_g, "k_norm_g": k_norm_g,
            "attn_sink": attn_sink, "lam_re": lam_re, "lam_im": lam_im,
            "ssm_b_re": ssm_b_re, "ssm_b_im": ssm_b_im, "ssm_c_re": ssm_c_re, "ssm_c_im": ssm_c_im,
            "ssm_d": ssm_d, "ssm_log_dt": ssm_log_dt, "w_glu": w_glu,
            "attn_out_g": attn_out_g, "ssm_out_g": ssm_out_g, "w_out": w_out, "norm2_g": norm2_g,
            "w_router": w_router, "router_bias": router_bias,
            "w_exp_gate": w_exp_gate, "w_exp_up": w_exp_up, "w_exp_down": w_exp_down}


def reference(x, c, positions, ada_w, ada_b, norm1_g, w_in, q_norm_g, k_norm_g, attn_sink,
              lam_re, lam_im, ssm_b_re, ssm_b_im, ssm_c_re, ssm_c_im, ssm_d, ssm_log_dt, w_glu,
              attn_out_g, ssm_out_g, w_out, norm2_g, w_router, router_bias,
              w_exp_gate, w_exp_up, w_exp_down):
    bsz, s, _ = x.shape
    for l in range(DEPTH):
        mod = jax.nn.silu(c) @ ada_w[l] + ada_b[l]
        sh1, sc1, g1, sh2, sc2, g2 = jnp.split(mod, 6, axis=-1)
        h = rmsnorm(x, norm1_g[l]) * (1.0 + sc1[:, None]) + sh1[:, None]
        proj = h @ w_in[l]
        q = proj[..., :ATTN_WIDTH].reshape(bsz, s, N_Q_HEADS, HEAD_DIM)
        k = proj[..., ATTN_WIDTH:ATTN_WIDTH + KV_WIDTH].reshape(bsz, s, N_KV_HEADS, HEAD_DIM)
        v = proj[..., ATTN_WIDTH + KV_WIDTH:ATTN_WIDTH + 2 * KV_WIDTH].reshape(bsz, s, N_KV_HEADS, HEAD_DIM)
        u = proj[..., ATTN_WIDTH + 2 * KV_WIDTH:]
        q = rope(rmsnorm(q, q_norm_g[l]), positions)
        k = rope(rmsnorm(k, k_norm_g[l]), positions)
        attn = swa_attention(q, k, v, attn_sink[l])
        ssm = s5_mixer(u, lam_re[l], lam_im[l], ssm_b_re[l], ssm_b_im[l], ssm_c_re[l], ssm_c_im[l],
                       ssm_d[l], ssm_log_dt[l], w_glu[l])
        merged = jnp.concatenate([rmsnorm(attn, attn_out_g[l]), rmsnorm(ssm, ssm_out_g[l])], axis=-1)
        x = x + g1[:, None] * (merged @ w_out[l])
        h2 = rmsnorm(x, norm2_g[l]) * (1.0 + sc2[:, None]) + sh2[:, None]
        x = x + g2[:, None] * moe_ffn(h2, w_router, router_bias, w_exp_gate[l], w_exp_up[l], w_exp_down[l])
    return x
```

```python
import functools

import numpy as np
import jax
import jax.numpy as jnp
from jax import lax
from jax.experimental import pallas as pl
from jax.experimental.pallas import tpu as pltpu

F32 = jnp.float32
BF16 = jnp.bfloat16

HEAD_DIM = 64
N_Q_HEADS = 8
N_KV_HEADS = 2
Q_PER_KV = N_Q_HEADS // N_KV_HEADS
ATTN_WIDTH = N_Q_HEADS * HEAD_DIM
KV_WIDTH = N_KV_HEADS * HEAD_DIM
ATTN_BLOCK = 128
ROPE_THETA = 10000.0
SSM_GROUP_CH = 16
SSM_GROUPS = 32
SSM_WIDTH = SSM_GROUPS * SSM_GROUP_CH
SSM_STATE = 64
SSM_CHUNK = 16
N_EXPERTS = 16
N_EXPERT_GROUPS = 4
EXPERTS_PER_GROUP = N_EXPERTS // N_EXPERT_GROUPS
D_FF_EXPERT = 512
EPS = 1e-6
MASK_BIAS = -1e30

TOKEN_TILE = 512
MOE_ROWS = 256
VMEM_LIMIT = 48 * 1024 * 1024


def _params(sem, vmem=None):
    return pltpu.CompilerParams(dimension_semantics=sem, vmem_limit_bytes=vmem)


def _rms(x, g):
    return x * lax.rsqrt(jnp.mean(x * x, axis=-1, keepdims=True) + EPS) * g


def _mod_kernel(c_ref, w_ref, b_ref, o_ref):
    c = c_ref[...]
    s = c * jax.nn.sigmoid(c)
    o_ref[...] = jnp.dot(s.astype(BF16), w_ref[...].astype(BF16), preferred_element_type=F32) + b_ref[...]


def _adaln_mod(c, ada_w, ada_b):
    depth, d, d6 = ada_w.shape
    nb = c.shape[0]
    n6 = d6 // d
    return pl.pallas_call(
        _mod_kernel,
        grid=(depth, n6),
        in_specs=[pl.BlockSpec((nb, d), lambda l, j: (0, 0)),
                  pl.BlockSpec((None, d, d), lambda l, j: (l, 0, j)),
                  pl.BlockSpec((None, None, 1, d), lambda l, j: (l, j, 0, 0))],
        out_specs=pl.BlockSpec((None, None, nb, d), lambda l, j: (l, j, 0, 0)),
        out_shape=jax.ShapeDtypeStruct((depth, n6, nb, d), F32),
        compiler_params=_params(("arbitrary", "arbitrary"), VMEM_LIMIT),
        name="adaln_mod",
    )(c, ada_w, ada_b.reshape(depth, n6, 1, d))


def _rope_kernel(pos_ref, freq_ref, cos_ref, sin_ref):
    ang = pos_ref[...].astype(F32) * freq_ref[...]
    cos_ref[...] = jnp.cos(ang)
    sin_ref[...] = jnp.sin(ang)


def _rope_tables(positions):
    half = HEAD_DIM // 2
    t = positions.size
    per_row = 128 // half
    rows = t // per_row
    pos_rep = jnp.repeat(positions.reshape(rows, per_row), half, axis=1)
    freq = (ROPE_THETA ** (-np.arange(half, dtype=np.float64) / half)).astype(np.float32)
    freq_row = jnp.asarray(np.tile(freq, per_row)[None, :])
    blk = min(rows, 512)
    cos, sin = pl.pallas_call(
        _rope_kernel,
        grid=(rows // blk,),
        in_specs=[pl.BlockSpec((blk, 128), lambda i: (i, 0)),
                  pl.BlockSpec((1, 128), lambda i: (0, 0))],
        out_specs=[pl.BlockSpec((blk, 128), lambda i: (i, 0))] * 2,
        out_shape=[jax.ShapeDtypeStruct((rows, 128), F32)] * 2,
        compiler_params=_params(("arbitrary",)),
        name="rope_tables",
    )(pos_rep, freq_row)
    widen = lambda a: jnp.tile(a.reshape(t, half), (1, 128 // half))
    return widen(cos), widen(sin)


def _rope_constants():
    lane = np.arange(ATTN_WIDTH)
    head_sum = (lane[:, None] // HEAD_DIM == lane[None, :] // HEAD_DIM).astype(np.float32)
    half = HEAD_DIM // 2
    rot = np.zeros((ATTN_WIDTH, ATTN_WIDTH), np.float32)
    for d in range(ATTN_WIDTH):
        if d % HEAD_DIM < half:
            rot[d + half, d] = -1.0
        else:
            rot[d - half, d] = 1.0
    return jnp.asarray(head_sum, BF16), jnp.asarray(rot, BF16)


def _inproj_kernel(*refs, has_res):
    if has_res:
        (x_ref, y_ref, g2_ref, sc_ref, sh_ref, n1_ref, w_ref, qg_ref, kg_ref, hs_ref, rot_ref,
         cos_ref, sin_ref, q_ref, k_ref, v_ref, u_ref, xo_ref) = refs
        x = x_ref[...] + g2_ref[...] * y_ref[...].astype(F32)
        xo_ref[...] = x
    else:
        (x_ref, sc_ref, sh_ref, n1_ref, w_ref, qg_ref, kg_ref, hs_ref, rot_ref,
         cos_ref, sin_ref, q_ref, k_ref, v_ref, u_ref) = refs
        x = x_ref[...]
    h = _rms(x, n1_ref[...]) * (1.0 + sc_ref[...]) + sh_ref[...]
    proj = jnp.dot(h.astype(BF16), w_ref[...], preferred_element_type=F32)
    q = proj[:, :ATTN_WIDTH]
    k = proj[:, ATTN_WIDTH:ATTN_WIDTH + KV_WIDTH]
    v = proj[:, ATTN_WIDTH + KV_WIDTH:ATTN_WIDTH + 2 * KV_WIDTH]
    u = proj[:, ATTN_WIDTH + 2 * KV_WIDTH:]
    cos = cos_ref[...]
    sin = sin_ref[...]
    reps = ATTN_WIDTH // 128
    cos_q = jnp.concatenate([cos] * reps, axis=1)
    sin_q = jnp.concatenate([sin] * reps, axis=1)

    def head_norm_rope(t, head_sum, rot, gain, c, s):
        ssq = jnp.dot((t * t).astype(BF16), head_sum, preferred_element_type=F32)
        tn = (t * lax.rsqrt(ssq * (1.0 / HEAD_DIM) + EPS) * gain).astype(BF16)
        tr = jnp.dot(tn, rot, preferred_element_type=F32)
        return tn.astype(F32) * c + tr * s

    qo = head_norm_rope(q, hs_ref[...], rot_ref[...], qg_ref[...], cos_q, sin_q)
    ko = head_norm_rope(k, hs_ref[:KV_WIDTH, :KV_WIDTH], rot_ref[:KV_WIDTH, :KV_WIDTH], kg_ref[...], cos, sin)
    q_ref[...] = qo.astype(BF16)
    k_ref[...] = jnp.concatenate([ko, pltpu.roll(ko, HEAD_DIM, axis=1)], axis=1).astype(BF16)
    v_ref[...] = jnp.concatenate([v, pltpu.roll(v, HEAD_DIM, axis=1)], axis=1).astype(BF16)
    u_ref[...] = u.astype(BF16)


def _inproj(x, res, sc1, sh1, n1g, w_in, qg, kg, head_sum, rot, cos, sin, seq):
    t, d = x.shape
    tm = min(TOKEN_TILE, seq)
    per_b = seq // tm
    in_width = w_in.shape[1]
    tok = lambda w: pl.BlockSpec((tm, w), lambda i: (i, 0))
    const = lambda a: pl.BlockSpec(a.shape, lambda i: (0,) * a.ndim)
    per_batch = pl.BlockSpec((None, 1, d), lambda i: (i // per_b, 0, 0))
    ins, specs = [x], [tok(d)]
    if res is not None:
        y_prev, g2_prev = res
        ins += [y_prev, g2_prev]
        specs += [tok(d), per_batch]
    ins += [sc1, sh1, n1g, w_in, qg, kg, head_sum, rot, cos, sin]
    specs += [per_batch, per_batch, const(n1g), const(w_in), const(qg), const(kg), const(head_sum), const(rot),
              tok(128), tok(128)]
    out_shape = [jax.ShapeDtypeStruct((t, ATTN_WIDTH), BF16), jax.ShapeDtypeStruct((t, 2 * KV_WIDTH), BF16),
                 jax.ShapeDtypeStruct((t, 2 * KV_WIDTH), BF16), jax.ShapeDtypeStruct((t, SSM_WIDTH), BF16)]
    out_specs = [tok(ATTN_WIDTH), tok(2 * KV_WIDTH), tok(2 * KV_WIDTH), tok(SSM_WIDTH)]
    if res is not None:
        out_shape.append(jax.ShapeDtypeStruct((t, d), F32))
        out_specs.append(tok(d))
    assert in_width == ATTN_WIDTH + 2 * KV_WIDTH + SSM_WIDTH
    return pl.pallas_call(
        functools.partial(_inproj_kernel, has_res=res is not None),
        grid=(t // tm,),
        in_specs=specs,
        out_specs=out_specs,
        out_shape=out_shape,
        compiler_params=_params(("parallel",), VMEM_LIMIT),
        name="inproj",
    )(*ins)


def _attn_kernel(sink_ref, q_ref, kc_ref, kp_ref, vc_ref, vp_ref, bias_ref, g_ref, o_ref):
    kk = jnp.concatenate([kp_ref[...], kc_ref[...]], axis=0)
    vv = jnp.concatenate([vp_ref[...], vc_ref[...]], axis=0)
    lane = lax.broadcasted_iota(jnp.int32, (2 * ATTN_BLOCK, KV_WIDTH), 1)
    low = lane < HEAD_DIM
    zero = jnp.zeros((2 * ATTN_BLOCK, KV_WIDTH), BF16)

    def variants(a):
        nat, swp = a[:, :KV_WIDTH], a[:, KV_WIDTH:]
        return {(0, 0): jnp.where(low, nat, zero), (0, 1): jnp.where(low, zero, swp),
                (1, 0): jnp.where(low, swp, zero), (1, 1): jnp.where(low, zero, nat)}

    kvar, vvar = variants(kk), variants(vv)
    bias = bias_ref[...]
    outs = []
    for pair in range(N_Q_HEADS // 2):
        qp = q_ref[:, pair * 128:(pair + 1) * 128]
        acc = jnp.zeros((ATTN_BLOCK, 128), F32)
        for half in range(2):
            head = 2 * pair + half
            kv = head // Q_PER_KV
            s = lax.dot_general(qp, kvar[(kv, half)], (((1,), (1,)), ((), ())), preferred_element_type=F32) + bias
            sink = sink_ref[head]
            m = jnp.maximum(jnp.max(s, axis=-1, keepdims=True), sink)
            p = jnp.exp(s - m)
            denom = jnp.sum(p, axis=-1, keepdims=True) + jnp.exp(sink - m)
            o = jnp.dot(p.astype(BF16), vvar[(kv, half)], preferred_element_type=F32)
            acc = acc + o * (1.0 / denom)
        outs.append(acc)
    a = jnp.concatenate(outs, axis=1)
    o_ref[...] = _rms(a, g_ref[...]).astype(BF16)


def _attn_bias():
    qi = np.arange(ATTN_BLOCK)[:, None]
    sj = np.arange(2 * ATTN_BLOCK)[None, :]
    diff = qi + ATTN_BLOCK - sj
    band = (diff >= 0) & (diff < ATTN_BLOCK)
    first = band & (sj >= ATTN_BLOCK)
    return jnp.asarray(np.where(np.stack([first, band]), 0.0, MASK_BIAS).astype(np.float32))


def _attention(q, kx, vx, sink, out_g, bias, batch, seq):
    t = q.shape[0]
    nb = seq // ATTN_BLOCK
    cur = lambda w: pl.BlockSpec((ATTN_BLOCK, w), lambda b, n, s: (b * nb + n, 0))
    prev = lambda w: pl.BlockSpec((ATTN_BLOCK, w), lambda b, n, s: (b * nb + jnp.maximum(n - 1, 0), 0))
    grid_spec = pltpu.PrefetchScalarGridSpec(
        num_scalar_prefetch=1,
        grid=(batch, nb),
        in_specs=[cur(ATTN_WIDTH), cur(2 * KV_WIDTH), prev(2 * KV_WIDTH), cur(2 * KV_WIDTH), prev(2 * KV_WIDTH),
                  pl.BlockSpec((None, ATTN_BLOCK, 2 * ATTN_BLOCK), lambda b, n, s: (jnp.minimum(n, 1), 0, 0)),
                  pl.BlockSpec((1, ATTN_WIDTH), lambda b, n, s: (0, 0))],
        out_specs=cur(ATTN_WIDTH),
    )
    return pl.pallas_call(
        _attn_kernel,
        grid_spec=grid_spec,
        out_shape=jax.ShapeDtypeStruct((t, ATTN_WIDTH), BF16),
        compiler_params=_params(("parallel", "arbitrary")),
        name="swa_attention",
    )(sink, q, kx, kx, vx, vx, bias, out_g)


def _s5_prep_kernel(lr_re_ref, lr_im_ref, lc_re_ref, lc_im_ref, ldt_ref, bt_re_ref, bt_im_ref, ct_re_ref, ct_im_ref,
                    exp_ref, t_ref, w_ref, wsw_ref, v_ref, la_ref, lb_ref):
    hi = lax.Precision.HIGHEST
    nl = SSM_CHUNK
    dt = jnp.exp(ldt_ref[...])
    lam_re, lam_im = lr_re_ref[...], lr_im_ref[...]
    a_r, th_r = lam_re * dt, lam_im * dt
    a_c, th_c = lc_re_ref[...] * dt, lc_im_ref[...] * dt
    low = lax.broadcasted_iota(jnp.int32, (1, 2 * SSM_STATE), 1) < SSM_STATE
    row_low = lax.broadcasted_iota(jnp.int32, (2 * SSM_STATE, 1), 0) < SSM_STATE

    jcol = lax.broadcasted_iota(jnp.int32, (nl, 1), 0).astype(F32)
    er = jnp.exp(jcol * a_r)
    pw_re, pw_im = er * jnp.cos(jcol * th_r), er * jnp.sin(jcol * th_r)

    e1 = jnp.exp(a_r)
    nr, ni = e1 * jnp.cos(th_r) - 1.0, e1 * jnp.sin(th_r)
    den = lam_re * lam_re + lam_im * lam_im
    c_re, c_im = (nr * lam_re + ni * lam_im) / den, (ni * lam_re - nr * lam_im) / den
    bt_re, bt_im = bt_re_ref[...], bt_im_ref[...]
    bb_re, bb_im = c_re * bt_re - c_im * bt_im, c_re * bt_im + c_im * bt_re

    w_rows, wsw_rows = [], []
    for s in range(nl):
        j = nl - 1 - s
        pr, pi = pw_re[j:j + 1, :], pw_im[j:j + 1, :]
        w_re, w_im = pr * bb_re - pi * bb_im, pr * bb_im + pi * bb_re
        w_rows.append(jnp.where(low, w_re, w_im))
        wsw_rows.append(jnp.where(low, w_im, w_re))
    w_ref[...] = jnp.concatenate(w_rows, axis=0).astype(BF16)
    wsw_ref[...] = jnp.concatenate(wsw_rows, axis=0).astype(BF16)

    jrow = lax.broadcasted_iota(jnp.int32, (1, nl), 1).astype(F32)
    ec = jnp.exp(a_c * jrow)
    pc = jnp.dot(ec * jnp.cos(th_c * jrow), exp_ref[...], precision=hi, preferred_element_type=F32)
    ps = jnp.dot(ec * jnp.sin(th_c * jrow), exp_ref[...], precision=hi, preferred_element_type=F32)
    ct_re, ct_im = ct_re_ref[...], ct_im_ref[...]
    a_re, a_im = ct_re * pc - ct_im * ps, ct_re * ps + ct_im * pc
    a_cat = jnp.where(row_low, a_re, -a_im)
    e1c = jnp.exp(a_c)
    l1_re, l1_im = e1c * jnp.cos(th_c), e1c * jnp.sin(th_c)
    v_re, v_im = a_re * l1_re - a_im * l1_im, a_re * l1_im + a_im * l1_re
    v_ref[...] = jnp.where(row_low, v_re, -v_im).astype(BF16)

    kt = jnp.dot(jnp.where(low, bb_re, bb_im), a_cat, precision=hi, preferred_element_type=F32)
    lane = lax.broadcasted_iota(jnp.int32, kt.shape, 1)
    t_rows = [kt]
    for s in range(1, nl):
        t_rows.append(jnp.where(lane >= SSM_GROUP_CH * s, pltpu.roll(kt, SSM_GROUP_CH * s, axis=1), 0.0))
    t_ref[...] = jnp.concatenate(t_rows, axis=0).astype(BF16)

    e16 = jnp.exp(nl * a_r)
    la_ref[...] = e16 * jnp.cos(nl * th_r)
    lim = e16 * jnp.sin(nl * th_r)
    lb_ref[...] = jnp.where(low, -lim, lim)


def _s5_prep(lam_re, lam_im, b_re, b_im, c_re, c_im, log_dt):
    g, p, h, nl = SSM_GROUPS, SSM_STATE, SSM_GROUP_CH, SSM_CHUNK
    dup_row = lambda a: jnp.tile(a, (1, 2)).reshape(g, 1, 2 * p)
    dup_col = lambda a: jnp.tile(a, (1, 2)).reshape(g, 2 * p, 1)
    bt = lambda a: jnp.tile(jnp.swapaxes(a, 1, 2), (1, 1, 2))
    ct = lambda a: jnp.tile(jnp.swapaxes(a, 1, 2), (1, 2, nl))
    expand = jnp.asarray(np.repeat(np.eye(nl, dtype=np.float32), h, axis=1))
    blk = lambda *s: pl.BlockSpec((None,) + s, lambda i: (i,) + (0,) * len(s))
    lw = nl * h
    return pl.pallas_call(
        _s5_prep_kernel,
        grid=(g,),
        in_specs=[blk(1, 2 * p), blk(1, 2 * p), blk(2 * p, 1), blk(2 * p, 1), blk(1, 1),
                  blk(h, 2 * p), blk(h, 2 * p), blk(2 * p, lw), blk(2 * p, lw),
                  pl.BlockSpec((nl, lw), lambda i: (0, 0))],
        out_specs=[blk(lw, lw), blk(lw, 2 * p), blk(lw, 2 * p), blk(2 * p, lw), blk(1, 2 * p), blk(1, 2 * p)],
        out_shape=[jax.ShapeDtypeStruct((g, lw, lw), BF16), jax.ShapeDtypeStruct((g, lw, 2 * p), BF16),
                   jax.ShapeDtypeStruct((g, lw, 2 * p), BF16), jax.ShapeDtypeStruct((g, 2 * p, lw), BF16),
                   jax.ShapeDtypeStruct((g, 1, 2 * p), F32), jax.ShapeDtypeStruct((g, 1, 2 * p), F32)],
        compiler_params=_params(("parallel",)),
        name="s5_prep",
    )(dup_row(lam_re), dup_row(lam_im), dup_col(lam_re), dup_col(lam_im), log_dt.reshape(g, 1, 1),
      bt(b_re), bt(b_im), ct(c_re), ct(c_im), expand)


def _s5_kernel(x_ref, t_ref, w_ref, wsw_ref, v_ref, la_ref, lb_ref, o_ref, sa_ref, sb_ref, xp_ref, *, nchunks, nb):
    x = x_ref[...]
    sa_ref[...] = jnp.dot(x, w_ref[...], preferred_element_type=F32)
    sb_ref[...] = jnp.dot(x, wsw_ref[...], preferred_element_type=F32)
    la = jnp.broadcast_to(la_ref[...], (nb, 2 * SSM_STATE))
    lb = jnp.broadcast_to(lb_ref[...], (nb, 2 * SSM_STATE))

    def step(c, carry):
        xa, xb = carry
        r = pl.multiple_of(c * nb, nb)
        xp_ref[pl.ds(r, nb), :] = xa
        xa_new = la * xa + lb * xb + sa_ref[pl.ds(r, nb), :]
        xb_new = la * xb - lb * xa + sb_ref[pl.ds(r, nb), :]
        return xa_new, xb_new

    zero = jnp.zeros((nb, 2 * SSM_STATE), F32)
    lax.fori_loop(0, nchunks, step, (zero, zero), unroll=8)
    inter = jnp.dot(xp_ref[...].astype(BF16), v_ref[...], preferred_element_type=F32)
    o_ref[...] = (jnp.dot(x, t_ref[...], preferred_element_type=F32) + inter).astype(BF16)


def _s5_scan(xg, tm, wm, wsw, vm, la, lb, nchunks, nb):
    g, rows, lw = xg.shape
    blk = lambda *s: pl.BlockSpec((None,) + s, lambda i: (i,) + (0,) * len(s))
    return pl.pallas_call(
        functools.partial(_s5_kernel, nchunks=nchunks, nb=nb),
        grid=(g,),
        in_specs=[blk(rows, lw), blk(lw, lw), blk(lw, 2 * SSM_STATE), blk(lw, 2 * SSM_STATE), blk(2 * SSM_STATE, lw),
                  blk(1, 2 * SSM_STATE), blk(1, 2 * SSM_STATE)],
        out_specs=blk(rows, lw),
        out_shape=jax.ShapeDtypeStruct((g, rows, lw), BF16),
        scratch_shapes=[pltpu.VMEM((rows, 2 * SSM_STATE), F32)] * 3,
        compiler_params=_params(("parallel",)),
        name="s5_scan",
    )(xg, tm, wm, wsw, vm, la, lb)


def _to_groups(u, batch, seq):
    nc = seq // SSM_CHUNK
    a = u.reshape(batch, nc, SSM_CHUNK, SSM_GROUPS, SSM_GROUP_CH)
    return a.transpose(3, 1, 0, 2, 4).reshape(SSM_GROUPS, nc * batch, SSM_CHUNK * SSM_GROUP_CH)


def _from_groups(y, batch, seq):
    nc = seq // SSM_CHUNK
    a = y.reshape(SSM_GROUPS, nc, batch, SSM_CHUNK, SSM_GROUP_CH)
    return a.transpose(2, 1, 3, 0, 4).reshape(batch * seq, SSM_WIDTH)


def _route(logits, bias):
    m = jnp.max(logits, axis=0, keepdims=True)
    e = jnp.exp(logits - m)
    probs = e / jnp.sum(e, axis=0, keepdims=True)
    sel = probs + bias
    row = lambda a, i: a[i:i + 1, :]
    best_score, best = None, None
    for grp in range(N_EXPERT_GROUPS):
        a, b, c, d = (row(sel, EXPERTS_PER_GROUP * grp + i) for i in range(EXPERTS_PER_GROUP))
        hab, lab, hcd, lcd = jnp.maximum(a, b), jnp.minimum(a, b), jnp.maximum(c, d), jnp.minimum(c, d)
        top1 = jnp.maximum(hab, hcd)
        top2 = jnp.maximum(jnp.maximum(lab, lcd), jnp.minimum(hab, hcd))
        score = top1 + top2
        if grp == 0:
            best_score, best = score, jnp.zeros(score.shape, jnp.int32)
        else:
            better = score > best_score
            best = jnp.where(better, grp, best)
            best_score = jnp.where(better, score, best_score)

    def pick(a, i):
        out = row(a, i)
        for grp in range(1, N_EXPERT_GROUPS):
            out = jnp.where(best == grp, row(a, EXPERTS_PER_GROUP * grp + i), out)
        return out

    s_in = [pick(sel, i) for i in range(EXPERTS_PER_GROUP)]
    p_in = [pick(probs, i) for i in range(EXPERTS_PER_GROUP)]
    neg = jnp.full(s_in[0].shape, -jnp.inf, F32)

    def argmax_first(vals):
        idx, val = jnp.zeros(vals[0].shape, jnp.int32), vals[0]
        for i in range(1, len(vals)):
            better = vals[i] > val
            idx = jnp.where(better, i, idx)
            val = jnp.where(better, vals[i], val)
        return idx

    i1 = argmax_first(s_in)
    i2 = argmax_first([jnp.where(i1 == i, neg, s_in[i]) for i in range(EXPERTS_PER_GROUP)])
    zero = jnp.zeros(p_in[0].shape, F32)
    g1 = sum(jnp.where(i1 == i, p_in[i], zero) for i in range(EXPERTS_PER_GROUP))
    g2 = sum(jnp.where(i2 == i, p_in[i], zero) for i in range(EXPERTS_PER_GROUP))
    tot = g1 + g2
    w1, w2 = g1 / tot, g2 / tot
    cw = [jnp.where(i1 == i, w1, zero) + jnp.where(i2 == i, w2, zero) for i in range(EXPERTS_PER_GROUP)]
    return jnp.concatenate(cw, axis=0), best


def _post_kernel(x_ref, at_ref, ys_ref, u_ref, d_ref, wglu_ref, gs_ref, wo_ref, g1_ref, n2_ref, sc_ref, sh_ref,
                 wrt_ref, rb_ref, x1_ref, h2_ref, cw_ref, gid_ref):
    y = ys_ref[...].astype(F32) + d_ref[...] * u_ref[...].astype(F32)
    yg = jax.nn.gelu(y)
    z = yg * jax.nn.sigmoid(jnp.dot(yg.astype(BF16), wglu_ref[...], preferred_element_type=F32))
    zn = _rms(z, gs_ref[...]).astype(BF16)
    o = (jnp.dot(at_ref[...], wo_ref[:ATTN_WIDTH, :], preferred_element_type=F32)
         + jnp.dot(zn, wo_ref[ATTN_WIDTH:, :], preferred_element_type=F32))
    x1 = x_ref[...] + g1_ref[...] * o
    x1_ref[...] = x1
    h2 = _rms(x1, n2_ref[...]) * (1.0 + sc_ref[...]) + sh_ref[...]
    h2_ref[...] = h2.astype(BF16)
    logits = lax.dot_general(wrt_ref[...], h2, (((1,), (1,)), ((), ())), precision=lax.Precision.HIGHEST,
                             preferred_element_type=F32)
    cw, best = _route(logits, rb_ref[...])
    cw_ref[...] = cw
    gid_ref[...] = best


def _post(x, attn, ys, u, d_skip, w_glu, ssm_g, w_out, g1, n2g, sc2, sh2, w_router_t, router_bias, seq):
    t, d = x.shape
    tm = min(TOKEN_TILE, seq)
    per_b = seq // tm
    tok = lambda w: pl.BlockSpec((tm, w), lambda i: (i, 0))
    const = lambda a: pl.BlockSpec(a.shape, lambda i: (0,) * a.ndim)
    per_batch = pl.BlockSpec((None, 1, d), lambda i: (i // per_b, 0, 0))
    col = lambda r: pl.BlockSpec((r, tm), lambda i: (0, i))
    return pl.pallas_call(
        _post_kernel,
        grid=(t // tm,),
        in_specs=[tok(d), tok(ATTN_WIDTH), tok(SSM_WIDTH), tok(SSM_WIDTH), const(d_skip), const(w_glu), const(ssm_g),
                  const(w_out), per_batch, const(n2g), per_batch, per_batch, const(w_router_t), const(router_bias)],
        out_specs=[tok(d), tok(d), col(EXPERTS_PER_GROUP), col(1)],
        out_shape=[jax.ShapeDtypeStruct((t, d), F32), jax.ShapeDtypeStruct((t, d), BF16),
                   jax.ShapeDtypeStruct((EXPERTS_PER_GROUP, t), F32), jax.ShapeDtypeStruct((1, t), jnp.int32)],
        compiler_params=_params(("parallel",), VMEM_LIMIT),
        name="post_mix",
    )(x, attn, ys, u, d_skip, w_glu, ssm_g, w_out, g1, n2g, sc2, sh2, w_router_t, router_bias)


def _moe_kernel(rb_ref, grp_ref, first_ref, meta_ref, offs_ref, x_ref, cw_ref, wg_ref, wu_ref, wd_ref, o_ref):
    i = pl.program_id(0)

    @pl.when(i < meta_ref[0])
    def _():
        grp = grp_ref[i]
        rows = rb_ref[i] * MOE_ROWS + lax.broadcasted_iota(jnp.int32, (MOE_ROWS, 1), 0)
        in_grp = (rows >= offs_ref[grp]) & (rows < offs_ref[grp + 1])
        cw = jnp.where(in_grp, cw_ref[...], 0.0)
        x = x_ref[...]
        gate = jnp.dot(x, wg_ref[...], preferred_element_type=F32)
        up = jnp.dot(x, wu_ref[...], preferred_element_type=F32)
        act = gate * jax.nn.sigmoid(gate) * up
        parts = [(act[:, e * D_FF_EXPERT:(e + 1) * D_FF_EXPERT] * cw[:, e:e + 1]).astype(BF16)
                 for e in range(EXPERTS_PER_GROUP)]
        y = jnp.dot(jnp.concatenate(parts, axis=1), wd_ref[...], preferred_element_type=F32)

        @pl.when(first_ref[i] == 1)
        def _():
            o_ref[...] = y

        @pl.when(first_ref[i] == 0)
        def _():
            o_ref[...] += y


def _moe_items(gid, t):
    order = jnp.argsort(gid, stable=True).astype(jnp.int32)
    counts = jnp.sum((gid[None, :] == jnp.arange(N_EXPERT_GROUPS, dtype=jnp.int32)[:, None]).astype(jnp.int32), axis=1)
    offs = jnp.concatenate([jnp.zeros((1,), jnp.int32), jnp.cumsum(counts).astype(jnp.int32)])
    nblk = t // MOE_ROWS
    n_items_max = nblk + N_EXPERT_GROUPS - 1
    lo = jnp.arange(nblk, dtype=jnp.int32)[:, None] * MOE_ROWS
    valid = ((offs[None, :-1] < lo + MOE_ROWS) & (offs[None, 1:] > lo) & (offs[None, 1:] > offs[None, :-1])).reshape(-1)
    flat = jnp.arange(nblk * N_EXPERT_GROUPS, dtype=jnp.int32)
    ranked = jnp.sort(jnp.where(valid, flat, nblk * N_EXPERT_GROUPS))[:n_items_max]
    n_items = jnp.sum(valid.astype(jnp.int32))
    ranked = ranked[jnp.minimum(jnp.arange(n_items_max, dtype=jnp.int32), n_items - 1)]
    item_rb, item_grp = ranked // N_EXPERT_GROUPS, ranked % N_EXPERT_GROUPS
    first = jnp.concatenate([jnp.ones((1,), jnp.int32), (item_rb[1:] != item_rb[:-1]).astype(jnp.int32)])
    return order, offs, item_rb, item_grp, first, n_items.reshape(1), n_items_max


def _moe(h2s, cws, wg, wu, wd, item_rb, item_grp, first, meta, offs, n_items_max):
    t, d = h2s.shape
    ff = wg.shape[2]
    grid_spec = pltpu.PrefetchScalarGridSpec(
        num_scalar_prefetch=5,
        grid=(n_items_max,),
        in_specs=[pl.BlockSpec((MOE_ROWS, d), lambda i, rb, gr, fi, me, of: (rb[i], 0)),
                  pl.BlockSpec((MOE_ROWS, EXPERTS_PER_GROUP), lambda i, rb, gr, fi, me, of: (rb[i], 0)),
                  pl.BlockSpec((None, d, ff), lambda i, rb, gr, fi, me, of: (gr[i], 0, 0)),
                  pl.BlockSpec((None, d, ff), lambda i, rb, gr, fi, me, of: (gr[i], 0, 0)),
                  pl.BlockSpec((None, ff, d), lambda i, rb, gr, fi, me, of: (gr[i], 0, 0))],
        out_specs=pl.BlockSpec((MOE_ROWS, d), lambda i, rb, gr, fi, me, of: (rb[i], 0)),
    )
    return pl.pallas_call(
        _moe_kernel,
        grid_spec=grid_spec,
        out_shape=jax.ShapeDtypeStruct((t, d), F32),
        compiler_params=_params(("arbitrary",), VMEM_LIMIT),
        name="moe_grouped",
    )(item_rb, item_grp, first, meta, offs, h2s, cws, wg, wu, wd)


def _group_expert_weights(w_gate, w_up, w_down):
    e, d, f = w_gate.shape
    cat = lambda w: (w.reshape(N_EXPERT_GROUPS, EXPERTS_PER_GROUP, d, f).transpose(0, 2, 1, 3)
                     .reshape(N_EXPERT_GROUPS, d, EXPERTS_PER_GROUP * f).astype(BF16))
    return cat(w_gate), cat(w_up), w_down.reshape(N_EXPERT_GROUPS, EXPERTS_PER_GROUP * f, d).astype(BF16)


def _final_kernel(x_ref, y_ref, g_ref, o_ref):
    o_ref[...] = x_ref[...] + g_ref[...] * y_ref[...]


def _final(x1, y, g2, seq):
    t, d = x1.shape
    tm = min(TOKEN_TILE, seq)
    per_b = seq // tm
    tok = pl.BlockSpec((tm, d), lambda i: (i, 0))
    return pl.pallas_call(
        _final_kernel,
        grid=(t // tm,),
        in_specs=[tok, tok, pl.BlockSpec((None, 1, d), lambda i: (i // per_b, 0, 0))],
        out_specs=tok,
        out_shape=jax.ShapeDtypeStruct((t, d), F32),
        compiler_params=_params(("parallel",)),
        name="final_residual",
    )(x1, y, g2)


def kernel(x, c, positions, ada_w, ada_b, norm1_g, w_in, q_norm_g, k_norm_g, attn_sink, lam_re, lam_im, ssm_b_re, ssm_b_im, ssm_c_re, ssm_c_im, ssm_d, ssm_log_dt, w_glu, attn_out_g, ssm_out_g, w_out, norm2_g, w_router, router_bias, w_exp_gate, w_exp_up, w_exp_down):
    batch, seq, d = x.shape
    depth = ada_w.shape[0]
    t = batch * seq
    assert seq % ATTN_BLOCK == 0 and seq % SSM_CHUNK == 0 and t % MOE_ROWS == 0

    mod = _adaln_mod(c, ada_w, ada_b).reshape(depth, 6, batch, 1, d)
    cos, sin = _rope_tables(positions)
    head_sum, rot = _rope_constants()
    bias = _attn_bias()
    w_router_t = w_router.T
    router_bias_col = router_bias.reshape(N_EXPERTS, 1)

    xf = x.reshape(t, d)
    res = None
    for l in range(depth):
        sh1, sc1, g1, sh2, sc2, g2 = (mod[l, j] for j in range(6))
        qg = (jnp.tile(q_norm_g[l], N_Q_HEADS) * HEAD_DIM ** -0.5).reshape(1, ATTN_WIDTH)
        kg = jnp.tile(k_norm_g[l], N_KV_HEADS).reshape(1, KV_WIDTH)
        outs = _inproj(xf, res, sc1, sh1, norm1_g[l].reshape(1, d), w_in[l].astype(BF16), qg, kg, head_sum, rot,
                       cos, sin, seq)
        if res is None:
            q, kx, vx, u = outs
        else:
            q, kx, vx, u, xf = outs
        attn = _attention(q, kx, vx, attn_sink[l], attn_out_g[l].reshape(1, ATTN_WIDTH), bias, batch, seq)
        tm_, wm, wsw, vm, la, lb = _s5_prep(lam_re[l], lam_im[l], ssm_b_re[l], ssm_b_im[l], ssm_c_re[l], ssm_c_im[l],
                                            ssm_log_dt[l])
        yg = _s5_scan(_to_groups(u, batch, seq), tm_, wm, wsw, vm, la, lb, seq // SSM_CHUNK, batch)
        ys = _from_groups(yg, batch, seq)
        x1, h2, cw, gid = _post(xf, attn, ys, u, ssm_d[l].reshape(1, SSM_WIDTH), w_glu[l].astype(BF16),
                                ssm_out_g[l].reshape(1, SSM_WIDTH), w_out[l].astype(BF16), g1,
                                norm2_g[l].reshape(1, d), sc2, sh2, w_router_t, router_bias_col, seq)
        order, offs, item_rb, item_grp, first, meta, n_items_max = _moe_items(gid.reshape(t), t)
        wg, wu, wd = _group_expert_weights(w_exp_gate[l], w_exp_up[l], w_exp_down[l])
        y_sorted = _moe(jnp.take(h2, order, axis=0), jnp.take(cw.T, order, axis=0), wg, wu, wd,
                        item_rb, item_grp, first, meta, offs, n_items_max)
        inv = jnp.zeros((t,), jnp.int32).at[order].set(jnp.arange(t, dtype=jnp.int32))
        y = jnp.take(y_sorted, inv, axis=0)
        xf, res = x1, (y, g2)
    y, g2 = res
    return _final(xf, y, g2, seq).reshape(batch, seq, d)
```

```python
import functools

import numpy as np
import jax
import jax.numpy as jnp
from jax import lax
from jax.experimental import pallas as pl
from jax.experimental.pallas import tpu as pltpu

F32 = jnp.float32
BF16 = jnp.bfloat16

HEAD_DIM = 64
N_Q_HEADS = 8
N_KV_HEADS = 2
Q_PER_KV = N_Q_HEADS // N_KV_HEADS
ATTN_WIDTH = N_Q_HEADS * HEAD_DIM
KV_WIDTH = N_KV_HEADS * HEAD_DIM
ATTN_BLOCK = 128
ROPE_THETA = 10000.0
LANES = 128
SSM_GROUP_CH = 16
SSM_GROUPS = 32
SSM_WIDTH = SSM_GROUPS * SSM_GROUP_CH
SSM_STATE = 64
SSM_CHUNK = 16
SSM_COLS = SSM_WIDTH // LANES
COL_GROUPS = LANES // SSM_GROUP_CH
COL_STATE = COL_GROUPS * SSM_STATE
CHUNK_LANES = SSM_CHUNK * LANES
SSM_NSPLIT = 4
N_EXPERTS = 16
N_EXPERT_GROUPS = 4
EXPERTS_PER_GROUP = N_EXPERTS // N_EXPERT_GROUPS
EPS = 1e-6
MASK_BIAS = -1e30

TOKEN_TILE = 512
MOE_ROWS = 256
VMEM_LIMIT = 48 * 1024 * 1024


def _params(sem, vmem=None):
    return pltpu.CompilerParams(dimension_semantics=sem, vmem_limit_bytes=vmem)


def _rms(x, g):
    return x * lax.rsqrt(jnp.mean(x * x, axis=-1, keepdims=True) + EPS) * g


def _mod_kernel(c_ref, w_ref, b_ref, o_ref):
    c = c_ref[...]
    s = c * jax.nn.sigmoid(c)
    o_ref[...] = jnp.dot(s.astype(BF16), w_ref[...].astype(BF16), preferred_element_type=F32) + b_ref[...]


def _adaln_mod(c, ada_w, ada_b):
    depth, d, d6 = ada_w.shape
    nb = c.shape[0]
    n6 = d6 // d
    return pl.pallas_call(
        _mod_kernel,
        grid=(depth, n6),
        in_specs=[pl.BlockSpec((nb, d), lambda l, j: (0, 0)),
                  pl.BlockSpec((None, d, d), lambda l, j: (l, 0, j)),
                  pl.BlockSpec((None, None, 1, d), lambda l, j: (l, j, 0, 0))],
        out_specs=pl.BlockSpec((None, None, nb, d), lambda l, j: (l, j, 0, 0)),
        out_shape=jax.ShapeDtypeStruct((depth, n6, nb, d), F32),
        compiler_params=_params(("arbitrary", "arbitrary"), VMEM_LIMIT),
        name="adaln_mod",
    )(c, ada_w, ada_b.reshape(depth, n6, 1, d))


def _rope_kernel(pos_ref, freq_ref, cos_ref, sin_ref):
    ang = pos_ref[...].astype(F32) * freq_ref[...]
    cos_ref[...] = jnp.cos(ang)
    sin_ref[...] = jnp.sin(ang)


def _rope_tables(positions):
    half = HEAD_DIM // 2
    t = positions.size
    per_row = LANES // half
    rows = t // per_row
    pos_rep = jnp.repeat(positions.reshape(rows, per_row), half, axis=1)
    freq = (ROPE_THETA ** (-np.arange(half, dtype=np.float64) / half)).astype(np.float32)
    freq_row = jnp.asarray(np.tile(freq, per_row)[None, :])
    blk = min(rows, 512)
    cos, sin = pl.pallas_call(
        _rope_kernel,
        grid=(rows // blk,),
        in_specs=[pl.BlockSpec((blk, LANES), lambda i: (i, 0)),
                  pl.BlockSpec((1, LANES), lambda i: (0, 0))],
        out_specs=[pl.BlockSpec((blk, LANES), lambda i: (i, 0))] * 2,
        out_shape=[jax.ShapeDtypeStruct((rows, LANES), F32)] * 2,
        compiler_params=_params(("arbitrary",)),
        name="rope_tables",
    )(pos_rep, freq_row)
    widen = lambda a: jnp.tile(a.reshape(t, half), (1, per_row))
    return widen(cos), widen(sin)


def _rope_constants():
    lane = np.arange(ATTN_WIDTH)
    head_sum = (lane[:, None] // HEAD_DIM == lane[None, :] // HEAD_DIM).astype(np.float32)
    half = HEAD_DIM // 2
    rot = np.zeros((ATTN_WIDTH, ATTN_WIDTH), np.float32)
    for d in range(ATTN_WIDTH):
        if d % HEAD_DIM < half:
            rot[d + half, d] = -1.0
        else:
            rot[d - half, d] = 1.0
    return jnp.asarray(head_sum, BF16), jnp.asarray(rot, BF16)


def _inproj_kernel(*refs, has_res):
    if has_res:
        (x_ref, y_ref, g2_ref, sc_ref, sh_ref, n1_ref, w_ref, qg_ref, kg_ref, hs_ref, rot_ref,
         cos_ref, sin_ref, q_ref, k_ref, v_ref, uc_ref, xo_ref, u_scr) = refs
        x = x_ref[...] + g2_ref[...] * y_ref[...]
        xo_ref[...] = x
    else:
        (x_ref, sc_ref, sh_ref, n1_ref, w_ref, qg_ref, kg_ref, hs_ref, rot_ref,
         cos_ref, sin_ref, q_ref, k_ref, v_ref, uc_ref, u_scr) = refs
        x = x_ref[...]
    h = _rms(x, n1_ref[...]) * (1.0 + sc_ref[...]) + sh_ref[...]
    proj = jnp.dot(h.astype(BF16), w_ref[...], preferred_element_type=F32)
    q = proj[:, :ATTN_WIDTH]
    k = proj[:, ATTN_WIDTH:ATTN_WIDTH + KV_WIDTH]
    v = proj[:, ATTN_WIDTH + KV_WIDTH:ATTN_WIDTH + 2 * KV_WIDTH]
    cos = cos_ref[...]
    sin = sin_ref[...]
    reps = ATTN_WIDTH // LANES
    cos_q = jnp.concatenate([cos] * reps, axis=1)
    sin_q = jnp.concatenate([sin] * reps, axis=1)

    def head_norm_rope(t, head_sum, rot, gain, c, s):
        ssq = jnp.dot((t * t).astype(BF16), head_sum, preferred_element_type=F32)
        tn = (t * lax.rsqrt(ssq * (1.0 / HEAD_DIM) + EPS) * gain).astype(BF16)
        tr = jnp.dot(tn, rot, preferred_element_type=F32)
        return tn.astype(F32) * c + tr * s

    qo = head_norm_rope(q, hs_ref[...], rot_ref[...], qg_ref[...], cos_q, sin_q)
    ko = head_norm_rope(k, hs_ref[:KV_WIDTH, :KV_WIDTH], rot_ref[:KV_WIDTH, :KV_WIDTH], kg_ref[...], cos, sin)
    q_ref[...] = qo.astype(BF16)
    k_ref[...] = jnp.concatenate([ko, pltpu.roll(ko, HEAD_DIM, axis=1)], axis=1).astype(BF16)
    v_ref[...] = jnp.concatenate([v, pltpu.roll(v, HEAD_DIM, axis=1)], axis=1).astype(BF16)
    u0 = ATTN_WIDTH + 2 * KV_WIDTH
    nchunk = u_scr.shape[1] // SSM_CHUNK
    for j in range(SSM_COLS):
        u_scr[j] = proj[:, u0 + j * LANES:u0 + (j + 1) * LANES]
    for s in range(SSM_CHUNK):
        for j in range(SSM_COLS):
            lanes = slice(s * SSM_WIDTH + j * LANES, s * SSM_WIDTH + (j + 1) * LANES)
            uc_ref[:, lanes] = u_scr[j, pl.ds(s, nchunk, stride=SSM_CHUNK), :].astype(BF16)


def _inproj(x, res, sc1, sh1, n1g, w_in, qg, kg, head_sum, rot, cos, sin, seq):
    t, d = x.shape
    tm = min(TOKEN_TILE, seq)
    per_b = seq // tm
    in_width = w_in.shape[1]
    tok = lambda w: pl.BlockSpec((tm, w), lambda i: (i, 0))
    const = lambda a: pl.BlockSpec(a.shape, lambda i: (0,) * a.ndim)
    per_batch = pl.BlockSpec((None, 1, d), lambda i: (i // per_b, 0, 0))
    chunked = pl.BlockSpec((tm // SSM_CHUNK, SSM_CHUNK * SSM_WIDTH), lambda i: (i, 0))
    ins, specs = [x], [tok(d)]
    if res is not None:
        y_prev, g2_prev = res
        ins += [y_prev, g2_prev]
        specs += [tok(d), per_batch]
    ins += [sc1, sh1, n1g, w_in, qg, kg, head_sum, rot, cos, sin]
    specs += [per_batch, per_batch, const(n1g), const(w_in), const(qg), const(kg), const(head_sum), const(rot),
              tok(LANES), tok(LANES)]
    out_shape = [jax.ShapeDtypeStruct((t, ATTN_WIDTH), BF16), jax.ShapeDtypeStruct((t, 2 * KV_WIDTH), BF16),
                 jax.ShapeDtypeStruct((t, 2 * KV_WIDTH), BF16),
                 jax.ShapeDtypeStruct((t // SSM_CHUNK, SSM_CHUNK * SSM_WIDTH), BF16)]
    out_specs = [tok(ATTN_WIDTH), tok(2 * KV_WIDTH), tok(2 * KV_WIDTH), chunked]
    if res is not None:
        out_shape.append(jax.ShapeDtypeStruct((t, d), F32))
        out_specs.append(tok(d))
    assert in_width == ATTN_WIDTH + 2 * KV_WIDTH + SSM_WIDTH
    return pl.pallas_call(
        functools.partial(_inproj_kernel, has_res=res is not None),
        grid=(t // tm,),
        in_specs=specs,
        out_specs=out_specs,
        out_shape=out_shape,
        scratch_shapes=[pltpu.VMEM((SSM_COLS, tm, LANES), F32)],
        compiler_params=_params(("parallel",), VMEM_LIMIT),
        name="inproj",
    )(*ins)


def _attn_kernel(sink_ref, q_ref, kc_ref, kp_ref, vc_ref, vp_ref, bias_ref, g_ref, o_ref):
    kk = jnp.concatenate([kp_ref[...], kc_ref[...]], axis=0)
    vv = jnp.concatenate([vp_ref[...], vc_ref[...]], axis=0)
    lane = lax.broadcasted_iota(jnp.int32, (2 * ATTN_BLOCK, KV_WIDTH), 1)
    low = lane < HEAD_DIM
    zero = jnp.zeros((2 * ATTN_BLOCK, KV_WIDTH), BF16)

    def variants(a):
        nat, swp = a[:, :KV_WIDTH], a[:, KV_WIDTH:]
        return {(0, 0): jnp.where(low, nat, zero), (0, 1): jnp.where(low, zero, swp),
                (1, 0): jnp.where(low, swp, zero), (1, 1): jnp.where(low, zero, nat)}

    kvar, vvar = variants(kk), variants(vv)
    bias = bias_ref[...]
    outs = []
    for pair in range(N_Q_HEADS // 2):
        qp = q_ref[:, pair * LANES:(pair + 1) * LANES]
        acc = jnp.zeros((ATTN_BLOCK, LANES), F32)
        for half in range(2):
            head = 2 * pair + half
            kv = head // Q_PER_KV
            s = lax.dot_general(qp, kvar[(kv, half)], (((1,), (1,)), ((), ())), preferred_element_type=F32) + bias
            sink = sink_ref[head]
            m = jnp.maximum(jnp.max(s, axis=-1, keepdims=True), sink)
            p = jnp.exp(s - m)
            denom = jnp.sum(p, axis=-1, keepdims=True) + jnp.exp(sink - m)
            o = jnp.dot(p.astype(BF16), vvar[(kv, half)], preferred_element_type=F32)
            acc = acc + o * (1.0 / denom)
        outs.append(acc)
    a = jnp.concatenate(outs, axis=1)
    o_ref[...] = _rms(a, g_ref[...]).astype(BF16)


def _attn_bias():
    qi = np.arange(ATTN_BLOCK)[:, None]
    sj = np.arange(2 * ATTN_BLOCK)[None, :]
    diff = qi + ATTN_BLOCK - sj
    band = (diff >= 0) & (diff < ATTN_BLOCK)
    first = band & (sj >= ATTN_BLOCK)
    return jnp.asarray(np.where(np.stack([first, band]), 0.0, MASK_BIAS).astype(np.float32))


def _attention(q, kx, vx, sink, out_g, bias, batch, seq):
    t = q.shape[0]
    nb = seq // ATTN_BLOCK
    cur = lambda w: pl.BlockSpec((ATTN_BLOCK, w), lambda b, n, s: (b * nb + n, 0))
    prev = lambda w: pl.BlockSpec((ATTN_BLOCK, w), lambda b, n, s: (b * nb + jnp.maximum(n - 1, 0), 0))
    grid_spec = pltpu.PrefetchScalarGridSpec(
        num_scalar_prefetch=1,
        grid=(batch, nb),
        in_specs=[cur(ATTN_WIDTH), cur(2 * KV_WIDTH), prev(2 * KV_WIDTH), cur(2 * KV_WIDTH), prev(2 * KV_WIDTH),
                  pl.BlockSpec((None, ATTN_BLOCK, 2 * ATTN_BLOCK), lambda b, n, s: (jnp.minimum(n, 1), 0, 0)),
                  pl.BlockSpec((1, ATTN_WIDTH), lambda b, n, s: (0, 0))],
        out_specs=cur(ATTN_WIDTH),
    )
    return pl.pallas_call(
        _attn_kernel,
        grid_spec=grid_spec,
        out_shape=jax.ShapeDtypeStruct((t, ATTN_WIDTH), BF16),
        compiler_params=_params(("parallel", "arbitrary")),
        name="swa_attention",
    )(sink, q, kx, kx, vx, vx, bias, out_g)


def _s5_prep_kernel(lr_re_ref, lr_im_ref, lc_re_ref, lc_im_ref, ldt_ref, bt_re_ref, bt_im_ref, ct_re_ref, ct_im_ref,
                    d_ref, lcol_re_ref, lcol_im_ref, ldtcol_ref, exp_ref, expt_ref, expw_ref,
                    t_ref, w_ref, v_ref, la_ref, lb_ref):
    hi = lax.Precision.HIGHEST
    nl = SSM_CHUNK
    low = lax.broadcasted_iota(jnp.int32, (1, 2 * SSM_STATE), 1) < SSM_STATE
    row_low = lax.broadcasted_iota(jnp.int32, (2 * SSM_STATE, 1), 0) < SSM_STATE
    jcol = lax.broadcasted_iota(jnp.int32, (nl, 1), 0).astype(F32)
    jrow = lax.broadcasted_iota(jnp.int32, (1, nl), 1).astype(F32)
    kt_lane = lax.broadcasted_iota(jnp.int32, (SSM_GROUP_CH, nl * SSM_GROUP_CH), 1)
    kt_row = lax.broadcasted_iota(jnp.int32, (SSM_GROUP_CH, nl * SSM_GROUP_CH), 0)

    w_all, v_all, kt_all = [], [], []
    for gm in range(COL_GROUPS):
        dt = jnp.exp(ldt_ref[gm])
        lam_re, lam_im = lr_re_ref[gm], lr_im_ref[gm]
        a_r, th_r = lam_re * dt, lam_im * dt
        a_c, th_c = lc_re_ref[gm] * dt, lc_im_ref[gm] * dt

        er = jnp.exp(jcol * a_r)
        pw_re, pw_im = er * jnp.cos(jcol * th_r), er * jnp.sin(jcol * th_r)

        e1 = jnp.exp(a_r)
        nr, ni = e1 * jnp.cos(th_r) - 1.0, e1 * jnp.sin(th_r)
        den = lam_re * lam_re + lam_im * lam_im
        c_re, c_im = (nr * lam_re + ni * lam_im) / den, (ni * lam_re - nr * lam_im) / den
        bt_re, bt_im = bt_re_ref[gm], bt_im_ref[gm]
        bb_re, bb_im = c_re * bt_re - c_im * bt_im, c_re * bt_im + c_im * bt_re

        w_rows = []
        for s in range(nl):
            j = nl - 1 - s
            pr, pi = pw_re[j:j + 1, :], pw_im[j:j + 1, :]
            w_rows.append(jnp.where(low, pr * bb_re - pi * bb_im, pr * bb_im + pi * bb_re))
        w_all.append(w_rows)

        ec = jnp.exp(a_c * jrow)
        pc = jnp.dot(ec * jnp.cos(th_c * jrow), exp_ref[...], precision=hi, preferred_element_type=F32)
        ps = jnp.dot(ec * jnp.sin(th_c * jrow), exp_ref[...], precision=hi, preferred_element_type=F32)
        ct_re, ct_im = ct_re_ref[gm], ct_im_ref[gm]
        a_re, a_im = ct_re * pc - ct_im * ps, ct_re * ps + ct_im * pc
        a_cat = jnp.where(row_low, a_re, -a_im)
        e1c = jnp.exp(a_c)
        l1_re, l1_im = e1c * jnp.cos(th_c), e1c * jnp.sin(th_c)
        v_re, v_im = a_re * l1_re - a_im * l1_im, a_re * l1_im + a_im * l1_re
        v_all.append(jnp.where(row_low, v_re, -v_im))

        kt = jnp.dot(jnp.where(low, bb_re, bb_im), a_cat, precision=hi, preferred_element_type=F32)
        kt_all.append(kt + jnp.where(kt_lane == kt_row, d_ref[gm], 0.0))

    def expand(stacked, expander, row_group, lane_group):
        wide = jnp.dot(stacked.astype(BF16), expander, preferred_element_type=F32)
        r = lax.broadcasted_iota(jnp.int32, wide.shape, 0)
        c = lax.broadcasted_iota(jnp.int32, wide.shape, 1)
        return jnp.where(row_group(r) == lane_group(c), wide, 0.0).astype(BF16)

    chan_group = lambda i: (i >> 4) & (COL_GROUPS - 1)
    state_group = lambda i: (i >> 6) & (COL_GROUPS - 1)

    bd = expand(jnp.concatenate(kt_all, axis=0), expt_ref[...], chan_group, chan_group)
    t_ref[0:LANES, :] = bd
    for s in range(1, nl):
        t_ref[s * LANES:(s + 1) * LANES, :] = jnp.concatenate(
            [jnp.zeros((LANES, s * LANES), BF16), bd[:, :CHUNK_LANES - s * LANES]], axis=1)

    w_stack = jnp.concatenate([w_all[gm][s] for s in range(nl) for gm in range(COL_GROUPS)], axis=0)
    w_ref[...] = expand(w_stack, expw_ref[...], chan_group, state_group)

    v_stack = jnp.concatenate([v_all[gm][half * SSM_STATE:(half + 1) * SSM_STATE, :]
                               for half in range(2) for gm in range(COL_GROUPS)], axis=0)
    v_ref[...] = expand(v_stack, expt_ref[...], state_group, chan_group)

    dtc = jnp.exp(ldtcol_ref[...])
    e16 = jnp.exp(nl * lcol_re_ref[...] * dtc)
    ang = nl * lcol_im_ref[...] * dtc
    la_ref[...] = e16 * jnp.cos(ang)
    lb_ref[...] = e16 * jnp.sin(ang)


def _s5_prep(lam_re, lam_im, b_re, b_im, c_re, c_im, d_skip, log_dt):
    g, p, h, nl = SSM_GROUPS, SSM_STATE, SSM_GROUP_CH, SSM_CHUNK
    nc, cg = SSM_COLS, COL_GROUPS
    col = lambda a: a.reshape((nc, cg) + a.shape[1:])
    dup_row = lambda a: col(jnp.tile(a, (1, 2)).reshape(g, 1, 2 * p))
    dup_col = lambda a: col(jnp.tile(a, (1, 2)).reshape(g, 2 * p, 1))
    bt = lambda a: col(jnp.tile(jnp.swapaxes(a, 1, 2), (1, 1, 2)))
    ct = lambda a: col(jnp.tile(jnp.swapaxes(a, 1, 2), (1, 2, nl)))
    d_pad = col(jnp.pad(d_skip.reshape(g, 1, h), ((0, 0), (0, 0), (0, nl * h - h))))
    wide = lambda a: a.reshape(nc, 1, cg * p)
    expand = jnp.asarray(np.repeat(np.eye(nl, dtype=np.float32), h, axis=1))
    exp_t = np.zeros((nl, h, nl, cg, h), np.float32)
    exp_w = np.zeros((2, p, 2, cg, p), np.float32)
    for gm in range(cg):
        exp_t[:, :, :, gm, :] = np.eye(nl * h, dtype=np.float32).reshape(nl, h, nl, h)
        exp_w[:, :, :, gm, :] = np.eye(2 * p, dtype=np.float32).reshape(2, p, 2, p)
    exp_t = jnp.asarray(exp_t.reshape(nl * h, CHUNK_LANES), BF16)
    exp_w = jnp.asarray(exp_w.reshape(2 * p, 2 * COL_STATE), BF16)
    blk = lambda *s: pl.BlockSpec((None,) + s, lambda i: (i,) + (0,) * len(s))
    const = lambda a: pl.BlockSpec(a.shape, lambda i: (0,) * a.ndim)
    lw = nl * h
    return pl.pallas_call(
        _s5_prep_kernel,
        grid=(nc,),
        in_specs=[blk(cg, 1, 2 * p), blk(cg, 1, 2 * p), blk(cg, 2 * p, 1), blk(cg, 2 * p, 1), blk(cg, 1, 1),
                  blk(cg, h, 2 * p), blk(cg, h, 2 * p), blk(cg, 2 * p, lw), blk(cg, 2 * p, lw), blk(cg, 1, lw),
                  blk(1, cg * p), blk(1, cg * p), blk(1, cg * p), const(expand), const(exp_t), const(exp_w)],
        out_specs=[blk(CHUNK_LANES, CHUNK_LANES), blk(CHUNK_LANES, 2 * COL_STATE), blk(2 * COL_STATE, CHUNK_LANES),
                   blk(1, COL_STATE), blk(1, COL_STATE)],
        out_shape=[jax.ShapeDtypeStruct((nc, CHUNK_LANES, CHUNK_LANES), BF16),
                   jax.ShapeDtypeStruct((nc, CHUNK_LANES, 2 * COL_STATE), BF16),
                   jax.ShapeDtypeStruct((nc, 2 * COL_STATE, CHUNK_LANES), BF16),
                   jax.ShapeDtypeStruct((nc, 1, COL_STATE), F32), jax.ShapeDtypeStruct((nc, 1, COL_STATE), F32)],
        compiler_params=_params(("parallel",), VMEM_LIMIT),
        name="s5_prep",
    )(dup_row(lam_re), dup_row(lam_im), dup_col(lam_re), dup_col(lam_im), col(log_dt.reshape(g, 1, 1)),
      bt(b_re), bt(b_im), ct(c_re), ct(c_im), d_pad, wide(lam_re), wide(lam_im),
      wide(jnp.repeat(log_dt, p)), expand, exp_t, exp_w)


def _s5_kernel(*refs, nchunks, nb):
    uc_refs = refs[:SSM_CHUNK]
    t_ref, w_ref, v_ref, la_ref, lb_ref, o_ref, ucat_ref, s_ref, xp_ref = refs[SSM_CHUNK:]

    @pl.when(pl.program_id(1) == 0)
    def _():
        for s in range(SSM_CHUNK):
            ucat_ref[:, s * LANES:(s + 1) * LANES] = uc_refs[s][...]
        s_in = jnp.dot(ucat_ref[...], w_ref[...], preferred_element_type=F32)
        nblk = COL_STATE // LANES
        for b in range(2 * nblk):
            s_ref[b] = s_in[:, b * LANES:(b + 1) * LANES]
        lr = [jnp.broadcast_to(la_ref[:, b * LANES:(b + 1) * LANES], (nb, LANES)) for b in range(nblk)]
        li = [jnp.broadcast_to(lb_ref[:, b * LANES:(b + 1) * LANES], (nb, LANES)) for b in range(nblk)]

        def step(c, carry):
            rows = pl.ds(c, nb, stride=nchunks)
            out = []
            for b in range(nblk):
                re, im = carry[2 * b], carry[2 * b + 1]
                xp_ref[b, rows, :] = re
                xp_ref[nblk + b, rows, :] = im
                out.append(lr[b] * re - li[b] * im + s_ref[b, rows, :])
                out.append(lr[b] * im + li[b] * re + s_ref[nblk + b, rows, :])
            return tuple(out)

        zero = jnp.zeros((nb, LANES), F32)
        lax.fori_loop(0, nchunks, step, (zero,) * (2 * nblk), unroll=4)

    xp = jnp.concatenate([xp_ref[b] for b in range(2 * COL_STATE // LANES)], axis=1).astype(BF16)
    o_ref[...] = (jnp.dot(ucat_ref[...], t_ref[...], preferred_element_type=F32)
                  + jnp.dot(xp, v_ref[...], preferred_element_type=F32)).astype(BF16)


def _s5_scan(uc, tcol, wcol, vcol, la, lb, nchunks, nb):
    rows = uc.shape[0]
    split = CHUNK_LANES // SSM_NSPLIT
    u_spec = lambda s: pl.BlockSpec((rows, LANES), lambda j, k: (0, SSM_COLS * s + j))
    return pl.pallas_call(
        functools.partial(_s5_kernel, nchunks=nchunks, nb=nb),
        grid=(SSM_COLS, SSM_NSPLIT),
        in_specs=[u_spec(s) for s in range(SSM_CHUNK)] + [
            pl.BlockSpec((None, CHUNK_LANES, split), lambda j, k: (j, 0, k)),
            pl.BlockSpec((None, CHUNK_LANES, 2 * COL_STATE), lambda j, k: (j, 0, 0)),
            pl.BlockSpec((None, 2 * COL_STATE, split), lambda j, k: (j, 0, k)),
            pl.BlockSpec((None, 1, COL_STATE), lambda j, k: (j, 0, 0)),
            pl.BlockSpec((None, 1, COL_STATE), lambda j, k: (j, 0, 0))],
        out_specs=pl.BlockSpec((None, rows, split), lambda j, k: (j, 0, k)),
        out_shape=jax.ShapeDtypeStruct((SSM_COLS, rows, CHUNK_LANES), BF16),
        scratch_shapes=[pltpu.VMEM((rows, CHUNK_LANES), BF16),
                        pltpu.VMEM((2 * COL_STATE // LANES, rows, LANES), F32),
                        pltpu.VMEM((2 * COL_STATE // LANES, rows, LANES), F32)],
        compiler_params=_params(("parallel", "arbitrary"), VMEM_LIMIT),
        name="s5_scan",
    )(*([uc] * SSM_CHUNK), tcol, wcol, vcol, la, lb)


def _route(logits, bias):
    m = jnp.max(logits, axis=0, keepdims=True)
    e = jnp.exp(logits - m)
    probs = e / jnp.sum(e, axis=0, keepdims=True)
    sel = probs + bias
    row = lambda a, i: a[i:i + 1, :]
    best_score, best = None, None
    for grp in range(N_EXPERT_GROUPS):
        a, b, c, d = (row(sel, EXPERTS_PER_GROUP * grp + i) for i in range(EXPERTS_PER_GROUP))
        hab, lab, hcd, lcd = jnp.maximum(a, b), jnp.minimum(a, b), jnp.maximum(c, d), jnp.minimum(c, d)
        top1 = jnp.maximum(hab, hcd)
        top2 = jnp.maximum(jnp.maximum(lab, lcd), jnp.minimum(hab, hcd))
        score = top1 + top2
        if grp == 0:
            best_score, best = score, jnp.zeros(score.shape, jnp.int32)
        else:
            better = score > best_score
            best = jnp.where(better, grp, best)
            best_score = jnp.where(better, score, best_score)

    def pick(a, i):
        out = row(a, i)
        for grp in range(1, N_EXPERT_GROUPS):
            out = jnp.where(best == grp, row(a, EXPERTS_PER_GROUP * grp + i), out)
        return out

    s_in = [pick(sel, i) for i in range(EXPERTS_PER_GROUP)]
    p_in = [pick(probs, i) for i in range(EXPERTS_PER_GROUP)]
    neg = jnp.full(s_in[0].shape, -jnp.inf, F32)

    def argmax_first(vals):
        idx, val = jnp.zeros(vals[0].shape, jnp.int32), vals[0]
        for i in range(1, len(vals)):
            better = vals[i] > val
            idx = jnp.where(better, i, idx)
            val = jnp.where(better, vals[i], val)
        return idx

    i1 = argmax_first(s_in)
    i2 = argmax_first([jnp.where(i1 == i, neg, s_in[i]) for i in range(EXPERTS_PER_GROUP)])
    zero = jnp.zeros(p_in[0].shape, F32)
    g1 = sum(jnp.where(i1 == i, p_in[i], zero) for i in range(EXPERTS_PER_GROUP))
    g2 = sum(jnp.where(i2 == i, p_in[i], zero) for i in range(EXPERTS_PER_GROUP))
    tot = g1 + g2
    w1, w2 = g1 / tot, g2 / tot
    cw = [jnp.where(i1 == i, w1, zero) + jnp.where(i2 == i, w2, zero) for i in range(EXPERTS_PER_GROUP)]
    return jnp.concatenate(cw, axis=0), best


def _post_kernel(x_ref, at_ref, yc_ref, wglu_ref, gs_ref, wo_ref, g1_ref, n2_ref, sc_ref, sh_ref,
                 wrt_ref, rb_ref, x1_ref, h2_ref, cw_ref, gid_ref, y_scr):
    nchunk = y_scr.shape[1] // SSM_CHUNK
    for s in range(SSM_CHUNK):
        for j in range(SSM_COLS):
            y_scr[j, pl.ds(s, nchunk, stride=SSM_CHUNK), :] = yc_ref[j, :, s * LANES:(s + 1) * LANES].astype(F32)
    yg = jax.nn.gelu(jnp.concatenate([y_scr[j] for j in range(SSM_COLS)], axis=1))
    z = yg * jax.nn.sigmoid(jnp.dot(yg.astype(BF16), wglu_ref[...], preferred_element_type=F32))
    zn = _rms(z, gs_ref[...]).astype(BF16)
    o = (jnp.dot(at_ref[...], wo_ref[:ATTN_WIDTH, :], preferred_element_type=F32)
         + jnp.dot(zn, wo_ref[ATTN_WIDTH:, :], preferred_element_type=F32))
    x1 = x_ref[...] + g1_ref[...] * o
    x1_ref[...] = x1
    h2 = _rms(x1, n2_ref[...]) * (1.0 + sc_ref[...]) + sh_ref[...]
    h2_ref[...] = h2
    logits = lax.dot_general(wrt_ref[...], h2, (((1,), (1,)), ((), ())), precision=lax.Precision.HIGHEST,
                             preferred_element_type=F32)
    cw, best = _route(logits, rb_ref[...])
    cw_ref[...] = cw
    gid_ref[...] = best


def _post(x, attn, yc, w_glu, ssm_g, w_out, g1, n2g, sc2, sh2, w_router_t, router_bias, seq):
    t, d = x.shape
    tm = min(TOKEN_TILE, seq)
    per_b = seq // tm
    tok = lambda w: pl.BlockSpec((tm, w), lambda i: (i, 0))
    const = lambda a: pl.BlockSpec(a.shape, lambda i: (0,) * a.ndim)
    per_batch = pl.BlockSpec((None, 1, d), lambda i: (i // per_b, 0, 0))
    col = lambda r: pl.BlockSpec((r, tm), lambda i: (0, i))
    chunked = pl.BlockSpec((SSM_COLS, tm // SSM_CHUNK, CHUNK_LANES), lambda i: (0, i, 0))
    return pl.pallas_call(
        _post_kernel,
        grid=(t // tm,),
        in_specs=[tok(d), tok(ATTN_WIDTH), chunked, const(w_glu), const(ssm_g),
                  const(w_out), per_batch, const(n2g), per_batch, per_batch, const(w_router_t), const(router_bias)],
        out_specs=[tok(d), tok(d), col(EXPERTS_PER_GROUP), col(1)],
        out_shape=[jax.ShapeDtypeStruct((t, d), F32), jax.ShapeDtypeStruct((t, d), F32),
                   jax.ShapeDtypeStruct((EXPERTS_PER_GROUP, t), F32), jax.ShapeDtypeStruct((1, t), jnp.int32)],
        scratch_shapes=[pltpu.VMEM((SSM_COLS, tm, LANES), F32)],
        compiler_params=_params(("parallel",), VMEM_LIMIT),
        name="post_mix",
    )(x, attn, yc, w_glu, ssm_g, w_out, g1, n2g, sc2, sh2, w_router_t, router_bias)


def _moe_kernel(rb_ref, grp_ref, first_ref, meta_ref, offs_ref, x_ref, cw_ref, wg_ref, wu_ref, wd_ref, o_ref):
    i = pl.program_id(0)

    @pl.when(i < meta_ref[0])
    def _():
        grp = grp_ref[i]
        rows = rb_ref[i] * MOE_ROWS + lax.broadcasted_iota(jnp.int32, (MOE_ROWS, 1), 0)
        in_grp = (rows >= offs_ref[grp]) & (rows < offs_ref[grp + 1])
        cw = jnp.where(in_grp, cw_ref[...], 0.0)
        x = x_ref[...].astype(BF16)
        y = None
        for e in range(EXPERTS_PER_GROUP):
            gate = jnp.dot(x, wg_ref[e], preferred_element_type=F32)
            up = jnp.dot(x, wu_ref[e], preferred_element_type=F32)
            act = (gate * jax.nn.sigmoid(gate) * up * cw[:, e:e + 1]).astype(BF16)
            ye = jnp.dot(act, wd_ref[e], preferred_element_type=F32)
            y = ye if y is None else y + ye

        @pl.when(first_ref[i] == 1)
        def _():
            o_ref[...] = y

        @pl.when(first_ref[i] == 0)
        def _():
            o_ref[...] += y


def _moe_items(gid, t):
    order = jnp.argsort(gid, stable=True).astype(jnp.int32)
    counts = jnp.sum((gid[None, :] == jnp.arange(N_EXPERT_GROUPS, dtype=jnp.int32)[:, None]).astype(jnp.int32), axis=1)
    offs = jnp.concatenate([jnp.zeros((1,), jnp.int32), jnp.cumsum(counts).astype(jnp.int32)])
    nblk = t // MOE_ROWS
    n_items_max = nblk + N_EXPERT_GROUPS - 1
    lo = jnp.arange(nblk, dtype=jnp.int32)[:, None] * MOE_ROWS
    valid = ((offs[None, :-1] < lo + MOE_ROWS) & (offs[None, 1:] > lo) & (offs[None, 1:] > offs[None, :-1])).reshape(-1)
    flat = jnp.arange(nblk * N_EXPERT_GROUPS, dtype=jnp.int32)
    ranked = jnp.sort(jnp.where(valid, flat, nblk * N_EXPERT_GROUPS))[:n_items_max]
    n_items = jnp.sum(valid.astype(jnp.int32))
    ranked = ranked[jnp.minimum(jnp.arange(n_items_max, dtype=jnp.int32), n_items - 1)]
    item_rb, item_grp = ranked // N_EXPERT_GROUPS, ranked % N_EXPERT_GROUPS
    first = jnp.concatenate([jnp.ones((1,), jnp.int32), (item_rb[1:] != item_rb[:-1]).astype(jnp.int32)])
    return order, offs, item_rb, item_grp, first, n_items.reshape(1), n_items_max


def _moe(xs, cws, wg, wu, wd, item_rb, item_grp, first, meta, offs, n_items_max):
    t, d = xs.shape
    ff = wg.shape[2]
    ng = EXPERTS_PER_GROUP
    imap = lambda f: (lambda i, rb, gr, fi, me, of: f(i, rb, gr))
    grid_spec = pltpu.PrefetchScalarGridSpec(
        num_scalar_prefetch=5,
        grid=(n_items_max,),
        in_specs=[pl.BlockSpec((MOE_ROWS, d), imap(lambda i, rb, gr: (rb[i], 0))),
                  pl.BlockSpec((MOE_ROWS, ng), imap(lambda i, rb, gr: (rb[i], 0))),
                  pl.BlockSpec((ng, d, ff), imap(lambda i, rb, gr: (gr[i], 0, 0))),
                  pl.BlockSpec((ng, d, ff), imap(lambda i, rb, gr: (gr[i], 0, 0))),
                  pl.BlockSpec((ng, ff, d), imap(lambda i, rb, gr: (gr[i], 0, 0)))],
        out_specs=pl.BlockSpec((MOE_ROWS, d), imap(lambda i, rb, gr: (rb[i], 0))),
    )
    return pl.pallas_call(
        _moe_kernel,
        grid_spec=grid_spec,
        out_shape=jax.ShapeDtypeStruct((t, d), F32),
        compiler_params=_params(("arbitrary",), VMEM_LIMIT),
        name="moe_grouped",
    )(item_rb, item_grp, first, meta, offs, xs, cws, wg, wu, wd)


def _take_rows(a, idx):
    return a.at[idx].get(mode="promise_in_bounds", unique_indices=True)


def _final_kernel(x_ref, y_ref, g_ref, o_ref):
    o_ref[...] = x_ref[...] + g_ref[...] * y_ref[...]


def _final(x1, y, g2, seq):
    t, d = x1.shape
    tm = min(TOKEN_TILE, seq)
    per_b = seq // tm
    tok = lambda w: pl.BlockSpec((tm, w), lambda i: (i, 0))
    return pl.pallas_call(
        _final_kernel,
        grid=(t // tm,),
        in_specs=[tok(d), tok(d), pl.BlockSpec((None, 1, d), lambda i: (i // per_b, 0, 0))],
        out_specs=tok(d),
        out_shape=jax.ShapeDtypeStruct((t, d), F32),
        compiler_params=_params(("parallel",)),
        name="final_residual",
    )(x1, y, g2)


def kernel(x, c, positions, ada_w, ada_b, norm1_g, w_in, q_norm_g, k_norm_g, attn_sink, lam_re, lam_im, ssm_b_re, ssm_b_im, ssm_c_re, ssm_c_im, ssm_d, ssm_log_dt, w_glu, attn_out_g, ssm_out_g, w_out, norm2_g, w_router, router_bias, w_exp_gate, w_exp_up, w_exp_down):
    batch, seq, d = x.shape
    depth = ada_w.shape[0]
    t = batch * seq
    assert seq % ATTN_BLOCK == 0 and seq % SSM_CHUNK == 0 and t % MOE_ROWS == 0

    mod = _adaln_mod(c, ada_w, ada_b).reshape(depth, 6, batch, 1, d)
    cos, sin = _rope_tables(positions)
    head_sum, rot = _rope_constants()
    bias = _attn_bias()
    w_router_t = w_router.T
    router_bias_col = router_bias.reshape(N_EXPERTS, 1)

    xf = x.reshape(t, d)
    res = None
    for l in range(depth):
        sh1, sc1, g1, sh2, sc2, g2 = (mod[l, j] for j in range(6))
        qg = (jnp.tile(q_norm_g[l], N_Q_HEADS) * HEAD_DIM ** -0.5).reshape(1, ATTN_WIDTH)
        kg = jnp.tile(k_norm_g[l], N_KV_HEADS).reshape(1, KV_WIDTH)
        outs = _inproj(xf, res, sc1, sh1, norm1_g[l].reshape(1, d), w_in[l].astype(BF16), qg, kg, head_sum, rot,
                       cos, sin, seq)
        if res is None:
            q, kx, vx, uc = outs
        else:
            q, kx, vx, uc, xf = outs
        attn = _attention(q, kx, vx, attn_sink[l], attn_out_g[l].reshape(1, ATTN_WIDTH), bias, batch, seq)
        tcol, wcol, vcol, la, lb = _s5_prep(lam_re[l], lam_im[l], ssm_b_re[l], ssm_b_im[l], ssm_c_re[l], ssm_c_im[l],
                                            ssm_d[l], ssm_log_dt[l])
        yc = _s5_scan(uc, tcol, wcol, vcol, la, lb, seq // SSM_CHUNK, batch)
        x1, h2, cw, gid = _post(xf, attn, yc, w_glu[l].astype(BF16), ssm_out_g[l].reshape(1, SSM_WIDTH),
                                 w_out[l].astype(BF16), g1, norm2_g[l].reshape(1, d), sc2, sh2, w_router_t,
                                 router_bias_col, seq)
        order, offs, item_rb, item_grp, first, meta, n_items_max = _moe_items(gid.reshape(t), t)
        y_sorted = _moe(_take_rows(h2, order), _take_rows(cw.T, order), w_exp_gate[l].astype(BF16),
                        w_exp_up[l].astype(BF16), w_exp_down[l].astype(BF16),
                        item_rb, item_grp, first, meta, offs, n_items_max)
        y = _take_rows(y_sorted, jnp.argsort(order).astype(jnp.int32))
        xf, res = x1, (y, g2)
    y, g2 = res
    return _final(xf, y, g2, seq).reshape(batch, seq, d)
```

```python
import functools

import numpy as np
import jax
import jax.numpy as jnp
from jax import lax
from jax.experimental import pallas as pl
from jax.experimental.pallas import tpu as pltpu

F32 = jnp.float32
BF16 = jnp.bfloat16

HEAD_DIM = 64
N_Q_HEADS = 8
N_KV_HEADS = 2
Q_PER_KV = N_Q_HEADS // N_KV_HEADS
ATTN_WIDTH = N_Q_HEADS * HEAD_DIM
KV_WIDTH = N_KV_HEADS * HEAD_DIM
ATTN_BLOCK = 128
ROPE_THETA = 10000.0
LANES = 128
SSM_GROUP_CH = 16
SSM_GROUPS = 32
SSM_WIDTH = SSM_GROUPS * SSM_GROUP_CH
SSM_STATE = 64
SSM_CHUNK = 16
SSM_COLS = SSM_WIDTH // LANES
COL_GROUPS = LANES // SSM_GROUP_CH
COL_STATE = COL_GROUPS * SSM_STATE
CHUNK_LANES = SSM_CHUNK * LANES
SSM_NSPLIT = 4
N_EXPERTS = 16
N_EXPERT_GROUPS = 4
EXPERTS_PER_GROUP = N_EXPERTS // N_EXPERT_GROUPS
PAIRS_PER_GROUP = EXPERTS_PER_GROUP * (EXPERTS_PER_GROUP - 1) // 2
EPS = 1e-6
MASK_BIAS = -1e30

TOKEN_TILE = 512
MOE_ROWS = 256
VMEM_LIMIT = 48 * 1024 * 1024


def _params(sem, vmem=None):
    return pltpu.CompilerParams(dimension_semantics=sem, vmem_limit_bytes=vmem)


def _rms(x, g):
    return x * lax.rsqrt(jnp.mean(x * x, axis=-1, keepdims=True) + EPS) * g


def _mod_kernel(c_ref, w_ref, b_ref, o_ref):
    c = c_ref[...]
    s = c * jax.nn.sigmoid(c)
    o_ref[...] = jnp.dot(s.astype(BF16), w_ref[...].astype(BF16), preferred_element_type=F32) + b_ref[...]


def _adaln_mod(c, ada_w, ada_b):
    depth, d, d6 = ada_w.shape
    nb = c.shape[0]
    n6 = d6 // d
    return pl.pallas_call(
        _mod_kernel,
        grid=(depth, n6),
        in_specs=[pl.BlockSpec((nb, d), lambda l, j: (0, 0)),
                  pl.BlockSpec((None, d, d), lambda l, j: (l, 0, j)),
                  pl.BlockSpec((None, None, 1, d), lambda l, j: (l, j, 0, 0))],
        out_specs=pl.BlockSpec((None, None, nb, d), lambda l, j: (l, j, 0, 0)),
        out_shape=jax.ShapeDtypeStruct((depth, n6, nb, d), F32),
        compiler_params=_params(("arbitrary", "arbitrary"), VMEM_LIMIT),
        name="adaln_mod",
    )(c, ada_w, ada_b.reshape(depth, n6, 1, d))


def _rope_kernel(pos_ref, freq_ref, cos_ref, sin_ref):
    ang = pos_ref[...].astype(F32) * freq_ref[...]
    cos_ref[...] = jnp.cos(ang)
    sin_ref[...] = jnp.sin(ang)


def _rope_tables(positions):
    half = HEAD_DIM // 2
    t = positions.size
    per_row = LANES // half
    rows = t // per_row
    pos_rep = jnp.repeat(positions.reshape(rows, per_row), half, axis=1)
    freq = (ROPE_THETA ** (-np.arange(half, dtype=np.float64) / half)).astype(np.float32)
    freq_row = jnp.asarray(np.tile(freq, per_row)[None, :])
    blk = min(rows, 512)
    cos, sin = pl.pallas_call(
        _rope_kernel,
        grid=(rows // blk,),
        in_specs=[pl.BlockSpec((blk, LANES), lambda i: (i, 0)),
                  pl.BlockSpec((1, LANES), lambda i: (0, 0))],
        out_specs=[pl.BlockSpec((blk, LANES), lambda i: (i, 0))] * 2,
        out_shape=[jax.ShapeDtypeStruct((rows, LANES), F32)] * 2,
        compiler_params=_params(("arbitrary",)),
        name="rope_tables",
    )(pos_rep, freq_row)
    widen = lambda a: jnp.tile(a.reshape(t, half), (1, per_row))
    return widen(cos), widen(sin)


def _rope_constants():
    lane = np.arange(ATTN_WIDTH)
    head_sum = (lane[:, None] // HEAD_DIM == lane[None, :] // HEAD_DIM).astype(np.float32)
    half = HEAD_DIM // 2
    rot = np.zeros((ATTN_WIDTH, ATTN_WIDTH), np.float32)
    for d in range(ATTN_WIDTH):
        if d % HEAD_DIM < half:
            rot[d + half, d] = -1.0
        else:
            rot[d - half, d] = 1.0
    return jnp.asarray(head_sum, BF16), jnp.asarray(rot, BF16)


def _inproj_kernel(*refs, has_res):
    if has_res:
        (x_ref, y_ref, g2_ref, sc_ref, sh_ref, n1_ref, w_ref, qg_ref, kg_ref, hs_ref, rot_ref,
         cos_ref, sin_ref, q_ref, k_ref, v_ref, uc_ref, xo_ref, u_scr) = refs
        x = x_ref[...] + g2_ref[...] * y_ref[...]
        xo_ref[...] = x
    else:
        (x_ref, sc_ref, sh_ref, n1_ref, w_ref, qg_ref, kg_ref, hs_ref, rot_ref,
         cos_ref, sin_ref, q_ref, k_ref, v_ref, uc_ref, u_scr) = refs
        x = x_ref[...]
    h = _rms(x, n1_ref[...]) * (1.0 + sc_ref[...]) + sh_ref[...]
    proj = jnp.dot(h.astype(BF16), w_ref[...], preferred_element_type=F32)
    q = proj[:, :ATTN_WIDTH]
    k = proj[:, ATTN_WIDTH:ATTN_WIDTH + KV_WIDTH]
    v = proj[:, ATTN_WIDTH + KV_WIDTH:ATTN_WIDTH + 2 * KV_WIDTH]
    cos = cos_ref[...]
    sin = sin_ref[...]
    reps = ATTN_WIDTH // LANES
    cos_q = jnp.concatenate([cos] * reps, axis=1)
    sin_q = jnp.concatenate([sin] * reps, axis=1)

    def head_norm_rope(t, head_sum, rot, gain, c, s):
        ssq = jnp.dot((t * t).astype(BF16), head_sum, preferred_element_type=F32)
        tn = (t * lax.rsqrt(ssq * (1.0 / HEAD_DIM) + EPS) * gain).astype(BF16)
        tr = jnp.dot(tn, rot, preferred_element_type=F32)
        return tn.astype(F32) * c + tr * s

    qo = head_norm_rope(q, hs_ref[...], rot_ref[...], qg_ref[...], cos_q, sin_q)
    ko = head_norm_rope(k, hs_ref[:KV_WIDTH, :KV_WIDTH], rot_ref[:KV_WIDTH, :KV_WIDTH], kg_ref[...], cos, sin)
    q_ref[...] = qo.astype(BF16)
    k_ref[...] = jnp.concatenate([ko, pltpu.roll(ko, HEAD_DIM, axis=1)], axis=1).astype(BF16)
    v_ref[...] = jnp.concatenate([v, pltpu.roll(v, HEAD_DIM, axis=1)], axis=1).astype(BF16)
    u0 = ATTN_WIDTH + 2 * KV_WIDTH
    nchunk = u_scr.shape[1] // SSM_CHUNK
    for j in range(SSM_COLS):
        u_scr[j] = proj[:, u0 + j * LANES:u0 + (j + 1) * LANES]
    for s in range(SSM_CHUNK):
        for j in range(SSM_COLS):
            lanes = slice(s * SSM_WIDTH + j * LANES, s * SSM_WIDTH + (j + 1) * LANES)
            uc_ref[:, lanes] = u_scr[j, pl.ds(s, nchunk, stride=SSM_CHUNK), :].astype(BF16)


def _inproj(x, res, sc1, sh1, n1g, w_in, qg, kg, head_sum, rot, cos, sin, seq):
    t, d = x.shape
    tm = min(TOKEN_TILE, seq)
    per_b = seq // tm
    in_width = w_in.shape[1]
    tok = lambda w: pl.BlockSpec((tm, w), lambda i: (i, 0))
    const = lambda a: pl.BlockSpec(a.shape, lambda i: (0,) * a.ndim)
    per_batch = pl.BlockSpec((None, 1, d), lambda i: (i // per_b, 0, 0))
    chunked = pl.BlockSpec((tm // SSM_CHUNK, SSM_CHUNK * SSM_WIDTH), lambda i: (i, 0))
    ins, specs = [x], [tok(d)]
    if res is not None:
        y_prev, g2_prev = res
        ins += [y_prev, g2_prev]
        specs += [tok(d), per_batch]
    ins += [sc1, sh1, n1g, w_in, qg, kg, head_sum, rot, cos, sin]
    specs += [per_batch, per_batch, const(n1g), const(w_in), const(qg), const(kg), const(head_sum), const(rot),
              tok(LANES), tok(LANES)]
    out_shape = [jax.ShapeDtypeStruct((t, ATTN_WIDTH), BF16), jax.ShapeDtypeStruct((t, 2 * KV_WIDTH), BF16),
                 jax.ShapeDtypeStruct((t, 2 * KV_WIDTH), BF16),
                 jax.ShapeDtypeStruct((t // SSM_CHUNK, SSM_CHUNK * SSM_WIDTH), BF16)]
    out_specs = [tok(ATTN_WIDTH), tok(2 * KV_WIDTH), tok(2 * KV_WIDTH), chunked]
    if res is not None:
        out_shape.append(jax.ShapeDtypeStruct((t, d), F32))
        out_specs.append(tok(d))
    assert in_width == ATTN_WIDTH + 2 * KV_WIDTH + SSM_WIDTH
    return pl.pallas_call(
        functools.partial(_inproj_kernel, has_res=res is not None),
        grid=(t // tm,),
        in_specs=specs,
        out_specs=out_specs,
        out_shape=out_shape,
        scratch_shapes=[pltpu.VMEM((SSM_COLS, tm, LANES), F32)],
        compiler_params=_params(("parallel",), VMEM_LIMIT),
        name="inproj",
    )(*ins)


def _attn_kernel(sink_ref, q_ref, kc_ref, kp_ref, vc_ref, vp_ref, bias_ref, g_ref, o_ref):
    kk = jnp.concatenate([kp_ref[...], kc_ref[...]], axis=0)
    vv = jnp.concatenate([vp_ref[...], vc_ref[...]], axis=0)
    lane = lax.broadcasted_iota(jnp.int32, (2 * ATTN_BLOCK, KV_WIDTH), 1)
    low = lane < HEAD_DIM
    zero = jnp.zeros((2 * ATTN_BLOCK, KV_WIDTH), BF16)

    def variants(a):
        nat, swp = a[:, :KV_WIDTH], a[:, KV_WIDTH:]
        return {(0, 0): jnp.where(low, nat, zero), (0, 1): jnp.where(low, zero, swp),
                (1, 0): jnp.where(low, swp, zero), (1, 1): jnp.where(low, zero, nat)}

    kvar, vvar = variants(kk), variants(vv)
    bias = bias_ref[...]
    outs = []
    for pair in range(N_Q_HEADS // 2):
        qp = q_ref[:, pair * LANES:(pair + 1) * LANES]
        acc = jnp.zeros((ATTN_BLOCK, LANES), F32)
        for half in range(2):
            head = 2 * pair + half
            kv = head // Q_PER_KV
            s = lax.dot_general(qp, kvar[(kv, half)], (((1,), (1,)), ((), ())), preferred_element_type=F32) + bias
            sink = sink_ref[head]
            m = jnp.maximum(jnp.max(s, axis=-1, keepdims=True), sink)
            p = jnp.exp(s - m)
            denom = jnp.sum(p, axis=-1, keepdims=True) + jnp.exp(sink - m)
            o = jnp.dot(p.astype(BF16), vvar[(kv, half)], preferred_element_type=F32)
            acc = acc + o * (1.0 / denom)
        outs.append(acc)
    a = jnp.concatenate(outs, axis=1)
    o_ref[...] = _rms(a, g_ref[...]).astype(BF16)


def _attn_bias():
    qi = np.arange(ATTN_BLOCK)[:, None]
    sj = np.arange(2 * ATTN_BLOCK)[None, :]
    diff = qi + ATTN_BLOCK - sj
    band = (diff >= 0) & (diff < ATTN_BLOCK)
    first = band & (sj >= ATTN_BLOCK)
    return jnp.asarray(np.where(np.stack([first, band]), 0.0, MASK_BIAS).astype(np.float32))


def _attention(q, kx, vx, sink, out_g, bias, batch, seq):
    t = q.shape[0]
    nb = seq // ATTN_BLOCK
    cur = lambda w: pl.BlockSpec((ATTN_BLOCK, w), lambda b, n, s: (b * nb + n, 0))
    prev = lambda w: pl.BlockSpec((ATTN_BLOCK, w), lambda b, n, s: (b * nb + jnp.maximum(n - 1, 0), 0))
    grid_spec = pltpu.PrefetchScalarGridSpec(
        num_scalar_prefetch=1,
        grid=(batch, nb),
        in_specs=[cur(ATTN_WIDTH), cur(2 * KV_WIDTH), prev(2 * KV_WIDTH), cur(2 * KV_WIDTH), prev(2 * KV_WIDTH),
                  pl.BlockSpec((None, ATTN_BLOCK, 2 * ATTN_BLOCK), lambda b, n, s: (jnp.minimum(n, 1), 0, 0)),
                  pl.BlockSpec((1, ATTN_WIDTH), lambda b, n, s: (0, 0))],
        out_specs=cur(ATTN_WIDTH),
    )
    return pl.pallas_call(
        _attn_kernel,
        grid_spec=grid_spec,
        out_shape=jax.ShapeDtypeStruct((t, ATTN_WIDTH), BF16),
        compiler_params=_params(("parallel", "arbitrary")),
        name="swa_attention",
    )(sink, q, kx, kx, vx, vx, bias, out_g)


def _s5_prep_kernel(lr_re_ref, lr_im_ref, lc_re_ref, lc_im_ref, ldt_ref, bt_re_ref, bt_im_ref, ct_re_ref, ct_im_ref,
                    d_ref, lcol_re_ref, lcol_im_ref, ldtcol_ref, exp_ref, expt_ref, expw_ref,
                    t_ref, w_ref, v_ref, la_ref, lb_ref):
    hi = lax.Precision.HIGHEST
    nl = SSM_CHUNK
    low = lax.broadcasted_iota(jnp.int32, (1, 2 * SSM_STATE), 1) < SSM_STATE
    row_low = lax.broadcasted_iota(jnp.int32, (2 * SSM_STATE, 1), 0) < SSM_STATE
    jcol = lax.broadcasted_iota(jnp.int32, (nl, 1), 0).astype(F32)
    jrow = lax.broadcasted_iota(jnp.int32, (1, nl), 1).astype(F32)
    kt_lane = lax.broadcasted_iota(jnp.int32, (SSM_GROUP_CH, nl * SSM_GROUP_CH), 1)
    kt_row = lax.broadcasted_iota(jnp.int32, (SSM_GROUP_CH, nl * SSM_GROUP_CH), 0)

    w_all, v_all, kt_all = [], [], []
    for gm in range(COL_GROUPS):
        dt = jnp.exp(ldt_ref[gm])
        lam_re, lam_im = lr_re_ref[gm], lr_im_ref[gm]
        a_r, th_r = lam_re * dt, lam_im * dt
        a_c, th_c = lc_re_ref[gm] * dt, lc_im_ref[gm] * dt

        er = jnp.exp(jcol * a_r)
        pw_re, pw_im = er * jnp.cos(jcol * th_r), er * jnp.sin(jcol * th_r)

        e1 = jnp.exp(a_r)
        nr, ni = e1 * jnp.cos(th_r) - 1.0, e1 * jnp.sin(th_r)
        den = lam_re * lam_re + lam_im * lam_im
        c_re, c_im = (nr * lam_re + ni * lam_im) / den, (ni * lam_re - nr * lam_im) / den
        bt_re, bt_im = bt_re_ref[gm], bt_im_ref[gm]
        bb_re, bb_im = c_re * bt_re - c_im * bt_im, c_re * bt_im + c_im * bt_re

        w_rows = []
        for s in range(nl):
            j = nl - 1 - s
            pr, pi = pw_re[j:j + 1, :], pw_im[j:j + 1, :]
            w_rows.append(jnp.where(low, pr * bb_re - pi * bb_im, pr * bb_im + pi * bb_re))
        w_all.append(w_rows)

        ec = jnp.exp(a_c * jrow)
        pc = jnp.dot(ec * jnp.cos(th_c * jrow), exp_ref[...], precision=hi, preferred_element_type=F32)
        ps = jnp.dot(ec * jnp.sin(th_c * jrow), exp_ref[...], precision=hi, preferred_element_type=F32)
        ct_re, ct_im = ct_re_ref[gm], ct_im_ref[gm]
        a_re, a_im = ct_re * pc - ct_im * ps, ct_re * ps + ct_im * pc
        a_cat = jnp.where(row_low, a_re, -a_im)
        e1c = jnp.exp(a_c)
        l1_re, l1_im = e1c * jnp.cos(th_c), e1c * jnp.sin(th_c)
        v_re, v_im = a_re * l1_re - a_im * l1_im, a_re * l1_im + a_im * l1_re
        v_all.append(jnp.where(row_low, v_re, -v_im))

        kt = jnp.dot(jnp.where(low, bb_re, bb_im), a_cat, precision=hi, preferred_element_type=F32)
        kt_all.append(kt + jnp.where(kt_lane == kt_row, d_ref[gm], 0.0))

    def expand(stacked, expander, row_group, lane_group):
        wide = jnp.dot(stacked.astype(BF16), expander, preferred_element_type=F32)
        r = lax.broadcasted_iota(jnp.int32, wide.shape, 0)
        c = lax.broadcasted_iota(jnp.int32, wide.shape, 1)
        return jnp.where(row_group(r) == lane_group(c), wide, 0.0).astype(BF16)

    chan_group = lambda i: (i >> 4) & (COL_GROUPS - 1)
    state_group = lambda i: (i >> 6) & (COL_GROUPS - 1)

    bd = expand(jnp.concatenate(kt_all, axis=0), expt_ref[...], chan_group, chan_group)
    t_ref[0:LANES, :] = bd
    for s in range(1, nl):
        t_ref[s * LANES:(s + 1) * LANES, :] = jnp.concatenate(
            [jnp.zeros((LANES, s * LANES), BF16), bd[:, :CHUNK_LANES - s * LANES]], axis=1)

    w_stack = jnp.concatenate([w_all[gm][s] for s in range(nl) for gm in range(COL_GROUPS)], axis=0)
    w_ref[...] = expand(w_stack, expw_ref[...], chan_group, state_group)

    v_stack = jnp.concatenate([v_all[gm][half * SSM_STATE:(half + 1) * SSM_STATE, :]
                               for half in range(2) for gm in range(COL_GROUPS)], axis=0)
    v_ref[...] = expand(v_stack, expt_ref[...], state_group, chan_group)

    dtc = jnp.exp(ldtcol_ref[...])
    e16 = jnp.exp(nl * lcol_re_ref[...] * dtc)
    ang = nl * lcol_im_ref[...] * dtc
    la_ref[...] = e16 * jnp.cos(ang)
    lb_ref[...] = e16 * jnp.sin(ang)


def _s5_prep(lam_re, lam_im, b_re, b_im, c_re, c_im, d_skip, log_dt):
    g, p, h, nl = SSM_GROUPS, SSM_STATE, SSM_GROUP_CH, SSM_CHUNK
    nc, cg = SSM_COLS, COL_GROUPS
    col = lambda a: a.reshape((nc, cg) + a.shape[1:])
    dup_row = lambda a: col(jnp.tile(a, (1, 2)).reshape(g, 1, 2 * p))
    dup_col = lambda a: col(jnp.tile(a, (1, 2)).reshape(g, 2 * p, 1))
    bt = lambda a: col(jnp.tile(jnp.swapaxes(a, 1, 2), (1, 1, 2)))
    ct = lambda a: col(jnp.tile(jnp.swapaxes(a, 1, 2), (1, 2, nl)))
    d_pad = col(jnp.pad(d_skip.reshape(g, 1, h), ((0, 0), (0, 0), (0, nl * h - h))))
    wide = lambda a: a.reshape(nc, 1, cg * p)
    expand = jnp.asarray(np.repeat(np.eye(nl, dtype=np.float32), h, axis=1))
    exp_t = np.zeros((nl, h, nl, cg, h), np.float32)
    exp_w = np.zeros((2, p, 2, cg, p), np.float32)
    for gm in range(cg):
        exp_t[:, :, :, gm, :] = np.eye(nl * h, dtype=np.float32).reshape(nl, h, nl, h)
        exp_w[:, :, :, gm, :] = np.eye(2 * p, dtype=np.float32).reshape(2, p, 2, p)
    exp_t = jnp.asarray(exp_t.reshape(nl * h, CHUNK_LANES), BF16)
    exp_w = jnp.asarray(exp_w.reshape(2 * p, 2 * COL_STATE), BF16)
    blk = lambda *s: pl.BlockSpec((None,) + s, lambda i: (i,) + (0,) * len(s))
    const = lambda a: pl.BlockSpec(a.shape, lambda i: (0,) * a.ndim)
    lw = nl * h
    return pl.pallas_call(
        _s5_prep_kernel,
        grid=(nc,),
        in_specs=[blk(cg, 1, 2 * p), blk(cg, 1, 2 * p), blk(cg, 2 * p, 1), blk(cg, 2 * p, 1), blk(cg, 1, 1),
                  blk(cg, h, 2 * p), blk(cg, h, 2 * p), blk(cg, 2 * p, lw), blk(cg, 2 * p, lw), blk(cg, 1, lw),
                  blk(1, cg * p), blk(1, cg * p), blk(1, cg * p), const(expand), const(exp_t), const(exp_w)],
        out_specs=[blk(CHUNK_LANES, CHUNK_LANES), blk(CHUNK_LANES, 2 * COL_STATE), blk(2 * COL_STATE, CHUNK_LANES),
                   blk(1, COL_STATE), blk(1, COL_STATE)],
        out_shape=[jax.ShapeDtypeStruct((nc, CHUNK_LANES, CHUNK_LANES), BF16),
                   jax.ShapeDtypeStruct((nc, CHUNK_LANES, 2 * COL_STATE), BF16),
                   jax.ShapeDtypeStruct((nc, 2 * COL_STATE, CHUNK_LANES), BF16),
                   jax.ShapeDtypeStruct((nc, 1, COL_STATE), F32), jax.ShapeDtypeStruct((nc, 1, COL_STATE), F32)],
        compiler_params=_params(("parallel",), VMEM_LIMIT),
        name="s5_prep",
    )(dup_row(lam_re), dup_row(lam_im), dup_col(lam_re), dup_col(lam_im), col(log_dt.reshape(g, 1, 1)),
      bt(b_re), bt(b_im), ct(c_re), ct(c_im), d_pad, wide(lam_re), wide(lam_im),
      wide(jnp.repeat(log_dt, p)), expand, exp_t, exp_w)


def _s5_kernel(*refs, nchunks, nb):
    uc_refs = refs[:SSM_CHUNK]
    t_ref, w_ref, v_ref, la_ref, lb_ref, o_ref, ucat_ref, s_ref, xp_ref = refs[SSM_CHUNK:]

    @pl.when(pl.program_id(1) == 0)
    def _():
        for s in range(SSM_CHUNK):
            ucat_ref[:, s * LANES:(s + 1) * LANES] = uc_refs[s][...]
        s_in = jnp.dot(ucat_ref[...], w_ref[...], preferred_element_type=F32)
        nblk = COL_STATE // LANES
        for b in range(2 * nblk):
            s_ref[b] = s_in[:, b * LANES:(b + 1) * LANES]
        lr = [jnp.broadcast_to(la_ref[:, b * LANES:(b + 1) * LANES], (nb, LANES)) for b in range(nblk)]
        li = [jnp.broadcast_to(lb_ref[:, b * LANES:(b + 1) * LANES], (nb, LANES)) for b in range(nblk)]

        def step(c, carry):
            rows = pl.ds(c, nb, stride=nchunks)
            out = []
            for b in range(nblk):
                re, im = carry[2 * b], carry[2 * b + 1]
                xp_ref[b, rows, :] = re
                xp_ref[nblk + b, rows, :] = im
                out.append(lr[b] * re - li[b] * im + s_ref[b, rows, :])
                out.append(lr[b] * im + li[b] * re + s_ref[nblk + b, rows, :])
            return tuple(out)

        zero = jnp.zeros((nb, LANES), F32)
        lax.fori_loop(0, nchunks, step, (zero,) * (2 * nblk), unroll=4)

    xp = jnp.concatenate([xp_ref[b] for b in range(2 * COL_STATE // LANES)], axis=1).astype(BF16)
    o_ref[...] = (jnp.dot(ucat_ref[...], t_ref[...], preferred_element_type=F32)
                  + jnp.dot(xp, v_ref[...], preferred_element_type=F32)).astype(BF16)


def _s5_scan(uc, tcol, wcol, vcol, la, lb, nchunks, nb):
    rows = uc.shape[0]
    split = CHUNK_LANES // SSM_NSPLIT
    u_spec = lambda s: pl.BlockSpec((rows, LANES), lambda j, k: (0, SSM_COLS * s + j))
    return pl.pallas_call(
        functools.partial(_s5_kernel, nchunks=nchunks, nb=nb),
        grid=(SSM_COLS, SSM_NSPLIT),
        in_specs=[u_spec(s) for s in range(SSM_CHUNK)] + [
            pl.BlockSpec((None, CHUNK_LANES, split), lambda j, k: (j, 0, k)),
            pl.BlockSpec((None, CHUNK_LANES, 2 * COL_STATE), lambda j, k: (j, 0, 0)),
            pl.BlockSpec((None, 2 * COL_STATE, split), lambda j, k: (j, 0, k)),
            pl.BlockSpec((None, 1, COL_STATE), lambda j, k: (j, 0, 0)),
            pl.BlockSpec((None, 1, COL_STATE), lambda j, k: (j, 0, 0))],
        out_specs=pl.BlockSpec((None, rows, split), lambda j, k: (j, 0, k)),
        out_shape=jax.ShapeDtypeStruct((SSM_COLS, rows, CHUNK_LANES), BF16),
        scratch_shapes=[pltpu.VMEM((rows, CHUNK_LANES), BF16),
                        pltpu.VMEM((2 * COL_STATE // LANES, rows, LANES), F32),
                        pltpu.VMEM((2 * COL_STATE // LANES, rows, LANES), F32)],
        compiler_params=_params(("parallel", "arbitrary"), VMEM_LIMIT),
        name="s5_scan",
    )(*([uc] * SSM_CHUNK), tcol, wcol, vcol, la, lb)


def _route(logits, bias):
    m = jnp.max(logits, axis=0, keepdims=True)
    e = jnp.exp(logits - m)
    probs = e / jnp.sum(e, axis=0, keepdims=True)
    sel = probs + bias
    row = lambda a, i: a[i:i + 1, :]
    best_score, best = None, None
    for grp in range(N_EXPERT_GROUPS):
        a, b, c, d = (row(sel, EXPERTS_PER_GROUP * grp + i) for i in range(EXPERTS_PER_GROUP))
        hab, lab, hcd, lcd = jnp.maximum(a, b), jnp.minimum(a, b), jnp.maximum(c, d), jnp.minimum(c, d)
        top1 = jnp.maximum(hab, hcd)
        top2 = jnp.maximum(jnp.maximum(lab, lcd), jnp.minimum(hab, hcd))
        score = top1 + top2
        if grp == 0:
            best_score, best = score, jnp.zeros(score.shape, jnp.int32)
        else:
            better = score > best_score
            best = jnp.where(better, grp, best)
            best_score = jnp.where(better, score, best_score)

    def pick(a, i):
        out = row(a, i)
        for grp in range(1, N_EXPERT_GROUPS):
            out = jnp.where(best == grp, row(a, EXPERTS_PER_GROUP * grp + i), out)
        return out

    s_in = [pick(sel, i) for i in range(EXPERTS_PER_GROUP)]
    p_in = [pick(probs, i) for i in range(EXPERTS_PER_GROUP)]
    neg = jnp.full(s_in[0].shape, -jnp.inf, F32)

    def argmax_first(vals):
        idx, val = jnp.zeros(vals[0].shape, jnp.int32), vals[0]
        for i in range(1, len(vals)):
            better = vals[i] > val
            idx = jnp.where(better, i, idx)
            val = jnp.where(better, vals[i], val)
        return idx

    i1 = argmax_first(s_in)
    i2 = argmax_first([jnp.where(i1 == i, neg, s_in[i]) for i in range(EXPERTS_PER_GROUP)])
    zero = jnp.zeros(p_in[0].shape, F32)
    g1 = sum(jnp.where(i1 == i, p_in[i], zero) for i in range(EXPERTS_PER_GROUP))
    g2 = sum(jnp.where(i2 == i, p_in[i], zero) for i in range(EXPERTS_PER_GROUP))
    tot = g1 + g2
    w1, w2 = g1 / tot, g2 / tot
    first_low = i1 < i2
    low, high = jnp.minimum(i1, i2), jnp.maximum(i1, i2)
    pair_base = jnp.where(low == 0, 0, jnp.where(low == 1, 3, 5))
    bucket = best * PAIRS_PER_GROUP + pair_base + (high - low - 1)
    return jnp.concatenate([jnp.where(first_low, w1, w2), jnp.where(first_low, w2, w1)], axis=0), bucket


def _post_kernel(x_ref, at_ref, yc_ref, wglu_ref, gs_ref, wo_ref, g1_ref, n2_ref, sc_ref, sh_ref,
                 wrt_ref, rb_ref, x1_ref, h2_ref, cw_ref, gid_ref, y_scr):
    nchunk = y_scr.shape[1] // SSM_CHUNK
    for s in range(SSM_CHUNK):
        for j in range(SSM_COLS):
            y_scr[j, pl.ds(s, nchunk, stride=SSM_CHUNK), :] = yc_ref[j, :, s * LANES:(s + 1) * LANES].astype(F32)
    yg = jax.nn.gelu(jnp.concatenate([y_scr[j] for j in range(SSM_COLS)], axis=1))
    z = yg * jax.nn.sigmoid(jnp.dot(yg.astype(BF16), wglu_ref[...], preferred_element_type=F32))
    zn = _rms(z, gs_ref[...]).astype(BF16)
    o = (jnp.dot(at_ref[...], wo_ref[:ATTN_WIDTH, :], preferred_element_type=F32)
         + jnp.dot(zn, wo_ref[ATTN_WIDTH:, :], preferred_element_type=F32))
    x1 = x_ref[...] + g1_ref[...] * o
    x1_ref[...] = x1
    h2 = _rms(x1, n2_ref[...]) * (1.0 + sc_ref[...]) + sh_ref[...]
    h2_ref[...] = h2
    logits = lax.dot_general(wrt_ref[...], h2, (((1,), (1,)), ((), ())), precision=lax.Precision.HIGHEST,
                             preferred_element_type=F32)
    cw, bucket = _route(logits, rb_ref[...])
    cw_ref[...] = cw
    gid_ref[...] = bucket


def _post(x, attn, yc, w_glu, ssm_g, w_out, g1, n2g, sc2, sh2, w_router_t, router_bias, seq):
    t, d = x.shape
    tm = min(TOKEN_TILE, seq)
    per_b = seq // tm
    tok = lambda w: pl.BlockSpec((tm, w), lambda i: (i, 0))
    const = lambda a: pl.BlockSpec(a.shape, lambda i: (0,) * a.ndim)
    per_batch = pl.BlockSpec((None, 1, d), lambda i: (i // per_b, 0, 0))
    col = lambda r: pl.BlockSpec((r, tm), lambda i: (0, i))
    chunked = pl.BlockSpec((SSM_COLS, tm // SSM_CHUNK, CHUNK_LANES), lambda i: (0, i, 0))
    return pl.pallas_call(
        _post_kernel,
        grid=(t // tm,),
        in_specs=[tok(d), tok(ATTN_WIDTH), chunked, const(w_glu), const(ssm_g),
                  const(w_out), per_batch, const(n2g), per_batch, per_batch, const(w_router_t), const(router_bias)],
        out_specs=[tok(d), tok(d), col(2), col(1)],
        out_shape=[jax.ShapeDtypeStruct((t, d), F32), jax.ShapeDtypeStruct((t, d), F32),
                   jax.ShapeDtypeStruct((2, t), F32), jax.ShapeDtypeStruct((1, t), jnp.int32)],
        scratch_shapes=[pltpu.VMEM((SSM_COLS, tm, LANES), F32)],
        compiler_params=_params(("parallel",), VMEM_LIMIT),
        name="post_mix",
    )(x, attn, yc, w_glu, ssm_g, w_out, g1, n2g, sc2, sh2, w_router_t, router_bias)


def _moe_kernel(rb_ref, bk_ref, lo_ref, hi_ref, first_ref, fresh_ref, meta_ref, offs_ref,
                x_ref, cw_ref, wg_lo_ref, wg_hi_ref, wu_lo_ref, wu_hi_ref, wd_lo_ref, wd_hi_ref,
                o_ref, wg_s, wu_s, wd_s):
    i = pl.program_id(0)

    @pl.when(i < meta_ref[0])
    def _():
        @pl.when(fresh_ref[i] == 1)
        def _():
            for k, (g_ref, u_ref, d_ref) in enumerate(((wg_lo_ref, wu_lo_ref, wd_lo_ref),
                                                        (wg_hi_ref, wu_hi_ref, wd_hi_ref))):
                wg_s[k] = g_ref[...].astype(BF16)
                wu_s[k] = u_ref[...].astype(BF16)
                wd_s[k] = d_ref[...].astype(BF16)

        bucket = bk_ref[i]
        rows = rb_ref[i] * MOE_ROWS + lax.broadcasted_iota(jnp.int32, (MOE_ROWS, 1), 0)
        in_bucket = (rows >= offs_ref[bucket]) & (rows < offs_ref[bucket + 1])
        cw = jnp.where(in_bucket, cw_ref[...], 0.0)
        x = x_ref[...].astype(BF16)
        y = None
        for k in range(2):
            gate = jnp.dot(x, wg_s[k], preferred_element_type=F32)
            up = jnp.dot(x, wu_s[k], preferred_element_type=F32)
            act = (gate * jax.nn.sigmoid(gate) * up * cw[:, k:k + 1]).astype(BF16)
            yk = jnp.dot(act, wd_s[k], preferred_element_type=F32)
            y = yk if y is None else y + yk

        @pl.when(first_ref[i] == 1)
        def _():
            o_ref[...] = y

        @pl.when(first_ref[i] == 0)
        def _():
            o_ref[...] += y


def _moe_items(bucket, t):
    nbk = N_EXPERT_GROUPS * PAIRS_PER_GROUP
    order = jnp.argsort(bucket, stable=True).astype(jnp.int32)
    counts = jnp.sum((bucket[None, :] == jnp.arange(nbk, dtype=jnp.int32)[:, None]).astype(jnp.int32), axis=1)
    offs = jnp.concatenate([jnp.zeros((1,), jnp.int32), jnp.cumsum(counts).astype(jnp.int32)])
    nblk = t // MOE_ROWS
    n_items_max = nblk + nbk - 1
    lo = jnp.arange(nblk, dtype=jnp.int32)[:, None] * MOE_ROWS
    valid = ((offs[None, :-1] < lo + MOE_ROWS) & (offs[None, 1:] > lo) & (offs[None, 1:] > offs[None, :-1])).reshape(-1)
    flat = jnp.arange(nblk * nbk, dtype=jnp.int32)
    ranked = jnp.sort(jnp.where(valid, flat, nblk * nbk))[:n_items_max]
    n_items = jnp.sum(valid.astype(jnp.int32))
    ranked = ranked[jnp.minimum(jnp.arange(n_items_max, dtype=jnp.int32), n_items - 1)]
    item_rb, item_bk = ranked // nbk, ranked % nbk
    pair_lo = jnp.asarray(np.array([0, 0, 0, 1, 1, 2], np.int32))
    pair_hi = jnp.asarray(np.array([1, 2, 3, 2, 3, 3], np.int32))
    base = (item_bk // PAIRS_PER_GROUP) * EXPERTS_PER_GROUP
    item_lo, item_hi = base + pair_lo[item_bk % PAIRS_PER_GROUP], base + pair_hi[item_bk % PAIRS_PER_GROUP]
    one = jnp.ones((1,), jnp.int32)
    first = jnp.concatenate([one, (item_rb[1:] != item_rb[:-1]).astype(jnp.int32)])
    fresh = jnp.concatenate([one, (item_bk[1:] != item_bk[:-1]).astype(jnp.int32)])
    return order, (item_rb, item_bk, item_lo, item_hi, first, fresh, n_items.reshape(1), offs), n_items_max


def _moe(xs, cws, w_gate, w_up, w_down, layer, tables, n_items_max):
    t, d = xs.shape
    ff = w_gate.shape[3]
    lo_map = lambda i, rb, bk, lo, hi, fi, fr, me, of: (layer, lo[i], 0, 0)
    hi_map = lambda i, rb, bk, lo, hi, fi, fr, me, of: (layer, hi[i], 0, 0)
    row_map = lambda i, rb, bk, lo, hi, fi, fr, me, of: (rb[i], 0)
    up_blk = lambda m: pl.BlockSpec((None, None, d, ff), m)
    down_blk = lambda m: pl.BlockSpec((None, None, ff, d), m)
    grid_spec = pltpu.PrefetchScalarGridSpec(
        num_scalar_prefetch=len(tables),
        grid=(n_items_max,),
        in_specs=[pl.BlockSpec((MOE_ROWS, d), row_map), pl.BlockSpec((MOE_ROWS, 2), row_map),
                  up_blk(lo_map), up_blk(hi_map), up_blk(lo_map), up_blk(hi_map), down_blk(lo_map), down_blk(hi_map)],
        out_specs=pl.BlockSpec((MOE_ROWS, d), row_map),
        scratch_shapes=[pltpu.VMEM((2, d, ff), BF16), pltpu.VMEM((2, d, ff), BF16), pltpu.VMEM((2, ff, d), BF16)],
    )
    return pl.pallas_call(
        _moe_kernel,
        grid_spec=grid_spec,
        out_shape=jax.ShapeDtypeStruct((t, d), F32),
        compiler_params=_params(("arbitrary",), VMEM_LIMIT),
        name="moe_grouped",
    )(*tables, xs, cws, w_gate, w_gate, w_up, w_up, w_down, w_down)


def _take_rows(a, idx):
    return a.at[idx].get(mode="promise_in_bounds", unique_indices=True)


def _final_kernel(x_ref, y_ref, g_ref, o_ref):
    o_ref[...] = x_ref[...] + g_ref[...] * y_ref[...]


def _final(x1, y, g2, seq):
    t, d = x1.shape
    tm = min(TOKEN_TILE, seq)
    per_b = seq // tm
    tok = lambda w: pl.BlockSpec((tm, w), lambda i: (i, 0))
    return pl.pallas_call(
        _final_kernel,
        grid=(t // tm,),
        in_specs=[tok(d), tok(d), pl.BlockSpec((None, 1, d), lambda i: (i // per_b, 0, 0))],
        out_specs=tok(d),
        out_shape=jax.ShapeDtypeStruct((t, d), F32),
        compiler_params=_params(("parallel",)),
        name="final_residual",
    )(x1, y, g2)


def kernel(x, c, positions, ada_w, ada_b, norm1_g, w_in, q_norm_g, k_norm_g, attn_sink, lam_re, lam_im, ssm_b_re, ssm_b_im, ssm_c_re, ssm_c_im, ssm_d, ssm_log_dt, w_glu, attn_out_g, ssm_out_g, w_out, norm2_g, w_router, router_bias, w_exp_gate, w_exp_up, w_exp_down):
    batch, seq, d = x.shape
    depth = ada_w.shape[0]
    t = batch * seq
    assert seq % ATTN_BLOCK == 0 and seq % SSM_CHUNK == 0 and t % MOE_ROWS == 0

    mod = _adaln_mod(c, ada_w, ada_b).reshape(depth, 6, batch, 1, d)
    cos, sin = _rope_tables(positions)
    head_sum, rot = _rope_constants()
    bias = _attn_bias()
    w_router_t = w_router.T
    router_bias_col = router_bias.reshape(N_EXPERTS, 1)

    xf = x.reshape(t, d)
    res = None
    for l in range(depth):
        sh1, sc1, g1, sh2, sc2, g2 = (mod[l, j] for j in range(6))
        qg = (jnp.tile(q_norm_g[l], N_Q_HEADS) * HEAD_DIM ** -0.5).reshape(1, ATTN_WIDTH)
        kg = jnp.tile(k_norm_g[l], N_KV_HEADS).reshape(1, KV_WIDTH)
        outs = _inproj(xf, res, sc1, sh1, norm1_g[l].reshape(1, d), w_in[l].astype(BF16), qg, kg, head_sum, rot,
                       cos, sin, seq)
        if res is None:
            q, kx, vx, uc = outs
        else:
            q, kx, vx, uc, xf = outs
        attn = _attention(q, kx, vx, attn_sink[l], attn_out_g[l].reshape(1, ATTN_WIDTH), bias, batch, seq)
        tcol, wcol, vcol, la, lb = _s5_prep(lam_re[l], lam_im[l], ssm_b_re[l], ssm_b_im[l], ssm_c_re[l], ssm_c_im[l],
                                            ssm_d[l], ssm_log_dt[l])
        yc = _s5_scan(uc, tcol, wcol, vcol, la, lb, seq // SSM_CHUNK, batch)
        x1, h2, cw, gid = _post(xf, attn, yc, w_glu[l].astype(BF16), ssm_out_g[l].reshape(1, SSM_WIDTH),
                                 w_out[l].astype(BF16), g1, norm2_g[l].reshape(1, d), sc2, sh2, w_router_t,
                                 router_bias_col, seq)
        order, tables, n_items_max = _moe_items(gid.reshape(t), t)
        y_sorted = _moe(_take_rows(h2, order), _take_rows(cw.T, order), w_exp_gate, w_exp_up, w_exp_down, l,
                        tables, n_items_max)
        y = _take_rows(y_sorted, jnp.argsort(order).astype(jnp.int32))
        xf, res = x1, (y, g2)
    y, g2 = res
    return _final(xf, y, g2, seq).reshape(batch, seq, d)
```

```python
import functools

import numpy as np
import jax
import jax.numpy as jnp
from jax import lax
from jax.experimental import pallas as pl
from jax.experimental.pallas import tpu as pltpu

F32 = jnp.float32
BF16 = jnp.bfloat16

HEAD_DIM = 64
N_Q_HEADS = 8
N_KV_HEADS = 2
Q_PER_KV = N_Q_HEADS // N_KV_HEADS
ATTN_WIDTH = N_Q_HEADS * HEAD_DIM
KV_WIDTH = N_KV_HEADS * HEAD_DIM
ATTN_BLOCK = 128
ROPE_THETA = 10000.0
LANES = 128
SSM_GROUP_CH = 16
SSM_GROUPS = 32
SSM_WIDTH = SSM_GROUPS * SSM_GROUP_CH
SSM_STATE = 64
SSM_CHUNK = 16
SSM_COLS = SSM_WIDTH // LANES
COL_GROUPS = LANES // SSM_GROUP_CH
COL_STATE = COL_GROUPS * SSM_STATE
CHUNK_LANES = SSM_CHUNK * LANES
SSM_NSPLIT = 4
N_EXPERTS = 16
N_EXPERT_GROUPS = 4
EXPERTS_PER_GROUP = N_EXPERTS // N_EXPERT_GROUPS
PAIRS_PER_GROUP = EXPERTS_PER_GROUP * (EXPERTS_PER_GROUP - 1) // 2
PAIR_SLOTS = ((0, 1), (0, 2), (0, 3), (1, 3), (1, 2), (3, 2))
EPS = 1e-6
MASK_BIAS = -1e30

TOKEN_TILE = 512
MOE_ROWS = 256
VMEM_LIMIT = 48 * 1024 * 1024


def _params(sem, vmem=None):
    return pltpu.CompilerParams(dimension_semantics=sem, vmem_limit_bytes=vmem)


def _rms(x, g):
    return x * lax.rsqrt(jnp.mean(x * x, axis=-1, keepdims=True) + EPS) * g


def _mod_kernel(c_ref, w_ref, b_ref, o_ref):
    c = c_ref[...]
    s = c * jax.nn.sigmoid(c)
    o_ref[...] = jnp.dot(s.astype(BF16), w_ref[...].astype(BF16), preferred_element_type=F32) + b_ref[...]


def _adaln_mod(c, ada_w, ada_b):
    depth, d, d6 = ada_w.shape
    nb = c.shape[0]
    n6 = d6 // d
    return pl.pallas_call(
        _mod_kernel,
        grid=(depth, n6),
        in_specs=[pl.BlockSpec((nb, d), lambda l, j: (0, 0)),
                  pl.BlockSpec((None, d, d), lambda l, j: (l, 0, j)),
                  pl.BlockSpec((None, None, 1, d), lambda l, j: (l, j, 0, 0))],
        out_specs=pl.BlockSpec((None, None, nb, d), lambda l, j: (l, j, 0, 0)),
        out_shape=jax.ShapeDtypeStruct((depth, n6, nb, d), F32),
        compiler_params=_params(("arbitrary", "arbitrary"), VMEM_LIMIT),
        name="adaln_mod",
    )(c, ada_w, ada_b.reshape(depth, n6, 1, d))


def _rope_kernel(pos_ref, freq_ref, cos_ref, sin_ref):
    ang = pos_ref[...].astype(F32) * freq_ref[...]
    cos_ref[...] = jnp.cos(ang)
    sin_ref[...] = jnp.sin(ang)


def _rope_tables(positions):
    half = HEAD_DIM // 2
    t = positions.size
    per_row = LANES // half
    rows = t // per_row
    pos_rep = jnp.repeat(positions.reshape(rows, per_row), half, axis=1)
    freq = (ROPE_THETA ** (-np.arange(half, dtype=np.float64) / half)).astype(np.float32)
    freq_row = jnp.asarray(np.tile(freq, per_row)[None, :])
    blk = min(rows, 512)
    cos, sin = pl.pallas_call(
        _rope_kernel,
        grid=(rows // blk,),
        in_specs=[pl.BlockSpec((blk, LANES), lambda i: (i, 0)),
                  pl.BlockSpec((1, LANES), lambda i: (0, 0))],
        out_specs=[pl.BlockSpec((blk, LANES), lambda i: (i, 0))] * 2,
        out_shape=[jax.ShapeDtypeStruct((rows, LANES), F32)] * 2,
        compiler_params=_params(("arbitrary",)),
        name="rope_tables",
    )(pos_rep, freq_row)
    widen = lambda a: jnp.tile(a.reshape(t, half), (1, per_row))
    return widen(cos), widen(sin)


def _rope_constants():
    lane = np.arange(ATTN_WIDTH)
    head_sum = (lane[:, None] // HEAD_DIM == lane[None, :] // HEAD_DIM).astype(np.float32)
    half = HEAD_DIM // 2
    rot = np.zeros((ATTN_WIDTH, ATTN_WIDTH), np.float32)
    for d in range(ATTN_WIDTH):
        if d % HEAD_DIM < half:
            rot[d + half, d] = -1.0
        else:
            rot[d - half, d] = 1.0
    return jnp.asarray(head_sum, BF16), jnp.asarray(rot, BF16)


def _inproj_kernel(*refs, has_res):
    if has_res:
        (x_ref, y_ref, g2_ref, sc_ref, sh_ref, n1_ref, w_ref, qg_ref, kg_ref, hs_ref, rot_ref,
         cos_ref, sin_ref, q_ref, k_ref, v_ref, uc_ref, xo_ref, u_scr) = refs
        x = x_ref[...] + g2_ref[...] * y_ref[...]
        xo_ref[...] = x
    else:
        (x_ref, sc_ref, sh_ref, n1_ref, w_ref, qg_ref, kg_ref, hs_ref, rot_ref,
         cos_ref, sin_ref, q_ref, k_ref, v_ref, uc_ref, u_scr) = refs
        x = x_ref[...]
    h = _rms(x, n1_ref[...]) * (1.0 + sc_ref[...]) + sh_ref[...]
    proj = jnp.dot(h.astype(BF16), w_ref[...], preferred_element_type=F32)
    q = proj[:, :ATTN_WIDTH]
    k = proj[:, ATTN_WIDTH:ATTN_WIDTH + KV_WIDTH]
    v = proj[:, ATTN_WIDTH + KV_WIDTH:ATTN_WIDTH + 2 * KV_WIDTH]
    cos = cos_ref[...]
    sin = sin_ref[...]
    reps = ATTN_WIDTH // LANES
    cos_q = jnp.concatenate([cos] * reps, axis=1)
    sin_q = jnp.concatenate([sin] * reps, axis=1)

    def head_norm_rope(t, head_sum, rot, gain, c, s):
        ssq = jnp.dot((t * t).astype(BF16), head_sum, preferred_element_type=F32)
        tn = (t * lax.rsqrt(ssq * (1.0 / HEAD_DIM) + EPS) * gain).astype(BF16)
        tr = jnp.dot(tn, rot, preferred_element_type=F32)
        return tn.astype(F32) * c + tr * s

    qo = head_norm_rope(q, hs_ref[...], rot_ref[...], qg_ref[...], cos_q, sin_q)
    ko = head_norm_rope(k, hs_ref[:KV_WIDTH, :KV_WIDTH], rot_ref[:KV_WIDTH, :KV_WIDTH], kg_ref[...], cos, sin)
    q_ref[...] = qo.astype(BF16)
    k_ref[...] = jnp.concatenate([ko, pltpu.roll(ko, HEAD_DIM, axis=1)], axis=1).astype(BF16)
    v_ref[...] = jnp.concatenate([v, pltpu.roll(v, HEAD_DIM, axis=1)], axis=1).astype(BF16)
    u0 = ATTN_WIDTH + 2 * KV_WIDTH
    nchunk = u_scr.shape[1] // SSM_CHUNK
    for j in range(SSM_COLS):
        u_scr[j] = proj[:, u0 + j * LANES:u0 + (j + 1) * LANES]
    for s in range(SSM_CHUNK):
        for j in range(SSM_COLS):
            lanes = slice(s * SSM_WIDTH + j * LANES, s * SSM_WIDTH + (j + 1) * LANES)
            uc_ref[:, lanes] = u_scr[j, pl.ds(s, nchunk, stride=SSM_CHUNK), :].astype(BF16)


def _inproj(x, res, sc1, sh1, n1g, w_in, qg, kg, head_sum, rot, cos, sin, seq):
    t, d = x.shape
    tm = min(TOKEN_TILE, seq)
    per_b = seq // tm
    in_width = w_in.shape[1]
    tok = lambda w: pl.BlockSpec((tm, w), lambda i: (i, 0))
    const = lambda a: pl.BlockSpec(a.shape, lambda i: (0,) * a.ndim)
    per_batch = pl.BlockSpec((None, 1, d), lambda i: (i // per_b, 0, 0))
    chunked = pl.BlockSpec((tm // SSM_CHUNK, SSM_CHUNK * SSM_WIDTH), lambda i: (i, 0))
    ins, specs = [x], [tok(d)]
    if res is not None:
        y_prev, g2_prev = res
        ins += [y_prev, g2_prev]
        specs += [tok(d), per_batch]
    ins += [sc1, sh1, n1g, w_in, qg, kg, head_sum, rot, cos, sin]
    specs += [per_batch, per_batch, const(n1g), const(w_in), const(qg), const(kg), const(head_sum), const(rot),
              tok(LANES), tok(LANES)]
    out_shape = [jax.ShapeDtypeStruct((t, ATTN_WIDTH), BF16), jax.ShapeDtypeStruct((t, 2 * KV_WIDTH), BF16),
                 jax.ShapeDtypeStruct((t, 2 * KV_WIDTH), BF16),
                 jax.ShapeDtypeStruct((t // SSM_CHUNK, SSM_CHUNK * SSM_WIDTH), BF16)]
    out_specs = [tok(ATTN_WIDTH), tok(2 * KV_WIDTH), tok(2 * KV_WIDTH), chunked]
    if res is not None:
        out_shape.append(jax.ShapeDtypeStruct((t, d), F32))
        out_specs.append(tok(d))
    assert in_width == ATTN_WIDTH + 2 * KV_WIDTH + SSM_WIDTH
    return pl.pallas_call(
        functools.partial(_inproj_kernel, has_res=res is not None),
        grid=(t // tm,),
        in_specs=specs,
        out_specs=out_specs,
        out_shape=out_shape,
        scratch_shapes=[pltpu.VMEM((SSM_COLS, tm, LANES), F32)],
        compiler_params=_params(("parallel",), VMEM_LIMIT),
        name="inproj",
    )(*ins)


def _attn_kernel(sink_ref, q_ref, kc_ref, kp_ref, vc_ref, vp_ref, bias_ref, g_ref, o_ref):
    kk = jnp.concatenate([kp_ref[...], kc_ref[...]], axis=0)
    vv = jnp.concatenate([vp_ref[...], vc_ref[...]], axis=0)
    lane = lax.broadcasted_iota(jnp.int32, (2 * ATTN_BLOCK, KV_WIDTH), 1)
    low = lane < HEAD_DIM
    zero = jnp.zeros((2 * ATTN_BLOCK, KV_WIDTH), BF16)

    def variants(a):
        nat, swp = a[:, :KV_WIDTH], a[:, KV_WIDTH:]
        return {(0, 0): jnp.where(low, nat, zero), (0, 1): jnp.where(low, zero, swp),
                (1, 0): jnp.where(low, swp, zero), (1, 1): jnp.where(low, zero, nat)}

    kvar, vvar = variants(kk), variants(vv)
    bias = bias_ref[...]
    outs = []
    for pair in range(N_Q_HEADS // 2):
        qp = q_ref[:, pair * LANES:(pair + 1) * LANES]
        acc = jnp.zeros((ATTN_BLOCK, LANES), F32)
        for half in range(2):
            head = 2 * pair + half
            kv = head // Q_PER_KV
            s = lax.dot_general(qp, kvar[(kv, half)], (((1,), (1,)), ((), ())), preferred_element_type=F32) + bias
            sink = sink_ref[head]
            m = jnp.maximum(jnp.max(s, axis=-1, keepdims=True), sink)
            p = jnp.exp(s - m)
            denom = jnp.sum(p, axis=-1, keepdims=True) + jnp.exp(sink - m)
            o = jnp.dot(p.astype(BF16), vvar[(kv, half)], preferred_element_type=F32)
            acc = acc + o * (1.0 / denom)
        outs.append(acc)
    a = jnp.concatenate(outs, axis=1)
    o_ref[...] = _rms(a, g_ref[...]).astype(BF16)


def _attn_bias():
    qi = np.arange(ATTN_BLOCK)[:, None]
    sj = np.arange(2 * ATTN_BLOCK)[None, :]
    diff = qi + ATTN_BLOCK - sj
    band = (diff >= 0) & (diff < ATTN_BLOCK)
    first = band & (sj >= ATTN_BLOCK)
    return jnp.asarray(np.where(np.stack([first, band]), 0.0, MASK_BIAS).astype(np.float32))


def _attention(q, kx, vx, sink, out_g, bias, batch, seq):
    t = q.shape[0]
    nb = seq // ATTN_BLOCK
    cur = lambda w: pl.BlockSpec((ATTN_BLOCK, w), lambda b, n, s: (b * nb + n, 0))
    prev = lambda w: pl.BlockSpec((ATTN_BLOCK, w), lambda b, n, s: (b * nb + jnp.maximum(n - 1, 0), 0))
    grid_spec = pltpu.PrefetchScalarGridSpec(
        num_scalar_prefetch=1,
        grid=(batch, nb),
        in_specs=[cur(ATTN_WIDTH), cur(2 * KV_WIDTH), prev(2 * KV_WIDTH), cur(2 * KV_WIDTH), prev(2 * KV_WIDTH),
                  pl.BlockSpec((None, ATTN_BLOCK, 2 * ATTN_BLOCK), lambda b, n, s: (jnp.minimum(n, 1), 0, 0)),
                  pl.BlockSpec((1, ATTN_WIDTH), lambda b, n, s: (0, 0))],
        out_specs=cur(ATTN_WIDTH),
    )
    return pl.pallas_call(
        _attn_kernel,
        grid_spec=grid_spec,
        out_shape=jax.ShapeDtypeStruct((t, ATTN_WIDTH), BF16),
        compiler_params=_params(("parallel", "arbitrary")),
        name="swa_attention",
    )(sink, q, kx, kx, vx, vx, bias, out_g)


def _s5_prep_kernel(lr_re_ref, lr_im_ref, lc_re_ref, lc_im_ref, ldt_ref, bt_re_ref, bt_im_ref, ct_re_ref, ct_im_ref,
                    d_ref, lcol_re_ref, lcol_im_ref, ldtcol_ref, exp_ref, exph_ref, expt_ref, expw_ref,
                    t_ref, w_ref, v_ref, la_ref, lb_ref):
    hi = lax.Precision.HIGHEST
    nl = SSM_CHUNK
    low = lax.broadcasted_iota(jnp.int32, (1, 2 * SSM_STATE), 1) < SSM_STATE
    row_low = lax.broadcasted_iota(jnp.int32, (2 * SSM_STATE, 1), 0) < SSM_STATE
    jcol = lax.broadcasted_iota(jnp.int32, (nl, 1), 0).astype(F32)
    jrow = lax.broadcasted_iota(jnp.int32, (1, nl), 1).astype(F32)
    kt_lane = lax.broadcasted_iota(jnp.int32, (SSM_GROUP_CH, nl * SSM_GROUP_CH), 1)
    kt_row = lax.broadcasted_iota(jnp.int32, (SSM_GROUP_CH, nl * SSM_GROUP_CH), 0)

    w_all, v_all, kt_all = [], [], []
    for gm in range(COL_GROUPS):
        dt = jnp.exp(ldt_ref[gm])
        lam_re, lam_im = lr_re_ref[gm], lr_im_ref[gm]
        a_r, th_r = lam_re * dt, lam_im * dt
        a_c, th_c = lc_re_ref[gm] * dt, lc_im_ref[gm] * dt

        er = jnp.exp(jcol * a_r)
        pw_re, pw_im = er * jnp.cos(jcol * th_r), er * jnp.sin(jcol * th_r)

        e1 = jnp.exp(a_r)
        nr, ni = e1 * jnp.cos(th_r) - 1.0, e1 * jnp.sin(th_r)
        den = lam_re * lam_re + lam_im * lam_im
        c_re, c_im = (nr * lam_re + ni * lam_im) / den, (ni * lam_re - nr * lam_im) / den
        bt_re, bt_im = bt_re_ref[gm], bt_im_ref[gm]
        bb_re, bb_im = c_re * bt_re - c_im * bt_im, c_re * bt_im + c_im * bt_re

        w_rows = []
        for s in range(nl):
            j = nl - 1 - s
            pr, pi = pw_re[j:j + 1, :], pw_im[j:j + 1, :]
            w_rows.append(jnp.where(low, pr * bb_re - pi * bb_im, pr * bb_im + pi * bb_re))
        w_all.append(w_rows)

        ec = jnp.exp(a_c * jrow)
        pc = jnp.dot(ec * jnp.cos(th_c * jrow), exp_ref[...], precision=hi, preferred_element_type=F32)
        ps = jnp.dot(ec * jnp.sin(th_c * jrow), exp_ref[...], precision=hi, preferred_element_type=F32)
        ct_re = jnp.dot(ct_re_ref[gm], exph_ref[...], precision=hi, preferred_element_type=F32)
        ct_im = jnp.dot(ct_im_ref[gm], exph_ref[...], precision=hi, preferred_element_type=F32)
        a_re, a_im = ct_re * pc - ct_im * ps, ct_re * ps + ct_im * pc
        a_cat = jnp.where(row_low, a_re, -a_im)
        e1c = jnp.exp(a_c)
        l1_re, l1_im = e1c * jnp.cos(th_c), e1c * jnp.sin(th_c)
        v_re, v_im = a_re * l1_re - a_im * l1_im, a_re * l1_im + a_im * l1_re
        v_all.append(jnp.where(row_low, v_re, -v_im))

        kt = jnp.dot(jnp.where(low, bb_re, bb_im), a_cat, precision=hi, preferred_element_type=F32)
        kt_all.append(kt + jnp.where(kt_lane == kt_row, d_ref[gm], 0.0))

    def expand(stacked, expander, row_group, lane_group):
        wide = jnp.dot(stacked.astype(BF16), expander, preferred_element_type=F32)
        r = lax.broadcasted_iota(jnp.int32, wide.shape, 0)
        c = lax.broadcasted_iota(jnp.int32, wide.shape, 1)
        return jnp.where(row_group(r) == lane_group(c), wide, 0.0).astype(BF16)

    chan_group = lambda i: (i >> 4) & (COL_GROUPS - 1)
    state_group = lambda i: (i >> 6) & (COL_GROUPS - 1)

    bd = expand(jnp.concatenate(kt_all, axis=0), expt_ref[...], chan_group, chan_group)
    t_ref[0:LANES, :] = bd
    for s in range(1, nl):
        t_ref[s * LANES:(s + 1) * LANES, :] = jnp.concatenate(
            [jnp.zeros((LANES, s * LANES), BF16), bd[:, :CHUNK_LANES - s * LANES]], axis=1)

    w_stack = jnp.concatenate([w_all[gm][s] for s in range(nl) for gm in range(COL_GROUPS)], axis=0)
    w_ref[...] = expand(w_stack, expw_ref[...], chan_group, state_group)

    v_stack = jnp.concatenate([v_all[gm][half * SSM_STATE:(half + 1) * SSM_STATE, :]
                               for half in range(2) for gm in range(COL_GROUPS)], axis=0)
    v_ref[...] = expand(v_stack, expt_ref[...], state_group, chan_group)

    dtc = jnp.exp(ldtcol_ref[...])
    e16 = jnp.exp(nl * lcol_re_ref[...] * dtc)
    ang = nl * lcol_im_ref[...] * dtc
    la_ref[...] = e16 * jnp.cos(ang)
    lb_ref[...] = e16 * jnp.sin(ang)


def _s5_prep(lam_re, lam_im, b_re, b_im, c_re, c_im, d_skip, log_dt):
    g, p, h, nl = SSM_GROUPS, SSM_STATE, SSM_GROUP_CH, SSM_CHUNK
    cg = COL_GROUPS
    nc = lam_re.shape[0] * SSM_COLS
    col = lambda a: a.reshape((nc, cg) + a.shape[2:])
    dup_row = lambda a: col(jnp.tile(a, (1, 1, 2))[:, :, None, :])
    dup_col = lambda a: col(jnp.tile(a, (1, 1, 2))[:, :, :, None])
    bt = lambda a: col(jnp.tile(jnp.swapaxes(a, 2, 3), (1, 1, 1, 2)))
    ct = lambda a: col(jnp.tile(jnp.swapaxes(a, 2, 3), (1, 1, 2, 1)))
    d_pad = col(jnp.pad(d_skip.reshape(-1, g, 1, h), ((0, 0), (0, 0), (0, 0), (0, nl * h - h))))
    wide = lambda a: a.reshape(nc, 1, cg * p)
    expand = jnp.asarray(np.repeat(np.eye(nl, dtype=np.float32), h, axis=1))
    expand_h = jnp.asarray(np.tile(np.eye(h, dtype=np.float32), (1, nl)))
    exp_t = np.zeros((nl, h, nl, cg, h), np.float32)
    exp_w = np.zeros((2, p, 2, cg, p), np.float32)
    for gm in range(cg):
        exp_t[:, :, :, gm, :] = np.eye(nl * h, dtype=np.float32).reshape(nl, h, nl, h)
        exp_w[:, :, :, gm, :] = np.eye(2 * p, dtype=np.float32).reshape(2, p, 2, p)
    exp_t = jnp.asarray(exp_t.reshape(nl * h, CHUNK_LANES), BF16)
    exp_w = jnp.asarray(exp_w.reshape(2 * p, 2 * COL_STATE), BF16)
    blk = lambda *s: pl.BlockSpec((None,) + s, lambda i: (i,) + (0,) * len(s))
    const = lambda a: pl.BlockSpec(a.shape, lambda i: (0,) * a.ndim)
    lw = nl * h
    return pl.pallas_call(
        _s5_prep_kernel,
        grid=(nc,),
        in_specs=[blk(cg, 1, 2 * p), blk(cg, 1, 2 * p), blk(cg, 2 * p, 1), blk(cg, 2 * p, 1), blk(cg, 1, 1),
                  blk(cg, h, 2 * p), blk(cg, h, 2 * p), blk(cg, 2 * p, h), blk(cg, 2 * p, h), blk(cg, 1, lw),
                  blk(1, cg * p), blk(1, cg * p), blk(1, cg * p), const(expand), const(expand_h), const(exp_t),
                  const(exp_w)],
        out_specs=[blk(CHUNK_LANES, CHUNK_LANES), blk(CHUNK_LANES, 2 * COL_STATE), blk(2 * COL_STATE, CHUNK_LANES),
                   blk(1, COL_STATE), blk(1, COL_STATE)],
        out_shape=[jax.ShapeDtypeStruct((nc, CHUNK_LANES, CHUNK_LANES), BF16),
                   jax.ShapeDtypeStruct((nc, CHUNK_LANES, 2 * COL_STATE), BF16),
                   jax.ShapeDtypeStruct((nc, 2 * COL_STATE, CHUNK_LANES), BF16),
                   jax.ShapeDtypeStruct((nc, 1, COL_STATE), F32), jax.ShapeDtypeStruct((nc, 1, COL_STATE), F32)],
        compiler_params=_params(("parallel",), VMEM_LIMIT),
        name="s5_prep",
    )(dup_row(lam_re), dup_row(lam_im), dup_col(lam_re), dup_col(lam_im), col(log_dt[:, :, None, None]),
      bt(b_re), bt(b_im), ct(c_re), ct(c_im), d_pad, wide(lam_re), wide(lam_im),
      wide(jnp.repeat(log_dt, p, axis=1)), expand, expand_h, exp_t, exp_w)


def _s5_kernel(*refs, nchunks, nb):
    uc_refs = refs[:SSM_CHUNK]
    t_ref, w_ref, v_ref, la_ref, lb_ref, o_ref, ucat_ref, s_ref, xp_ref = refs[SSM_CHUNK:]

    @pl.when(pl.program_id(1) == 0)
    def _():
        for s in range(SSM_CHUNK):
            ucat_ref[:, s * LANES:(s + 1) * LANES] = uc_refs[s][...]
        s_in = jnp.dot(ucat_ref[...], w_ref[...], preferred_element_type=F32)
        nblk = COL_STATE // LANES
        for b in range(2 * nblk):
            s_ref[b] = s_in[:, b * LANES:(b + 1) * LANES]
        lr = [jnp.broadcast_to(la_ref[:, b * LANES:(b + 1) * LANES], (nb, LANES)) for b in range(nblk)]
        li = [jnp.broadcast_to(lb_ref[:, b * LANES:(b + 1) * LANES], (nb, LANES)) for b in range(nblk)]

        def step(c, carry):
            rows = pl.ds(c, nb, stride=nchunks)
            out = []
            for b in range(nblk):
                re, im = carry[2 * b], carry[2 * b + 1]
                xp_ref[b, rows, :] = re
                xp_ref[nblk + b, rows, :] = im
                out.append(lr[b] * re - li[b] * im + s_ref[b, rows, :])
                out.append(lr[b] * im + li[b] * re + s_ref[nblk + b, rows, :])
            return tuple(out)

        zero = jnp.zeros((nb, LANES), F32)
        lax.fori_loop(0, nchunks, step, (zero,) * (2 * nblk), unroll=4)

    xp = jnp.concatenate([xp_ref[b] for b in range(2 * COL_STATE // LANES)], axis=1).astype(BF16)
    inter = jnp.dot(xp, v_ref[...], preferred_element_type=F32)
    for kk in range(SSM_NSPLIT):
        @pl.when(pl.program_id(1) == kk)
        def _():
            live = (kk + 1) * (CHUNK_LANES // SSM_NSPLIT)
            intra = jnp.dot(ucat_ref[:, :live], t_ref[:live, :], preferred_element_type=F32)
            o_ref[...] = (intra + inter).astype(BF16)


def _s5_scan(uc, mats, layer, nchunks, nb):
    rows = uc.shape[0]
    c0 = layer * SSM_COLS
    split = CHUNK_LANES // SSM_NSPLIT
    u_spec = lambda s: pl.BlockSpec((rows, LANES), lambda j, k: (0, SSM_COLS * s + j))
    return pl.pallas_call(
        functools.partial(_s5_kernel, nchunks=nchunks, nb=nb),
        grid=(SSM_COLS, SSM_NSPLIT),
        in_specs=[u_spec(s) for s in range(SSM_CHUNK)] + [
            pl.BlockSpec((None, CHUNK_LANES, split), lambda j, k: (c0 + j, 0, k)),
            pl.BlockSpec((None, CHUNK_LANES, 2 * COL_STATE), lambda j, k: (c0 + j, 0, 0)),
            pl.BlockSpec((None, 2 * COL_STATE, split), lambda j, k: (c0 + j, 0, k)),
            pl.BlockSpec((None, 1, COL_STATE), lambda j, k: (c0 + j, 0, 0)),
            pl.BlockSpec((None, 1, COL_STATE), lambda j, k: (c0 + j, 0, 0))],
        out_specs=pl.BlockSpec((None, rows, split), lambda j, k: (j, 0, k)),
        out_shape=jax.ShapeDtypeStruct((SSM_COLS, rows, CHUNK_LANES), BF16),
        scratch_shapes=[pltpu.VMEM((rows, CHUNK_LANES), BF16),
                        pltpu.VMEM((2 * COL_STATE // LANES, rows, LANES), F32),
                        pltpu.VMEM((2 * COL_STATE // LANES, rows, LANES), F32)],
        compiler_params=_params(("parallel", "arbitrary"), VMEM_LIMIT),
        name="s5_scan",
    )(*([uc] * SSM_CHUNK), *mats)


def _route(logits, bias):
    m = jnp.max(logits, axis=0, keepdims=True)
    e = jnp.exp(logits - m)
    probs = e / jnp.sum(e, axis=0, keepdims=True)
    sel = probs + bias
    row = lambda a, i: a[i:i + 1, :]
    best_score, best = None, None
    for grp in range(N_EXPERT_GROUPS):
        a, b, c, d = (row(sel, EXPERTS_PER_GROUP * grp + i) for i in range(EXPERTS_PER_GROUP))
        hab, lab, hcd, lcd = jnp.maximum(a, b), jnp.minimum(a, b), jnp.maximum(c, d), jnp.minimum(c, d)
        top1 = jnp.maximum(hab, hcd)
        top2 = jnp.maximum(jnp.maximum(lab, lcd), jnp.minimum(hab, hcd))
        score = top1 + top2
        if grp == 0:
            best_score, best = score, jnp.zeros(score.shape, jnp.int32)
        else:
            better = score > best_score
            best = jnp.where(better, grp, best)
            best_score = jnp.where(better, score, best_score)

    def pick(a, i):
        out = row(a, i)
        for grp in range(1, N_EXPERT_GROUPS):
            out = jnp.where(best == grp, row(a, EXPERTS_PER_GROUP * grp + i), out)
        return out

    s_in = [pick(sel, i) for i in range(EXPERTS_PER_GROUP)]
    p_in = [pick(probs, i) for i in range(EXPERTS_PER_GROUP)]
    neg = jnp.full(s_in[0].shape, -jnp.inf, F32)

    def argmax_first(vals):
        idx, val = jnp.zeros(vals[0].shape, jnp.int32), vals[0]
        for i in range(1, len(vals)):
            better = vals[i] > val
            idx = jnp.where(better, i, idx)
            val = jnp.where(better, vals[i], val)
        return idx

    i1 = argmax_first(s_in)
    i2 = argmax_first([jnp.where(i1 == i, neg, s_in[i]) for i in range(EXPERTS_PER_GROUP)])
    zero = jnp.zeros(p_in[0].shape, F32)
    g1 = sum(jnp.where(i1 == i, p_in[i], zero) for i in range(EXPERTS_PER_GROUP))
    g2 = sum(jnp.where(i2 == i, p_in[i], zero) for i in range(EXPERTS_PER_GROUP))
    tot = g1 + g2
    w1, w2 = g1 / tot, g2 / tot
    first_low = i1 < i2
    low, high = jnp.minimum(i1, i2), jnp.maximum(i1, i2)
    w_low, w_high = jnp.where(first_low, w1, w2), jnp.where(first_low, w2, w1)
    pos = jnp.where(low == 0, high - 1, jnp.where(low == 1, jnp.where(high == 2, 4, 3), 5))
    swap = low == 2
    bucket = best * PAIRS_PER_GROUP + pos
    return jnp.concatenate([jnp.where(swap, w_high, w_low), jnp.where(swap, w_low, w_high)], axis=0), bucket


def _post_kernel(x_ref, at_ref, yc_ref, wglu_ref, gs_ref, wo_ref, g1_ref, n2_ref, sc_ref, sh_ref,
                 wrt_ref, rb_ref, x1_ref, h2_ref, cw_ref, gid_ref, y_scr):
    nchunk = y_scr.shape[1] // SSM_CHUNK
    for s in range(SSM_CHUNK):
        for j in range(SSM_COLS):
            y_scr[j, pl.ds(s, nchunk, stride=SSM_CHUNK), :] = yc_ref[j, :, s * LANES:(s + 1) * LANES].astype(F32)
    yg = jax.nn.gelu(jnp.concatenate([y_scr[j] for j in range(SSM_COLS)], axis=1))
    z = yg * jax.nn.sigmoid(jnp.dot(yg.astype(BF16), wglu_ref[...], preferred_element_type=F32))
    zn = _rms(z, gs_ref[...]).astype(BF16)
    o = (jnp.dot(at_ref[...], wo_ref[:ATTN_WIDTH, :], preferred_element_type=F32)
         + jnp.dot(zn, wo_ref[ATTN_WIDTH:, :], preferred_element_type=F32))
    x1 = x_ref[...] + g1_ref[...] * o
    x1_ref[...] = x1
    h2 = _rms(x1, n2_ref[...]) * (1.0 + sc_ref[...]) + sh_ref[...]
    h2_ref[...] = h2
    logits = lax.dot_general(wrt_ref[...], h2, (((1,), (1,)), ((), ())), precision=lax.Precision.HIGHEST,
                             preferred_element_type=F32)
    cw, bucket = _route(logits, rb_ref[...])
    cw_ref[...] = cw
    gid_ref[...] = bucket


def _post(x, attn, yc, w_glu, ssm_g, w_out, g1, n2g, sc2, sh2, w_router_t, router_bias, seq):
    t, d = x.shape
    tm = min(TOKEN_TILE, seq)
    per_b = seq // tm
    tok = lambda w: pl.BlockSpec((tm, w), lambda i: (i, 0))
    const = lambda a: pl.BlockSpec(a.shape, lambda i: (0,) * a.ndim)
    per_batch = pl.BlockSpec((None, 1, d), lambda i: (i // per_b, 0, 0))
    col = lambda r: pl.BlockSpec((r, tm), lambda i: (0, i))
    chunked = pl.BlockSpec((SSM_COLS, tm // SSM_CHUNK, CHUNK_LANES), lambda i: (0, i, 0))
    return pl.pallas_call(
        _post_kernel,
        grid=(t // tm,),
        in_specs=[tok(d), tok(ATTN_WIDTH), chunked, const(w_glu), const(ssm_g),
                  const(w_out), per_batch, const(n2g), per_batch, per_batch, const(w_router_t), const(router_bias)],
        out_specs=[tok(d), tok(d), col(2), col(1)],
        out_shape=[jax.ShapeDtypeStruct((t, d), F32), jax.ShapeDtypeStruct((t, d), F32),
                   jax.ShapeDtypeStruct((2, t), F32), jax.ShapeDtypeStruct((1, t), jnp.int32)],
        scratch_shapes=[pltpu.VMEM((SSM_COLS, tm, LANES), F32)],
        compiler_params=_params(("parallel",), VMEM_LIMIT),
        name="post_mix",
    )(x, attn, yc, w_glu, ssm_g, w_out, g1, n2g, sc2, sh2, w_router_t, router_bias)


def _moe_kernel(rb_ref, bk_ref, ea_ref, eb_ref, first_ref, fresh_a_ref, fresh_b_ref, meta_ref, offs_ref,
                x_ref, cw_ref, wg_a_ref, wg_b_ref, wu_a_ref, wu_b_ref, wd_a_ref, wd_b_ref,
                o_ref, wg_s, wu_s, wd_s):
    i = pl.program_id(0)

    @pl.when(i < meta_ref[0])
    def _():
        for k, (fresh_ref, g_ref, u_ref, d_ref) in enumerate(((fresh_a_ref, wg_a_ref, wu_a_ref, wd_a_ref),
                                                               (fresh_b_ref, wg_b_ref, wu_b_ref, wd_b_ref))):
            @pl.when(fresh_ref[i] == 1)
            def _():
                wg_s[k] = g_ref[...].astype(BF16)
                wu_s[k] = u_ref[...].astype(BF16)
                wd_s[k] = d_ref[...].astype(BF16)

        bucket = bk_ref[i]
        rows = rb_ref[i] * MOE_ROWS + lax.broadcasted_iota(jnp.int32, (MOE_ROWS, 1), 0)
        in_bucket = (rows >= offs_ref[bucket]) & (rows < offs_ref[bucket + 1])
        cw = jnp.where(in_bucket, cw_ref[...], 0.0)
        x = x_ref[...].astype(BF16)
        y = None
        for k in range(2):
            gate = jnp.dot(x, wg_s[k], preferred_element_type=F32)
            up = jnp.dot(x, wu_s[k], preferred_element_type=F32)
            act = (gate * jax.nn.sigmoid(gate) * up * cw[:, k:k + 1]).astype(BF16)
            yk = jnp.dot(act, wd_s[k], preferred_element_type=F32)
            y = yk if y is None else y + yk

        @pl.when(first_ref[i] == 1)
        def _():
            o_ref[...] = y

        @pl.when(first_ref[i] == 0)
        def _():
            o_ref[...] += y


def _moe_items(bucket, t):
    nbk = N_EXPERT_GROUPS * PAIRS_PER_GROUP
    order = jnp.argsort(bucket, stable=True).astype(jnp.int32)
    counts = jnp.sum((bucket[None, :] == jnp.arange(nbk, dtype=jnp.int32)[:, None]).astype(jnp.int32), axis=1)
    offs = jnp.concatenate([jnp.zeros((1,), jnp.int32), jnp.cumsum(counts).astype(jnp.int32)])
    nblk = t // MOE_ROWS
    n_items_max = nblk + nbk - 1
    lo = jnp.arange(nblk, dtype=jnp.int32)[:, None] * MOE_ROWS
    valid = ((offs[None, :-1] < lo + MOE_ROWS) & (offs[None, 1:] > lo) & (offs[None, 1:] > offs[None, :-1])).reshape(-1)
    flat = jnp.arange(nblk * nbk, dtype=jnp.int32)
    ranked = jnp.sort(jnp.where(valid, flat, nblk * nbk))[:n_items_max]
    n_items = jnp.sum(valid.astype(jnp.int32))
    ranked = ranked[jnp.minimum(jnp.arange(n_items_max, dtype=jnp.int32), n_items - 1)]
    item_rb, item_bk = ranked // nbk, ranked % nbk
    slots = jnp.asarray(np.array(PAIR_SLOTS, np.int32))
    base = (item_bk // PAIRS_PER_GROUP) * EXPERTS_PER_GROUP
    item_a, item_b = base + slots[item_bk % PAIRS_PER_GROUP, 0], base + slots[item_bk % PAIRS_PER_GROUP, 1]
    one = jnp.ones((1,), jnp.int32)
    changed = lambda a: jnp.concatenate([one, (a[1:] != a[:-1]).astype(jnp.int32)])
    tables = (item_rb, item_bk, item_a, item_b, changed(item_rb), changed(item_a), changed(item_b),
              n_items.reshape(1), offs)
    return order, tables, n_items_max


def _moe(xs, cws, w_gate, w_up, w_down, layer, tables, n_items_max):
    t, d = xs.shape
    ff = w_gate.shape[3]
    lo_map = lambda i, rb, bk, ea, eb, *_: (layer, ea[i], 0, 0)
    hi_map = lambda i, rb, bk, ea, eb, *_: (layer, eb[i], 0, 0)
    row_map = lambda i, rb, *_: (rb[i], 0)
    up_blk = lambda m: pl.BlockSpec((None, None, d, ff), m)
    down_blk = lambda m: pl.BlockSpec((None, None, ff, d), m)
    grid_spec = pltpu.PrefetchScalarGridSpec(
        num_scalar_prefetch=len(tables),
        grid=(n_items_max,),
        in_specs=[pl.BlockSpec((MOE_ROWS, d), row_map), pl.BlockSpec((MOE_ROWS, 2), row_map),
                  up_blk(lo_map), up_blk(hi_map), up_blk(lo_map), up_blk(hi_map), down_blk(lo_map), down_blk(hi_map)],
        out_specs=pl.BlockSpec((MOE_ROWS, d), row_map),
        scratch_shapes=[pltpu.VMEM((2, d, ff), BF16), pltpu.VMEM((2, d, ff), BF16), pltpu.VMEM((2, ff, d), BF16)],
    )
    return pl.pallas_call(
        _moe_kernel,
        grid_spec=grid_spec,
        out_shape=jax.ShapeDtypeStruct((t, d), F32),
        compiler_params=_params(("arbitrary",), VMEM_LIMIT),
        name="moe_grouped",
    )(*tables, xs, cws, w_gate, w_gate, w_up, w_up, w_down, w_down)


def _take_rows(a, idx):
    return a.at[idx].get(mode="promise_in_bounds", unique_indices=True)


def _final_kernel(x_ref, y_ref, g_ref, o_ref):
    o_ref[...] = x_ref[...] + g_ref[...] * y_ref[...]


def _final(x1, y, g2, seq):
    t, d = x1.shape
    tm = min(TOKEN_TILE, seq)
    per_b = seq // tm
    tok = lambda w: pl.BlockSpec((tm, w), lambda i: (i, 0))
    return pl.pallas_call(
        _final_kernel,
        grid=(t // tm,),
        in_specs=[tok(d), tok(d), pl.BlockSpec((None, 1, d), lambda i: (i // per_b, 0, 0))],
        out_specs=tok(d),
        out_shape=jax.ShapeDtypeStruct((t, d), F32),
        compiler_params=_params(("parallel",)),
        name="final_residual",
    )(x1, y, g2)


def kernel(x, c, positions, ada_w, ada_b, norm1_g, w_in, q_norm_g, k_norm_g, attn_sink, lam_re, lam_im, ssm_b_re, ssm_b_im, ssm_c_re, ssm_c_im, ssm_d, ssm_log_dt, w_glu, attn_out_g, ssm_out_g, w_out, norm2_g, w_router, router_bias, w_exp_gate, w_exp_up, w_exp_down):
    batch, seq, d = x.shape
    depth = ada_w.shape[0]
    t = batch * seq
    assert seq % ATTN_BLOCK == 0 and seq % SSM_CHUNK == 0 and t % MOE_ROWS == 0

    mod = _adaln_mod(c, ada_w, ada_b).reshape(depth, 6, batch, 1, d)
    cos, sin = _rope_tables(positions)
    head_sum, rot = _rope_constants()
    bias = _attn_bias()
    w_router_t = w_router.T
    s5_mats = _s5_prep(lam_re, lam_im, ssm_b_re, ssm_b_im, ssm_c_re, ssm_c_im, ssm_d, ssm_log_dt)
    router_bias_col = router_bias.reshape(N_EXPERTS, 1)

    xf = x.reshape(t, d)
    res = None
    for l in range(depth):
        sh1, sc1, g1, sh2, sc2, g2 = (mod[l, j] for j in range(6))
        qg = (jnp.tile(q_norm_g[l], N_Q_HEADS) * HEAD_DIM ** -0.5).reshape(1, ATTN_WIDTH)
        kg = jnp.tile(k_norm_g[l], N_KV_HEADS).reshape(1, KV_WIDTH)
        outs = _inproj(xf, res, sc1, sh1, norm1_g[l].reshape(1, d), w_in[l].astype(BF16), qg, kg, head_sum, rot,
                       cos, sin, seq)
        if res is None:
            q, kx, vx, uc = outs
        else:
            q, kx, vx, uc, xf = outs
        attn = _attention(q, kx, vx, attn_sink[l], attn_out_g[l].reshape(1, ATTN_WIDTH), bias, batch, seq)
        yc = _s5_scan(uc, s5_mats, l, seq // SSM_CHUNK, batch)
        x1, h2, cw, gid = _post(xf, attn, yc, w_glu[l].astype(BF16), ssm_out_g[l].reshape(1, SSM_WIDTH),
                                 w_out[l].astype(BF16), g1, norm2_g[l].reshape(1, d), sc2, sh2, w_router_t,
                                 router_bias_col, seq)
        order, tables, n_items_max = _moe_items(gid.reshape(t), t)
        y_sorted = _moe(_take_rows(h2, order), _take_rows(cw.T, order), w_exp_gate, w_exp_up, w_exp_down, l,
                        tables, n_items_max)
        y = _take_rows(y_sorted, jnp.argsort(order).astype(jnp.int32))
        xf, res = x1, (y, g2)
    y, g2 = res
    return _final(xf, y, g2, seq).reshape(batch, seq, d)
```

```python
import functools

import numpy as np
import jax
import jax.numpy as jnp
from jax import lax
from jax.experimental import pallas as pl
from jax.experimental.pallas import tpu as pltpu

F32 = jnp.float32
BF16 = jnp.bfloat16

HEAD_DIM = 64
N_Q_HEADS = 8
N_KV_HEADS = 2
Q_PER_KV = N_Q_HEADS // N_KV_HEADS
ATTN_WIDTH = N_Q_HEADS * HEAD_DIM
KV_WIDTH = N_KV_HEADS * HEAD_DIM
ATTN_BLOCK = 128
ATTN_Q_TILE = 512
ROPE_THETA = 10000.0
LANES = 128
SSM_GROUP_CH = 16
SSM_GROUPS = 32
SSM_WIDTH = SSM_GROUPS * SSM_GROUP_CH
SSM_STATE = 64
SSM_CHUNK = 16
SSM_COLS = SSM_WIDTH // LANES
COL_GROUPS = LANES // SSM_GROUP_CH
COL_STATE = COL_GROUPS * SSM_STATE
CHUNK_LANES = SSM_CHUNK * LANES
SSM_NSPLIT = 4
N_EXPERTS = 16
N_EXPERT_GROUPS = 4
EXPERTS_PER_GROUP = N_EXPERTS // N_EXPERT_GROUPS
PAIRS_PER_GROUP = EXPERTS_PER_GROUP * (EXPERTS_PER_GROUP - 1) // 2
PAIR_SLOTS = ((0, 1), (0, 2), (0, 3), (1, 3), (1, 2), (3, 2))
EPS = 1e-6
MASK_BIAS = -1e30

TOKEN_TILE = 512
POST_SPLIT = 1
MOE_ROWS = 256
VMEM_LIMIT = 48 * 1024 * 1024


def _params(sem, vmem=None):
    return pltpu.CompilerParams(dimension_semantics=sem, vmem_limit_bytes=vmem)


def _rms(x, g):
    return x * lax.rsqrt(jnp.mean(x * x, axis=-1, keepdims=True) + EPS) * g


def _mod_kernel(c_ref, w_ref, b_ref, o_ref):
    c = c_ref[...]
    s = c * jax.nn.sigmoid(c)
    o_ref[...] = jnp.dot(s.astype(BF16), w_ref[...].astype(BF16), preferred_element_type=F32) + b_ref[...]


def _adaln_mod(c, ada_w, ada_b):
    depth, d, d6 = ada_w.shape
    nb = c.shape[0]
    n6 = d6 // d
    return pl.pallas_call(
        _mod_kernel,
        grid=(depth, n6),
        in_specs=[pl.BlockSpec((nb, d), lambda l, j: (0, 0)),
                  pl.BlockSpec((None, d, d), lambda l, j: (l, 0, j)),
                  pl.BlockSpec((None, None, 1, d), lambda l, j: (l, j, 0, 0))],
        out_specs=pl.BlockSpec((None, None, nb, d), lambda l, j: (l, j, 0, 0)),
        out_shape=jax.ShapeDtypeStruct((depth, n6, nb, d), F32),
        compiler_params=_params(("arbitrary", "arbitrary"), VMEM_LIMIT),
        name="adaln_mod",
    )(c, ada_w, ada_b.reshape(depth, n6, 1, d))


def _rope_kernel(pos_ref, freq_ref, cos_ref, sin_ref):
    ang = pos_ref[...].astype(F32) * freq_ref[...]
    cos_ref[...] = jnp.cos(ang)
    sin_ref[...] = jnp.sin(ang)


def _rope_tables(positions):
    half = HEAD_DIM // 2
    t = positions.size
    per_row = LANES // half
    rows = t // per_row
    pos_rep = jnp.repeat(positions.reshape(rows, per_row), half, axis=1)
    freq = (ROPE_THETA ** (-np.arange(half, dtype=np.float64) / half)).astype(np.float32)
    freq_row = jnp.asarray(np.tile(freq, per_row)[None, :])
    blk = min(rows, 512)
    cos, sin = pl.pallas_call(
        _rope_kernel,
        grid=(rows // blk,),
        in_specs=[pl.BlockSpec((blk, LANES), lambda i: (i, 0)),
                  pl.BlockSpec((1, LANES), lambda i: (0, 0))],
        out_specs=[pl.BlockSpec((blk, LANES), lambda i: (i, 0))] * 2,
        out_shape=[jax.ShapeDtypeStruct((rows, LANES), F32)] * 2,
        compiler_params=_params(("arbitrary",)),
        name="rope_tables",
    )(pos_rep, freq_row)
    widen = lambda a: jnp.tile(a.reshape(t, half), (1, per_row))
    return widen(cos), widen(sin)


def _rope_constants():
    lane = np.arange(ATTN_WIDTH)
    head_sum = (lane[:, None] // HEAD_DIM == lane[None, :] // HEAD_DIM).astype(np.float32)
    half = HEAD_DIM // 2
    rot = np.zeros((ATTN_WIDTH, ATTN_WIDTH), np.float32)
    for d in range(ATTN_WIDTH):
        if d % HEAD_DIM < half:
            rot[d + half, d] = -1.0
        else:
            rot[d - half, d] = 1.0
    return jnp.asarray(head_sum, BF16), jnp.asarray(rot, BF16)


def _inproj_kernel(*refs, has_res):
    if has_res:
        (x_ref, y_ref, g2_ref, sc_ref, sh_ref, n1_ref, w_ref, qg_ref, kg_ref, hs_ref, rot_ref,
         cos_ref, sin_ref, q_ref, k_ref, v_ref, uc_ref, xo_ref, u_scr) = refs
        x = x_ref[...] + g2_ref[...] * y_ref[...]
        xo_ref[...] = x
    else:
        (x_ref, sc_ref, sh_ref, n1_ref, w_ref, qg_ref, kg_ref, hs_ref, rot_ref,
         cos_ref, sin_ref, q_ref, k_ref, v_ref, uc_ref, u_scr) = refs
        x = x_ref[...]
    h = _rms(x, n1_ref[...]) * (1.0 + sc_ref[...]) + sh_ref[...]
    proj = jnp.dot(h.astype(BF16), w_ref[...], preferred_element_type=F32)
    q = proj[:, :ATTN_WIDTH]
    k = proj[:, ATTN_WIDTH:ATTN_WIDTH + KV_WIDTH]
    v = proj[:, ATTN_WIDTH + KV_WIDTH:ATTN_WIDTH + 2 * KV_WIDTH]
    cos = cos_ref[...]
    sin = sin_ref[...]
    reps = ATTN_WIDTH // LANES
    cos_q = jnp.concatenate([cos] * reps, axis=1)
    sin_q = jnp.concatenate([sin] * reps, axis=1)

    def head_norm_rope(t, head_sum, rot, gain, c, s):
        ssq = jnp.dot((t * t).astype(BF16), head_sum, preferred_element_type=F32)
        tn = (t * lax.rsqrt(ssq * (1.0 / HEAD_DIM) + EPS) * gain).astype(BF16)
        tr = jnp.dot(tn, rot, preferred_element_type=F32)
        return tn.astype(F32) * c + tr * s

    qo = head_norm_rope(q, hs_ref[...], rot_ref[...], qg_ref[...], cos_q, sin_q)
    ko = head_norm_rope(k, hs_ref[:KV_WIDTH, :KV_WIDTH], rot_ref[:KV_WIDTH, :KV_WIDTH], kg_ref[...], cos, sin)
    q_ref[...] = qo.astype(BF16)
    k_ref[...] = jnp.concatenate([ko, pltpu.roll(ko, HEAD_DIM, axis=1)], axis=1).astype(BF16)
    v_ref[...] = jnp.concatenate([v, pltpu.roll(v, HEAD_DIM, axis=1)], axis=1).astype(BF16)
    u0 = ATTN_WIDTH + 2 * KV_WIDTH
    nchunk = u_scr.shape[1] // SSM_CHUNK
    for j in range(SSM_COLS):
        u_scr[j] = proj[:, u0 + j * LANES:u0 + (j + 1) * LANES]
    for s in range(SSM_CHUNK):
        for j in range(SSM_COLS):
            lanes = slice(s * SSM_WIDTH + j * LANES, s * SSM_WIDTH + (j + 1) * LANES)
            uc_ref[:, lanes] = u_scr[j, pl.ds(s, nchunk, stride=SSM_CHUNK), :].astype(BF16)


def _inproj(x, res, sc1, sh1, n1g, w_in, qg, kg, head_sum, rot, cos, sin, seq):
    t, d = x.shape
    tm = min(TOKEN_TILE, seq)
    per_b = seq // tm
    in_width = w_in.shape[1]
    tok = lambda w: pl.BlockSpec((tm, w), lambda i: (i, 0))
    const = lambda a: pl.BlockSpec(a.shape, lambda i: (0,) * a.ndim)
    per_batch = pl.BlockSpec((None, 1, d), lambda i: (i // per_b, 0, 0))
    chunked = pl.BlockSpec((tm // SSM_CHUNK, SSM_CHUNK * SSM_WIDTH), lambda i: (i, 0))
    ins, specs = [x], [tok(d)]
    if res is not None:
        y_prev, g2_prev = res
        ins += [y_prev, g2_prev]
        specs += [tok(d), per_batch]
    ins += [sc1, sh1, n1g, w_in, qg, kg, head_sum, rot, cos, sin]
    specs += [per_batch, per_batch, const(n1g), const(w_in), const(qg), const(kg), const(head_sum), const(rot),
              tok(LANES), tok(LANES)]
    out_shape = [jax.ShapeDtypeStruct((t, ATTN_WIDTH), BF16), jax.ShapeDtypeStruct((t, 2 * KV_WIDTH), BF16),
                 jax.ShapeDtypeStruct((t, 2 * KV_WIDTH), BF16),
                 jax.ShapeDtypeStruct((t // SSM_CHUNK, SSM_CHUNK * SSM_WIDTH), BF16)]
    out_specs = [tok(ATTN_WIDTH), tok(2 * KV_WIDTH), tok(2 * KV_WIDTH), chunked]
    if res is not None:
        out_shape.append(jax.ShapeDtypeStruct((t, d), F32))
        out_specs.append(tok(d))
    assert in_width == ATTN_WIDTH + 2 * KV_WIDTH + SSM_WIDTH
    return pl.pallas_call(
        functools.partial(_inproj_kernel, has_res=res is not None),
        grid=(t // tm,),
        in_specs=specs,
        out_specs=out_specs,
        out_shape=out_shape,
        scratch_shapes=[pltpu.VMEM((SSM_COLS, tm, LANES), F32)],
        compiler_params=_params(("parallel",), VMEM_LIMIT),
        name="inproj",
    )(*ins)


def _attn_kernel(sink_ref, q_ref, kc_ref, kp_ref, vc_ref, vp_ref, bias_ref, g_ref, o_ref):
    nsub = q_ref.shape[0] // ATTN_BLOCK
    kk = jnp.concatenate([kp_ref[...], kc_ref[...]], axis=0)
    vv = jnp.concatenate([vp_ref[...], vc_ref[...]], axis=0)
    low = lax.broadcasted_iota(jnp.int32, (kk.shape[0], KV_WIDTH), 1) < HEAD_DIM
    zero = jnp.zeros((kk.shape[0], KV_WIDTH), BF16)

    def variants(a):
        nat, swp = a[:, :KV_WIDTH], a[:, KV_WIDTH:]
        return {(0, 0): jnp.where(low, nat, zero), (0, 1): jnp.where(low, zero, swp),
                (1, 0): jnp.where(low, swp, zero), (1, 1): jnp.where(low, zero, nat)}

    kvar, vvar = variants(kk), variants(vv)
    band = bias_ref[1]
    first = bias_ref[jnp.minimum(pl.program_id(1), 1)]
    upper = lax.broadcasted_iota(jnp.int32, (2 * ATTN_BLOCK, 1), 0) < ATTN_BLOCK
    for j in range(nsub):
        bias = first if j == 0 else band
        bias2 = jnp.concatenate([bias, bias], axis=0)
        keys = slice(j * ATTN_BLOCK, (j + 2) * ATTN_BLOCK)
        qrows = slice(j * ATTN_BLOCK, (j + 1) * ATTN_BLOCK)
        tiles = [None] * (N_Q_HEADS // 2)
        for kv in range(N_KV_HEADS):
            for half in range(2):
                pairs = (2 * kv, 2 * kv + 1)
                heads = (2 * pairs[0] + half, 2 * pairs[1] + half)
                qs = jnp.concatenate([q_ref[qrows, p * LANES:(p + 1) * LANES] for p in pairs], axis=0)
                s = lax.dot_general(qs, kvar[(kv, half)][keys], (((1,), (1,)), ((), ())),
                                    preferred_element_type=F32) + bias2
                sink = jnp.where(upper, sink_ref[heads[0]], sink_ref[heads[1]])
                m = jnp.maximum(jnp.max(s, axis=-1, keepdims=True), sink)
                p = jnp.exp(s - m)
                denom = jnp.sum(p, axis=-1, keepdims=True) + jnp.exp(sink - m)
                o = jnp.dot(p.astype(BF16), vvar[(kv, half)][keys], preferred_element_type=F32) * (1.0 / denom)
                for r, pr in enumerate(pairs):
                    part = o[r * ATTN_BLOCK:(r + 1) * ATTN_BLOCK]
                    tiles[pr] = part if tiles[pr] is None else tiles[pr] + part
        a = jnp.concatenate(tiles, axis=1)
        o_ref[qrows, :] = _rms(a, g_ref[...]).astype(BF16)


def _attn_bias():
    qi = np.arange(ATTN_BLOCK)[:, None]
    sj = np.arange(2 * ATTN_BLOCK)[None, :]
    diff = qi + ATTN_BLOCK - sj
    band = (diff >= 0) & (diff < ATTN_BLOCK)
    first = band & (sj >= ATTN_BLOCK)
    return jnp.asarray(np.where(np.stack([first, band]), 0.0, MASK_BIAS).astype(np.float32))


def _attention(q, kx, vx, sink, out_g, bias, batch, seq):
    t = q.shape[0]
    qb = min(ATTN_Q_TILE, seq)
    nsub = qb // ATTN_BLOCK
    nq = seq // qb
    nb = seq // ATTN_BLOCK
    cur = lambda w: pl.BlockSpec((qb, w), lambda b, n, s: (b * nq + n, 0))
    prev = lambda w: pl.BlockSpec((ATTN_BLOCK, w), lambda b, n, s: (b * nb + jnp.maximum(n * nsub - 1, 0), 0))
    grid_spec = pltpu.PrefetchScalarGridSpec(
        num_scalar_prefetch=1,
        grid=(batch, nq),
        in_specs=[cur(ATTN_WIDTH), cur(2 * KV_WIDTH), prev(2 * KV_WIDTH), cur(2 * KV_WIDTH), prev(2 * KV_WIDTH),
                  pl.BlockSpec(bias.shape, lambda b, n, s: (0, 0, 0)),
                  pl.BlockSpec((1, ATTN_WIDTH), lambda b, n, s: (0, 0))],
        out_specs=cur(ATTN_WIDTH),
    )
    return pl.pallas_call(
        _attn_kernel,
        grid_spec=grid_spec,
        out_shape=jax.ShapeDtypeStruct((t, ATTN_WIDTH), BF16),
        compiler_params=_params(("parallel", "arbitrary")),
        name="swa_attention",
    )(sink, q, kx, kx, vx, vx, bias, out_g)


def _spread(x, expander3):
    hi = x.astype(BF16)
    r1 = x - hi.astype(F32)
    mid = r1.astype(BF16)
    lo = (r1 - mid.astype(F32)).astype(BF16)
    return jnp.dot(jnp.concatenate([hi, mid, lo], axis=1), expander3, preferred_element_type=F32)


def _s5_prep_kernel(lr_re_ref, lr_im_ref, ldt_ref, bt_re_ref, bt_im_ref, ct_re_ref, ct_im_ref,
                    d_ref, lcol_re_ref, lcol_im_ref, ldtcol_ref, exp_ref, exph_ref, expt_ref, expw_ref,
                    t_ref, w_ref, v_ref, la_ref, lb_ref):
    hi = lax.Precision.HIGHEST
    nl = SSM_CHUNK
    low = lax.broadcasted_iota(jnp.int32, (1, 2 * SSM_STATE), 1) < SSM_STATE
    row_low = lax.broadcasted_iota(jnp.int32, (2 * SSM_STATE, 1), 0) < SSM_STATE
    jcol = lax.broadcasted_iota(jnp.int32, (nl, 1), 0).astype(F32)
    kt_lane = lax.broadcasted_iota(jnp.int32, (SSM_GROUP_CH, nl * SSM_GROUP_CH), 1)
    kt_row = lax.broadcasted_iota(jnp.int32, (SSM_GROUP_CH, nl * SSM_GROUP_CH), 0)

    w_all, v_all, kt_all = [], [], []
    for gm in range(COL_GROUPS):
        dt = jnp.exp(ldt_ref[gm])
        lam_re, lam_im = lr_re_ref[gm], lr_im_ref[gm]
        a_r, th_r = lam_re * dt, lam_im * dt

        er = jnp.exp(jcol * a_r)
        pw_re, pw_im = er * jnp.cos(jcol * th_r), er * jnp.sin(jcol * th_r)

        nr, ni = pw_re[1:2, :] - 1.0, pw_im[1:2, :]
        den = lam_re * lam_re + lam_im * lam_im
        c_re, c_im = (nr * lam_re + ni * lam_im) / den, (ni * lam_re - nr * lam_im) / den
        bt_re, bt_im = bt_re_ref[gm], bt_im_ref[gm]
        bb_re, bb_im = c_re * bt_re - c_im * bt_im, c_re * bt_im + c_im * bt_re

        w_rows = []
        for s in range(nl):
            j = nl - 1 - s
            pr, pi = pw_re[j:j + 1, :], pw_im[j:j + 1, :]
            w_rows.append(jnp.where(low, pr * bb_re - pi * bb_im, pr * bb_im + pi * bb_re))
        w_all.append(w_rows)

        pw_re_t, pw_im_t = pw_re.T, pw_im.T
        pc, ps = _spread(pw_re_t, exp_ref[...]), _spread(pw_im_t, exp_ref[...])
        ct_re, ct_im = _spread(ct_re_ref[gm], exph_ref[...]), _spread(ct_im_ref[gm], exph_ref[...])
        a_re, a_im = ct_re * pc - ct_im * ps, ct_re * ps + ct_im * pc
        a_cat = jnp.where(row_low, a_re, -a_im)
        l1_re, l1_im = pw_re_t[:, 1:2], pw_im_t[:, 1:2]
        v_re, v_im = a_re * l1_re - a_im * l1_im, a_re * l1_im + a_im * l1_re
        v_all.append(jnp.where(row_low, v_re, -v_im))

        kt = jnp.dot(jnp.where(low, bb_re, bb_im), a_cat, precision=hi, preferred_element_type=F32)
        kt_all.append(kt + jnp.where(kt_lane == kt_row, d_ref[gm], 0.0))

    def expand(stacked, expander, row_group, lane_group):
        wide = jnp.dot(stacked.astype(BF16), expander, preferred_element_type=F32)
        r = lax.broadcasted_iota(jnp.int32, wide.shape, 0)
        c = lax.broadcasted_iota(jnp.int32, wide.shape, 1)
        return jnp.where(row_group(r) == lane_group(c), wide, 0.0).astype(BF16)

    chan_group = lambda i: (i >> 4) & (COL_GROUPS - 1)
    state_group = lambda i: (i >> 6) & (COL_GROUPS - 1)

    bd = expand(jnp.concatenate(kt_all, axis=0), expt_ref[...], chan_group, chan_group)
    t_ref[0:LANES, :] = bd
    for s in range(1, nl):
        t_ref[s * LANES:(s + 1) * LANES, :] = jnp.concatenate(
            [jnp.zeros((LANES, s * LANES), BF16), bd[:, :CHUNK_LANES - s * LANES]], axis=1)

    w_stack = jnp.concatenate([w_all[gm][s] for s in range(nl) for gm in range(COL_GROUPS)], axis=0)
    w_ref[...] = expand(w_stack, expw_ref[...], chan_group, state_group)

    v_stack = jnp.concatenate([v_all[gm][half * SSM_STATE:(half + 1) * SSM_STATE, :]
                               for half in range(2) for gm in range(COL_GROUPS)], axis=0)
    v_ref[...] = expand(v_stack, expt_ref[...], state_group, chan_group)

    dtc = jnp.exp(ldtcol_ref[...])
    e16 = jnp.exp(nl * lcol_re_ref[...] * dtc)
    ang = nl * lcol_im_ref[...] * dtc
    la_ref[...] = e16 * jnp.cos(ang)
    lb_ref[...] = e16 * jnp.sin(ang)


def _s5_prep(lam_re, lam_im, b_re, b_im, c_re, c_im, d_skip, log_dt):
    g, p, h, nl = SSM_GROUPS, SSM_STATE, SSM_GROUP_CH, SSM_CHUNK
    cg = COL_GROUPS
    nc = lam_re.shape[0] * SSM_COLS
    col = lambda a: a.reshape((nc, cg) + a.shape[2:])
    dup_row = lambda a: col(jnp.tile(a, (1, 1, 2))[:, :, None, :])
    bt = lambda a: col(jnp.tile(jnp.swapaxes(a, 2, 3), (1, 1, 1, 2)))
    ct = lambda a: col(jnp.tile(jnp.swapaxes(a, 2, 3), (1, 1, 2, 1)))
    d_pad = col(jnp.pad(d_skip.reshape(-1, g, 1, h), ((0, 0), (0, 0), (0, 0), (0, nl * h - h))))
    wide = lambda a: a.reshape(nc, 1, cg * p)
    expand = jnp.asarray(np.tile(np.repeat(np.eye(nl, dtype=np.float32), h, axis=1), (3, 1)), BF16)
    expand_h = jnp.asarray(np.tile(np.eye(h, dtype=np.float32), (3, nl)), BF16)
    exp_t = np.zeros((nl, h, nl, cg, h), np.float32)
    exp_w = np.zeros((2, p, 2, cg, p), np.float32)
    for gm in range(cg):
        exp_t[:, :, :, gm, :] = np.eye(nl * h, dtype=np.float32).reshape(nl, h, nl, h)
        exp_w[:, :, :, gm, :] = np.eye(2 * p, dtype=np.float32).reshape(2, p, 2, p)
    exp_t = jnp.asarray(exp_t.reshape(nl * h, CHUNK_LANES), BF16)
    exp_w = jnp.asarray(exp_w.reshape(2 * p, 2 * COL_STATE), BF16)
    blk = lambda *s: pl.BlockSpec((None,) + s, lambda i: (i,) + (0,) * len(s))
    const = lambda a: pl.BlockSpec(a.shape, lambda i: (0,) * a.ndim)
    lw = nl * h
    return pl.pallas_call(
        _s5_prep_kernel,
        grid=(nc,),
        in_specs=[blk(cg, 1, 2 * p), blk(cg, 1, 2 * p), blk(cg, 1, 1),
                  blk(cg, h, 2 * p), blk(cg, h, 2 * p), blk(cg, 2 * p, h), blk(cg, 2 * p, h), blk(cg, 1, lw),
                  blk(1, cg * p), blk(1, cg * p), blk(1, cg * p), const(expand), const(expand_h), const(exp_t),
                  const(exp_w)],
        out_specs=[blk(CHUNK_LANES, CHUNK_LANES), blk(CHUNK_LANES, 2 * COL_STATE), blk(2 * COL_STATE, CHUNK_LANES),
                   blk(1, COL_STATE), blk(1, COL_STATE)],
        out_shape=[jax.ShapeDtypeStruct((nc, CHUNK_LANES, CHUNK_LANES), BF16),
                   jax.ShapeDtypeStruct((nc, CHUNK_LANES, 2 * COL_STATE), BF16),
                   jax.ShapeDtypeStruct((nc, 2 * COL_STATE, CHUNK_LANES), BF16),
                   jax.ShapeDtypeStruct((nc, 1, COL_STATE), F32), jax.ShapeDtypeStruct((nc, 1, COL_STATE), F32)],
        compiler_params=_params(("parallel",), VMEM_LIMIT),
        name="s5_prep",
    )(dup_row(lam_re), dup_row(lam_im), col(log_dt[:, :, None, None]),
      bt(b_re), bt(b_im), ct(c_re), ct(c_im), d_pad, wide(lam_re), wide(lam_im),
      wide(jnp.repeat(log_dt, p, axis=1)), expand, expand_h, exp_t, exp_w)


def _s5_kernel(*refs, nchunks, nb):
    uc_refs = refs[:SSM_CHUNK]
    t_ref, w_ref, v_ref, la_ref, lb_ref, o_ref, ucat_ref, s_ref, xp_ref = refs[SSM_CHUNK:]

    @pl.when(pl.program_id(1) == 0)
    def _():
        for s in range(SSM_CHUNK):
            ucat_ref[:, s * LANES:(s + 1) * LANES] = uc_refs[s][...]
        s_in = jnp.dot(ucat_ref[...], w_ref[...], preferred_element_type=F32)
        nblk = COL_STATE // LANES
        for b in range(2 * nblk):
            s_ref[b] = s_in[:, b * LANES:(b + 1) * LANES]
        lr = [jnp.broadcast_to(la_ref[:, b * LANES:(b + 1) * LANES], (nb, LANES)) for b in range(nblk)]
        li = [jnp.broadcast_to(lb_ref[:, b * LANES:(b + 1) * LANES], (nb, LANES)) for b in range(nblk)]

        def step(c, carry):
            rows = pl.ds(c, nb, stride=nchunks)
            out = []
            for b in range(nblk):
                re, im = carry[2 * b], carry[2 * b + 1]
                xp_ref[b, rows, :] = re
                xp_ref[nblk + b, rows, :] = im
                out.append(lr[b] * re - li[b] * im + s_ref[b, rows, :])
                out.append(lr[b] * im + li[b] * re + s_ref[nblk + b, rows, :])
            return tuple(out)

        zero = jnp.zeros((nb, LANES), F32)
        lax.fori_loop(0, nchunks, step, (zero,) * (2 * nblk), unroll=4)

    xp = jnp.concatenate([xp_ref[b] for b in range(2 * COL_STATE // LANES)], axis=1).astype(BF16)
    inter = jnp.dot(xp, v_ref[...], preferred_element_type=F32)
    for kk in range(SSM_NSPLIT):
        @pl.when(pl.program_id(1) == kk)
        def _():
            live = (kk + 1) * (CHUNK_LANES // SSM_NSPLIT)
            intra = jnp.dot(ucat_ref[:, :live], t_ref[:live, :], preferred_element_type=F32)
            o_ref[...] = (intra + inter).astype(BF16)


def _s5_scan(uc, mats, layer, nchunks, nb):
    rows = uc.shape[0]
    c0 = layer * SSM_COLS
    split = CHUNK_LANES // SSM_NSPLIT
    u_spec = lambda s: pl.BlockSpec((rows, LANES), lambda j, k: (0, SSM_COLS * s + j))
    return pl.pallas_call(
        functools.partial(_s5_kernel, nchunks=nchunks, nb=nb),
        grid=(SSM_COLS, SSM_NSPLIT),
        in_specs=[u_spec(s) for s in range(SSM_CHUNK)] + [
            pl.BlockSpec((None, CHUNK_LANES, split), lambda j, k: (c0 + j, 0, k)),
            pl.BlockSpec((None, CHUNK_LANES, 2 * COL_STATE), lambda j, k: (c0 + j, 0, 0)),
            pl.BlockSpec((None, 2 * COL_STATE, split), lambda j, k: (c0 + j, 0, k)),
            pl.BlockSpec((None, 1, COL_STATE), lambda j, k: (c0 + j, 0, 0)),
            pl.BlockSpec((None, 1, COL_STATE), lambda j, k: (c0 + j, 0, 0))],
        out_specs=pl.BlockSpec((None, rows, split), lambda j, k: (j, 0, k)),
        out_shape=jax.ShapeDtypeStruct((SSM_COLS, rows, CHUNK_LANES), BF16),
        scratch_shapes=[pltpu.VMEM((rows, CHUNK_LANES), BF16),
                        pltpu.VMEM((2 * COL_STATE // LANES, rows, LANES), F32),
                        pltpu.VMEM((2 * COL_STATE // LANES, rows, LANES), F32)],
        compiler_params=_params(("parallel", "arbitrary"), VMEM_LIMIT),
        name="s5_scan",
    )(*([uc] * SSM_CHUNK), *mats)


def _route(logits, bias):
    m = jnp.max(logits, axis=0, keepdims=True)
    e = jnp.exp(logits - m)
    probs = e / jnp.sum(e, axis=0, keepdims=True)
    sel = probs + bias
    row = lambda a, i: a[i:i + 1, :]
    best_score, best = None, None
    for grp in range(N_EXPERT_GROUPS):
        a, b, c, d = (row(sel, EXPERTS_PER_GROUP * grp + i) for i in range(EXPERTS_PER_GROUP))
        hab, lab, hcd, lcd = jnp.maximum(a, b), jnp.minimum(a, b), jnp.maximum(c, d), jnp.minimum(c, d)
        top1 = jnp.maximum(hab, hcd)
        top2 = jnp.maximum(jnp.maximum(lab, lcd), jnp.minimum(hab, hcd))
        score = top1 + top2
        if grp == 0:
            best_score, best = score, jnp.zeros(score.shape, jnp.int32)
        else:
            better = score > best_score
            best = jnp.where(better, grp, best)
            best_score = jnp.where(better, score, best_score)

    def pick(a, i):
        out = row(a, i)
        for grp in range(1, N_EXPERT_GROUPS):
            out = jnp.where(best == grp, row(a, EXPERTS_PER_GROUP * grp + i), out)
        return out

    s_in = [pick(sel, i) for i in range(EXPERTS_PER_GROUP)]
    p_in = [pick(probs, i) for i in range(EXPERTS_PER_GROUP)]
    neg = jnp.full(s_in[0].shape, -jnp.inf, F32)

    def argmax_first(vals):
        idx, val = jnp.zeros(vals[0].shape, jnp.int32), vals[0]
        for i in range(1, len(vals)):
            better = vals[i] > val
            idx = jnp.where(better, i, idx)
            val = jnp.where(better, vals[i], val)
        return idx

    i1 = argmax_first(s_in)
    i2 = argmax_first([jnp.where(i1 == i, neg, s_in[i]) for i in range(EXPERTS_PER_GROUP)])
    zero = jnp.zeros(p_in[0].shape, F32)
    g1 = sum(jnp.where(i1 == i, p_in[i], zero) for i in range(EXPERTS_PER_GROUP))
    g2 = sum(jnp.where(i2 == i, p_in[i], zero) for i in range(EXPERTS_PER_GROUP))
    tot = g1 + g2
    w1, w2 = g1 / tot, g2 / tot
    first_low = i1 < i2
    low, high = jnp.minimum(i1, i2), jnp.maximum(i1, i2)
    w_low, w_high = jnp.where(first_low, w1, w2), jnp.where(first_low, w2, w1)
    pos = jnp.where(low == 0, high - 1, jnp.where(low == 1, jnp.where(high == 2, 4, 3), 5))
    swap = low == 2
    bucket = best * PAIRS_PER_GROUP + pos
    return jnp.concatenate([jnp.where(swap, w_high, w_low), jnp.where(swap, w_low, w_high)], axis=0), bucket


def _post_kernel(x_ref, at_ref, yc_ref, wglu_ref, gs_ref, wo_ref, g1_ref, n2_ref, sc_ref, sh_ref,
                 wrt_ref, rb_ref, x1_ref, h2_ref, cw_ref, gid_ref, y_scr):
    nchunk = y_scr.shape[1] // SSM_CHUNK
    for s in range(SSM_CHUNK):
        for j in range(SSM_COLS):
            y_scr[j, pl.ds(s, nchunk, stride=SSM_CHUNK), :] = yc_ref[j, :, s * LANES:(s + 1) * LANES].astype(F32)
    tm = x_ref.shape[0]
    sub = tm // POST_SPLIT
    for part in range(POST_SPLIT):
        rows = slice(part * sub, (part + 1) * sub)
        yg = jax.nn.gelu(jnp.concatenate([y_scr[j, rows, :] for j in range(SSM_COLS)], axis=1))
        z = yg * jax.nn.sigmoid(jnp.dot(yg.astype(BF16), wglu_ref[...], preferred_element_type=F32))
        zn = _rms(z, gs_ref[...]).astype(BF16)
        o = (jnp.dot(at_ref[rows, :], wo_ref[:ATTN_WIDTH, :], preferred_element_type=F32)
             + jnp.dot(zn, wo_ref[ATTN_WIDTH:, :], preferred_element_type=F32))
        x1 = x_ref[rows, :] + g1_ref[...] * o
        x1_ref[rows, :] = x1
        h2 = _rms(x1, n2_ref[...]) * (1.0 + sc_ref[...]) + sh_ref[...]
        h2_ref[rows, :] = h2
        logits = lax.dot_general(wrt_ref[...], h2, (((1,), (1,)), ((), ())), precision=lax.Precision.HIGHEST,
                                 preferred_element_type=F32)
        cw, bucket = _route(logits, rb_ref[...])
        cw_ref[:, rows] = cw
        gid_ref[:, rows] = bucket


def _post(x, attn, yc, w_glu, ssm_g, w_out, g1, n2g, sc2, sh2, w_router_t, router_bias, seq):
    t, d = x.shape
    tm = min(TOKEN_TILE, seq)
    per_b = seq // tm
    tok = lambda w: pl.BlockSpec((tm, w), lambda i: (i, 0))
    const = lambda a: pl.BlockSpec(a.shape, lambda i: (0,) * a.ndim)
    per_batch = pl.BlockSpec((None, 1, d), lambda i: (i // per_b, 0, 0))
    col = lambda r: pl.BlockSpec((r, tm), lambda i: (0, i))
    chunked = pl.BlockSpec((SSM_COLS, tm // SSM_CHUNK, CHUNK_LANES), lambda i: (0, i, 0))
    return pl.pallas_call(
        _post_kernel,
        grid=(t // tm,),
        in_specs=[tok(d), tok(ATTN_WIDTH), chunked, const(w_glu), const(ssm_g),
                  const(w_out), per_batch, const(n2g), per_batch, per_batch, const(w_router_t), const(router_bias)],
        out_specs=[tok(d), tok(d), col(2), col(1)],
        out_shape=[jax.ShapeDtypeStruct((t, d), F32), jax.ShapeDtypeStruct((t, d), F32),
                   jax.ShapeDtypeStruct((2, t), F32), jax.ShapeDtypeStruct((1, t), jnp.int32)],
        scratch_shapes=[pltpu.VMEM((SSM_COLS, tm, LANES), F32)],
        compiler_params=_params(("parallel",), VMEM_LIMIT),
        name="post_mix",
    )(x, attn, yc, w_glu, ssm_g, w_out, g1, n2g, sc2, sh2, w_router_t, router_bias)


def _moe_kernel(rb_ref, bk_ref, ea_ref, eb_ref, first_ref, fresh_a_ref, fresh_b_ref, meta_ref, offs_ref,
                x_ref, cw_ref, wg_a_ref, wg_b_ref, wu_a_ref, wu_b_ref, wd_a_ref, wd_b_ref,
                o_ref, wg_s, wu_s, wd_s):
    i = pl.program_id(0)

    @pl.when(i < meta_ref[0])
    def _():
        for k, (fresh_ref, g_ref, u_ref, d_ref) in enumerate(((fresh_a_ref, wg_a_ref, wu_a_ref, wd_a_ref),
                                                               (fresh_b_ref, wg_b_ref, wu_b_ref, wd_b_ref))):
            @pl.when(fresh_ref[i] == 1)
            def _():
                wg_s[k] = g_ref[...].astype(BF16)
                wu_s[k] = u_ref[...].astype(BF16)
                wd_s[k] = d_ref[...].astype(BF16)

        bucket = bk_ref[i]
        rows = rb_ref[i] * MOE_ROWS + lax.broadcasted_iota(jnp.int32, (MOE_ROWS, 1), 0)
        in_bucket = (rows >= offs_ref[bucket]) & (rows < offs_ref[bucket + 1])
        cw = jnp.where(in_bucket, cw_ref[...], 0.0)
        x = x_ref[...].astype(BF16)
        y = None
        for k in range(2):
            gate = jnp.dot(x, wg_s[k], preferred_element_type=F32)
            up = jnp.dot(x, wu_s[k], preferred_element_type=F32)
            act = (gate * jax.nn.sigmoid(gate) * up * cw[:, k:k + 1]).astype(BF16)
            yk = jnp.dot(act, wd_s[k], preferred_element_type=F32)
            y = yk if y is None else y + yk

        @pl.when(first_ref[i] == 1)
        def _():
            o_ref[...] = y

        @pl.when(first_ref[i] == 0)
        def _():
            o_ref[...] += y


def _moe_items(bucket, t):
    nbk = N_EXPERT_GROUPS * PAIRS_PER_GROUP
    order = jnp.argsort(bucket, stable=True).astype(jnp.int32)
    counts = jnp.sum((bucket[None, :] == jnp.arange(nbk, dtype=jnp.int32)[:, None]).astype(jnp.int32), axis=1)
    offs = jnp.concatenate([jnp.zeros((1,), jnp.int32), jnp.cumsum(counts).astype(jnp.int32)])
    nblk = t // MOE_ROWS
    n_items_max = nblk + nbk - 1
    lo = jnp.arange(nblk, dtype=jnp.int32)[:, None] * MOE_ROWS
    valid = ((offs[None, :-1] < lo + MOE_ROWS) & (offs[None, 1:] > lo) & (offs[None, 1:] > offs[None, :-1])).reshape(-1)
    flat = jnp.arange(nblk * nbk, dtype=jnp.int32)
    ranked = jnp.sort(jnp.where(valid, flat, nblk * nbk))[:n_items_max]
    n_items = jnp.sum(valid.astype(jnp.int32))
    ranked = ranked[jnp.minimum(jnp.arange(n_items_max, dtype=jnp.int32), n_items - 1)]
    item_rb, item_bk = ranked // nbk, ranked % nbk
    slots = jnp.asarray(np.array(PAIR_SLOTS, np.int32))
    base = (item_bk // PAIRS_PER_GROUP) * EXPERTS_PER_GROUP
    item_a, item_b = base + slots[item_bk % PAIRS_PER_GROUP, 0], base + slots[item_bk % PAIRS_PER_GROUP, 1]
    one = jnp.ones((1,), jnp.int32)
    changed = lambda a: jnp.concatenate([one, (a[1:] != a[:-1]).astype(jnp.int32)])
    tables = (item_rb, item_bk, item_a, item_b, changed(item_rb), changed(item_a), changed(item_b),
              n_items.reshape(1), offs)
    return order, tables, n_items_max


def _moe(xs, cws, w_gate, w_up, w_down, layer, tables, n_items_max):
    t, d = xs.shape
    ff = w_gate.shape[3]
    lo_map = lambda i, rb, bk, ea, eb, *_: (layer, ea[i], 0, 0)
    hi_map = lambda i, rb, bk, ea, eb, *_: (layer, eb[i], 0, 0)
    row_map = lambda i, rb, *_: (rb[i], 0)
    up_blk = lambda m: pl.BlockSpec((None, None, d, ff), m)
    down_blk = lambda m: pl.BlockSpec((None, None, ff, d), m)
    grid_spec = pltpu.PrefetchScalarGridSpec(
        num_scalar_prefetch=len(tables),
        grid=(n_items_max,),
        in_specs=[pl.BlockSpec((MOE_ROWS, d), row_map), pl.BlockSpec((MOE_ROWS, 2), row_map),
                  up_blk(lo_map), up_blk(hi_map), up_blk(lo_map), up_blk(hi_map), down_blk(lo_map), down_blk(hi_map)],
        out_specs=pl.BlockSpec((MOE_ROWS, d), row_map),
        scratch_shapes=[pltpu.VMEM((2, d, ff), BF16), pltpu.VMEM((2, d, ff), BF16), pltpu.VMEM((2, ff, d), BF16)],
    )
    return pl.pallas_call(
        _moe_kernel,
        grid_spec=grid_spec,
        out_shape=jax.ShapeDtypeStruct((t, d), F32),
        compiler_params=_params(("arbitrary",), VMEM_LIMIT),
        name="moe_grouped",
    )(*tables, xs, cws, w_gate, w_gate, w_up, w_up, w_down, w_down)


def _take_rows(a, idx):
    return a.at[idx].get(mode="promise_in_bounds", unique_indices=True)


def _final_kernel(x_ref, y_ref, g_ref, o_ref):
    o_ref[...] = x_ref[...] + g_ref[...] * y_ref[...]


def _final(x1, y, g2, seq):
    t, d = x1.shape
    tm = min(TOKEN_TILE, seq)
    per_b = seq // tm
    tok = lambda w: pl.BlockSpec((tm, w), lambda i: (i, 0))
    return pl.pallas_call(
        _final_kernel,
        grid=(t // tm,),
        in_specs=[tok(d), tok(d), pl.BlockSpec((None, 1, d), lambda i: (i // per_b, 0, 0))],
        out_specs=tok(d),
        out_shape=jax.ShapeDtypeStruct((t, d), F32),
        compiler_params=_params(("parallel",)),
        name="final_residual",
    )(x1, y, g2)


def kernel(x, c, positions, ada_w, ada_b, norm1_g, w_in, q_norm_g, k_norm_g, attn_sink, lam_re, lam_im, ssm_b_re, ssm_b_im, ssm_c_re, ssm_c_im, ssm_d, ssm_log_dt, w_glu, attn_out_g, ssm_out_g, w_out, norm2_g, w_router, router_bias, w_exp_gate, w_exp_up, w_exp_down):
    batch, seq, d = x.shape
    depth = ada_w.shape[0]
    t = batch * seq
    assert seq % ATTN_BLOCK == 0 and seq % SSM_CHUNK == 0 and t % MOE_ROWS == 0

    mod = _adaln_mod(c, ada_w, ada_b).reshape(depth, 6, batch, 1, d)
    cos, sin = _rope_tables(positions)
    head_sum, rot = _rope_constants()
    bias = _attn_bias()
    w_router_t = w_router.T
    s5_mats = _s5_prep(lam_re, lam_im, ssm_b_re, ssm_b_im, ssm_c_re, ssm_c_im, ssm_d, ssm_log_dt)
    router_bias_col = router_bias.reshape(N_EXPERTS, 1)

    xf = x.reshape(t, d)
    res = None
    for l in range(depth):
        sh1, sc1, g1, sh2, sc2, g2 = (mod[l, j] for j in range(6))
        qg = (jnp.tile(q_norm_g[l], N_Q_HEADS) * HEAD_DIM ** -0.5).reshape(1, ATTN_WIDTH)
        kg = jnp.tile(k_norm_g[l], N_KV_HEADS).reshape(1, KV_WIDTH)
        outs = _inproj(xf, res, sc1, sh1, norm1_g[l].reshape(1, d), w_in[l].astype(BF16), qg, kg, head_sum, rot,
                       cos, sin, seq)
        if res is None:
            q, kx, vx, uc = outs
        else:
            q, kx, vx, uc, xf = outs
        attn = _attention(q, kx, vx, attn_sink[l], attn_out_g[l].reshape(1, ATTN_WIDTH), bias, batch, seq)
        yc = _s5_scan(uc, s5_mats, l, seq // SSM_CHUNK, batch)
        x1, h2, cw, gid = _post(xf, attn, yc, w_glu[l].astype(BF16), ssm_out_g[l].reshape(1, SSM_WIDTH),
                                 w_out[l].astype(BF16), g1, norm2_g[l].reshape(1, d), sc2, sh2, w_router_t,
                                 router_bias_col, seq)
        order, tables, n_items_max = _moe_items(gid.reshape(t), t)
        y_sorted = _moe(_take_rows(h2, order), _take_rows(cw.T, order), w_exp_gate, w_exp_up, w_exp_down, l,
                        tables, n_items_max)
        y = _take_rows(y_sorted, jnp.argsort(order).astype(jnp.int32))
        xf, res = x1, (y, g2)
    y, g2 = res
    return _final(xf, y, g2, seq).reshape(batch, seq, d)
```

```python
import functools

import numpy as np
import jax
import jax.numpy as jnp
from jax import lax
from jax.experimental import pallas as pl
from jax.experimental.pallas import tpu as pltpu

F32 = jnp.float32
BF16 = jnp.bfloat16

HEAD_DIM = 64
N_Q_HEADS = 8
N_KV_HEADS = 2
Q_PER_KV = N_Q_HEADS // N_KV_HEADS
ATTN_WIDTH = N_Q_HEADS * HEAD_DIM
KV_WIDTH = N_KV_HEADS * HEAD_DIM
ATTN_BLOCK = 128
ATTN_Q_TILE = 512
ROPE_THETA = 10000.0
ROPE_SLAB = 256
LANES = 128
SSM_GROUP_CH = 16
SSM_GROUPS = 32
SSM_WIDTH = SSM_GROUPS * SSM_GROUP_CH
SSM_STATE = 64
SSM_CHUNK = 16
SSM_COLS = SSM_WIDTH // LANES
COL_GROUPS = LANES // SSM_GROUP_CH
COL_STATE = COL_GROUPS * SSM_STATE
CHUNK_LANES = SSM_CHUNK * LANES
SSM_NSPLIT = 4
N_EXPERTS = 16
N_EXPERT_GROUPS = 4
EXPERTS_PER_GROUP = N_EXPERTS // N_EXPERT_GROUPS
PAIRS_PER_GROUP = EXPERTS_PER_GROUP * (EXPERTS_PER_GROUP - 1) // 2
PAIR_SLOTS = ((0, 1), (0, 2), (0, 3), (1, 3), (1, 2), (3, 2))
EPS = 1e-6
MASK_BIAS = -1e30

TOKEN_TILE = 512
POST_SPLIT = 1
MOE_ROWS = 256
VMEM_LIMIT = 48 * 1024 * 1024
SEQ_GROUPS = 2


def _params(sem, vmem=None):
    return pltpu.CompilerParams(dimension_semantics=sem, vmem_limit_bytes=vmem)


def _rms(x, g):
    return x * lax.rsqrt(jnp.mean(x * x, axis=-1, keepdims=True) + EPS) * g


def _mod_kernel(c_ref, w_ref, b_ref, o_ref):
    c = c_ref[...]
    s = c * jax.nn.sigmoid(c)
    o_ref[...] = jnp.dot(s.astype(BF16), w_ref[...].astype(BF16), preferred_element_type=F32) + b_ref[...]


def _adaln_mod(c, ada_w, ada_b):
    depth, d, d6 = ada_w.shape
    nb = c.shape[0]
    n6 = d6 // d
    return pl.pallas_call(
        _mod_kernel,
        grid=(depth, n6),
        in_specs=[pl.BlockSpec((nb, d), lambda l, j: (0, 0)),
                  pl.BlockSpec((None, d, d), lambda l, j: (l, 0, j)),
                  pl.BlockSpec((None, None, 1, d), lambda l, j: (l, j, 0, 0))],
        out_specs=pl.BlockSpec((None, None, nb, d), lambda l, j: (l, j, 0, 0)),
        out_shape=jax.ShapeDtypeStruct((depth, n6, nb, d), F32),
        compiler_params=_params(("arbitrary", "arbitrary"), VMEM_LIMIT),
        name="adaln_mod",
    )(c, ada_w, ada_b.reshape(depth, n6, 1, d))


def _rope_kernel(pos_ref, freq_ref, cos_ref, sin_ref):
    ang = pos_ref[...].astype(F32) * freq_ref[...]
    cos_ref[...] = jnp.cos(ang)
    sin_ref[...] = jnp.sin(ang)


def _rope_tables(positions):
    half = HEAD_DIM // 2
    t = positions.size
    per_row = LANES // half
    rows = t // per_row
    pos_rep = jnp.repeat(positions.reshape(rows, per_row), half, axis=1)
    freq = (ROPE_THETA ** (-np.arange(half, dtype=np.float64) / half)).astype(np.float32)
    freq_row = jnp.asarray(np.tile(freq, per_row)[None, :])
    blk = min(rows, 512)
    cos, sin = pl.pallas_call(
        _rope_kernel,
        grid=(rows // blk,),
        in_specs=[pl.BlockSpec((blk, LANES), lambda i: (i, 0)),
                  pl.BlockSpec((1, LANES), lambda i: (0, 0))],
        out_specs=[pl.BlockSpec((blk, LANES), lambda i: (i, 0))] * 2,
        out_shape=[jax.ShapeDtypeStruct((rows, LANES), F32)] * 2,
        compiler_params=_params(("arbitrary",)),
        name="rope_tables",
    )(pos_rep, freq_row)
    widen = lambda a: jnp.tile(a.reshape(t, half), (1, per_row))
    return widen(cos), widen(sin)


def _rope_constants():
    lane = np.arange(ROPE_SLAB)
    head_sum = (lane[:, None] // HEAD_DIM == lane[None, :] // HEAD_DIM).astype(np.float32)
    half = HEAD_DIM // 2
    rot = np.zeros((ROPE_SLAB, ROPE_SLAB), np.float32)
    for d in range(ROPE_SLAB):
        if d % HEAD_DIM < half:
            rot[d + half, d] = -1.0
        else:
            rot[d - half, d] = 1.0
    return jnp.asarray(head_sum, BF16), jnp.asarray(rot, BF16)


def _inproj_kernel(*refs, has_res):
    if has_res:
        (x_ref, y_ref, g2_ref, sc_ref, sh_ref, n1_ref, w_ref, qg_ref, kg_ref, hs_ref, rot_ref,
         cos_ref, sin_ref, q_ref, k_ref, v_ref, uc_ref, xo_ref, u_scr) = refs
        x = x_ref[...] + g2_ref[...] * y_ref[...]
        xo_ref[...] = x
    else:
        (x_ref, sc_ref, sh_ref, n1_ref, w_ref, qg_ref, kg_ref, hs_ref, rot_ref,
         cos_ref, sin_ref, q_ref, k_ref, v_ref, uc_ref, u_scr) = refs
        x = x_ref[...]
    h = _rms(x, n1_ref[...]) * (1.0 + sc_ref[...]) + sh_ref[...]
    proj = jnp.dot(h.astype(BF16), w_ref[...], preferred_element_type=F32)
    q = proj[:, :ATTN_WIDTH]
    k = proj[:, ATTN_WIDTH:ATTN_WIDTH + KV_WIDTH]
    v = proj[:, ATTN_WIDTH + KV_WIDTH:ATTN_WIDTH + 2 * KV_WIDTH]
    cos = cos_ref[...]
    sin = sin_ref[...]
    reps = ATTN_WIDTH // LANES
    cos_q = jnp.concatenate([cos] * reps, axis=1)
    sin_q = jnp.concatenate([sin] * reps, axis=1)

    def head_norm_rope(t, gain, c, s):
        outs = []
        for lo in range(0, t.shape[1], ROPE_SLAB):
            wd = min(ROPE_SLAB, t.shape[1] - lo)
            ts, lanes = t[:, lo:lo + wd], slice(lo, lo + wd)
            ssq = jnp.dot((ts * ts).astype(BF16), hs_ref[:wd, :wd], preferred_element_type=F32)
            tn = (ts * lax.rsqrt(ssq * (1.0 / HEAD_DIM) + EPS) * gain[:, lanes]).astype(BF16)
            tr = jnp.dot(tn, rot_ref[:wd, :wd], preferred_element_type=F32)
            outs.append(tn.astype(F32) * c[:, lanes] + tr * s[:, lanes])
        return outs[0] if len(outs) == 1 else jnp.concatenate(outs, axis=1)

    qo = head_norm_rope(q, qg_ref[...], cos_q, sin_q)
    ko = head_norm_rope(k, kg_ref[...], cos, sin)
    q_ref[...] = qo.astype(BF16)
    k_ref[...] = jnp.concatenate([ko, pltpu.roll(ko, HEAD_DIM, axis=1)], axis=1).astype(BF16)
    v_ref[...] = jnp.concatenate([v, pltpu.roll(v, HEAD_DIM, axis=1)], axis=1).astype(BF16)
    u0 = ATTN_WIDTH + 2 * KV_WIDTH
    nchunk = u_scr.shape[1] // SSM_CHUNK
    for j in range(SSM_COLS):
        u_scr[j] = proj[:, u0 + j * LANES:u0 + (j + 1) * LANES]
    for s in range(SSM_CHUNK):
        for j in range(SSM_COLS):
            lanes = slice(s * SSM_WIDTH + j * LANES, s * SSM_WIDTH + (j + 1) * LANES)
            uc_ref[:, lanes] = u_scr[j, pl.ds(s, nchunk, stride=SSM_CHUNK), :].astype(BF16)


def _inproj(x, res, sc1, sh1, n1g, w_in, qg, kg, head_sum, rot, cos, sin, seq):
    t, d = x.shape
    tm = min(TOKEN_TILE, seq)
    per_b = seq // tm
    in_width = w_in.shape[1]
    tok = lambda w: pl.BlockSpec((tm, w), lambda i: (i, 0))
    const = lambda a: pl.BlockSpec(a.shape, lambda i: (0,) * a.ndim)
    per_batch = pl.BlockSpec((None, 1, d), lambda i: (i // per_b, 0, 0))
    chunked = pl.BlockSpec((tm // SSM_CHUNK, SSM_CHUNK * SSM_WIDTH), lambda i: (i, 0))
    ins, specs = [x], [tok(d)]
    if res is not None:
        y_prev, g2_prev = res
        ins += [y_prev, g2_prev]
        specs += [tok(d), per_batch]
    ins += [sc1, sh1, n1g, w_in, qg, kg, head_sum, rot, cos, sin]
    specs += [per_batch, per_batch, const(n1g), const(w_in), const(qg), const(kg), const(head_sum), const(rot),
              tok(LANES), tok(LANES)]
    out_shape = [jax.ShapeDtypeStruct((t, ATTN_WIDTH), BF16), jax.ShapeDtypeStruct((t, 2 * KV_WIDTH), BF16),
                 jax.ShapeDtypeStruct((t, 2 * KV_WIDTH), BF16),
                 jax.ShapeDtypeStruct((t // SSM_CHUNK, SSM_CHUNK * SSM_WIDTH), BF16)]
    out_specs = [tok(ATTN_WIDTH), tok(2 * KV_WIDTH), tok(2 * KV_WIDTH), chunked]
    if res is not None:
        out_shape.append(jax.ShapeDtypeStruct((t, d), F32))
        out_specs.append(tok(d))
    assert in_width == ATTN_WIDTH + 2 * KV_WIDTH + SSM_WIDTH
    return pl.pallas_call(
        functools.partial(_inproj_kernel, has_res=res is not None),
        grid=(t // tm,),
        in_specs=specs,
        out_specs=out_specs,
        out_shape=out_shape,
        scratch_shapes=[pltpu.VMEM((SSM_COLS, tm, LANES), F32)],
        compiler_params=_params(("parallel",), VMEM_LIMIT),
        name="inproj",
    )(*ins)


def _attn_kernel(sink_ref, q_ref, kc_ref, kp_ref, vc_ref, vp_ref, bias_ref, g_ref, o_ref):
    nsub = q_ref.shape[0] // ATTN_BLOCK
    kk = jnp.concatenate([kp_ref[...], kc_ref[...]], axis=0)
    vv = jnp.concatenate([vp_ref[...], vc_ref[...]], axis=0)
    low = lax.broadcasted_iota(jnp.int32, (kk.shape[0], KV_WIDTH), 1) < HEAD_DIM
    zero = jnp.zeros((kk.shape[0], KV_WIDTH), BF16)

    def variants(a):
        nat, swp = a[:, :KV_WIDTH], a[:, KV_WIDTH:]
        return {(0, 0): jnp.where(low, nat, zero), (0, 1): jnp.where(low, zero, swp),
                (1, 0): jnp.where(low, swp, zero), (1, 1): jnp.where(low, zero, nat)}

    kvar, vvar = variants(kk), variants(vv)
    band = bias_ref[1]
    first = bias_ref[jnp.minimum(pl.program_id(1), 1)]
    upper = lax.broadcasted_iota(jnp.int32, (2 * ATTN_BLOCK, 1), 0) < ATTN_BLOCK
    for j in range(nsub):
        bias = first if j == 0 else band
        bias2 = jnp.concatenate([bias, bias], axis=0)
        keys = slice(j * ATTN_BLOCK, (j + 2) * ATTN_BLOCK)
        qrows = slice(j * ATTN_BLOCK, (j + 1) * ATTN_BLOCK)
        tiles = [None] * (N_Q_HEADS // 2)
        for kv in range(N_KV_HEADS):
            for half in range(2):
                pairs = (2 * kv, 2 * kv + 1)
                heads = (2 * pairs[0] + half, 2 * pairs[1] + half)
                qs = jnp.concatenate([q_ref[qrows, p * LANES:(p + 1) * LANES] for p in pairs], axis=0)
                s = lax.dot_general(qs, kvar[(kv, half)][keys], (((1,), (1,)), ((), ())),
                                    preferred_element_type=F32) + bias2
                sink = jnp.where(upper, sink_ref[heads[0]], sink_ref[heads[1]])
                m = jnp.maximum(jnp.max(s, axis=-1, keepdims=True), sink)
                p = jnp.exp(s - m)
                denom = jnp.sum(p, axis=-1, keepdims=True) + jnp.exp(sink - m)
                o = jnp.dot(p.astype(BF16), vvar[(kv, half)][keys], preferred_element_type=F32) * (1.0 / denom)
                for r, pr in enumerate(pairs):
                    part = o[r * ATTN_BLOCK:(r + 1) * ATTN_BLOCK]
                    tiles[pr] = part if tiles[pr] is None else tiles[pr] + part
        a = jnp.concatenate(tiles, axis=1)
        o_ref[qrows, :] = _rms(a, g_ref[...]).astype(BF16)


def _attn_bias():
    qi = np.arange(ATTN_BLOCK)[:, None]
    sj = np.arange(2 * ATTN_BLOCK)[None, :]
    diff = qi + ATTN_BLOCK - sj
    band = (diff >= 0) & (diff < ATTN_BLOCK)
    first = band & (sj >= ATTN_BLOCK)
    return jnp.asarray(np.where(np.stack([first, band]), 0.0, MASK_BIAS).astype(np.float32))


def _attention(q, kx, vx, sink, out_g, bias, batch, seq):
    t = q.shape[0]
    qb = min(ATTN_Q_TILE, seq)
    nsub = qb // ATTN_BLOCK
    nq = seq // qb
    nb = seq // ATTN_BLOCK
    cur = lambda w: pl.BlockSpec((qb, w), lambda b, n, s: (b * nq + n, 0))
    prev = lambda w: pl.BlockSpec((ATTN_BLOCK, w), lambda b, n, s: (b * nb + jnp.maximum(n * nsub - 1, 0), 0))
    grid_spec = pltpu.PrefetchScalarGridSpec(
        num_scalar_prefetch=1,
        grid=(batch, nq),
        in_specs=[cur(ATTN_WIDTH), cur(2 * KV_WIDTH), prev(2 * KV_WIDTH), cur(2 * KV_WIDTH), prev(2 * KV_WIDTH),
                  pl.BlockSpec(bias.shape, lambda b, n, s: (0, 0, 0)),
                  pl.BlockSpec((1, ATTN_WIDTH), lambda b, n, s: (0, 0))],
        out_specs=cur(ATTN_WIDTH),
    )
    return pl.pallas_call(
        _attn_kernel,
        grid_spec=grid_spec,
        out_shape=jax.ShapeDtypeStruct((t, ATTN_WIDTH), BF16),
        compiler_params=_params(("parallel", "arbitrary")),
        name="swa_attention",
    )(sink, q, kx, kx, vx, vx, bias, out_g)


def _spread(x, expander3):
    hi = x.astype(BF16)
    r1 = x - hi.astype(F32)
    mid = r1.astype(BF16)
    lo = (r1 - mid.astype(F32)).astype(BF16)
    return jnp.dot(jnp.concatenate([hi, mid, lo], axis=1), expander3, preferred_element_type=F32)


def _s5_prep_kernel(lr_re_ref, lr_im_ref, ldt_ref, bt_re_ref, bt_im_ref, ct_re_ref, ct_im_ref,
                    d_ref, lcol_re_ref, lcol_im_ref, ldtcol_ref, exp_ref, exph_ref, expt_ref, expw_ref,
                    t_ref, w_ref, v_ref, la_ref, lb_ref):
    hi = lax.Precision.HIGHEST
    nl = SSM_CHUNK
    low = lax.broadcasted_iota(jnp.int32, (1, 2 * SSM_STATE), 1) < SSM_STATE
    row_low = lax.broadcasted_iota(jnp.int32, (2 * SSM_STATE, 1), 0) < SSM_STATE
    jcol = lax.broadcasted_iota(jnp.int32, (nl, 1), 0).astype(F32)
    kt_lane = lax.broadcasted_iota(jnp.int32, (SSM_GROUP_CH, nl * SSM_GROUP_CH), 1)
    kt_row = lax.broadcasted_iota(jnp.int32, (SSM_GROUP_CH, nl * SSM_GROUP_CH), 0)

    w_all, v_all, kt_all = [], [], []
    for gm in range(COL_GROUPS):
        dt = jnp.exp(ldt_ref[gm])
        lam_re, lam_im = lr_re_ref[gm], lr_im_ref[gm]
        a_r, th_r = lam_re * dt, lam_im * dt

        er = jnp.exp(jcol * a_r)
        pw_re, pw_im = er * jnp.cos(jcol * th_r), er * jnp.sin(jcol * th_r)

        nr, ni = pw_re[1:2, :] - 1.0, pw_im[1:2, :]
        den = lam_re * lam_re + lam_im * lam_im
        c_re, c_im = (nr * lam_re + ni * lam_im) / den, (ni * lam_re - nr * lam_im) / den
        bt_re, bt_im = bt_re_ref[gm], bt_im_ref[gm]
        bb_re, bb_im = c_re * bt_re - c_im * bt_im, c_re * bt_im + c_im * bt_re

        w_rows = []
        for s in range(nl):
            j = nl - 1 - s
            pr, pi = pw_re[j:j + 1, :], pw_im[j:j + 1, :]
            w_rows.append(jnp.where(low, pr * bb_re - pi * bb_im, pr * bb_im + pi * bb_re))
        w_all.append(w_rows)

        pw_re_t, pw_im_t = pw_re.T, pw_im.T
        pc, ps = _spread(pw_re_t, exp_ref[...]), _spread(pw_im_t, exp_ref[...])
        ct_re, ct_im = _spread(ct_re_ref[gm], exph_ref[...]), _spread(ct_im_ref[gm], exph_ref[...])
        a_re, a_im = ct_re * pc - ct_im * ps, ct_re * ps + ct_im * pc
        a_cat = jnp.where(row_low, a_re, -a_im)
        l1_re, l1_im = pw_re_t[:, 1:2], pw_im_t[:, 1:2]
        v_re, v_im = a_re * l1_re - a_im * l1_im, a_re * l1_im + a_im * l1_re
        v_all.append(jnp.where(row_low, v_re, -v_im))

        kt = jnp.dot(jnp.where(low, bb_re, bb_im), a_cat, precision=hi, preferred_element_type=F32)
        kt_all.append(kt + jnp.where(kt_lane == kt_row, d_ref[gm], 0.0))

    def expand(stacked, expander, row_group, lane_group):
        wide = jnp.dot(stacked.astype(BF16), expander, preferred_element_type=F32)
        r = lax.broadcasted_iota(jnp.int32, wide.shape, 0)
        c = lax.broadcasted_iota(jnp.int32, wide.shape, 1)
        return jnp.where(row_group(r) == lane_group(c), wide, 0.0).astype(BF16)

    chan_group = lambda i: (i >> 4) & (COL_GROUPS - 1)
    state_group = lambda i: (i >> 6) & (COL_GROUPS - 1)

    bd = expand(jnp.concatenate(kt_all, axis=0), expt_ref[...], chan_group, chan_group)
    t_ref[0:LANES, :] = bd
    for s in range(1, nl):
        t_ref[s * LANES:(s + 1) * LANES, :] = jnp.concatenate(
            [jnp.zeros((LANES, s * LANES), BF16), bd[:, :CHUNK_LANES - s * LANES]], axis=1)

    w_stack = jnp.concatenate([w_all[gm][s] for s in range(nl) for gm in range(COL_GROUPS)], axis=0)
    w_ref[...] = expand(w_stack, expw_ref[...], chan_group, state_group)

    v_stack = jnp.concatenate([v_all[gm][half * SSM_STATE:(half + 1) * SSM_STATE, :]
                               for half in range(2) for gm in range(COL_GROUPS)], axis=0)
    v_ref[...] = expand(v_stack, expt_ref[...], state_group, chan_group)

    dtc = jnp.exp(ldtcol_ref[...])
    e16 = jnp.exp(nl * lcol_re_ref[...] * dtc)
    ang = nl * lcol_im_ref[...] * dtc
    la_ref[...] = e16 * jnp.cos(ang)
    lb_ref[...] = e16 * jnp.sin(ang)


def _s5_prep(lam_re, lam_im, b_re, b_im, c_re, c_im, d_skip, log_dt):
    g, p, h, nl = SSM_GROUPS, SSM_STATE, SSM_GROUP_CH, SSM_CHUNK
    cg = COL_GROUPS
    nc = lam_re.shape[0] * SSM_COLS
    col = lambda a: a.reshape((nc, cg) + a.shape[2:])
    dup_row = lambda a: col(jnp.tile(a, (1, 1, 2))[:, :, None, :])
    bt = lambda a: col(jnp.tile(jnp.swapaxes(a, 2, 3), (1, 1, 1, 2)))
    ct = lambda a: col(jnp.tile(jnp.swapaxes(a, 2, 3), (1, 1, 2, 1)))
    d_pad = col(jnp.pad(d_skip.reshape(-1, g, 1, h), ((0, 0), (0, 0), (0, 0), (0, nl * h - h))))
    wide = lambda a: a.reshape(nc, 1, cg * p)
    expand = jnp.asarray(np.tile(np.repeat(np.eye(nl, dtype=np.float32), h, axis=1), (3, 1)), BF16)
    expand_h = jnp.asarray(np.tile(np.eye(h, dtype=np.float32), (3, nl)), BF16)
    exp_t = np.zeros((nl, h, nl, cg, h), np.float32)
    exp_w = np.zeros((2, p, 2, cg, p), np.float32)
    for gm in range(cg):
        exp_t[:, :, :, gm, :] = np.eye(nl * h, dtype=np.float32).reshape(nl, h, nl, h)
        exp_w[:, :, :, gm, :] = np.eye(2 * p, dtype=np.float32).reshape(2, p, 2, p)
    exp_t = jnp.asarray(exp_t.reshape(nl * h, CHUNK_LANES), BF16)
    exp_w = jnp.asarray(exp_w.reshape(2 * p, 2 * COL_STATE), BF16)
    blk = lambda *s: pl.BlockSpec((None,) + s, lambda i: (i,) + (0,) * len(s))
    const = lambda a: pl.BlockSpec(a.shape, lambda i: (0,) * a.ndim)
    lw = nl * h
    return pl.pallas_call(
        _s5_prep_kernel,
        grid=(nc,),
        in_specs=[blk(cg, 1, 2 * p), blk(cg, 1, 2 * p), blk(cg, 1, 1),
                  blk(cg, h, 2 * p), blk(cg, h, 2 * p), blk(cg, 2 * p, h), blk(cg, 2 * p, h), blk(cg, 1, lw),
                  blk(1, cg * p), blk(1, cg * p), blk(1, cg * p), const(expand), const(expand_h), const(exp_t),
                  const(exp_w)],
        out_specs=[blk(CHUNK_LANES, CHUNK_LANES), blk(CHUNK_LANES, 2 * COL_STATE), blk(2 * COL_STATE, CHUNK_LANES),
                   blk(1, COL_STATE), blk(1, COL_STATE)],
        out_shape=[jax.ShapeDtypeStruct((nc, CHUNK_LANES, CHUNK_LANES), BF16),
                   jax.ShapeDtypeStruct((nc, CHUNK_LANES, 2 * COL_STATE), BF16),
                   jax.ShapeDtypeStruct((nc, 2 * COL_STATE, CHUNK_LANES), BF16),
                   jax.ShapeDtypeStruct((nc, 1, COL_STATE), F32), jax.ShapeDtypeStruct((nc, 1, COL_STATE), F32)],
        compiler_params=_params(("parallel",), VMEM_LIMIT),
        name="s5_prep",
    )(dup_row(lam_re), dup_row(lam_im), col(log_dt[:, :, None, None]),
      bt(b_re), bt(b_im), ct(c_re), ct(c_im), d_pad, wide(lam_re), wide(lam_im),
      wide(jnp.repeat(log_dt, p, axis=1)), expand, expand_h, exp_t, exp_w)


def _s5_kernel(*refs, nchunks, nb):
    uc_refs = refs[:SSM_CHUNK]
    t_ref, w_ref, v_ref, la_ref, lb_ref, o_ref, ucat_ref, s_ref, xp_ref = refs[SSM_CHUNK:]

    @pl.when(pl.program_id(1) == 0)
    def _():
        for s in range(SSM_CHUNK):
            ucat_ref[:, s * LANES:(s + 1) * LANES] = uc_refs[s][...]
        s_in = jnp.dot(ucat_ref[...], w_ref[...], preferred_element_type=F32)
        nblk = COL_STATE // LANES
        for b in range(2 * nblk):
            s_ref[b] = s_in[:, b * LANES:(b + 1) * LANES]
        lr = [jnp.broadcast_to(la_ref[:, b * LANES:(b + 1) * LANES], (nb, LANES)) for b in range(nblk)]
        li = [jnp.broadcast_to(lb_ref[:, b * LANES:(b + 1) * LANES], (nb, LANES)) for b in range(nblk)]

        def step(c, carry):
            rows = pl.ds(c, nb, stride=nchunks)
            out = []
            for b in range(nblk):
                re, im = carry[2 * b], carry[2 * b + 1]
                xp_ref[b, rows, :] = re
                xp_ref[nblk + b, rows, :] = im
                out.append(lr[b] * re - li[b] * im + s_ref[b, rows, :])
                out.append(lr[b] * im + li[b] * re + s_ref[nblk + b, rows, :])
            return tuple(out)

        zero = jnp.zeros((nb, LANES), F32)
        lax.fori_loop(0, nchunks, step, (zero,) * (2 * nblk), unroll=4)

    xp = jnp.concatenate([xp_ref[b] for b in range(2 * COL_STATE // LANES)], axis=1).astype(BF16)
    inter = jnp.dot(xp, v_ref[...], preferred_element_type=F32)
    for kk in range(SSM_NSPLIT):
        @pl.when(pl.program_id(1) == kk)
        def _():
            live = (kk + 1) * (CHUNK_LANES // SSM_NSPLIT)
            intra = jnp.dot(ucat_ref[:, :live], t_ref[:live, :], preferred_element_type=F32)
            o_ref[...] = (intra + inter).astype(BF16)


def _s5_scan(uc, mats, layer, nchunks, nb):
    rows = uc.shape[0]
    c0 = layer * SSM_COLS
    split = CHUNK_LANES // SSM_NSPLIT
    u_spec = lambda s: pl.BlockSpec((rows, LANES), lambda j, k: (0, SSM_COLS * s + j))
    return pl.pallas_call(
        functools.partial(_s5_kernel, nchunks=nchunks, nb=nb),
        grid=(SSM_COLS, SSM_NSPLIT),
        in_specs=[u_spec(s) for s in range(SSM_CHUNK)] + [
            pl.BlockSpec((None, CHUNK_LANES, split), lambda j, k: (c0 + j, 0, k)),
            pl.BlockSpec((None, CHUNK_LANES, 2 * COL_STATE), lambda j, k: (c0 + j, 0, 0)),
            pl.BlockSpec((None, 2 * COL_STATE, split), lambda j, k: (c0 + j, 0, k)),
            pl.BlockSpec((None, 1, COL_STATE), lambda j, k: (c0 + j, 0, 0)),
            pl.BlockSpec((None, 1, COL_STATE), lambda j, k: (c0 + j, 0, 0))],
        out_specs=pl.BlockSpec((None, rows, split), lambda j, k: (j, 0, k)),
        out_shape=jax.ShapeDtypeStruct((SSM_COLS, rows, CHUNK_LANES), BF16),
        scratch_shapes=[pltpu.VMEM((rows, CHUNK_LANES), BF16),
                        pltpu.VMEM((2 * COL_STATE // LANES, rows, LANES), F32),
                        pltpu.VMEM((2 * COL_STATE // LANES, rows, LANES), F32)],
        compiler_params=_params(("parallel", "arbitrary"), VMEM_LIMIT),
        name="s5_scan",
    )(*([uc] * SSM_CHUNK), *mats)


def _route(logits, bias):
    m = jnp.max(logits, axis=0, keepdims=True)
    e = jnp.exp(logits - m)
    probs = e / jnp.sum(e, axis=0, keepdims=True)
    sel = probs + bias
    row = lambda a, i: a[i:i + 1, :]
    best_score, best = None, None
    for grp in range(N_EXPERT_GROUPS):
        a, b, c, d = (row(sel, EXPERTS_PER_GROUP * grp + i) for i in range(EXPERTS_PER_GROUP))
        hab, lab, hcd, lcd = jnp.maximum(a, b), jnp.minimum(a, b), jnp.maximum(c, d), jnp.minimum(c, d)
        top1 = jnp.maximum(hab, hcd)
        top2 = jnp.maximum(jnp.maximum(lab, lcd), jnp.minimum(hab, hcd))
        score = top1 + top2
        if grp == 0:
            best_score, best = score, jnp.zeros(score.shape, jnp.int32)
        else:
            better = score > best_score
            best = jnp.where(better, grp, best)
            best_score = jnp.where(better, score, best_score)

    def pick(a, i):
        out = row(a, i)
        for grp in range(1, N_EXPERT_GROUPS):
            out = jnp.where(best == grp, row(a, EXPERTS_PER_GROUP * grp + i), out)
        return out

    s_in = [pick(sel, i) for i in range(EXPERTS_PER_GROUP)]
    p_in = [pick(probs, i) for i in range(EXPERTS_PER_GROUP)]
    neg = jnp.full(s_in[0].shape, -jnp.inf, F32)

    def argmax_first(vals):
        idx, val = jnp.zeros(vals[0].shape, jnp.int32), vals[0]
        for i in range(1, len(vals)):
            better = vals[i] > val
            idx = jnp.where(better, i, idx)
            val = jnp.where(better, vals[i], val)
        return idx

    i1 = argmax_first(s_in)
    i2 = argmax_first([jnp.where(i1 == i, neg, s_in[i]) for i in range(EXPERTS_PER_GROUP)])
    zero = jnp.zeros(p_in[0].shape, F32)
    g1 = sum(jnp.where(i1 == i, p_in[i], zero) for i in range(EXPERTS_PER_GROUP))
    g2 = sum(jnp.where(i2 == i, p_in[i], zero) for i in range(EXPERTS_PER_GROUP))
    tot = g1 + g2
    w1, w2 = g1 / tot, g2 / tot
    first_low = i1 < i2
    low, high = jnp.minimum(i1, i2), jnp.maximum(i1, i2)
    w_low, w_high = jnp.where(first_low, w1, w2), jnp.where(first_low, w2, w1)
    pos = jnp.where(low == 0, high - 1, jnp.where(low == 1, jnp.where(high == 2, 4, 3), 5))
    swap = low == 2
    bucket = best * PAIRS_PER_GROUP + pos
    return jnp.concatenate([jnp.where(swap, w_high, w_low), jnp.where(swap, w_low, w_high)], axis=0), bucket


def _router_logits(w_t, h):
    w_hi = w_t.astype(BF16)
    w_r = w_t - w_hi.astype(F32)
    w_mid = w_r.astype(BF16)
    w_lo = (w_r - w_mid.astype(F32)).astype(BF16)
    h_hi = h.astype(BF16)
    h_lo = (h - h_hi.astype(F32)).astype(BF16)
    dims = (((1,), (1,)), ((), ()))
    a = lax.dot_general(jnp.concatenate([w_hi, w_mid, w_lo], axis=0), h_hi, dims, preferred_element_type=F32)
    b = lax.dot_general(jnp.concatenate([w_hi, w_mid], axis=0), h_lo, dims, preferred_element_type=F32)
    e = w_t.shape[0]
    return a[:e] + a[e:2 * e] + a[2 * e:] + b[:e] + b[e:]


def _post_kernel(x_ref, at_ref, yc_ref, wglu_ref, gs_ref, wo_ref, g1_ref, n2_ref, sc_ref, sh_ref,
                 wrt_ref, rb_ref, x1_ref, h2_ref, cw_ref, gid_ref, y_scr):
    nchunk = y_scr.shape[1] // SSM_CHUNK
    for s in range(SSM_CHUNK):
        for j in range(SSM_COLS):
            y_scr[j, pl.ds(s, nchunk, stride=SSM_CHUNK), :] = yc_ref[j, :, s * LANES:(s + 1) * LANES].astype(F32)
    tm = x_ref.shape[0]
    sub = tm // POST_SPLIT
    for part in range(POST_SPLIT):
        rows = slice(part * sub, (part + 1) * sub)
        yg = jax.nn.gelu(jnp.concatenate([y_scr[j, rows, :] for j in range(SSM_COLS)], axis=1))
        z = yg * jax.nn.sigmoid(jnp.dot(yg.astype(BF16), wglu_ref[...], preferred_element_type=F32))
        zn = _rms(z, gs_ref[...]).astype(BF16)
        o = (jnp.dot(at_ref[rows, :], wo_ref[:ATTN_WIDTH, :], preferred_element_type=F32)
             + jnp.dot(zn, wo_ref[ATTN_WIDTH:, :], preferred_element_type=F32))
        x1 = x_ref[rows, :] + g1_ref[...] * o
        x1_ref[rows, :] = x1
        h2 = _rms(x1, n2_ref[...]) * (1.0 + sc_ref[...]) + sh_ref[...]
        h2_ref[rows, :] = h2
        logits = _router_logits(wrt_ref[...], h2)
        cw, bucket = _route(logits, rb_ref[...])
        cw_ref[:, rows] = cw
        gid_ref[:, rows] = bucket


def _post(x, attn, yc, w_glu, ssm_g, w_out, g1, n2g, sc2, sh2, w_router_t, router_bias, seq):
    t, d = x.shape
    tm = min(TOKEN_TILE, seq)
    per_b = seq // tm
    tok = lambda w: pl.BlockSpec((tm, w), lambda i: (i, 0))
    const = lambda a: pl.BlockSpec(a.shape, lambda i: (0,) * a.ndim)
    per_batch = pl.BlockSpec((None, 1, d), lambda i: (i // per_b, 0, 0))
    col = lambda r: pl.BlockSpec((r, tm), lambda i: (0, i))
    chunked = pl.BlockSpec((SSM_COLS, tm // SSM_CHUNK, CHUNK_LANES), lambda i: (0, i, 0))
    return pl.pallas_call(
        _post_kernel,
        grid=(t // tm,),
        in_specs=[tok(d), tok(ATTN_WIDTH), chunked, const(w_glu), const(ssm_g),
                  const(w_out), per_batch, const(n2g), per_batch, per_batch, const(w_router_t), const(router_bias)],
        out_specs=[tok(d), tok(d), col(2), col(1)],
        out_shape=[jax.ShapeDtypeStruct((t, d), F32), jax.ShapeDtypeStruct((t, d), F32),
                   jax.ShapeDtypeStruct((2, t), F32), jax.ShapeDtypeStruct((1, t), jnp.int32)],
        scratch_shapes=[pltpu.VMEM((SSM_COLS, tm, LANES), F32)],
        compiler_params=_params(("parallel",), VMEM_LIMIT),
        name="post_mix",
    )(x, attn, yc, w_glu, ssm_g, w_out, g1, n2g, sc2, sh2, w_router_t, router_bias)


def _moe_kernel(rb_ref, bk_ref, ea_ref, eb_ref, first_ref, fresh_a_ref, fresh_b_ref, meta_ref, offs_ref,
                x_ref, cw_ref, wg_a_ref, wg_b_ref, wu_a_ref, wu_b_ref, wd_a_ref, wd_b_ref,
                o_ref, wg_s, wu_s, wd_s):
    i = pl.program_id(0)

    @pl.when(i < meta_ref[0])
    def _():
        for k, (fresh_ref, g_ref, u_ref, d_ref) in enumerate(((fresh_a_ref, wg_a_ref, wu_a_ref, wd_a_ref),
                                                               (fresh_b_ref, wg_b_ref, wu_b_ref, wd_b_ref))):
            @pl.when(fresh_ref[i] == 1)
            def _():
                wg_s[k] = g_ref[...].astype(BF16)
                wu_s[k] = u_ref[...].astype(BF16)
                wd_s[k] = d_ref[...].astype(BF16)

        bucket = bk_ref[i]
        base = rb_ref[i] * MOE_ROWS
        lo_row, hi_row = offs_ref[bucket] - base, offs_ref[bucket + 1] - base
        half = MOE_ROWS // 2

        def run(r0, r1):
            rows = r0 + lax.broadcasted_iota(jnp.int32, (r1 - r0, 1), 0)
            cw = jnp.where((rows >= lo_row) & (rows < hi_row), cw_ref[r0:r1, :], 0.0)
            x = x_ref[r0:r1, :].astype(BF16)
            y = None
            for k in range(2):
                gate = jnp.dot(x, wg_s[k], preferred_element_type=F32)
                up = jnp.dot(x, wu_s[k], preferred_element_type=F32)
                act = (gate * jax.nn.sigmoid(gate) * up * cw[:, k:k + 1]).astype(BF16)
                yk = jnp.dot(act, wd_s[k], preferred_element_type=F32)
                y = yk if y is None else y + yk

            @pl.when(first_ref[i] == 1)
            def _():
                o_ref[r0:r1, :] = y
                for z0, z1 in ((0, r0), (r1, MOE_ROWS)):
                    if z1 > z0:
                        o_ref[z0:z1, :] = jnp.zeros((z1 - z0, o_ref.shape[1]), F32)

            @pl.when(first_ref[i] == 0)
            def _():
                o_ref[r0:r1, :] += y

        needs_lower, needs_upper = lo_row < half, hi_row > half
        pl.when(needs_lower & needs_upper)(lambda: run(0, MOE_ROWS))
        pl.when(needs_lower & jnp.logical_not(needs_upper))(lambda: run(0, half))
        pl.when(jnp.logical_not(needs_lower) & needs_upper)(lambda: run(half, MOE_ROWS))


def _moe_items(bucket, t):
    nbk = N_EXPERT_GROUPS * PAIRS_PER_GROUP
    order = jnp.argsort(bucket, stable=True).astype(jnp.int32)
    counts = jnp.sum((bucket[None, :] == jnp.arange(nbk, dtype=jnp.int32)[:, None]).astype(jnp.int32), axis=1)
    offs = jnp.concatenate([jnp.zeros((1,), jnp.int32), jnp.cumsum(counts).astype(jnp.int32)])
    nblk = t // MOE_ROWS
    n_items_max = nblk + nbk - 1
    lo = jnp.arange(nblk, dtype=jnp.int32)[:, None] * MOE_ROWS
    valid = ((offs[None, :-1] < lo + MOE_ROWS) & (offs[None, 1:] > lo) & (offs[None, 1:] > offs[None, :-1])).reshape(-1)
    flat = jnp.arange(nblk * nbk, dtype=jnp.int32)
    ranked = jnp.sort(jnp.where(valid, flat, nblk * nbk))[:n_items_max]
    n_items = jnp.sum(valid.astype(jnp.int32))
    ranked = ranked[jnp.minimum(jnp.arange(n_items_max, dtype=jnp.int32), n_items - 1)]
    item_rb, item_bk = ranked // nbk, ranked % nbk
    slots = jnp.asarray(np.array(PAIR_SLOTS, np.int32))
    base = (item_bk // PAIRS_PER_GROUP) * EXPERTS_PER_GROUP
    item_a, item_b = base + slots[item_bk % PAIRS_PER_GROUP, 0], base + slots[item_bk % PAIRS_PER_GROUP, 1]
    one = jnp.ones((1,), jnp.int32)
    changed = lambda a: jnp.concatenate([one, (a[1:] != a[:-1]).astype(jnp.int32)])
    tables = (item_rb, item_bk, item_a, item_b, changed(item_rb), changed(item_a), changed(item_b),
              n_items.reshape(1), offs)
    return order, tables, n_items_max


def _moe(xs, cws, w_gate, w_up, w_down, layer, tables, n_items_max):
    t, d = xs.shape
    ff = w_gate.shape[3]
    lo_map = lambda i, rb, bk, ea, eb, *_: (layer, ea[i], 0, 0)
    hi_map = lambda i, rb, bk, ea, eb, *_: (layer, eb[i], 0, 0)
    row_map = lambda i, rb, *_: (rb[i], 0)
    up_blk = lambda m: pl.BlockSpec((None, None, d, ff), m)
    down_blk = lambda m: pl.BlockSpec((None, None, ff, d), m)
    grid_spec = pltpu.PrefetchScalarGridSpec(
        num_scalar_prefetch=len(tables),
        grid=(n_items_max,),
        in_specs=[pl.BlockSpec((MOE_ROWS, d), row_map), pl.BlockSpec((MOE_ROWS, 2), row_map),
                  up_blk(lo_map), up_blk(hi_map), up_blk(lo_map), up_blk(hi_map), down_blk(lo_map), down_blk(hi_map)],
        out_specs=pl.BlockSpec((MOE_ROWS, d), row_map),
        scratch_shapes=[pltpu.VMEM((2, d, ff), BF16), pltpu.VMEM((2, d, ff), BF16), pltpu.VMEM((2, ff, d), BF16)],
    )
    return pl.pallas_call(
        _moe_kernel,
        grid_spec=grid_spec,
        out_shape=jax.ShapeDtypeStruct((t, d), F32),
        compiler_params=_params(("arbitrary",), VMEM_LIMIT),
        name="moe_grouped",
    )(*tables, xs, cws, w_gate, w_gate, w_up, w_up, w_down, w_down)


def _take_rows(a, idx):
    return a.at[idx].get(mode="promise_in_bounds", unique_indices=True)


def _final_kernel(*refs, ngroups, per_group):
    g_ref, o_ref = refs[2 * ngroups], refs[2 * ngroups + 1]
    for k in range(ngroups):
        @pl.when(pl.program_id(0) // per_group == k)
        def _():
            o_ref[...] = refs[2 * k][...] + g_ref[...] * refs[2 * k + 1][...]


def _final(groups, g2, seq):
    tg, d = groups[0][0].shape
    tm = min(TOKEN_TILE, seq)
    per_b = seq // tm
    per_group = tg // tm
    ngroups = len(groups)
    tok = lambda k: pl.BlockSpec((tm, d), lambda i: (jnp.clip(i - k * per_group, 0, per_group - 1), 0))
    ins, specs = [], []
    for k, (x1, y) in enumerate(groups):
        ins += [x1, y]
        specs += [tok(k), tok(k)]
    return pl.pallas_call(
        functools.partial(_final_kernel, ngroups=ngroups, per_group=per_group),
        grid=(ngroups * per_group,),
        in_specs=specs + [pl.BlockSpec((None, 1, d), lambda i: (i // per_b, 0, 0))],
        out_specs=pl.BlockSpec((tm, d), lambda i: (i, 0)),
        out_shape=jax.ShapeDtypeStruct((ngroups * tg, d), F32),
        compiler_params=_params(("parallel",)),
        name="final_residual",
    )(*ins, g2)


def kernel(x, c, positions, ada_w, ada_b, norm1_g, w_in, q_norm_g, k_norm_g, attn_sink, lam_re, lam_im, ssm_b_re, ssm_b_im, ssm_c_re, ssm_c_im, ssm_d, ssm_log_dt, w_glu, attn_out_g, ssm_out_g, w_out, norm2_g, w_router, router_bias, w_exp_gate, w_exp_up, w_exp_down):
    batch, seq, d = x.shape
    depth = ada_w.shape[0]
    t = batch * seq
    assert seq % ATTN_BLOCK == 0 and seq % SSM_CHUNK == 0 and batch % SEQ_GROUPS == 0
    assert (t // SEQ_GROUPS) % MOE_ROWS == 0

    mod = _adaln_mod(c, ada_w, ada_b).reshape(depth, 6, batch, 1, d)
    head_sum, rot = _rope_constants()
    bias = _attn_bias()
    w_router_t = w_router.T
    s5_mats = _s5_prep(lam_re, lam_im, ssm_b_re, ssm_b_im, ssm_c_re, ssm_c_im, ssm_d, ssm_log_dt)
    router_bias_col = router_bias.reshape(N_EXPERTS, 1)

    nb = batch // SEQ_GROUPS
    tg = nb * seq
    cos_sin = [_rope_tables(positions[k * nb:(k + 1) * nb]) for k in range(SEQ_GROUPS)]
    xs = [x[k * nb:(k + 1) * nb].reshape(tg, d) for k in range(SEQ_GROUPS)]
    res = [None] * SEQ_GROUPS
    for l in range(depth):
        qg = (jnp.tile(q_norm_g[l], N_Q_HEADS) * HEAD_DIM ** -0.5).reshape(1, ATTN_WIDTH)
        kg = jnp.tile(k_norm_g[l], N_KV_HEADS).reshape(1, KV_WIDTH)
        w_in_l, w_glu_l, w_out_l = w_in[l].astype(BF16), w_glu[l].astype(BF16), w_out[l].astype(BF16)
        for k in range(SEQ_GROUPS):
            sh1, sc1, g1, sh2, sc2, g2 = (mod[l, j, k * nb:(k + 1) * nb] for j in range(6))
            cos, sin = cos_sin[k]
            outs = _inproj(xs[k], res[k], sc1, sh1, norm1_g[l].reshape(1, d), w_in_l, qg, kg, head_sum, rot,
                           cos, sin, seq)
            if res[k] is None:
                q, kx, vx, uc = outs
            else:
                q, kx, vx, uc, xs[k] = outs
            attn = _attention(q, kx, vx, attn_sink[l], attn_out_g[l].reshape(1, ATTN_WIDTH), bias, nb, seq)
            yc = _s5_scan(uc, s5_mats, l, seq // SSM_CHUNK, nb)
            x1, h2, cw, gid = _post(xs[k], attn, yc, w_glu_l, ssm_out_g[l].reshape(1, SSM_WIDTH), w_out_l, g1,
                                    norm2_g[l].reshape(1, d), sc2, sh2, w_router_t, router_bias_col, seq)
            order, tables, n_items_max = _moe_items(gid.reshape(tg), tg)
            y_sorted = _moe(_take_rows(h2, order), _take_rows(cw.T, order), w_exp_gate, w_exp_up, w_exp_down, l,
                            tables, n_items_max)
            y = _take_rows(y_sorted, jnp.argsort(order).astype(jnp.int32))
            xs[k], res[k] = x1, (y, g2)
    g2_all = mod[depth - 1, 5]
    return _final([(xs[k], res[k][0]) for k in range(SEQ_GROUPS)], g2_all, seq).reshape(batch, seq, d)
```

```python
import functools

import numpy as np
import jax
import jax.numpy as jnp
from jax import lax
from jax.experimental import pallas as pl
from jax.experimental.pallas import tpu as pltpu

F32 = jnp.float32
BF16 = jnp.bfloat16

HEAD_DIM = 64
N_Q_HEADS = 8
N_KV_HEADS = 2
Q_PER_KV = N_Q_HEADS // N_KV_HEADS
ATTN_WIDTH = N_Q_HEADS * HEAD_DIM
KV_WIDTH = N_KV_HEADS * HEAD_DIM
ATTN_BLOCK = 128
ATTN_Q_TILE = 512
ROPE_THETA = 10000.0
ROPE_SLAB = 256
LANES = 128
SSM_GROUP_CH = 16
SSM_GROUPS = 32
SSM_WIDTH = SSM_GROUPS * SSM_GROUP_CH
SSM_STATE = 64
SSM_CHUNK = 16
SSM_COLS = SSM_WIDTH // LANES
COL_GROUPS = LANES // SSM_GROUP_CH
COL_STATE = COL_GROUPS * SSM_STATE
CHUNK_LANES = SSM_CHUNK * LANES
SSM_NSPLIT = 4
N_EXPERTS = 16
N_EXPERT_GROUPS = 4
EXPERTS_PER_GROUP = N_EXPERTS // N_EXPERT_GROUPS
PAIRS_PER_GROUP = EXPERTS_PER_GROUP * (EXPERTS_PER_GROUP - 1) // 2
PAIR_SLOTS = ((0, 1), (0, 2), (0, 3), (1, 3), (1, 2), (3, 2))
EPS = 1e-6
MASK_BIAS = -1e30

TOKEN_TILE = 512
POST_SPLIT = 1
MOE_ROWS = 256
VMEM_LIMIT = 48 * 1024 * 1024
MOE_VMEM_LIMIT = 56 * 1024 * 1024
STEP_PAD, STEP_LOAD, STEP_ITEM = 0, 1, 2


def _params(sem, vmem=None):
    return pltpu.CompilerParams(dimension_semantics=sem, vmem_limit_bytes=vmem)


def _rms(x, g):
    return x * lax.rsqrt(jnp.mean(x * x, axis=-1, keepdims=True) + EPS) * g


def _mod_kernel(c_ref, w_ref, b_ref, o_ref):
    c = c_ref[...]
    s = c * jax.nn.sigmoid(c)
    o_ref[...] = jnp.dot(s.astype(BF16), w_ref[...].astype(BF16), preferred_element_type=F32) + b_ref[...]


def _adaln_mod(c, ada_w, ada_b):
    depth, d, d6 = ada_w.shape
    nb = c.shape[0]
    n6 = d6 // d
    return pl.pallas_call(
        _mod_kernel,
        grid=(depth, n6),
        in_specs=[pl.BlockSpec((nb, d), lambda l, j: (0, 0)),
                  pl.BlockSpec((None, d, d), lambda l, j: (l, 0, j)),
                  pl.BlockSpec((None, None, 1, d), lambda l, j: (l, j, 0, 0))],
        out_specs=pl.BlockSpec((None, None, nb, d), lambda l, j: (l, j, 0, 0)),
        out_shape=jax.ShapeDtypeStruct((depth, n6, nb, d), F32),
        compiler_params=_params(("arbitrary", "arbitrary"), VMEM_LIMIT),
        name="adaln_mod",
    )(c, ada_w, ada_b.reshape(depth, n6, 1, d))


def _rope_kernel(pos_ref, freq_ref, cos_ref, sin_ref):
    ang = pos_ref[...].astype(F32) * freq_ref[...]
    cos_ref[...] = jnp.cos(ang)
    sin_ref[...] = jnp.sin(ang)


def _rope_tables(positions):
    half = HEAD_DIM // 2
    t = positions.size
    per_row = LANES // half
    rows = t // per_row
    pos_rep = jnp.repeat(positions.reshape(rows, per_row), half, axis=1)
    freq = (ROPE_THETA ** (-np.arange(half, dtype=np.float64) / half)).astype(np.float32)
    freq_row = jnp.asarray(np.tile(freq, per_row)[None, :])
    blk = min(rows, 512)
    cos, sin = pl.pallas_call(
        _rope_kernel,
        grid=(rows // blk,),
        in_specs=[pl.BlockSpec((blk, LANES), lambda i: (i, 0)),
                  pl.BlockSpec((1, LANES), lambda i: (0, 0))],
        out_specs=[pl.BlockSpec((blk, LANES), lambda i: (i, 0))] * 2,
        out_shape=[jax.ShapeDtypeStruct((rows, LANES), F32)] * 2,
        compiler_params=_params(("arbitrary",)),
        name="rope_tables",
    )(pos_rep, freq_row)
    widen = lambda a: jnp.tile(a.reshape(t, half), (1, per_row))
    return widen(cos), widen(sin)


def _rope_constants():
    lane = np.arange(ROPE_SLAB)
    head_sum = (lane[:, None] // HEAD_DIM == lane[None, :] // HEAD_DIM).astype(np.float32)
    half = HEAD_DIM // 2
    rot = np.zeros((ROPE_SLAB, ROPE_SLAB), np.float32)
    for d in range(ROPE_SLAB):
        if d % HEAD_DIM < half:
            rot[d + half, d] = -1.0
        else:
            rot[d - half, d] = 1.0
    return jnp.asarray(head_sum, BF16), jnp.asarray(rot, BF16)


def _inproj_kernel(*refs, has_res):
    if has_res:
        (x_ref, y_ref, g2_ref, sc_ref, sh_ref, n1_ref, w_ref, qg_ref, kg_ref, hs_ref, rot_ref,
         cos_ref, sin_ref, q_ref, k_ref, v_ref, uc_ref, xo_ref, u_scr) = refs
        x = x_ref[...] + g2_ref[...] * y_ref[...]
        xo_ref[...] = x
    else:
        (x_ref, sc_ref, sh_ref, n1_ref, w_ref, qg_ref, kg_ref, hs_ref, rot_ref,
         cos_ref, sin_ref, q_ref, k_ref, v_ref, uc_ref, u_scr) = refs
        x = x_ref[...]
    h = _rms(x, n1_ref[...]) * (1.0 + sc_ref[...]) + sh_ref[...]
    proj = jnp.dot(h.astype(BF16), w_ref[...], preferred_element_type=F32)
    q = proj[:, :ATTN_WIDTH]
    k = proj[:, ATTN_WIDTH:ATTN_WIDTH + KV_WIDTH]
    v = proj[:, ATTN_WIDTH + KV_WIDTH:ATTN_WIDTH + 2 * KV_WIDTH]
    cos = cos_ref[...]
    sin = sin_ref[...]
    reps = ATTN_WIDTH // LANES
    cos_q = jnp.concatenate([cos] * reps, axis=1)
    sin_q = jnp.concatenate([sin] * reps, axis=1)

    def head_norm_rope(t, gain, c, s):
        outs = []
        for lo in range(0, t.shape[1], ROPE_SLAB):
            wd = min(ROPE_SLAB, t.shape[1] - lo)
            ts, lanes = t[:, lo:lo + wd], slice(lo, lo + wd)
            ssq = jnp.dot((ts * ts).astype(BF16), hs_ref[:wd, :wd], preferred_element_type=F32)
            tn = (ts * lax.rsqrt(ssq * (1.0 / HEAD_DIM) + EPS) * gain[:, lanes]).astype(BF16)
            tr = jnp.dot(tn, rot_ref[:wd, :wd], preferred_element_type=F32)
            outs.append(tn.astype(F32) * c[:, lanes] + tr * s[:, lanes])
        return outs[0] if len(outs) == 1 else jnp.concatenate(outs, axis=1)

    qo = head_norm_rope(q, qg_ref[...], cos_q, sin_q)
    ko = head_norm_rope(k, kg_ref[...], cos, sin)
    q_ref[...] = qo.astype(BF16)
    k_ref[...] = jnp.concatenate([ko, pltpu.roll(ko, HEAD_DIM, axis=1)], axis=1).astype(BF16)
    v_ref[...] = jnp.concatenate([v, pltpu.roll(v, HEAD_DIM, axis=1)], axis=1).astype(BF16)
    u0 = ATTN_WIDTH + 2 * KV_WIDTH
    nchunk = u_scr.shape[1] // SSM_CHUNK
    for j in range(SSM_COLS):
        u_scr[j] = proj[:, u0 + j * LANES:u0 + (j + 1) * LANES]
    for s in range(SSM_CHUNK):
        for j in range(SSM_COLS):
            lanes = slice(s * SSM_WIDTH + j * LANES, s * SSM_WIDTH + (j + 1) * LANES)
            uc_ref[:, lanes] = u_scr[j, pl.ds(s, nchunk, stride=SSM_CHUNK), :].astype(BF16)


def _inproj(x, res, sc1, sh1, n1g, w_in, qg, kg, head_sum, rot, cos, sin, seq):
    t, d = x.shape
    tm = min(TOKEN_TILE, seq)
    per_b = seq // tm
    in_width = w_in.shape[1]
    tok = lambda w: pl.BlockSpec((tm, w), lambda i: (i, 0))
    const = lambda a: pl.BlockSpec(a.shape, lambda i: (0,) * a.ndim)
    per_batch = pl.BlockSpec((None, 1, d), lambda i: (i // per_b, 0, 0))
    chunked = pl.BlockSpec((tm // SSM_CHUNK, SSM_CHUNK * SSM_WIDTH), lambda i: (i, 0))
    ins, specs = [x], [tok(d)]
    if res is not None:
        y_prev, g2_prev = res
        ins += [y_prev, g2_prev]
        specs += [tok(d), per_batch]
    ins += [sc1, sh1, n1g, w_in, qg, kg, head_sum, rot, cos, sin]
    specs += [per_batch, per_batch, const(n1g), const(w_in), const(qg), const(kg), const(head_sum), const(rot),
              tok(LANES), tok(LANES)]
    out_shape = [jax.ShapeDtypeStruct((t, ATTN_WIDTH), BF16), jax.ShapeDtypeStruct((t, 2 * KV_WIDTH), BF16),
                 jax.ShapeDtypeStruct((t, 2 * KV_WIDTH), BF16),
                 jax.ShapeDtypeStruct((t // SSM_CHUNK, SSM_CHUNK * SSM_WIDTH), BF16)]
    out_specs = [tok(ATTN_WIDTH), tok(2 * KV_WIDTH), tok(2 * KV_WIDTH), chunked]
    if res is not None:
        out_shape.append(jax.ShapeDtypeStruct((t, d), F32))
        out_specs.append(tok(d))
    assert in_width == ATTN_WIDTH + 2 * KV_WIDTH + SSM_WIDTH
    return pl.pallas_call(
        functools.partial(_inproj_kernel, has_res=res is not None),
        grid=(t // tm,),
        in_specs=specs,
        out_specs=out_specs,
        out_shape=out_shape,
        scratch_shapes=[pltpu.VMEM((SSM_COLS, tm, LANES), F32)],
        compiler_params=_params(("parallel",), VMEM_LIMIT),
        name="inproj",
    )(*ins)


def _attn_kernel(sink_ref, q_ref, kc_ref, kp_ref, vc_ref, vp_ref, bias_ref, g_ref, o_ref):
    nsub = q_ref.shape[0] // ATTN_BLOCK
    kk = jnp.concatenate([kp_ref[...], kc_ref[...]], axis=0)
    vv = jnp.concatenate([vp_ref[...], vc_ref[...]], axis=0)
    low = lax.broadcasted_iota(jnp.int32, (kk.shape[0], KV_WIDTH), 1) < HEAD_DIM
    zero = jnp.zeros((kk.shape[0], KV_WIDTH), BF16)

    def variants(a):
        nat, swp = a[:, :KV_WIDTH], a[:, KV_WIDTH:]
        return {(0, 0): jnp.where(low, nat, zero), (0, 1): jnp.where(low, zero, swp),
                (1, 0): jnp.where(low, swp, zero), (1, 1): jnp.where(low, zero, nat)}

    kvar, vvar = variants(kk), variants(vv)
    band = bias_ref[1]
    first = bias_ref[jnp.minimum(pl.program_id(1), 1)]
    upper = lax.broadcasted_iota(jnp.int32, (2 * ATTN_BLOCK, 1), 0) < ATTN_BLOCK
    for j in range(nsub):
        bias = first if j == 0 else band
        bias2 = jnp.concatenate([bias, bias], axis=0)
        keys = slice(j * ATTN_BLOCK, (j + 2) * ATTN_BLOCK)
        qrows = slice(j * ATTN_BLOCK, (j + 1) * ATTN_BLOCK)
        tiles = [None] * (N_Q_HEADS // 2)
        for kv in range(N_KV_HEADS):
            for half in range(2):
                pairs = (2 * kv, 2 * kv + 1)
                heads = (2 * pairs[0] + half, 2 * pairs[1] + half)
                qs = jnp.concatenate([q_ref[qrows, p * LANES:(p + 1) * LANES] for p in pairs], axis=0)
                s = lax.dot_general(qs, kvar[(kv, half)][keys], (((1,), (1,)), ((), ())),
                                    preferred_element_type=F32) + bias2
                sink = jnp.where(upper, sink_ref[heads[0]], sink_ref[heads[1]])
                m = jnp.maximum(jnp.max(s, axis=-1, keepdims=True), sink)
                p = jnp.exp(s - m)
                denom = jnp.sum(p, axis=-1, keepdims=True) + jnp.exp(sink - m)
                o = jnp.dot(p.astype(BF16), vvar[(kv, half)][keys], preferred_element_type=F32) * (1.0 / denom)
                for r, pr in enumerate(pairs):
                    part = o[r * ATTN_BLOCK:(r + 1) * ATTN_BLOCK]
                    tiles[pr] = part if tiles[pr] is None else tiles[pr] + part
        a = jnp.concatenate(tiles, axis=1)
        o_ref[qrows, :] = _rms(a, g_ref[...]).astype(BF16)


def _attn_bias():
    qi = np.arange(ATTN_BLOCK)[:, None]
    sj = np.arange(2 * ATTN_BLOCK)[None, :]
    diff = qi + ATTN_BLOCK - sj
    band = (diff >= 0) & (diff < ATTN_BLOCK)
    first = band & (sj >= ATTN_BLOCK)
    return jnp.asarray(np.where(np.stack([first, band]), 0.0, MASK_BIAS).astype(np.float32))


def _attention(q, kx, vx, sink, out_g, bias, batch, seq):
    t = q.shape[0]
    qb = min(ATTN_Q_TILE, seq)
    nsub = qb // ATTN_BLOCK
    nq = seq // qb
    nb = seq // ATTN_BLOCK
    cur = lambda w: pl.BlockSpec((qb, w), lambda b, n, s: (b * nq + n, 0))
    prev = lambda w: pl.BlockSpec((ATTN_BLOCK, w), lambda b, n, s: (b * nb + jnp.maximum(n * nsub - 1, 0), 0))
    grid_spec = pltpu.PrefetchScalarGridSpec(
        num_scalar_prefetch=1,
        grid=(batch, nq),
        in_specs=[cur(ATTN_WIDTH), cur(2 * KV_WIDTH), prev(2 * KV_WIDTH), cur(2 * KV_WIDTH), prev(2 * KV_WIDTH),
                  pl.BlockSpec(bias.shape, lambda b, n, s: (0, 0, 0)),
                  pl.BlockSpec((1, ATTN_WIDTH), lambda b, n, s: (0, 0))],
        out_specs=cur(ATTN_WIDTH),
    )
    return pl.pallas_call(
        _attn_kernel,
        grid_spec=grid_spec,
        out_shape=jax.ShapeDtypeStruct((t, ATTN_WIDTH), BF16),
        compiler_params=_params(("parallel", "arbitrary")),
        name="swa_attention",
    )(sink, q, kx, kx, vx, vx, bias, out_g)


def _spread(x, expander3):
    hi = x.astype(BF16)
    r1 = x - hi.astype(F32)
    mid = r1.astype(BF16)
    lo = (r1 - mid.astype(F32)).astype(BF16)
    return jnp.dot(jnp.concatenate([hi, mid, lo], axis=1), expander3, preferred_element_type=F32)


def _s5_prep_kernel(lr_re_ref, lr_im_ref, ldt_ref, bt_re_ref, bt_im_ref, ct_re_ref, ct_im_ref,
                    d_ref, lcol_re_ref, lcol_im_ref, ldtcol_ref, exp_ref, exph_ref, expt_ref, expw_ref,
                    t_ref, w_ref, v_ref, la_ref, lb_ref):
    hi = lax.Precision.HIGHEST
    nl = SSM_CHUNK
    low = lax.broadcasted_iota(jnp.int32, (1, 2 * SSM_STATE), 1) < SSM_STATE
    row_low = lax.broadcasted_iota(jnp.int32, (2 * SSM_STATE, 1), 0) < SSM_STATE
    jcol = lax.broadcasted_iota(jnp.int32, (nl, 1), 0).astype(F32)
    kt_lane = lax.broadcasted_iota(jnp.int32, (SSM_GROUP_CH, nl * SSM_GROUP_CH), 1)
    kt_row = lax.broadcasted_iota(jnp.int32, (SSM_GROUP_CH, nl * SSM_GROUP_CH), 0)

    w_all, v_all, kt_all = [], [], []
    for gm in range(COL_GROUPS):
        dt = jnp.exp(ldt_ref[gm])
        lam_re, lam_im = lr_re_ref[gm], lr_im_ref[gm]
        a_r, th_r = lam_re * dt, lam_im * dt

        er = jnp.exp(jcol * a_r)
        pw_re, pw_im = er * jnp.cos(jcol * th_r), er * jnp.sin(jcol * th_r)

        nr, ni = pw_re[1:2, :] - 1.0, pw_im[1:2, :]
        den = lam_re * lam_re + lam_im * lam_im
        c_re, c_im = (nr * lam_re + ni * lam_im) / den, (ni * lam_re - nr * lam_im) / den
        bt_re, bt_im = bt_re_ref[gm], bt_im_ref[gm]
        bb_re, bb_im = c_re * bt_re - c_im * bt_im, c_re * bt_im + c_im * bt_re

        w_rows = []
        for s in range(nl):
            j = nl - 1 - s
            pr, pi = pw_re[j:j + 1, :], pw_im[j:j + 1, :]
            w_rows.append(jnp.where(low, pr * bb_re - pi * bb_im, pr * bb_im + pi * bb_re))
        w_all.append(w_rows)

        pw_re_t, pw_im_t = pw_re.T, pw_im.T
        pc, ps = _spread(pw_re_t, exp_ref[...]), _spread(pw_im_t, exp_ref[...])
        ct_re, ct_im = _spread(ct_re_ref[gm], exph_ref[...]), _spread(ct_im_ref[gm], exph_ref[...])
        a_re, a_im = ct_re * pc - ct_im * ps, ct_re * ps + ct_im * pc
        a_cat = jnp.where(row_low, a_re, -a_im)
        l1_re, l1_im = pw_re_t[:, 1:2], pw_im_t[:, 1:2]
        v_re, v_im = a_re * l1_re - a_im * l1_im, a_re * l1_im + a_im * l1_re
        v_all.append(jnp.where(row_low, v_re, -v_im))

        kt = jnp.dot(jnp.where(low, bb_re, bb_im), a_cat, precision=hi, preferred_element_type=F32)
        kt_all.append(kt + jnp.where(kt_lane == kt_row, d_ref[gm], 0.0))

    def expand(stacked, expander, row_group, lane_group):
        wide = jnp.dot(stacked.astype(BF16), expander, preferred_element_type=F32)
        r = lax.broadcasted_iota(jnp.int32, wide.shape, 0)
        c = lax.broadcasted_iota(jnp.int32, wide.shape, 1)
        return jnp.where(row_group(r) == lane_group(c), wide, 0.0).astype(BF16)

    chan_group = lambda i: (i >> 4) & (COL_GROUPS - 1)
    state_group = lambda i: (i >> 6) & (COL_GROUPS - 1)

    bd = expand(jnp.concatenate(kt_all, axis=0), expt_ref[...], chan_group, chan_group)
    t_ref[0:LANES, :] = bd
    for s in range(1, nl):
        t_ref[s * LANES:(s + 1) * LANES, :] = jnp.concatenate(
            [jnp.zeros((LANES, s * LANES), BF16), bd[:, :CHUNK_LANES - s * LANES]], axis=1)

    w_stack = jnp.concatenate([w_all[gm][s] for s in range(nl) for gm in range(COL_GROUPS)], axis=0)
    w_ref[...] = expand(w_stack, expw_ref[...], chan_group, state_group)

    v_stack = jnp.concatenate([v_all[gm][half * SSM_STATE:(half + 1) * SSM_STATE, :]
                               for half in range(2) for gm in range(COL_GROUPS)], axis=0)
    v_ref[...] = expand(v_stack, expt_ref[...], state_group, chan_group)

    dtc = jnp.exp(ldtcol_ref[...])
    e16 = jnp.exp(nl * lcol_re_ref[...] * dtc)
    ang = nl * lcol_im_ref[...] * dtc
    la_ref[...] = e16 * jnp.cos(ang)
    lb_ref[...] = e16 * jnp.sin(ang)


def _s5_prep(lam_re, lam_im, b_re, b_im, c_re, c_im, d_skip, log_dt):
    g, p, h, nl = SSM_GROUPS, SSM_STATE, SSM_GROUP_CH, SSM_CHUNK
    cg = COL_GROUPS
    nc = lam_re.shape[0] * SSM_COLS
    col = lambda a: a.reshape((nc, cg) + a.shape[2:])
    dup_row = lambda a: col(jnp.tile(a, (1, 1, 2))[:, :, None, :])
    bt = lambda a: col(jnp.tile(jnp.swapaxes(a, 2, 3), (1, 1, 1, 2)))
    ct = lambda a: col(jnp.tile(jnp.swapaxes(a, 2, 3), (1, 1, 2, 1)))
    d_pad = col(jnp.pad(d_skip.reshape(-1, g, 1, h), ((0, 0), (0, 0), (0, 0), (0, nl * h - h))))
    wide = lambda a: a.reshape(nc, 1, cg * p)
    expand = jnp.asarray(np.tile(np.repeat(np.eye(nl, dtype=np.float32), h, axis=1), (3, 1)), BF16)
    expand_h = jnp.asarray(np.tile(np.eye(h, dtype=np.float32), (3, nl)), BF16)
    exp_t = np.zeros((nl, h, nl, cg, h), np.float32)
    exp_w = np.zeros((2, p, 2, cg, p), np.float32)
    for gm in range(cg):
        exp_t[:, :, :, gm, :] = np.eye(nl * h, dtype=np.float32).reshape(nl, h, nl, h)
        exp_w[:, :, :, gm, :] = np.eye(2 * p, dtype=np.float32).reshape(2, p, 2, p)
    exp_t = jnp.asarray(exp_t.reshape(nl * h, CHUNK_LANES), BF16)
    exp_w = jnp.asarray(exp_w.reshape(2 * p, 2 * COL_STATE), BF16)
    blk = lambda *s: pl.BlockSpec((None,) + s, lambda i: (i,) + (0,) * len(s))
    const = lambda a: pl.BlockSpec(a.shape, lambda i: (0,) * a.ndim)
    lw = nl * h
    return pl.pallas_call(
        _s5_prep_kernel,
        grid=(nc,),
        in_specs=[blk(cg, 1, 2 * p), blk(cg, 1, 2 * p), blk(cg, 1, 1),
                  blk(cg, h, 2 * p), blk(cg, h, 2 * p), blk(cg, 2 * p, h), blk(cg, 2 * p, h), blk(cg, 1, lw),
                  blk(1, cg * p), blk(1, cg * p), blk(1, cg * p), const(expand), const(expand_h), const(exp_t),
                  const(exp_w)],
        out_specs=[blk(CHUNK_LANES, CHUNK_LANES), blk(CHUNK_LANES, 2 * COL_STATE), blk(2 * COL_STATE, CHUNK_LANES),
                   blk(1, COL_STATE), blk(1, COL_STATE)],
        out_shape=[jax.ShapeDtypeStruct((nc, CHUNK_LANES, CHUNK_LANES), BF16),
                   jax.ShapeDtypeStruct((nc, CHUNK_LANES, 2 * COL_STATE), BF16),
                   jax.ShapeDtypeStruct((nc, 2 * COL_STATE, CHUNK_LANES), BF16),
                   jax.ShapeDtypeStruct((nc, 1, COL_STATE), F32), jax.ShapeDtypeStruct((nc, 1, COL_STATE), F32)],
        compiler_params=_params(("parallel",), VMEM_LIMIT),
        name="s5_prep",
    )(dup_row(lam_re), dup_row(lam_im), col(log_dt[:, :, None, None]),
      bt(b_re), bt(b_im), ct(c_re), ct(c_im), d_pad, wide(lam_re), wide(lam_im),
      wide(jnp.repeat(log_dt, p, axis=1)), expand, expand_h, exp_t, exp_w)


def _s5_kernel(*refs, nchunks, nb):
    uc_refs = refs[:SSM_CHUNK]
    t_ref, w_ref, v_ref, la_ref, lb_ref, o_ref, ucat_ref, s_ref, xp_ref = refs[SSM_CHUNK:]

    @pl.when(pl.program_id(1) == 0)
    def _():
        for s in range(SSM_CHUNK):
            ucat_ref[:, s * LANES:(s + 1) * LANES] = uc_refs[s][...]
        s_in = jnp.dot(ucat_ref[...], w_ref[...], preferred_element_type=F32)
        nblk = COL_STATE // LANES
        for b in range(2 * nblk):
            s_ref[b] = s_in[:, b * LANES:(b + 1) * LANES]
        lr = [jnp.broadcast_to(la_ref[:, b * LANES:(b + 1) * LANES], (nb, LANES)) for b in range(nblk)]
        li = [jnp.broadcast_to(lb_ref[:, b * LANES:(b + 1) * LANES], (nb, LANES)) for b in range(nblk)]

        def step(c, carry):
            rows = pl.ds(c, nb, stride=nchunks)
            out = []
            for b in range(nblk):
                re, im = carry[2 * b], carry[2 * b + 1]
                xp_ref[b, rows, :] = re
                xp_ref[nblk + b, rows, :] = im
                out.append(lr[b] * re - li[b] * im + s_ref[b, rows, :])
                out.append(lr[b] * im + li[b] * re + s_ref[nblk + b, rows, :])
            return tuple(out)

        zero = jnp.zeros((nb, LANES), F32)
        lax.fori_loop(0, nchunks, step, (zero,) * (2 * nblk), unroll=4)

    xp = jnp.concatenate([xp_ref[b] for b in range(2 * COL_STATE // LANES)], axis=1).astype(BF16)
    inter = jnp.dot(xp, v_ref[...], preferred_element_type=F32)
    for kk in range(SSM_NSPLIT):
        @pl.when(pl.program_id(1) == kk)
        def _():
            live = (kk + 1) * (CHUNK_LANES // SSM_NSPLIT)
            intra = jnp.dot(ucat_ref[:, :live], t_ref[:live, :], preferred_element_type=F32)
            o_ref[...] = (intra + inter).astype(BF16)


def _s5_scan(uc, mats, layer, nchunks, nb):
    rows = uc.shape[0]
    c0 = layer * SSM_COLS
    split = CHUNK_LANES // SSM_NSPLIT
    u_spec = lambda s: pl.BlockSpec((rows, LANES), lambda j, k: (0, SSM_COLS * s + j))
    return pl.pallas_call(
        functools.partial(_s5_kernel, nchunks=nchunks, nb=nb),
        grid=(SSM_COLS, SSM_NSPLIT),
        in_specs=[u_spec(s) for s in range(SSM_CHUNK)] + [
            pl.BlockSpec((None, CHUNK_LANES, split), lambda j, k: (c0 + j, 0, k)),
            pl.BlockSpec((None, CHUNK_LANES, 2 * COL_STATE), lambda j, k: (c0 + j, 0, 0)),
            pl.BlockSpec((None, 2 * COL_STATE, split), lambda j, k: (c0 + j, 0, k)),
            pl.BlockSpec((None, 1, COL_STATE), lambda j, k: (c0 + j, 0, 0)),
            pl.BlockSpec((None, 1, COL_STATE), lambda j, k: (c0 + j, 0, 0))],
        out_specs=pl.BlockSpec((None, rows, split), lambda j, k: (j, 0, k)),
        out_shape=jax.ShapeDtypeStruct((SSM_COLS, rows, CHUNK_LANES), BF16),
        scratch_shapes=[pltpu.VMEM((rows, CHUNK_LANES), BF16),
                        pltpu.VMEM((2 * COL_STATE // LANES, rows, LANES), F32),
                        pltpu.VMEM((2 * COL_STATE // LANES, rows, LANES), F32)],
        compiler_params=_params(("parallel", "arbitrary"), VMEM_LIMIT),
        name="s5_scan",
    )(*([uc] * SSM_CHUNK), *mats)


def _route(logits, bias):
    m = jnp.max(logits, axis=0, keepdims=True)
    e = jnp.exp(logits - m)
    probs = e / jnp.sum(e, axis=0, keepdims=True)
    sel = probs + bias
    row = lambda a, i: a[i:i + 1, :]
    best_score, best = None, None
    for grp in range(N_EXPERT_GROUPS):
        a, b, c, d = (row(sel, EXPERTS_PER_GROUP * grp + i) for i in range(EXPERTS_PER_GROUP))
        hab, lab, hcd, lcd = jnp.maximum(a, b), jnp.minimum(a, b), jnp.maximum(c, d), jnp.minimum(c, d)
        top1 = jnp.maximum(hab, hcd)
        top2 = jnp.maximum(jnp.maximum(lab, lcd), jnp.minimum(hab, hcd))
        score = top1 + top2
        if grp == 0:
            best_score, best = score, jnp.zeros(score.shape, jnp.int32)
        else:
            better = score > best_score
            best = jnp.where(better, grp, best)
            best_score = jnp.where(better, score, best_score)

    def pick(a, i):
        out = row(a, i)
        for grp in range(1, N_EXPERT_GROUPS):
            out = jnp.where(best == grp, row(a, EXPERTS_PER_GROUP * grp + i), out)
        return out

    s_in = [pick(sel, i) for i in range(EXPERTS_PER_GROUP)]
    p_in = [pick(probs, i) for i in range(EXPERTS_PER_GROUP)]
    neg = jnp.full(s_in[0].shape, -jnp.inf, F32)

    def argmax_first(vals):
        idx, val = jnp.zeros(vals[0].shape, jnp.int32), vals[0]
        for i in range(1, len(vals)):
            better = vals[i] > val
            idx = jnp.where(better, i, idx)
            val = jnp.where(better, vals[i], val)
        return idx

    i1 = argmax_first(s_in)
    i2 = argmax_first([jnp.where(i1 == i, neg, s_in[i]) for i in range(EXPERTS_PER_GROUP)])
    zero = jnp.zeros(p_in[0].shape, F32)
    g1 = sum(jnp.where(i1 == i, p_in[i], zero) for i in range(EXPERTS_PER_GROUP))
    g2 = sum(jnp.where(i2 == i, p_in[i], zero) for i in range(EXPERTS_PER_GROUP))
    tot = g1 + g2
    w1, w2 = g1 / tot, g2 / tot
    first_low = i1 < i2
    low, high = jnp.minimum(i1, i2), jnp.maximum(i1, i2)
    w_low, w_high = jnp.where(first_low, w1, w2), jnp.where(first_low, w2, w1)
    pos = jnp.where(low == 0, high - 1, jnp.where(low == 1, jnp.where(high == 2, 4, 3), 5))
    swap = low == 2
    bucket = best * PAIRS_PER_GROUP + pos
    return jnp.concatenate([jnp.where(swap, w_high, w_low), jnp.where(swap, w_low, w_high)], axis=0), bucket


def _router_logits(w_t, h):
    w_hi = w_t.astype(BF16)
    w_r = w_t - w_hi.astype(F32)
    w_mid = w_r.astype(BF16)
    w_lo = (w_r - w_mid.astype(F32)).astype(BF16)
    h_hi = h.astype(BF16)
    h_lo = (h - h_hi.astype(F32)).astype(BF16)
    dims = (((1,), (1,)), ((), ()))
    a = lax.dot_general(jnp.concatenate([w_hi, w_mid, w_lo], axis=0), h_hi, dims, preferred_element_type=F32)
    b = lax.dot_general(jnp.concatenate([w_hi, w_mid], axis=0), h_lo, dims, preferred_element_type=F32)
    e = w_t.shape[0]
    return a[:e] + a[e:2 * e] + a[2 * e:] + b[:e] + b[e:]


def _post_kernel(x_ref, at_ref, yc_ref, wglu_ref, gs_ref, wo_ref, g1_ref, n2_ref, sc_ref, sh_ref,
                 wrt_ref, rb_ref, x1_ref, h2_ref, cw_ref, gid_ref, y_scr):
    nchunk = y_scr.shape[1] // SSM_CHUNK
    for s in range(SSM_CHUNK):
        for j in range(SSM_COLS):
            y_scr[j, pl.ds(s, nchunk, stride=SSM_CHUNK), :] = yc_ref[j, :, s * LANES:(s + 1) * LANES].astype(F32)
    tm = x_ref.shape[0]
    sub = tm // POST_SPLIT
    for part in range(POST_SPLIT):
        rows = slice(part * sub, (part + 1) * sub)
        yg = jax.nn.gelu(jnp.concatenate([y_scr[j, rows, :] for j in range(SSM_COLS)], axis=1))
        z = yg * jax.nn.sigmoid(jnp.dot(yg.astype(BF16), wglu_ref[...], preferred_element_type=F32))
        zn = _rms(z, gs_ref[...]).astype(BF16)
        o = (jnp.dot(at_ref[rows, :], wo_ref[:ATTN_WIDTH, :], preferred_element_type=F32)
             + jnp.dot(zn, wo_ref[ATTN_WIDTH:, :], preferred_element_type=F32))
        x1 = x_ref[rows, :] + g1_ref[...] * o
        x1_ref[rows, :] = x1
        h2 = _rms(x1, n2_ref[...]) * (1.0 + sc_ref[...]) + sh_ref[...]
        h2_ref[rows, :] = h2
        logits = _router_logits(wrt_ref[...], h2)
        cw, bucket = _route(logits, rb_ref[...])
        cw_ref[:, rows] = cw
        gid_ref[:, rows] = bucket


def _post(x, attn, yc, w_glu, ssm_g, w_out, g1, n2g, sc2, sh2, w_router_t, router_bias, seq):
    t, d = x.shape
    tm = min(TOKEN_TILE, seq)
    per_b = seq // tm
    tok = lambda w: pl.BlockSpec((tm, w), lambda i: (i, 0))
    const = lambda a: pl.BlockSpec(a.shape, lambda i: (0,) * a.ndim)
    per_batch = pl.BlockSpec((None, 1, d), lambda i: (i // per_b, 0, 0))
    col = lambda r: pl.BlockSpec((r, tm), lambda i: (0, i))
    chunked = pl.BlockSpec((SSM_COLS, tm // SSM_CHUNK, CHUNK_LANES), lambda i: (0, i, 0))
    return pl.pallas_call(
        _post_kernel,
        grid=(t // tm,),
        in_specs=[tok(d), tok(ATTN_WIDTH), chunked, const(w_glu), const(ssm_g),
                  const(w_out), per_batch, const(n2g), per_batch, per_batch, const(w_router_t), const(router_bias)],
        out_specs=[tok(d), tok(d), col(2), col(1)],
        out_shape=[jax.ShapeDtypeStruct((t, d), F32), jax.ShapeDtypeStruct((t, d), F32),
                   jax.ShapeDtypeStruct((2, t), F32), jax.ShapeDtypeStruct((1, t), jnp.int32)],
        scratch_shapes=[pltpu.VMEM((SSM_COLS, tm, LANES), F32)],
        compiler_params=_params(("parallel",), VMEM_LIMIT),
        name="post_mix",
    )(x, attn, yc, w_glu, ssm_g, w_out, g1, n2g, sc2, sh2, w_router_t, router_bias)


def _moe_kernel(kind_ref, rb_ref, bk_ref, pa_ref, pb_ref, first_ref, cast_ref, cpos_ref, pe_ref, offs_ref,
                x_ref, cw_ref, wg_ref, wu_ref, wd_ref, o_ref, wg_s, wu_s, wd_s):
    s = pl.program_id(0)

    @pl.when(cast_ref[s] == 1)
    def _():
        slot = cpos_ref[s]
        wg_s[slot] = wg_ref[...].astype(BF16)
        wu_s[slot] = wu_ref[...].astype(BF16)
        wd_s[slot] = wd_ref[...].astype(BF16)

    @pl.when(kind_ref[s] == STEP_ITEM)
    def _():
        bucket = bk_ref[s]
        base = rb_ref[s] * MOE_ROWS
        lo_row, hi_row = offs_ref[bucket] - base, offs_ref[bucket + 1] - base
        half = MOE_ROWS // 2
        slots = (pa_ref[s], pb_ref[s])

        def run(r0, r1):
            rows = r0 + lax.broadcasted_iota(jnp.int32, (r1 - r0, 1), 0)
            cw = jnp.where((rows >= lo_row) & (rows < hi_row), cw_ref[r0:r1, :], 0.0)
            x = x_ref[r0:r1, :].astype(BF16)
            y = None
            for k in range(2):
                gate = jnp.dot(x, wg_s[slots[k]], preferred_element_type=F32)
                up = jnp.dot(x, wu_s[slots[k]], preferred_element_type=F32)
                act = (gate * jax.nn.sigmoid(gate) * up * cw[:, k:k + 1]).astype(BF16)
                yk = jnp.dot(act, wd_s[slots[k]], preferred_element_type=F32)
                y = yk if y is None else y + yk

            @pl.when(first_ref[s] == 1)
            def _():
                o_ref[r0:r1, :] = y
                for z0, z1 in ((0, r0), (r1, MOE_ROWS)):
                    if z1 > z0:
                        o_ref[z0:z1, :] = jnp.zeros((z1 - z0, o_ref.shape[1]), F32)

            @pl.when(first_ref[s] == 0)
            def _():
                o_ref[r0:r1, :] += y

        needs_lower, needs_upper = lo_row < half, hi_row > half
        pl.when(needs_lower & needs_upper)(lambda: run(0, MOE_ROWS))
        pl.when(needs_lower & jnp.logical_not(needs_upper))(lambda: run(0, half))
        pl.when(jnp.logical_not(needs_lower) & needs_upper)(lambda: run(half, MOE_ROWS))


def _moe_steps(bucket, t):
    i32 = jnp.int32
    ng, epg, ppg = N_EXPERT_GROUPS, EXPERTS_PER_GROUP, PAIRS_PER_GROUP
    nbk = ng * ppg
    order = jnp.argsort(bucket, stable=True).astype(i32)
    counts = jnp.sum((bucket[None, :] == jnp.arange(nbk, dtype=i32)[:, None]).astype(i32), axis=1)
    offs = jnp.concatenate([jnp.zeros((1,), i32), jnp.cumsum(counts).astype(i32)])
    nblk = t // MOE_ROWS
    n_items_max = nblk + nbk - 1
    lo = jnp.arange(nblk, dtype=i32)[:, None] * MOE_ROWS
    valid = ((offs[None, :-1] < lo + MOE_ROWS) & (offs[None, 1:] > lo) & (offs[None, 1:] > offs[None, :-1])).reshape(-1)
    ranked = jnp.sort(jnp.where(valid, jnp.arange(nblk * nbk, dtype=i32), nblk * nbk))[:n_items_max]
    n_items = jnp.sum(valid.astype(i32))
    item_valid = jnp.arange(n_items_max, dtype=i32) < n_items
    ranked = ranked[jnp.minimum(jnp.arange(n_items_max, dtype=i32), n_items - 1)]
    item_rb, item_bk = ranked // nbk, ranked % nbk
    item_first = jnp.concatenate([jnp.ones((1,), i32), (item_rb[1:] != item_rb[:-1]).astype(i32)])

    grp = jnp.arange(ng, dtype=i32)
    m = jnp.sum((item_valid[None, :] & (item_bk[None, :] // ppg == grp[:, None])).astype(i32), axis=1)
    present = m > 0
    rank = jnp.cumsum(present.astype(i32)) - 1
    start_item = jnp.cumsum(m) - m
    prev_m = jnp.zeros((ng,), i32)
    has_prev = jnp.zeros((ng,), bool)
    nxt = jnp.full((ng,), -1, i32)
    for g in range(ng):
        for g2 in range(g):
            prev_m = prev_m.at[g].set(jnp.where(present[g2], m[g2], prev_m[g]))
            has_prev = has_prev.at[g].set(has_prev[g] | present[g2])
        for g2 in range(ng - 1, g, -1):
            nxt = nxt.at[g].set(jnp.where(present[g2], g2, nxt[g]))
    loaders = jnp.where(present, jnp.where(has_prev, jnp.maximum(epg - prev_m, 0), epg), 0)

    seg_len = jnp.stack([loaders, m], axis=1).reshape(-1)
    seg_end = jnp.cumsum(seg_len)
    seg_start = seg_end - seg_len
    n_steps_max = n_items_max + ng * epg
    step = jnp.arange(n_steps_max, dtype=i32)
    seg = jnp.sum((step[:, None] >= seg_end[None, :]).astype(i32), axis=1)
    live = seg < 2 * ng
    segc = jnp.minimum(seg, 2 * ng - 1)
    g = segc // 2
    within = step - seg_start[segc]
    is_loader = live & (segc % 2 == 0)
    is_item = live & (segc % 2 == 1)
    parity = rank[g] % 2

    local_load = epg - loaders[g] + within
    preload = is_item & (within < epg) & (nxt[g] >= 0)
    cast = is_loader | preload
    pe_def = jnp.where(is_loader, epg * g + local_load, epg * jnp.maximum(nxt[g], 0) + within)
    cpos = jnp.where(is_loader, parity * epg + local_load, (1 - parity) * epg + within)
    last_def = lax.cummax(jnp.where(cast, step, 0))
    pe = pe_def[last_def]

    item = jnp.clip(jnp.where(is_item, start_item[g] + within, jnp.where(is_loader, start_item[g], n_items - 1)),
                    0, n_items_max - 1)
    slots = jnp.asarray(np.array(PAIR_SLOTS, np.int32))
    pos_in_group = item_bk[item] % ppg
    kind = jnp.where(is_item, STEP_ITEM, jnp.where(is_loader, STEP_LOAD, STEP_PAD)).astype(i32)
    tables = (kind, item_rb[item], item_bk[item], parity * epg + slots[pos_in_group, 0],
              parity * epg + slots[pos_in_group, 1], item_first[item], cast.astype(i32), cpos.astype(i32),
              pe.astype(i32), offs)
    return order, tables, n_steps_max


def _moe(xs, cws, w_gate, w_up, w_down, layer, tables, n_steps_max):
    t, d = xs.shape
    ff = w_gate.shape[3]
    w_map = lambda s, kind, rb, bk, pa, pb, fi, ca, cp, pe, of: (layer, pe[s], 0, 0)
    row_map = lambda s, kind, rb, *_: (rb[s], 0)
    nres = 2 * EXPERTS_PER_GROUP
    grid_spec = pltpu.PrefetchScalarGridSpec(
        num_scalar_prefetch=len(tables),
        grid=(n_steps_max,),
        in_specs=[pl.BlockSpec((MOE_ROWS, d), row_map), pl.BlockSpec((MOE_ROWS, 2), row_map),
                  pl.BlockSpec((None, None, d, ff), w_map), pl.BlockSpec((None, None, d, ff), w_map),
                  pl.BlockSpec((None, None, ff, d), w_map)],
        out_specs=pl.BlockSpec((MOE_ROWS, d), row_map),
        scratch_shapes=[pltpu.VMEM((nres, d, ff), BF16), pltpu.VMEM((nres, d, ff), BF16),
                        pltpu.VMEM((nres, ff, d), BF16)],
    )
    return pl.pallas_call(
        _moe_kernel,
        grid_spec=grid_spec,
        out_shape=jax.ShapeDtypeStruct((t, d), F32),
        compiler_params=_params(("arbitrary",), MOE_VMEM_LIMIT),
        name="moe_grouped",
    )(*tables, xs, cws, w_gate, w_up, w_down)


def _take_rows(a, idx):
    return a.at[idx].get(mode="promise_in_bounds", unique_indices=True)


def _final_kernel(x_ref, y_ref, g_ref, o_ref):
    o_ref[...] = x_ref[...] + g_ref[...] * y_ref[...]


def _final(x1, y, g2, seq):
    t, d = x1.shape
    tm = min(TOKEN_TILE, seq)
    per_b = seq // tm
    tok = lambda w: pl.BlockSpec((tm, w), lambda i: (i, 0))
    return pl.pallas_call(
        _final_kernel,
        grid=(t // tm,),
        in_specs=[tok(d), tok(d), pl.BlockSpec((None, 1, d), lambda i: (i // per_b, 0, 0))],
        out_specs=tok(d),
        out_shape=jax.ShapeDtypeStruct((t, d), F32),
        compiler_params=_params(("parallel",)),
        name="final_residual",
    )(x1, y, g2)


def kernel(x, c, positions, ada_w, ada_b, norm1_g, w_in, q_norm_g, k_norm_g, attn_sink, lam_re, lam_im, ssm_b_re, ssm_b_im, ssm_c_re, ssm_c_im, ssm_d, ssm_log_dt, w_glu, attn_out_g, ssm_out_g, w_out, norm2_g, w_router, router_bias, w_exp_gate, w_exp_up, w_exp_down):
    batch, seq, d = x.shape
    depth = ada_w.shape[0]
    t = batch * seq
    assert seq % ATTN_BLOCK == 0 and seq % SSM_CHUNK == 0 and t % MOE_ROWS == 0

    mod = _adaln_mod(c, ada_w, ada_b).reshape(depth, 6, batch, 1, d)
    cos, sin = _rope_tables(positions)
    head_sum, rot = _rope_constants()
    bias = _attn_bias()
    w_router_t = w_router.T
    s5_mats = _s5_prep(lam_re, lam_im, ssm_b_re, ssm_b_im, ssm_c_re, ssm_c_im, ssm_d, ssm_log_dt)
    router_bias_col = router_bias.reshape(N_EXPERTS, 1)

    xf = x.reshape(t, d)
    res = None
    for l in range(depth):
        sh1, sc1, g1, sh2, sc2, g2 = (mod[l, j] for j in range(6))
        qg = (jnp.tile(q_norm_g[l], N_Q_HEADS) * HEAD_DIM ** -0.5).reshape(1, ATTN_WIDTH)
        kg = jnp.tile(k_norm_g[l], N_KV_HEADS).reshape(1, KV_WIDTH)
        outs = _inproj(xf, res, sc1, sh1, norm1_g[l].reshape(1, d), w_in[l].astype(BF16), qg, kg, head_sum, rot,
                       cos, sin, seq)
        if res is None:
            q, kx, vx, uc = outs
        else:
            q, kx, vx, uc, xf = outs
        attn = _attention(q, kx, vx, attn_sink[l], attn_out_g[l].reshape(1, ATTN_WIDTH), bias, batch, seq)
        yc = _s5_scan(uc, s5_mats, l, seq // SSM_CHUNK, batch)
        x1, h2, cw, gid = _post(xf, attn, yc, w_glu[l].astype(BF16), ssm_out_g[l].reshape(1, SSM_WIDTH),
                                 w_out[l].astype(BF16), g1, norm2_g[l].reshape(1, d), sc2, sh2, w_router_t,
                                 router_bias_col, seq)
        order, tables, n_steps_max = _moe_steps(gid.reshape(t), t)
        y_sorted = _moe(_take_rows(h2, order), _take_rows(cw.T, order), w_exp_gate, w_exp_up, w_exp_down, l,
                        tables, n_steps_max)
        y = _take_rows(y_sorted, jnp.argsort(order).astype(jnp.int32))
        xf, res = x1, (y, g2)
    y, g2 = res
    return _final(xf, y, g2, seq).reshape(batch, seq, d)
```

```python
import functools

import numpy as np
import jax
import jax.numpy as jnp
from jax import lax
from jax.experimental import pallas as pl
from jax.experimental.pallas import tpu as pltpu

F32 = jnp.float32
BF16 = jnp.bfloat16

HEAD_DIM = 64
N_Q_HEADS = 8
N_KV_HEADS = 2
Q_PER_KV = N_Q_HEADS // N_KV_HEADS
ATTN_WIDTH = N_Q_HEADS * HEAD_DIM
KV_WIDTH = N_KV_HEADS * HEAD_DIM
ATTN_BLOCK = 128
ATTN_Q_TILE = 512
ROPE_THETA = 10000.0
ROPE_SLAB = 256
LANES = 128
SSM_GROUP_CH = 16
SSM_GROUPS = 32
SSM_WIDTH = SSM_GROUPS * SSM_GROUP_CH
SSM_STATE = 64
SSM_CHUNK = 16
SSM_COLS = SSM_WIDTH // LANES
COL_GROUPS = LANES // SSM_GROUP_CH
COL_STATE = COL_GROUPS * SSM_STATE
CHUNK_LANES = SSM_CHUNK * LANES
SSM_NSPLIT = 4
N_EXPERTS = 16
N_EXPERT_GROUPS = 4
EXPERTS_PER_GROUP = N_EXPERTS // N_EXPERT_GROUPS
PAIRS_PER_GROUP = EXPERTS_PER_GROUP * (EXPERTS_PER_GROUP - 1) // 2
PAIR_SLOTS = ((0, 1), (0, 2), (0, 3), (1, 3), (1, 2), (3, 2))
EPS = 1e-6
MASK_BIAS = -1e30

TOKEN_TILE = 512
POST_SPLIT = 1
MOE_ROWS = 256
VMEM_LIMIT = 48 * 1024 * 1024
MOE_VMEM_LIMIT = 56 * 1024 * 1024
STEP_PAD, STEP_LOAD, STEP_ITEM = 0, 1, 2


def _params(sem, vmem=None):
    return pltpu.CompilerParams(dimension_semantics=sem, vmem_limit_bytes=vmem)


def _rms(x, g):
    return x * lax.rsqrt(jnp.mean(x * x, axis=-1, keepdims=True) + EPS) * g


def _mod_kernel(c_ref, w_ref, b_ref, o_ref):
    c = c_ref[...]
    s = c * jax.nn.sigmoid(c)
    o_ref[...] = jnp.dot(s.astype(BF16), w_ref[...].astype(BF16), preferred_element_type=F32) + b_ref[...]


def _adaln_mod(c, ada_w, ada_b):
    depth, d, d6 = ada_w.shape
    nb = c.shape[0]
    n6 = d6 // d
    return pl.pallas_call(
        _mod_kernel,
        grid=(depth, n6),
        in_specs=[pl.BlockSpec((nb, d), lambda l, j: (0, 0)),
                  pl.BlockSpec((None, d, d), lambda l, j: (l, 0, j)),
                  pl.BlockSpec((None, None, 1, d), lambda l, j: (l, j, 0, 0))],
        out_specs=pl.BlockSpec((None, None, nb, d), lambda l, j: (l, j, 0, 0)),
        out_shape=jax.ShapeDtypeStruct((depth, n6, nb, d), F32),
        compiler_params=_params(("arbitrary", "arbitrary"), VMEM_LIMIT),
        name="adaln_mod",
    )(c, ada_w, ada_b.reshape(depth, n6, 1, d))


def _rope_kernel(pos_ref, freq_ref, cos_ref, sin_ref):
    ang = pos_ref[...].astype(F32) * freq_ref[...]
    cos_ref[...] = jnp.cos(ang)
    sin_ref[...] = jnp.sin(ang)


def _rope_tables(positions):
    half = HEAD_DIM // 2
    t = positions.size
    per_row = LANES // half
    rows = t // per_row
    pos_rep = jnp.repeat(positions.reshape(rows, per_row), half, axis=1)
    freq = (ROPE_THETA ** (-np.arange(half, dtype=np.float64) / half)).astype(np.float32)
    freq_row = jnp.asarray(np.tile(freq, per_row)[None, :])
    blk = min(rows, 512)
    cos, sin = pl.pallas_call(
        _rope_kernel,
        grid=(rows // blk,),
        in_specs=[pl.BlockSpec((blk, LANES), lambda i: (i, 0)),
                  pl.BlockSpec((1, LANES), lambda i: (0, 0))],
        out_specs=[pl.BlockSpec((blk, LANES), lambda i: (i, 0))] * 2,
        out_shape=[jax.ShapeDtypeStruct((rows, LANES), F32)] * 2,
        compiler_params=_params(("arbitrary",)),
        name="rope_tables",
    )(pos_rep, freq_row)
    widen = lambda a: jnp.tile(a.reshape(t, half), (1, per_row))
    return widen(cos), widen(sin)


def _rope_constants():
    lane = np.arange(ROPE_SLAB)
    head_sum = (lane[:, None] // HEAD_DIM == lane[None, :] // HEAD_DIM).astype(np.float32)
    half = HEAD_DIM // 2
    rot = np.zeros((ROPE_SLAB, ROPE_SLAB), np.float32)
    for d in range(ROPE_SLAB):
        if d % HEAD_DIM < half:
            rot[d + half, d] = -1.0
        else:
            rot[d - half, d] = 1.0
    return jnp.asarray(head_sum, BF16), jnp.asarray(rot, BF16)


def _inproj_kernel(*refs, has_res):
    if has_res:
        (x_ref, y_ref, g2_ref, sc_ref, sh_ref, n1_ref, w_ref, qg_ref, kg_ref, hs_ref, rot_ref,
         cos_ref, sin_ref, q_ref, k_ref, v_ref, uc_ref, xo_ref, u_scr) = refs
        x = x_ref[...] + g2_ref[...] * y_ref[...]
        xo_ref[...] = x
    else:
        (x_ref, sc_ref, sh_ref, n1_ref, w_ref, qg_ref, kg_ref, hs_ref, rot_ref,
         cos_ref, sin_ref, q_ref, k_ref, v_ref, uc_ref, u_scr) = refs
        x = x_ref[...]
    h = _rms(x, n1_ref[...]) * (1.0 + sc_ref[...]) + sh_ref[...]
    proj = jnp.dot(h.astype(BF16), w_ref[...], preferred_element_type=F32)
    q = proj[:, :ATTN_WIDTH]
    k = proj[:, ATTN_WIDTH:ATTN_WIDTH + KV_WIDTH]
    v = proj[:, ATTN_WIDTH + KV_WIDTH:ATTN_WIDTH + 2 * KV_WIDTH]
    cos = cos_ref[...]
    sin = sin_ref[...]
    reps = ATTN_WIDTH // LANES
    cos_q = jnp.concatenate([cos] * reps, axis=1)
    sin_q = jnp.concatenate([sin] * reps, axis=1)

    def head_norm_rope(t, gain, c, s):
        outs = []
        for lo in range(0, t.shape[1], ROPE_SLAB):
            wd = min(ROPE_SLAB, t.shape[1] - lo)
            ts, lanes = t[:, lo:lo + wd], slice(lo, lo + wd)
            ssq = jnp.dot((ts * ts).astype(BF16), hs_ref[:wd, :wd], preferred_element_type=F32)
            tn = (ts * lax.rsqrt(ssq * (1.0 / HEAD_DIM) + EPS) * gain[:, lanes]).astype(BF16)
            tr = jnp.dot(tn, rot_ref[:wd, :wd], preferred_element_type=F32)
            outs.append(tn.astype(F32) * c[:, lanes] + tr * s[:, lanes])
        return outs[0] if len(outs) == 1 else jnp.concatenate(outs, axis=1)

    qo = head_norm_rope(q, qg_ref[...], cos_q, sin_q)
    ko = head_norm_rope(k, kg_ref[...], cos, sin)
    q_ref[...] = qo.astype(BF16)
    k_ref[...] = jnp.concatenate([ko, pltpu.roll(ko, HEAD_DIM, axis=1)], axis=1).astype(BF16)
    v_ref[...] = jnp.concatenate([v, pltpu.roll(v, HEAD_DIM, axis=1)], axis=1).astype(BF16)
    u0 = ATTN_WIDTH + 2 * KV_WIDTH
    nchunk = u_scr.shape[1] // SSM_CHUNK
    for j in range(SSM_COLS):
        u_scr[j] = proj[:, u0 + j * LANES:u0 + (j + 1) * LANES]
    for s in range(SSM_CHUNK):
        for j in range(SSM_COLS):
            lanes = slice(s * SSM_WIDTH + j * LANES, s * SSM_WIDTH + (j + 1) * LANES)
            uc_ref[:, lanes] = u_scr[j, pl.ds(s, nchunk, stride=SSM_CHUNK), :].astype(BF16)


def _inproj(x, res, sc1, sh1, n1g, w_in, qg, kg, head_sum, rot, cos, sin, seq):
    t, d = x.shape
    tm = min(TOKEN_TILE, seq)
    per_b = seq // tm
    in_width = w_in.shape[1]
    tok = lambda w: pl.BlockSpec((tm, w), lambda i: (i, 0))
    const = lambda a: pl.BlockSpec(a.shape, lambda i: (0,) * a.ndim)
    per_batch = pl.BlockSpec((None, 1, d), lambda i: (i // per_b, 0, 0))
    chunked = pl.BlockSpec((tm // SSM_CHUNK, SSM_CHUNK * SSM_WIDTH), lambda i: (i, 0))
    ins, specs = [x], [tok(d)]
    if res is not None:
        y_prev, g2_prev = res
        ins += [y_prev, g2_prev]
        specs += [tok(d), per_batch]
    ins += [sc1, sh1, n1g, w_in, qg, kg, head_sum, rot, cos, sin]
    specs += [per_batch, per_batch, const(n1g), const(w_in), const(qg), const(kg), const(head_sum), const(rot),
              tok(LANES), tok(LANES)]
    out_shape = [jax.ShapeDtypeStruct((t, ATTN_WIDTH), BF16), jax.ShapeDtypeStruct((t, 2 * KV_WIDTH), BF16),
                 jax.ShapeDtypeStruct((t, 2 * KV_WIDTH), BF16),
                 jax.ShapeDtypeStruct((t // SSM_CHUNK, SSM_CHUNK * SSM_WIDTH), BF16)]
    out_specs = [tok(ATTN_WIDTH), tok(2 * KV_WIDTH), tok(2 * KV_WIDTH), chunked]
    if res is not None:
        out_shape.append(jax.ShapeDtypeStruct((t, d), F32))
        out_specs.append(tok(d))
    assert in_width == ATTN_WIDTH + 2 * KV_WIDTH + SSM_WIDTH
    return pl.pallas_call(
        functools.partial(_inproj_kernel, has_res=res is not None),
        grid=(t // tm,),
        in_specs=specs,
        out_specs=out_specs,
        out_shape=out_shape,
        scratch_shapes=[pltpu.VMEM((SSM_COLS, tm, LANES), F32)],
        compiler_params=_params(("parallel",), VMEM_LIMIT),
        name="inproj",
    )(*ins)


def _attn_kernel(sink_ref, q_ref, kc_ref, kp_ref, vc_ref, vp_ref, bias_ref, g_ref, o_ref):
    nsub = q_ref.shape[0] // ATTN_BLOCK
    kk = jnp.concatenate([kp_ref[...], kc_ref[...]], axis=0)
    vv = jnp.concatenate([vp_ref[...], vc_ref[...]], axis=0)
    low = lax.broadcasted_iota(jnp.int32, (kk.shape[0], KV_WIDTH), 1) < HEAD_DIM
    zero = jnp.zeros((kk.shape[0], KV_WIDTH), BF16)

    def variants(a):
        nat, swp = a[:, :KV_WIDTH], a[:, KV_WIDTH:]
        return {(0, 0): jnp.where(low, nat, zero), (0, 1): jnp.where(low, zero, swp),
                (1, 0): jnp.where(low, swp, zero), (1, 1): jnp.where(low, zero, nat)}

    kvar, vvar = variants(kk), variants(vv)
    band = bias_ref[1]
    first = bias_ref[jnp.minimum(pl.program_id(1), 1)]
    upper = lax.broadcasted_iota(jnp.int32, (2 * ATTN_BLOCK, 1), 0) < ATTN_BLOCK
    for j in range(nsub):
        bias = first if j == 0 else band
        bias2 = jnp.concatenate([bias, bias], axis=0)
        keys = slice(j * ATTN_BLOCK, (j + 2) * ATTN_BLOCK)
        qrows = slice(j * ATTN_BLOCK, (j + 1) * ATTN_BLOCK)
        tiles = [None] * (N_Q_HEADS // 2)
        for kv in range(N_KV_HEADS):
            for half in range(2):
                pairs = (2 * kv, 2 * kv + 1)
                heads = (2 * pairs[0] + half, 2 * pairs[1] + half)
                qs = jnp.concatenate([q_ref[qrows, p * LANES:(p + 1) * LANES] for p in pairs], axis=0)
                s = lax.dot_general(qs, kvar[(kv, half)][keys], (((1,), (1,)), ((), ())),
                                    preferred_element_type=F32) + bias2
                sink = jnp.where(upper, sink_ref[heads[0]], sink_ref[heads[1]])
                m = jnp.maximum(jnp.max(s, axis=-1, keepdims=True), sink)
                p = jnp.exp(s - m)
                denom = jnp.sum(p, axis=-1, keepdims=True) + jnp.exp(sink - m)
                o = jnp.dot(p.astype(BF16), vvar[(kv, half)][keys], preferred_element_type=F32) * (1.0 / denom)
                for r, pr in enumerate(pairs):
                    part = o[r * ATTN_BLOCK:(r + 1) * ATTN_BLOCK]
                    tiles[pr] = part if tiles[pr] is None else tiles[pr] + part
        a = jnp.concatenate(tiles, axis=1)
        o_ref[qrows, :] = _rms(a, g_ref[...]).astype(BF16)


def _attn_bias():
    qi = np.arange(ATTN_BLOCK)[:, None]
    sj = np.arange(2 * ATTN_BLOCK)[None, :]
    diff = qi + ATTN_BLOCK - sj
    band = (diff >= 0) & (diff < ATTN_BLOCK)
    first = band & (sj >= ATTN_BLOCK)
    return jnp.asarray(np.where(np.stack([first, band]), 0.0, MASK_BIAS).astype(np.float32))


def _attention(q, kx, vx, sink, out_g, bias, batch, seq):
    t = q.shape[0]
    qb = min(ATTN_Q_TILE, seq)
    nsub = qb // ATTN_BLOCK
    nq = seq // qb
    nb = seq // ATTN_BLOCK
    cur = lambda w: pl.BlockSpec((qb, w), lambda b, n, s: (b * nq + n, 0))
    prev = lambda w: pl.BlockSpec((ATTN_BLOCK, w), lambda b, n, s: (b * nb + jnp.maximum(n * nsub - 1, 0), 0))
    grid_spec = pltpu.PrefetchScalarGridSpec(
        num_scalar_prefetch=1,
        grid=(batch, nq),
        in_specs=[cur(ATTN_WIDTH), cur(2 * KV_WIDTH), prev(2 * KV_WIDTH), cur(2 * KV_WIDTH), prev(2 * KV_WIDTH),
                  pl.BlockSpec(bias.shape, lambda b, n, s: (0, 0, 0)),
                  pl.BlockSpec((1, ATTN_WIDTH), lambda b, n, s: (0, 0))],
        out_specs=cur(ATTN_WIDTH),
    )
    return pl.pallas_call(
        _attn_kernel,
        grid_spec=grid_spec,
        out_shape=jax.ShapeDtypeStruct((t, ATTN_WIDTH), BF16),
        compiler_params=_params(("parallel", "arbitrary")),
        name="swa_attention",
    )(sink, q, kx, kx, vx, vx, bias, out_g)


def _spread(x, expander3):
    hi = x.astype(BF16)
    r1 = x - hi.astype(F32)
    mid = r1.astype(BF16)
    lo = (r1 - mid.astype(F32)).astype(BF16)
    return jnp.dot(jnp.concatenate([hi, mid, lo], axis=1), expander3, preferred_element_type=F32)


def _s5_prep_kernel(lr_re_ref, lr_im_ref, ldt_ref, bt_re_ref, bt_im_ref, ct_re_ref, ct_im_ref,
                    d_ref, lcol_re_ref, lcol_im_ref, ldtcol_ref, exp_ref, exph_ref, expt_ref, expw_ref,
                    t_ref, w_ref, v_ref, la_ref, lb_ref):
    hi = lax.Precision.HIGHEST
    nl = SSM_CHUNK
    low = lax.broadcasted_iota(jnp.int32, (1, 2 * SSM_STATE), 1) < SSM_STATE
    row_low = lax.broadcasted_iota(jnp.int32, (2 * SSM_STATE, 1), 0) < SSM_STATE
    jcol = lax.broadcasted_iota(jnp.int32, (nl, 1), 0).astype(F32)
    kt_lane = lax.broadcasted_iota(jnp.int32, (SSM_GROUP_CH, nl * SSM_GROUP_CH), 1)
    kt_row = lax.broadcasted_iota(jnp.int32, (SSM_GROUP_CH, nl * SSM_GROUP_CH), 0)

    w_all, v_all, kt_all = [], [], []
    for gm in range(COL_GROUPS):
        dt = jnp.exp(ldt_ref[gm])
        lam_re, lam_im = lr_re_ref[gm], lr_im_ref[gm]
        a_r, th_r = lam_re * dt, lam_im * dt

        er = jnp.exp(jcol * a_r)
        pw_re, pw_im = er * jnp.cos(jcol * th_r), er * jnp.sin(jcol * th_r)

        nr, ni = pw_re[1:2, :] - 1.0, pw_im[1:2, :]
        den = lam_re * lam_re + lam_im * lam_im
        c_re, c_im = (nr * lam_re + ni * lam_im) / den, (ni * lam_re - nr * lam_im) / den
        bt_re, bt_im = bt_re_ref[gm], bt_im_ref[gm]
        bb_re, bb_im = c_re * bt_re - c_im * bt_im, c_re * bt_im + c_im * bt_re

        w_rows = []
        for s in range(nl):
            j = nl - 1 - s
            pr, pi = pw_re[j:j + 1, :], pw_im[j:j + 1, :]
            w_rows.append(jnp.where(low, pr * bb_re - pi * bb_im, pr * bb_im + pi * bb_re))
        w_all.append(w_rows)

        pw_re_t, pw_im_t = pw_re.T, pw_im.T
        pc, ps = _spread(pw_re_t, exp_ref[...]), _spread(pw_im_t, exp_ref[...])
        ct_re, ct_im = _spread(ct_re_ref[gm], exph_ref[...]), _spread(ct_im_ref[gm], exph_ref[...])
        a_re, a_im = ct_re * pc - ct_im * ps, ct_re * ps + ct_im * pc
        a_cat = jnp.where(row_low, a_re, -a_im)
        l1_re, l1_im = pw_re_t[:, 1:2], pw_im_t[:, 1:2]
        v_re, v_im = a_re * l1_re - a_im * l1_im, a_re * l1_im + a_im * l1_re
        v_all.append(jnp.where(row_low, v_re, -v_im))

        kt = jnp.dot(jnp.where(low, bb_re, bb_im), a_cat, precision=hi, preferred_element_type=F32)
        kt_all.append(kt + jnp.where(kt_lane == kt_row, d_ref[gm], 0.0))

    def expand(stacked, expander, row_group, lane_group):
        wide = jnp.dot(stacked.astype(BF16), expander, preferred_element_type=F32)
        r = lax.broadcasted_iota(jnp.int32, wide.shape, 0)
        c = lax.broadcasted_iota(jnp.int32, wide.shape, 1)
        return jnp.where(row_group(r) == lane_group(c), wide, 0.0).astype(BF16)

    chan_group = lambda i: (i >> 4) & (COL_GROUPS - 1)
    state_group = lambda i: (i >> 6) & (COL_GROUPS - 1)

    bd = expand(jnp.concatenate(kt_all, axis=0), expt_ref[...], chan_group, chan_group)
    t_ref[0:LANES, :] = bd
    for s in range(1, nl):
        t_ref[s * LANES:(s + 1) * LANES, :] = jnp.concatenate(
            [jnp.zeros((LANES, s * LANES), BF16), bd[:, :CHUNK_LANES - s * LANES]], axis=1)

    w_stack = jnp.concatenate([w_all[gm][s] for s in range(nl) for gm in range(COL_GROUPS)], axis=0)
    w_ref[...] = expand(w_stack, expw_ref[...], chan_group, state_group)

    v_stack = jnp.concatenate([v_all[gm][half * SSM_STATE:(half + 1) * SSM_STATE, :]
                               for half in range(2) for gm in range(COL_GROUPS)], axis=0)
    v_ref[...] = expand(v_stack, expt_ref[...], state_group, chan_group)

    dtc = jnp.exp(ldtcol_ref[...])
    e16 = jnp.exp(nl * lcol_re_ref[...] * dtc)
    ang = nl * lcol_im_ref[...] * dtc
    la_ref[...] = e16 * jnp.cos(ang)
    lb_ref[...] = e16 * jnp.sin(ang)


def _s5_prep(lam_re, lam_im, b_re, b_im, c_re, c_im, d_skip, log_dt):
    g, p, h, nl = SSM_GROUPS, SSM_STATE, SSM_GROUP_CH, SSM_CHUNK
    cg = COL_GROUPS
    nc = lam_re.shape[0] * SSM_COLS
    col = lambda a: a.reshape((nc, cg) + a.shape[2:])
    dup_row = lambda a: col(jnp.tile(a, (1, 1, 2))[:, :, None, :])
    bt = lambda a: col(jnp.tile(jnp.swapaxes(a, 2, 3), (1, 1, 1, 2)))
    ct = lambda a: col(jnp.tile(jnp.swapaxes(a, 2, 3), (1, 1, 2, 1)))
    d_pad = col(jnp.pad(d_skip.reshape(-1, g, 1, h), ((0, 0), (0, 0), (0, 0), (0, nl * h - h))))
    wide = lambda a: a.reshape(nc, 1, cg * p)
    expand = jnp.asarray(np.tile(np.repeat(np.eye(nl, dtype=np.float32), h, axis=1), (3, 1)), BF16)
    expand_h = jnp.asarray(np.tile(np.eye(h, dtype=np.float32), (3, nl)), BF16)
    exp_t = np.zeros((nl, h, nl, cg, h), np.float32)
    exp_w = np.zeros((2, p, 2, cg, p), np.float32)
    for gm in range(cg):
        exp_t[:, :, :, gm, :] = np.eye(nl * h, dtype=np.float32).reshape(nl, h, nl, h)
        exp_w[:, :, :, gm, :] = np.eye(2 * p, dtype=np.float32).reshape(2, p, 2, p)
    exp_t = jnp.asarray(exp_t.reshape(nl * h, CHUNK_LANES), BF16)
    exp_w = jnp.asarray(exp_w.reshape(2 * p, 2 * COL_STATE), BF16)
    blk = lambda *s: pl.BlockSpec((None,) + s, lambda i: (i,) + (0,) * len(s))
    const = lambda a: pl.BlockSpec(a.shape, lambda i: (0,) * a.ndim)
    lw = nl * h
    return pl.pallas_call(
        _s5_prep_kernel,
        grid=(nc,),
        in_specs=[blk(cg, 1, 2 * p), blk(cg, 1, 2 * p), blk(cg, 1, 1),
                  blk(cg, h, 2 * p), blk(cg, h, 2 * p), blk(cg, 2 * p, h), blk(cg, 2 * p, h), blk(cg, 1, lw),
                  blk(1, cg * p), blk(1, cg * p), blk(1, cg * p), const(expand), const(expand_h), const(exp_t),
                  const(exp_w)],
        out_specs=[blk(CHUNK_LANES, CHUNK_LANES), blk(CHUNK_LANES, 2 * COL_STATE), blk(2 * COL_STATE, CHUNK_LANES),
                   blk(1, COL_STATE), blk(1, COL_STATE)],
        out_shape=[jax.ShapeDtypeStruct((nc, CHUNK_LANES, CHUNK_LANES), BF16),
                   jax.ShapeDtypeStruct((nc, CHUNK_LANES, 2 * COL_STATE), BF16),
                   jax.ShapeDtypeStruct((nc, 2 * COL_STATE, CHUNK_LANES), BF16),
                   jax.ShapeDtypeStruct((nc, 1, COL_STATE), F32), jax.ShapeDtypeStruct((nc, 1, COL_STATE), F32)],
        compiler_params=_params(("parallel",), VMEM_LIMIT),
        name="s5_prep",
    )(dup_row(lam_re), dup_row(lam_im), col(log_dt[:, :, None, None]),
      bt(b_re), bt(b_im), ct(c_re), ct(c_im), d_pad, wide(lam_re), wide(lam_im),
      wide(jnp.repeat(log_dt, p, axis=1)), expand, expand_h, exp_t, exp_w)


def _s5_kernel(*refs, nchunks, nb):
    uc_refs = refs[:SSM_CHUNK]
    t_ref, w_ref, v_ref, la_ref, lb_ref, o_ref, ucat_ref, s_ref, xp_ref = refs[SSM_CHUNK:]

    @pl.when(pl.program_id(1) == 0)
    def _():
        for s in range(SSM_CHUNK):
            ucat_ref[:, s * LANES:(s + 1) * LANES] = uc_refs[s][...]
        s_in = jnp.dot(ucat_ref[...], w_ref[...], preferred_element_type=F32)
        nblk = COL_STATE // LANES
        for b in range(2 * nblk):
            s_ref[b] = s_in[:, b * LANES:(b + 1) * LANES]
        lr = [jnp.broadcast_to(la_ref[:, b * LANES:(b + 1) * LANES], (nb, LANES)) for b in range(nblk)]
        li = [jnp.broadcast_to(lb_ref[:, b * LANES:(b + 1) * LANES], (nb, LANES)) for b in range(nblk)]

        def step(c, carry):
            rows = pl.ds(c, nb, stride=nchunks)
            out = []
            for b in range(nblk):
                re, im = carry[2 * b], carry[2 * b + 1]
                xp_ref[b, rows, :] = re
                xp_ref[nblk + b, rows, :] = im
                out.append(lr[b] * re - li[b] * im + s_ref[b, rows, :])
                out.append(lr[b] * im + li[b] * re + s_ref[nblk + b, rows, :])
            return tuple(out)

        zero = jnp.zeros((nb, LANES), F32)
        lax.fori_loop(0, nchunks, step, (zero,) * (2 * nblk), unroll=4)

    xp = jnp.concatenate([xp_ref[b] for b in range(2 * COL_STATE // LANES)], axis=1).astype(BF16)
    inter = jnp.dot(xp, v_ref[...], preferred_element_type=F32)
    for kk in range(SSM_NSPLIT):
        @pl.when(pl.program_id(1) == kk)
        def _():
            live = (kk + 1) * (CHUNK_LANES // SSM_NSPLIT)
            intra = jnp.dot(ucat_ref[:, :live], t_ref[:live, :], preferred_element_type=F32)
            o_ref[...] = (intra + inter).astype(BF16)


def _s5_scan(uc, mats, layer, nchunks, nb):
    rows = uc.shape[0]
    c0 = layer * SSM_COLS
    split = CHUNK_LANES // SSM_NSPLIT
    u_spec = lambda s: pl.BlockSpec((rows, LANES), lambda j, k: (0, SSM_COLS * s + j))
    return pl.pallas_call(
        functools.partial(_s5_kernel, nchunks=nchunks, nb=nb),
        grid=(SSM_COLS, SSM_NSPLIT),
        in_specs=[u_spec(s) for s in range(SSM_CHUNK)] + [
            pl.BlockSpec((None, CHUNK_LANES, split), lambda j, k: (c0 + j, 0, k)),
            pl.BlockSpec((None, CHUNK_LANES, 2 * COL_STATE), lambda j, k: (c0 + j, 0, 0)),
            pl.BlockSpec((None, 2 * COL_STATE, split), lambda j, k: (c0 + j, 0, k)),
            pl.BlockSpec((None, 1, COL_STATE), lambda j, k: (c0 + j, 0, 0)),
            pl.BlockSpec((None, 1, COL_STATE), lambda j, k: (c0 + j, 0, 0))],
        out_specs=pl.BlockSpec((None, rows, split), lambda j, k: (j, 0, k)),
        out_shape=jax.ShapeDtypeStruct((SSM_COLS, rows, CHUNK_LANES), BF16),
        scratch_shapes=[pltpu.VMEM((rows, CHUNK_LANES), BF16),
                        pltpu.VMEM((2 * COL_STATE // LANES, rows, LANES), F32),
                        pltpu.VMEM((2 * COL_STATE // LANES, rows, LANES), F32)],
        compiler_params=_params(("parallel", "arbitrary"), VMEM_LIMIT),
        name="s5_scan",
    )(*([uc] * SSM_CHUNK), *mats)


def _route(logits, bias):
    m = jnp.max(logits, axis=0, keepdims=True)
    e = jnp.exp(logits - m)
    probs = e / jnp.sum(e, axis=0, keepdims=True)
    sel = probs + bias
    row = lambda a, i: a[i:i + 1, :]
    best_score, best = None, None
    for grp in range(N_EXPERT_GROUPS):
        a, b, c, d = (row(sel, EXPERTS_PER_GROUP * grp + i) for i in range(EXPERTS_PER_GROUP))
        hab, lab, hcd, lcd = jnp.maximum(a, b), jnp.minimum(a, b), jnp.maximum(c, d), jnp.minimum(c, d)
        top1 = jnp.maximum(hab, hcd)
        top2 = jnp.maximum(jnp.maximum(lab, lcd), jnp.minimum(hab, hcd))
        score = top1 + top2
        if grp == 0:
            best_score, best = score, jnp.zeros(score.shape, jnp.int32)
        else:
            better = score > best_score
            best = jnp.where(better, grp, best)
            best_score = jnp.where(better, score, best_score)

    def pick(a, i):
        out = row(a, i)
        for grp in range(1, N_EXPERT_GROUPS):
            out = jnp.where(best == grp, row(a, EXPERTS_PER_GROUP * grp + i), out)
        return out

    s_in = [pick(sel, i) for i in range(EXPERTS_PER_GROUP)]
    p_in = [pick(probs, i) for i in range(EXPERTS_PER_GROUP)]
    neg = jnp.full(s_in[0].shape, -jnp.inf, F32)

    def argmax_first(vals):
        idx, val = jnp.zeros(vals[0].shape, jnp.int32), vals[0]
        for i in range(1, len(vals)):
            better = vals[i] > val
            idx = jnp.where(better, i, idx)
            val = jnp.where(better, vals[i], val)
        return idx

    i1 = argmax_first(s_in)
    i2 = argmax_first([jnp.where(i1 == i, neg, s_in[i]) for i in range(EXPERTS_PER_GROUP)])
    zero = jnp.zeros(p_in[0].shape, F32)
    g1 = sum(jnp.where(i1 == i, p_in[i], zero) for i in range(EXPERTS_PER_GROUP))
    g2 = sum(jnp.where(i2 == i, p_in[i], zero) for i in range(EXPERTS_PER_GROUP))
    tot = g1 + g2
    w1, w2 = g1 / tot, g2 / tot
    first_low = i1 < i2
    low, high = jnp.minimum(i1, i2), jnp.maximum(i1, i2)
    w_low, w_high = jnp.where(first_low, w1, w2), jnp.where(first_low, w2, w1)
    pos = jnp.where(low == 0, high - 1, jnp.where(low == 1, jnp.where(high == 2, 4, 3), 5))
    swap = low == 2
    bucket = best * PAIRS_PER_GROUP + pos
    return jnp.concatenate([jnp.where(swap, w_high, w_low), jnp.where(swap, w_low, w_high)], axis=0), bucket


def _router_logits(w_t, h):
    w_hi = w_t.astype(BF16)
    w_r = w_t - w_hi.astype(F32)
    w_mid = w_r.astype(BF16)
    w_lo = (w_r - w_mid.astype(F32)).astype(BF16)
    h_hi = h.astype(BF16)
    h_lo = (h - h_hi.astype(F32)).astype(BF16)
    dims = (((1,), (1,)), ((), ()))
    a = lax.dot_general(jnp.concatenate([w_hi, w_mid, w_lo], axis=0), h_hi, dims, preferred_element_type=F32)
    b = lax.dot_general(jnp.concatenate([w_hi, w_mid], axis=0), h_lo, dims, preferred_element_type=F32)
    e = w_t.shape[0]
    return a[:e] + a[e:2 * e] + a[2 * e:] + b[:e] + b[e:]


def _post_kernel(x_ref, at_ref, yc_ref, wglu_ref, gs_ref, wo_ref, g1_ref, n2_ref, sc_ref, sh_ref,
                 wrt_ref, rb_ref, x1_ref, h2_ref, cw_ref, gid_ref, y_scr):
    nchunk = y_scr.shape[1] // SSM_CHUNK
    for s in range(SSM_CHUNK):
        for j in range(SSM_COLS):
            y_scr[j, pl.ds(s, nchunk, stride=SSM_CHUNK), :] = yc_ref[j, :, s * LANES:(s + 1) * LANES].astype(F32)
    tm = x_ref.shape[0]
    sub = tm // POST_SPLIT
    for part in range(POST_SPLIT):
        rows = slice(part * sub, (part + 1) * sub)
        yg = jax.nn.gelu(jnp.concatenate([y_scr[j, rows, :] for j in range(SSM_COLS)], axis=1))
        z = yg * jax.nn.sigmoid(jnp.dot(yg.astype(BF16), wglu_ref[...], preferred_element_type=F32))
        zn = _rms(z, gs_ref[...]).astype(BF16)
        o = (jnp.dot(at_ref[rows, :], wo_ref[:ATTN_WIDTH, :], preferred_element_type=F32)
             + jnp.dot(zn, wo_ref[ATTN_WIDTH:, :], preferred_element_type=F32))
        x1 = x_ref[rows, :] + g1_ref[...] * o
        x1_ref[rows, :] = x1
        h2 = _rms(x1, n2_ref[...]) * (1.0 + sc_ref[...]) + sh_ref[...]
        h2_ref[rows, :] = h2
        logits = _router_logits(wrt_ref[...], h2)
        cw, bucket = _route(logits, rb_ref[...])
        cw_ref[:, rows] = cw
        gid_ref[:, rows] = bucket


def _post(x, attn, yc, w_glu, ssm_g, w_out, g1, n2g, sc2, sh2, w_router_t, router_bias, seq):
    t, d = x.shape
    tm = min(TOKEN_TILE, seq)
    per_b = seq // tm
    tok = lambda w: pl.BlockSpec((tm, w), lambda i: (i, 0))
    const = lambda a: pl.BlockSpec(a.shape, lambda i: (0,) * a.ndim)
    per_batch = pl.BlockSpec((None, 1, d), lambda i: (i // per_b, 0, 0))
    col = lambda r: pl.BlockSpec((r, tm), lambda i: (0, i))
    chunked = pl.BlockSpec((SSM_COLS, tm // SSM_CHUNK, CHUNK_LANES), lambda i: (0, i, 0))
    return pl.pallas_call(
        _post_kernel,
        grid=(t // tm,),
        in_specs=[tok(d), tok(ATTN_WIDTH), chunked, const(w_glu), const(ssm_g),
                  const(w_out), per_batch, const(n2g), per_batch, per_batch, const(w_router_t), const(router_bias)],
        out_specs=[tok(d), tok(d), col(2), col(1)],
        out_shape=[jax.ShapeDtypeStruct((t, d), F32), jax.ShapeDtypeStruct((t, d), F32),
                   jax.ShapeDtypeStruct((2, t), F32), jax.ShapeDtypeStruct((1, t), jnp.int32)],
        scratch_shapes=[pltpu.VMEM((SSM_COLS, tm, LANES), F32)],
        compiler_params=_params(("parallel",), VMEM_LIMIT),
        name="post_mix",
    )(x, attn, yc, w_glu, ssm_g, w_out, g1, n2g, sc2, sh2, w_router_t, router_bias)


def _moe_kernel(kind_ref, rb_ref, bk_ref, pa_ref, pb_ref, first_ref, cast_ref, cpos_ref, pe_ref, offs_ref,
                x_ref, cw_ref, wg_ref, wu_ref, wd_ref, o_ref, wg_s, wu_s, wd_s):
    s = pl.program_id(0)

    @pl.when(cast_ref[s] == 1)
    def _():
        slot = cpos_ref[s]
        wg_s[slot] = wg_ref[...].astype(BF16)
        wu_s[slot] = wu_ref[...].astype(BF16)
        wd_s[slot] = wd_ref[...].astype(BF16)

    @pl.when(kind_ref[s] == STEP_ITEM)
    def _():
        bucket = bk_ref[s]
        base = rb_ref[s] * MOE_ROWS
        lo_row, hi_row = offs_ref[bucket] - base, offs_ref[bucket + 1] - base
        half = MOE_ROWS // 2
        slots = (pa_ref[s], pb_ref[s])

        def run(r0, r1):
            rows = r0 + lax.broadcasted_iota(jnp.int32, (r1 - r0, 1), 0)
            cw = jnp.where((rows >= lo_row) & (rows < hi_row), cw_ref[r0:r1, :], 0.0)
            x = x_ref[r0:r1, :].astype(BF16)
            y = None
            for k in range(2):
                gate = jnp.dot(x, wg_s[slots[k]], preferred_element_type=F32)
                up = jnp.dot(x, wu_s[slots[k]], preferred_element_type=F32)
                act = (gate * jax.nn.sigmoid(gate) * up * cw[:, k:k + 1]).astype(BF16)
                yk = jnp.dot(act, wd_s[slots[k]], preferred_element_type=F32)
                y = yk if y is None else y + yk

            @pl.when(first_ref[s] == 1)
            def _():
                o_ref[r0:r1, :] = y
                for z0, z1 in ((0, r0), (r1, MOE_ROWS)):
                    if z1 > z0:
                        o_ref[z0:z1, :] = jnp.zeros((z1 - z0, o_ref.shape[1]), F32)

            @pl.when(first_ref[s] == 0)
            def _():
                o_ref[r0:r1, :] += y

        needs_lower, needs_upper = lo_row < half, hi_row > half
        pl.when(needs_lower & needs_upper)(lambda: run(0, MOE_ROWS))
        pl.when(needs_lower & jnp.logical_not(needs_upper))(lambda: run(0, half))
        pl.when(jnp.logical_not(needs_lower) & needs_upper)(lambda: run(half, MOE_ROWS))


def _moe_plan_kernel(offs_ref, kind_ref, rb_ref, bk_ref, pa_ref, pb_ref, first_ref, cast_ref, cpos_ref, pe_ref,
                     irb, ibk, *, nblk, n_steps_max):
    i32 = jnp.int32
    ng, epg, ppg = N_EXPERT_GROUPS, EXPERTS_PER_GROUP, PAIRS_PER_GROUP

    def block_body(rb, cnt):
        lo = rb * MOE_ROWS

        def bucket_body(bk, cnt):
            a, b = offs_ref[bk], offs_ref[bk + 1]
            irb[cnt] = rb
            ibk[cnt] = bk
            return cnt + ((a < lo + MOE_ROWS) & (b > lo) & (b > a)).astype(i32)

        return lax.fori_loop(0, ng * ppg, bucket_body, cnt)

    n_items = lax.fori_loop(0, nblk, block_body, i32(0))

    def count_body(i, m):
        g = ibk[i] // ppg
        return tuple(m[k] + (g == k).astype(i32) for k in range(ng))

    m = lax.fori_loop(0, n_items, count_body, (i32(0),) * ng)

    def next_group(g):
        nxt = i32(-1)
        for k in range(ng - 1, 0, -1):
            nxt = jnp.where((k > g) & (m[k] > 0), k, nxt)
        return nxt

    def emit(s, kind, rb, bk, pa, pb, first, cast, cpos, pe):
        kind_ref[s], rb_ref[s], bk_ref[s], pa_ref[s], pb_ref[s] = kind, rb, bk, pa, pb
        first_ref[s], cast_ref[s], cpos_ref[s], pe_ref[s] = first, cast, cpos, pe

    def item_body(i, carry):
        s, gcur, parity, q, last_pe, last_rb = carry
        rb, bk = irb[i], ibk[i]
        g, pos = bk // ppg, bk % ppg
        new = g != gcur
        started = gcur >= 0
        loaders = jnp.where(new, jnp.where(started, jnp.maximum(epg - q, 0), epg), 0)
        parity = jnp.where(new & started, 1 - parity, parity)
        q = jnp.where(new, 0, q)
        for j in range(epg):
            on = j >= epg - loaders
            emit(s, STEP_LOAD, rb, bk, 0, 0, 0, 1, parity * epg + j, epg * g + j)
            last_pe = jnp.where(on, epg * g + j, last_pe)
            s = s + on.astype(i32)
        nxt = next_group(g)
        pre = (q < epg) & (nxt >= 0)
        pe = jnp.where(pre, epg * nxt + q, last_pe)
        slot_a, slot_b = i32(PAIR_SLOTS[0][0]), i32(PAIR_SLOTS[0][1])
        for p in range(1, ppg):
            slot_a = jnp.where(pos == p, PAIR_SLOTS[p][0], slot_a)
            slot_b = jnp.where(pos == p, PAIR_SLOTS[p][1], slot_b)
        emit(s, STEP_ITEM, rb, bk, parity * epg + slot_a, parity * epg + slot_b, (rb != last_rb).astype(i32),
             pre.astype(i32), (1 - parity) * epg + q, pe)
        return s + 1, g, parity, q + 1, pe, rb

    s, _, _, _, last_pe, last_rb = lax.fori_loop(
        0, n_items, item_body, (i32(0), i32(-1), i32(0), i32(0), i32(0), i32(-1)))
    last_bk = ibk[jnp.maximum(n_items - 1, 0)]

    def pad_body(s, _):
        emit(s, STEP_PAD, last_rb, last_bk, 0, 0, 0, 0, 0, last_pe)
        return 0

    lax.fori_loop(s, n_steps_max, pad_body, 0)


def _moe_steps(bucket, t):
    i32 = jnp.int32
    nbk = N_EXPERT_GROUPS * PAIRS_PER_GROUP
    order = jnp.argsort(bucket, stable=True).astype(i32)
    counts = jnp.sum((bucket[None, :] == jnp.arange(nbk, dtype=i32)[:, None]).astype(i32), axis=1)
    offs = jnp.concatenate([jnp.zeros((1,), i32), jnp.cumsum(counts).astype(i32)])
    nblk = t // MOE_ROWS
    n_items_max = nblk + nbk - 1
    n_steps_max = n_items_max + N_EXPERTS
    smem = pl.BlockSpec(memory_space=pltpu.SMEM)
    tables = pl.pallas_call(
        functools.partial(_moe_plan_kernel, nblk=nblk, n_steps_max=n_steps_max),
        in_specs=[smem],
        out_specs=[smem] * 9,
        out_shape=[jax.ShapeDtypeStruct((n_steps_max,), i32)] * 9,
        scratch_shapes=[pltpu.SMEM((n_items_max + 1,), i32)] * 2,
        name="moe_plan",
    )(offs)
    return order, (*tables, offs), n_steps_max


def _moe(xs, cws, w_gate, w_up, w_down, layer, tables, n_steps_max):
    t, d = xs.shape
    ff = w_gate.shape[3]
    w_map = lambda s, kind, rb, bk, pa, pb, fi, ca, cp, pe, of: (layer, pe[s], 0, 0)
    row_map = lambda s, kind, rb, *_: (rb[s], 0)
    nres = 2 * EXPERTS_PER_GROUP
    grid_spec = pltpu.PrefetchScalarGridSpec(
        num_scalar_prefetch=len(tables),
        grid=(n_steps_max,),
        in_specs=[pl.BlockSpec((MOE_ROWS, d), row_map), pl.BlockSpec((MOE_ROWS, 2), row_map),
                  pl.BlockSpec((None, None, d, ff), w_map), pl.BlockSpec((None, None, d, ff), w_map),
                  pl.BlockSpec((None, None, ff, d), w_map)],
        out_specs=pl.BlockSpec((MOE_ROWS, d), row_map),
        scratch_shapes=[pltpu.VMEM((nres, d, ff), BF16), pltpu.VMEM((nres, d, ff), BF16),
                        pltpu.VMEM((nres, ff, d), BF16)],
    )
    return pl.pallas_call(
        _moe_kernel,
        grid_spec=grid_spec,
        out_shape=jax.ShapeDtypeStruct((t, d), F32),
        compiler_params=_params(("arbitrary",), MOE_VMEM_LIMIT),
        name="moe_grouped",
    )(*tables, xs, cws, w_gate, w_up, w_down)


def _take_rows(a, idx):
    return a.at[idx].get(mode="promise_in_bounds", unique_indices=True)


def _final_kernel(x_ref, y_ref, g_ref, o_ref):
    o_ref[...] = x_ref[...] + g_ref[...] * y_ref[...]


def _final(x1, y, g2, seq):
    t, d = x1.shape
    tm = min(TOKEN_TILE, seq)
    per_b = seq // tm
    tok = lambda w: pl.BlockSpec((tm, w), lambda i: (i, 0))
    return pl.pallas_call(
        _final_kernel,
        grid=(t // tm,),
        in_specs=[tok(d), tok(d), pl.BlockSpec((None, 1, d), lambda i: (i // per_b, 0, 0))],
        out_specs=tok(d),
        out_shape=jax.ShapeDtypeStruct((t, d), F32),
        compiler_params=_params(("parallel",)),
        name="final_residual",
    )(x1, y, g2)


def kernel(x, c, positions, ada_w, ada_b, norm1_g, w_in, q_norm_g, k_norm_g, attn_sink, lam_re, lam_im, ssm_b_re, ssm_b_im, ssm_c_re, ssm_c_im, ssm_d, ssm_log_dt, w_glu, attn_out_g, ssm_out_g, w_out, norm2_g, w_router, router_bias, w_exp_gate, w_exp_up, w_exp_down):
    batch, seq, d = x.shape
    depth = ada_w.shape[0]
    t = batch * seq
    assert seq % ATTN_BLOCK == 0 and seq % SSM_CHUNK == 0 and t % MOE_ROWS == 0

    mod = _adaln_mod(c, ada_w, ada_b).reshape(depth, 6, batch, 1, d)
    cos, sin = _rope_tables(positions)
    head_sum, rot = _rope_constants()
    bias = _attn_bias()
    w_router_t = w_router.T
    s5_mats = _s5_prep(lam_re, lam_im, ssm_b_re, ssm_b_im, ssm_c_re, ssm_c_im, ssm_d, ssm_log_dt)
    router_bias_col = router_bias.reshape(N_EXPERTS, 1)

    xf = x.reshape(t, d)
    res = None
    for l in range(depth):
        sh1, sc1, g1, sh2, sc2, g2 = (mod[l, j] for j in range(6))
        qg = (jnp.tile(q_norm_g[l], N_Q_HEADS) * HEAD_DIM ** -0.5).reshape(1, ATTN_WIDTH)
        kg = jnp.tile(k_norm_g[l], N_KV_HEADS).reshape(1, KV_WIDTH)
        outs = _inproj(xf, res, sc1, sh1, norm1_g[l].reshape(1, d), w_in[l].astype(BF16), qg, kg, head_sum, rot,
                       cos, sin, seq)
        if res is None:
            q, kx, vx, uc = outs
        else:
            q, kx, vx, uc, xf = outs
        attn = _attention(q, kx, vx, attn_sink[l], attn_out_g[l].reshape(1, ATTN_WIDTH), bias, batch, seq)
        yc = _s5_scan(uc, s5_mats, l, seq // SSM_CHUNK, batch)
        x1, h2, cw, gid = _post(xf, attn, yc, w_glu[l].astype(BF16), ssm_out_g[l].reshape(1, SSM_WIDTH),
                                 w_out[l].astype(BF16), g1, norm2_g[l].reshape(1, d), sc2, sh2, w_router_t,
                                 router_bias_col, seq)
        order, tables, n_steps_max = _moe_steps(gid.reshape(t), t)
        y_sorted = _moe(_take_rows(h2, order), _take_rows(cw.T, order), w_exp_gate, w_exp_up, w_exp_down, l,
                        tables, n_steps_max)
        y = _take_rows(y_sorted, jnp.argsort(order).astype(jnp.int32))
        xf, res = x1, (y, g2)
    y, g2 = res
    return _final(xf, y, g2, seq).reshape(batch, seq, d)
```

```python
import functools

import numpy as np
import jax
import jax.numpy as jnp
from jax import lax
from jax.experimental import pallas as pl
from jax.experimental.pallas import tpu as pltpu

F32 = jnp.float32
BF16 = jnp.bfloat16

HEAD_DIM = 64
N_Q_HEADS = 8
N_KV_HEADS = 2
Q_PER_KV = N_Q_HEADS // N_KV_HEADS
ATTN_WIDTH = N_Q_HEADS * HEAD_DIM
KV_WIDTH = N_KV_HEADS * HEAD_DIM
ATTN_BLOCK = 128
ATTN_Q_TILE = 2048
ROPE_THETA = 10000.0
ROPE_SLAB = 256
LANES = 128
SSM_GROUP_CH = 16
SSM_GROUPS = 32
SSM_WIDTH = SSM_GROUPS * SSM_GROUP_CH
SSM_STATE = 64
SSM_CHUNK = 16
SSM_COLS = SSM_WIDTH // LANES
COL_GROUPS = LANES // SSM_GROUP_CH
COL_STATE = COL_GROUPS * SSM_STATE
CHUNK_LANES = SSM_CHUNK * LANES
SSM_NSPLIT = 4
N_EXPERTS = 16
N_EXPERT_GROUPS = 4
EXPERTS_PER_GROUP = N_EXPERTS // N_EXPERT_GROUPS
PAIRS_PER_GROUP = EXPERTS_PER_GROUP * (EXPERTS_PER_GROUP - 1) // 2
PAIR_SLOTS = ((0, 1), (0, 2), (0, 3), (1, 3), (1, 2), (3, 2))
EPS = 1e-6
MASK_BIAS = -1e30

TOKEN_TILE = 1024
POST_SPLIT = 1
MOE_ROWS = 256
VMEM_LIMIT = 48 * 1024 * 1024
MOE_VMEM_LIMIT = 56 * 1024 * 1024
STEP_PAD, STEP_LOAD, STEP_ITEM = 0, 1, 2


def _params(sem, vmem=None):
    return pltpu.CompilerParams(dimension_semantics=sem, vmem_limit_bytes=vmem)


def _rms(x, g):
    return x * lax.rsqrt(jnp.mean(x * x, axis=-1, keepdims=True) + EPS) * g


def _mod_kernel(c_ref, w_ref, b_ref, o_ref):
    c = c_ref[...]
    s = c * jax.nn.sigmoid(c)
    o_ref[...] = jnp.dot(s.astype(BF16), w_ref[...].astype(BF16), preferred_element_type=F32) + b_ref[...]


def _adaln_mod(c, ada_w, ada_b):
    depth, d, d6 = ada_w.shape
    nb = c.shape[0]
    n6 = d6 // d
    return pl.pallas_call(
        _mod_kernel,
        grid=(depth, n6),
        in_specs=[pl.BlockSpec((nb, d), lambda l, j: (0, 0)),
                  pl.BlockSpec((None, d, d), lambda l, j: (l, 0, j)),
                  pl.BlockSpec((None, None, 1, d), lambda l, j: (l, j, 0, 0))],
        out_specs=pl.BlockSpec((None, None, nb, d), lambda l, j: (l, j, 0, 0)),
        out_shape=jax.ShapeDtypeStruct((depth, n6, nb, d), F32),
        compiler_params=_params(("arbitrary", "arbitrary"), VMEM_LIMIT),
        name="adaln_mod",
    )(c, ada_w, ada_b.reshape(depth, n6, 1, d))


def _rope_kernel(pos_ref, freq_ref, cos_ref, sin_ref):
    ang = pos_ref[...].astype(F32) * freq_ref[...]
    cos_ref[...] = jnp.cos(ang)
    sin_ref[...] = jnp.sin(ang)


def _rope_tables(positions):
    half = HEAD_DIM // 2
    t = positions.size
    per_row = LANES // half
    rows = t // per_row
    pos_rep = jnp.repeat(positions.reshape(rows, per_row), half, axis=1)
    freq = (ROPE_THETA ** (-np.arange(half, dtype=np.float64) / half)).astype(np.float32)
    freq_row = jnp.asarray(np.tile(freq, per_row)[None, :])
    blk = min(rows, 512)
    cos, sin = pl.pallas_call(
        _rope_kernel,
        grid=(rows // blk,),
        in_specs=[pl.BlockSpec((blk, LANES), lambda i: (i, 0)),
                  pl.BlockSpec((1, LANES), lambda i: (0, 0))],
        out_specs=[pl.BlockSpec((blk, LANES), lambda i: (i, 0))] * 2,
        out_shape=[jax.ShapeDtypeStruct((rows, LANES), F32)] * 2,
        compiler_params=_params(("arbitrary",)),
        name="rope_tables",
    )(pos_rep, freq_row)
    widen = lambda a: jnp.tile(a.reshape(t, half), (1, per_row))
    return widen(cos), widen(sin)


def _rope_constants():
    lane = np.arange(ROPE_SLAB)
    head_sum = (lane[:, None] // HEAD_DIM == lane[None, :] // HEAD_DIM).astype(np.float32)
    half = HEAD_DIM // 2
    rot = np.zeros((ROPE_SLAB, ROPE_SLAB), np.float32)
    for d in range(ROPE_SLAB):
        if d % HEAD_DIM < half:
            rot[d + half, d] = -1.0
        else:
            rot[d - half, d] = 1.0
    return jnp.asarray(head_sum, BF16), jnp.asarray(rot, BF16)


def _inproj_kernel(*refs, has_res):
    if has_res:
        (x_ref, y_ref, g2_ref, sc_ref, sh_ref, n1_ref, w_ref, qg_ref, kg_ref, hs_ref, rot_ref,
         cos_ref, sin_ref, q_ref, k_ref, v_ref, uc_ref, xo_ref, u_scr) = refs
        x = x_ref[...] + g2_ref[...] * y_ref[...]
        xo_ref[...] = x
    else:
        (x_ref, sc_ref, sh_ref, n1_ref, w_ref, qg_ref, kg_ref, hs_ref, rot_ref,
         cos_ref, sin_ref, q_ref, k_ref, v_ref, uc_ref, u_scr) = refs
        x = x_ref[...]
    h = _rms(x, n1_ref[...]) * (1.0 + sc_ref[...]) + sh_ref[...]
    proj = jnp.dot(h.astype(BF16), w_ref[...], preferred_element_type=F32)
    q = proj[:, :ATTN_WIDTH]
    k = proj[:, ATTN_WIDTH:ATTN_WIDTH + KV_WIDTH]
    v = proj[:, ATTN_WIDTH + KV_WIDTH:ATTN_WIDTH + 2 * KV_WIDTH]
    cos = cos_ref[...]
    sin = sin_ref[...]
    reps = ATTN_WIDTH // LANES
    cos_q = jnp.concatenate([cos] * reps, axis=1)
    sin_q = jnp.concatenate([sin] * reps, axis=1)

    def head_norm_rope(t, gain, c, s):
        outs = []
        for lo in range(0, t.shape[1], ROPE_SLAB):
            wd = min(ROPE_SLAB, t.shape[1] - lo)
            ts, lanes = t[:, lo:lo + wd], slice(lo, lo + wd)
            ssq = jnp.dot((ts * ts).astype(BF16), hs_ref[:wd, :wd], preferred_element_type=F32)
            tn = (ts * lax.rsqrt(ssq * (1.0 / HEAD_DIM) + EPS) * gain[:, lanes]).astype(BF16)
            tr = jnp.dot(tn, rot_ref[:wd, :wd], preferred_element_type=F32)
            outs.append(tn.astype(F32) * c[:, lanes] + tr * s[:, lanes])
        return outs[0] if len(outs) == 1 else jnp.concatenate(outs, axis=1)

    qo = head_norm_rope(q, qg_ref[...], cos_q, sin_q)
    ko = head_norm_rope(k, kg_ref[...], cos, sin)
    q_ref[...] = qo.astype(BF16)
    k_ref[...] = jnp.concatenate([ko, pltpu.roll(ko, HEAD_DIM, axis=1)], axis=1).astype(BF16)
    v_ref[...] = jnp.concatenate([v, pltpu.roll(v, HEAD_DIM, axis=1)], axis=1).astype(BF16)
    u0 = ATTN_WIDTH + 2 * KV_WIDTH
    nchunk = u_scr.shape[1] // SSM_CHUNK
    for j in range(SSM_COLS):
        u_scr[j] = proj[:, u0 + j * LANES:u0 + (j + 1) * LANES]
    for s in range(SSM_CHUNK):
        for j in range(SSM_COLS):
            lanes = slice(s * SSM_WIDTH + j * LANES, s * SSM_WIDTH + (j + 1) * LANES)
            uc_ref[:, lanes] = u_scr[j, pl.ds(s, nchunk, stride=SSM_CHUNK), :].astype(BF16)


def _inproj(x, res, sc1, sh1, n1g, w_in, qg, kg, head_sum, rot, cos, sin, seq):
    t, d = x.shape
    tm = min(TOKEN_TILE, seq)
    per_b = seq // tm
    in_width = w_in.shape[1]
    tok = lambda w: pl.BlockSpec((tm, w), lambda i: (i, 0))
    const = lambda a: pl.BlockSpec(a.shape, lambda i: (0,) * a.ndim)
    per_batch = pl.BlockSpec((None, 1, d), lambda i: (i // per_b, 0, 0))
    chunked = pl.BlockSpec((tm // SSM_CHUNK, SSM_CHUNK * SSM_WIDTH), lambda i: (i, 0))
    ins, specs = [x], [tok(d)]
    if res is not None:
        y_prev, g2_prev = res
        ins += [y_prev, g2_prev]
        specs += [tok(d), per_batch]
    ins += [sc1, sh1, n1g, w_in, qg, kg, head_sum, rot, cos, sin]
    specs += [per_batch, per_batch, const(n1g), const(w_in), const(qg), const(kg), const(head_sum), const(rot),
              tok(LANES), tok(LANES)]
    out_shape = [jax.ShapeDtypeStruct((t, ATTN_WIDTH), BF16), jax.ShapeDtypeStruct((t, 2 * KV_WIDTH), BF16),
                 jax.ShapeDtypeStruct((t, 2 * KV_WIDTH), BF16),
                 jax.ShapeDtypeStruct((t // SSM_CHUNK, SSM_CHUNK * SSM_WIDTH), BF16)]
    out_specs = [tok(ATTN_WIDTH), tok(2 * KV_WIDTH), tok(2 * KV_WIDTH), chunked]
    if res is not None:
        out_shape.append(jax.ShapeDtypeStruct((t, d), F32))
        out_specs.append(tok(d))
    assert in_width == ATTN_WIDTH + 2 * KV_WIDTH + SSM_WIDTH
    return pl.pallas_call(
        functools.partial(_inproj_kernel, has_res=res is not None),
        grid=(t // tm,),
        in_specs=specs,
        out_specs=out_specs,
        out_shape=out_shape,
        scratch_shapes=[pltpu.VMEM((SSM_COLS, tm, LANES), F32)],
        compiler_params=_params(("parallel",), VMEM_LIMIT),
        name="inproj",
    )(*ins)


def _attn_kernel(sink_ref, q_ref, kc_ref, kp_ref, vc_ref, vp_ref, bias_ref, g_ref, o_ref):
    nsub = q_ref.shape[0] // ATTN_BLOCK
    kk = jnp.concatenate([kp_ref[...], kc_ref[...]], axis=0)
    vv = jnp.concatenate([vp_ref[...], vc_ref[...]], axis=0)
    low = lax.broadcasted_iota(jnp.int32, (kk.shape[0], KV_WIDTH), 1) < HEAD_DIM
    zero = jnp.zeros((kk.shape[0], KV_WIDTH), BF16)

    def variants(a):
        nat, swp = a[:, :KV_WIDTH], a[:, KV_WIDTH:]
        return {(0, 0): jnp.where(low, nat, zero), (0, 1): jnp.where(low, zero, swp),
                (1, 0): jnp.where(low, swp, zero), (1, 1): jnp.where(low, zero, nat)}

    kvar, vvar = variants(kk), variants(vv)
    band = bias_ref[1]
    first = bias_ref[jnp.minimum(pl.program_id(1), 1)]
    upper = lax.broadcasted_iota(jnp.int32, (2 * ATTN_BLOCK, 1), 0) < ATTN_BLOCK
    for j in range(nsub):
        bias = first if j == 0 else band
        bias2 = jnp.concatenate([bias, bias], axis=0)
        keys = slice(j * ATTN_BLOCK, (j + 2) * ATTN_BLOCK)
        qrows = slice(j * ATTN_BLOCK, (j + 1) * ATTN_BLOCK)
        tiles = [None] * (N_Q_HEADS // 2)
        for kv in range(N_KV_HEADS):
            for half in range(2):
                pairs = (2 * kv, 2 * kv + 1)
                heads = (2 * pairs[0] + half, 2 * pairs[1] + half)
                qs = jnp.concatenate([q_ref[qrows, p * LANES:(p + 1) * LANES] for p in pairs], axis=0)
                s = lax.dot_general(qs, kvar[(kv, half)][keys], (((1,), (1,)), ((), ())),
                                    preferred_element_type=F32) + bias2
                sink = jnp.where(upper, sink_ref[heads[0]], sink_ref[heads[1]])
                m = jnp.maximum(jnp.max(s, axis=-1, keepdims=True), sink)
                p = jnp.exp(s - m)
                denom = jnp.sum(p, axis=-1, keepdims=True) + jnp.exp(sink - m)
                o = jnp.dot(p.astype(BF16), vvar[(kv, half)][keys], preferred_element_type=F32) * (1.0 / denom)
                for r, pr in enumerate(pairs):
                    part = o[r * ATTN_BLOCK:(r + 1) * ATTN_BLOCK]
                    tiles[pr] = part if tiles[pr] is None else tiles[pr] + part
        a = jnp.concatenate(tiles, axis=1)
        o_ref[qrows, :] = _rms(a, g_ref[...]).astype(BF16)


def _attn_bias():
    qi = np.arange(ATTN_BLOCK)[:, None]
    sj = np.arange(2 * ATTN_BLOCK)[None, :]
    diff = qi + ATTN_BLOCK - sj
    band = (diff >= 0) & (diff < ATTN_BLOCK)
    first = band & (sj >= ATTN_BLOCK)
    return jnp.asarray(np.where(np.stack([first, band]), 0.0, MASK_BIAS).astype(np.float32))


def _attention(q, kx, vx, sink, out_g, bias, batch, seq):
    t = q.shape[0]
    qb = min(ATTN_Q_TILE, seq)
    nsub = qb // ATTN_BLOCK
    nq = seq // qb
    nb = seq // ATTN_BLOCK
    cur = lambda w: pl.BlockSpec((qb, w), lambda b, n, s: (b * nq + n, 0))
    prev = lambda w: pl.BlockSpec((ATTN_BLOCK, w), lambda b, n, s: (b * nb + jnp.maximum(n * nsub - 1, 0), 0))
    grid_spec = pltpu.PrefetchScalarGridSpec(
        num_scalar_prefetch=1,
        grid=(batch, nq),
        in_specs=[cur(ATTN_WIDTH), cur(2 * KV_WIDTH), prev(2 * KV_WIDTH), cur(2 * KV_WIDTH), prev(2 * KV_WIDTH),
                  pl.BlockSpec(bias.shape, lambda b, n, s: (0, 0, 0)),
                  pl.BlockSpec((1, ATTN_WIDTH), lambda b, n, s: (0, 0))],
        out_specs=cur(ATTN_WIDTH),
    )
    return pl.pallas_call(
        _attn_kernel,
        grid_spec=grid_spec,
        out_shape=jax.ShapeDtypeStruct((t, ATTN_WIDTH), BF16),
        compiler_params=_params(("parallel", "arbitrary")),
        name="swa_attention",
    )(sink, q, kx, kx, vx, vx, bias, out_g)


def _spread(x, expander3):
    hi = x.astype(BF16)
    r1 = x - hi.astype(F32)
    mid = r1.astype(BF16)
    lo = (r1 - mid.astype(F32)).astype(BF16)
    return jnp.dot(jnp.concatenate([hi, mid, lo], axis=1), expander3, preferred_element_type=F32)


def _s5_prep_kernel(lr_re_ref, lr_im_ref, ldt_ref, bt_re_ref, bt_im_ref, ct_re_ref, ct_im_ref,
                    d_ref, lcol_re_ref, lcol_im_ref, ldtcol_ref, exp_ref, exph_ref, expt_ref, expw_ref,
                    t_ref, w_ref, v_ref, la_ref, lb_ref):
    hi = lax.Precision.HIGHEST
    nl = SSM_CHUNK
    low = lax.broadcasted_iota(jnp.int32, (1, 2 * SSM_STATE), 1) < SSM_STATE
    row_low = lax.broadcasted_iota(jnp.int32, (2 * SSM_STATE, 1), 0) < SSM_STATE
    jcol = lax.broadcasted_iota(jnp.int32, (nl, 1), 0).astype(F32)
    kt_lane = lax.broadcasted_iota(jnp.int32, (SSM_GROUP_CH, nl * SSM_GROUP_CH), 1)
    kt_row = lax.broadcasted_iota(jnp.int32, (SSM_GROUP_CH, nl * SSM_GROUP_CH), 0)

    w_all, v_all, kt_all = [], [], []
    for gm in range(COL_GROUPS):
        dt = jnp.exp(ldt_ref[gm])
        lam_re, lam_im = lr_re_ref[gm], lr_im_ref[gm]
        a_r, th_r = lam_re * dt, lam_im * dt

        er = jnp.exp(jcol * a_r)
        pw_re, pw_im = er * jnp.cos(jcol * th_r), er * jnp.sin(jcol * th_r)

        nr, ni = pw_re[1:2, :] - 1.0, pw_im[1:2, :]
        den = lam_re * lam_re + lam_im * lam_im
        c_re, c_im = (nr * lam_re + ni * lam_im) / den, (ni * lam_re - nr * lam_im) / den
        bt_re, bt_im = bt_re_ref[gm], bt_im_ref[gm]
        bb_re, bb_im = c_re * bt_re - c_im * bt_im, c_re * bt_im + c_im * bt_re

        w_rows = []
        for s in range(nl):
            j = nl - 1 - s
            pr, pi = pw_re[j:j + 1, :], pw_im[j:j + 1, :]
            w_rows.append(jnp.where(low, pr * bb_re - pi * bb_im, pr * bb_im + pi * bb_re))
        w_all.append(w_rows)

        pw_re_t, pw_im_t = pw_re.T, pw_im.T
        pc, ps = _spread(pw_re_t, exp_ref[...]), _spread(pw_im_t, exp_ref[...])
        ct_re, ct_im = _spread(ct_re_ref[gm], exph_ref[...]), _spread(ct_im_ref[gm], exph_ref[...])
        a_re, a_im = ct_re * pc - ct_im * ps, ct_re * ps + ct_im * pc
        a_cat = jnp.where(row_low, a_re, -a_im)
        l1_re, l1_im = pw_re_t[:, 1:2], pw_im_t[:, 1:2]
        v_re, v_im = a_re * l1_re - a_im * l1_im, a_re * l1_im + a_im * l1_re
        v_all.append(jnp.where(row_low, v_re, -v_im))

        kt = jnp.dot(jnp.where(low, bb_re, bb_im), a_cat, precision=hi, preferred_element_type=F32)
        kt_all.append(kt + jnp.where(kt_lane == kt_row, d_ref[gm], 0.0))

    def expand(stacked, expander, row_group, lane_group):
        wide = jnp.dot(stacked.astype(BF16), expander, preferred_element_type=F32)
        r = lax.broadcasted_iota(jnp.int32, wide.shape, 0)
        c = lax.broadcasted_iota(jnp.int32, wide.shape, 1)
        return jnp.where(row_group(r) == lane_group(c), wide, 0.0).astype(BF16)

    chan_group = lambda i: (i >> 4) & (COL_GROUPS - 1)
    state_group = lambda i: (i >> 6) & (COL_GROUPS - 1)

    bd = expand(jnp.concatenate(kt_all, axis=0), expt_ref[...], chan_group, chan_group)
    t_ref[0:LANES, :] = bd
    for s in range(1, nl):
        t_ref[s * LANES:(s + 1) * LANES, :] = jnp.concatenate(
            [jnp.zeros((LANES, s * LANES), BF16), bd[:, :CHUNK_LANES - s * LANES]], axis=1)

    w_stack = jnp.concatenate([w_all[gm][s] for s in range(nl) for gm in range(COL_GROUPS)], axis=0)
    w_ref[...] = expand(w_stack, expw_ref[...], chan_group, state_group)

    v_stack = jnp.concatenate([v_all[gm][half * SSM_STATE:(half + 1) * SSM_STATE, :]
                               for half in range(2) for gm in range(COL_GROUPS)], axis=0)
    v_ref[...] = expand(v_stack, expt_ref[...], state_group, chan_group)

    dtc = jnp.exp(ldtcol_ref[...])
    e16 = jnp.exp(nl * lcol_re_ref[...] * dtc)
    ang = nl * lcol_im_ref[...] * dtc
    la_ref[...] = e16 * jnp.cos(ang)
    lb_ref[...] = e16 * jnp.sin(ang)


def _s5_prep(lam_re, lam_im, b_re, b_im, c_re, c_im, d_skip, log_dt):
    g, p, h, nl = SSM_GROUPS, SSM_STATE, SSM_GROUP_CH, SSM_CHUNK
    cg = COL_GROUPS
    nc = lam_re.shape[0] * SSM_COLS
    col = lambda a: a.reshape((nc, cg) + a.shape[2:])
    dup_row = lambda a: col(jnp.tile(a, (1, 1, 2))[:, :, None, :])
    bt = lambda a: col(jnp.tile(jnp.swapaxes(a, 2, 3), (1, 1, 1, 2)))
    ct = lambda a: col(jnp.tile(jnp.swapaxes(a, 2, 3), (1, 1, 2, 1)))
    d_pad = col(jnp.pad(d_skip.reshape(-1, g, 1, h), ((0, 0), (0, 0), (0, 0), (0, nl * h - h))))
    wide = lambda a: a.reshape(nc, 1, cg * p)
    expand = jnp.asarray(np.tile(np.repeat(np.eye(nl, dtype=np.float32), h, axis=1), (3, 1)), BF16)
    expand_h = jnp.asarray(np.tile(np.eye(h, dtype=np.float32), (3, nl)), BF16)
    exp_t = np.zeros((nl, h, nl, cg, h), np.float32)
    exp_w = np.zeros((2, p, 2, cg, p), np.float32)
    for gm in range(cg):
        exp_t[:, :, :, gm, :] = np.eye(nl * h, dtype=np.float32).reshape(nl, h, nl, h)
        exp_w[:, :, :, gm, :] = np.eye(2 * p, dtype=np.float32).reshape(2, p, 2, p)
    exp_t = jnp.asarray(exp_t.reshape(nl * h, CHUNK_LANES), BF16)
    exp_w = jnp.asarray(exp_w.reshape(2 * p, 2 * COL_STATE), BF16)
    blk = lambda *s: pl.BlockSpec((None,) + s, lambda i: (i,) + (0,) * len(s))
    const = lambda a: pl.BlockSpec(a.shape, lambda i: (0,) * a.ndim)
    lw = nl * h
    return pl.pallas_call(
        _s5_prep_kernel,
        grid=(nc,),
        in_specs=[blk(cg, 1, 2 * p), blk(cg, 1, 2 * p), blk(cg, 1, 1),
                  blk(cg, h, 2 * p), blk(cg, h, 2 * p), blk(cg, 2 * p, h), blk(cg, 2 * p, h), blk(cg, 1, lw),
                  blk(1, cg * p), blk(1, cg * p), blk(1, cg * p), const(expand), const(expand_h), const(exp_t),
                  const(exp_w)],
        out_specs=[blk(CHUNK_LANES, CHUNK_LANES), blk(CHUNK_LANES, 2 * COL_STATE), blk(2 * COL_STATE, CHUNK_LANES),
                   blk(1, COL_STATE), blk(1, COL_STATE)],
        out_shape=[jax.ShapeDtypeStruct((nc, CHUNK_LANES, CHUNK_LANES), BF16),
                   jax.ShapeDtypeStruct((nc, CHUNK_LANES, 2 * COL_STATE), BF16),
                   jax.ShapeDtypeStruct((nc, 2 * COL_STATE, CHUNK_LANES), BF16),
                   jax.ShapeDtypeStruct((nc, 1, COL_STATE), F32), jax.ShapeDtypeStruct((nc, 1, COL_STATE), F32)],
        compiler_params=_params(("parallel",), VMEM_LIMIT),
        name="s5_prep",
    )(dup_row(lam_re), dup_row(lam_im), col(log_dt[:, :, None, None]),
      bt(b_re), bt(b_im), ct(c_re), ct(c_im), d_pad, wide(lam_re), wide(lam_im),
      wide(jnp.repeat(log_dt, p, axis=1)), expand, expand_h, exp_t, exp_w)


def _s5_kernel(*refs, nchunks, nb):
    uc_refs = refs[:SSM_CHUNK]
    t_ref, w_ref, v_ref, la_ref, lb_ref, o_ref, ucat_ref, s_ref, xp_ref = refs[SSM_CHUNK:]

    @pl.when(pl.program_id(1) == 0)
    def _():
        for s in range(SSM_CHUNK):
            ucat_ref[:, s * LANES:(s + 1) * LANES] = uc_refs[s][...]
        s_in = jnp.dot(ucat_ref[...], w_ref[...], preferred_element_type=F32)
        nblk = COL_STATE // LANES
        for b in range(2 * nblk):
            s_ref[b] = s_in[:, b * LANES:(b + 1) * LANES]
        lr = [jnp.broadcast_to(la_ref[:, b * LANES:(b + 1) * LANES], (nb, LANES)) for b in range(nblk)]
        li = [jnp.broadcast_to(lb_ref[:, b * LANES:(b + 1) * LANES], (nb, LANES)) for b in range(nblk)]

        def step(c, carry):
            rows = pl.ds(c, nb, stride=nchunks)
            out = []
            for b in range(nblk):
                re, im = carry[2 * b], carry[2 * b + 1]
                xp_ref[b, rows, :] = re
                xp_ref[nblk + b, rows, :] = im
                out.append(lr[b] * re - li[b] * im + s_ref[b, rows, :])
                out.append(lr[b] * im + li[b] * re + s_ref[nblk + b, rows, :])
            return tuple(out)

        zero = jnp.zeros((nb, LANES), F32)
        lax.fori_loop(0, nchunks, step, (zero,) * (2 * nblk), unroll=4)

    xp = jnp.concatenate([xp_ref[b] for b in range(2 * COL_STATE // LANES)], axis=1).astype(BF16)
    inter = jnp.dot(xp, v_ref[...], preferred_element_type=F32)
    for kk in range(SSM_NSPLIT):
        @pl.when(pl.program_id(1) == kk)
        def _():
            live = (kk + 1) * (CHUNK_LANES // SSM_NSPLIT)
            intra = jnp.dot(ucat_ref[:, :live], t_ref[:live, :], preferred_element_type=F32)
            o_ref[...] = (intra + inter).astype(BF16)


def _s5_scan(uc, mats, layer, nchunks, nb):
    rows = uc.shape[0]
    c0 = layer * SSM_COLS
    split = CHUNK_LANES // SSM_NSPLIT
    u_spec = lambda s: pl.BlockSpec((rows, LANES), lambda j, k: (0, SSM_COLS * s + j))
    return pl.pallas_call(
        functools.partial(_s5_kernel, nchunks=nchunks, nb=nb),
        grid=(SSM_COLS, SSM_NSPLIT),
        in_specs=[u_spec(s) for s in range(SSM_CHUNK)] + [
            pl.BlockSpec((None, CHUNK_LANES, split), lambda j, k: (c0 + j, 0, k)),
            pl.BlockSpec((None, CHUNK_LANES, 2 * COL_STATE), lambda j, k: (c0 + j, 0, 0)),
            pl.BlockSpec((None, 2 * COL_STATE, split), lambda j, k: (c0 + j, 0, k)),
            pl.BlockSpec((None, 1, COL_STATE), lambda j, k: (c0 + j, 0, 0)),
            pl.BlockSpec((None, 1, COL_STATE), lambda j, k: (c0 + j, 0, 0))],
        out_specs=pl.BlockSpec((None, rows, split), lambda j, k: (j, 0, k)),
        out_shape=jax.ShapeDtypeStruct((SSM_COLS, rows, CHUNK_LANES), BF16),
        scratch_shapes=[pltpu.VMEM((rows, CHUNK_LANES), BF16),
                        pltpu.VMEM((2 * COL_STATE // LANES, rows, LANES), F32),
                        pltpu.VMEM((2 * COL_STATE // LANES, rows, LANES), F32)],
        compiler_params=_params(("parallel", "arbitrary"), VMEM_LIMIT),
        name="s5_scan",
    )(*([uc] * SSM_CHUNK), *mats)


def _route(logits, bias):
    m = jnp.max(logits, axis=0, keepdims=True)
    e = jnp.exp(logits - m)
    probs = e / jnp.sum(e, axis=0, keepdims=True)
    sel = probs + bias
    row = lambda a, i: a[i:i + 1, :]
    best_score, best = None, None
    for grp in range(N_EXPERT_GROUPS):
        a, b, c, d = (row(sel, EXPERTS_PER_GROUP * grp + i) for i in range(EXPERTS_PER_GROUP))
        hab, lab, hcd, lcd = jnp.maximum(a, b), jnp.minimum(a, b), jnp.maximum(c, d), jnp.minimum(c, d)
        top1 = jnp.maximum(hab, hcd)
        top2 = jnp.maximum(jnp.maximum(lab, lcd), jnp.minimum(hab, hcd))
        score = top1 + top2
        if grp == 0:
            best_score, best = score, jnp.zeros(score.shape, jnp.int32)
        else:
            better = score > best_score
            best = jnp.where(better, grp, best)
            best_score = jnp.where(better, score, best_score)

    def pick(a, i):
        out = row(a, i)
        for grp in range(1, N_EXPERT_GROUPS):
            out = jnp.where(best == grp, row(a, EXPERTS_PER_GROUP * grp + i), out)
        return out

    s_in = [pick(sel, i) for i in range(EXPERTS_PER_GROUP)]
    p_in = [pick(probs, i) for i in range(EXPERTS_PER_GROUP)]
    neg = jnp.full(s_in[0].shape, -jnp.inf, F32)

    def argmax_first(vals):
        idx, val = jnp.zeros(vals[0].shape, jnp.int32), vals[0]
        for i in range(1, len(vals)):
            better = vals[i] > val
            idx = jnp.where(better, i, idx)
            val = jnp.where(better, vals[i], val)
        return idx

    i1 = argmax_first(s_in)
    i2 = argmax_first([jnp.where(i1 == i, neg, s_in[i]) for i in range(EXPERTS_PER_GROUP)])
    zero = jnp.zeros(p_in[0].shape, F32)
    g1 = sum(jnp.where(i1 == i, p_in[i], zero) for i in range(EXPERTS_PER_GROUP))
    g2 = sum(jnp.where(i2 == i, p_in[i], zero) for i in range(EXPERTS_PER_GROUP))
    tot = g1 + g2
    w1, w2 = g1 / tot, g2 / tot
    first_low = i1 < i2
    low, high = jnp.minimum(i1, i2), jnp.maximum(i1, i2)
    w_low, w_high = jnp.where(first_low, w1, w2), jnp.where(first_low, w2, w1)
    pos = jnp.where(low == 0, high - 1, jnp.where(low == 1, jnp.where(high == 2, 4, 3), 5))
    swap = low == 2
    bucket = best * PAIRS_PER_GROUP + pos
    return jnp.concatenate([jnp.where(swap, w_high, w_low), jnp.where(swap, w_low, w_high)], axis=0), bucket


def _router_logits(w_t, h):
    w_hi = w_t.astype(BF16)
    w_r = w_t - w_hi.astype(F32)
    w_mid = w_r.astype(BF16)
    w_lo = (w_r - w_mid.astype(F32)).astype(BF16)
    h_hi = h.astype(BF16)
    h_lo = (h - h_hi.astype(F32)).astype(BF16)
    dims = (((1,), (1,)), ((), ()))
    a = lax.dot_general(jnp.concatenate([w_hi, w_mid, w_lo], axis=0), h_hi, dims, preferred_element_type=F32)
    b = lax.dot_general(jnp.concatenate([w_hi, w_mid], axis=0), h_lo, dims, preferred_element_type=F32)
    e = w_t.shape[0]
    return a[:e] + a[e:2 * e] + a[2 * e:] + b[:e] + b[e:]


def _post_kernel(x_ref, at_ref, yc_ref, wglu_ref, gs_ref, wo_ref, g1_ref, n2_ref, sc_ref, sh_ref,
                 wrt_ref, rb_ref, x1_ref, h2_ref, cw_ref, gid_ref, y_scr):
    nchunk = y_scr.shape[1] // SSM_CHUNK
    for s in range(SSM_CHUNK):
        for j in range(SSM_COLS):
            y_scr[j, pl.ds(s, nchunk, stride=SSM_CHUNK), :] = yc_ref[j, :, s * LANES:(s + 1) * LANES].astype(F32)
    tm = x_ref.shape[0]
    sub = tm // POST_SPLIT
    for part in range(POST_SPLIT):
        rows = slice(part * sub, (part + 1) * sub)
        yg = jax.nn.gelu(jnp.concatenate([y_scr[j, rows, :] for j in range(SSM_COLS)], axis=1))
        z = yg * jax.nn.sigmoid(jnp.dot(yg.astype(BF16), wglu_ref[...], preferred_element_type=F32))
        zn = _rms(z, gs_ref[...]).astype(BF16)
        o = (jnp.dot(at_ref[rows, :], wo_ref[:ATTN_WIDTH, :], preferred_element_type=F32)
             + jnp.dot(zn, wo_ref[ATTN_WIDTH:, :], preferred_element_type=F32))
        x1 = x_ref[rows, :] + g1_ref[...] * o
        x1_ref[rows, :] = x1
        h2 = _rms(x1, n2_ref[...]) * (1.0 + sc_ref[...]) + sh_ref[...]
        h2_ref[rows, :] = h2
        logits = _router_logits(wrt_ref[...], h2)
        cw, bucket = _route(logits, rb_ref[...])
        cw_ref[:, rows] = cw
        gid_ref[:, rows] = bucket


def _post(x, attn, yc, w_glu, ssm_g, w_out, g1, n2g, sc2, sh2, w_router_t, router_bias, seq):
    t, d = x.shape
    tm = min(TOKEN_TILE, seq)
    per_b = seq // tm
    tok = lambda w: pl.BlockSpec((tm, w), lambda i: (i, 0))
    const = lambda a: pl.BlockSpec(a.shape, lambda i: (0,) * a.ndim)
    per_batch = pl.BlockSpec((None, 1, d), lambda i: (i // per_b, 0, 0))
    col = lambda r: pl.BlockSpec((r, tm), lambda i: (0, i))
    chunked = pl.BlockSpec((SSM_COLS, tm // SSM_CHUNK, CHUNK_LANES), lambda i: (0, i, 0))
    return pl.pallas_call(
        _post_kernel,
        grid=(t // tm,),
        in_specs=[tok(d), tok(ATTN_WIDTH), chunked, const(w_glu), const(ssm_g),
                  const(w_out), per_batch, const(n2g), per_batch, per_batch, const(w_router_t), const(router_bias)],
        out_specs=[tok(d), tok(d), col(2), col(1)],
        out_shape=[jax.ShapeDtypeStruct((t, d), F32), jax.ShapeDtypeStruct((t, d), F32),
                   jax.ShapeDtypeStruct((2, t), F32), jax.ShapeDtypeStruct((1, t), jnp.int32)],
        scratch_shapes=[pltpu.VMEM((SSM_COLS, tm, LANES), F32)],
        compiler_params=_params(("parallel",), VMEM_LIMIT),
        name="post_mix",
    )(x, attn, yc, w_glu, ssm_g, w_out, g1, n2g, sc2, sh2, w_router_t, router_bias)


def _moe_kernel(kind_ref, rb_ref, bk_ref, pa_ref, pb_ref, first_ref, cast_ref, cpos_ref, pe_ref, offs_ref,
                x_ref, cw_ref, wg_ref, wu_ref, wd_ref, o_ref, wg_s, wu_s, wd_s):
    s = pl.program_id(0)

    @pl.when(cast_ref[s] == 1)
    def _():
        slot = cpos_ref[s]
        wg_s[slot] = wg_ref[...].astype(BF16)
        wu_s[slot] = wu_ref[...].astype(BF16)
        wd_s[slot] = wd_ref[...].astype(BF16)

    @pl.when(kind_ref[s] == STEP_ITEM)
    def _():
        bucket = bk_ref[s]
        base = rb_ref[s] * MOE_ROWS
        lo_row, hi_row = offs_ref[bucket] - base, offs_ref[bucket + 1] - base
        half = MOE_ROWS // 2
        slots = (pa_ref[s], pb_ref[s])

        def run(r0, r1):
            rows = r0 + lax.broadcasted_iota(jnp.int32, (r1 - r0, 1), 0)
            cw = jnp.where((rows >= lo_row) & (rows < hi_row), cw_ref[r0:r1, :], 0.0)
            x = x_ref[r0:r1, :].astype(BF16)
            y = None
            for k in range(2):
                gate = jnp.dot(x, wg_s[slots[k]], preferred_element_type=F32)
                up = jnp.dot(x, wu_s[slots[k]], preferred_element_type=F32)
                act = (gate * jax.nn.sigmoid(gate) * up * cw[:, k:k + 1]).astype(BF16)
                yk = jnp.dot(act, wd_s[slots[k]], preferred_element_type=F32)
                y = yk if y is None else y + yk

            @pl.when(first_ref[s] == 1)
            def _():
                o_ref[r0:r1, :] = y
                for z0, z1 in ((0, r0), (r1, MOE_ROWS)):
                    if z1 > z0:
                        o_ref[z0:z1, :] = jnp.zeros((z1 - z0, o_ref.shape[1]), F32)

            @pl.when(first_ref[s] == 0)
            def _():
                o_ref[r0:r1, :] += y

        needs_lower, needs_upper = lo_row < half, hi_row > half
        pl.when(needs_lower & needs_upper)(lambda: run(0, MOE_ROWS))
        pl.when(needs_lower & jnp.logical_not(needs_upper))(lambda: run(0, half))
        pl.when(jnp.logical_not(needs_lower) & needs_upper)(lambda: run(half, MOE_ROWS))


def _moe_plan_kernel(offs_ref, kind_ref, rb_ref, bk_ref, pa_ref, pb_ref, first_ref, cast_ref, cpos_ref, pe_ref,
                     irb, ibk, *, n_steps_max):
    i32 = jnp.int32
    ng, epg, ppg = N_EXPERT_GROUPS, EXPERTS_PER_GROUP, PAIRS_PER_GROUP

    shift = MOE_ROWS.bit_length() - 1

    def bucket_body(bk, cnt):
        a, b = offs_ref[bk], offs_ref[bk + 1]
        first_blk = lax.shift_right_logical(a, shift)
        n_blk = jnp.where(b > a, lax.shift_right_logical(b - 1, shift) - first_blk + 1, 0)

        def block_body(j, cnt):
            irb[cnt] = first_blk + j
            ibk[cnt] = bk
            return cnt + 1

        return lax.fori_loop(0, n_blk, block_body, cnt)

    n_items = lax.fori_loop(0, ng * ppg, bucket_body, i32(0))

    def count_body(i, m):
        g = ibk[i] // ppg
        return tuple(m[k] + (g == k).astype(i32) for k in range(ng))

    m = lax.fori_loop(0, n_items, count_body, (i32(0),) * ng)

    def next_group(g):
        nxt = i32(-1)
        for k in range(ng - 1, 0, -1):
            nxt = jnp.where((k > g) & (m[k] > 0), k, nxt)
        return nxt

    def emit(s, kind, rb, bk, pa, pb, first, cast, cpos, pe):
        kind_ref[s], rb_ref[s], bk_ref[s], pa_ref[s], pb_ref[s] = kind, rb, bk, pa, pb
        first_ref[s], cast_ref[s], cpos_ref[s], pe_ref[s] = first, cast, cpos, pe

    def item_body(i, carry):
        s, gcur, parity, q, last_pe, last_rb = carry
        rb, bk = irb[i], ibk[i]
        g, pos = bk // ppg, bk % ppg
        new = g != gcur
        started = gcur >= 0
        loaders = jnp.where(new, jnp.where(started, jnp.maximum(epg - q, 0), epg), 0)
        parity = jnp.where(new & started, 1 - parity, parity)
        q = jnp.where(new, 0, q)
        for j in range(epg):
            on = j >= epg - loaders
            emit(s, STEP_LOAD, rb, bk, 0, 0, 0, 1, parity * epg + j, epg * g + j)
            last_pe = jnp.where(on, epg * g + j, last_pe)
            s = s + on.astype(i32)
        nxt = next_group(g)
        pre = (q < epg) & (nxt >= 0)
        pe = jnp.where(pre, epg * nxt + q, last_pe)
        slot_a, slot_b = i32(PAIR_SLOTS[0][0]), i32(PAIR_SLOTS[0][1])
        for p in range(1, ppg):
            slot_a = jnp.where(pos == p, PAIR_SLOTS[p][0], slot_a)
            slot_b = jnp.where(pos == p, PAIR_SLOTS[p][1], slot_b)
        emit(s, STEP_ITEM, rb, bk, parity * epg + slot_a, parity * epg + slot_b, (rb != last_rb).astype(i32),
             pre.astype(i32), (1 - parity) * epg + q, pe)
        return s + 1, g, parity, q + 1, pe, rb

    s, _, _, _, last_pe, last_rb = lax.fori_loop(
        0, n_items, item_body, (i32(0), i32(-1), i32(0), i32(0), i32(0), i32(-1)))
    last_bk = ibk[jnp.maximum(n_items - 1, 0)]

    def pad_body(s, _):
        emit(s, STEP_PAD, last_rb, last_bk, 0, 0, 0, 0, 0, last_pe)
        return 0

    lax.fori_loop(s, n_steps_max, pad_body, 0)


def _moe_steps(bucket, t):
    i32 = jnp.int32
    nbk = N_EXPERT_GROUPS * PAIRS_PER_GROUP
    order = jnp.argsort(bucket, stable=True).astype(i32)
    counts = jnp.sum((bucket[None, :] == jnp.arange(nbk, dtype=i32)[:, None]).astype(i32), axis=1)
    offs = jnp.concatenate([jnp.zeros((1,), i32), jnp.cumsum(counts).astype(i32)])
    assert MOE_ROWS & (MOE_ROWS - 1) == 0
    n_items_max = t // MOE_ROWS + nbk - 1
    n_steps_max = n_items_max + N_EXPERTS
    smem = pl.BlockSpec(memory_space=pltpu.SMEM)
    tables = pl.pallas_call(
        functools.partial(_moe_plan_kernel, n_steps_max=n_steps_max),
        in_specs=[smem],
        out_specs=[smem] * 9,
        out_shape=[jax.ShapeDtypeStruct((n_steps_max,), i32)] * 9,
        scratch_shapes=[pltpu.SMEM((n_items_max + 1,), i32)] * 2,
        name="moe_plan",
    )(offs)
    return order, (*tables, offs), n_steps_max


def _moe(xs, cws, w_gate, w_up, w_down, layer, tables, n_steps_max):
    t, d = xs.shape
    ff = w_gate.shape[3]
    w_map = lambda s, kind, rb, bk, pa, pb, fi, ca, cp, pe, of: (layer, pe[s], 0, 0)
    row_map = lambda s, kind, rb, *_: (rb[s], 0)
    nres = 2 * EXPERTS_PER_GROUP
    grid_spec = pltpu.PrefetchScalarGridSpec(
        num_scalar_prefetch=len(tables),
        grid=(n_steps_max,),
        in_specs=[pl.BlockSpec((MOE_ROWS, d), row_map), pl.BlockSpec((MOE_ROWS, 2), row_map),
                  pl.BlockSpec((None, None, d, ff), w_map), pl.BlockSpec((None, None, d, ff), w_map),
                  pl.BlockSpec((None, None, ff, d), w_map)],
        out_specs=pl.BlockSpec((MOE_ROWS, d), row_map),
        scratch_shapes=[pltpu.VMEM((nres, d, ff), BF16), pltpu.VMEM((nres, d, ff), BF16),
                        pltpu.VMEM((nres, ff, d), BF16)],
    )
    return pl.pallas_call(
        _moe_kernel,
        grid_spec=grid_spec,
        out_shape=jax.ShapeDtypeStruct((t, d), F32),
        compiler_params=_params(("arbitrary",), MOE_VMEM_LIMIT),
        name="moe_grouped",
    )(*tables, xs, cws, w_gate, w_up, w_down)


def _take_rows(a, idx):
    return a.at[idx].get(mode="promise_in_bounds", unique_indices=True)


def _final_kernel(x_ref, y_ref, g_ref, o_ref):
    o_ref[...] = x_ref[...] + g_ref[...] * y_ref[...]


def _final(x1, y, g2, seq):
    t, d = x1.shape
    tm = min(TOKEN_TILE, seq)
    per_b = seq // tm
    tok = lambda w: pl.BlockSpec((tm, w), lambda i: (i, 0))
    return pl.pallas_call(
        _final_kernel,
        grid=(t // tm,),
        in_specs=[tok(d), tok(d), pl.BlockSpec((None, 1, d), lambda i: (i // per_b, 0, 0))],
        out_specs=tok(d),
        out_shape=jax.ShapeDtypeStruct((t, d), F32),
        compiler_params=_params(("parallel",)),
        name="final_residual",
    )(x1, y, g2)


def kernel(x, c, positions, ada_w, ada_b, norm1_g, w_in, q_norm_g, k_norm_g, attn_sink, lam_re, lam_im, ssm_b_re, ssm_b_im, ssm_c_re, ssm_c_im, ssm_d, ssm_log_dt, w_glu, attn_out_g, ssm_out_g, w_out, norm2_g, w_router, router_bias, w_exp_gate, w_exp_up, w_exp_down):
    batch, seq, d = x.shape
    depth = ada_w.shape[0]
    t = batch * seq
    assert seq % ATTN_BLOCK == 0 and seq % SSM_CHUNK == 0 and t % MOE_ROWS == 0

    mod = _adaln_mod(c, ada_w, ada_b).reshape(depth, 6, batch, 1, d)
    cos, sin = _rope_tables(positions)
    head_sum, rot = _rope_constants()
    bias = _attn_bias()
    w_router_t = w_router.T
    s5_mats = _s5_prep(lam_re, lam_im, ssm_b_re, ssm_b_im, ssm_c_re, ssm_c_im, ssm_d, ssm_log_dt)
    router_bias_col = router_bias.reshape(N_EXPERTS, 1)

    xf = x.reshape(t, d)
    res = None
    for l in range(depth):
        sh1, sc1, g1, sh2, sc2, g2 = (mod[l, j] for j in range(6))
        qg = (jnp.tile(q_norm_g[l], N_Q_HEADS) * HEAD_DIM ** -0.5).reshape(1, ATTN_WIDTH)
        kg = jnp.tile(k_norm_g[l], N_KV_HEADS).reshape(1, KV_WIDTH)
        outs = _inproj(xf, res, sc1, sh1, norm1_g[l].reshape(1, d), w_in[l].astype(BF16), qg, kg, head_sum, rot,
                       cos, sin, seq)
        if res is None:
            q, kx, vx, uc = outs
        else:
            q, kx, vx, uc, xf = outs
        attn = _attention(q, kx, vx, attn_sink[l], attn_out_g[l].reshape(1, ATTN_WIDTH), bias, batch, seq)
        yc = _s5_scan(uc, s5_mats, l, seq // SSM_CHUNK, batch)
        x1, h2, cw, gid = _post(xf, attn, yc, w_glu[l].astype(BF16), ssm_out_g[l].reshape(1, SSM_WIDTH),
                                 w_out[l].astype(BF16), g1, norm2_g[l].reshape(1, d), sc2, sh2, w_router_t,
                                 router_bias_col, seq)
        order, tables, n_steps_max = _moe_steps(gid.reshape(t), t)
        y_sorted = _moe(_take_rows(h2, order), _take_rows(cw.T, order), w_exp_gate, w_exp_up, w_exp_down, l,
                        tables, n_steps_max)
        y = _take_rows(y_sorted, jnp.argsort(order).astype(jnp.int32))
        xf, res = x1, (y, g2)
    y, g2 = res
    return _final(xf, y, g2, seq).reshape(batch, seq, d)
```

```python
import functools

import numpy as np
import jax
import jax.numpy as jnp
from jax import lax
from jax.experimental import pallas as pl
from jax.experimental.pallas import tpu as pltpu

F32 = jnp.float32
BF16 = jnp.bfloat16

HEAD_DIM = 64
N_Q_HEADS = 8
N_KV_HEADS = 2
Q_PER_KV = N_Q_HEADS // N_KV_HEADS
ATTN_WIDTH = N_Q_HEADS * HEAD_DIM
KV_WIDTH = N_KV_HEADS * HEAD_DIM
ATTN_BLOCK = 128
ATTN_Q_TILE = 2048
ROPE_THETA = 10000.0
ROPE_SLAB = 256
LANES = 128
SSM_GROUP_CH = 16
SSM_GROUPS = 32
SSM_WIDTH = SSM_GROUPS * SSM_GROUP_CH
SSM_STATE = 64
SSM_CHUNK = 16
SSM_COLS = SSM_WIDTH // LANES
COL_GROUPS = LANES // SSM_GROUP_CH
COL_STATE = COL_GROUPS * SSM_STATE
CHUNK_LANES = SSM_CHUNK * LANES
SSM_NSPLIT = 4
N_EXPERTS = 16
N_EXPERT_GROUPS = 4
EXPERTS_PER_GROUP = N_EXPERTS // N_EXPERT_GROUPS
PAIRS_PER_GROUP = EXPERTS_PER_GROUP * (EXPERTS_PER_GROUP - 1) // 2
PAIR_SLOTS = ((0, 1), (0, 2), (0, 3), (1, 3), (1, 2), (3, 2))
EPS = 1e-6
MASK_BIAS = -1e30

TOKEN_TILE = 1024
POST_SPLIT = 1
MOE_ROWS = 256
VMEM_LIMIT = 48 * 1024 * 1024
MOE_VMEM_LIMIT = 56 * 1024 * 1024
STEP_PAD, STEP_LOAD, STEP_ITEM = 0, 1, 2


def _params(sem, vmem=None):
    return pltpu.CompilerParams(dimension_semantics=sem, vmem_limit_bytes=vmem)


def _rms(x, g):
    return x * lax.rsqrt(jnp.mean(x * x, axis=-1, keepdims=True) + EPS) * g


def _mod_kernel(c_ref, w_ref, b_ref, o_ref):
    c = c_ref[...]
    s = c * jax.nn.sigmoid(c)
    o_ref[...] = jnp.dot(s.astype(BF16), w_ref[...].astype(BF16), preferred_element_type=F32) + b_ref[...]


def _adaln_mod(c, ada_w, ada_b):
    depth, d, d6 = ada_w.shape
    nb = c.shape[0]
    n6 = d6 // d
    return pl.pallas_call(
        _mod_kernel,
        grid=(depth, n6),
        in_specs=[pl.BlockSpec((nb, d), lambda l, j: (0, 0)),
                  pl.BlockSpec((None, d, d), lambda l, j: (l, 0, j)),
                  pl.BlockSpec((None, None, 1, d), lambda l, j: (l, j, 0, 0))],
        out_specs=pl.BlockSpec((None, None, nb, d), lambda l, j: (l, j, 0, 0)),
        out_shape=jax.ShapeDtypeStruct((depth, n6, nb, d), F32),
        compiler_params=_params(("arbitrary", "arbitrary"), VMEM_LIMIT),
        name="adaln_mod",
    )(c, ada_w, ada_b.reshape(depth, n6, 1, d))


def _rope_kernel(pos_ref, freq_ref, cos_ref, sin_ref):
    ang = pos_ref[...].astype(F32) * freq_ref[...]
    cos_ref[...] = jnp.cos(ang)
    sin_ref[...] = jnp.sin(ang)


def _rope_tables(positions):
    half = HEAD_DIM // 2
    t = positions.size
    per_row = LANES // half
    rows = t // per_row
    pos_rep = jnp.repeat(positions.reshape(rows, per_row), half, axis=1)
    freq = (ROPE_THETA ** (-np.arange(half, dtype=np.float64) / half)).astype(np.float32)
    freq_row = jnp.asarray(np.tile(freq, per_row)[None, :])
    blk = min(rows, 512)
    cos, sin = pl.pallas_call(
        _rope_kernel,
        grid=(rows // blk,),
        in_specs=[pl.BlockSpec((blk, LANES), lambda i: (i, 0)),
                  pl.BlockSpec((1, LANES), lambda i: (0, 0))],
        out_specs=[pl.BlockSpec((blk, LANES), lambda i: (i, 0))] * 2,
        out_shape=[jax.ShapeDtypeStruct((rows, LANES), F32)] * 2,
        compiler_params=_params(("arbitrary",)),
        name="rope_tables",
    )(pos_rep, freq_row)
    widen = lambda a: jnp.tile(a.reshape(t, half), (1, per_row))
    return widen(cos), widen(sin)


def _rope_constants():
    lane = np.arange(ROPE_SLAB)
    head_sum = (lane[:, None] // HEAD_DIM == lane[None, :] // HEAD_DIM).astype(np.float32)
    half = HEAD_DIM // 2
    rot = np.zeros((ROPE_SLAB, ROPE_SLAB), np.float32)
    for d in range(ROPE_SLAB):
        if d % HEAD_DIM < half:
            rot[d + half, d] = -1.0
        else:
            rot[d - half, d] = 1.0
    return jnp.asarray(head_sum, BF16), jnp.asarray(rot, BF16)


def _inproj_kernel(*refs, has_res):
    if has_res:
        (x_ref, y_ref, g2_ref, sc_ref, sh_ref, n1_ref, w_ref, qg_ref, kg_ref, hs_ref, rot_ref,
         cos_ref, sin_ref, q_ref, k_ref, v_ref, uc_ref, xo_ref, u_scr) = refs
        x = x_ref[...] + g2_ref[...] * y_ref[...]
        xo_ref[...] = x
    else:
        (x_ref, sc_ref, sh_ref, n1_ref, w_ref, qg_ref, kg_ref, hs_ref, rot_ref,
         cos_ref, sin_ref, q_ref, k_ref, v_ref, uc_ref, u_scr) = refs
        x = x_ref[...]
    h = _rms(x, n1_ref[...]) * (1.0 + sc_ref[...]) + sh_ref[...]
    proj = jnp.dot(h.astype(BF16), w_ref[...], preferred_element_type=F32)
    q = proj[:, :ATTN_WIDTH]
    k = proj[:, ATTN_WIDTH:ATTN_WIDTH + KV_WIDTH]
    v = proj[:, ATTN_WIDTH + KV_WIDTH:ATTN_WIDTH + 2 * KV_WIDTH]
    cos = cos_ref[...]
    sin = sin_ref[...]
    reps = ATTN_WIDTH // LANES
    cos_q = jnp.concatenate([cos] * reps, axis=1)
    sin_q = jnp.concatenate([sin] * reps, axis=1)

    def head_norm_rope(t, gain, c, s):
        outs = []
        for lo in range(0, t.shape[1], ROPE_SLAB):
            wd = min(ROPE_SLAB, t.shape[1] - lo)
            ts, lanes = t[:, lo:lo + wd], slice(lo, lo + wd)
            ssq = jnp.dot((ts * ts).astype(BF16), hs_ref[:wd, :wd], preferred_element_type=F32)
            tn = (ts * lax.rsqrt(ssq * (1.0 / HEAD_DIM) + EPS) * gain[:, lanes]).astype(BF16)
            tr = jnp.dot(tn, rot_ref[:wd, :wd], preferred_element_type=F32)
            outs.append(tn.astype(F32) * c[:, lanes] + tr * s[:, lanes])
        return outs[0] if len(outs) == 1 else jnp.concatenate(outs, axis=1)

    qo = head_norm_rope(q, qg_ref[...], cos_q, sin_q)
    ko = head_norm_rope(k, kg_ref[...], cos, sin)
    q_ref[...] = qo.astype(BF16)
    k_ref[...] = jnp.concatenate([ko, pltpu.roll(ko, HEAD_DIM, axis=1)], axis=1).astype(BF16)
    v_ref[...] = jnp.concatenate([v, pltpu.roll(v, HEAD_DIM, axis=1)], axis=1).astype(BF16)
    u0 = ATTN_WIDTH + 2 * KV_WIDTH
    nchunk = u_scr.shape[1] // SSM_CHUNK
    for j in range(SSM_COLS):
        u_scr[j] = proj[:, u0 + j * LANES:u0 + (j + 1) * LANES]
    for s in range(SSM_CHUNK):
        for j in range(SSM_COLS):
            lanes = slice(s * SSM_WIDTH + j * LANES, s * SSM_WIDTH + (j + 1) * LANES)
            uc_ref[:, lanes] = u_scr[j, pl.ds(s, nchunk, stride=SSM_CHUNK), :].astype(BF16)


def _inproj(x, res, sc1, sh1, n1g, w_in, qg, kg, head_sum, rot, cos, sin, seq):
    t, d = x.shape
    tm = min(TOKEN_TILE, seq)
    per_b = seq // tm
    in_width = w_in.shape[1]
    tok = lambda w: pl.BlockSpec((tm, w), lambda i: (i, 0))
    const = lambda a: pl.BlockSpec(a.shape, lambda i: (0,) * a.ndim)
    per_batch = pl.BlockSpec((None, 1, d), lambda i: (i // per_b, 0, 0))
    chunked = pl.BlockSpec((tm // SSM_CHUNK, SSM_CHUNK * SSM_WIDTH), lambda i: (i % per_b, i // per_b))
    ins, specs = [x], [tok(d)]
    if res is not None:
        y_prev, g2_prev = res
        ins += [y_prev, g2_prev]
        specs += [tok(d), per_batch]
    ins += [sc1, sh1, n1g, w_in, qg, kg, head_sum, rot, cos, sin]
    specs += [per_batch, per_batch, const(n1g), const(w_in), const(qg), const(kg), const(head_sum), const(rot),
              tok(LANES), tok(LANES)]
    out_shape = [jax.ShapeDtypeStruct((t, ATTN_WIDTH), BF16), jax.ShapeDtypeStruct((t, 2 * KV_WIDTH), BF16),
                 jax.ShapeDtypeStruct((t, 2 * KV_WIDTH), BF16),
                 jax.ShapeDtypeStruct((seq // SSM_CHUNK, (t // seq) * SSM_CHUNK * SSM_WIDTH), BF16)]
    out_specs = [tok(ATTN_WIDTH), tok(2 * KV_WIDTH), tok(2 * KV_WIDTH), chunked]
    if res is not None:
        out_shape.append(jax.ShapeDtypeStruct((t, d), F32))
        out_specs.append(tok(d))
    assert in_width == ATTN_WIDTH + 2 * KV_WIDTH + SSM_WIDTH
    return pl.pallas_call(
        functools.partial(_inproj_kernel, has_res=res is not None),
        grid=(t // tm,),
        in_specs=specs,
        out_specs=out_specs,
        out_shape=out_shape,
        scratch_shapes=[pltpu.VMEM((SSM_COLS, tm, LANES), F32)],
        compiler_params=_params(("parallel",), VMEM_LIMIT),
        name="inproj",
    )(*ins)


def _attn_kernel(sink_ref, q_ref, kc_ref, kp_ref, vc_ref, vp_ref, bias_ref, g_ref, o_ref):
    nsub = q_ref.shape[0] // ATTN_BLOCK
    kk = jnp.concatenate([kp_ref[...], kc_ref[...]], axis=0)
    vv = jnp.concatenate([vp_ref[...], vc_ref[...]], axis=0)
    low = lax.broadcasted_iota(jnp.int32, (kk.shape[0], KV_WIDTH), 1) < HEAD_DIM
    zero = jnp.zeros((kk.shape[0], KV_WIDTH), BF16)

    def variants(a):
        nat, swp = a[:, :KV_WIDTH], a[:, KV_WIDTH:]
        return {(0, 0): jnp.where(low, nat, zero), (0, 1): jnp.where(low, zero, swp),
                (1, 0): jnp.where(low, swp, zero), (1, 1): jnp.where(low, zero, nat)}

    kvar, vvar = variants(kk), variants(vv)
    band = bias_ref[1]
    first = bias_ref[jnp.minimum(pl.program_id(1), 1)]
    upper = lax.broadcasted_iota(jnp.int32, (2 * ATTN_BLOCK, 1), 0) < ATTN_BLOCK
    for j in range(nsub):
        bias = first if j == 0 else band
        bias2 = jnp.concatenate([bias, bias], axis=0)
        keys = slice(j * ATTN_BLOCK, (j + 2) * ATTN_BLOCK)
        qrows = slice(j * ATTN_BLOCK, (j + 1) * ATTN_BLOCK)
        tiles = [None] * (N_Q_HEADS // 2)
        for kv in range(N_KV_HEADS):
            for half in range(2):
                pairs = (2 * kv, 2 * kv + 1)
                heads = (2 * pairs[0] + half, 2 * pairs[1] + half)
                qs = jnp.concatenate([q_ref[qrows, p * LANES:(p + 1) * LANES] for p in pairs], axis=0)
                s = lax.dot_general(qs, kvar[(kv, half)][keys], (((1,), (1,)), ((), ())),
                                    preferred_element_type=F32) + bias2
                sink = jnp.where(upper, sink_ref[heads[0]], sink_ref[heads[1]])
                m = jnp.maximum(jnp.max(s, axis=-1, keepdims=True), sink)
                p = jnp.exp(s - m)
                denom = jnp.sum(p, axis=-1, keepdims=True) + jnp.exp(sink - m)
                o = jnp.dot(p.astype(BF16), vvar[(kv, half)][keys], preferred_element_type=F32) * (1.0 / denom)
                for r, pr in enumerate(pairs):
                    part = o[r * ATTN_BLOCK:(r + 1) * ATTN_BLOCK]
                    tiles[pr] = part if tiles[pr] is None else tiles[pr] + part
        a = jnp.concatenate(tiles, axis=1)
        o_ref[qrows, :] = _rms(a, g_ref[...]).astype(BF16)


def _attn_bias():
    qi = np.arange(ATTN_BLOCK)[:, None]
    sj = np.arange(2 * ATTN_BLOCK)[None, :]
    diff = qi + ATTN_BLOCK - sj
    band = (diff >= 0) & (diff < ATTN_BLOCK)
    first = band & (sj >= ATTN_BLOCK)
    return jnp.asarray(np.where(np.stack([first, band]), 0.0, MASK_BIAS).astype(np.float32))


def _attention(q, kx, vx, sink, out_g, bias, batch, seq):
    t = q.shape[0]
    qb = min(ATTN_Q_TILE, seq)
    nsub = qb // ATTN_BLOCK
    nq = seq // qb
    nb = seq // ATTN_BLOCK
    cur = lambda w: pl.BlockSpec((qb, w), lambda b, n, s: (b * nq + n, 0))
    prev = lambda w: pl.BlockSpec((ATTN_BLOCK, w), lambda b, n, s: (b * nb + jnp.maximum(n * nsub - 1, 0), 0))
    grid_spec = pltpu.PrefetchScalarGridSpec(
        num_scalar_prefetch=1,
        grid=(batch, nq),
        in_specs=[cur(ATTN_WIDTH), cur(2 * KV_WIDTH), prev(2 * KV_WIDTH), cur(2 * KV_WIDTH), prev(2 * KV_WIDTH),
                  pl.BlockSpec(bias.shape, lambda b, n, s: (0, 0, 0)),
                  pl.BlockSpec((1, ATTN_WIDTH), lambda b, n, s: (0, 0))],
        out_specs=cur(ATTN_WIDTH),
    )
    return pl.pallas_call(
        _attn_kernel,
        grid_spec=grid_spec,
        out_shape=jax.ShapeDtypeStruct((t, ATTN_WIDTH), BF16),
        compiler_params=_params(("parallel", "arbitrary")),
        name="swa_attention",
    )(sink, q, kx, kx, vx, vx, bias, out_g)


def _spread(x, expander3):
    hi = x.astype(BF16)
    r1 = x - hi.astype(F32)
    mid = r1.astype(BF16)
    lo = (r1 - mid.astype(F32)).astype(BF16)
    return jnp.dot(jnp.concatenate([hi, mid, lo], axis=1), expander3, preferred_element_type=F32)


def _s5_prep_kernel(lr_re_ref, lr_im_ref, ldt_ref, bt_re_ref, bt_im_ref, ct_re_ref, ct_im_ref,
                    d_ref, lcol_re_ref, lcol_im_ref, ldtcol_ref, exp_ref, exph_ref, expt_ref, expw_ref,
                    t_ref, w_ref, v_ref, la_ref, lb_ref):
    hi = lax.Precision.HIGHEST
    nl = SSM_CHUNK
    low = lax.broadcasted_iota(jnp.int32, (1, 2 * SSM_STATE), 1) < SSM_STATE
    row_low = lax.broadcasted_iota(jnp.int32, (2 * SSM_STATE, 1), 0) < SSM_STATE
    jcol = lax.broadcasted_iota(jnp.int32, (nl, 1), 0).astype(F32)
    kt_lane = lax.broadcasted_iota(jnp.int32, (SSM_GROUP_CH, nl * SSM_GROUP_CH), 1)
    kt_row = lax.broadcasted_iota(jnp.int32, (SSM_GROUP_CH, nl * SSM_GROUP_CH), 0)

    w_all, v_all, kt_all = [], [], []
    for gm in range(COL_GROUPS):
        dt = jnp.exp(ldt_ref[gm])
        lam_re, lam_im = lr_re_ref[gm], lr_im_ref[gm]
        a_r, th_r = lam_re * dt, lam_im * dt

        er = jnp.exp(jcol * a_r)
        pw_re, pw_im = er * jnp.cos(jcol * th_r), er * jnp.sin(jcol * th_r)

        nr, ni = pw_re[1:2, :] - 1.0, pw_im[1:2, :]
        den = lam_re * lam_re + lam_im * lam_im
        c_re, c_im = (nr * lam_re + ni * lam_im) / den, (ni * lam_re - nr * lam_im) / den
        bt_re, bt_im = bt_re_ref[gm], bt_im_ref[gm]
        bb_re, bb_im = c_re * bt_re - c_im * bt_im, c_re * bt_im + c_im * bt_re

        w_rows = []
        for s in range(nl):
            j = nl - 1 - s
            pr, pi = pw_re[j:j + 1, :], pw_im[j:j + 1, :]
            w_rows.append(jnp.where(low, pr * bb_re - pi * bb_im, pr * bb_im + pi * bb_re))
        w_all.append(w_rows)

        pw_re_t, pw_im_t = pw_re.T, pw_im.T
        pc, ps = _spread(pw_re_t, exp_ref[...]), _spread(pw_im_t, exp_ref[...])
        ct_re, ct_im = _spread(ct_re_ref[gm], exph_ref[...]), _spread(ct_im_ref[gm], exph_ref[...])
        a_re, a_im = ct_re * pc - ct_im * ps, ct_re * ps + ct_im * pc
        a_cat = jnp.where(row_low, a_re, -a_im)
        l1_re, l1_im = pw_re_t[:, 1:2], pw_im_t[:, 1:2]
        v_re, v_im = a_re * l1_re - a_im * l1_im, a_re * l1_im + a_im * l1_re
        v_all.append(jnp.where(row_low, v_re, -v_im))

        kt = jnp.dot(jnp.where(low, bb_re, bb_im), a_cat, precision=hi, preferred_element_type=F32)
        kt_all.append(kt + jnp.where(kt_lane == kt_row, d_ref[gm], 0.0))

    def expand(stacked, expander, row_group, lane_group):
        wide = jnp.dot(stacked.astype(BF16), expander, preferred_element_type=F32)
        r = lax.broadcasted_iota(jnp.int32, wide.shape, 0)
        c = lax.broadcasted_iota(jnp.int32, wide.shape, 1)
        return jnp.where(row_group(r) == lane_group(c), wide, 0.0).astype(BF16)

    chan_group = lambda i: (i >> 4) & (COL_GROUPS - 1)
    state_group = lambda i: (i >> 6) & (COL_GROUPS - 1)

    bd = expand(jnp.concatenate(kt_all, axis=0), expt_ref[...], chan_group, chan_group)
    t_ref[0:LANES, :] = bd
    for s in range(1, nl):
        t_ref[s * LANES:(s + 1) * LANES, :] = jnp.concatenate(
            [jnp.zeros((LANES, s * LANES), BF16), bd[:, :CHUNK_LANES - s * LANES]], axis=1)

    w_stack = jnp.concatenate([w_all[gm][s] for s in range(nl) for gm in range(COL_GROUPS)], axis=0)
    w_ref[...] = expand(w_stack, expw_ref[...], chan_group, state_group)

    v_stack = jnp.concatenate([v_all[gm][half * SSM_STATE:(half + 1) * SSM_STATE, :]
                               for half in range(2) for gm in range(COL_GROUPS)], axis=0)
    v_ref[...] = expand(v_stack, expt_ref[...], state_group, chan_group)

    dtc = jnp.exp(ldtcol_ref[...])
    e16 = jnp.exp(nl * lcol_re_ref[...] * dtc)
    ang = nl * lcol_im_ref[...] * dtc
    la_ref[...] = e16 * jnp.cos(ang)
    lb_ref[...] = e16 * jnp.sin(ang)


def _s5_prep(lam_re, lam_im, b_re, b_im, c_re, c_im, d_skip, log_dt):
    g, p, h, nl = SSM_GROUPS, SSM_STATE, SSM_GROUP_CH, SSM_CHUNK
    cg = COL_GROUPS
    nc = lam_re.shape[0] * SSM_COLS
    col = lambda a: a.reshape((nc, cg) + a.shape[2:])
    dup_row = lambda a: col(jnp.tile(a, (1, 1, 2))[:, :, None, :])
    bt = lambda a: col(jnp.tile(jnp.swapaxes(a, 2, 3), (1, 1, 1, 2)))
    ct = lambda a: col(jnp.tile(jnp.swapaxes(a, 2, 3), (1, 1, 2, 1)))
    d_pad = col(jnp.pad(d_skip.reshape(-1, g, 1, h), ((0, 0), (0, 0), (0, 0), (0, nl * h - h))))
    wide = lambda a: a.reshape(nc, 1, cg * p)
    expand = jnp.asarray(np.tile(np.repeat(np.eye(nl, dtype=np.float32), h, axis=1), (3, 1)), BF16)
    expand_h = jnp.asarray(np.tile(np.eye(h, dtype=np.float32), (3, nl)), BF16)
    exp_t = np.zeros((nl, h, nl, cg, h), np.float32)
    exp_w = np.zeros((2, p, 2, cg, p), np.float32)
    for gm in range(cg):
        exp_t[:, :, :, gm, :] = np.eye(nl * h, dtype=np.float32).reshape(nl, h, nl, h)
        exp_w[:, :, :, gm, :] = np.eye(2 * p, dtype=np.float32).reshape(2, p, 2, p)
    exp_t = jnp.asarray(exp_t.reshape(nl * h, CHUNK_LANES), BF16)
    exp_w = jnp.asarray(exp_w.reshape(2 * p, 2 * COL_STATE), BF16)
    blk = lambda *s: pl.BlockSpec((None,) + s, lambda i: (i,) + (0,) * len(s))
    const = lambda a: pl.BlockSpec(a.shape, lambda i: (0,) * a.ndim)
    lw = nl * h
    return pl.pallas_call(
        _s5_prep_kernel,
        grid=(nc,),
        in_specs=[blk(cg, 1, 2 * p), blk(cg, 1, 2 * p), blk(cg, 1, 1),
                  blk(cg, h, 2 * p), blk(cg, h, 2 * p), blk(cg, 2 * p, h), blk(cg, 2 * p, h), blk(cg, 1, lw),
                  blk(1, cg * p), blk(1, cg * p), blk(1, cg * p), const(expand), const(expand_h), const(exp_t),
                  const(exp_w)],
        out_specs=[blk(CHUNK_LANES, CHUNK_LANES), blk(CHUNK_LANES, 2 * COL_STATE), blk(2 * COL_STATE, CHUNK_LANES),
                   blk(1, COL_STATE), blk(1, COL_STATE)],
        out_shape=[jax.ShapeDtypeStruct((nc, CHUNK_LANES, CHUNK_LANES), BF16),
                   jax.ShapeDtypeStruct((nc, CHUNK_LANES, 2 * COL_STATE), BF16),
                   jax.ShapeDtypeStruct((nc, 2 * COL_STATE, CHUNK_LANES), BF16),
                   jax.ShapeDtypeStruct((nc, 1, COL_STATE), F32), jax.ShapeDtypeStruct((nc, 1, COL_STATE), F32)],
        compiler_params=_params(("parallel",), VMEM_LIMIT),
        name="s5_prep",
    )(dup_row(lam_re), dup_row(lam_im), col(log_dt[:, :, None, None]),
      bt(b_re), bt(b_im), ct(c_re), ct(c_im), d_pad, wide(lam_re), wide(lam_im),
      wide(jnp.repeat(log_dt, p, axis=1)), expand, expand_h, exp_t, exp_w)


def _s5_kernel(*refs, nchunks, nb):
    uc_refs = refs[:SSM_CHUNK]
    t_ref, w_ref, v_ref, la_ref, lb_ref, o_ref, ucat_ref, s_ref, xp_ref = refs[SSM_CHUNK:]

    @pl.when(pl.program_id(1) == 0)
    def _():
        for s in range(SSM_CHUNK):
            ucat_ref[:, s * LANES:(s + 1) * LANES] = uc_refs[s][...]
        s_in = jnp.dot(ucat_ref[...], w_ref[...], preferred_element_type=F32)
        nblk = COL_STATE // LANES
        for b in range(2 * nblk):
            s_ref[b] = s_in[:, b * LANES:(b + 1) * LANES]
        lr = [jnp.broadcast_to(la_ref[:, b * LANES:(b + 1) * LANES], (nb, LANES)) for b in range(nblk)]
        li = [jnp.broadcast_to(lb_ref[:, b * LANES:(b + 1) * LANES], (nb, LANES)) for b in range(nblk)]

        def step(c, carry):
            rows = pl.ds(pl.multiple_of(c * nb, nb), nb)
            out = []
            for b in range(nblk):
                re, im = carry[2 * b], carry[2 * b + 1]
                xp_ref[b, rows, :] = re
                xp_ref[nblk + b, rows, :] = im
                out.append(lr[b] * re - li[b] * im + s_ref[b, rows, :])
                out.append(lr[b] * im + li[b] * re + s_ref[nblk + b, rows, :])
            return tuple(out)

        zero = jnp.zeros((nb, LANES), F32)
        lax.fori_loop(0, nchunks, step, (zero,) * (2 * nblk), unroll=4)

    xp = jnp.concatenate([xp_ref[b] for b in range(2 * COL_STATE // LANES)], axis=1).astype(BF16)
    inter = jnp.dot(xp, v_ref[...], preferred_element_type=F32)
    for kk in range(SSM_NSPLIT):
        @pl.when(pl.program_id(1) == kk)
        def _():
            live = (kk + 1) * (CHUNK_LANES // SSM_NSPLIT)
            intra = jnp.dot(ucat_ref[:, :live], t_ref[:live, :], preferred_element_type=F32)
            o_ref[...] = (intra + inter).astype(BF16)


def _s5_scan(uc, mats, layer):
    nchunks = uc.shape[0]
    nb = uc.shape[1] // (SSM_CHUNK * SSM_WIDTH)
    rows = nchunks * nb
    uc = uc.reshape(rows, SSM_CHUNK * SSM_WIDTH)
    c0 = layer * SSM_COLS
    split = CHUNK_LANES // SSM_NSPLIT
    u_spec = lambda s: pl.BlockSpec((rows, LANES), lambda j, k: (0, SSM_COLS * s + j))
    return pl.pallas_call(
        functools.partial(_s5_kernel, nchunks=nchunks, nb=nb),
        grid=(SSM_COLS, SSM_NSPLIT),
        in_specs=[u_spec(s) for s in range(SSM_CHUNK)] + [
            pl.BlockSpec((None, CHUNK_LANES, split), lambda j, k: (c0 + j, 0, k)),
            pl.BlockSpec((None, CHUNK_LANES, 2 * COL_STATE), lambda j, k: (c0 + j, 0, 0)),
            pl.BlockSpec((None, 2 * COL_STATE, split), lambda j, k: (c0 + j, 0, k)),
            pl.BlockSpec((None, 1, COL_STATE), lambda j, k: (c0 + j, 0, 0)),
            pl.BlockSpec((None, 1, COL_STATE), lambda j, k: (c0 + j, 0, 0))],
        out_specs=pl.BlockSpec((None, rows, split), lambda j, k: (j, 0, k)),
        out_shape=jax.ShapeDtypeStruct((SSM_COLS, rows, CHUNK_LANES), BF16),
        scratch_shapes=[pltpu.VMEM((rows, CHUNK_LANES), BF16),
                        pltpu.VMEM((2 * COL_STATE // LANES, rows, LANES), F32),
                        pltpu.VMEM((2 * COL_STATE // LANES, rows, LANES), F32)],
        compiler_params=_params(("parallel", "arbitrary"), VMEM_LIMIT),
        name="s5_scan",
    )(*([uc] * SSM_CHUNK), *mats).reshape(SSM_COLS, nchunks, nb * CHUNK_LANES)


def _route(logits, bias):
    m = jnp.max(logits, axis=0, keepdims=True)
    e = jnp.exp(logits - m)
    probs = e / jnp.sum(e, axis=0, keepdims=True)
    sel = probs + bias
    row = lambda a, i: a[i:i + 1, :]
    best_score, best = None, None
    for grp in range(N_EXPERT_GROUPS):
        a, b, c, d = (row(sel, EXPERTS_PER_GROUP * grp + i) for i in range(EXPERTS_PER_GROUP))
        hab, lab, hcd, lcd = jnp.maximum(a, b), jnp.minimum(a, b), jnp.maximum(c, d), jnp.minimum(c, d)
        top1 = jnp.maximum(hab, hcd)
        top2 = jnp.maximum(jnp.maximum(lab, lcd), jnp.minimum(hab, hcd))
        score = top1 + top2
        if grp == 0:
            best_score, best = score, jnp.zeros(score.shape, jnp.int32)
        else:
            better = score > best_score
            best = jnp.where(better, grp, best)
            best_score = jnp.where(better, score, best_score)

    def pick(a, i):
        out = row(a, i)
        for grp in range(1, N_EXPERT_GROUPS):
            out = jnp.where(best == grp, row(a, EXPERTS_PER_GROUP * grp + i), out)
        return out

    s_in = [pick(sel, i) for i in range(EXPERTS_PER_GROUP)]
    p_in = [pick(probs, i) for i in range(EXPERTS_PER_GROUP)]
    neg = jnp.full(s_in[0].shape, -jnp.inf, F32)

    def argmax_first(vals):
        idx, val = jnp.zeros(vals[0].shape, jnp.int32), vals[0]
        for i in range(1, len(vals)):
            better = vals[i] > val
            idx = jnp.where(better, i, idx)
            val = jnp.where(better, vals[i], val)
        return idx

    i1 = argmax_first(s_in)
    i2 = argmax_first([jnp.where(i1 == i, neg, s_in[i]) for i in range(EXPERTS_PER_GROUP)])
    zero = jnp.zeros(p_in[0].shape, F32)
    g1 = sum(jnp.where(i1 == i, p_in[i], zero) for i in range(EXPERTS_PER_GROUP))
    g2 = sum(jnp.where(i2 == i, p_in[i], zero) for i in range(EXPERTS_PER_GROUP))
    tot = g1 + g2
    w1, w2 = g1 / tot, g2 / tot
    first_low = i1 < i2
    low, high = jnp.minimum(i1, i2), jnp.maximum(i1, i2)
    w_low, w_high = jnp.where(first_low, w1, w2), jnp.where(first_low, w2, w1)
    pos = jnp.where(low == 0, high - 1, jnp.where(low == 1, jnp.where(high == 2, 4, 3), 5))
    swap = low == 2
    bucket = best * PAIRS_PER_GROUP + pos
    return jnp.concatenate([jnp.where(swap, w_high, w_low), jnp.where(swap, w_low, w_high)], axis=0), bucket


def _router_logits(w_t, h):
    w_hi = w_t.astype(BF16)
    w_r = w_t - w_hi.astype(F32)
    w_mid = w_r.astype(BF16)
    w_lo = (w_r - w_mid.astype(F32)).astype(BF16)
    h_hi = h.astype(BF16)
    h_lo = (h - h_hi.astype(F32)).astype(BF16)
    dims = (((1,), (1,)), ((), ()))
    a = lax.dot_general(jnp.concatenate([w_hi, w_mid, w_lo], axis=0), h_hi, dims, preferred_element_type=F32)
    b = lax.dot_general(jnp.concatenate([w_hi, w_mid], axis=0), h_lo, dims, preferred_element_type=F32)
    e = w_t.shape[0]
    return a[:e] + a[e:2 * e] + a[2 * e:] + b[:e] + b[e:]


def _post_kernel(x_ref, at_ref, yc_ref, wglu_ref, gs_ref, wo_ref, g1_ref, n2_ref, sc_ref, sh_ref,
                 wrt_ref, rb_ref, x1_ref, h2_ref, cw_ref, gid_ref, y_scr):
    nchunk = y_scr.shape[1] // SSM_CHUNK
    for s in range(SSM_CHUNK):
        for j in range(SSM_COLS):
            y_scr[j, pl.ds(s, nchunk, stride=SSM_CHUNK), :] = yc_ref[j, :, s * LANES:(s + 1) * LANES].astype(F32)
    tm = x_ref.shape[0]
    sub = tm // POST_SPLIT
    for part in range(POST_SPLIT):
        rows = slice(part * sub, (part + 1) * sub)
        yg = jax.nn.gelu(jnp.concatenate([y_scr[j, rows, :] for j in range(SSM_COLS)], axis=1))
        z = yg * jax.nn.sigmoid(jnp.dot(yg.astype(BF16), wglu_ref[...], preferred_element_type=F32))
        zn = _rms(z, gs_ref[...]).astype(BF16)
        o = (jnp.dot(at_ref[rows, :], wo_ref[:ATTN_WIDTH, :], preferred_element_type=F32)
             + jnp.dot(zn, wo_ref[ATTN_WIDTH:, :], preferred_element_type=F32))
        x1 = x_ref[rows, :] + g1_ref[...] * o
        x1_ref[rows, :] = x1
        h2 = _rms(x1, n2_ref[...]) * (1.0 + sc_ref[...]) + sh_ref[...]
        h2_ref[rows, :] = h2
        logits = _router_logits(wrt_ref[...], h2)
        cw, bucket = _route(logits, rb_ref[...])
        cw_ref[:, rows] = cw
        gid_ref[:, rows] = bucket


def _post(x, attn, yc, w_glu, ssm_g, w_out, g1, n2g, sc2, sh2, w_router_t, router_bias, seq):
    t, d = x.shape
    tm = min(TOKEN_TILE, seq)
    per_b = seq // tm
    tok = lambda w: pl.BlockSpec((tm, w), lambda i: (i, 0))
    const = lambda a: pl.BlockSpec(a.shape, lambda i: (0,) * a.ndim)
    per_batch = pl.BlockSpec((None, 1, d), lambda i: (i // per_b, 0, 0))
    col = lambda r: pl.BlockSpec((r, tm), lambda i: (0, i))
    chunked = pl.BlockSpec((SSM_COLS, tm // SSM_CHUNK, CHUNK_LANES), lambda i: (0, i % per_b, i // per_b))
    return pl.pallas_call(
        _post_kernel,
        grid=(t // tm,),
        in_specs=[tok(d), tok(ATTN_WIDTH), chunked, const(w_glu), const(ssm_g),
                  const(w_out), per_batch, const(n2g), per_batch, per_batch, const(w_router_t), const(router_bias)],
        out_specs=[tok(d), tok(d), col(2), col(1)],
        out_shape=[jax.ShapeDtypeStruct((t, d), F32), jax.ShapeDtypeStruct((t, d), F32),
                   jax.ShapeDtypeStruct((2, t), F32), jax.ShapeDtypeStruct((1, t), jnp.int32)],
        scratch_shapes=[pltpu.VMEM((SSM_COLS, tm, LANES), F32)],
        compiler_params=_params(("parallel",), VMEM_LIMIT),
        name="post_mix",
    )(x, attn, yc, w_glu, ssm_g, w_out, g1, n2g, sc2, sh2, w_router_t, router_bias)


def _moe_kernel(kind_ref, rb_ref, bk_ref, pa_ref, pb_ref, first_ref, cast_ref, cpos_ref, pe_ref, offs_ref,
                x_ref, cw_ref, wg_ref, wu_ref, wd_ref, o_ref, wg_s, wu_s, wd_s):
    s = pl.program_id(0)

    @pl.when(cast_ref[s] == 1)
    def _():
        slot = cpos_ref[s]
        wg_s[slot] = wg_ref[...].astype(BF16)
        wu_s[slot] = wu_ref[...].astype(BF16)
        wd_s[slot] = wd_ref[...].astype(BF16)

    @pl.when(kind_ref[s] == STEP_ITEM)
    def _():
        bucket = bk_ref[s]
        base = rb_ref[s] * MOE_ROWS
        lo_row, hi_row = offs_ref[bucket] - base, offs_ref[bucket + 1] - base
        half = MOE_ROWS // 2
        slots = (pa_ref[s], pb_ref[s])

        def run(r0, r1):
            rows = r0 + lax.broadcasted_iota(jnp.int32, (r1 - r0, 1), 0)
            cw = jnp.where((rows >= lo_row) & (rows < hi_row), cw_ref[r0:r1, :], 0.0)
            x = x_ref[r0:r1, :].astype(BF16)
            y = None
            for k in range(2):
                gate = jnp.dot(x, wg_s[slots[k]], preferred_element_type=F32)
                up = jnp.dot(x, wu_s[slots[k]], preferred_element_type=F32)
                act = (gate * jax.nn.sigmoid(gate) * up * cw[:, k:k + 1]).astype(BF16)
                yk = jnp.dot(act, wd_s[slots[k]], preferred_element_type=F32)
                y = yk if y is None else y + yk

            @pl.when(first_ref[s] == 1)
            def _():
                o_ref[r0:r1, :] = y
                for z0, z1 in ((0, r0), (r1, MOE_ROWS)):
                    if z1 > z0:
                        o_ref[z0:z1, :] = jnp.zeros((z1 - z0, o_ref.shape[1]), F32)

            @pl.when(first_ref[s] == 0)
            def _():
                o_ref[r0:r1, :] += y

        needs_lower, needs_upper = lo_row < half, hi_row > half
        pl.when(needs_lower & needs_upper)(lambda: run(0, MOE_ROWS))
        pl.when(needs_lower & jnp.logical_not(needs_upper))(lambda: run(0, half))
        pl.when(jnp.logical_not(needs_lower) & needs_upper)(lambda: run(half, MOE_ROWS))


def _moe_plan_kernel(offs_ref, kind_ref, rb_ref, bk_ref, pa_ref, pb_ref, first_ref, cast_ref, cpos_ref, pe_ref,
                     irb, ibk, *, n_steps_max):
    i32 = jnp.int32
    ng, epg, ppg = N_EXPERT_GROUPS, EXPERTS_PER_GROUP, PAIRS_PER_GROUP

    shift = MOE_ROWS.bit_length() - 1

    def bucket_body(bk, cnt):
        a, b = offs_ref[bk], offs_ref[bk + 1]
        first_blk = lax.shift_right_logical(a, shift)
        n_blk = jnp.where(b > a, lax.shift_right_logical(b - 1, shift) - first_blk + 1, 0)

        def block_body(j, cnt):
            irb[cnt] = first_blk + j
            ibk[cnt] = bk
            return cnt + 1

        return lax.fori_loop(0, n_blk, block_body, cnt)

    n_items = lax.fori_loop(0, ng * ppg, bucket_body, i32(0))

    def count_body(i, m):
        g = ibk[i] // ppg
        return tuple(m[k] + (g == k).astype(i32) for k in range(ng))

    m = lax.fori_loop(0, n_items, count_body, (i32(0),) * ng)

    def next_group(g):
        nxt = i32(-1)
        for k in range(ng - 1, 0, -1):
            nxt = jnp.where((k > g) & (m[k] > 0), k, nxt)
        return nxt

    def emit(s, kind, rb, bk, pa, pb, first, cast, cpos, pe):
        kind_ref[s], rb_ref[s], bk_ref[s], pa_ref[s], pb_ref[s] = kind, rb, bk, pa, pb
        first_ref[s], cast_ref[s], cpos_ref[s], pe_ref[s] = first, cast, cpos, pe

    def item_body(i, carry):
        s, gcur, parity, q, last_pe, last_rb = carry
        rb, bk = irb[i], ibk[i]
        g, pos = bk // ppg, bk % ppg
        new = g != gcur
        started = gcur >= 0
        loaders = jnp.where(new, jnp.where(started, jnp.maximum(epg - q, 0), epg), 0)
        parity = jnp.where(new & started, 1 - parity, parity)
        q = jnp.where(new, 0, q)
        for j in range(epg):
            on = j >= epg - loaders
            emit(s, STEP_LOAD, rb, bk, 0, 0, 0, 1, parity * epg + j, epg * g + j)
            last_pe = jnp.where(on, epg * g + j, last_pe)
            s = s + on.astype(i32)
        nxt = next_group(g)
        pre = (q < epg) & (nxt >= 0)
        pe = jnp.where(pre, epg * nxt + q, last_pe)
        slot_a, slot_b = i32(PAIR_SLOTS[0][0]), i32(PAIR_SLOTS[0][1])
        for p in range(1, ppg):
            slot_a = jnp.where(pos == p, PAIR_SLOTS[p][0], slot_a)
            slot_b = jnp.where(pos == p, PAIR_SLOTS[p][1], slot_b)
        emit(s, STEP_ITEM, rb, bk, parity * epg + slot_a, parity * epg + slot_b, (rb != last_rb).astype(i32),
             pre.astype(i32), (1 - parity) * epg + q, pe)
        return s + 1, g, parity, q + 1, pe, rb

    s, _, _, _, last_pe, last_rb = lax.fori_loop(
        0, n_items, item_body, (i32(0), i32(-1), i32(0), i32(0), i32(0), i32(-1)))
    last_bk = ibk[jnp.maximum(n_items - 1, 0)]

    def pad_body(s, _):
        emit(s, STEP_PAD, last_rb, last_bk, 0, 0, 0, 0, 0, last_pe)
        return 0

    lax.fori_loop(s, n_steps_max, pad_body, 0)


def _moe_steps(bucket, t):
    i32 = jnp.int32
    nbk = N_EXPERT_GROUPS * PAIRS_PER_GROUP
    order = jnp.argsort(bucket, stable=True).astype(i32)
    counts = jnp.sum((bucket[None, :] == jnp.arange(nbk, dtype=i32)[:, None]).astype(i32), axis=1)
    offs = jnp.concatenate([jnp.zeros((1,), i32), jnp.cumsum(counts).astype(i32)])
    assert MOE_ROWS & (MOE_ROWS - 1) == 0
    n_items_max = t // MOE_ROWS + nbk - 1
    n_steps_max = n_items_max + N_EXPERTS
    smem = pl.BlockSpec(memory_space=pltpu.SMEM)
    tables = pl.pallas_call(
        functools.partial(_moe_plan_kernel, n_steps_max=n_steps_max),
        in_specs=[smem],
        out_specs=[smem] * 9,
        out_shape=[jax.ShapeDtypeStruct((n_steps_max,), i32)] * 9,
        scratch_shapes=[pltpu.SMEM((n_items_max + 1,), i32)] * 2,
        name="moe_plan",
    )(offs)
    return order, (*tables, offs), n_steps_max


def _moe(xs, cws, w_gate, w_up, w_down, layer, tables, n_steps_max):
    t, d = xs.shape
    ff = w_gate.shape[3]
    w_map = lambda s, kind, rb, bk, pa, pb, fi, ca, cp, pe, of: (layer, pe[s], 0, 0)
    row_map = lambda s, kind, rb, *_: (rb[s], 0)
    nres = 2 * EXPERTS_PER_GROUP
    grid_spec = pltpu.PrefetchScalarGridSpec(
        num_scalar_prefetch=len(tables),
        grid=(n_steps_max,),
        in_specs=[pl.BlockSpec((MOE_ROWS, d), row_map), pl.BlockSpec((MOE_ROWS, 2), row_map),
                  pl.BlockSpec((None, None, d, ff), w_map), pl.BlockSpec((None, None, d, ff), w_map),
                  pl.BlockSpec((None, None, ff, d), w_map)],
        out_specs=pl.BlockSpec((MOE_ROWS, d), row_map),
        scratch_shapes=[pltpu.VMEM((nres, d, ff), BF16), pltpu.VMEM((nres, d, ff), BF16),
                        pltpu.VMEM((nres, ff, d), BF16)],
    )
    return pl.pallas_call(
        _moe_kernel,
        grid_spec=grid_spec,
        out_shape=jax.ShapeDtypeStruct((t, d), F32),
        compiler_params=_params(("arbitrary",), MOE_VMEM_LIMIT),
        name="moe_grouped",
    )(*tables, xs, cws, w_gate, w_up, w_down)


def _take_rows(a, idx):
    return a.at[idx].get(mode="promise_in_bounds", unique_indices=True)


def _final_kernel(x_ref, y_ref, g_ref, o_ref):
    o_ref[...] = x_ref[...] + g_ref[...] * y_ref[...]


def _final(x1, y, g2, seq):
    t, d = x1.shape
    tm = min(TOKEN_TILE, seq)
    per_b = seq // tm
    tok = lambda w: pl.BlockSpec((tm, w), lambda i: (i, 0))
    return pl.pallas_call(
        _final_kernel,
        grid=(t // tm,),
        in_specs=[tok(d), tok(d), pl.BlockSpec((None, 1, d), lambda i: (i // per_b, 0, 0))],
        out_specs=tok(d),
        out_shape=jax.ShapeDtypeStruct((t, d), F32),
        compiler_params=_params(("parallel",)),
        name="final_residual",
    )(x1, y, g2)


def kernel(x, c, positions, ada_w, ada_b, norm1_g, w_in, q_norm_g, k_norm_g, attn_sink, lam_re, lam_im, ssm_b_re, ssm_b_im, ssm_c_re, ssm_c_im, ssm_d, ssm_log_dt, w_glu, attn_out_g, ssm_out_g, w_out, norm2_g, w_router, router_bias, w_exp_gate, w_exp_up, w_exp_down):
    batch, seq, d = x.shape
    depth = ada_w.shape[0]
    t = batch * seq
    assert seq % ATTN_BLOCK == 0 and seq % SSM_CHUNK == 0 and t % MOE_ROWS == 0

    mod = _adaln_mod(c, ada_w, ada_b).reshape(depth, 6, batch, 1, d)
    cos, sin = _rope_tables(positions)
    head_sum, rot = _rope_constants()
    bias = _attn_bias()
    w_router_t = w_router.T
    s5_mats = _s5_prep(lam_re, lam_im, ssm_b_re, ssm_b_im, ssm_c_re, ssm_c_im, ssm_d, ssm_log_dt)
    router_bias_col = router_bias.reshape(N_EXPERTS, 1)

    xf = x.reshape(t, d)
    res = None
    for l in range(depth):
        sh1, sc1, g1, sh2, sc2, g2 = (mod[l, j] for j in range(6))
        qg = (jnp.tile(q_norm_g[l], N_Q_HEADS) * HEAD_DIM ** -0.5).reshape(1, ATTN_WIDTH)
        kg = jnp.tile(k_norm_g[l], N_KV_HEADS).reshape(1, KV_WIDTH)
        outs = _inproj(xf, res, sc1, sh1, norm1_g[l].reshape(1, d), w_in[l].astype(BF16), qg, kg, head_sum, rot,
                       cos, sin, seq)
        if res is None:
            q, kx, vx, uc = outs
        else:
            q, kx, vx, uc, xf = outs
        attn = _attention(q, kx, vx, attn_sink[l], attn_out_g[l].reshape(1, ATTN_WIDTH), bias, batch, seq)
        yc = _s5_scan(uc, s5_mats, l)
        x1, h2, cw, gid = _post(xf, attn, yc, w_glu[l].astype(BF16), ssm_out_g[l].reshape(1, SSM_WIDTH),
                                 w_out[l].astype(BF16), g1, norm2_g[l].reshape(1, d), sc2, sh2, w_router_t,
                                 router_bias_col, seq)
        order, tables, n_steps_max = _moe_steps(gid.reshape(t), t)
        y_sorted = _moe(_take_rows(h2, order), _take_rows(cw.T, order), w_exp_gate, w_exp_up, w_exp_down, l,
                        tables, n_steps_max)
        y = _take_rows(y_sorted, jnp.argsort(order).astype(jnp.int32))
        xf, res = x1, (y, g2)
    y, g2 = res
    return _final(xf, y, g2, seq).reshape(batch, seq, d)
```

```python
import functools

import numpy as np
import jax
import jax.numpy as jnp
from jax import lax
from jax.experimental import pallas as pl
from jax.experimental.pallas import tpu as pltpu

F32 = jnp.float32
BF16 = jnp.bfloat16

HEAD_DIM = 64
N_Q_HEADS = 8
N_KV_HEADS = 2
Q_PER_KV = N_Q_HEADS // N_KV_HEADS
ATTN_WIDTH = N_Q_HEADS * HEAD_DIM
KV_WIDTH = N_KV_HEADS * HEAD_DIM
ATTN_BLOCK = 128
ATTN_Q_TILE = 2048
ROPE_THETA = 10000.0
ROPE_SLAB = 256
LANES = 128
SSM_GROUP_CH = 16
SSM_GROUPS = 32
SSM_WIDTH = SSM_GROUPS * SSM_GROUP_CH
SSM_STATE = 64
SSM_CHUNK = 16
SSM_COLS = SSM_WIDTH // LANES
COL_GROUPS = LANES // SSM_GROUP_CH
COL_STATE = COL_GROUPS * SSM_STATE
CHUNK_LANES = SSM_CHUNK * LANES
SSM_NSPLIT = 4
S5_ROW_PAD = 8
N_EXPERTS = 16
N_EXPERT_GROUPS = 4
EXPERTS_PER_GROUP = N_EXPERTS // N_EXPERT_GROUPS
PAIRS_PER_GROUP = EXPERTS_PER_GROUP * (EXPERTS_PER_GROUP - 1) // 2
PAIR_SLOTS = ((0, 1), (0, 2), (0, 3), (1, 3), (1, 2), (3, 2))
EPS = 1e-6
MASK_BIAS = -1e30

TOKEN_TILE = 1024
POST_SPLIT = 1
MOE_ROWS = 256
VMEM_LIMIT = 48 * 1024 * 1024
MOE_VMEM_LIMIT = 56 * 1024 * 1024
STEP_PAD, STEP_LOAD, STEP_ITEM = 0, 1, 2


def _params(sem, vmem=None):
    return pltpu.CompilerParams(dimension_semantics=sem, vmem_limit_bytes=vmem)


def _rms(x, g):
    return x * lax.rsqrt(jnp.mean(x * x, axis=-1, keepdims=True) + EPS) * g


def _mod_kernel(c_ref, w_ref, b_ref, o_ref):
    c = c_ref[...]
    s = c * jax.nn.sigmoid(c)
    o_ref[...] = jnp.dot(s.astype(BF16), w_ref[...].astype(BF16), preferred_element_type=F32) + b_ref[...]


def _adaln_mod(c, ada_w, ada_b):
    depth, d, d6 = ada_w.shape
    nb = c.shape[0]
    n6 = d6 // d
    return pl.pallas_call(
        _mod_kernel,
        grid=(depth, n6),
        in_specs=[pl.BlockSpec((nb, d), lambda l, j: (0, 0)),
                  pl.BlockSpec((None, d, d), lambda l, j: (l, 0, j)),
                  pl.BlockSpec((None, None, 1, d), lambda l, j: (l, j, 0, 0))],
        out_specs=pl.BlockSpec((None, None, nb, d), lambda l, j: (l, j, 0, 0)),
        out_shape=jax.ShapeDtypeStruct((depth, n6, nb, d), F32),
        compiler_params=_params(("arbitrary", "arbitrary"), VMEM_LIMIT),
        name="adaln_mod",
    )(c, ada_w, ada_b.reshape(depth, n6, 1, d))


def _rope_kernel(pos_ref, freq_ref, cos_ref, sin_ref):
    ang = pos_ref[...].astype(F32) * freq_ref[...]
    cos_ref[...] = jnp.cos(ang)
    sin_ref[...] = jnp.sin(ang)


def _rope_tables(positions):
    half = HEAD_DIM // 2
    t = positions.size
    per_row = LANES // half
    rows = t // per_row
    pos_rep = jnp.repeat(positions.reshape(rows, per_row), half, axis=1)
    freq = (ROPE_THETA ** (-np.arange(half, dtype=np.float64) / half)).astype(np.float32)
    freq_row = jnp.asarray(np.tile(freq, per_row)[None, :])
    blk = min(rows, 512)
    cos, sin = pl.pallas_call(
        _rope_kernel,
        grid=(rows // blk,),
        in_specs=[pl.BlockSpec((blk, LANES), lambda i: (i, 0)),
                  pl.BlockSpec((1, LANES), lambda i: (0, 0))],
        out_specs=[pl.BlockSpec((blk, LANES), lambda i: (i, 0))] * 2,
        out_shape=[jax.ShapeDtypeStruct((rows, LANES), F32)] * 2,
        compiler_params=_params(("arbitrary",)),
        name="rope_tables",
    )(pos_rep, freq_row)
    widen = lambda a: jnp.tile(a.reshape(t, half), (1, per_row))
    return widen(cos), widen(sin)


def _rope_constants():
    lane = np.arange(ROPE_SLAB)
    head_sum = (lane[:, None] // HEAD_DIM == lane[None, :] // HEAD_DIM).astype(np.float32)
    half = HEAD_DIM // 2
    rot = np.zeros((ROPE_SLAB, ROPE_SLAB), np.float32)
    for d in range(ROPE_SLAB):
        if d % HEAD_DIM < half:
            rot[d + half, d] = -1.0
        else:
            rot[d - half, d] = 1.0
    return jnp.asarray(head_sum, BF16), jnp.asarray(rot, BF16)


def _inproj_kernel(*refs, has_res):
    if has_res:
        (x_ref, y_ref, g2_ref, sc_ref, sh_ref, n1_ref, w_ref, qg_ref, kg_ref, hs_ref, rot_ref,
         cos_ref, sin_ref, q_ref, k_ref, v_ref, uc_ref, xo_ref, u_scr) = refs
        x = x_ref[...] + g2_ref[...] * y_ref[...]
        xo_ref[...] = x
    else:
        (x_ref, sc_ref, sh_ref, n1_ref, w_ref, qg_ref, kg_ref, hs_ref, rot_ref,
         cos_ref, sin_ref, q_ref, k_ref, v_ref, uc_ref, u_scr) = refs
        x = x_ref[...]
    h = _rms(x, n1_ref[...]) * (1.0 + sc_ref[...]) + sh_ref[...]
    proj = jnp.dot(h.astype(BF16), w_ref[...], preferred_element_type=F32)
    q = proj[:, :ATTN_WIDTH]
    k = proj[:, ATTN_WIDTH:ATTN_WIDTH + KV_WIDTH]
    v = proj[:, ATTN_WIDTH + KV_WIDTH:ATTN_WIDTH + 2 * KV_WIDTH]
    cos = cos_ref[...]
    sin = sin_ref[...]
    reps = ATTN_WIDTH // LANES
    cos_q = jnp.concatenate([cos] * reps, axis=1)
    sin_q = jnp.concatenate([sin] * reps, axis=1)

    def head_norm_rope(t, gain, c, s):
        outs = []
        for lo in range(0, t.shape[1], ROPE_SLAB):
            wd = min(ROPE_SLAB, t.shape[1] - lo)
            ts, lanes = t[:, lo:lo + wd], slice(lo, lo + wd)
            ssq = jnp.dot((ts * ts).astype(BF16), hs_ref[:wd, :wd], preferred_element_type=F32)
            tn = (ts * lax.rsqrt(ssq * (1.0 / HEAD_DIM) + EPS) * gain[:, lanes]).astype(BF16)
            tr = jnp.dot(tn, rot_ref[:wd, :wd], preferred_element_type=F32)
            outs.append(tn.astype(F32) * c[:, lanes] + tr * s[:, lanes])
        return outs[0] if len(outs) == 1 else jnp.concatenate(outs, axis=1)

    qo = head_norm_rope(q, qg_ref[...], cos_q, sin_q)
    ko = head_norm_rope(k, kg_ref[...], cos, sin)
    q_ref[...] = qo.astype(BF16)
    k_ref[...] = jnp.concatenate([ko, pltpu.roll(ko, HEAD_DIM, axis=1)], axis=1).astype(BF16)
    v_ref[...] = jnp.concatenate([v, pltpu.roll(v, HEAD_DIM, axis=1)], axis=1).astype(BF16)
    u0 = ATTN_WIDTH + 2 * KV_WIDTH
    nchunk = u_scr.shape[1] // SSM_CHUNK
    for j in range(SSM_COLS):
        u_scr[j] = proj[:, u0 + j * LANES:u0 + (j + 1) * LANES]
    for s in range(SSM_CHUNK):
        for j in range(SSM_COLS):
            lanes = slice(s * SSM_WIDTH + j * LANES, s * SSM_WIDTH + (j + 1) * LANES)
            uc_ref[:, lanes] = u_scr[j, pl.ds(s, nchunk, stride=SSM_CHUNK), :].astype(BF16)


def _inproj(x, res, sc1, sh1, n1g, w_in, qg, kg, head_sum, rot, cos, sin, seq):
    t, d = x.shape
    tm = min(TOKEN_TILE, seq)
    per_b = seq // tm
    in_width = w_in.shape[1]
    tok = lambda w: pl.BlockSpec((tm, w), lambda i: (i, 0))
    const = lambda a: pl.BlockSpec(a.shape, lambda i: (0,) * a.ndim)
    per_batch = pl.BlockSpec((None, 1, d), lambda i: (i // per_b, 0, 0))
    chunked = pl.BlockSpec((tm // SSM_CHUNK, SSM_CHUNK * SSM_WIDTH), lambda i: (i, 0))
    ins, specs = [x], [tok(d)]
    if res is not None:
        y_prev, g2_prev = res
        ins += [y_prev, g2_prev]
        specs += [tok(d), per_batch]
    ins += [sc1, sh1, n1g, w_in, qg, kg, head_sum, rot, cos, sin]
    specs += [per_batch, per_batch, const(n1g), const(w_in), const(qg), const(kg), const(head_sum), const(rot),
              tok(LANES), tok(LANES)]
    out_shape = [jax.ShapeDtypeStruct((t, ATTN_WIDTH), BF16), jax.ShapeDtypeStruct((t, 2 * KV_WIDTH), BF16),
                 jax.ShapeDtypeStruct((t, 2 * KV_WIDTH), BF16),
                 jax.ShapeDtypeStruct((t // SSM_CHUNK, SSM_CHUNK * SSM_WIDTH), BF16)]
    out_specs = [tok(ATTN_WIDTH), tok(2 * KV_WIDTH), tok(2 * KV_WIDTH), chunked]
    if res is not None:
        out_shape.append(jax.ShapeDtypeStruct((t, d), F32))
        out_specs.append(tok(d))
    assert in_width == ATTN_WIDTH + 2 * KV_WIDTH + SSM_WIDTH
    return pl.pallas_call(
        functools.partial(_inproj_kernel, has_res=res is not None),
        grid=(t // tm,),
        in_specs=specs,
        out_specs=out_specs,
        out_shape=out_shape,
        scratch_shapes=[pltpu.VMEM((SSM_COLS, tm, LANES), F32)],
        compiler_params=_params(("parallel",), VMEM_LIMIT),
        name="inproj",
    )(*ins)


def _attn_kernel(sink_ref, q_ref, kc_ref, kp_ref, vc_ref, vp_ref, bias_ref, g_ref, o_ref):
    nsub = q_ref.shape[0] // ATTN_BLOCK
    kk = jnp.concatenate([kp_ref[...], kc_ref[...]], axis=0)
    vv = jnp.concatenate([vp_ref[...], vc_ref[...]], axis=0)
    low = lax.broadcasted_iota(jnp.int32, (kk.shape[0], KV_WIDTH), 1) < HEAD_DIM
    zero = jnp.zeros((kk.shape[0], KV_WIDTH), BF16)

    def variants(a):
        nat, swp = a[:, :KV_WIDTH], a[:, KV_WIDTH:]
        return {(0, 0): jnp.where(low, nat, zero), (0, 1): jnp.where(low, zero, swp),
                (1, 0): jnp.where(low, swp, zero), (1, 1): jnp.where(low, zero, nat)}

    kvar, vvar = variants(kk), variants(vv)
    band = bias_ref[1]
    first = bias_ref[jnp.minimum(pl.program_id(1), 1)]
    upper = lax.broadcasted_iota(jnp.int32, (2 * ATTN_BLOCK, 1), 0) < ATTN_BLOCK
    for j in range(nsub):
        bias = first if j == 0 else band
        bias2 = jnp.concatenate([bias, bias], axis=0)
        keys = slice(j * ATTN_BLOCK, (j + 2) * ATTN_BLOCK)
        qrows = slice(j * ATTN_BLOCK, (j + 1) * ATTN_BLOCK)
        tiles = [None] * (N_Q_HEADS // 2)
        for kv in range(N_KV_HEADS):
            for half in range(2):
                pairs = (2 * kv, 2 * kv + 1)
                heads = (2 * pairs[0] + half, 2 * pairs[1] + half)
                qs = jnp.concatenate([q_ref[qrows, p * LANES:(p + 1) * LANES] for p in pairs], axis=0)
                s = lax.dot_general(qs, kvar[(kv, half)][keys], (((1,), (1,)), ((), ())),
                                    preferred_element_type=F32) + bias2
                sink = jnp.where(upper, sink_ref[heads[0]], sink_ref[heads[1]])
                m = jnp.maximum(jnp.max(s, axis=-1, keepdims=True), sink)
                p = jnp.exp(s - m)
                denom = jnp.sum(p, axis=-1, keepdims=True) + jnp.exp(sink - m)
                o = jnp.dot(p.astype(BF16), vvar[(kv, half)][keys], preferred_element_type=F32) * (1.0 / denom)
                for r, pr in enumerate(pairs):
                    part = o[r * ATTN_BLOCK:(r + 1) * ATTN_BLOCK]
                    tiles[pr] = part if tiles[pr] is None else tiles[pr] + part
        a = jnp.concatenate(tiles, axis=1)
        o_ref[qrows, :] = _rms(a, g_ref[...]).astype(BF16)


def _attn_bias():
    qi = np.arange(ATTN_BLOCK)[:, None]
    sj = np.arange(2 * ATTN_BLOCK)[None, :]
    diff = qi + ATTN_BLOCK - sj
    band = (diff >= 0) & (diff < ATTN_BLOCK)
    first = band & (sj >= ATTN_BLOCK)
    return jnp.asarray(np.where(np.stack([first, band]), 0.0, MASK_BIAS).astype(np.float32))


def _attention(q, kx, vx, sink, out_g, bias, batch, seq):
    t = q.shape[0]
    qb = min(ATTN_Q_TILE, seq)
    nsub = qb // ATTN_BLOCK
    nq = seq // qb
    nb = seq // ATTN_BLOCK
    cur = lambda w: pl.BlockSpec((qb, w), lambda b, n, s: (b * nq + n, 0))
    prev = lambda w: pl.BlockSpec((ATTN_BLOCK, w), lambda b, n, s: (b * nb + jnp.maximum(n * nsub - 1, 0), 0))
    grid_spec = pltpu.PrefetchScalarGridSpec(
        num_scalar_prefetch=1,
        grid=(batch, nq),
        in_specs=[cur(ATTN_WIDTH), cur(2 * KV_WIDTH), prev(2 * KV_WIDTH), cur(2 * KV_WIDTH), prev(2 * KV_WIDTH),
                  pl.BlockSpec(bias.shape, lambda b, n, s: (0, 0, 0)),
                  pl.BlockSpec((1, ATTN_WIDTH), lambda b, n, s: (0, 0))],
        out_specs=cur(ATTN_WIDTH),
    )
    return pl.pallas_call(
        _attn_kernel,
        grid_spec=grid_spec,
        out_shape=jax.ShapeDtypeStruct((t, ATTN_WIDTH), BF16),
        compiler_params=_params(("parallel", "arbitrary")),
        name="swa_attention",
    )(sink, q, kx, kx, vx, vx, bias, out_g)


def _spread(x, expander3):
    hi = x.astype(BF16)
    r1 = x - hi.astype(F32)
    mid = r1.astype(BF16)
    lo = (r1 - mid.astype(F32)).astype(BF16)
    return jnp.dot(jnp.concatenate([hi, mid, lo], axis=1), expander3, preferred_element_type=F32)


def _s5_prep_kernel(lr_re_ref, lr_im_ref, ldt_ref, bt_re_ref, bt_im_ref, ct_re_ref, ct_im_ref,
                    d_ref, lcol_re_ref, lcol_im_ref, ldtcol_ref, exp_ref, exph_ref, expt_ref, expw_ref,
                    t_ref, w_ref, v_ref, la_ref, lb_ref):
    hi = lax.Precision.HIGHEST
    nl = SSM_CHUNK
    low = lax.broadcasted_iota(jnp.int32, (1, 2 * SSM_STATE), 1) < SSM_STATE
    row_low = lax.broadcasted_iota(jnp.int32, (2 * SSM_STATE, 1), 0) < SSM_STATE
    jcol = lax.broadcasted_iota(jnp.int32, (nl, 1), 0).astype(F32)
    kt_lane = lax.broadcasted_iota(jnp.int32, (SSM_GROUP_CH, nl * SSM_GROUP_CH), 1)
    kt_row = lax.broadcasted_iota(jnp.int32, (SSM_GROUP_CH, nl * SSM_GROUP_CH), 0)

    w_all, v_all, kt_all = [], [], []
    for gm in range(COL_GROUPS):
        dt = jnp.exp(ldt_ref[gm])
        lam_re, lam_im = lr_re_ref[gm], lr_im_ref[gm]
        a_r, th_r = lam_re * dt, lam_im * dt

        er = jnp.exp(jcol * a_r)
        pw_re, pw_im = er * jnp.cos(jcol * th_r), er * jnp.sin(jcol * th_r)

        nr, ni = pw_re[1:2, :] - 1.0, pw_im[1:2, :]
        den = lam_re * lam_re + lam_im * lam_im
        c_re, c_im = (nr * lam_re + ni * lam_im) / den, (ni * lam_re - nr * lam_im) / den
        bt_re, bt_im = bt_re_ref[gm], bt_im_ref[gm]
        bb_re, bb_im = c_re * bt_re - c_im * bt_im, c_re * bt_im + c_im * bt_re

        w_rows = []
        for s in range(nl):
            j = nl - 1 - s
            pr, pi = pw_re[j:j + 1, :], pw_im[j:j + 1, :]
            w_rows.append(jnp.where(low, pr * bb_re - pi * bb_im, pr * bb_im + pi * bb_re))
        w_all.append(w_rows)

        pw_re_t, pw_im_t = pw_re.T, pw_im.T
        pc, ps = _spread(pw_re_t, exp_ref[...]), _spread(pw_im_t, exp_ref[...])
        ct_re, ct_im = _spread(ct_re_ref[gm], exph_ref[...]), _spread(ct_im_ref[gm], exph_ref[...])
        a_re, a_im = ct_re * pc - ct_im * ps, ct_re * ps + ct_im * pc
        a_cat = jnp.where(row_low, a_re, -a_im)
        l1_re, l1_im = pw_re_t[:, 1:2], pw_im_t[:, 1:2]
        v_re, v_im = a_re * l1_re - a_im * l1_im, a_re * l1_im + a_im * l1_re
        v_all.append(jnp.where(row_low, v_re, -v_im))

        kt = jnp.dot(jnp.where(low, bb_re, bb_im), a_cat, precision=hi, preferred_element_type=F32)
        kt_all.append(kt + jnp.where(kt_lane == kt_row, d_ref[gm], 0.0))

    def expand(stacked, expander, row_group, lane_group):
        wide = jnp.dot(stacked.astype(BF16), expander, preferred_element_type=F32)
        r = lax.broadcasted_iota(jnp.int32, wide.shape, 0)
        c = lax.broadcasted_iota(jnp.int32, wide.shape, 1)
        return jnp.where(row_group(r) == lane_group(c), wide, 0.0).astype(BF16)

    chan_group = lambda i: (i >> 4) & (COL_GROUPS - 1)
    state_group = lambda i: (i >> 6) & (COL_GROUPS - 1)

    bd = expand(jnp.concatenate(kt_all, axis=0), expt_ref[...], chan_group, chan_group)
    t_ref[0:LANES, :] = bd
    for s in range(1, nl):
        t_ref[s * LANES:(s + 1) * LANES, :] = jnp.concatenate(
            [jnp.zeros((LANES, s * LANES), BF16), bd[:, :CHUNK_LANES - s * LANES]], axis=1)

    w_stack = jnp.concatenate([w_all[gm][s] for s in range(nl) for gm in range(COL_GROUPS)], axis=0)
    w_ref[...] = expand(w_stack, expw_ref[...], chan_group, state_group)

    v_stack = jnp.concatenate([v_all[gm][half * SSM_STATE:(half + 1) * SSM_STATE, :]
                               for half in range(2) for gm in range(COL_GROUPS)], axis=0)
    v_ref[...] = expand(v_stack, expt_ref[...], state_group, chan_group)

    dtc = jnp.exp(ldtcol_ref[...])
    e16 = jnp.exp(nl * lcol_re_ref[...] * dtc)
    ang = nl * lcol_im_ref[...] * dtc
    la_ref[...] = e16 * jnp.cos(ang)
    lb_ref[...] = e16 * jnp.sin(ang)


def _s5_prep(lam_re, lam_im, b_re, b_im, c_re, c_im, d_skip, log_dt):
    g, p, h, nl = SSM_GROUPS, SSM_STATE, SSM_GROUP_CH, SSM_CHUNK
    cg = COL_GROUPS
    nc = lam_re.shape[0] * SSM_COLS
    col = lambda a: a.reshape((nc, cg) + a.shape[2:])
    dup_row = lambda a: col(jnp.tile(a, (1, 1, 2))[:, :, None, :])
    bt = lambda a: col(jnp.tile(jnp.swapaxes(a, 2, 3), (1, 1, 1, 2)))
    ct = lambda a: col(jnp.tile(jnp.swapaxes(a, 2, 3), (1, 1, 2, 1)))
    d_pad = col(jnp.pad(d_skip.reshape(-1, g, 1, h), ((0, 0), (0, 0), (0, 0), (0, nl * h - h))))
    wide = lambda a: a.reshape(nc, 1, cg * p)
    expand = jnp.asarray(np.tile(np.repeat(np.eye(nl, dtype=np.float32), h, axis=1), (3, 1)), BF16)
    expand_h = jnp.asarray(np.tile(np.eye(h, dtype=np.float32), (3, nl)), BF16)
    exp_t = np.zeros((nl, h, nl, cg, h), np.float32)
    exp_w = np.zeros((2, p, 2, cg, p), np.float32)
    for gm in range(cg):
        exp_t[:, :, :, gm, :] = np.eye(nl * h, dtype=np.float32).reshape(nl, h, nl, h)
        exp_w[:, :, :, gm, :] = np.eye(2 * p, dtype=np.float32).reshape(2, p, 2, p)
    exp_t = jnp.asarray(exp_t.reshape(nl * h, CHUNK_LANES), BF16)
    exp_w = jnp.asarray(exp_w.reshape(2 * p, 2 * COL_STATE), BF16)
    blk = lambda *s: pl.BlockSpec((None,) + s, lambda i: (i,) + (0,) * len(s))
    const = lambda a: pl.BlockSpec(a.shape, lambda i: (0,) * a.ndim)
    lw = nl * h
    return pl.pallas_call(
        _s5_prep_kernel,
        grid=(nc,),
        in_specs=[blk(cg, 1, 2 * p), blk(cg, 1, 2 * p), blk(cg, 1, 1),
                  blk(cg, h, 2 * p), blk(cg, h, 2 * p), blk(cg, 2 * p, h), blk(cg, 2 * p, h), blk(cg, 1, lw),
                  blk(1, cg * p), blk(1, cg * p), blk(1, cg * p), const(expand), const(expand_h), const(exp_t),
                  const(exp_w)],
        out_specs=[blk(CHUNK_LANES, CHUNK_LANES), blk(CHUNK_LANES, 2 * COL_STATE), blk(2 * COL_STATE, CHUNK_LANES),
                   blk(1, COL_STATE), blk(1, COL_STATE)],
        out_shape=[jax.ShapeDtypeStruct((nc, CHUNK_LANES, CHUNK_LANES), BF16),
                   jax.ShapeDtypeStruct((nc, CHUNK_LANES, 2 * COL_STATE), BF16),
                   jax.ShapeDtypeStruct((nc, 2 * COL_STATE, CHUNK_LANES), BF16),
                   jax.ShapeDtypeStruct((nc, 1, COL_STATE), F32), jax.ShapeDtypeStruct((nc, 1, COL_STATE), F32)],
        compiler_params=_params(("parallel",), VMEM_LIMIT),
        name="s5_prep",
    )(dup_row(lam_re), dup_row(lam_im), col(log_dt[:, :, None, None]),
      bt(b_re), bt(b_im), ct(c_re), ct(c_im), d_pad, wide(lam_re), wide(lam_im),
      wide(jnp.repeat(log_dt, p, axis=1)), expand, expand_h, exp_t, exp_w)


def _s5_kernel(*refs, nchunks, nb):
    uc_refs = refs[:SSM_CHUNK]
    t_ref, w_ref, v_ref, la_ref, lb_ref, o_ref, ucat_ref, s_ref, xp_ref = refs[SSM_CHUNK:]

    @pl.when(pl.program_id(1) == 0)
    def _():
        for s in range(SSM_CHUNK):
            ucat_ref[:, s * LANES:(s + 1) * LANES] = uc_refs[s][...]
        s_in = jnp.dot(ucat_ref[...], w_ref[...], preferred_element_type=F32)
        nblk = COL_STATE // LANES
        pitch = s_ref.shape[1] // nb
        for b in range(2 * nblk):
            for q in range(nb):
                s_ref[b, q * pitch:q * pitch + nchunks, :] = s_in[q * nchunks:(q + 1) * nchunks, b * LANES:(b + 1) * LANES]
        lr = [jnp.broadcast_to(la_ref[:, b * LANES:(b + 1) * LANES], (nb, LANES)) for b in range(nblk)]
        li = [jnp.broadcast_to(lb_ref[:, b * LANES:(b + 1) * LANES], (nb, LANES)) for b in range(nblk)]

        def step(c, carry):
            rows = pl.ds(c, nb, stride=pitch)
            out = []
            for b in range(nblk):
                re, im = carry[2 * b], carry[2 * b + 1]
                xp_ref[b, rows, :] = re
                xp_ref[nblk + b, rows, :] = im
                out.append(lr[b] * re - li[b] * im + s_ref[b, rows, :])
                out.append(lr[b] * im + li[b] * re + s_ref[nblk + b, rows, :])
            return tuple(out)

        zero = jnp.zeros((nb, LANES), F32)
        lax.fori_loop(0, nchunks, step, (zero,) * (2 * nblk), unroll=4)

    pitch = xp_ref.shape[1] // nb
    xp = jnp.concatenate(
        [jnp.concatenate([xp_ref[b, q * pitch:q * pitch + nchunks, :] for q in range(nb)], axis=0)
         for b in range(2 * COL_STATE // LANES)], axis=1).astype(BF16)
    inter = jnp.dot(xp, v_ref[...], preferred_element_type=F32)
    for kk in range(SSM_NSPLIT):
        @pl.when(pl.program_id(1) == kk)
        def _():
            live = (kk + 1) * (CHUNK_LANES // SSM_NSPLIT)
            intra = jnp.dot(ucat_ref[:, :live], t_ref[:live, :], preferred_element_type=F32)
            o_ref[...] = (intra + inter).astype(BF16)


def _s5_scan(uc, mats, layer, nchunks, nb):
    rows = uc.shape[0]
    c0 = layer * SSM_COLS
    split = CHUNK_LANES // SSM_NSPLIT
    u_spec = lambda s: pl.BlockSpec((rows, LANES), lambda j, k: (0, SSM_COLS * s + j))
    return pl.pallas_call(
        functools.partial(_s5_kernel, nchunks=nchunks, nb=nb),
        grid=(SSM_COLS, SSM_NSPLIT),
        in_specs=[u_spec(s) for s in range(SSM_CHUNK)] + [
            pl.BlockSpec((None, CHUNK_LANES, split), lambda j, k: (c0 + j, 0, k)),
            pl.BlockSpec((None, CHUNK_LANES, 2 * COL_STATE), lambda j, k: (c0 + j, 0, 0)),
            pl.BlockSpec((None, 2 * COL_STATE, split), lambda j, k: (c0 + j, 0, k)),
            pl.BlockSpec((None, 1, COL_STATE), lambda j, k: (c0 + j, 0, 0)),
            pl.BlockSpec((None, 1, COL_STATE), lambda j, k: (c0 + j, 0, 0))],
        out_specs=pl.BlockSpec((None, rows, split), lambda j, k: (j, 0, k)),
        out_shape=jax.ShapeDtypeStruct((SSM_COLS, rows, CHUNK_LANES), BF16),
        scratch_shapes=[pltpu.VMEM((rows, CHUNK_LANES), BF16),
                        pltpu.VMEM((2 * COL_STATE // LANES, nb * (nchunks + S5_ROW_PAD), LANES), F32),
                        pltpu.VMEM((2 * COL_STATE // LANES, nb * (nchunks + S5_ROW_PAD), LANES), F32)],
        compiler_params=_params(("parallel", "arbitrary"), VMEM_LIMIT),
        name="s5_scan",
    )(*([uc] * SSM_CHUNK), *mats)


def _route(logits, bias):
    m = jnp.max(logits, axis=0, keepdims=True)
    e = jnp.exp(logits - m)
    probs = e / jnp.sum(e, axis=0, keepdims=True)
    sel = probs + bias
    row = lambda a, i: a[i:i + 1, :]
    best_score, best = None, None
    for grp in range(N_EXPERT_GROUPS):
        a, b, c, d = (row(sel, EXPERTS_PER_GROUP * grp + i) for i in range(EXPERTS_PER_GROUP))
        hab, lab, hcd, lcd = jnp.maximum(a, b), jnp.minimum(a, b), jnp.maximum(c, d), jnp.minimum(c, d)
        top1 = jnp.maximum(hab, hcd)
        top2 = jnp.maximum(jnp.maximum(lab, lcd), jnp.minimum(hab, hcd))
        score = top1 + top2
        if grp == 0:
            best_score, best = score, jnp.zeros(score.shape, jnp.int32)
        else:
            better = score > best_score
            best = jnp.where(better, grp, best)
            best_score = jnp.where(better, score, best_score)

    def pick(a, i):
        out = row(a, i)
        for grp in range(1, N_EXPERT_GROUPS):
            out = jnp.where(best == grp, row(a, EXPERTS_PER_GROUP * grp + i), out)
        return out

    s_in = [pick(sel, i) for i in range(EXPERTS_PER_GROUP)]
    p_in = [pick(probs, i) for i in range(EXPERTS_PER_GROUP)]
    neg = jnp.full(s_in[0].shape, -jnp.inf, F32)

    def argmax_first(vals):
        idx, val = jnp.zeros(vals[0].shape, jnp.int32), vals[0]
        for i in range(1, len(vals)):
            better = vals[i] > val
            idx = jnp.where(better, i, idx)
            val = jnp.where(better, vals[i], val)
        return idx

    i1 = argmax_first(s_in)
    i2 = argmax_first([jnp.where(i1 == i, neg, s_in[i]) for i in range(EXPERTS_PER_GROUP)])
    zero = jnp.zeros(p_in[0].shape, F32)
    g1 = sum(jnp.where(i1 == i, p_in[i], zero) for i in range(EXPERTS_PER_GROUP))
    g2 = sum(jnp.where(i2 == i, p_in[i], zero) for i in range(EXPERTS_PER_GROUP))
    tot = g1 + g2
    w1, w2 = g1 / tot, g2 / tot
    first_low = i1 < i2
    low, high = jnp.minimum(i1, i2), jnp.maximum(i1, i2)
    w_low, w_high = jnp.where(first_low, w1, w2), jnp.where(first_low, w2, w1)
    pos = jnp.where(low == 0, high - 1, jnp.where(low == 1, jnp.where(high == 2, 4, 3), 5))
    swap = low == 2
    bucket = best * PAIRS_PER_GROUP + pos
    return jnp.concatenate([jnp.where(swap, w_high, w_low), jnp.where(swap, w_low, w_high)], axis=0), bucket


def _router_logits(w_t, h):
    w_hi = w_t.astype(BF16)
    w_r = w_t - w_hi.astype(F32)
    w_mid = w_r.astype(BF16)
    w_lo = (w_r - w_mid.astype(F32)).astype(BF16)
    h_hi = h.astype(BF16)
    h_lo = (h - h_hi.astype(F32)).astype(BF16)
    dims = (((1,), (1,)), ((), ()))
    a = lax.dot_general(jnp.concatenate([w_hi, w_mid, w_lo], axis=0), h_hi, dims, preferred_element_type=F32)
    b = lax.dot_general(jnp.concatenate([w_hi, w_mid], axis=0), h_lo, dims, preferred_element_type=F32)
    e = w_t.shape[0]
    return a[:e] + a[e:2 * e] + a[2 * e:] + b[:e] + b[e:]


def _post_kernel(x_ref, at_ref, yc_ref, wglu_ref, gs_ref, wo_ref, g1_ref, n2_ref, sc_ref, sh_ref,
                 wrt_ref, rb_ref, x1_ref, h2_ref, cw_ref, gid_ref, y_scr):
    nchunk = y_scr.shape[1] // SSM_CHUNK
    for s in range(SSM_CHUNK):
        for j in range(SSM_COLS):
            y_scr[j, pl.ds(s, nchunk, stride=SSM_CHUNK), :] = yc_ref[j, :, s * LANES:(s + 1) * LANES].astype(F32)
    tm = x_ref.shape[0]
    sub = tm // POST_SPLIT
    for part in range(POST_SPLIT):
        rows = slice(part * sub, (part + 1) * sub)
        yg = jax.nn.gelu(jnp.concatenate([y_scr[j, rows, :] for j in range(SSM_COLS)], axis=1))
        z = yg * jax.nn.sigmoid(jnp.dot(yg.astype(BF16), wglu_ref[...], preferred_element_type=F32))
        zn = _rms(z, gs_ref[...]).astype(BF16)
        o = (jnp.dot(at_ref[rows, :], wo_ref[:ATTN_WIDTH, :], preferred_element_type=F32)
             + jnp.dot(zn, wo_ref[ATTN_WIDTH:, :], preferred_element_type=F32))
        x1 = x_ref[rows, :] + g1_ref[...] * o
        x1_ref[rows, :] = x1
        h2 = _rms(x1, n2_ref[...]) * (1.0 + sc_ref[...]) + sh_ref[...]
        h2_ref[rows, :] = h2
        logits = _router_logits(wrt_ref[...], h2)
        cw, bucket = _route(logits, rb_ref[...])
        cw_ref[:, rows] = cw
        gid_ref[:, rows] = bucket


def _post(x, attn, yc, w_glu, ssm_g, w_out, g1, n2g, sc2, sh2, w_router_t, router_bias, seq):
    t, d = x.shape
    tm = min(TOKEN_TILE, seq)
    per_b = seq // tm
    tok = lambda w: pl.BlockSpec((tm, w), lambda i: (i, 0))
    const = lambda a: pl.BlockSpec(a.shape, lambda i: (0,) * a.ndim)
    per_batch = pl.BlockSpec((None, 1, d), lambda i: (i // per_b, 0, 0))
    col = lambda r: pl.BlockSpec((r, tm), lambda i: (0, i))
    chunked = pl.BlockSpec((SSM_COLS, tm // SSM_CHUNK, CHUNK_LANES), lambda i: (0, i, 0))
    return pl.pallas_call(
        _post_kernel,
        grid=(t // tm,),
        in_specs=[tok(d), tok(ATTN_WIDTH), chunked, const(w_glu), const(ssm_g),
                  const(w_out), per_batch, const(n2g), per_batch, per_batch, const(w_router_t), const(router_bias)],
        out_specs=[tok(d), tok(d), col(2), col(1)],
        out_shape=[jax.ShapeDtypeStruct((t, d), F32), jax.ShapeDtypeStruct((t, d), F32),
                   jax.ShapeDtypeStruct((2, t), F32), jax.ShapeDtypeStruct((1, t), jnp.int32)],
        scratch_shapes=[pltpu.VMEM((SSM_COLS, tm, LANES), F32)],
        compiler_params=_params(("parallel",), VMEM_LIMIT),
        name="post_mix",
    )(x, attn, yc, w_glu, ssm_g, w_out, g1, n2g, sc2, sh2, w_router_t, router_bias)


def _moe_kernel(kind_ref, rb_ref, bk_ref, pa_ref, pb_ref, first_ref, cast_ref, cpos_ref, pe_ref, offs_ref,
                x_ref, cw_ref, wg_ref, wu_ref, wd_ref, o_ref, wg_s, wu_s, wd_s):
    s = pl.program_id(0)

    @pl.when(cast_ref[s] == 1)
    def _():
        slot = cpos_ref[s]
        wg_s[slot] = wg_ref[...].astype(BF16)
        wu_s[slot] = wu_ref[...].astype(BF16)
        wd_s[slot] = wd_ref[...].astype(BF16)

    @pl.when(kind_ref[s] == STEP_ITEM)
    def _():
        bucket = bk_ref[s]
        base = rb_ref[s] * MOE_ROWS
        lo_row, hi_row = offs_ref[bucket] - base, offs_ref[bucket + 1] - base
        half = MOE_ROWS // 2
        slots = (pa_ref[s], pb_ref[s])

        def run(r0, r1):
            rows = r0 + lax.broadcasted_iota(jnp.int32, (r1 - r0, 1), 0)
            cw = jnp.where((rows >= lo_row) & (rows < hi_row), cw_ref[r0:r1, :], 0.0)
            x = x_ref[r0:r1, :].astype(BF16)
            y = None
            for k in range(2):
                gate = jnp.dot(x, wg_s[slots[k]], preferred_element_type=F32)
                up = jnp.dot(x, wu_s[slots[k]], preferred_element_type=F32)
                act = (gate * jax.nn.sigmoid(gate) * up * cw[:, k:k + 1]).astype(BF16)
                yk = jnp.dot(act, wd_s[slots[k]], preferred_element_type=F32)
                y = yk if y is None else y + yk

            @pl.when(first_ref[s] == 1)
            def _():
                o_ref[r0:r1, :] = y
                for z0, z1 in ((0, r0), (r1, MOE_ROWS)):
                    if z1 > z0:
                        o_ref[z0:z1, :] = jnp.zeros((z1 - z0, o_ref.shape[1]), F32)

            @pl.when(first_ref[s] == 0)
            def _():
                o_ref[r0:r1, :] += y

        needs_lower, needs_upper = lo_row < half, hi_row > half
        pl.when(needs_lower & needs_upper)(lambda: run(0, MOE_ROWS))
        pl.when(needs_lower & jnp.logical_not(needs_upper))(lambda: run(0, half))
        pl.when(jnp.logical_not(needs_lower) & needs_upper)(lambda: run(half, MOE_ROWS))


def _moe_plan_kernel(offs_ref, kind_ref, rb_ref, bk_ref, pa_ref, pb_ref, first_ref, cast_ref, cpos_ref, pe_ref,
                     irb, ibk, *, n_steps_max):
    i32 = jnp.int32
    ng, epg, ppg = N_EXPERT_GROUPS, EXPERTS_PER_GROUP, PAIRS_PER_GROUP

    shift = MOE_ROWS.bit_length() - 1

    def bucket_body(bk, cnt):
        a, b = offs_ref[bk], offs_ref[bk + 1]
        first_blk = lax.shift_right_logical(a, shift)
        n_blk = jnp.where(b > a, lax.shift_right_logical(b - 1, shift) - first_blk + 1, 0)

        def block_body(j, cnt):
            irb[cnt] = first_blk + j
            ibk[cnt] = bk
            return cnt + 1

        return lax.fori_loop(0, n_blk, block_body, cnt)

    n_items = lax.fori_loop(0, ng * ppg, bucket_body, i32(0))

    def count_body(i, m):
        g = ibk[i] // ppg
        return tuple(m[k] + (g == k).astype(i32) for k in range(ng))

    m = lax.fori_loop(0, n_items, count_body, (i32(0),) * ng)

    def next_group(g):
        nxt = i32(-1)
        for k in range(ng - 1, 0, -1):
            nxt = jnp.where((k > g) & (m[k] > 0), k, nxt)
        return nxt

    def emit(s, kind, rb, bk, pa, pb, first, cast, cpos, pe):
        kind_ref[s], rb_ref[s], bk_ref[s], pa_ref[s], pb_ref[s] = kind, rb, bk, pa, pb
        first_ref[s], cast_ref[s], cpos_ref[s], pe_ref[s] = first, cast, cpos, pe

    def item_body(i, carry):
        s, gcur, parity, q, last_pe, last_rb = carry
        rb, bk = irb[i], ibk[i]
        g, pos = bk // ppg, bk % ppg
        new = g != gcur
        started = gcur >= 0
        loaders = jnp.where(new, jnp.where(started, jnp.maximum(epg - q, 0), epg), 0)
        parity = jnp.where(new & started, 1 - parity, parity)
        q = jnp.where(new, 0, q)
        for j in range(epg):
            on = j >= epg - loaders
            emit(s, STEP_LOAD, rb, bk, 0, 0, 0, 1, parity * epg + j, epg * g + j)
            last_pe = jnp.where(on, epg * g + j, last_pe)
            s = s + on.astype(i32)
        nxt = next_group(g)
        pre = (q < epg) & (nxt >= 0)
        pe = jnp.where(pre, epg * nxt + q, last_pe)
        slot_a, slot_b = i32(PAIR_SLOTS[0][0]), i32(PAIR_SLOTS[0][1])
        for p in range(1, ppg):
            slot_a = jnp.where(pos == p, PAIR_SLOTS[p][0], slot_a)
            slot_b = jnp.where(pos == p, PAIR_SLOTS[p][1], slot_b)
        emit(s, STEP_ITEM, rb, bk, parity * epg + slot_a, parity * epg + slot_b, (rb != last_rb).astype(i32),
             pre.astype(i32), (1 - parity) * epg + q, pe)
        return s + 1, g, parity, q + 1, pe, rb

    s, _, _, _, last_pe, last_rb = lax.fori_loop(
        0, n_items, item_body, (i32(0), i32(-1), i32(0), i32(0), i32(0), i32(-1)))
    last_bk = ibk[jnp.maximum(n_items - 1, 0)]

    def pad_body(s, _):
        emit(s, STEP_PAD, last_rb, last_bk, 0, 0, 0, 0, 0, last_pe)
        return 0

    lax.fori_loop(s, n_steps_max, pad_body, 0)


def _moe_steps(bucket, t):
    i32 = jnp.int32
    nbk = N_EXPERT_GROUPS * PAIRS_PER_GROUP
    order = jnp.argsort(bucket, stable=True).astype(i32)
    counts = jnp.sum((bucket[None, :] == jnp.arange(nbk, dtype=i32)[:, None]).astype(i32), axis=1)
    offs = jnp.concatenate([jnp.zeros((1,), i32), jnp.cumsum(counts).astype(i32)])
    assert MOE_ROWS & (MOE_ROWS - 1) == 0
    n_items_max = t // MOE_ROWS + nbk - 1
    n_steps_max = n_items_max + N_EXPERTS
    smem = pl.BlockSpec(memory_space=pltpu.SMEM)
    tables = pl.pallas_call(
        functools.partial(_moe_plan_kernel, n_steps_max=n_steps_max),
        in_specs=[smem],
        out_specs=[smem] * 9,
        out_shape=[jax.ShapeDtypeStruct((n_steps_max,), i32)] * 9,
        scratch_shapes=[pltpu.SMEM((n_items_max + 1,), i32)] * 2,
        name="moe_plan",
    )(offs)
    return order, (*tables, offs), n_steps_max


def _moe(xs, cws, w_gate, w_up, w_down, layer, tables, n_steps_max):
    t, d = xs.shape
    ff = w_gate.shape[3]
    w_map = lambda s, kind, rb, bk, pa, pb, fi, ca, cp, pe, of: (layer, pe[s], 0, 0)
    row_map = lambda s, kind, rb, *_: (rb[s], 0)
    nres = 2 * EXPERTS_PER_GROUP
    grid_spec = pltpu.PrefetchScalarGridSpec(
        num_scalar_prefetch=len(tables),
        grid=(n_steps_max,),
        in_specs=[pl.BlockSpec((MOE_ROWS, d), row_map), pl.BlockSpec((MOE_ROWS, 2), row_map),
                  pl.BlockSpec((None, None, d, ff), w_map), pl.BlockSpec((None, None, d, ff), w_map),
                  pl.BlockSpec((None, None, ff, d), w_map)],
        out_specs=pl.BlockSpec((MOE_ROWS, d), row_map),
        scratch_shapes=[pltpu.VMEM((nres, d, ff), BF16), pltpu.VMEM((nres, d, ff), BF16),
                        pltpu.VMEM((nres, ff, d), BF16)],
    )
    return pl.pallas_call(
        _moe_kernel,
        grid_spec=grid_spec,
        out_shape=jax.ShapeDtypeStruct((t, d), F32),
        compiler_params=_params(("arbitrary",), MOE_VMEM_LIMIT),
        name="moe_grouped",
    )(*tables, xs, cws, w_gate, w_up, w_down)


def _take_rows(a, idx):
    return a.at[idx].get(mode="promise_in_bounds", unique_indices=True)


def _final_kernel(x_ref, y_ref, g_ref, o_ref):
    o_ref[...] = x_ref[...] + g_ref[...] * y_ref[...]


def _final(x1, y, g2, seq):
    t, d = x1.shape
    tm = min(TOKEN_TILE, seq)
    per_b = seq // tm
    tok = lambda w: pl.BlockSpec((tm, w), lambda i: (i, 0))
    return pl.pallas_call(
        _final_kernel,
        grid=(t // tm,),
        in_specs=[tok(d), tok(d), pl.BlockSpec((None, 1, d), lambda i: (i // per_b, 0, 0))],
        out_specs=tok(d),
        out_shape=jax.ShapeDtypeStruct((t, d), F32),
        compiler_params=_params(("parallel",)),
        name="final_residual",
    )(x1, y, g2)


def kernel(x, c, positions, ada_w, ada_b, norm1_g, w_in, q_norm_g, k_norm_g, attn_sink, lam_re, lam_im, ssm_b_re, ssm_b_im, ssm_c_re, ssm_c_im, ssm_d, ssm_log_dt, w_glu, attn_out_g, ssm_out_g, w_out, norm2_g, w_router, router_bias, w_exp_gate, w_exp_up, w_exp_down):
    batch, seq, d = x.shape
    depth = ada_w.shape[0]
    t = batch * seq
    assert seq % ATTN_BLOCK == 0 and seq % SSM_CHUNK == 0 and t % MOE_ROWS == 0

    mod = _adaln_mod(c, ada_w, ada_b).reshape(depth, 6, batch, 1, d)
    cos, sin = _rope_tables(positions)
    head_sum, rot = _rope_constants()
    bias = _attn_bias()
    w_router_t = w_router.T
    s5_mats = _s5_prep(lam_re, lam_im, ssm_b_re, ssm_b_im, ssm_c_re, ssm_c_im, ssm_d, ssm_log_dt)
    router_bias_col = router_bias.reshape(N_EXPERTS, 1)

    xf = x.reshape(t, d)
    res = None
    for l in range(depth):
        sh1, sc1, g1, sh2, sc2, g2 = (mod[l, j] for j in range(6))
        qg = (jnp.tile(q_norm_g[l], N_Q_HEADS) * HEAD_DIM ** -0.5).reshape(1, ATTN_WIDTH)
        kg = jnp.tile(k_norm_g[l], N_KV_HEADS).reshape(1, KV_WIDTH)
        outs = _inproj(xf, res, sc1, sh1, norm1_g[l].reshape(1, d), w_in[l].astype(BF16), qg, kg, head_sum, rot,
                       cos, sin, seq)
        if res is None:
            q, kx, vx, uc = outs
        else:
            q, kx, vx, uc, xf = outs
        attn = _attention(q, kx, vx, attn_sink[l], attn_out_g[l].reshape(1, ATTN_WIDTH), bias, batch, seq)
        yc = _s5_scan(uc, s5_mats, l, seq // SSM_CHUNK, batch)
        x1, h2, cw, gid = _post(xf, attn, yc, w_glu[l].astype(BF16), ssm_out_g[l].reshape(1, SSM_WIDTH),
                                 w_out[l].astype(BF16), g1, norm2_g[l].reshape(1, d), sc2, sh2, w_router_t,
                                 router_bias_col, seq)
        order, tables, n_steps_max = _moe_steps(gid.reshape(t), t)
        y_sorted = _moe(_take_rows(h2, order), _take_rows(cw.T, order), w_exp_gate, w_exp_up, w_exp_down, l,
                        tables, n_steps_max)
        y = _take_rows(y_sorted, jnp.argsort(order).astype(jnp.int32))
        xf, res = x1, (y, g2)
    y, g2 = res
    return _final(xf, y, g2, seq).reshape(batch, seq, d)
```

```python
import functools
import math

import numpy as np
import jax
import jax.numpy as jnp
from jax import lax
from jax.experimental import pallas as pl
from jax.experimental.pallas import tpu as pltpu

F32 = jnp.float32
BF16 = jnp.bfloat16

HEAD_DIM = 64
N_Q_HEADS = 8
N_KV_HEADS = 2
Q_PER_KV = N_Q_HEADS // N_KV_HEADS
ATTN_WIDTH = N_Q_HEADS * HEAD_DIM
KV_WIDTH = N_KV_HEADS * HEAD_DIM
ATTN_BLOCK = 128
ATTN_Q_TILE = 2048
ROPE_THETA = 10000.0
ROPE_SLAB = 256
LANES = 128
SSM_GROUP_CH = 16
SSM_GROUPS = 32
SSM_WIDTH = SSM_GROUPS * SSM_GROUP_CH
SSM_STATE = 64
SSM_CHUNK = 16
SSM_COLS = SSM_WIDTH // LANES
COL_GROUPS = LANES // SSM_GROUP_CH
COL_STATE = COL_GROUPS * SSM_STATE
CHUNK_LANES = SSM_CHUNK * LANES
SSM_NSPLIT = 4
S5_ROW_PAD = 8
N_EXPERTS = 16
N_EXPERT_GROUPS = 4
EXPERTS_PER_GROUP = N_EXPERTS // N_EXPERT_GROUPS
PAIRS_PER_GROUP = EXPERTS_PER_GROUP * (EXPERTS_PER_GROUP - 1) // 2
PAIR_SLOTS = ((0, 1), (0, 2), (0, 3), (1, 3), (1, 2), (3, 2))
EPS = 1e-6
LOG2_E = math.log2(math.e)
MASK_BIAS = -1e30

TOKEN_TILE = 1024
POST_SPLIT = 1
MOE_ROWS = 256
VMEM_LIMIT = 48 * 1024 * 1024
MOE_VMEM_LIMIT = 56 * 1024 * 1024
STEP_PAD, STEP_LOAD, STEP_ITEM = 0, 1, 2


def _params(sem, vmem=None):
    return pltpu.CompilerParams(dimension_semantics=sem, vmem_limit_bytes=vmem)


def _rms(x, g):
    return x * lax.rsqrt(jnp.mean(x * x, axis=-1, keepdims=True) + EPS) * g


def _mod_kernel(c_ref, w_ref, b_ref, o_ref):
    c = c_ref[...]
    s = c * jax.nn.sigmoid(c)
    o_ref[...] = jnp.dot(s.astype(BF16), w_ref[...].astype(BF16), preferred_element_type=F32) + b_ref[...]


def _adaln_mod(c, ada_w, ada_b):
    depth, d, d6 = ada_w.shape
    nb = c.shape[0]
    n6 = d6 // d
    return pl.pallas_call(
        _mod_kernel,
        grid=(depth, n6),
        in_specs=[pl.BlockSpec((nb, d), lambda l, j: (0, 0)),
                  pl.BlockSpec((None, d, d), lambda l, j: (l, 0, j)),
                  pl.BlockSpec((None, None, 1, d), lambda l, j: (l, j, 0, 0))],
        out_specs=pl.BlockSpec((None, None, nb, d), lambda l, j: (l, j, 0, 0)),
        out_shape=jax.ShapeDtypeStruct((depth, n6, nb, d), F32),
        compiler_params=_params(("arbitrary", "arbitrary"), VMEM_LIMIT),
        name="adaln_mod",
    )(c, ada_w, ada_b.reshape(depth, n6, 1, d))


def _rope_kernel(pos_ref, freq_ref, cos_ref, sin_ref):
    ang = pos_ref[...].astype(F32) * freq_ref[...]
    cos_ref[...] = jnp.cos(ang)
    sin_ref[...] = jnp.sin(ang)


def _rope_tables(positions):
    half = HEAD_DIM // 2
    t = positions.size
    per_row = LANES // half
    rows = t // per_row
    pos_rep = jnp.repeat(positions.reshape(rows, per_row), half, axis=1)
    freq = (ROPE_THETA ** (-np.arange(half, dtype=np.float64) / half)).astype(np.float32)
    freq_row = jnp.asarray(np.tile(freq, per_row)[None, :])
    blk = min(rows, 512)
    cos, sin = pl.pallas_call(
        _rope_kernel,
        grid=(rows // blk,),
        in_specs=[pl.BlockSpec((blk, LANES), lambda i: (i, 0)),
                  pl.BlockSpec((1, LANES), lambda i: (0, 0))],
        out_specs=[pl.BlockSpec((blk, LANES), lambda i: (i, 0))] * 2,
        out_shape=[jax.ShapeDtypeStruct((rows, LANES), F32)] * 2,
        compiler_params=_params(("arbitrary",)),
        name="rope_tables",
    )(pos_rep, freq_row)
    widen = lambda a: jnp.tile(a.reshape(t, half), (1, per_row))
    return widen(cos), widen(sin)


def _rope_constants():
    lane = np.arange(ROPE_SLAB)
    head_sum = (lane[:, None] // HEAD_DIM == lane[None, :] // HEAD_DIM).astype(np.float32)
    half = HEAD_DIM // 2
    rot = np.zeros((ROPE_SLAB, ROPE_SLAB), np.float32)
    for d in range(ROPE_SLAB):
        if d % HEAD_DIM < half:
            rot[d + half, d] = -1.0
        else:
            rot[d - half, d] = 1.0
    return jnp.asarray(head_sum, BF16), jnp.asarray(rot, BF16)


def _inproj_kernel(*refs, has_res):
    if has_res:
        (x_ref, y_ref, g2_ref, sc_ref, sh_ref, n1_ref, w_ref, qg_ref, kg_ref, hs_ref, rot_ref,
         cos_ref, sin_ref, q_ref, k_ref, v_ref, uc_ref, xo_ref, u_scr) = refs
        x = x_ref[...] + g2_ref[...] * y_ref[...]
        xo_ref[...] = x
    else:
        (x_ref, sc_ref, sh_ref, n1_ref, w_ref, qg_ref, kg_ref, hs_ref, rot_ref,
         cos_ref, sin_ref, q_ref, k_ref, v_ref, uc_ref, u_scr) = refs
        x = x_ref[...]
    h = _rms(x, n1_ref[...] * (1.0 + sc_ref[...])) + sh_ref[...]
    proj = jnp.dot(h.astype(BF16), w_ref[...], preferred_element_type=F32)
    q = proj[:, :ATTN_WIDTH]
    k = proj[:, ATTN_WIDTH:ATTN_WIDTH + KV_WIDTH]
    v = proj[:, ATTN_WIDTH + KV_WIDTH:ATTN_WIDTH + 2 * KV_WIDTH]
    cos = cos_ref[...]
    sin = sin_ref[...]
    reps = ATTN_WIDTH // LANES
    cos_q = jnp.concatenate([cos] * reps, axis=1)
    sin_q = jnp.concatenate([sin] * reps, axis=1)

    def head_norm_rope(t, gain, c, s):
        outs = []
        for lo in range(0, t.shape[1], ROPE_SLAB):
            wd = min(ROPE_SLAB, t.shape[1] - lo)
            ts, lanes = t[:, lo:lo + wd], slice(lo, lo + wd)
            ssq = jnp.dot((ts * ts).astype(BF16), hs_ref[:wd, :wd], preferred_element_type=F32)
            tn = (ts * lax.rsqrt(ssq * (1.0 / HEAD_DIM) + EPS) * gain[:, lanes]).astype(BF16)
            tr = jnp.dot(tn, rot_ref[:wd, :wd], preferred_element_type=F32)
            outs.append(tn.astype(F32) * c[:, lanes] + tr * s[:, lanes])
        return outs[0] if len(outs) == 1 else jnp.concatenate(outs, axis=1)

    qo = head_norm_rope(q, qg_ref[...], cos_q, sin_q)
    ko = head_norm_rope(k, kg_ref[...], cos, sin)
    q_ref[...] = qo.astype(BF16)
    k_ref[...] = jnp.concatenate([ko, pltpu.roll(ko, HEAD_DIM, axis=1)], axis=1).astype(BF16)
    v_ref[...] = jnp.concatenate([v, pltpu.roll(v, HEAD_DIM, axis=1)], axis=1).astype(BF16)
    u0 = ATTN_WIDTH + 2 * KV_WIDTH
    nchunk = u_scr.shape[1] // SSM_CHUNK
    for j in range(SSM_COLS):
        u_scr[j] = proj[:, u0 + j * LANES:u0 + (j + 1) * LANES]
    for s in range(SSM_CHUNK):
        for j in range(SSM_COLS):
            lanes = slice(s * SSM_WIDTH + j * LANES, s * SSM_WIDTH + (j + 1) * LANES)
            uc_ref[:, lanes] = u_scr[j, pl.ds(s, nchunk, stride=SSM_CHUNK), :].astype(BF16)


def _inproj(x, res, sc1, sh1, n1g, w_in, qg, kg, head_sum, rot, cos, sin, seq):
    t, d = x.shape
    tm = min(TOKEN_TILE, seq)
    per_b = seq // tm
    in_width = w_in.shape[1]
    tok = lambda w: pl.BlockSpec((tm, w), lambda i: (i, 0))
    const = lambda a: pl.BlockSpec(a.shape, lambda i: (0,) * a.ndim)
    per_batch = pl.BlockSpec((None, 1, d), lambda i: (i // per_b, 0, 0))
    chunked = pl.BlockSpec((tm // SSM_CHUNK, SSM_CHUNK * SSM_WIDTH), lambda i: (i, 0))
    ins, specs = [x], [tok(d)]
    if res is not None:
        y_prev, g2_prev = res
        ins += [y_prev, g2_prev]
        specs += [tok(d), per_batch]
    ins += [sc1, sh1, n1g, w_in, qg, kg, head_sum, rot, cos, sin]
    specs += [per_batch, per_batch, const(n1g), const(w_in), const(qg), const(kg), const(head_sum), const(rot),
              tok(LANES), tok(LANES)]
    out_shape = [jax.ShapeDtypeStruct((t, ATTN_WIDTH), BF16), jax.ShapeDtypeStruct((t, 2 * KV_WIDTH), BF16),
                 jax.ShapeDtypeStruct((t, 2 * KV_WIDTH), BF16),
                 jax.ShapeDtypeStruct((t // SSM_CHUNK, SSM_CHUNK * SSM_WIDTH), BF16)]
    out_specs = [tok(ATTN_WIDTH), tok(2 * KV_WIDTH), tok(2 * KV_WIDTH), chunked]
    if res is not None:
        out_shape.append(jax.ShapeDtypeStruct((t, d), F32))
        out_specs.append(tok(d))
    assert in_width == ATTN_WIDTH + 2 * KV_WIDTH + SSM_WIDTH
    return pl.pallas_call(
        functools.partial(_inproj_kernel, has_res=res is not None),
        grid=(t // tm,),
        in_specs=specs,
        out_specs=out_specs,
        out_shape=out_shape,
        scratch_shapes=[pltpu.VMEM((SSM_COLS, tm, LANES), F32)],
        compiler_params=_params(("parallel",), VMEM_LIMIT),
        name="inproj",
    )(*ins)


def _attn_kernel(sink_ref, q_ref, kc_ref, kp_ref, vc_ref, vp_ref, bias_ref, g_ref, o_ref):
    nsub = q_ref.shape[0] // ATTN_BLOCK
    kk = jnp.concatenate([kp_ref[...], kc_ref[...]], axis=0)
    vv = jnp.concatenate([vp_ref[...], vc_ref[...]], axis=0)
    low = lax.broadcasted_iota(jnp.int32, (kk.shape[0], KV_WIDTH), 1) < HEAD_DIM
    zero = jnp.zeros((kk.shape[0], KV_WIDTH), BF16)

    def variants(a):
        nat, swp = a[:, :KV_WIDTH], a[:, KV_WIDTH:]
        return {(0, 0): jnp.where(low, nat, zero), (0, 1): jnp.where(low, zero, swp),
                (1, 0): jnp.where(low, swp, zero), (1, 1): jnp.where(low, zero, nat)}

    kvar, vvar = variants(kk), variants(vv)
    band = bias_ref[1]
    first = bias_ref[jnp.minimum(pl.program_id(1), 1)]
    upper = lax.broadcasted_iota(jnp.int32, (2 * ATTN_BLOCK, 1), 0) < ATTN_BLOCK
    for j in range(nsub):
        bias = first if j == 0 else band
        bias2 = jnp.concatenate([bias, bias], axis=0)
        keys = slice(j * ATTN_BLOCK, (j + 2) * ATTN_BLOCK)
        qrows = slice(j * ATTN_BLOCK, (j + 1) * ATTN_BLOCK)
        tiles = [None] * (N_Q_HEADS // 2)
        for kv in range(N_KV_HEADS):
            for half in range(2):
                pairs = (2 * kv, 2 * kv + 1)
                heads = (2 * pairs[0] + half, 2 * pairs[1] + half)
                qs = jnp.concatenate([q_ref[qrows, p * LANES:(p + 1) * LANES] for p in pairs], axis=0)
                s = lax.dot_general(qs, kvar[(kv, half)][keys], (((1,), (1,)), ((), ())),
                                    preferred_element_type=F32) + bias2
                sink = jnp.where(upper, sink_ref[heads[0]], sink_ref[heads[1]]) * LOG2_E
                m = jnp.maximum(jnp.max(s, axis=-1, keepdims=True), sink)
                p = jnp.exp2(s - m)
                denom = jnp.sum(p, axis=-1, keepdims=True) + jnp.exp2(sink - m)
                o = jnp.dot(p.astype(BF16), vvar[(kv, half)][keys], preferred_element_type=F32) * (1.0 / denom)
                for r, pr in enumerate(pairs):
                    part = o[r * ATTN_BLOCK:(r + 1) * ATTN_BLOCK]
                    tiles[pr] = part if tiles[pr] is None else tiles[pr] + part
        a = jnp.concatenate(tiles, axis=1)
        o_ref[qrows, :] = _rms(a, g_ref[...]).astype(BF16)


def _attn_bias():
    qi = np.arange(ATTN_BLOCK)[:, None]
    sj = np.arange(2 * ATTN_BLOCK)[None, :]
    diff = qi + ATTN_BLOCK - sj
    band = (diff >= 0) & (diff < ATTN_BLOCK)
    first = band & (sj >= ATTN_BLOCK)
    return jnp.asarray(np.where(np.stack([first, band]), 0.0, MASK_BIAS).astype(np.float32))


def _attention(q, kx, vx, sink, out_g, bias, batch, seq):
    t = q.shape[0]
    qb = min(ATTN_Q_TILE, seq)
    nsub = qb // ATTN_BLOCK
    nq = seq // qb
    nb = seq // ATTN_BLOCK
    cur = lambda w: pl.BlockSpec((qb, w), lambda b, n, s: (b * nq + n, 0))
    prev = lambda w: pl.BlockSpec((ATTN_BLOCK, w), lambda b, n, s: (b * nb + jnp.maximum(n * nsub - 1, 0), 0))
    grid_spec = pltpu.PrefetchScalarGridSpec(
        num_scalar_prefetch=1,
        grid=(batch, nq),
        in_specs=[cur(ATTN_WIDTH), cur(2 * KV_WIDTH), prev(2 * KV_WIDTH), cur(2 * KV_WIDTH), prev(2 * KV_WIDTH),
                  pl.BlockSpec(bias.shape, lambda b, n, s: (0, 0, 0)),
                  pl.BlockSpec((1, ATTN_WIDTH), lambda b, n, s: (0, 0))],
        out_specs=cur(ATTN_WIDTH),
    )
    return pl.pallas_call(
        _attn_kernel,
        grid_spec=grid_spec,
        out_shape=jax.ShapeDtypeStruct((t, ATTN_WIDTH), BF16),
        compiler_params=_params(("parallel", "arbitrary")),
        name="swa_attention",
    )(sink, q, kx, kx, vx, vx, bias, out_g)


def _spread(x, expander3):
    hi = x.astype(BF16)
    r1 = x - hi.astype(F32)
    mid = r1.astype(BF16)
    lo = (r1 - mid.astype(F32)).astype(BF16)
    return jnp.dot(jnp.concatenate([hi, mid, lo], axis=1), expander3, preferred_element_type=F32)


def _s5_prep_kernel(lr_re_ref, lr_im_ref, ldt_ref, bt_re_ref, bt_im_ref, ct_re_ref, ct_im_ref,
                    d_ref, lcol_re_ref, lcol_im_ref, ldtcol_ref, exp_ref, exph_ref, expt_ref, expw_ref,
                    t_ref, w_ref, v_ref, la_ref, lb_ref):
    hi = lax.Precision.HIGHEST
    nl = SSM_CHUNK
    low = lax.broadcasted_iota(jnp.int32, (1, 2 * SSM_STATE), 1) < SSM_STATE
    row_low = lax.broadcasted_iota(jnp.int32, (2 * SSM_STATE, 1), 0) < SSM_STATE
    jcol = lax.broadcasted_iota(jnp.int32, (nl, 1), 0).astype(F32)
    kt_lane = lax.broadcasted_iota(jnp.int32, (SSM_GROUP_CH, nl * SSM_GROUP_CH), 1)
    kt_row = lax.broadcasted_iota(jnp.int32, (SSM_GROUP_CH, nl * SSM_GROUP_CH), 0)

    w_all, v_all, kt_all = [], [], []
    for gm in range(COL_GROUPS):
        dt = jnp.exp(ldt_ref[gm])
        lam_re, lam_im = lr_re_ref[gm], lr_im_ref[gm]
        a_r, th_r = lam_re * dt, lam_im * dt

        er = jnp.exp(jcol * a_r)
        pw_re, pw_im = er * jnp.cos(jcol * th_r), er * jnp.sin(jcol * th_r)

        nr, ni = pw_re[1:2, :] - 1.0, pw_im[1:2, :]
        den = lam_re * lam_re + lam_im * lam_im
        c_re, c_im = (nr * lam_re + ni * lam_im) / den, (ni * lam_re - nr * lam_im) / den
        bt_re, bt_im = bt_re_ref[gm], bt_im_ref[gm]
        bb_re, bb_im = c_re * bt_re - c_im * bt_im, c_re * bt_im + c_im * bt_re

        w_rows = []
        for s in range(nl):
            j = nl - 1 - s
            pr, pi = pw_re[j:j + 1, :], pw_im[j:j + 1, :]
            w_rows.append(jnp.where(low, pr * bb_re - pi * bb_im, pr * bb_im + pi * bb_re))
        w_all.append(w_rows)

        pw_re_t, pw_im_t = pw_re.T, pw_im.T
        pc, ps = _spread(pw_re_t, exp_ref[...]), _spread(pw_im_t, exp_ref[...])
        ct_re, ct_im = _spread(ct_re_ref[gm], exph_ref[...]), _spread(ct_im_ref[gm], exph_ref[...])
        a_re, a_im = ct_re * pc - ct_im * ps, ct_re * ps + ct_im * pc
        a_cat = jnp.where(row_low, a_re, -a_im)
        l1_re, l1_im = pw_re_t[:, 1:2], pw_im_t[:, 1:2]
        v_re, v_im = a_re * l1_re - a_im * l1_im, a_re * l1_im + a_im * l1_re
        v_all.append(jnp.where(row_low, v_re, -v_im))

        kt = jnp.dot(jnp.where(low, bb_re, bb_im), a_cat, precision=hi, preferred_element_type=F32)
        kt_all.append(kt + jnp.where(kt_lane == kt_row, d_ref[gm], 0.0))

    def expand(stacked, expander, row_group, lane_group):
        wide = jnp.dot(stacked.astype(BF16), expander, preferred_element_type=F32)
        r = lax.broadcasted_iota(jnp.int32, wide.shape, 0)
        c = lax.broadcasted_iota(jnp.int32, wide.shape, 1)
        return jnp.where(row_group(r) == lane_group(c), wide, 0.0).astype(BF16)

    chan_group = lambda i: (i >> 4) & (COL_GROUPS - 1)
    state_group = lambda i: (i >> 6) & (COL_GROUPS - 1)

    bd = expand(jnp.concatenate(kt_all, axis=0), expt_ref[...], chan_group, chan_group)
    t_ref[0:LANES, :] = bd
    for s in range(1, nl):
        t_ref[s * LANES:(s + 1) * LANES, :] = jnp.concatenate(
            [jnp.zeros((LANES, s * LANES), BF16), bd[:, :CHUNK_LANES - s * LANES]], axis=1)

    w_stack = jnp.concatenate([w_all[gm][s] for s in range(nl) for gm in range(COL_GROUPS)], axis=0)
    w_ref[...] = expand(w_stack, expw_ref[...], chan_group, state_group)

    v_stack = jnp.concatenate([v_all[gm][half * SSM_STATE:(half + 1) * SSM_STATE, :]
                               for half in range(2) for gm in range(COL_GROUPS)], axis=0)
    v_ref[...] = expand(v_stack, expt_ref[...], state_group, chan_group)

    dtc = jnp.exp(ldtcol_ref[...])
    e16 = jnp.exp(nl * lcol_re_ref[...] * dtc)
    ang = nl * lcol_im_ref[...] * dtc
    la_ref[...] = e16 * jnp.cos(ang)
    lb_ref[...] = e16 * jnp.sin(ang)


def _s5_prep(lam_re, lam_im, b_re, b_im, c_re, c_im, d_skip, log_dt):
    g, p, h, nl = SSM_GROUPS, SSM_STATE, SSM_GROUP_CH, SSM_CHUNK
    cg = COL_GROUPS
    nc = lam_re.shape[0] * SSM_COLS
    col = lambda a: a.reshape((nc, cg) + a.shape[2:])
    dup_row = lambda a: col(jnp.tile(a, (1, 1, 2))[:, :, None, :])
    bt = lambda a: col(jnp.tile(jnp.swapaxes(a, 2, 3), (1, 1, 1, 2)))
    ct = lambda a: col(jnp.tile(jnp.swapaxes(a, 2, 3), (1, 1, 2, 1)))
    d_pad = col(jnp.pad(d_skip.reshape(-1, g, 1, h), ((0, 0), (0, 0), (0, 0), (0, nl * h - h))))
    wide = lambda a: a.reshape(nc, 1, cg * p)
    expand = jnp.asarray(np.tile(np.repeat(np.eye(nl, dtype=np.float32), h, axis=1), (3, 1)), BF16)
    expand_h = jnp.asarray(np.tile(np.eye(h, dtype=np.float32), (3, nl)), BF16)
    exp_t = np.zeros((nl, h, nl, cg, h), np.float32)
    exp_w = np.zeros((2, p, 2, cg, p), np.float32)
    for gm in range(cg):
        exp_t[:, :, :, gm, :] = np.eye(nl * h, dtype=np.float32).reshape(nl, h, nl, h)
        exp_w[:, :, :, gm, :] = np.eye(2 * p, dtype=np.float32).reshape(2, p, 2, p)
    exp_t = jnp.asarray(exp_t.reshape(nl * h, CHUNK_LANES), BF16)
    exp_w = jnp.asarray(exp_w.reshape(2 * p, 2 * COL_STATE), BF16)
    blk = lambda *s: pl.BlockSpec((None,) + s, lambda i: (i,) + (0,) * len(s))
    const = lambda a: pl.BlockSpec(a.shape, lambda i: (0,) * a.ndim)
    lw = nl * h
    return pl.pallas_call(
        _s5_prep_kernel,
        grid=(nc,),
        in_specs=[blk(cg, 1, 2 * p), blk(cg, 1, 2 * p), blk(cg, 1, 1),
                  blk(cg, h, 2 * p), blk(cg, h, 2 * p), blk(cg, 2 * p, h), blk(cg, 2 * p, h), blk(cg, 1, lw),
                  blk(1, cg * p), blk(1, cg * p), blk(1, cg * p), const(expand), const(expand_h), const(exp_t),
                  const(exp_w)],
        out_specs=[blk(CHUNK_LANES, CHUNK_LANES), blk(CHUNK_LANES, 2 * COL_STATE), blk(2 * COL_STATE, CHUNK_LANES),
                   blk(1, COL_STATE), blk(1, COL_STATE)],
        out_shape=[jax.ShapeDtypeStruct((nc, CHUNK_LANES, CHUNK_LANES), BF16),
                   jax.ShapeDtypeStruct((nc, CHUNK_LANES, 2 * COL_STATE), BF16),
                   jax.ShapeDtypeStruct((nc, 2 * COL_STATE, CHUNK_LANES), BF16),
                   jax.ShapeDtypeStruct((nc, 1, COL_STATE), F32), jax.ShapeDtypeStruct((nc, 1, COL_STATE), F32)],
        compiler_params=_params(("parallel",), VMEM_LIMIT),
        name="s5_prep",
    )(dup_row(lam_re), dup_row(lam_im), col(log_dt[:, :, None, None]),
      bt(b_re), bt(b_im), ct(c_re), ct(c_im), d_pad, wide(lam_re), wide(lam_im),
      wide(jnp.repeat(log_dt, p, axis=1)), expand, expand_h, exp_t, exp_w)


def _s5_kernel(*refs, nchunks, nb):
    uc_refs = refs[:SSM_CHUNK]
    t_ref, w_ref, v_ref, la_ref, lb_ref, o_ref, ucat_ref, s_ref, xp_ref = refs[SSM_CHUNK:]

    @pl.when(pl.program_id(1) == 0)
    def _():
        for s in range(SSM_CHUNK):
            ucat_ref[:, s * LANES:(s + 1) * LANES] = uc_refs[s][...]
        s_in = jnp.dot(ucat_ref[...], w_ref[...], preferred_element_type=F32)
        nblk = COL_STATE // LANES
        pitch = s_ref.shape[1] // nb
        for b in range(2 * nblk):
            for q in range(nb):
                s_ref[b, q * pitch:q * pitch + nchunks, :] = s_in[q * nchunks:(q + 1) * nchunks, b * LANES:(b + 1) * LANES]
        lr = [jnp.broadcast_to(la_ref[:, b * LANES:(b + 1) * LANES], (nb, LANES)) for b in range(nblk)]
        li = [jnp.broadcast_to(lb_ref[:, b * LANES:(b + 1) * LANES], (nb, LANES)) for b in range(nblk)]

        def step(c, carry):
            rows = pl.ds(c, nb, stride=pitch)
            out = []
            for b in range(nblk):
                re, im = carry[2 * b], carry[2 * b + 1]
                xp_ref[b, rows, :] = re
                xp_ref[nblk + b, rows, :] = im
                out.append(lr[b] * re - li[b] * im + s_ref[b, rows, :])
                out.append(lr[b] * im + li[b] * re + s_ref[nblk + b, rows, :])
            return tuple(out)

        zero = jnp.zeros((nb, LANES), F32)
        lax.fori_loop(0, nchunks, step, (zero,) * (2 * nblk), unroll=4)

    pitch = xp_ref.shape[1] // nb
    xp = jnp.concatenate(
        [jnp.concatenate([xp_ref[b, q * pitch:q * pitch + nchunks, :] for q in range(nb)], axis=0)
         for b in range(2 * COL_STATE // LANES)], axis=1).astype(BF16)
    inter = jnp.dot(xp, v_ref[...], preferred_element_type=F32)
    for kk in range(SSM_NSPLIT):
        @pl.when(pl.program_id(1) == kk)
        def _():
            live = (kk + 1) * (CHUNK_LANES // SSM_NSPLIT)
            intra = jnp.dot(ucat_ref[:, :live], t_ref[:live, :], preferred_element_type=F32)
            o_ref[...] = (intra + inter).astype(BF16)


def _s5_scan(uc, mats, layer, nchunks, nb):
    rows = uc.shape[0]
    c0 = layer * SSM_COLS
    split = CHUNK_LANES // SSM_NSPLIT
    u_spec = lambda s: pl.BlockSpec((rows, LANES), lambda j, k: (0, SSM_COLS * s + j))
    return pl.pallas_call(
        functools.partial(_s5_kernel, nchunks=nchunks, nb=nb),
        grid=(SSM_COLS, SSM_NSPLIT),
        in_specs=[u_spec(s) for s in range(SSM_CHUNK)] + [
            pl.BlockSpec((None, CHUNK_LANES, split), lambda j, k: (c0 + j, 0, k)),
            pl.BlockSpec((None, CHUNK_LANES, 2 * COL_STATE), lambda j, k: (c0 + j, 0, 0)),
            pl.BlockSpec((None, 2 * COL_STATE, split), lambda j, k: (c0 + j, 0, k)),
            pl.BlockSpec((None, 1, COL_STATE), lambda j, k: (c0 + j, 0, 0)),
            pl.BlockSpec((None, 1, COL_STATE), lambda j, k: (c0 + j, 0, 0))],
        out_specs=pl.BlockSpec((None, rows, split), lambda j, k: (j, 0, k)),
        out_shape=jax.ShapeDtypeStruct((SSM_COLS, rows, CHUNK_LANES), BF16),
        scratch_shapes=[pltpu.VMEM((rows, CHUNK_LANES), BF16),
                        pltpu.VMEM((2 * COL_STATE // LANES, nb * (nchunks + S5_ROW_PAD), LANES), F32),
                        pltpu.VMEM((2 * COL_STATE // LANES, nb * (nchunks + S5_ROW_PAD), LANES), F32)],
        compiler_params=_params(("parallel", "arbitrary"), VMEM_LIMIT),
        name="s5_scan",
    )(*([uc] * SSM_CHUNK), *mats)


def _route(logits, bias):
    m = jnp.max(logits, axis=0, keepdims=True)
    e = jnp.exp(logits - m)
    probs = e / jnp.sum(e, axis=0, keepdims=True)
    sel = probs + bias
    row = lambda a, i: a[i:i + 1, :]
    best_score, best = None, None
    for grp in range(N_EXPERT_GROUPS):
        a, b, c, d = (row(sel, EXPERTS_PER_GROUP * grp + i) for i in range(EXPERTS_PER_GROUP))
        hab, lab, hcd, lcd = jnp.maximum(a, b), jnp.minimum(a, b), jnp.maximum(c, d), jnp.minimum(c, d)
        top1 = jnp.maximum(hab, hcd)
        top2 = jnp.maximum(jnp.maximum(lab, lcd), jnp.minimum(hab, hcd))
        score = top1 + top2
        if grp == 0:
            best_score, best = score, jnp.zeros(score.shape, jnp.int32)
        else:
            better = score > best_score
            best = jnp.where(better, grp, best)
            best_score = jnp.where(better, score, best_score)

    def pick(a, i):
        out = row(a, i)
        for grp in range(1, N_EXPERT_GROUPS):
            out = jnp.where(best == grp, row(a, EXPERTS_PER_GROUP * grp + i), out)
        return out

    s_in = [pick(sel, i) for i in range(EXPERTS_PER_GROUP)]
    p_in = [pick(probs, i) for i in range(EXPERTS_PER_GROUP)]
    neg = jnp.full(s_in[0].shape, -jnp.inf, F32)

    def argmax_first(vals):
        idx, val = jnp.zeros(vals[0].shape, jnp.int32), vals[0]
        for i in range(1, len(vals)):
            better = vals[i] > val
            idx = jnp.where(better, i, idx)
            val = jnp.where(better, vals[i], val)
        return idx

    i1 = argmax_first(s_in)
    i2 = argmax_first([jnp.where(i1 == i, neg, s_in[i]) for i in range(EXPERTS_PER_GROUP)])
    zero = jnp.zeros(p_in[0].shape, F32)
    g1 = sum(jnp.where(i1 == i, p_in[i], zero) for i in range(EXPERTS_PER_GROUP))
    g2 = sum(jnp.where(i2 == i, p_in[i], zero) for i in range(EXPERTS_PER_GROUP))
    tot = g1 + g2
    w1, w2 = g1 / tot, g2 / tot
    first_low = i1 < i2
    low, high = jnp.minimum(i1, i2), jnp.maximum(i1, i2)
    w_low, w_high = jnp.where(first_low, w1, w2), jnp.where(first_low, w2, w1)
    pos = jnp.where(low == 0, high - 1, jnp.where(low == 1, jnp.where(high == 2, 4, 3), 5))
    swap = low == 2
    bucket = best * PAIRS_PER_GROUP + pos
    return jnp.concatenate([jnp.where(swap, w_high, w_low), jnp.where(swap, w_low, w_high)], axis=0), bucket


def _router_logits(w_t, h):
    w_hi = w_t.astype(BF16)
    w_r = w_t - w_hi.astype(F32)
    w_mid = w_r.astype(BF16)
    w_lo = (w_r - w_mid.astype(F32)).astype(BF16)
    h_hi = h.astype(BF16)
    h_lo = (h - h_hi.astype(F32)).astype(BF16)
    dims = (((1,), (1,)), ((), ()))
    a = lax.dot_general(jnp.concatenate([w_hi, w_mid, w_lo], axis=0), h_hi, dims, preferred_element_type=F32)
    b = lax.dot_general(jnp.concatenate([w_hi, w_mid], axis=0), h_lo, dims, preferred_element_type=F32)
    e = w_t.shape[0]
    return a[:e] + a[e:2 * e] + a[2 * e:] + b[:e] + b[e:]


def _post_kernel(x_ref, at_ref, yc_ref, wglu_ref, gs_ref, wo_ref, g1_ref, n2_ref, sc_ref, sh_ref,
                 wrt_ref, rb_ref, x1_ref, h2_ref, cw_ref, gid_ref, y_scr):
    nchunk = y_scr.shape[1] // SSM_CHUNK
    for s in range(SSM_CHUNK):
        for j in range(SSM_COLS):
            y_scr[j, pl.ds(s, nchunk, stride=SSM_CHUNK), :] = yc_ref[j, :, s * LANES:(s + 1) * LANES].astype(F32)
    tm = x_ref.shape[0]
    sub = tm // POST_SPLIT
    for part in range(POST_SPLIT):
        rows = slice(part * sub, (part + 1) * sub)
        yg = jax.nn.gelu(jnp.concatenate([y_scr[j, rows, :] for j in range(SSM_COLS)], axis=1))
        z = yg * jax.nn.sigmoid(jnp.dot(yg.astype(BF16), wglu_ref[...], preferred_element_type=F32))
        zn = _rms(z, gs_ref[...]).astype(BF16)
        o = (jnp.dot(at_ref[rows, :], wo_ref[:ATTN_WIDTH, :], preferred_element_type=F32)
             + jnp.dot(zn, wo_ref[ATTN_WIDTH:, :], preferred_element_type=F32))
        x1 = x_ref[rows, :] + g1_ref[...] * o
        x1_ref[rows, :] = x1
        h2 = _rms(x1, n2_ref[...] * (1.0 + sc_ref[...])) + sh_ref[...]
        h2_ref[rows, :] = h2
        logits = _router_logits(wrt_ref[...], h2)
        cw, bucket = _route(logits, rb_ref[...])
        cw_ref[:, rows] = cw
        gid_ref[:, rows] = bucket


def _post(x, attn, yc, w_glu, ssm_g, w_out, g1, n2g, sc2, sh2, w_router_t, router_bias, seq):
    t, d = x.shape
    tm = min(TOKEN_TILE, seq)
    per_b = seq // tm
    tok = lambda w: pl.BlockSpec((tm, w), lambda i: (i, 0))
    const = lambda a: pl.BlockSpec(a.shape, lambda i: (0,) * a.ndim)
    per_batch = pl.BlockSpec((None, 1, d), lambda i: (i // per_b, 0, 0))
    col = lambda r: pl.BlockSpec((r, tm), lambda i: (0, i))
    chunked = pl.BlockSpec((SSM_COLS, tm // SSM_CHUNK, CHUNK_LANES), lambda i: (0, i, 0))
    return pl.pallas_call(
        _post_kernel,
        grid=(t // tm,),
        in_specs=[tok(d), tok(ATTN_WIDTH), chunked, const(w_glu), const(ssm_g),
                  const(w_out), per_batch, const(n2g), per_batch, per_batch, const(w_router_t), const(router_bias)],
        out_specs=[tok(d), tok(d), col(2), col(1)],
        out_shape=[jax.ShapeDtypeStruct((t, d), F32), jax.ShapeDtypeStruct((t, d), F32),
                   jax.ShapeDtypeStruct((2, t), F32), jax.ShapeDtypeStruct((1, t), jnp.int32)],
        scratch_shapes=[pltpu.VMEM((SSM_COLS, tm, LANES), F32)],
        compiler_params=_params(("parallel",), VMEM_LIMIT),
        name="post_mix",
    )(x, attn, yc, w_glu, ssm_g, w_out, g1, n2g, sc2, sh2, w_router_t, router_bias)


def _moe_kernel(kind_ref, rb_ref, bk_ref, pa_ref, pb_ref, first_ref, cast_ref, cpos_ref, pe_ref, offs_ref,
                x_ref, cw_ref, wg_ref, wu_ref, wd_ref, o_ref, wg_s, wu_s, wd_s):
    s = pl.program_id(0)

    @pl.when(cast_ref[s] == 1)
    def _():
        slot = cpos_ref[s]
        wg_s[slot] = wg_ref[...].astype(BF16)
        wu_s[slot] = wu_ref[...].astype(BF16)
        wd_s[slot] = wd_ref[...].astype(BF16)

    @pl.when(kind_ref[s] == STEP_ITEM)
    def _():
        bucket = bk_ref[s]
        base = rb_ref[s] * MOE_ROWS
        lo_row, hi_row = offs_ref[bucket] - base, offs_ref[bucket + 1] - base
        half = MOE_ROWS // 2
        slots = (pa_ref[s], pb_ref[s])

        def run(r0, r1):
            rows = r0 + lax.broadcasted_iota(jnp.int32, (r1 - r0, 1), 0)
            cw = jnp.where((rows >= lo_row) & (rows < hi_row), cw_ref[r0:r1, :], 0.0)
            x = x_ref[r0:r1, :].astype(BF16)
            y = None
            for k in range(2):
                gate = jnp.dot(x, wg_s[slots[k]], preferred_element_type=F32)
                up = jnp.dot(x, wu_s[slots[k]], preferred_element_type=F32)
                act = (gate * jax.nn.sigmoid(gate) * up * cw[:, k:k + 1]).astype(BF16)
                yk = jnp.dot(act, wd_s[slots[k]], preferred_element_type=F32)
                y = yk if y is None else y + yk

            @pl.when(first_ref[s] == 1)
            def _():
                o_ref[r0:r1, :] = y
                for z0, z1 in ((0, r0), (r1, MOE_ROWS)):
                    if z1 > z0:
                        o_ref[z0:z1, :] = jnp.zeros((z1 - z0, o_ref.shape[1]), F32)

            @pl.when(first_ref[s] == 0)
            def _():
                o_ref[r0:r1, :] += y

        needs_lower, needs_upper = lo_row < half, hi_row > half
        pl.when(needs_lower & needs_upper)(lambda: run(0, MOE_ROWS))
        pl.when(needs_lower & jnp.logical_not(needs_upper))(lambda: run(0, half))
        pl.when(jnp.logical_not(needs_lower) & needs_upper)(lambda: run(half, MOE_ROWS))


def _moe_plan_kernel(offs_ref, kind_ref, rb_ref, bk_ref, pa_ref, pb_ref, first_ref, cast_ref, cpos_ref, pe_ref,
                     irb, ibk, *, n_steps_max):
    i32 = jnp.int32
    ng, epg, ppg = N_EXPERT_GROUPS, EXPERTS_PER_GROUP, PAIRS_PER_GROUP

    shift = MOE_ROWS.bit_length() - 1

    def bucket_body(bk, cnt):
        a, b = offs_ref[bk], offs_ref[bk + 1]
        first_blk = lax.shift_right_logical(a, shift)
        n_blk = jnp.where(b > a, lax.shift_right_logical(b - 1, shift) - first_blk + 1, 0)

        def block_body(j, cnt):
            irb[cnt] = first_blk + j
            ibk[cnt] = bk
            return cnt + 1

        return lax.fori_loop(0, n_blk, block_body, cnt)

    n_items = lax.fori_loop(0, ng * ppg, bucket_body, i32(0))

    def count_body(i, m):
        g = ibk[i] // ppg
        return tuple(m[k] + (g == k).astype(i32) for k in range(ng))

    m = lax.fori_loop(0, n_items, count_body, (i32(0),) * ng)

    def next_group(g):
        nxt = i32(-1)
        for k in range(ng - 1, 0, -1):
            nxt = jnp.where((k > g) & (m[k] > 0), k, nxt)
        return nxt

    def emit(s, kind, rb, bk, pa, pb, first, cast, cpos, pe):
        kind_ref[s], rb_ref[s], bk_ref[s], pa_ref[s], pb_ref[s] = kind, rb, bk, pa, pb
        first_ref[s], cast_ref[s], cpos_ref[s], pe_ref[s] = first, cast, cpos, pe

    def item_body(i, carry):
        s, gcur, parity, q, last_pe, last_rb = carry
        rb, bk = irb[i], ibk[i]
        g, pos = bk // ppg, bk % ppg
        new = g != gcur
        started = gcur >= 0
        loaders = jnp.where(new, jnp.where(started, jnp.maximum(epg - q, 0), epg), 0)
        parity = jnp.where(new & started, 1 - parity, parity)
        q = jnp.where(new, 0, q)
        for j in range(epg):
            on = j >= epg - loaders
            emit(s, STEP_LOAD, rb, bk, 0, 0, 0, 1, parity * epg + j, epg * g + j)
            last_pe = jnp.where(on, epg * g + j, last_pe)
            s = s + on.astype(i32)
        nxt = next_group(g)
        pre = (q < epg) & (nxt >= 0)
        pe = jnp.where(pre, epg * nxt + q, last_pe)
        slot_a, slot_b = i32(PAIR_SLOTS[0][0]), i32(PAIR_SLOTS[0][1])
        for p in range(1, ppg):
            slot_a = jnp.where(pos == p, PAIR_SLOTS[p][0], slot_a)
            slot_b = jnp.where(pos == p, PAIR_SLOTS[p][1], slot_b)
        emit(s, STEP_ITEM, rb, bk, parity * epg + slot_a, parity * epg + slot_b, (rb != last_rb).astype(i32),
             pre.astype(i32), (1 - parity) * epg + q, pe)
        return s + 1, g, parity, q + 1, pe, rb

    s, _, _, _, last_pe, last_rb = lax.fori_loop(
        0, n_items, item_body, (i32(0), i32(-1), i32(0), i32(0), i32(0), i32(-1)))
    last_bk = ibk[jnp.maximum(n_items - 1, 0)]

    def pad_body(s, _):
        emit(s, STEP_PAD, last_rb, last_bk, 0, 0, 0, 0, 0, last_pe)
        return 0

    lax.fori_loop(s, n_steps_max, pad_body, 0)


def _moe_steps(bucket, t):
    i32 = jnp.int32
    nbk = N_EXPERT_GROUPS * PAIRS_PER_GROUP
    order = jnp.argsort(bucket, stable=True).astype(i32)
    counts = jnp.sum((bucket[None, :] == jnp.arange(nbk, dtype=i32)[:, None]).astype(i32), axis=1)
    offs = jnp.concatenate([jnp.zeros((1,), i32), jnp.cumsum(counts).astype(i32)])
    assert MOE_ROWS & (MOE_ROWS - 1) == 0
    n_items_max = t // MOE_ROWS + nbk - 1
    n_steps_max = n_items_max + N_EXPERTS
    smem = pl.BlockSpec(memory_space=pltpu.SMEM)
    tables = pl.pallas_call(
        functools.partial(_moe_plan_kernel, n_steps_max=n_steps_max),
        in_specs=[smem],
        out_specs=[smem] * 9,
        out_shape=[jax.ShapeDtypeStruct((n_steps_max,), i32)] * 9,
        scratch_shapes=[pltpu.SMEM((n_items_max + 1,), i32)] * 2,
        name="moe_plan",
    )(offs)
    return order, (*tables, offs), n_steps_max


def _moe(xs, cws, w_gate, w_up, w_down, layer, tables, n_steps_max):
    t, d = xs.shape
    ff = w_gate.shape[3]
    w_map = lambda s, kind, rb, bk, pa, pb, fi, ca, cp, pe, of: (layer, pe[s], 0, 0)
    row_map = lambda s, kind, rb, *_: (rb[s], 0)
    nres = 2 * EXPERTS_PER_GROUP
    grid_spec = pltpu.PrefetchScalarGridSpec(
        num_scalar_prefetch=len(tables),
        grid=(n_steps_max,),
        in_specs=[pl.BlockSpec((MOE_ROWS, d), row_map), pl.BlockSpec((MOE_ROWS, 2), row_map),
                  pl.BlockSpec((None, None, d, ff), w_map), pl.BlockSpec((None, None, d, ff), w_map),
                  pl.BlockSpec((None, None, ff, d), w_map)],
        out_specs=pl.BlockSpec((MOE_ROWS, d), row_map),
        scratch_shapes=[pltpu.VMEM((nres, d, ff), BF16), pltpu.VMEM((nres, d, ff), BF16),
                        pltpu.VMEM((nres, ff, d), BF16)],
    )
    return pl.pallas_call(
        _moe_kernel,
        grid_spec=grid_spec,
        out_shape=jax.ShapeDtypeStruct((t, d), F32),
        compiler_params=_params(("arbitrary",), MOE_VMEM_LIMIT),
        name="moe_grouped",
    )(*tables, xs, cws, w_gate, w_up, w_down)


def _take_rows(a, idx):
    return a.at[idx].get(mode="promise_in_bounds", unique_indices=True)


def _final_kernel(x_ref, y_ref, g_ref, o_ref):
    o_ref[...] = x_ref[...] + g_ref[...] * y_ref[...]


def _final(x1, y, g2, seq):
    t, d = x1.shape
    tm = min(TOKEN_TILE, seq)
    per_b = seq // tm
    tok = lambda w: pl.BlockSpec((tm, w), lambda i: (i, 0))
    return pl.pallas_call(
        _final_kernel,
        grid=(t // tm,),
        in_specs=[tok(d), tok(d), pl.BlockSpec((None, 1, d), lambda i: (i // per_b, 0, 0))],
        out_specs=tok(d),
        out_shape=jax.ShapeDtypeStruct((t, d), F32),
        compiler_params=_params(("parallel",)),
        name="final_residual",
    )(x1, y, g2)


def kernel(x, c, positions, ada_w, ada_b, norm1_g, w_in, q_norm_g, k_norm_g, attn_sink, lam_re, lam_im, ssm_b_re, ssm_b_im, ssm_c_re, ssm_c_im, ssm_d, ssm_log_dt, w_glu, attn_out_g, ssm_out_g, w_out, norm2_g, w_router, router_bias, w_exp_gate, w_exp_up, w_exp_down):
    batch, seq, d = x.shape
    depth = ada_w.shape[0]
    t = batch * seq
    assert seq % ATTN_BLOCK == 0 and seq % SSM_CHUNK == 0 and t % MOE_ROWS == 0

    mod = _adaln_mod(c, ada_w, ada_b).reshape(depth, 6, batch, 1, d)
    cos, sin = _rope_tables(positions)
    head_sum, rot = _rope_constants()
    bias = _attn_bias()
    w_router_t = w_router.T
    s5_mats = _s5_prep(lam_re, lam_im, ssm_b_re, ssm_b_im, ssm_c_re, ssm_c_im, ssm_d, ssm_log_dt)
    router_bias_col = router_bias.reshape(N_EXPERTS, 1)

    xf = x.reshape(t, d)
    res = None
    for l in range(depth):
        sh1, sc1, g1, sh2, sc2, g2 = (mod[l, j] for j in range(6))
        qg = (jnp.tile(q_norm_g[l], N_Q_HEADS) * (HEAD_DIM ** -0.5 * LOG2_E)).reshape(1, ATTN_WIDTH)
        kg = jnp.tile(k_norm_g[l], N_KV_HEADS).reshape(1, KV_WIDTH)
        outs = _inproj(xf, res, sc1, sh1, norm1_g[l].reshape(1, d), w_in[l].astype(BF16), qg, kg, head_sum, rot,
                       cos, sin, seq)
        if res is None:
            q, kx, vx, uc = outs
        else:
            q, kx, vx, uc, xf = outs
        attn = _attention(q, kx, vx, attn_sink[l], attn_out_g[l].reshape(1, ATTN_WIDTH), bias, batch, seq)
        yc = _s5_scan(uc, s5_mats, l, seq // SSM_CHUNK, batch)
        x1, h2, cw, gid = _post(xf, attn, yc, w_glu[l].astype(BF16), ssm_out_g[l].reshape(1, SSM_WIDTH),
                                 w_out[l].astype(BF16), g1, norm2_g[l].reshape(1, d), sc2, sh2, w_router_t,
                                 router_bias_col, seq)
        order, tables, n_steps_max = _moe_steps(gid.reshape(t), t)
        y_sorted = _moe(_take_rows(h2, order), _take_rows(cw.T, order), w_exp_gate, w_exp_up, w_exp_down, l,
                        tables, n_steps_max)
        y = _take_rows(y_sorted, jnp.argsort(order).astype(jnp.int32))
        xf, res = x1, (y, g2)
    y, g2 = res
    return _final(xf, y, g2, seq).reshape(batch, seq, d)
```

```python
import functools
import math

import numpy as np
import jax
import jax.numpy as jnp
from jax import lax
from jax.experimental import pallas as pl
from jax.experimental.pallas import tpu as pltpu

F32 = jnp.float32
BF16 = jnp.bfloat16

HEAD_DIM = 64
N_Q_HEADS = 8
N_KV_HEADS = 2
Q_PER_KV = N_Q_HEADS // N_KV_HEADS
ATTN_WIDTH = N_Q_HEADS * HEAD_DIM
KV_WIDTH = N_KV_HEADS * HEAD_DIM
ATTN_BLOCK = 128
ATTN_Q_TILE = 2048
ROPE_THETA = 10000.0
ROPE_SLAB = 256
LANES = 128
SSM_GROUP_CH = 16
SSM_GROUPS = 32
SSM_WIDTH = SSM_GROUPS * SSM_GROUP_CH
SSM_STATE = 64
SSM_CHUNK = 16
SSM_COLS = SSM_WIDTH // LANES
COL_GROUPS = LANES // SSM_GROUP_CH
COL_STATE = COL_GROUPS * SSM_STATE
CHUNK_LANES = SSM_CHUNK * LANES
SSM_NSPLIT = 4
S5_ROW_PAD = 8
N_EXPERTS = 16
N_EXPERT_GROUPS = 4
EXPERTS_PER_GROUP = N_EXPERTS // N_EXPERT_GROUPS
PAIRS_PER_GROUP = EXPERTS_PER_GROUP * (EXPERTS_PER_GROUP - 1) // 2
PAIR_SLOTS = ((0, 1), (0, 2), (0, 3), (1, 3), (1, 2), (3, 2))
EPS = 1e-6
LOG2_E = math.log2(math.e)
MASK_BIAS = -1e30

TOKEN_TILE = 1024
POST_SPLIT = 1
MOE_ROWS = 256
VMEM_LIMIT = 48 * 1024 * 1024
MOE_VMEM_LIMIT = 56 * 1024 * 1024
STEP_PAD, STEP_LOAD, STEP_ITEM = 0, 1, 2


def _params(sem, vmem=None):
    return pltpu.CompilerParams(dimension_semantics=sem, vmem_limit_bytes=vmem)


def _rms(x, g):
    return x * lax.rsqrt(jnp.mean(x * x, axis=-1, keepdims=True) + EPS) * g


def _mod_kernel(c_ref, w_ref, b_ref, o_ref):
    c = c_ref[...]
    s = c * jax.nn.sigmoid(c)
    o_ref[...] = jnp.dot(s.astype(BF16), w_ref[...].astype(BF16), preferred_element_type=F32) + b_ref[...]


def _adaln_mod(c, ada_w, ada_b):
    depth, d, d6 = ada_w.shape
    nb = c.shape[0]
    n6 = d6 // d
    return pl.pallas_call(
        _mod_kernel,
        grid=(depth, n6),
        in_specs=[pl.BlockSpec((nb, d), lambda l, j: (0, 0)),
                  pl.BlockSpec((None, d, d), lambda l, j: (l, 0, j)),
                  pl.BlockSpec((None, None, 1, d), lambda l, j: (l, j, 0, 0))],
        out_specs=pl.BlockSpec((None, None, nb, d), lambda l, j: (l, j, 0, 0)),
        out_shape=jax.ShapeDtypeStruct((depth, n6, nb, d), F32),
        compiler_params=_params(("arbitrary", "arbitrary"), VMEM_LIMIT),
        name="adaln_mod",
    )(c, ada_w, ada_b.reshape(depth, n6, 1, d))


def _rope_kernel(pos_ref, freq_ref, cos_ref, sin_ref):
    ang = pos_ref[...].astype(F32) * freq_ref[...]
    cos_ref[...] = jnp.cos(ang)
    sin_ref[...] = jnp.sin(ang)


def _rope_tables(positions):
    half = HEAD_DIM // 2
    t = positions.size
    per_row = LANES // half
    rows = t // per_row
    pos_rep = jnp.repeat(positions.reshape(rows, per_row), half, axis=1)
    freq = (ROPE_THETA ** (-np.arange(half, dtype=np.float64) / half)).astype(np.float32)
    freq_row = jnp.asarray(np.tile(freq, per_row)[None, :])
    blk = min(rows, 512)
    cos, sin = pl.pallas_call(
        _rope_kernel,
        grid=(rows // blk,),
        in_specs=[pl.BlockSpec((blk, LANES), lambda i: (i, 0)),
                  pl.BlockSpec((1, LANES), lambda i: (0, 0))],
        out_specs=[pl.BlockSpec((blk, LANES), lambda i: (i, 0))] * 2,
        out_shape=[jax.ShapeDtypeStruct((rows, LANES), F32)] * 2,
        compiler_params=_params(("arbitrary",)),
        name="rope_tables",
    )(pos_rep, freq_row)
    widen = lambda a: jnp.tile(a.reshape(t, half), (1, per_row))
    return widen(cos), widen(sin)


def _rope_constants():
    lane = np.arange(ROPE_SLAB)
    head_sum = (lane[:, None] // HEAD_DIM == lane[None, :] // HEAD_DIM).astype(np.float32)
    half = HEAD_DIM // 2
    rot = np.zeros((ROPE_SLAB, ROPE_SLAB), np.float32)
    for d in range(ROPE_SLAB):
        if d % HEAD_DIM < half:
            rot[d + half, d] = -1.0
        else:
            rot[d - half, d] = 1.0
    return jnp.asarray(head_sum, BF16), jnp.asarray(rot, BF16)


def _inproj_kernel(*refs, has_res):
    if has_res:
        (x_ref, y_ref, g2_ref, sc_ref, sh_ref, n1_ref, w_ref, qg_ref, kg_ref, hs_ref, rot_ref,
         cos_ref, sin_ref, q_ref, k_ref, v_ref, uc_ref, xo_ref, u_scr) = refs
        x = x_ref[...] + g2_ref[...] * y_ref[...]
        xo_ref[...] = x
    else:
        (x_ref, sc_ref, sh_ref, n1_ref, w_ref, qg_ref, kg_ref, hs_ref, rot_ref,
         cos_ref, sin_ref, q_ref, k_ref, v_ref, uc_ref, u_scr) = refs
        x = x_ref[...]
    h = _rms(x, n1_ref[...] * (1.0 + sc_ref[...])) + sh_ref[...]
    proj = jnp.dot(h.astype(BF16), w_ref[...], preferred_element_type=F32)
    q = proj[:, :ATTN_WIDTH]
    k = proj[:, ATTN_WIDTH:ATTN_WIDTH + KV_WIDTH]
    v = proj[:, ATTN_WIDTH + KV_WIDTH:ATTN_WIDTH + 2 * KV_WIDTH]
    cos = cos_ref[...]
    sin = sin_ref[...]
    reps = ATTN_WIDTH // LANES
    cos_q = jnp.concatenate([cos] * reps, axis=1)
    sin_q = jnp.concatenate([sin] * reps, axis=1)

    def head_norm_rope(t, gain, c, s):
        outs = []
        for lo in range(0, t.shape[1], ROPE_SLAB):
            wd = min(ROPE_SLAB, t.shape[1] - lo)
            ts, lanes = t[:, lo:lo + wd], slice(lo, lo + wd)
            ssq = jnp.dot((ts * ts).astype(BF16), hs_ref[:wd, :wd], preferred_element_type=F32)
            tn = (ts * lax.rsqrt(ssq * (1.0 / HEAD_DIM) + EPS) * gain[:, lanes]).astype(BF16)
            tr = jnp.dot(tn, rot_ref[:wd, :wd], preferred_element_type=F32)
            outs.append(tn.astype(F32) * c[:, lanes] + tr * s[:, lanes])
        return outs[0] if len(outs) == 1 else jnp.concatenate(outs, axis=1)

    qo = head_norm_rope(q, qg_ref[...], cos_q, sin_q)
    ko = head_norm_rope(k, kg_ref[...], cos, sin)
    q_ref[...] = qo.astype(BF16)
    k_ref[...] = jnp.concatenate([ko, pltpu.roll(ko, HEAD_DIM, axis=1)], axis=1).astype(BF16)
    v_ref[...] = jnp.concatenate([v, pltpu.roll(v, HEAD_DIM, axis=1)], axis=1).astype(BF16)
    u0 = ATTN_WIDTH + 2 * KV_WIDTH
    nchunk = u_scr.shape[1] // SSM_CHUNK
    for j in range(SSM_COLS):
        u_scr[j] = proj[:, u0 + j * LANES:u0 + (j + 1) * LANES]
    for s in range(SSM_CHUNK):
        for j in range(SSM_COLS):
            lanes = slice(s * SSM_WIDTH + j * LANES, s * SSM_WIDTH + (j + 1) * LANES)
            uc_ref[:, lanes] = u_scr[j, pl.ds(s, nchunk, stride=SSM_CHUNK), :].astype(BF16)


def _inproj(x, res, sc1, sh1, n1g, w_in, qg, kg, head_sum, rot, cos, sin, seq):
    t, d = x.shape
    tm = min(TOKEN_TILE, seq)
    per_b = seq // tm
    in_width = w_in.shape[1]
    tok = lambda w: pl.BlockSpec((tm, w), lambda i: (i, 0))
    const = lambda a: pl.BlockSpec(a.shape, lambda i: (0,) * a.ndim)
    per_batch = pl.BlockSpec((None, 1, d), lambda i: (i // per_b, 0, 0))
    chunked = pl.BlockSpec((tm // SSM_CHUNK, SSM_CHUNK * SSM_WIDTH), lambda i: (i, 0))
    ins, specs = [x], [tok(d)]
    if res is not None:
        y_prev, g2_prev = res
        ins += [y_prev, g2_prev]
        specs += [tok(d), per_batch]
    ins += [sc1, sh1, n1g, w_in, qg, kg, head_sum, rot, cos, sin]
    specs += [per_batch, per_batch, const(n1g), const(w_in), const(qg), const(kg), const(head_sum), const(rot),
              tok(LANES), tok(LANES)]
    out_shape = [jax.ShapeDtypeStruct((t, ATTN_WIDTH), BF16), jax.ShapeDtypeStruct((t, 2 * KV_WIDTH), BF16),
                 jax.ShapeDtypeStruct((t, 2 * KV_WIDTH), BF16),
                 jax.ShapeDtypeStruct((t // SSM_CHUNK, SSM_CHUNK * SSM_WIDTH), BF16)]
    out_specs = [tok(ATTN_WIDTH), tok(2 * KV_WIDTH), tok(2 * KV_WIDTH), chunked]
    if res is not None:
        out_shape.append(jax.ShapeDtypeStruct((t, d), F32))
        out_specs.append(tok(d))
    assert in_width == ATTN_WIDTH + 2 * KV_WIDTH + SSM_WIDTH
    return pl.pallas_call(
        functools.partial(_inproj_kernel, has_res=res is not None),
        grid=(t // tm,),
        in_specs=specs,
        out_specs=out_specs,
        out_shape=out_shape,
        scratch_shapes=[pltpu.VMEM((SSM_COLS, tm, LANES), F32)],
        compiler_params=_params(("parallel",), VMEM_LIMIT),
        name="inproj",
    )(*ins)


def _attn_kernel(sink_ref, q_ref, kc_ref, kp_ref, vc_ref, vp_ref, bias_ref, g_ref, o_ref):
    nsub = q_ref.shape[0] // ATTN_BLOCK
    kk = jnp.concatenate([kp_ref[...], kc_ref[...]], axis=0)
    vv = jnp.concatenate([vp_ref[...], vc_ref[...]], axis=0)
    low = lax.broadcasted_iota(jnp.int32, (kk.shape[0], KV_WIDTH), 1) < HEAD_DIM
    zero = jnp.zeros((kk.shape[0], KV_WIDTH), BF16)

    def variants(a):
        nat, swp = a[:, :KV_WIDTH], a[:, KV_WIDTH:]
        return {(0, 0): jnp.where(low, nat, zero), (0, 1): jnp.where(low, zero, swp),
                (1, 0): jnp.where(low, swp, zero), (1, 1): jnp.where(low, zero, nat)}

    kvar, vvar = variants(kk), variants(vv)
    band = bias_ref[1]
    first = bias_ref[jnp.minimum(pl.program_id(1), 1)]
    upper = lax.broadcasted_iota(jnp.int32, (2 * ATTN_BLOCK, 1), 0) < ATTN_BLOCK
    for j in range(nsub):
        bias = first if j == 0 else band
        bias2 = jnp.concatenate([bias, bias], axis=0)
        keys = slice(j * ATTN_BLOCK, (j + 2) * ATTN_BLOCK)
        qrows = slice(j * ATTN_BLOCK, (j + 1) * ATTN_BLOCK)
        tiles = [None] * (N_Q_HEADS // 2)
        for kv in range(N_KV_HEADS):
            for half in range(2):
                pairs = (2 * kv, 2 * kv + 1)
                heads = (2 * pairs[0] + half, 2 * pairs[1] + half)
                qs = jnp.concatenate([q_ref[qrows, p * LANES:(p + 1) * LANES] for p in pairs], axis=0)
                s = lax.dot_general(qs, kvar[(kv, half)][keys], (((1,), (1,)), ((), ())),
                                    preferred_element_type=F32) + bias2
                sink = jnp.where(upper, sink_ref[heads[0]], sink_ref[heads[1]]) * LOG2_E
                m = jnp.maximum(jnp.max(s, axis=-1, keepdims=True), sink)
                p = jnp.exp2(s - m)
                denom = jnp.sum(p, axis=-1, keepdims=True) + jnp.exp2(sink - m)
                o = jnp.dot(p.astype(BF16), vvar[(kv, half)][keys], preferred_element_type=F32) * (1.0 / denom)
                for r, pr in enumerate(pairs):
                    part = o[r * ATTN_BLOCK:(r + 1) * ATTN_BLOCK]
                    tiles[pr] = part if tiles[pr] is None else tiles[pr] + part
        a = jnp.concatenate(tiles, axis=1)
        o_ref[qrows, :] = _rms(a, g_ref[...]).astype(BF16)


def _attn_bias():
    qi = np.arange(ATTN_BLOCK)[:, None]
    sj = np.arange(2 * ATTN_BLOCK)[None, :]
    diff = qi + ATTN_BLOCK - sj
    band = (diff >= 0) & (diff < ATTN_BLOCK)
    first = band & (sj >= ATTN_BLOCK)
    return jnp.asarray(np.where(np.stack([first, band]), 0.0, MASK_BIAS).astype(np.float32))


def _attention(q, kx, vx, sink, out_g, bias, batch, seq):
    t = q.shape[0]
    qb = min(ATTN_Q_TILE, seq)
    nsub = qb // ATTN_BLOCK
    nq = seq // qb
    nb = seq // ATTN_BLOCK
    cur = lambda w: pl.BlockSpec((qb, w), lambda b, n, s: (b * nq + n, 0))
    prev = lambda w: pl.BlockSpec((ATTN_BLOCK, w), lambda b, n, s: (b * nb + jnp.maximum(n * nsub - 1, 0), 0))
    grid_spec = pltpu.PrefetchScalarGridSpec(
        num_scalar_prefetch=1,
        grid=(batch, nq),
        in_specs=[cur(ATTN_WIDTH), cur(2 * KV_WIDTH), prev(2 * KV_WIDTH), cur(2 * KV_WIDTH), prev(2 * KV_WIDTH),
                  pl.BlockSpec(bias.shape, lambda b, n, s: (0, 0, 0)),
                  pl.BlockSpec((1, ATTN_WIDTH), lambda b, n, s: (0, 0))],
        out_specs=cur(ATTN_WIDTH),
    )
    return pl.pallas_call(
        _attn_kernel,
        grid_spec=grid_spec,
        out_shape=jax.ShapeDtypeStruct((t, ATTN_WIDTH), BF16),
        compiler_params=_params(("parallel", "arbitrary")),
        name="swa_attention",
    )(sink, q, kx, kx, vx, vx, bias, out_g)


def _spread(x, expander3):
    hi = x.astype(BF16)
    r1 = x - hi.astype(F32)
    mid = r1.astype(BF16)
    lo = (r1 - mid.astype(F32)).astype(BF16)
    return jnp.dot(jnp.concatenate([hi, mid, lo], axis=1), expander3, preferred_element_type=F32)


def _s5_prep_kernel(lr_re_ref, lr_im_ref, ldt_ref, bt_re_ref, bt_im_ref, ct_re_ref, ct_im_ref,
                    d_ref, lcol_re_ref, lcol_im_ref, ldtcol_ref, exp_ref, exph_ref, expt_ref, expw_ref,
                    t_ref, w_ref, v_ref, la_ref, lb_ref):
    hi = lax.Precision.HIGHEST
    nl = SSM_CHUNK
    low = lax.broadcasted_iota(jnp.int32, (1, 2 * SSM_STATE), 1) < SSM_STATE
    row_low = lax.broadcasted_iota(jnp.int32, (2 * SSM_STATE, 1), 0) < SSM_STATE
    jcol = lax.broadcasted_iota(jnp.int32, (nl, 1), 0).astype(F32)
    kt_lane = lax.broadcasted_iota(jnp.int32, (SSM_GROUP_CH, nl * SSM_GROUP_CH), 1)
    kt_row = lax.broadcasted_iota(jnp.int32, (SSM_GROUP_CH, nl * SSM_GROUP_CH), 0)

    w_all, v_all, kt_all = [], [], []
    for gm in range(COL_GROUPS):
        dt = jnp.exp(ldt_ref[gm])
        lam_re, lam_im = lr_re_ref[gm], lr_im_ref[gm]
        a_r, th_r = lam_re * dt, lam_im * dt

        er = jnp.exp(jcol * a_r)
        pw_re, pw_im = er * jnp.cos(jcol * th_r), er * jnp.sin(jcol * th_r)

        nr, ni = pw_re[1:2, :] - 1.0, pw_im[1:2, :]
        den = lam_re * lam_re + lam_im * lam_im
        c_re, c_im = (nr * lam_re + ni * lam_im) / den, (ni * lam_re - nr * lam_im) / den
        bt_re, bt_im = bt_re_ref[gm], bt_im_ref[gm]
        bb_re, bb_im = c_re * bt_re - c_im * bt_im, c_re * bt_im + c_im * bt_re

        w_rows = []
        for s in range(nl):
            j = nl - 1 - s
            pr, pi = pw_re[j:j + 1, :], pw_im[j:j + 1, :]
            w_rows.append(jnp.where(low, pr * bb_re - pi * bb_im, pr * bb_im + pi * bb_re))
        w_all.append(w_rows)

        pw_re_t, pw_im_t = pw_re.T, pw_im.T
        pc, ps = _spread(pw_re_t, exp_ref[...]), _spread(pw_im_t, exp_ref[...])
        ct_re, ct_im = _spread(ct_re_ref[gm], exph_ref[...]), _spread(ct_im_ref[gm], exph_ref[...])
        a_re, a_im = ct_re * pc - ct_im * ps, ct_re * ps + ct_im * pc
        a_cat = jnp.where(row_low, a_re, -a_im)
        l1_re, l1_im = pw_re_t[:, 1:2], pw_im_t[:, 1:2]
        v_re, v_im = a_re * l1_re - a_im * l1_im, a_re * l1_im + a_im * l1_re
        v_all.append(jnp.where(row_low, v_re, -v_im))

        kt = jnp.dot(jnp.where(low, bb_re, bb_im), a_cat, precision=hi, preferred_element_type=F32)
        kt_all.append(kt + jnp.where(kt_lane == kt_row, d_ref[gm], 0.0))

    def expand(stacked, expander, row_group, lane_group):
        wide = jnp.dot(stacked.astype(BF16), expander, preferred_element_type=F32)
        r = lax.broadcasted_iota(jnp.int32, wide.shape, 0)
        c = lax.broadcasted_iota(jnp.int32, wide.shape, 1)
        return jnp.where(row_group(r) == lane_group(c), wide, 0.0).astype(BF16)

    chan_group = lambda i: (i >> 4) & (COL_GROUPS - 1)
    state_group = lambda i: (i >> 6) & (COL_GROUPS - 1)

    bd = expand(jnp.concatenate(kt_all, axis=0), expt_ref[...], chan_group, chan_group)
    t_ref[0:LANES, :] = bd
    for s in range(1, nl):
        t_ref[s * LANES:(s + 1) * LANES, :] = jnp.concatenate(
            [jnp.zeros((LANES, s * LANES), BF16), bd[:, :CHUNK_LANES - s * LANES]], axis=1)

    w_stack = jnp.concatenate([w_all[gm][s] for s in range(nl) for gm in range(COL_GROUPS)], axis=0)
    w_ref[...] = expand(w_stack, expw_ref[...], chan_group, state_group)

    v_stack = jnp.concatenate([v_all[gm][half * SSM_STATE:(half + 1) * SSM_STATE, :]
                               for half in range(2) for gm in range(COL_GROUPS)], axis=0)
    v_ref[...] = expand(v_stack, expt_ref[...], state_group, chan_group)

    dtc = jnp.exp(ldtcol_ref[...])
    e16 = jnp.exp(nl * lcol_re_ref[...] * dtc)
    ang = nl * lcol_im_ref[...] * dtc
    la_ref[...] = e16 * jnp.cos(ang)
    lb_ref[...] = e16 * jnp.sin(ang)


def _s5_prep(lam_re, lam_im, b_re, b_im, c_re, c_im, d_skip, log_dt):
    g, p, h, nl = SSM_GROUPS, SSM_STATE, SSM_GROUP_CH, SSM_CHUNK
    cg = COL_GROUPS
    nc = lam_re.shape[0] * SSM_COLS
    col = lambda a: a.reshape((nc, cg) + a.shape[2:])
    dup_row = lambda a: col(jnp.tile(a, (1, 1, 2))[:, :, None, :])
    bt = lambda a: col(jnp.tile(jnp.swapaxes(a, 2, 3), (1, 1, 1, 2)))
    ct = lambda a: col(jnp.tile(jnp.swapaxes(a, 2, 3), (1, 1, 2, 1)))
    d_pad = col(jnp.pad(d_skip.reshape(-1, g, 1, h), ((0, 0), (0, 0), (0, 0), (0, nl * h - h))))
    wide = lambda a: a.reshape(nc, 1, cg * p)
    expand = jnp.asarray(np.tile(np.repeat(np.eye(nl, dtype=np.float32), h, axis=1), (3, 1)), BF16)
    expand_h = jnp.asarray(np.tile(np.eye(h, dtype=np.float32), (3, nl)), BF16)
    exp_t = np.zeros((nl, h, nl, cg, h), np.float32)
    exp_w = np.zeros((2, p, 2, cg, p), np.float32)
    for gm in range(cg):
        exp_t[:, :, :, gm, :] = np.eye(nl * h, dtype=np.float32).reshape(nl, h, nl, h)
        exp_w[:, :, :, gm, :] = np.eye(2 * p, dtype=np.float32).reshape(2, p, 2, p)
    exp_t = jnp.asarray(exp_t.reshape(nl * h, CHUNK_LANES), BF16)
    exp_w = jnp.asarray(exp_w.reshape(2 * p, 2 * COL_STATE), BF16)
    blk = lambda *s: pl.BlockSpec((None,) + s, lambda i: (i,) + (0,) * len(s))
    const = lambda a: pl.BlockSpec(a.shape, lambda i: (0,) * a.ndim)
    lw = nl * h
    return pl.pallas_call(
        _s5_prep_kernel,
        grid=(nc,),
        in_specs=[blk(cg, 1, 2 * p), blk(cg, 1, 2 * p), blk(cg, 1, 1),
                  blk(cg, h, 2 * p), blk(cg, h, 2 * p), blk(cg, 2 * p, h), blk(cg, 2 * p, h), blk(cg, 1, lw),
                  blk(1, cg * p), blk(1, cg * p), blk(1, cg * p), const(expand), const(expand_h), const(exp_t),
                  const(exp_w)],
        out_specs=[blk(CHUNK_LANES, CHUNK_LANES), blk(CHUNK_LANES, 2 * COL_STATE), blk(2 * COL_STATE, CHUNK_LANES),
                   blk(1, COL_STATE), blk(1, COL_STATE)],
        out_shape=[jax.ShapeDtypeStruct((nc, CHUNK_LANES, CHUNK_LANES), BF16),
                   jax.ShapeDtypeStruct((nc, CHUNK_LANES, 2 * COL_STATE), BF16),
                   jax.ShapeDtypeStruct((nc, 2 * COL_STATE, CHUNK_LANES), BF16),
                   jax.ShapeDtypeStruct((nc, 1, COL_STATE), F32), jax.ShapeDtypeStruct((nc, 1, COL_STATE), F32)],
        compiler_params=_params(("parallel",), VMEM_LIMIT),
        name="s5_prep",
    )(dup_row(lam_re), dup_row(lam_im), col(log_dt[:, :, None, None]),
      bt(b_re), bt(b_im), ct(c_re), ct(c_im), d_pad, wide(lam_re), wide(lam_im),
      wide(jnp.repeat(log_dt, p, axis=1)), expand, expand_h, exp_t, exp_w)


def _s5_kernel(*refs, nchunks, nb):
    uc_refs = refs[:SSM_CHUNK]
    t_ref, w_ref, v_ref, la_ref, lb_ref, o_ref, ucat_ref, s_ref, xp_ref = refs[SSM_CHUNK:]

    @pl.when(pl.program_id(1) == 0)
    def _():
        for s in range(SSM_CHUNK):
            ucat_ref[:, s * LANES:(s + 1) * LANES] = uc_refs[s][...]
        s_in = jnp.dot(ucat_ref[...], w_ref[...], preferred_element_type=F32)
        nblk = COL_STATE // LANES
        pitch = s_ref.shape[1] // nb
        for b in range(2 * nblk):
            for q in range(nb):
                s_ref[b, q * pitch:q * pitch + nchunks, :] = s_in[q * nchunks:(q + 1) * nchunks, b * LANES:(b + 1) * LANES]
        lr = [jnp.broadcast_to(la_ref[:, b * LANES:(b + 1) * LANES], (nb, LANES)) for b in range(nblk)]
        li = [jnp.broadcast_to(lb_ref[:, b * LANES:(b + 1) * LANES], (nb, LANES)) for b in range(nblk)]

        def step(c, carry):
            rows = pl.ds(c, nb, stride=pitch)
            out = []
            for b in range(nblk):
                re, im = carry[2 * b], carry[2 * b + 1]
                xp_ref[b, rows, :] = re
                xp_ref[nblk + b, rows, :] = im
                out.append(lr[b] * re - li[b] * im + s_ref[b, rows, :])
                out.append(lr[b] * im + li[b] * re + s_ref[nblk + b, rows, :])
            return tuple(out)

        zero = jnp.zeros((nb, LANES), F32)
        lax.fori_loop(0, nchunks, step, (zero,) * (2 * nblk), unroll=4)

    pitch = xp_ref.shape[1] // nb
    xp = jnp.concatenate(
        [jnp.concatenate([xp_ref[b, q * pitch:q * pitch + nchunks, :] for q in range(nb)], axis=0)
         for b in range(2 * COL_STATE // LANES)], axis=1).astype(BF16)
    inter = jnp.dot(xp, v_ref[...], preferred_element_type=F32)
    for kk in range(SSM_NSPLIT):
        @pl.when(pl.program_id(1) == kk)
        def _():
            live = (kk + 1) * (CHUNK_LANES // SSM_NSPLIT)
            intra = jnp.dot(ucat_ref[:, :live], t_ref[:live, :], preferred_element_type=F32)
            o_ref[...] = (intra + inter).astype(BF16)


def _s5_scan(uc, mats, layer, nchunks, nb):
    rows = uc.shape[0]
    c0 = layer * SSM_COLS
    split = CHUNK_LANES // SSM_NSPLIT
    u_spec = lambda s: pl.BlockSpec((rows, LANES), lambda j, k: (0, SSM_COLS * s + j))
    return pl.pallas_call(
        functools.partial(_s5_kernel, nchunks=nchunks, nb=nb),
        grid=(SSM_COLS, SSM_NSPLIT),
        in_specs=[u_spec(s) for s in range(SSM_CHUNK)] + [
            pl.BlockSpec((None, CHUNK_LANES, split), lambda j, k: (c0 + j, 0, k)),
            pl.BlockSpec((None, CHUNK_LANES, 2 * COL_STATE), lambda j, k: (c0 + j, 0, 0)),
            pl.BlockSpec((None, 2 * COL_STATE, split), lambda j, k: (c0 + j, 0, k)),
            pl.BlockSpec((None, 1, COL_STATE), lambda j, k: (c0 + j, 0, 0)),
            pl.BlockSpec((None, 1, COL_STATE), lambda j, k: (c0 + j, 0, 0))],
        out_specs=pl.BlockSpec((None, rows, split), lambda j, k: (j, 0, k)),
        out_shape=jax.ShapeDtypeStruct((SSM_COLS, rows, CHUNK_LANES), BF16),
        scratch_shapes=[pltpu.VMEM((rows, CHUNK_LANES), BF16),
                        pltpu.VMEM((2 * COL_STATE // LANES, nb * (nchunks + S5_ROW_PAD), LANES), F32),
                        pltpu.VMEM((2 * COL_STATE // LANES, nb * (nchunks + S5_ROW_PAD), LANES), F32)],
        compiler_params=_params(("parallel", "arbitrary"), VMEM_LIMIT),
        name="s5_scan",
    )(*([uc] * SSM_CHUNK), *mats)


def _route(logits, bias):
    m = jnp.max(logits, axis=0, keepdims=True)
    e = jnp.exp(logits - m)
    probs = e / jnp.sum(e, axis=0, keepdims=True)
    sel = probs + bias
    row = lambda a, i: a[i:i + 1, :]
    best_score, best = None, None
    for grp in range(N_EXPERT_GROUPS):
        a, b, c, d = (row(sel, EXPERTS_PER_GROUP * grp + i) for i in range(EXPERTS_PER_GROUP))
        hab, lab, hcd, lcd = jnp.maximum(a, b), jnp.minimum(a, b), jnp.maximum(c, d), jnp.minimum(c, d)
        top1 = jnp.maximum(hab, hcd)
        top2 = jnp.maximum(jnp.maximum(lab, lcd), jnp.minimum(hab, hcd))
        score = top1 + top2
        if grp == 0:
            best_score, best = score, jnp.zeros(score.shape, jnp.int32)
        else:
            better = score > best_score
            best = jnp.where(better, grp, best)
            best_score = jnp.where(better, score, best_score)

    def pick(a, i):
        out = row(a, i)
        for grp in range(1, N_EXPERT_GROUPS):
            out = jnp.where(best == grp, row(a, EXPERTS_PER_GROUP * grp + i), out)
        return out

    s_in = [pick(sel, i) for i in range(EXPERTS_PER_GROUP)]
    p_in = [pick(probs, i) for i in range(EXPERTS_PER_GROUP)]
    neg = jnp.full(s_in[0].shape, -jnp.inf, F32)

    def argmax_first(vals):
        idx, val = jnp.zeros(vals[0].shape, jnp.int32), vals[0]
        for i in range(1, len(vals)):
            better = vals[i] > val
            idx = jnp.where(better, i, idx)
            val = jnp.where(better, vals[i], val)
        return idx

    i1 = argmax_first(s_in)
    i2 = argmax_first([jnp.where(i1 == i, neg, s_in[i]) for i in range(EXPERTS_PER_GROUP)])
    zero = jnp.zeros(p_in[0].shape, F32)
    g1 = sum(jnp.where(i1 == i, p_in[i], zero) for i in range(EXPERTS_PER_GROUP))
    g2 = sum(jnp.where(i2 == i, p_in[i], zero) for i in range(EXPERTS_PER_GROUP))
    tot = g1 + g2
    w1, w2 = g1 / tot, g2 / tot
    first_low = i1 < i2
    low, high = jnp.minimum(i1, i2), jnp.maximum(i1, i2)
    w_low, w_high = jnp.where(first_low, w1, w2), jnp.where(first_low, w2, w1)
    pos = jnp.where(low == 0, high - 1, jnp.where(low == 1, jnp.where(high == 2, 4, 3), 5))
    swap = low == 2
    bucket = best * PAIRS_PER_GROUP + pos
    return jnp.concatenate([jnp.where(swap, w_high, w_low), jnp.where(swap, w_low, w_high)], axis=0), bucket


def _router_logits(w_t, h):
    w_hi = w_t.astype(BF16)
    w_r = w_t - w_hi.astype(F32)
    w_mid = w_r.astype(BF16)
    w_lo = (w_r - w_mid.astype(F32)).astype(BF16)
    h_hi = h.astype(BF16)
    h_lo = (h - h_hi.astype(F32)).astype(BF16)
    dims = (((1,), (1,)), ((), ()))
    a = lax.dot_general(jnp.concatenate([w_hi, w_mid, w_lo], axis=0), h_hi, dims, preferred_element_type=F32)
    b = lax.dot_general(jnp.concatenate([w_hi, w_mid], axis=0), h_lo, dims, preferred_element_type=F32)
    e = w_t.shape[0]
    return a[:e] + a[e:2 * e] + a[2 * e:] + b[:e] + b[e:]


def _post_kernel(x_ref, at_ref, yc_ref, wglu_ref, gs_ref, wo_ref, g1_ref, n2_ref, sc_ref, sh_ref,
                 wrt_ref, rb_ref, x1_ref, h2_ref, cw_ref, gid_ref, y_scr):
    nchunk = y_scr.shape[1] // SSM_CHUNK
    for s in range(SSM_CHUNK):
        for j in range(SSM_COLS):
            y_scr[j, pl.ds(s, nchunk, stride=SSM_CHUNK), :] = yc_ref[j, :, s * LANES:(s + 1) * LANES].astype(F32)
    tm = x_ref.shape[0]
    sub = tm // POST_SPLIT
    for part in range(POST_SPLIT):
        rows = slice(part * sub, (part + 1) * sub)
        yg = jax.nn.gelu(jnp.concatenate([y_scr[j, rows, :] for j in range(SSM_COLS)], axis=1))
        z = yg * jax.nn.sigmoid(jnp.dot(yg.astype(BF16), wglu_ref[...], preferred_element_type=F32))
        zn = _rms(z, gs_ref[...]).astype(BF16)
        o = (jnp.dot(at_ref[rows, :], wo_ref[:ATTN_WIDTH, :], preferred_element_type=F32)
             + jnp.dot(zn, wo_ref[ATTN_WIDTH:, :], preferred_element_type=F32))
        x1 = x_ref[rows, :] + g1_ref[...] * o
        x1_ref[rows, :] = x1
        h2 = _rms(x1, n2_ref[...] * (1.0 + sc_ref[...])) + sh_ref[...]
        h2_ref[rows, :] = h2
        logits = _router_logits(wrt_ref[...], h2)
        cw, bucket = _route(logits, rb_ref[...])
        cw_ref[:, rows] = cw
        gid_ref[:, rows] = bucket


def _post(x, attn, yc, w_glu, ssm_g, w_out, g1, n2g, sc2, sh2, w_router_t, router_bias, seq):
    t, d = x.shape
    tm = min(TOKEN_TILE, seq)
    per_b = seq // tm
    tok = lambda w: pl.BlockSpec((tm, w), lambda i: (i, 0))
    const = lambda a: pl.BlockSpec(a.shape, lambda i: (0,) * a.ndim)
    per_batch = pl.BlockSpec((None, 1, d), lambda i: (i // per_b, 0, 0))
    col = lambda r: pl.BlockSpec((r, tm), lambda i: (0, i))
    chunked = pl.BlockSpec((SSM_COLS, tm // SSM_CHUNK, CHUNK_LANES), lambda i: (0, i, 0))
    return pl.pallas_call(
        _post_kernel,
        grid=(t // tm,),
        in_specs=[tok(d), tok(ATTN_WIDTH), chunked, const(w_glu), const(ssm_g),
                  const(w_out), per_batch, const(n2g), per_batch, per_batch, const(w_router_t), const(router_bias)],
        out_specs=[tok(d), tok(d), col(2), col(1)],
        out_shape=[jax.ShapeDtypeStruct((t, d), F32), jax.ShapeDtypeStruct((t, d), F32),
                   jax.ShapeDtypeStruct((2, t), F32), jax.ShapeDtypeStruct((1, t), jnp.int32)],
        scratch_shapes=[pltpu.VMEM((SSM_COLS, tm, LANES), F32)],
        compiler_params=_params(("parallel",), VMEM_LIMIT),
        name="post_mix",
    )(x, attn, yc, w_glu, ssm_g, w_out, g1, n2g, sc2, sh2, w_router_t, router_bias)


def _moe_kernel(kind_ref, rb_ref, bk_ref, pa_ref, pb_ref, first_ref, cast_ref, cpos_ref, pe_ref, offs_ref,
                x_ref, cw_ref, wg_ref, wu_ref, wd_ref, o_ref, wg_s, wu_s, wd_s):
    s = pl.program_id(0)

    @pl.when(cast_ref[s] == 1)
    def _():
        slot = cpos_ref[s]
        wg_s[slot] = wg_ref[...].astype(BF16)
        wu_s[slot] = wu_ref[...].astype(BF16)
        wd_s[slot] = wd_ref[...].astype(BF16)

    @pl.when(kind_ref[s] == STEP_ITEM)
    def _():
        bucket = bk_ref[s]
        base = rb_ref[s] * MOE_ROWS
        lo_row, hi_row = offs_ref[bucket] - base, offs_ref[bucket + 1] - base
        half = MOE_ROWS // 2
        slots = (pa_ref[s], pb_ref[s])

        def run(r0, r1):
            rows = r0 + lax.broadcasted_iota(jnp.int32, (r1 - r0, 1), 0)
            cw = jnp.where((rows >= lo_row) & (rows < hi_row), cw_ref[r0:r1, :], 0.0)
            x = x_ref[r0:r1, :].astype(BF16)
            y = None
            for k in range(2):
                gate = jnp.dot(x, wg_s[slots[k]], preferred_element_type=F32)
                up = jnp.dot(x, wu_s[slots[k]], preferred_element_type=F32)
                act = (gate * jax.nn.sigmoid(gate) * up * cw[:, k:k + 1]).astype(BF16)
                yk = jnp.dot(act, wd_s[slots[k]], preferred_element_type=F32)
                y = yk if y is None else y + yk

            @pl.when(first_ref[s] == 1)
            def _():
                o_ref[r0:r1, :] = y
                for z0, z1 in ((0, r0), (r1, MOE_ROWS)):
                    if z1 > z0:
                        o_ref[z0:z1, :] = jnp.zeros((z1 - z0, o_ref.shape[1]), F32)

            @pl.when(first_ref[s] == 0)
            def _():
                o_ref[r0:r1, :] += y

        needs_lower, needs_upper = lo_row < half, hi_row > half
        pl.when(needs_lower & needs_upper)(lambda: run(0, MOE_ROWS))
        pl.when(needs_lower & jnp.logical_not(needs_upper))(lambda: run(0, half))
        pl.when(jnp.logical_not(needs_lower) & needs_upper)(lambda: run(half, MOE_ROWS))


def _moe_plan_kernel(offs_ref, kind_ref, rb_ref, bk_ref, pa_ref, pb_ref, first_ref, cast_ref, cpos_ref, pe_ref,
                     irb, ibk, *, n_steps_max):
    i32 = jnp.int32
    ng, epg, ppg = N_EXPERT_GROUPS, EXPERTS_PER_GROUP, PAIRS_PER_GROUP

    shift = MOE_ROWS.bit_length() - 1

    def bucket_body(bk, cnt):
        a, b = offs_ref[bk], offs_ref[bk + 1]
        first_blk = lax.shift_right_logical(a, shift)
        n_blk = jnp.where(b > a, lax.shift_right_logical(b - 1, shift) - first_blk + 1, 0)

        def block_body(j, cnt):
            irb[cnt] = first_blk + j
            ibk[cnt] = bk
            return cnt + 1

        return lax.fori_loop(0, n_blk, block_body, cnt)

    n_items = lax.fori_loop(0, ng * ppg, bucket_body, i32(0))

    def count_body(i, m):
        g = ibk[i] // ppg
        return tuple(m[k] + (g == k).astype(i32) for k in range(ng))

    m = lax.fori_loop(0, n_items, count_body, (i32(0),) * ng)

    def next_group(g):
        nxt = i32(-1)
        for k in range(ng - 1, 0, -1):
            nxt = jnp.where((k > g) & (m[k] > 0), k, nxt)
        return nxt

    def emit(s, kind, rb, bk, pa, pb, first, cast, cpos, pe):
        kind_ref[s], rb_ref[s], bk_ref[s], pa_ref[s], pb_ref[s] = kind, rb, bk, pa, pb
        first_ref[s], cast_ref[s], cpos_ref[s], pe_ref[s] = first, cast, cpos, pe

    def item_body(i, carry):
        s, gcur, parity, q, last_pe, last_rb = carry
        rb, bk = irb[i], ibk[i]
        g, pos = bk // ppg, bk % ppg
        new = g != gcur
        started = gcur >= 0
        loaders = jnp.where(new, jnp.where(started, jnp.maximum(epg - q, 0), epg), 0)
        parity = jnp.where(new & started, 1 - parity, parity)
        q = jnp.where(new, 0, q)
        for j in range(epg):
            on = j >= epg - loaders
            emit(s, STEP_LOAD, rb, bk, 0, 0, 0, 1, parity * epg + j, epg * g + j)
            last_pe = jnp.where(on, epg * g + j, last_pe)
            s = s + on.astype(i32)
        nxt = next_group(g)
        pre = (q < epg) & (nxt >= 0)
        pe = jnp.where(pre, epg * nxt + q, last_pe)
        slot_a, slot_b = i32(PAIR_SLOTS[0][0]), i32(PAIR_SLOTS[0][1])
        for p in range(1, ppg):
            slot_a = jnp.where(pos == p, PAIR_SLOTS[p][0], slot_a)
            slot_b = jnp.where(pos == p, PAIR_SLOTS[p][1], slot_b)
        emit(s, STEP_ITEM, rb, bk, parity * epg + slot_a, parity * epg + slot_b, (rb != last_rb).astype(i32),
             pre.astype(i32), (1 - parity) * epg + q, pe)
        return s + 1, g, parity, q + 1, pe, rb

    s, _, _, _, last_pe, last_rb = lax.fori_loop(
        0, n_items, item_body, (i32(0), i32(-1), i32(0), i32(0), i32(0), i32(-1)))
    last_bk = ibk[jnp.maximum(n_items - 1, 0)]

    def pad_body(s, _):
        emit(s, STEP_PAD, last_rb, last_bk, 0, 0, 0, 0, 0, last_pe)
        return 0

    lax.fori_loop(s, n_steps_max, pad_body, 0)


def _moe_steps(bucket, t):
    i32 = jnp.int32
    nbk = N_EXPERT_GROUPS * PAIRS_PER_GROUP
    order = jnp.argsort(bucket, stable=True).astype(i32)
    counts = jnp.sum((bucket[None, :] == jnp.arange(nbk, dtype=i32)[:, None]).astype(i32), axis=1)
    offs = jnp.concatenate([jnp.zeros((1,), i32), jnp.cumsum(counts).astype(i32)])
    assert MOE_ROWS & (MOE_ROWS - 1) == 0
    n_items_max = t // MOE_ROWS + nbk - 1
    n_steps_max = n_items_max + N_EXPERTS
    smem = pl.BlockSpec(memory_space=pltpu.SMEM)
    tables = pl.pallas_call(
        functools.partial(_moe_plan_kernel, n_steps_max=n_steps_max),
        in_specs=[smem],
        out_specs=[smem] * 9,
        out_shape=[jax.ShapeDtypeStruct((n_steps_max,), i32)] * 9,
        scratch_shapes=[pltpu.SMEM((n_items_max + 1,), i32)] * 2,
        name="moe_plan",
    )(offs)
    return order, (*tables, offs), n_steps_max


def _moe(xs, cws, w_gate, w_up, w_down, layer, tables, n_steps_max):
    t, d = xs.shape
    ff = w_gate.shape[3]
    w_map = lambda s, kind, rb, bk, pa, pb, fi, ca, cp, pe, of: (layer, pe[s], 0, 0)
    row_map = lambda s, kind, rb, *_: (rb[s], 0)
    nres = 2 * EXPERTS_PER_GROUP
    grid_spec = pltpu.PrefetchScalarGridSpec(
        num_scalar_prefetch=len(tables),
        grid=(n_steps_max,),
        in_specs=[pl.BlockSpec((MOE_ROWS, d), row_map), pl.BlockSpec((MOE_ROWS, 2), row_map),
                  pl.BlockSpec((None, None, d, ff), w_map), pl.BlockSpec((None, None, d, ff), w_map),
                  pl.BlockSpec((None, None, ff, d), w_map)],
        out_specs=pl.BlockSpec((MOE_ROWS, d), row_map),
        scratch_shapes=[pltpu.VMEM((nres, d, ff), BF16), pltpu.VMEM((nres, d, ff), BF16),
                        pltpu.VMEM((nres, ff, d), BF16)],
    )
    return pl.pallas_call(
        _moe_kernel,
        grid_spec=grid_spec,
        out_shape=jax.ShapeDtypeStruct((t, d), F32),
        compiler_params=_params(("arbitrary",), MOE_VMEM_LIMIT),
        name="moe_grouped",
    )(*tables, xs, cws, w_gate, w_up, w_down)


def _take_rows(a, idx):
    return a.at[idx].get(mode="promise_in_bounds", unique_indices=True)


def _final_kernel(x_ref, y_ref, g_ref, o_ref):
    o_ref[...] = x_ref[...] + g_ref[...] * y_ref[...]


def _final(x1, y, g2, seq):
    t, d = x1.shape
    tm = min(TOKEN_TILE, seq)
    per_b = seq // tm
    tok = lambda w: pl.BlockSpec((tm, w), lambda i: (i, 0))
    return pl.pallas_call(
        _final_kernel,
        grid=(t // tm,),
        in_specs=[tok(d), tok(d), pl.BlockSpec((None, 1, d), lambda i: (i // per_b, 0, 0))],
        out_specs=tok(d),
        out_shape=jax.ShapeDtypeStruct((t, d), F32),
        compiler_params=_params(("parallel",)),
        name="final_residual",
    )(x1, y, g2)


def kernel(x, c, positions, ada_w, ada_b, norm1_g, w_in, q_norm_g, k_norm_g, attn_sink, lam_re, lam_im, ssm_b_re, ssm_b_im, ssm_c_re, ssm_c_im, ssm_d, ssm_log_dt, w_glu, attn_out_g, ssm_out_g, w_out, norm2_g, w_router, router_bias, w_exp_gate, w_exp_up, w_exp_down):
    batch, seq, d = x.shape
    depth = ada_w.shape[0]
    t = batch * seq
    assert seq % ATTN_BLOCK == 0 and seq % SSM_CHUNK == 0 and t % MOE_ROWS == 0

    mod = _adaln_mod(c, ada_w, ada_b).reshape(depth, 6, batch, 1, d)
    cos, sin = _rope_tables(positions)
    head_sum, rot = _rope_constants()
    bias = _attn_bias()
    w_router_t = w_router.T
    s5_params = (lam_re, lam_im, ssm_b_re, ssm_b_im, ssm_c_re, ssm_c_im, ssm_d, ssm_log_dt)
    s5_prep = lambda l: _s5_prep(*(a[l:l + 1] for a in s5_params))
    s5_mats = s5_prep(0)
    router_bias_col = router_bias.reshape(N_EXPERTS, 1)

    xf = x.reshape(t, d)
    res = None
    for l in range(depth):
        sh1, sc1, g1, sh2, sc2, g2 = (mod[l, j] for j in range(6))
        qg = (jnp.tile(q_norm_g[l], N_Q_HEADS) * (HEAD_DIM ** -0.5 * LOG2_E)).reshape(1, ATTN_WIDTH)
        kg = jnp.tile(k_norm_g[l], N_KV_HEADS).reshape(1, KV_WIDTH)
        outs = _inproj(xf, res, sc1, sh1, norm1_g[l].reshape(1, d), w_in[l].astype(BF16), qg, kg, head_sum, rot,
                       cos, sin, seq)
        if res is None:
            q, kx, vx, uc = outs
        else:
            q, kx, vx, uc, xf = outs
        attn = _attention(q, kx, vx, attn_sink[l], attn_out_g[l].reshape(1, ATTN_WIDTH), bias, batch, seq)
        yc = _s5_scan(uc, s5_mats, 0, seq // SSM_CHUNK, batch)
        x1, h2, cw, gid = _post(xf, attn, yc, w_glu[l].astype(BF16), ssm_out_g[l].reshape(1, SSM_WIDTH),
                                 w_out[l].astype(BF16), g1, norm2_g[l].reshape(1, d), sc2, sh2, w_router_t,
                                 router_bias_col, seq)
        order, tables, n_steps_max = _moe_steps(gid.reshape(t), t)
        y_sorted = _moe(_take_rows(h2, order), _take_rows(cw.T, order), w_exp_gate, w_exp_up, w_exp_down, l,
                        tables, n_steps_max)
        y = _take_rows(y_sorted, jnp.argsort(order).astype(jnp.int32))
        if l + 1 < depth:
            s5_mats = s5_prep(l + 1)
        xf, res = x1, (y, g2)
    y, g2 = res
    return _final(xf, y, g2, seq).reshape(batch, seq, d)
```

```python
import functools
import math

import numpy as np
import jax
import jax.numpy as jnp
from jax import lax
from jax.experimental import pallas as pl
from jax.experimental.pallas import tpu as pltpu

F32 = jnp.float32
BF16 = jnp.bfloat16

HEAD_DIM = 64
N_Q_HEADS = 8
N_KV_HEADS = 2
Q_PER_KV = N_Q_HEADS // N_KV_HEADS
ATTN_WIDTH = N_Q_HEADS * HEAD_DIM
KV_WIDTH = N_KV_HEADS * HEAD_DIM
ATTN_BLOCK = 128
ATTN_Q_TILE = 2048
ROPE_THETA = 10000.0
ROPE_SLAB = 256
LANES = 128
SSM_GROUP_CH = 16
SSM_GROUPS = 32
SSM_WIDTH = SSM_GROUPS * SSM_GROUP_CH
SSM_STATE = 64
SSM_CHUNK = 16
SSM_COLS = SSM_WIDTH // LANES
COL_GROUPS = LANES // SSM_GROUP_CH
COL_STATE = COL_GROUPS * SSM_STATE
CHUNK_LANES = SSM_CHUNK * LANES
SSM_NSPLIT = 4
S5_ROW_PAD = 8
N_EXPERTS = 16
N_EXPERT_GROUPS = 4
EXPERTS_PER_GROUP = N_EXPERTS // N_EXPERT_GROUPS
PAIRS_PER_GROUP = EXPERTS_PER_GROUP * (EXPERTS_PER_GROUP - 1) // 2
PAIR_SLOTS = ((0, 1), (0, 2), (0, 3), (1, 3), (1, 2), (3, 2))
EPS = 1e-6
LOG2_E = math.log2(math.e)
MASK_BIAS = -1e30

TOKEN_TILE = 1024
MOE_ROWS = 512
MOE_SUB = 256
VMEM_LIMIT = 48 * 1024 * 1024
MOE_VMEM_LIMIT = 56 * 1024 * 1024
STEP_PAD, STEP_LOAD, STEP_ITEM = 0, 1, 2


def _params(sem, vmem=None):
    return pltpu.CompilerParams(dimension_semantics=sem, vmem_limit_bytes=vmem)


def _rms(x, g):
    return x * lax.rsqrt(jnp.mean(x * x, axis=-1, keepdims=True) + EPS) * g


def _mod_kernel(c_ref, w_ref, b_ref, o_ref):
    c = c_ref[...]
    s = c * jax.nn.sigmoid(c)
    o_ref[...] = jnp.dot(s.astype(BF16), w_ref[...].astype(BF16), preferred_element_type=F32) + b_ref[...]


def _adaln_mod(c, ada_w, ada_b):
    depth, d, d6 = ada_w.shape
    nb = c.shape[0]
    n6 = d6 // d
    return pl.pallas_call(
        _mod_kernel,
        grid=(depth, n6),
        in_specs=[pl.BlockSpec((nb, d), lambda l, j: (0, 0)),
                  pl.BlockSpec((None, d, d), lambda l, j: (l, 0, j)),
                  pl.BlockSpec((None, None, 1, d), lambda l, j: (l, j, 0, 0))],
        out_specs=pl.BlockSpec((None, None, nb, d), lambda l, j: (l, j, 0, 0)),
        out_shape=jax.ShapeDtypeStruct((depth, n6, nb, d), F32),
        compiler_params=_params(("arbitrary", "arbitrary"), VMEM_LIMIT),
        name="adaln_mod",
    )(c, ada_w, ada_b.reshape(depth, n6, 1, d))


def _rope_kernel(pos_ref, freq_ref, cos_ref, sin_ref):
    ang = pos_ref[...].astype(F32) * freq_ref[...]
    cos_ref[...] = jnp.cos(ang)
    sin_ref[...] = jnp.sin(ang)


def _rope_tables(positions):
    half = HEAD_DIM // 2
    t = positions.size
    per_row = LANES // half
    rows = t // per_row
    pos_rep = jnp.repeat(positions.reshape(rows, per_row), half, axis=1)
    freq = (ROPE_THETA ** (-np.arange(half, dtype=np.float64) / half)).astype(np.float32)
    freq_row = jnp.asarray(np.tile(freq, per_row)[None, :])
    blk = min(rows, 512)
    cos, sin = pl.pallas_call(
        _rope_kernel,
        grid=(rows // blk,),
        in_specs=[pl.BlockSpec((blk, LANES), lambda i: (i, 0)),
                  pl.BlockSpec((1, LANES), lambda i: (0, 0))],
        out_specs=[pl.BlockSpec((blk, LANES), lambda i: (i, 0))] * 2,
        out_shape=[jax.ShapeDtypeStruct((rows, LANES), F32)] * 2,
        compiler_params=_params(("arbitrary",)),
        name="rope_tables",
    )(pos_rep, freq_row)
    widen = lambda a: jnp.tile(a.reshape(t, half), (1, per_row))
    return widen(cos), widen(sin)


def _rope_constants():
    lane = np.arange(ROPE_SLAB)
    head_sum = (lane[:, None] // HEAD_DIM == lane[None, :] // HEAD_DIM).astype(np.float32)
    half = HEAD_DIM // 2
    rot = np.zeros((ROPE_SLAB, ROPE_SLAB), np.float32)
    for d in range(ROPE_SLAB):
        if d % HEAD_DIM < half:
            rot[d + half, d] = -1.0
        else:
            rot[d - half, d] = 1.0
    return jnp.asarray(head_sum, BF16), jnp.asarray(rot, BF16)


def _inproj_kernel(*refs, has_res):
    if has_res:
        (x_ref, y_ref, g2_ref, sc_ref, sh_ref, n1_ref, w_ref, qg_ref, kg_ref, hs_ref, rot_ref,
         cos_ref, sin_ref, q_ref, k_ref, v_ref, uc_ref, xo_ref, u_scr) = refs
        x = x_ref[...] + g2_ref[...] * y_ref[...]
        xo_ref[...] = x
    else:
        (x_ref, sc_ref, sh_ref, n1_ref, w_ref, qg_ref, kg_ref, hs_ref, rot_ref,
         cos_ref, sin_ref, q_ref, k_ref, v_ref, uc_ref, u_scr) = refs
        x = x_ref[...]
    h = _rms(x, n1_ref[...] * (1.0 + sc_ref[...])) + sh_ref[...]
    proj = jnp.dot(h.astype(BF16), w_ref[...], preferred_element_type=F32)
    q = proj[:, :ATTN_WIDTH]
    k = proj[:, ATTN_WIDTH:ATTN_WIDTH + KV_WIDTH]
    v = proj[:, ATTN_WIDTH + KV_WIDTH:ATTN_WIDTH + 2 * KV_WIDTH]
    cos = cos_ref[...]
    sin = sin_ref[...]
    reps = ATTN_WIDTH // LANES
    cos_q = jnp.concatenate([cos] * reps, axis=1)
    sin_q = jnp.concatenate([sin] * reps, axis=1)

    def head_norm_rope(t, gain, c, s):
        outs = []
        for lo in range(0, t.shape[1], ROPE_SLAB):
            wd = min(ROPE_SLAB, t.shape[1] - lo)
            ts, lanes = t[:, lo:lo + wd], slice(lo, lo + wd)
            ssq = jnp.dot((ts * ts).astype(BF16), hs_ref[:wd, :wd], preferred_element_type=F32)
            tn = (ts * lax.rsqrt(ssq * (1.0 / HEAD_DIM) + EPS) * gain[:, lanes]).astype(BF16)
            tr = jnp.dot(tn, rot_ref[:wd, :wd], preferred_element_type=F32)
            outs.append(tn.astype(F32) * c[:, lanes] + tr * s[:, lanes])
        return outs[0] if len(outs) == 1 else jnp.concatenate(outs, axis=1)

    qo = head_norm_rope(q, qg_ref[...], cos_q, sin_q)
    ko = head_norm_rope(k, kg_ref[...], cos, sin)
    q_ref[...] = qo.astype(BF16)
    k_ref[...] = jnp.concatenate([ko, pltpu.roll(ko, HEAD_DIM, axis=1)], axis=1).astype(BF16)
    v_ref[...] = jnp.concatenate([v, pltpu.roll(v, HEAD_DIM, axis=1)], axis=1).astype(BF16)
    u0 = ATTN_WIDTH + 2 * KV_WIDTH
    nchunk = u_scr.shape[1] // SSM_CHUNK
    for j in range(SSM_COLS):
        u_scr[j] = proj[:, u0 + j * LANES:u0 + (j + 1) * LANES]
    for s in range(SSM_CHUNK):
        for j in range(SSM_COLS):
            lanes = slice(s * SSM_WIDTH + j * LANES, s * SSM_WIDTH + (j + 1) * LANES)
            uc_ref[:, lanes] = u_scr[j, pl.ds(s, nchunk, stride=SSM_CHUNK), :].astype(BF16)


def _inproj(x, res, sc1, sh1, n1g, w_in, qg, kg, head_sum, rot, cos, sin, seq):
    t, d = x.shape
    tm = min(TOKEN_TILE, seq)
    per_b = seq // tm
    in_width = w_in.shape[1]
    tok = lambda w: pl.BlockSpec((tm, w), lambda i: (i, 0))
    const = lambda a: pl.BlockSpec(a.shape, lambda i: (0,) * a.ndim)
    per_batch = pl.BlockSpec((None, 1, d), lambda i: (i // per_b, 0, 0))
    chunked = pl.BlockSpec((tm // SSM_CHUNK, SSM_CHUNK * SSM_WIDTH), lambda i: (i, 0))
    ins, specs = [x], [tok(d)]
    if res is not None:
        y_prev, g2_prev = res
        ins += [y_prev, g2_prev]
        specs += [tok(d), per_batch]
    ins += [sc1, sh1, n1g, w_in, qg, kg, head_sum, rot, cos, sin]
    specs += [per_batch, per_batch, const(n1g), const(w_in), const(qg), const(kg), const(head_sum), const(rot),
              tok(LANES), tok(LANES)]
    out_shape = [jax.ShapeDtypeStruct((t, ATTN_WIDTH), BF16), jax.ShapeDtypeStruct((t, 2 * KV_WIDTH), BF16),
                 jax.ShapeDtypeStruct((t, 2 * KV_WIDTH), BF16),
                 jax.ShapeDtypeStruct((t // SSM_CHUNK, SSM_CHUNK * SSM_WIDTH), BF16)]
    out_specs = [tok(ATTN_WIDTH), tok(2 * KV_WIDTH), tok(2 * KV_WIDTH), chunked]
    if res is not None:
        out_shape.append(jax.ShapeDtypeStruct((t, d), F32))
        out_specs.append(tok(d))
    assert in_width == ATTN_WIDTH + 2 * KV_WIDTH + SSM_WIDTH
    return pl.pallas_call(
        functools.partial(_inproj_kernel, has_res=res is not None),
        grid=(t // tm,),
        in_specs=specs,
        out_specs=out_specs,
        out_shape=out_shape,
        scratch_shapes=[pltpu.VMEM((SSM_COLS, tm, LANES), F32)],
        compiler_params=_params(("parallel",), VMEM_LIMIT),
        name="inproj",
    )(*ins)


def _attn_kernel(sink_ref, q_ref, kc_ref, kp_ref, vc_ref, vp_ref, bias_ref, g_ref, o_ref):
    nsub = q_ref.shape[0] // ATTN_BLOCK
    kk = jnp.concatenate([kp_ref[...], kc_ref[...]], axis=0)
    vv = jnp.concatenate([vp_ref[...], vc_ref[...]], axis=0)
    low = lax.broadcasted_iota(jnp.int32, (kk.shape[0], KV_WIDTH), 1) < HEAD_DIM
    zero = jnp.zeros((kk.shape[0], KV_WIDTH), BF16)

    def variants(a):
        nat, swp = a[:, :KV_WIDTH], a[:, KV_WIDTH:]
        return {(0, 0): jnp.where(low, nat, zero), (0, 1): jnp.where(low, zero, swp),
                (1, 0): jnp.where(low, swp, zero), (1, 1): jnp.where(low, zero, nat)}

    kvar, vvar = variants(kk), variants(vv)
    band = bias_ref[1]
    first = bias_ref[jnp.minimum(pl.program_id(1), 1)]
    upper = lax.broadcasted_iota(jnp.int32, (2 * ATTN_BLOCK, 1), 0) < ATTN_BLOCK
    for j in range(nsub):
        bias = first if j == 0 else band
        bias2 = jnp.concatenate([bias, bias], axis=0)
        keys = slice(j * ATTN_BLOCK, (j + 2) * ATTN_BLOCK)
        qrows = slice(j * ATTN_BLOCK, (j + 1) * ATTN_BLOCK)
        tiles = [None] * (N_Q_HEADS // 2)
        for kv in range(N_KV_HEADS):
            for half in range(2):
                pairs = (2 * kv, 2 * kv + 1)
                heads = (2 * pairs[0] + half, 2 * pairs[1] + half)
                qs = jnp.concatenate([q_ref[qrows, p * LANES:(p + 1) * LANES] for p in pairs], axis=0)
                s = lax.dot_general(qs, kvar[(kv, half)][keys], (((1,), (1,)), ((), ())),
                                    preferred_element_type=F32) + bias2
                sink = jnp.where(upper, sink_ref[heads[0]], sink_ref[heads[1]]) * LOG2_E
                m = jnp.maximum(jnp.max(s, axis=-1, keepdims=True), sink)
                p = jnp.exp2(s - m)
                denom = jnp.sum(p, axis=-1, keepdims=True) + jnp.exp2(sink - m)
                o = jnp.dot(p.astype(BF16), vvar[(kv, half)][keys], preferred_element_type=F32) * (1.0 / denom)
                for r, pr in enumerate(pairs):
                    part = o[r * ATTN_BLOCK:(r + 1) * ATTN_BLOCK]
                    tiles[pr] = part if tiles[pr] is None else tiles[pr] + part
        a = jnp.concatenate(tiles, axis=1)
        o_ref[qrows, :] = _rms(a, g_ref[...]).astype(BF16)


def _attn_bias():
    qi = np.arange(ATTN_BLOCK)[:, None]
    sj = np.arange(2 * ATTN_BLOCK)[None, :]
    diff = qi + ATTN_BLOCK - sj
    band = (diff >= 0) & (diff < ATTN_BLOCK)
    first = band & (sj >= ATTN_BLOCK)
    return jnp.asarray(np.where(np.stack([first, band]), 0.0, MASK_BIAS).astype(np.float32))


def _attention(q, kx, vx, sink, out_g, bias, batch, seq):
    t = q.shape[0]
    qb = min(ATTN_Q_TILE, seq)
    nsub = qb // ATTN_BLOCK
    nq = seq // qb
    nb = seq // ATTN_BLOCK
    cur = lambda w: pl.BlockSpec((qb, w), lambda b, n, s: (b * nq + n, 0))
    prev = lambda w: pl.BlockSpec((ATTN_BLOCK, w), lambda b, n, s: (b * nb + jnp.maximum(n * nsub - 1, 0), 0))
    grid_spec = pltpu.PrefetchScalarGridSpec(
        num_scalar_prefetch=1,
        grid=(batch, nq),
        in_specs=[cur(ATTN_WIDTH), cur(2 * KV_WIDTH), prev(2 * KV_WIDTH), cur(2 * KV_WIDTH), prev(2 * KV_WIDTH),
                  pl.BlockSpec(bias.shape, lambda b, n, s: (0, 0, 0)),
                  pl.BlockSpec((1, ATTN_WIDTH), lambda b, n, s: (0, 0))],
        out_specs=cur(ATTN_WIDTH),
    )
    return pl.pallas_call(
        _attn_kernel,
        grid_spec=grid_spec,
        out_shape=jax.ShapeDtypeStruct((t, ATTN_WIDTH), BF16),
        compiler_params=_params(("parallel", "arbitrary")),
        name="swa_attention",
    )(sink, q, kx, kx, vx, vx, bias, out_g)


def _spread(x, expander3):
    hi = x.astype(BF16)
    r1 = x - hi.astype(F32)
    mid = r1.astype(BF16)
    lo = (r1 - mid.astype(F32)).astype(BF16)
    return jnp.dot(jnp.concatenate([hi, mid, lo], axis=1), expander3, preferred_element_type=F32)


def _s5_prep_kernel(lr_re_ref, lr_im_ref, ldt_ref, bt_re_ref, bt_im_ref, ct_re_ref, ct_im_ref,
                    d_ref, lcol_re_ref, lcol_im_ref, ldtcol_ref, exp_ref, exph_ref, expt_ref, expw_ref,
                    t_ref, w_ref, v_ref, la_ref, lb_ref):
    hi = lax.Precision.HIGHEST
    nl = SSM_CHUNK
    low = lax.broadcasted_iota(jnp.int32, (1, 2 * SSM_STATE), 1) < SSM_STATE
    row_low = lax.broadcasted_iota(jnp.int32, (2 * SSM_STATE, 1), 0) < SSM_STATE
    jcol = lax.broadcasted_iota(jnp.int32, (nl, 1), 0).astype(F32)
    kt_lane = lax.broadcasted_iota(jnp.int32, (SSM_GROUP_CH, nl * SSM_GROUP_CH), 1)
    kt_row = lax.broadcasted_iota(jnp.int32, (SSM_GROUP_CH, nl * SSM_GROUP_CH), 0)

    w_all, v_all, kt_all = [], [], []
    for gm in range(COL_GROUPS):
        dt = jnp.exp(ldt_ref[gm])
        lam_re, lam_im = lr_re_ref[gm], lr_im_ref[gm]
        a_r, th_r = lam_re * dt, lam_im * dt

        er = jnp.exp(jcol * a_r)
        pw_re, pw_im = er * jnp.cos(jcol * th_r), er * jnp.sin(jcol * th_r)

        nr, ni = pw_re[1:2, :] - 1.0, pw_im[1:2, :]
        den = lam_re * lam_re + lam_im * lam_im
        c_re, c_im = (nr * lam_re + ni * lam_im) / den, (ni * lam_re - nr * lam_im) / den
        bt_re, bt_im = bt_re_ref[gm], bt_im_ref[gm]
        bb_re, bb_im = c_re * bt_re - c_im * bt_im, c_re * bt_im + c_im * bt_re

        w_rows = []
        for s in range(nl):
            j = nl - 1 - s
            pr, pi = pw_re[j:j + 1, :], pw_im[j:j + 1, :]
            w_rows.append(jnp.where(low, pr * bb_re - pi * bb_im, pr * bb_im + pi * bb_re))
        w_all.append(w_rows)

        pw_re_t, pw_im_t = pw_re.T, pw_im.T
        pc, ps = _spread(pw_re_t, exp_ref[...]), _spread(pw_im_t, exp_ref[...])
        ct_re, ct_im = _spread(ct_re_ref[gm], exph_ref[...]), _spread(ct_im_ref[gm], exph_ref[...])
        a_re, a_im = ct_re * pc - ct_im * ps, ct_re * ps + ct_im * pc
        a_cat = jnp.where(row_low, a_re, -a_im)
        l1_re, l1_im = pw_re_t[:, 1:2], pw_im_t[:, 1:2]
        v_re, v_im = a_re * l1_re - a_im * l1_im, a_re * l1_im + a_im * l1_re
        v_all.append(jnp.where(row_low, v_re, -v_im))

        kt = jnp.dot(jnp.where(low, bb_re, bb_im), a_cat, precision=hi, preferred_element_type=F32)
        kt_all.append(kt + jnp.where(kt_lane == kt_row, d_ref[gm], 0.0))

    def same_group(shape, row_group, lane_group):
        r = lax.broadcasted_iota(jnp.int32, shape, 0)
        c = lax.broadcasted_iota(jnp.int32, shape, 1)
        return (row_group(r) == lane_group(c)).astype(F32)

    chan_group = lambda i: (i >> 4) & (COL_GROUPS - 1)
    state_group = lambda i: (i >> 6) & (COL_GROUPS - 1)
    over_steps = lambda m: jnp.concatenate([m] * nl, axis=1)

    kt_wide = jnp.dot(jnp.concatenate(kt_all, axis=0).astype(BF16), expt_ref[...], preferred_element_type=F32)
    bd = (kt_wide * over_steps(same_group((LANES, LANES), chan_group, chan_group))).astype(BF16)
    t_ref[0:LANES, :] = bd
    for s in range(1, nl):
        t_ref[s * LANES:(s + 1) * LANES, :] = jnp.concatenate(
            [jnp.zeros((LANES, s * LANES), BF16), bd[:, :CHUNK_LANES - s * LANES]], axis=1)

    w_stack = jnp.concatenate([w_all[gm][s] for s in range(nl) for gm in range(COL_GROUPS)], axis=0)
    w_wide = jnp.dot(w_stack.astype(BF16), expw_ref[...], preferred_element_type=F32)
    w_mask = same_group((LANES, 2 * COL_STATE), chan_group, state_group)
    w_ref[...] = (w_wide.reshape(nl, LANES, 2 * COL_STATE) * w_mask[None]).reshape(CHUNK_LANES, 2 * COL_STATE).astype(BF16)

    v_stack = jnp.concatenate([v_all[gm][half * SSM_STATE:(half + 1) * SSM_STATE, :]
                               for half in range(2) for gm in range(COL_GROUPS)], axis=0)
    v_wide = jnp.dot(v_stack.astype(BF16), expt_ref[...], preferred_element_type=F32)
    v_ref[...] = (v_wide * over_steps(same_group((2 * COL_STATE, LANES), state_group, chan_group))).astype(BF16)

    dtc = jnp.exp(ldtcol_ref[...])
    e16 = jnp.exp(nl * lcol_re_ref[...] * dtc)
    ang = nl * lcol_im_ref[...] * dtc
    la_ref[...] = e16 * jnp.cos(ang)
    lb_ref[...] = e16 * jnp.sin(ang)


def _s5_prep(lam_re, lam_im, b_re, b_im, c_re, c_im, d_skip, log_dt):
    g, p, h, nl = SSM_GROUPS, SSM_STATE, SSM_GROUP_CH, SSM_CHUNK
    cg = COL_GROUPS
    nc = lam_re.shape[0] * SSM_COLS
    col = lambda a: a.reshape((nc, cg) + a.shape[2:])
    dup_row = lambda a: col(jnp.tile(a, (1, 1, 2))[:, :, None, :])
    bt = lambda a: col(jnp.tile(jnp.swapaxes(a, 2, 3), (1, 1, 1, 2)))
    ct = lambda a: col(jnp.tile(jnp.swapaxes(a, 2, 3), (1, 1, 2, 1)))
    d_pad = col(jnp.pad(d_skip.reshape(-1, g, 1, h), ((0, 0), (0, 0), (0, 0), (0, nl * h - h))))
    wide = lambda a: a.reshape(nc, 1, cg * p)
    expand = jnp.asarray(np.tile(np.repeat(np.eye(nl, dtype=np.float32), h, axis=1), (3, 1)), BF16)
    expand_h = jnp.asarray(np.tile(np.eye(h, dtype=np.float32), (3, nl)), BF16)
    exp_t = np.zeros((nl, h, nl, cg, h), np.float32)
    exp_w = np.zeros((2, p, 2, cg, p), np.float32)
    for gm in range(cg):
        exp_t[:, :, :, gm, :] = np.eye(nl * h, dtype=np.float32).reshape(nl, h, nl, h)
        exp_w[:, :, :, gm, :] = np.eye(2 * p, dtype=np.float32).reshape(2, p, 2, p)
    exp_t = jnp.asarray(exp_t.reshape(nl * h, CHUNK_LANES), BF16)
    exp_w = jnp.asarray(exp_w.reshape(2 * p, 2 * COL_STATE), BF16)
    blk = lambda *s: pl.BlockSpec((None,) + s, lambda i: (i,) + (0,) * len(s))
    const = lambda a: pl.BlockSpec(a.shape, lambda i: (0,) * a.ndim)
    lw = nl * h
    return pl.pallas_call(
        _s5_prep_kernel,
        grid=(nc,),
        in_specs=[blk(cg, 1, 2 * p), blk(cg, 1, 2 * p), blk(cg, 1, 1),
                  blk(cg, h, 2 * p), blk(cg, h, 2 * p), blk(cg, 2 * p, h), blk(cg, 2 * p, h), blk(cg, 1, lw),
                  blk(1, cg * p), blk(1, cg * p), blk(1, cg * p), const(expand), const(expand_h), const(exp_t),
                  const(exp_w)],
        out_specs=[blk(CHUNK_LANES, CHUNK_LANES), blk(CHUNK_LANES, 2 * COL_STATE), blk(2 * COL_STATE, CHUNK_LANES),
                   blk(1, COL_STATE), blk(1, COL_STATE)],
        out_shape=[jax.ShapeDtypeStruct((nc, CHUNK_LANES, CHUNK_LANES), BF16),
                   jax.ShapeDtypeStruct((nc, CHUNK_LANES, 2 * COL_STATE), BF16),
                   jax.ShapeDtypeStruct((nc, 2 * COL_STATE, CHUNK_LANES), BF16),
                   jax.ShapeDtypeStruct((nc, 1, COL_STATE), F32), jax.ShapeDtypeStruct((nc, 1, COL_STATE), F32)],
        compiler_params=_params(("parallel",), VMEM_LIMIT),
        name="s5_prep",
    )(dup_row(lam_re), dup_row(lam_im), col(log_dt[:, :, None, None]),
      bt(b_re), bt(b_im), ct(c_re), ct(c_im), d_pad, wide(lam_re), wide(lam_im),
      wide(jnp.repeat(log_dt, p, axis=1)), expand, expand_h, exp_t, exp_w)


def _s5_kernel(*refs, nchunks, nb):
    uc_refs = refs[:SSM_CHUNK]
    t_ref, w_ref, v_ref, la_ref, lb_ref, o_ref, ucat_ref, s_ref, xp_ref = refs[SSM_CHUNK:]

    @pl.when(pl.program_id(1) == 0)
    def _():
        for s in range(SSM_CHUNK):
            ucat_ref[:, s * LANES:(s + 1) * LANES] = uc_refs[s][...]
        s_in = jnp.dot(ucat_ref[...], w_ref[...], preferred_element_type=F32)
        nblk = COL_STATE // LANES
        pitch = s_ref.shape[1] // nb
        for b in range(2 * nblk):
            for q in range(nb):
                s_ref[b, q * pitch:q * pitch + nchunks, :] = s_in[q * nchunks:(q + 1) * nchunks, b * LANES:(b + 1) * LANES]
        lr = [jnp.broadcast_to(la_ref[:, b * LANES:(b + 1) * LANES], (nb, LANES)) for b in range(nblk)]
        li = [jnp.broadcast_to(lb_ref[:, b * LANES:(b + 1) * LANES], (nb, LANES)) for b in range(nblk)]

        def step(c, carry):
            rows = pl.ds(c, nb, stride=pitch)
            out = []
            for b in range(nblk):
                re, im = carry[2 * b], carry[2 * b + 1]
                xp_ref[b, rows, :] = re
                xp_ref[nblk + b, rows, :] = im
                out.append(lr[b] * re - li[b] * im + s_ref[b, rows, :])
                out.append(lr[b] * im + li[b] * re + s_ref[nblk + b, rows, :])
            return tuple(out)

        zero = jnp.zeros((nb, LANES), F32)
        lax.fori_loop(0, nchunks, step, (zero,) * (2 * nblk), unroll=4)

    pitch = xp_ref.shape[1] // nb
    xp = jnp.concatenate(
        [jnp.concatenate([xp_ref[b, q * pitch:q * pitch + nchunks, :] for q in range(nb)], axis=0)
         for b in range(2 * COL_STATE // LANES)], axis=1).astype(BF16)
    inter = jnp.dot(xp, v_ref[...], preferred_element_type=F32)
    for kk in range(SSM_NSPLIT):
        @pl.when(pl.program_id(1) == kk)
        def _():
            live = (kk + 1) * (CHUNK_LANES // SSM_NSPLIT)
            intra = jnp.dot(ucat_ref[:, :live], t_ref[:live, :], preferred_element_type=F32)
            o_ref[...] = (intra + inter).astype(BF16)


def _s5_scan(uc, mats, layer, nchunks, nb):
    rows = uc.shape[0]
    c0 = layer * SSM_COLS
    split = CHUNK_LANES // SSM_NSPLIT
    u_spec = lambda s: pl.BlockSpec((rows, LANES), lambda j, k: (0, SSM_COLS * s + j))
    return pl.pallas_call(
        functools.partial(_s5_kernel, nchunks=nchunks, nb=nb),
        grid=(SSM_COLS, SSM_NSPLIT),
        in_specs=[u_spec(s) for s in range(SSM_CHUNK)] + [
            pl.BlockSpec((None, CHUNK_LANES, split), lambda j, k: (c0 + j, 0, k)),
            pl.BlockSpec((None, CHUNK_LANES, 2 * COL_STATE), lambda j, k: (c0 + j, 0, 0)),
            pl.BlockSpec((None, 2 * COL_STATE, split), lambda j, k: (c0 + j, 0, k)),
            pl.BlockSpec((None, 1, COL_STATE), lambda j, k: (c0 + j, 0, 0)),
            pl.BlockSpec((None, 1, COL_STATE), lambda j, k: (c0 + j, 0, 0))],
        out_specs=pl.BlockSpec((None, rows, split), lambda j, k: (j, 0, k)),
        out_shape=jax.ShapeDtypeStruct((SSM_COLS, rows, CHUNK_LANES), BF16),
        scratch_shapes=[pltpu.VMEM((rows, CHUNK_LANES), BF16),
                        pltpu.VMEM((2 * COL_STATE // LANES, nb * (nchunks + S5_ROW_PAD), LANES), F32),
                        pltpu.VMEM((2 * COL_STATE // LANES, nb * (nchunks + S5_ROW_PAD), LANES), F32)],
        compiler_params=_params(("parallel", "arbitrary"), VMEM_LIMIT),
        name="s5_scan",
    )(*([uc] * SSM_CHUNK), *mats)


def _route(logits, bias):
    m = jnp.max(logits, axis=0, keepdims=True)
    e = jnp.exp(logits - m)
    probs = e / jnp.sum(e, axis=0, keepdims=True)
    sel = probs + bias
    row = lambda a, i: a[i:i + 1, :]
    best_score, best = None, None
    for grp in range(N_EXPERT_GROUPS):
        a, b, c, d = (row(sel, EXPERTS_PER_GROUP * grp + i) for i in range(EXPERTS_PER_GROUP))
        hab, lab, hcd, lcd = jnp.maximum(a, b), jnp.minimum(a, b), jnp.maximum(c, d), jnp.minimum(c, d)
        top1 = jnp.maximum(hab, hcd)
        top2 = jnp.maximum(jnp.maximum(lab, lcd), jnp.minimum(hab, hcd))
        score = top1 + top2
        if grp == 0:
            best_score, best = score, jnp.zeros(score.shape, jnp.int32)
        else:
            better = score > best_score
            best = jnp.where(better, grp, best)
            best_score = jnp.where(better, score, best_score)

    def pick(a, i):
        out = row(a, i)
        for grp in range(1, N_EXPERT_GROUPS):
            out = jnp.where(best == grp, row(a, EXPERTS_PER_GROUP * grp + i), out)
        return out

    s_in = [pick(sel, i) for i in range(EXPERTS_PER_GROUP)]
    p_in = [pick(probs, i) for i in range(EXPERTS_PER_GROUP)]
    neg = jnp.full(s_in[0].shape, -jnp.inf, F32)

    def argmax_first(vals):
        idx, val = jnp.zeros(vals[0].shape, jnp.int32), vals[0]
        for i in range(1, len(vals)):
            better = vals[i] > val
            idx = jnp.where(better, i, idx)
            val = jnp.where(better, vals[i], val)
        return idx

    i1 = argmax_first(s_in)
    i2 = argmax_first([jnp.where(i1 == i, neg, s_in[i]) for i in range(EXPERTS_PER_GROUP)])
    zero = jnp.zeros(p_in[0].shape, F32)
    g1 = sum(jnp.where(i1 == i, p_in[i], zero) for i in range(EXPERTS_PER_GROUP))
    g2 = sum(jnp.where(i2 == i, p_in[i], zero) for i in range(EXPERTS_PER_GROUP))
    tot = g1 + g2
    w1, w2 = g1 / tot, g2 / tot
    first_low = i1 < i2
    low, high = jnp.minimum(i1, i2), jnp.maximum(i1, i2)
    w_low, w_high = jnp.where(first_low, w1, w2), jnp.where(first_low, w2, w1)
    pos = jnp.where(low == 0, high - 1, jnp.where(low == 1, jnp.where(high == 2, 4, 3), 5))
    swap = low == 2
    bucket = best * PAIRS_PER_GROUP + pos
    return jnp.concatenate([jnp.where(swap, w_high, w_low), jnp.where(swap, w_low, w_high)], axis=0), bucket


def _router_logits(w_t, h):
    w_hi = w_t.astype(BF16)
    w_r = w_t - w_hi.astype(F32)
    w_mid = w_r.astype(BF16)
    w_lo = (w_r - w_mid.astype(F32)).astype(BF16)
    h_hi = h.astype(BF16)
    h_lo = (h - h_hi.astype(F32)).astype(BF16)
    dims = (((1,), (1,)), ((), ()))
    a = lax.dot_general(jnp.concatenate([w_hi, w_mid, w_lo], axis=0), h_hi, dims, preferred_element_type=F32)
    b = lax.dot_general(jnp.concatenate([w_hi, w_mid], axis=0), h_lo, dims, preferred_element_type=F32)
    e = w_t.shape[0]
    return a[:e] + a[e:2 * e] + a[2 * e:] + b[:e] + b[e:]


def _post_kernel(x_ref, at_ref, yc_ref, wglu_ref, gs_ref, wo_ref, g1_ref, n2_ref, sc_ref, sh_ref,
                 wrt_ref, rb_ref, x1_ref, h2_ref, cw_ref, gid_ref, y_scr):
    nchunk = y_scr.shape[1] // SSM_CHUNK
    for s in range(SSM_CHUNK):
        for j in range(SSM_COLS):
            y_scr[j, pl.ds(s, nchunk, stride=SSM_CHUNK), :] = yc_ref[j, :, s * LANES:(s + 1) * LANES].astype(F32)
    yg = jax.nn.gelu(jnp.concatenate([y_scr[j] for j in range(SSM_COLS)], axis=1))
    z = yg * jax.nn.sigmoid(jnp.dot(yg.astype(BF16), wglu_ref[...], preferred_element_type=F32))
    zn = _rms(z, gs_ref[...]).astype(BF16)
    o = (jnp.dot(at_ref[...], wo_ref[:ATTN_WIDTH, :], preferred_element_type=F32)
         + jnp.dot(zn, wo_ref[ATTN_WIDTH:, :], preferred_element_type=F32))
    x1 = x_ref[...] + g1_ref[...] * o
    x1_ref[...] = x1
    h2 = _rms(x1, n2_ref[...] * (1.0 + sc_ref[...])) + sh_ref[...]
    h2_ref[...] = h2
    logits = _router_logits(wrt_ref[...], h2)
    cw, bucket = _route(logits, rb_ref[...])
    cw_ref[...] = cw
    gid_ref[...] = bucket


def _post(x, attn, yc, w_glu, ssm_g, w_out, g1, n2g, sc2, sh2, w_router_t, router_bias, seq):
    t, d = x.shape
    tm = min(TOKEN_TILE, seq)
    per_b = seq // tm
    tok = lambda w: pl.BlockSpec((tm, w), lambda i: (i, 0))
    const = lambda a: pl.BlockSpec(a.shape, lambda i: (0,) * a.ndim)
    per_batch = pl.BlockSpec((None, 1, d), lambda i: (i // per_b, 0, 0))
    col = lambda r: pl.BlockSpec((r, tm), lambda i: (0, i))
    chunked = pl.BlockSpec((SSM_COLS, tm // SSM_CHUNK, CHUNK_LANES), lambda i: (0, i, 0))
    return pl.pallas_call(
        _post_kernel,
        grid=(t // tm,),
        in_specs=[tok(d), tok(ATTN_WIDTH), chunked, const(w_glu), const(ssm_g),
                  const(w_out), per_batch, const(n2g), per_batch, per_batch, const(w_router_t), const(router_bias)],
        out_specs=[tok(d), tok(d), col(2), col(1)],
        out_shape=[jax.ShapeDtypeStruct((t, d), F32), jax.ShapeDtypeStruct((t, d), F32),
                   jax.ShapeDtypeStruct((2, t), F32), jax.ShapeDtypeStruct((1, t), jnp.int32)],
        scratch_shapes=[pltpu.VMEM((SSM_COLS, tm, LANES), F32)],
        compiler_params=_params(("parallel",), VMEM_LIMIT),
        name="post_mix",
    )(x, attn, yc, w_glu, ssm_g, w_out, g1, n2g, sc2, sh2, w_router_t, router_bias)


def _moe_kernel(kind_ref, rb_ref, bk_ref, pa_ref, pb_ref, first_ref, cast_ref, cpos_ref, pe_ref, offs_ref,
                x_ref, cw_ref, wg_ref, wu_ref, wd_ref, o_ref, wg_s, wu_s, wd_s):
    s = pl.program_id(0)

    @pl.when(cast_ref[s] == 1)
    def _():
        slot = cpos_ref[s]
        wg_s[slot] = wg_ref[...].astype(BF16)
        wu_s[slot] = wu_ref[...].astype(BF16)
        wd_s[slot] = wd_ref[...].astype(BF16)

    @pl.when(kind_ref[s] == STEP_ITEM)
    def _():
        bucket = bk_ref[s]
        base = rb_ref[s] * MOE_ROWS
        lo_row, hi_row = offs_ref[bucket] - base, offs_ref[bucket + 1] - base
        slots = (pa_ref[s], pb_ref[s])
        is_first = first_ref[s] == 1

        def run(r0, r1, z0, z1):
            rows = r0 + lax.broadcasted_iota(jnp.int32, (r1 - r0, 1), 0)
            cw = jnp.where((rows >= lo_row) & (rows < hi_row), cw_ref[r0:r1, :], 0.0)
            x = x_ref[r0:r1, :].astype(BF16)
            y = None
            for k in range(2):
                gate = jnp.dot(x, wg_s[slots[k]], preferred_element_type=F32)
                up = jnp.dot(x, wu_s[slots[k]], preferred_element_type=F32)
                act = (gate * jax.nn.sigmoid(gate) * up * cw[:, k:k + 1]).astype(BF16)
                yk = jnp.dot(act, wd_s[slots[k]], preferred_element_type=F32)
                y = yk if y is None else y + yk

            @pl.when(is_first)
            def _():
                o_ref[r0:r1, :] = y
                for a, b in ((z0, r0), (r1, z1)):
                    if b > a:
                        o_ref[a:b, :] = jnp.zeros((b - a, o_ref.shape[1]), F32)

            @pl.when(jnp.logical_not(is_first))
            def _():
                o_ref[r0:r1, :] += y

        for b0 in range(0, MOE_ROWS, MOE_SUB):
            b1, mid = b0 + MOE_SUB, b0 + MOE_SUB // 2
            has_rows = (lo_row < b1) & (hi_row > b0)
            needs_lower, needs_upper = has_rows & (lo_row < mid), has_rows & (hi_row > mid)
            pl.when(needs_lower & needs_upper)(lambda: run(b0, b1, b0, b1))
            pl.when(needs_lower & jnp.logical_not(needs_upper))(lambda: run(b0, mid, b0, b1))
            pl.when(jnp.logical_not(needs_lower) & needs_upper)(lambda: run(mid, b1, b0, b1))
            if MOE_ROWS > MOE_SUB:
                @pl.when(jnp.logical_not(has_rows) & is_first)
                def _():
                    o_ref[b0:b1, :] = jnp.zeros((MOE_SUB, o_ref.shape[1]), F32)


def _moe_plan_kernel(offs_ref, kind_ref, rb_ref, bk_ref, pa_ref, pb_ref, first_ref, cast_ref, cpos_ref, pe_ref,
                     irb, ibk, *, n_steps_max):
    i32 = jnp.int32
    ng, epg, ppg = N_EXPERT_GROUPS, EXPERTS_PER_GROUP, PAIRS_PER_GROUP

    shift = MOE_ROWS.bit_length() - 1

    def bucket_body(bk, cnt):
        a, b = offs_ref[bk], offs_ref[bk + 1]
        first_blk = lax.shift_right_logical(a, shift)
        n_blk = jnp.where(b > a, lax.shift_right_logical(b - 1, shift) - first_blk + 1, 0)

        def block_body(j, cnt):
            irb[cnt] = first_blk + j
            ibk[cnt] = bk
            return cnt + 1

        return lax.fori_loop(0, n_blk, block_body, cnt)

    n_items = lax.fori_loop(0, ng * ppg, bucket_body, i32(0))

    def count_body(i, m):
        g = ibk[i] // ppg
        return tuple(m[k] + (g == k).astype(i32) for k in range(ng))

    m = lax.fori_loop(0, n_items, count_body, (i32(0),) * ng)

    def next_group(g):
        nxt = i32(-1)
        for k in range(ng - 1, 0, -1):
            nxt = jnp.where((k > g) & (m[k] > 0), k, nxt)
        return nxt

    def emit(s, kind, rb, bk, pa, pb, first, cast, cpos, pe):
        kind_ref[s], rb_ref[s], bk_ref[s], pa_ref[s], pb_ref[s] = kind, rb, bk, pa, pb
        first_ref[s], cast_ref[s], cpos_ref[s], pe_ref[s] = first, cast, cpos, pe

    def item_body(i, carry):
        s, gcur, parity, q, last_pe, last_rb = carry
        rb, bk = irb[i], ibk[i]
        g, pos = bk // ppg, bk % ppg
        new = g != gcur
        started = gcur >= 0
        loaders = jnp.where(new, jnp.where(started, jnp.maximum(epg - q, 0), epg), 0)
        parity = jnp.where(new & started, 1 - parity, parity)
        q = jnp.where(new, 0, q)
        for j in range(epg):
            on = j >= epg - loaders
            emit(s, STEP_LOAD, rb, bk, 0, 0, 0, 1, parity * epg + j, epg * g + j)
            last_pe = jnp.where(on, epg * g + j, last_pe)
            s = s + on.astype(i32)
        nxt = next_group(g)
        pre = (q < epg) & (nxt >= 0)
        pe = jnp.where(pre, epg * nxt + q, last_pe)
        slot_a, slot_b = i32(PAIR_SLOTS[0][0]), i32(PAIR_SLOTS[0][1])
        for p in range(1, ppg):
            slot_a = jnp.where(pos == p, PAIR_SLOTS[p][0], slot_a)
            slot_b = jnp.where(pos == p, PAIR_SLOTS[p][1], slot_b)
        emit(s, STEP_ITEM, rb, bk, parity * epg + slot_a, parity * epg + slot_b, (rb != last_rb).astype(i32),
             pre.astype(i32), (1 - parity) * epg + q, pe)
        return s + 1, g, parity, q + 1, pe, rb

    s, _, _, _, last_pe, last_rb = lax.fori_loop(
        0, n_items, item_body, (i32(0), i32(-1), i32(0), i32(0), i32(0), i32(-1)))
    last_bk = ibk[jnp.maximum(n_items - 1, 0)]

    def pad_body(s, _):
        emit(s, STEP_PAD, last_rb, last_bk, 0, 0, 0, 0, 0, last_pe)
        return 0

    lax.fori_loop(s, n_steps_max, pad_body, 0)


def _moe_steps(bucket, t):
    i32 = jnp.int32
    nbk = N_EXPERT_GROUPS * PAIRS_PER_GROUP
    order = jnp.argsort(bucket, stable=True).astype(i32)
    counts = jnp.sum((bucket[None, :] == jnp.arange(nbk, dtype=i32)[:, None]).astype(i32), axis=1)
    offs = jnp.concatenate([jnp.zeros((1,), i32), jnp.cumsum(counts).astype(i32)])
    assert MOE_ROWS & (MOE_ROWS - 1) == 0
    n_items_max = t // MOE_ROWS + nbk - 1
    n_steps_max = n_items_max + N_EXPERTS
    smem = pl.BlockSpec(memory_space=pltpu.SMEM)
    tables = pl.pallas_call(
        functools.partial(_moe_plan_kernel, n_steps_max=n_steps_max),
        in_specs=[smem],
        out_specs=[smem] * 9,
        out_shape=[jax.ShapeDtypeStruct((n_steps_max,), i32)] * 9,
        scratch_shapes=[pltpu.SMEM((n_items_max + 1,), i32)] * 2,
        name="moe_plan",
    )(offs)
    return order, (*tables, offs), n_steps_max


def _moe(xs, cws, w_gate, w_up, w_down, layer, tables, n_steps_max):
    t, d = xs.shape
    ff = w_gate.shape[3]
    w_map = lambda s, kind, rb, bk, pa, pb, fi, ca, cp, pe, of: (layer, pe[s], 0, 0)
    row_map = lambda s, kind, rb, *_: (rb[s], 0)
    nres = 2 * EXPERTS_PER_GROUP
    grid_spec = pltpu.PrefetchScalarGridSpec(
        num_scalar_prefetch=len(tables),
        grid=(n_steps_max,),
        in_specs=[pl.BlockSpec((MOE_ROWS, d), row_map), pl.BlockSpec((MOE_ROWS, 2), row_map),
                  pl.BlockSpec((None, None, d, ff), w_map), pl.BlockSpec((None, None, d, ff), w_map),
                  pl.BlockSpec((None, None, ff, d), w_map)],
        out_specs=pl.BlockSpec((MOE_ROWS, d), row_map),
        scratch_shapes=[pltpu.VMEM((nres, d, ff), BF16), pltpu.VMEM((nres, d, ff), BF16),
                        pltpu.VMEM((nres, ff, d), BF16)],
    )
    return pl.pallas_call(
        _moe_kernel,
        grid_spec=grid_spec,
        out_shape=jax.ShapeDtypeStruct((t, d), F32),
        compiler_params=_params(("arbitrary",), MOE_VMEM_LIMIT),
        name="moe_grouped",
    )(*tables, xs, cws, w_gate, w_up, w_down)


def _take_rows(a, idx):
    return a.at[idx].get(mode="promise_in_bounds", unique_indices=True)


def _final_kernel(x_ref, y_ref, g_ref, o_ref):
    o_ref[...] = x_ref[...] + g_ref[...] * y_ref[...]


def _final(x1, y, g2, seq):
    t, d = x1.shape
    tm = min(TOKEN_TILE, seq)
    per_b = seq // tm
    tok = lambda w: pl.BlockSpec((tm, w), lambda i: (i, 0))
    return pl.pallas_call(
        _final_kernel,
        grid=(t // tm,),
        in_specs=[tok(d), tok(d), pl.BlockSpec((None, 1, d), lambda i: (i // per_b, 0, 0))],
        out_specs=tok(d),
        out_shape=jax.ShapeDtypeStruct((t, d), F32),
        compiler_params=_params(("parallel",)),
        name="final_residual",
    )(x1, y, g2)


def kernel(x, c, positions, ada_w, ada_b, norm1_g, w_in, q_norm_g, k_norm_g, attn_sink, lam_re, lam_im, ssm_b_re, ssm_b_im, ssm_c_re, ssm_c_im, ssm_d, ssm_log_dt, w_glu, attn_out_g, ssm_out_g, w_out, norm2_g, w_router, router_bias, w_exp_gate, w_exp_up, w_exp_down):
    batch, seq, d = x.shape
    depth = ada_w.shape[0]
    t = batch * seq
    assert seq % ATTN_BLOCK == 0 and seq % SSM_CHUNK == 0 and t % MOE_ROWS == 0

    mod = _adaln_mod(c, ada_w, ada_b).reshape(depth, 6, batch, 1, d)
    cos, sin = _rope_tables(positions)
    head_sum, rot = _rope_constants()
    bias = _attn_bias()
    w_router_t = w_router.T
    s5_mats = _s5_prep(lam_re, lam_im, ssm_b_re, ssm_b_im, ssm_c_re, ssm_c_im, ssm_d, ssm_log_dt)
    router_bias_col = router_bias.reshape(N_EXPERTS, 1)

    xf = x.reshape(t, d)
    res = None
    for l in range(depth):
        sh1, sc1, g1, sh2, sc2, g2 = (mod[l, j] for j in range(6))
        qg = (jnp.tile(q_norm_g[l], N_Q_HEADS) * (HEAD_DIM ** -0.5 * LOG2_E)).reshape(1, ATTN_WIDTH)
        kg = jnp.tile(k_norm_g[l], N_KV_HEADS).reshape(1, KV_WIDTH)
        outs = _inproj(xf, res, sc1, sh1, norm1_g[l].reshape(1, d), w_in[l].astype(BF16), qg, kg, head_sum, rot,
                       cos, sin, seq)
        if res is None:
            q, kx, vx, uc = outs
        else:
            q, kx, vx, uc, xf = outs
        attn = _attention(q, kx, vx, attn_sink[l], attn_out_g[l].reshape(1, ATTN_WIDTH), bias, batch, seq)
        yc = _s5_scan(uc, s5_mats, l, seq // SSM_CHUNK, batch)
        x1, h2, cw, gid = _post(xf, attn, yc, w_glu[l].astype(BF16), ssm_out_g[l].reshape(1, SSM_WIDTH),
                                 w_out[l].astype(BF16), g1, norm2_g[l].reshape(1, d), sc2, sh2, w_router_t,
                                 router_bias_col, seq)
        order, tables, n_steps_max = _moe_steps(gid.reshape(t), t)
        y_sorted = _moe(_take_rows(h2, order), _take_rows(cw.T, order), w_exp_gate, w_exp_up, w_exp_down, l,
                        tables, n_steps_max)
        y = _take_rows(y_sorted, jnp.argsort(order).astype(jnp.int32))
        xf, res = x1, (y, g2)
    y, g2 = res
    return _final(xf, y, g2, seq).reshape(batch, seq, d)
```

```python
import functools
import math

import numpy as np
import jax
import jax.numpy as jnp
from jax import lax
from jax.experimental import pallas as pl
from jax.experimental.pallas import tpu as pltpu

F32 = jnp.float32
BF16 = jnp.bfloat16

HEAD_DIM = 64
N_Q_HEADS = 8
N_KV_HEADS = 2
Q_PER_KV = N_Q_HEADS // N_KV_HEADS
ATTN_WIDTH = N_Q_HEADS * HEAD_DIM
KV_WIDTH = N_KV_HEADS * HEAD_DIM
ATTN_BLOCK = 128
ATTN_Q_TILE = 2048
ROPE_THETA = 10000.0
ROPE_SLAB = 256
LANES = 128
SSM_GROUP_CH = 16
SSM_GROUPS = 32
SSM_WIDTH = SSM_GROUPS * SSM_GROUP_CH
SSM_STATE = 64
SSM_CHUNK = 16
SSM_COLS = SSM_WIDTH // LANES
COL_GROUPS = LANES // SSM_GROUP_CH
COL_STATE = COL_GROUPS * SSM_STATE
CHUNK_LANES = SSM_CHUNK * LANES
SSM_NSPLIT = 4
S5_ROW_PAD = 8
N_EXPERTS = 16
N_EXPERT_GROUPS = 4
EXPERTS_PER_GROUP = N_EXPERTS // N_EXPERT_GROUPS
PAIRS_PER_GROUP = EXPERTS_PER_GROUP * (EXPERTS_PER_GROUP - 1) // 2
PAIR_SLOTS = ((0, 1), (0, 2), (0, 3), (1, 3), (1, 2), (3, 2))
EPS = 1e-6
LOG2_E = math.log2(math.e)
MASK_BIAS = -1e30

TOKEN_TILE = 1024
MOE_ROWS = 512
MOE_SUB = 256
VMEM_LIMIT = 48 * 1024 * 1024
MOE_VMEM_LIMIT = 56 * 1024 * 1024
STEP_PAD, STEP_LOAD, STEP_ITEM = 0, 1, 2


def _params(sem, vmem=None):
    return pltpu.CompilerParams(dimension_semantics=sem, vmem_limit_bytes=vmem)


def _rms(x, g):
    return x * lax.rsqrt(jnp.mean(x * x, axis=-1, keepdims=True) + EPS) * g


def _mod_kernel(c_ref, w_ref, b_ref, o_ref):
    c = c_ref[...]
    s = c * jax.nn.sigmoid(c)
    o_ref[...] = jnp.dot(s.astype(BF16), w_ref[...].astype(BF16), preferred_element_type=F32) + b_ref[...]


def _adaln_mod(c, ada_w, ada_b):
    depth, d, d6 = ada_w.shape
    nb = c.shape[0]
    n6 = d6 // d
    return pl.pallas_call(
        _mod_kernel,
        grid=(depth, n6),
        in_specs=[pl.BlockSpec((nb, d), lambda l, j: (0, 0)),
                  pl.BlockSpec((None, d, d), lambda l, j: (l, 0, j)),
                  pl.BlockSpec((None, None, 1, d), lambda l, j: (l, j, 0, 0))],
        out_specs=pl.BlockSpec((None, None, nb, d), lambda l, j: (l, j, 0, 0)),
        out_shape=jax.ShapeDtypeStruct((depth, n6, nb, d), F32),
        compiler_params=_params(("arbitrary", "arbitrary"), VMEM_LIMIT),
        name="adaln_mod",
    )(c, ada_w, ada_b.reshape(depth, n6, 1, d))


def _spread(x, expander3):
    hi = x.astype(BF16)
    r1 = x - hi.astype(F32)
    mid = r1.astype(BF16)
    lo = (r1 - mid.astype(F32)).astype(BF16)
    return jnp.dot(jnp.concatenate([hi, mid, lo], axis=1), expander3, preferred_element_type=F32)


def _rope_kernel(pos_ref, freq_ref, lane_ref, quarter_ref, cos_ref, sin_ref):
    per_row = pos_ref.shape[1]
    rows = pos_ref.shape[0]
    pos = _spread(pos_ref[...].astype(F32), lane_ref[...])
    ang = pos * freq_ref[...]
    cos, sin = jnp.cos(ang), jnp.sin(ang)
    for j in range(per_row):
        cos_ref[pl.ds(j, rows, stride=per_row), :] = _spread(cos, quarter_ref[j])
        sin_ref[pl.ds(j, rows, stride=per_row), :] = _spread(sin, quarter_ref[j])


def _rope_tables(positions):
    half = HEAD_DIM // 2
    t = positions.size
    per_row = LANES // half
    rows = t // per_row
    freq = (ROPE_THETA ** (-np.arange(half, dtype=np.float64) / half)).astype(np.float32)
    freq_row = jnp.asarray(np.tile(freq, per_row)[None, :])
    to_quarter = np.repeat(np.eye(per_row, dtype=np.float32), half, axis=1)
    spread = np.zeros((per_row, LANES, LANES), np.float32)
    for j in range(per_row):
        spread[j, j * half:(j + 1) * half, :] = np.tile(np.eye(half, dtype=np.float32), (1, per_row))
    lane_sel = jnp.asarray(np.tile(to_quarter, (3, 1)), BF16)
    quarter_sel = jnp.asarray(np.tile(spread, (1, 3, 1)), BF16)
    blk = min(rows, 512)
    out = pl.BlockSpec((blk * per_row, LANES), lambda i: (i, 0))
    const = lambda a: pl.BlockSpec(a.shape, lambda i: (0,) * a.ndim)
    return pl.pallas_call(
        _rope_kernel,
        grid=(rows // blk,),
        in_specs=[pl.BlockSpec((blk, per_row), lambda i: (i, 0)), const(freq_row), const(lane_sel), const(quarter_sel)],
        out_specs=[out, out],
        out_shape=[jax.ShapeDtypeStruct((t, LANES), F32)] * 2,
        compiler_params=_params(("arbitrary",)),
        name="rope_tables",
    )(positions.reshape(rows, per_row), freq_row, lane_sel, quarter_sel)


def _rope_constants():
    lane = np.arange(ROPE_SLAB)
    head_sum = (lane[:, None] // HEAD_DIM == lane[None, :] // HEAD_DIM).astype(np.float32)
    half = HEAD_DIM // 2
    rot = np.zeros((ROPE_SLAB, ROPE_SLAB), np.float32)
    for d in range(ROPE_SLAB):
        if d % HEAD_DIM < half:
            rot[d + half, d] = -1.0
        else:
            rot[d - half, d] = 1.0
    return jnp.asarray(head_sum, BF16), jnp.asarray(rot, BF16)


def _inproj_kernel(*refs, has_res):
    if has_res:
        (x_ref, y_ref, g2_ref, sc_ref, sh_ref, n1_ref, w_ref, qg_ref, kg_ref, hs_ref, rot_ref,
         cos_ref, sin_ref, q_ref, k_ref, v_ref, uc_ref, xo_ref, u_scr) = refs
        x = x_ref[...] + g2_ref[...] * y_ref[...]
        xo_ref[...] = x
    else:
        (x_ref, sc_ref, sh_ref, n1_ref, w_ref, qg_ref, kg_ref, hs_ref, rot_ref,
         cos_ref, sin_ref, q_ref, k_ref, v_ref, uc_ref, u_scr) = refs
        x = x_ref[...]
    h = _rms(x, n1_ref[...] * (1.0 + sc_ref[...])) + sh_ref[...]
    proj = jnp.dot(h.astype(BF16), w_ref[...], preferred_element_type=F32)
    q = proj[:, :ATTN_WIDTH]
    k = proj[:, ATTN_WIDTH:ATTN_WIDTH + KV_WIDTH]
    v = proj[:, ATTN_WIDTH + KV_WIDTH:ATTN_WIDTH + 2 * KV_WIDTH]
    cos = cos_ref[...]
    sin = sin_ref[...]
    reps = ATTN_WIDTH // LANES
    cos_q = jnp.concatenate([cos] * reps, axis=1)
    sin_q = jnp.concatenate([sin] * reps, axis=1)

    def head_norm_rope(t, gain, c, s):
        outs = []
        for lo in range(0, t.shape[1], ROPE_SLAB):
            wd = min(ROPE_SLAB, t.shape[1] - lo)
            ts, lanes = t[:, lo:lo + wd], slice(lo, lo + wd)
            ssq = jnp.dot((ts * ts).astype(BF16), hs_ref[:wd, :wd], preferred_element_type=F32)
            tn = (ts * lax.rsqrt(ssq * (1.0 / HEAD_DIM) + EPS) * gain[:, lanes]).astype(BF16)
            tr = jnp.dot(tn, rot_ref[:wd, :wd], preferred_element_type=F32)
            outs.append(tn.astype(F32) * c[:, lanes] + tr * s[:, lanes])
        return outs[0] if len(outs) == 1 else jnp.concatenate(outs, axis=1)

    qo = head_norm_rope(q, qg_ref[...], cos_q, sin_q)
    ko = head_norm_rope(k, kg_ref[...], cos, sin)
    q_ref[...] = qo.astype(BF16)
    k_ref[...] = jnp.concatenate([ko, pltpu.roll(ko, HEAD_DIM, axis=1)], axis=1).astype(BF16)
    v_ref[...] = jnp.concatenate([v, pltpu.roll(v, HEAD_DIM, axis=1)], axis=1).astype(BF16)
    u0 = ATTN_WIDTH + 2 * KV_WIDTH
    nchunk = u_scr.shape[1] // SSM_CHUNK
    for j in range(SSM_COLS):
        u_scr[j] = proj[:, u0 + j * LANES:u0 + (j + 1) * LANES]
    for s in range(SSM_CHUNK):
        for j in range(SSM_COLS):
            lanes = slice(s * SSM_WIDTH + j * LANES, s * SSM_WIDTH + (j + 1) * LANES)
            uc_ref[:, lanes] = u_scr[j, pl.ds(s, nchunk, stride=SSM_CHUNK), :].astype(BF16)


def _inproj(x, res, sc1, sh1, n1g, w_in, qg, kg, head_sum, rot, cos, sin, seq):
    t, d = x.shape
    tm = min(TOKEN_TILE, seq)
    per_b = seq // tm
    in_width = w_in.shape[1]
    tok = lambda w: pl.BlockSpec((tm, w), lambda i: (i, 0))
    const = lambda a: pl.BlockSpec(a.shape, lambda i: (0,) * a.ndim)
    per_batch = pl.BlockSpec((None, 1, d), lambda i: (i // per_b, 0, 0))
    chunked = pl.BlockSpec((tm // SSM_CHUNK, SSM_CHUNK * SSM_WIDTH), lambda i: (i, 0))
    ins, specs = [x], [tok(d)]
    if res is not None:
        y_prev, g2_prev = res
        ins += [y_prev, g2_prev]
        specs += [tok(d), per_batch]
    ins += [sc1, sh1, n1g, w_in, qg, kg, head_sum, rot, cos, sin]
    specs += [per_batch, per_batch, const(n1g), const(w_in), const(qg), const(kg), const(head_sum), const(rot),
              tok(LANES), tok(LANES)]
    out_shape = [jax.ShapeDtypeStruct((t, ATTN_WIDTH), BF16), jax.ShapeDtypeStruct((t, 2 * KV_WIDTH), BF16),
                 jax.ShapeDtypeStruct((t, 2 * KV_WIDTH), BF16),
                 jax.ShapeDtypeStruct((t // SSM_CHUNK, SSM_CHUNK * SSM_WIDTH), BF16)]
    out_specs = [tok(ATTN_WIDTH), tok(2 * KV_WIDTH), tok(2 * KV_WIDTH), chunked]
    if res is not None:
        out_shape.append(jax.ShapeDtypeStruct((t, d), F32))
        out_specs.append(tok(d))
    assert in_width == ATTN_WIDTH + 2 * KV_WIDTH + SSM_WIDTH
    return pl.pallas_call(
        functools.partial(_inproj_kernel, has_res=res is not None),
        grid=(t // tm,),
        in_specs=specs,
        out_specs=out_specs,
        out_shape=out_shape,
        scratch_shapes=[pltpu.VMEM((SSM_COLS, tm, LANES), F32)],
        compiler_params=_params(("parallel",), VMEM_LIMIT),
        name="inproj",
    )(*ins)


def _attn_kernel(sink_ref, q_ref, kc_ref, kp_ref, vc_ref, vp_ref, bias_ref, g_ref, o_ref):
    nsub = q_ref.shape[0] // ATTN_BLOCK
    kk = jnp.concatenate([kp_ref[...], kc_ref[...]], axis=0)
    vv = jnp.concatenate([vp_ref[...], vc_ref[...]], axis=0)
    low = lax.broadcasted_iota(jnp.int32, (kk.shape[0], KV_WIDTH), 1) < HEAD_DIM
    zero = jnp.zeros((kk.shape[0], KV_WIDTH), BF16)

    def variants(a):
        nat, swp = a[:, :KV_WIDTH], a[:, KV_WIDTH:]
        return {(0, 0): jnp.where(low, nat, zero), (0, 1): jnp.where(low, zero, swp),
                (1, 0): jnp.where(low, swp, zero), (1, 1): jnp.where(low, zero, nat)}

    kvar, vvar = variants(kk), variants(vv)
    band = bias_ref[1]
    first = bias_ref[jnp.minimum(pl.program_id(1), 1)]
    upper = lax.broadcasted_iota(jnp.int32, (2 * ATTN_BLOCK, 1), 0) < ATTN_BLOCK
    for j in range(nsub):
        bias = first if j == 0 else band
        bias2 = jnp.concatenate([bias, bias], axis=0)
        keys = slice(j * ATTN_BLOCK, (j + 2) * ATTN_BLOCK)
        qrows = slice(j * ATTN_BLOCK, (j + 1) * ATTN_BLOCK)
        tiles = [None] * (N_Q_HEADS // 2)
        for kv in range(N_KV_HEADS):
            for half in range(2):
                pairs = (2 * kv, 2 * kv + 1)
                heads = (2 * pairs[0] + half, 2 * pairs[1] + half)
                qs = jnp.concatenate([q_ref[qrows, p * LANES:(p + 1) * LANES] for p in pairs], axis=0)
                s = lax.dot_general(qs, kvar[(kv, half)][keys], (((1,), (1,)), ((), ())),
                                    preferred_element_type=F32) + bias2
                sink = jnp.where(upper, sink_ref[heads[0]], sink_ref[heads[1]]) * LOG2_E
                m = jnp.maximum(jnp.max(s, axis=-1, keepdims=True), sink)
                p = jnp.exp2(s - m)
                denom = jnp.sum(p, axis=-1, keepdims=True) + jnp.exp2(sink - m)
                o = jnp.dot(p.astype(BF16), vvar[(kv, half)][keys], preferred_element_type=F32) * (1.0 / denom)
                for r, pr in enumerate(pairs):
                    part = o[r * ATTN_BLOCK:(r + 1) * ATTN_BLOCK]
                    tiles[pr] = part if tiles[pr] is None else tiles[pr] + part
        a = jnp.concatenate(tiles, axis=1)
        o_ref[qrows, :] = _rms(a, g_ref[...]).astype(BF16)


def _attn_bias():
    qi = np.arange(ATTN_BLOCK)[:, None]
    sj = np.arange(2 * ATTN_BLOCK)[None, :]
    diff = qi + ATTN_BLOCK - sj
    band = (diff >= 0) & (diff < ATTN_BLOCK)
    first = band & (sj >= ATTN_BLOCK)
    return jnp.asarray(np.where(np.stack([first, band]), 0.0, MASK_BIAS).astype(np.float32))


def _attention(q, kx, vx, sink, out_g, bias, batch, seq):
    t = q.shape[0]
    qb = min(ATTN_Q_TILE, seq)
    nsub = qb // ATTN_BLOCK
    nq = seq // qb
    nb = seq // ATTN_BLOCK
    cur = lambda w: pl.BlockSpec((qb, w), lambda b, n, s: (b * nq + n, 0))
    prev = lambda w: pl.BlockSpec((ATTN_BLOCK, w), lambda b, n, s: (b * nb + jnp.maximum(n * nsub - 1, 0), 0))
    grid_spec = pltpu.PrefetchScalarGridSpec(
        num_scalar_prefetch=1,
        grid=(batch, nq),
        in_specs=[cur(ATTN_WIDTH), cur(2 * KV_WIDTH), prev(2 * KV_WIDTH), cur(2 * KV_WIDTH), prev(2 * KV_WIDTH),
                  pl.BlockSpec(bias.shape, lambda b, n, s: (0, 0, 0)),
                  pl.BlockSpec((1, ATTN_WIDTH), lambda b, n, s: (0, 0))],
        out_specs=cur(ATTN_WIDTH),
    )
    return pl.pallas_call(
        _attn_kernel,
        grid_spec=grid_spec,
        out_shape=jax.ShapeDtypeStruct((t, ATTN_WIDTH), BF16),
        compiler_params=_params(("parallel", "arbitrary")),
        name="swa_attention",
    )(sink, q, kx, kx, vx, vx, bias, out_g)


def _s5_prep_kernel(lr_re_ref, lr_im_ref, ldt_ref, bt_re_ref, bt_im_ref, ct_re_ref, ct_im_ref,
                    d_ref, lcol_re_ref, lcol_im_ref, ldtcol_ref, exp_ref, exph_ref, expt_ref, expw_ref,
                    t_ref, w_ref, v_ref, la_ref, lb_ref):
    hi = lax.Precision.HIGHEST
    nl = SSM_CHUNK
    low = lax.broadcasted_iota(jnp.int32, (1, 2 * SSM_STATE), 1) < SSM_STATE
    row_low = lax.broadcasted_iota(jnp.int32, (2 * SSM_STATE, 1), 0) < SSM_STATE
    jcol = lax.broadcasted_iota(jnp.int32, (nl, 1), 0).astype(F32)
    kt_lane = lax.broadcasted_iota(jnp.int32, (SSM_GROUP_CH, nl * SSM_GROUP_CH), 1)
    kt_row = lax.broadcasted_iota(jnp.int32, (SSM_GROUP_CH, nl * SSM_GROUP_CH), 0)

    w_all, v_all, kt_all = [], [], []
    for gm in range(COL_GROUPS):
        dt = jnp.exp(ldt_ref[gm])
        lam_re, lam_im = lr_re_ref[gm], lr_im_ref[gm]
        a_r, th_r = lam_re * dt, lam_im * dt

        er = jnp.exp(jcol * a_r)
        pw_re, pw_im = er * jnp.cos(jcol * th_r), er * jnp.sin(jcol * th_r)

        nr, ni = pw_re[1:2, :] - 1.0, pw_im[1:2, :]
        den = lam_re * lam_re + lam_im * lam_im
        c_re, c_im = (nr * lam_re + ni * lam_im) / den, (ni * lam_re - nr * lam_im) / den
        bt_re, bt_im = bt_re_ref[gm], bt_im_ref[gm]
        bb_re, bb_im = c_re * bt_re - c_im * bt_im, c_re * bt_im + c_im * bt_re

        w_rows = []
        for s in range(nl):
            j = nl - 1 - s
            pr, pi = pw_re[j:j + 1, :], pw_im[j:j + 1, :]
            w_rows.append(jnp.where(low, pr * bb_re - pi * bb_im, pr * bb_im + pi * bb_re))
        w_all.append(w_rows)

        pw_re_t, pw_im_t = pw_re.T, pw_im.T
        pc, ps = _spread(pw_re_t, exp_ref[...]), _spread(pw_im_t, exp_ref[...])
        ct_re, ct_im = _spread(ct_re_ref[gm], exph_ref[...]), _spread(ct_im_ref[gm], exph_ref[...])
        a_re, a_im = ct_re * pc - ct_im * ps, ct_re * ps + ct_im * pc
        a_cat = jnp.where(row_low, a_re, -a_im)
        l1_re, l1_im = pw_re_t[:, 1:2], pw_im_t[:, 1:2]
        v_re, v_im = a_re * l1_re - a_im * l1_im, a_re * l1_im + a_im * l1_re
        v_all.append(jnp.where(row_low, v_re, -v_im))

        kt = jnp.dot(jnp.where(low, bb_re, bb_im), a_cat, precision=hi, preferred_element_type=F32)
        kt_all.append(kt + jnp.where(kt_lane == kt_row, d_ref[gm], 0.0))

    def same_group(shape, row_group, lane_group):
        r = lax.broadcasted_iota(jnp.int32, shape, 0)
        c = lax.broadcasted_iota(jnp.int32, shape, 1)
        return (row_group(r) == lane_group(c)).astype(F32)

    chan_group = lambda i: (i >> 4) & (COL_GROUPS - 1)
    state_group = lambda i: (i >> 6) & (COL_GROUPS - 1)
    over_steps = lambda m: jnp.concatenate([m] * nl, axis=1)

    kt_wide = jnp.dot(jnp.concatenate(kt_all, axis=0).astype(BF16), expt_ref[...], preferred_element_type=F32)
    bd = (kt_wide * over_steps(same_group((LANES, LANES), chan_group, chan_group))).astype(BF16)
    t_ref[0:LANES, :] = bd
    for s in range(1, nl):
        t_ref[s * LANES:(s + 1) * LANES, :] = jnp.concatenate(
            [jnp.zeros((LANES, s * LANES), BF16), bd[:, :CHUNK_LANES - s * LANES]], axis=1)

    w_stack = jnp.concatenate([w_all[gm][s] for s in range(nl) for gm in range(COL_GROUPS)], axis=0)
    w_wide = jnp.dot(w_stack.astype(BF16), expw_ref[...], preferred_element_type=F32)
    w_mask = same_group((LANES, 2 * COL_STATE), chan_group, state_group)
    w_ref[...] = (w_wide.reshape(nl, LANES, 2 * COL_STATE) * w_mask[None]).reshape(CHUNK_LANES, 2 * COL_STATE).astype(BF16)

    v_stack = jnp.concatenate([v_all[gm][half * SSM_STATE:(half + 1) * SSM_STATE, :]
                               for half in range(2) for gm in range(COL_GROUPS)], axis=0)
    v_wide = jnp.dot(v_stack.astype(BF16), expt_ref[...], preferred_element_type=F32)
    v_ref[...] = (v_wide * over_steps(same_group((2 * COL_STATE, LANES), state_group, chan_group))).astype(BF16)

    dtc = jnp.exp(ldtcol_ref[...])
    e16 = jnp.exp(nl * lcol_re_ref[...] * dtc)
    ang = nl * lcol_im_ref[...] * dtc
    la_ref[...] = e16 * jnp.cos(ang)
    lb_ref[...] = e16 * jnp.sin(ang)


def _s5_prep(lam_re, lam_im, b_re, b_im, c_re, c_im, d_skip, log_dt):
    g, p, h, nl = SSM_GROUPS, SSM_STATE, SSM_GROUP_CH, SSM_CHUNK
    cg = COL_GROUPS
    nc = lam_re.shape[0] * SSM_COLS
    col = lambda a: a.reshape((nc, cg) + a.shape[2:])
    dup_row = lambda a: col(jnp.tile(a, (1, 1, 2))[:, :, None, :])
    bt = lambda a: col(jnp.tile(jnp.swapaxes(a, 2, 3), (1, 1, 1, 2)))
    ct = lambda a: col(jnp.tile(jnp.swapaxes(a, 2, 3), (1, 1, 2, 1)))
    d_pad = col(jnp.pad(d_skip.reshape(-1, g, 1, h), ((0, 0), (0, 0), (0, 0), (0, nl * h - h))))
    wide = lambda a: a.reshape(nc, 1, cg * p)
    expand = jnp.asarray(np.tile(np.repeat(np.eye(nl, dtype=np.float32), h, axis=1), (3, 1)), BF16)
    expand_h = jnp.asarray(np.tile(np.eye(h, dtype=np.float32), (3, nl)), BF16)
    exp_t = np.zeros((nl, h, nl, cg, h), np.float32)
    exp_w = np.zeros((2, p, 2, cg, p), np.float32)
    for gm in range(cg):
        exp_t[:, :, :, gm, :] = np.eye(nl * h, dtype=np.float32).reshape(nl, h, nl, h)
        exp_w[:, :, :, gm, :] = np.eye(2 * p, dtype=np.float32).reshape(2, p, 2, p)
    exp_t = jnp.asarray(exp_t.reshape(nl * h, CHUNK_LANES), BF16)
    exp_w = jnp.asarray(exp_w.reshape(2 * p, 2 * COL_STATE), BF16)
    blk = lambda *s: pl.BlockSpec((None,) + s, lambda i: (i,) + (0,) * len(s))
    const = lambda a: pl.BlockSpec(a.shape, lambda i: (0,) * a.ndim)
    lw = nl * h
    return pl.pallas_call(
        _s5_prep_kernel,
        grid=(nc,),
        in_specs=[blk(cg, 1, 2 * p), blk(cg, 1, 2 * p), blk(cg, 1, 1),
                  blk(cg, h, 2 * p), blk(cg, h, 2 * p), blk(cg, 2 * p, h), blk(cg, 2 * p, h), blk(cg, 1, lw),
                  blk(1, cg * p), blk(1, cg * p), blk(1, cg * p), const(expand), const(expand_h), const(exp_t),
                  const(exp_w)],
        out_specs=[blk(CHUNK_LANES, CHUNK_LANES), blk(CHUNK_LANES, 2 * COL_STATE), blk(2 * COL_STATE, CHUNK_LANES),
                   blk(1, COL_STATE), blk(1, COL_STATE)],
        out_shape=[jax.ShapeDtypeStruct((nc, CHUNK_LANES, CHUNK_LANES), BF16),
                   jax.ShapeDtypeStruct((nc, CHUNK_LANES, 2 * COL_STATE), BF16),
                   jax.ShapeDtypeStruct((nc, 2 * COL_STATE, CHUNK_LANES), BF16),
                   jax.ShapeDtypeStruct((nc, 1, COL_STATE), F32), jax.ShapeDtypeStruct((nc, 1, COL_STATE), F32)],
        compiler_params=_params(("parallel",), VMEM_LIMIT),
        name="s5_prep",
    )(dup_row(lam_re), dup_row(lam_im), col(log_dt[:, :, None, None]),
      bt(b_re), bt(b_im), ct(c_re), ct(c_im), d_pad, wide(lam_re), wide(lam_im),
      wide(jnp.repeat(log_dt, p, axis=1)), expand, expand_h, exp_t, exp_w)


def _s5_kernel(*refs, nchunks, nb):
    uc_refs = refs[:SSM_CHUNK]
    t_ref, w_ref, v_ref, la_ref, lb_ref, o_ref, ucat_ref, s_ref, xp_ref = refs[SSM_CHUNK:]

    @pl.when(pl.program_id(1) == 0)
    def _():
        for s in range(SSM_CHUNK):
            ucat_ref[:, s * LANES:(s + 1) * LANES] = uc_refs[s][...]
        s_in = jnp.dot(ucat_ref[...], w_ref[...], preferred_element_type=F32)
        nblk = COL_STATE // LANES
        pitch = s_ref.shape[1] // nb
        for b in range(2 * nblk):
            for q in range(nb):
                s_ref[b, q * pitch:q * pitch + nchunks, :] = s_in[q * nchunks:(q + 1) * nchunks, b * LANES:(b + 1) * LANES]
        lr = [jnp.broadcast_to(la_ref[:, b * LANES:(b + 1) * LANES], (nb, LANES)) for b in range(nblk)]
        li = [jnp.broadcast_to(lb_ref[:, b * LANES:(b + 1) * LANES], (nb, LANES)) for b in range(nblk)]

        def step(c, carry):
            rows = pl.ds(c, nb, stride=pitch)
            out = []
            for b in range(nblk):
                re, im = carry[2 * b], carry[2 * b + 1]
                xp_ref[b, rows, :] = re
                xp_ref[nblk + b, rows, :] = im
                out.append(lr[b] * re - li[b] * im + s_ref[b, rows, :])
                out.append(lr[b] * im + li[b] * re + s_ref[nblk + b, rows, :])
            return tuple(out)

        zero = jnp.zeros((nb, LANES), F32)
        lax.fori_loop(0, nchunks, step, (zero,) * (2 * nblk), unroll=4)

    pitch = xp_ref.shape[1] // nb
    xp = jnp.concatenate(
        [jnp.concatenate([xp_ref[b, q * pitch:q * pitch + nchunks, :] for q in range(nb)], axis=0)
         for b in range(2 * COL_STATE // LANES)], axis=1).astype(BF16)
    inter = jnp.dot(xp, v_ref[...], preferred_element_type=F32)
    for kk in range(SSM_NSPLIT):
        @pl.when(pl.program_id(1) == kk)
        def _():
            live = (kk + 1) * (CHUNK_LANES // SSM_NSPLIT)
            intra = jnp.dot(ucat_ref[:, :live], t_ref[:live, :], preferred_element_type=F32)
            o_ref[...] = (intra + inter).astype(BF16)


def _s5_scan(uc, mats, layer, nchunks, nb):
    rows = uc.shape[0]
    c0 = layer * SSM_COLS
    split = CHUNK_LANES // SSM_NSPLIT
    u_spec = lambda s: pl.BlockSpec((rows, LANES), lambda j, k: (0, SSM_COLS * s + j))
    return pl.pallas_call(
        functools.partial(_s5_kernel, nchunks=nchunks, nb=nb),
        grid=(SSM_COLS, SSM_NSPLIT),
        in_specs=[u_spec(s) for s in range(SSM_CHUNK)] + [
            pl.BlockSpec((None, CHUNK_LANES, split), lambda j, k: (c0 + j, 0, k)),
            pl.BlockSpec((None, CHUNK_LANES, 2 * COL_STATE), lambda j, k: (c0 + j, 0, 0)),
            pl.BlockSpec((None, 2 * COL_STATE, split), lambda j, k: (c0 + j, 0, k)),
            pl.BlockSpec((None, 1, COL_STATE), lambda j, k: (c0 + j, 0, 0)),
            pl.BlockSpec((None, 1, COL_STATE), lambda j, k: (c0 + j, 0, 0))],
        out_specs=pl.BlockSpec((None, rows, split), lambda j, k: (j, 0, k)),
        out_shape=jax.ShapeDtypeStruct((SSM_COLS, rows, CHUNK_LANES), BF16),
        scratch_shapes=[pltpu.VMEM((rows, CHUNK_LANES), BF16),
                        pltpu.VMEM((2 * COL_STATE // LANES, nb * (nchunks + S5_ROW_PAD), LANES), F32),
                        pltpu.VMEM((2 * COL_STATE // LANES, nb * (nchunks + S5_ROW_PAD), LANES), F32)],
        compiler_params=_params(("parallel", "arbitrary"), VMEM_LIMIT),
        name="s5_scan",
    )(*([uc] * SSM_CHUNK), *mats)


def _route(logits, bias):
    m = jnp.max(logits, axis=0, keepdims=True)
    e = jnp.exp(logits - m)
    probs = e / jnp.sum(e, axis=0, keepdims=True)
    sel = probs + bias
    row = lambda a, i: a[i:i + 1, :]
    best_score, best = None, None
    for grp in range(N_EXPERT_GROUPS):
        a, b, c, d = (row(sel, EXPERTS_PER_GROUP * grp + i) for i in range(EXPERTS_PER_GROUP))
        hab, lab, hcd, lcd = jnp.maximum(a, b), jnp.minimum(a, b), jnp.maximum(c, d), jnp.minimum(c, d)
        top1 = jnp.maximum(hab, hcd)
        top2 = jnp.maximum(jnp.maximum(lab, lcd), jnp.minimum(hab, hcd))
        score = top1 + top2
        if grp == 0:
            best_score, best = score, jnp.zeros(score.shape, jnp.int32)
        else:
            better = score > best_score
            best = jnp.where(better, grp, best)
            best_score = jnp.where(better, score, best_score)

    def pick(a, i):
        out = row(a, i)
        for grp in range(1, N_EXPERT_GROUPS):
            out = jnp.where(best == grp, row(a, EXPERTS_PER_GROUP * grp + i), out)
        return out

    s_in = [pick(sel, i) for i in range(EXPERTS_PER_GROUP)]
    p_in = [pick(probs, i) for i in range(EXPERTS_PER_GROUP)]
    neg = jnp.full(s_in[0].shape, -jnp.inf, F32)

    def argmax_first(vals):
        idx, val = jnp.zeros(vals[0].shape, jnp.int32), vals[0]
        for i in range(1, len(vals)):
            better = vals[i] > val
            idx = jnp.where(better, i, idx)
            val = jnp.where(better, vals[i], val)
        return idx

    i1 = argmax_first(s_in)
    i2 = argmax_first([jnp.where(i1 == i, neg, s_in[i]) for i in range(EXPERTS_PER_GROUP)])
    zero = jnp.zeros(p_in[0].shape, F32)
    g1 = sum(jnp.where(i1 == i, p_in[i], zero) for i in range(EXPERTS_PER_GROUP))
    g2 = sum(jnp.where(i2 == i, p_in[i], zero) for i in range(EXPERTS_PER_GROUP))
    tot = g1 + g2
    w1, w2 = g1 / tot, g2 / tot
    first_low = i1 < i2
    low, high = jnp.minimum(i1, i2), jnp.maximum(i1, i2)
    w_low, w_high = jnp.where(first_low, w1, w2), jnp.where(first_low, w2, w1)
    pos = jnp.where(low == 0, high - 1, jnp.where(low == 1, jnp.where(high == 2, 4, 3), 5))
    swap = low == 2
    bucket = best * PAIRS_PER_GROUP + pos
    return jnp.concatenate([jnp.where(swap, w_high, w_low), jnp.where(swap, w_low, w_high)], axis=0), bucket


def _router_logits(w_t, h):
    w_hi = w_t.astype(BF16)
    w_r = w_t - w_hi.astype(F32)
    w_mid = w_r.astype(BF16)
    w_lo = (w_r - w_mid.astype(F32)).astype(BF16)
    h_hi = h.astype(BF16)
    h_lo = (h - h_hi.astype(F32)).astype(BF16)
    dims = (((1,), (1,)), ((), ()))
    a = lax.dot_general(jnp.concatenate([w_hi, w_mid, w_lo], axis=0), h_hi, dims, preferred_element_type=F32)
    b = lax.dot_general(jnp.concatenate([w_hi, w_mid], axis=0), h_lo, dims, preferred_element_type=F32)
    e = w_t.shape[0]
    return a[:e] + a[e:2 * e] + a[2 * e:] + b[:e] + b[e:]


def _post_kernel(x_ref, at_ref, yc_ref, wglu_ref, gs_ref, wo_ref, g1_ref, n2_ref, sc_ref, sh_ref,
                 wrt_ref, rb_ref, x1_ref, h2_ref, cw_ref, gid_ref, y_scr):
    nchunk = y_scr.shape[1] // SSM_CHUNK
    for s in range(SSM_CHUNK):
        for j in range(SSM_COLS):
            y_scr[j, pl.ds(s, nchunk, stride=SSM_CHUNK), :] = yc_ref[j, :, s * LANES:(s + 1) * LANES].astype(F32)
    yg = jax.nn.gelu(jnp.concatenate([y_scr[j] for j in range(SSM_COLS)], axis=1))
    z = yg * jax.nn.sigmoid(jnp.dot(yg.astype(BF16), wglu_ref[...], preferred_element_type=F32))
    zn = _rms(z, gs_ref[...]).astype(BF16)
    o = (jnp.dot(at_ref[...], wo_ref[:ATTN_WIDTH, :], preferred_element_type=F32)
         + jnp.dot(zn, wo_ref[ATTN_WIDTH:, :], preferred_element_type=F32))
    x1 = x_ref[...] + g1_ref[...] * o
    x1_ref[...] = x1
    h2 = _rms(x1, n2_ref[...] * (1.0 + sc_ref[...])) + sh_ref[...]
    h2_ref[...] = h2
    logits = _router_logits(wrt_ref[...], h2)
    cw, bucket = _route(logits, rb_ref[...])
    cw_ref[...] = cw
    gid_ref[...] = bucket


def _post(x, attn, yc, w_glu, ssm_g, w_out, g1, n2g, sc2, sh2, w_router_t, router_bias, seq):
    t, d = x.shape
    tm = min(TOKEN_TILE, seq)
    per_b = seq // tm
    tok = lambda w: pl.BlockSpec((tm, w), lambda i: (i, 0))
    const = lambda a: pl.BlockSpec(a.shape, lambda i: (0,) * a.ndim)
    per_batch = pl.BlockSpec((None, 1, d), lambda i: (i // per_b, 0, 0))
    col = lambda r: pl.BlockSpec((r, tm), lambda i: (0, i))
    chunked = pl.BlockSpec((SSM_COLS, tm // SSM_CHUNK, CHUNK_LANES), lambda i: (0, i, 0))
    return pl.pallas_call(
        _post_kernel,
        grid=(t // tm,),
        in_specs=[tok(d), tok(ATTN_WIDTH), chunked, const(w_glu), const(ssm_g),
                  const(w_out), per_batch, const(n2g), per_batch, per_batch, const(w_router_t), const(router_bias)],
        out_specs=[tok(d), tok(d), col(2), col(1)],
        out_shape=[jax.ShapeDtypeStruct((t, d), F32), jax.ShapeDtypeStruct((t, d), F32),
                   jax.ShapeDtypeStruct((2, t), F32), jax.ShapeDtypeStruct((1, t), jnp.int32)],
        scratch_shapes=[pltpu.VMEM((SSM_COLS, tm, LANES), F32)],
        compiler_params=_params(("parallel",), VMEM_LIMIT),
        name="post_mix",
    )(x, attn, yc, w_glu, ssm_g, w_out, g1, n2g, sc2, sh2, w_router_t, router_bias)


def _moe_kernel(kind_ref, rb_ref, bk_ref, pa_ref, pb_ref, first_ref, cast_ref, cpos_ref, pe_ref, offs_ref,
                x_ref, cw_ref, wg_ref, wu_ref, wd_ref, o_ref, wg_s, wu_s, wd_s):
    s = pl.program_id(0)

    @pl.when(cast_ref[s] == 1)
    def _():
        slot = cpos_ref[s]
        wg_s[slot] = wg_ref[...].astype(BF16)
        wu_s[slot] = wu_ref[...].astype(BF16)
        wd_s[slot] = wd_ref[...].astype(BF16)

    @pl.when(kind_ref[s] == STEP_ITEM)
    def _():
        bucket = bk_ref[s]
        base = rb_ref[s] * MOE_ROWS
        lo_row, hi_row = offs_ref[bucket] - base, offs_ref[bucket + 1] - base
        slots = (pa_ref[s], pb_ref[s])
        is_first = first_ref[s] == 1

        def run(r0, r1, z0, z1):
            rows = r0 + lax.broadcasted_iota(jnp.int32, (r1 - r0, 1), 0)
            cw = jnp.where((rows >= lo_row) & (rows < hi_row), cw_ref[r0:r1, :], 0.0)
            x = x_ref[r0:r1, :].astype(BF16)
            y = None
            for k in range(2):
                gate = jnp.dot(x, wg_s[slots[k]], preferred_element_type=F32)
                up = jnp.dot(x, wu_s[slots[k]], preferred_element_type=F32)
                act = (gate * jax.nn.sigmoid(gate) * up * cw[:, k:k + 1]).astype(BF16)
                yk = jnp.dot(act, wd_s[slots[k]], preferred_element_type=F32)
                y = yk if y is None else y + yk

            @pl.when(is_first)
            def _():
                o_ref[r0:r1, :] = y
                for a, b in ((z0, r0), (r1, z1)):
                    if b > a:
                        o_ref[a:b, :] = jnp.zeros((b - a, o_ref.shape[1]), F32)

            @pl.when(jnp.logical_not(is_first))
            def _():
                o_ref[r0:r1, :] += y

        for b0 in range(0, MOE_ROWS, MOE_SUB):
            b1, mid = b0 + MOE_SUB, b0 + MOE_SUB // 2
            has_rows = (lo_row < b1) & (hi_row > b0)
            needs_lower, needs_upper = has_rows & (lo_row < mid), has_rows & (hi_row > mid)
            pl.when(needs_lower & needs_upper)(lambda: run(b0, b1, b0, b1))
            pl.when(needs_lower & jnp.logical_not(needs_upper))(lambda: run(b0, mid, b0, b1))
            pl.when(jnp.logical_not(needs_lower) & needs_upper)(lambda: run(mid, b1, b0, b1))
            if MOE_ROWS > MOE_SUB:
                @pl.when(jnp.logical_not(has_rows) & is_first)
                def _():
                    o_ref[b0:b1, :] = jnp.zeros((MOE_SUB, o_ref.shape[1]), F32)


def _moe_plan_kernel(offs_ref, kind_ref, rb_ref, bk_ref, pa_ref, pb_ref, first_ref, cast_ref, cpos_ref, pe_ref,
                     irb, ibk, *, n_steps_max):
    i32 = jnp.int32
    ng, epg, ppg = N_EXPERT_GROUPS, EXPERTS_PER_GROUP, PAIRS_PER_GROUP

    shift = MOE_ROWS.bit_length() - 1

    def bucket_body(bk, cnt):
        a, b = offs_ref[bk], offs_ref[bk + 1]
        first_blk = lax.shift_right_logical(a, shift)
        n_blk = jnp.where(b > a, lax.shift_right_logical(b - 1, shift) - first_blk + 1, 0)

        def block_body(j, cnt):
            irb[cnt] = first_blk + j
            ibk[cnt] = bk
            return cnt + 1

        return lax.fori_loop(0, n_blk, block_body, cnt)

    n_items = lax.fori_loop(0, ng * ppg, bucket_body, i32(0))

    def count_body(i, m):
        g = ibk[i] // ppg
        return tuple(m[k] + (g == k).astype(i32) for k in range(ng))

    m = lax.fori_loop(0, n_items, count_body, (i32(0),) * ng)

    def next_group(g):
        nxt = i32(-1)
        for k in range(ng - 1, 0, -1):
            nxt = jnp.where((k > g) & (m[k] > 0), k, nxt)
        return nxt

    def emit(s, kind, rb, bk, pa, pb, first, cast, cpos, pe):
        kind_ref[s], rb_ref[s], bk_ref[s], pa_ref[s], pb_ref[s] = kind, rb, bk, pa, pb
        first_ref[s], cast_ref[s], cpos_ref[s], pe_ref[s] = first, cast, cpos, pe

    def item_body(i, carry):
        s, gcur, parity, q, last_pe, last_rb = carry
        rb, bk = irb[i], ibk[i]
        g, pos = bk // ppg, bk % ppg
        new = g != gcur
        started = gcur >= 0
        loaders = jnp.where(new, jnp.where(started, jnp.maximum(epg - q, 0), epg), 0)
        parity = jnp.where(new & started, 1 - parity, parity)
        q = jnp.where(new, 0, q)
        for j in range(epg):
            on = j >= epg - loaders
            emit(s, STEP_LOAD, rb, bk, 0, 0, 0, 1, parity * epg + j, epg * g + j)
            last_pe = jnp.where(on, epg * g + j, last_pe)
            s = s + on.astype(i32)
        nxt = next_group(g)
        pre = (q < epg) & (nxt >= 0)
        pe = jnp.where(pre, epg * nxt + q, last_pe)
        slot_a, slot_b = i32(PAIR_SLOTS[0][0]), i32(PAIR_SLOTS[0][1])
        for p in range(1, ppg):
            slot_a = jnp.where(pos == p, PAIR_SLOTS[p][0], slot_a)
            slot_b = jnp.where(pos == p, PAIR_SLOTS[p][1], slot_b)
        emit(s, STEP_ITEM, rb, bk, parity * epg + slot_a, parity * epg + slot_b, (rb != last_rb).astype(i32),
             pre.astype(i32), (1 - parity) * epg + q, pe)
        return s + 1, g, parity, q + 1, pe, rb

    s, _, _, _, last_pe, last_rb = lax.fori_loop(
        0, n_items, item_body, (i32(0), i32(-1), i32(0), i32(0), i32(0), i32(-1)))
    last_bk = ibk[jnp.maximum(n_items - 1, 0)]

    def pad_body(s, _):
        emit(s, STEP_PAD, last_rb, last_bk, 0, 0, 0, 0, 0, last_pe)
        return 0

    lax.fori_loop(s, n_steps_max, pad_body, 0)


def _moe_steps(bucket, t):
    i32 = jnp.int32
    nbk = N_EXPERT_GROUPS * PAIRS_PER_GROUP
    order = jnp.argsort(bucket, stable=True).astype(i32)
    counts = jnp.sum((bucket[None, :] == jnp.arange(nbk, dtype=i32)[:, None]).astype(i32), axis=1)
    offs = jnp.concatenate([jnp.zeros((1,), i32), jnp.cumsum(counts).astype(i32)])
    assert MOE_ROWS & (MOE_ROWS - 1) == 0
    n_items_max = t // MOE_ROWS + nbk - 1
    n_steps_max = n_items_max + N_EXPERTS
    smem = pl.BlockSpec(memory_space=pltpu.SMEM)
    tables = pl.pallas_call(
        functools.partial(_moe_plan_kernel, n_steps_max=n_steps_max),
        in_specs=[smem],
        out_specs=[smem] * 9,
        out_shape=[jax.ShapeDtypeStruct((n_steps_max,), i32)] * 9,
        scratch_shapes=[pltpu.SMEM((n_items_max + 1,), i32)] * 2,
        name="moe_plan",
    )(offs)
    return order, (*tables, offs), n_steps_max


def _moe(xs, cws, w_gate, w_up, w_down, layer, tables, n_steps_max):
    t, d = xs.shape
    ff = w_gate.shape[3]
    w_map = lambda s, kind, rb, bk, pa, pb, fi, ca, cp, pe, of: (layer, pe[s], 0, 0)
    row_map = lambda s, kind, rb, *_: (rb[s], 0)
    nres = 2 * EXPERTS_PER_GROUP
    grid_spec = pltpu.PrefetchScalarGridSpec(
        num_scalar_prefetch=len(tables),
        grid=(n_steps_max,),
        in_specs=[pl.BlockSpec((MOE_ROWS, d), row_map), pl.BlockSpec((MOE_ROWS, 2), row_map),
                  pl.BlockSpec((None, None, d, ff), w_map), pl.BlockSpec((None, None, d, ff), w_map),
                  pl.BlockSpec((None, None, ff, d), w_map)],
        out_specs=pl.BlockSpec((MOE_ROWS, d), row_map),
        scratch_shapes=[pltpu.VMEM((nres, d, ff), BF16), pltpu.VMEM((nres, d, ff), BF16),
                        pltpu.VMEM((nres, ff, d), BF16)],
    )
    return pl.pallas_call(
        _moe_kernel,
        grid_spec=grid_spec,
        out_shape=jax.ShapeDtypeStruct((t, d), F32),
        compiler_params=_params(("arbitrary",), MOE_VMEM_LIMIT),
        name="moe_grouped",
    )(*tables, xs, cws, w_gate, w_up, w_down)


def _take_rows(a, idx):
    return a.at[idx].get(mode="promise_in_bounds", unique_indices=True)


def _final_kernel(x_ref, y_ref, g_ref, o_ref):
    o_ref[...] = x_ref[...] + g_ref[...] * y_ref[...]


def _final(x1, y, g2, seq):
    t, d = x1.shape
    tm = min(TOKEN_TILE, seq)
    per_b = seq // tm
    tok = lambda w: pl.BlockSpec((tm, w), lambda i: (i, 0))
    return pl.pallas_call(
        _final_kernel,
        grid=(t // tm,),
        in_specs=[tok(d), tok(d), pl.BlockSpec((None, 1, d), lambda i: (i // per_b, 0, 0))],
        out_specs=tok(d),
        out_shape=jax.ShapeDtypeStruct((t, d), F32),
        compiler_params=_params(("parallel",)),
        name="final_residual",
    )(x1, y, g2)


def kernel(x, c, positions, ada_w, ada_b, norm1_g, w_in, q_norm_g, k_norm_g, attn_sink, lam_re, lam_im, ssm_b_re, ssm_b_im, ssm_c_re, ssm_c_im, ssm_d, ssm_log_dt, w_glu, attn_out_g, ssm_out_g, w_out, norm2_g, w_router, router_bias, w_exp_gate, w_exp_up, w_exp_down):
    batch, seq, d = x.shape
    depth = ada_w.shape[0]
    t = batch * seq
    assert seq % ATTN_BLOCK == 0 and seq % SSM_CHUNK == 0 and t % MOE_ROWS == 0

    mod = _adaln_mod(c, ada_w, ada_b).reshape(depth, 6, batch, 1, d)
    cos, sin = _rope_tables(positions)
    head_sum, rot = _rope_constants()
    bias = _attn_bias()
    w_router_t = w_router.T
    s5_mats = _s5_prep(lam_re, lam_im, ssm_b_re, ssm_b_im, ssm_c_re, ssm_c_im, ssm_d, ssm_log_dt)
    router_bias_col = router_bias.reshape(N_EXPERTS, 1)

    xf = x.reshape(t, d)
    res = None
    for l in range(depth):
        sh1, sc1, g1, sh2, sc2, g2 = (mod[l, j] for j in range(6))
        qg = (jnp.tile(q_norm_g[l], N_Q_HEADS) * (HEAD_DIM ** -0.5 * LOG2_E)).reshape(1, ATTN_WIDTH)
        kg = jnp.tile(k_norm_g[l], N_KV_HEADS).reshape(1, KV_WIDTH)
        outs = _inproj(xf, res, sc1, sh1, norm1_g[l].reshape(1, d), w_in[l].astype(BF16), qg, kg, head_sum, rot,
                       cos, sin, seq)
        if res is None:
            q, kx, vx, uc = outs
        else:
            q, kx, vx, uc, xf = outs
        attn = _attention(q, kx, vx, attn_sink[l], attn_out_g[l].reshape(1, ATTN_WIDTH), bias, batch, seq)
        yc = _s5_scan(uc, s5_mats, l, seq // SSM_CHUNK, batch)
        x1, h2, cw, gid = _post(xf, attn, yc, w_glu[l].astype(BF16), ssm_out_g[l].reshape(1, SSM_WIDTH),
                                 w_out[l].astype(BF16), g1, norm2_g[l].reshape(1, d), sc2, sh2, w_router_t,
                                 router_bias_col, seq)
        order, tables, n_steps_max = _moe_steps(gid.reshape(t), t)
        y_sorted = _moe(_take_rows(h2, order), _take_rows(cw.T, order), w_exp_gate, w_exp_up, w_exp_down, l,
                        tables, n_steps_max)
        y = _take_rows(y_sorted, jnp.argsort(order).astype(jnp.int32))
        xf, res = x1, (y, g2)
    y, g2 = res
    return _final(xf, y, g2, seq).reshape(batch, seq, d)
```

```python
import functools
import math

import numpy as np
import jax
import jax.numpy as jnp
from jax import lax
from jax.experimental import pallas as pl
from jax.experimental.pallas import tpu as pltpu

F32 = jnp.float32
BF16 = jnp.bfloat16

HEAD_DIM = 64
N_Q_HEADS = 8
N_KV_HEADS = 2
Q_PER_KV = N_Q_HEADS // N_KV_HEADS
ATTN_WIDTH = N_Q_HEADS * HEAD_DIM
KV_WIDTH = N_KV_HEADS * HEAD_DIM
ATTN_BLOCK = 128
ATTN_Q_TILE = 2048
ROPE_THETA = 10000.0
ROPE_SLAB = 256
LANES = 128
SSM_GROUP_CH = 16
SSM_GROUPS = 32
SSM_WIDTH = SSM_GROUPS * SSM_GROUP_CH
SSM_STATE = 64
SSM_CHUNK = 16
SSM_COLS = SSM_WIDTH // LANES
COL_GROUPS = LANES // SSM_GROUP_CH
COL_STATE = COL_GROUPS * SSM_STATE
CHUNK_LANES = SSM_CHUNK * LANES
SSM_NSPLIT = 4
S5_ROW_PAD = 8
N_EXPERTS = 16
N_EXPERT_GROUPS = 4
EXPERTS_PER_GROUP = N_EXPERTS // N_EXPERT_GROUPS
PAIRS_PER_GROUP = EXPERTS_PER_GROUP * (EXPERTS_PER_GROUP - 1) // 2
PAIR_SLOTS = ((0, 1), (0, 2), (0, 3), (1, 3), (1, 2), (3, 2))
EPS = 1e-6
LOG2_E = math.log2(math.e)
MASK_BIAS = -1e30

TOKEN_TILE = 1024
MOE_ROWS = 512
MOE_SUB = 256
VMEM_LIMIT = 48 * 1024 * 1024
BIG_VMEM_LIMIT = 56 * 1024 * 1024
STEP_PAD, STEP_LOAD, STEP_ITEM = 0, 1, 2


def _params(sem, vmem=None):
    return pltpu.CompilerParams(dimension_semantics=sem, vmem_limit_bytes=vmem)


def _rms(x, g):
    return x * lax.rsqrt(jnp.mean(x * x, axis=-1, keepdims=True) + EPS) * g


def _mod_kernel(c_ref, w_ref, b_ref, o_ref):
    c = c_ref[...]
    s = c * jax.nn.sigmoid(c)
    o_ref[...] = jnp.dot(s.astype(BF16), w_ref[...].astype(BF16), preferred_element_type=F32) + b_ref[...]


def _adaln_mod(c, ada_w, ada_b):
    depth, d, d6 = ada_w.shape
    nb = c.shape[0]
    n6 = d6 // d
    return pl.pallas_call(
        _mod_kernel,
        grid=(depth, n6),
        in_specs=[pl.BlockSpec((nb, d), lambda l, j: (0, 0)),
                  pl.BlockSpec((None, d, d), lambda l, j: (l, 0, j)),
                  pl.BlockSpec((None, None, 1, d), lambda l, j: (l, j, 0, 0))],
        out_specs=pl.BlockSpec((None, None, nb, d), lambda l, j: (l, j, 0, 0)),
        out_shape=jax.ShapeDtypeStruct((depth, n6, nb, d), F32),
        compiler_params=_params(("arbitrary", "arbitrary"), VMEM_LIMIT),
        name="adaln_mod",
    )(c, ada_w, ada_b.reshape(depth, n6, 1, d))


def _spread(x, expander3):
    hi = x.astype(BF16)
    r1 = x - hi.astype(F32)
    mid = r1.astype(BF16)
    lo = (r1 - mid.astype(F32)).astype(BF16)
    return jnp.dot(jnp.concatenate([hi, mid, lo], axis=1), expander3, preferred_element_type=F32)


def _rope_kernel(pos_ref, freq_ref, lane_ref, quarter_ref, cos_ref, sin_ref):
    per_row = pos_ref.shape[1]
    rows = pos_ref.shape[0]
    pos = _spread(pos_ref[...].astype(F32), lane_ref[...])
    ang = pos * freq_ref[...]
    cos, sin = jnp.cos(ang), jnp.sin(ang)
    for j in range(per_row):
        cos_ref[pl.ds(j, rows, stride=per_row), :] = _spread(cos, quarter_ref[j])
        sin_ref[pl.ds(j, rows, stride=per_row), :] = _spread(sin, quarter_ref[j])


def _rope_tables(positions):
    half = HEAD_DIM // 2
    t = positions.size
    per_row = LANES // half
    rows = t // per_row
    freq = (ROPE_THETA ** (-np.arange(half, dtype=np.float64) / half)).astype(np.float32)
    freq_row = jnp.asarray(np.tile(freq, per_row)[None, :])
    to_quarter = np.repeat(np.eye(per_row, dtype=np.float32), half, axis=1)
    spread = np.zeros((per_row, LANES, LANES), np.float32)
    for j in range(per_row):
        spread[j, j * half:(j + 1) * half, :] = np.tile(np.eye(half, dtype=np.float32), (1, per_row))
    lane_sel = jnp.asarray(np.tile(to_quarter, (3, 1)), BF16)
    quarter_sel = jnp.asarray(np.tile(spread, (1, 3, 1)), BF16)
    blk = min(rows, 512)
    out = pl.BlockSpec((blk * per_row, LANES), lambda i: (i, 0))
    const = lambda a: pl.BlockSpec(a.shape, lambda i: (0,) * a.ndim)
    return pl.pallas_call(
        _rope_kernel,
        grid=(rows // blk,),
        in_specs=[pl.BlockSpec((blk, per_row), lambda i: (i, 0)), const(freq_row), const(lane_sel), const(quarter_sel)],
        out_specs=[out, out],
        out_shape=[jax.ShapeDtypeStruct((t, LANES), F32)] * 2,
        compiler_params=_params(("arbitrary",)),
        name="rope_tables",
    )(positions.reshape(rows, per_row), freq_row, lane_sel, quarter_sel)


def _rope_constants():
    lane = np.arange(ROPE_SLAB)
    head_sum = (lane[:, None] // HEAD_DIM == lane[None, :] // HEAD_DIM).astype(np.float32)
    half = HEAD_DIM // 2
    rot = np.zeros((ROPE_SLAB, ROPE_SLAB), np.float32)
    for d in range(ROPE_SLAB):
        if d % HEAD_DIM < half:
            rot[d + half, d] = -1.0
        else:
            rot[d - half, d] = 1.0
    return jnp.asarray(head_sum, BF16), jnp.asarray(rot, BF16)


def _inproj_kernel(*refs, has_res):
    if has_res:
        (x_ref, y_ref, g2_ref, sc_ref, sh_ref, n1_ref, w_ref, qg_ref, kg_ref, hs_ref, rot_ref,
         cos_ref, sin_ref, q_ref, k_ref, v_ref, uc_ref, xo_ref, u_scr, w_s) = refs
        x = x_ref[...] + g2_ref[...] * y_ref[...]
        xo_ref[...] = x
    else:
        (x_ref, sc_ref, sh_ref, n1_ref, w_ref, qg_ref, kg_ref, hs_ref, rot_ref,
         cos_ref, sin_ref, q_ref, k_ref, v_ref, uc_ref, u_scr, w_s) = refs
        x = x_ref[...]

    @pl.when(pl.program_id(0) == 0)
    def _():
        w_s[...] = w_ref[...].astype(BF16)

    h = _rms(x, n1_ref[...] * (1.0 + sc_ref[...])) + sh_ref[...]
    proj = jnp.dot(h.astype(BF16), w_s[...], preferred_element_type=F32)
    q = proj[:, :ATTN_WIDTH]
    k = proj[:, ATTN_WIDTH:ATTN_WIDTH + KV_WIDTH]
    v = proj[:, ATTN_WIDTH + KV_WIDTH:ATTN_WIDTH + 2 * KV_WIDTH]
    cos = cos_ref[...]
    sin = sin_ref[...]
    reps = ATTN_WIDTH // LANES
    cos_q = jnp.concatenate([cos] * reps, axis=1)
    sin_q = jnp.concatenate([sin] * reps, axis=1)

    def head_norm_rope(t, gain, c, s):
        outs = []
        for lo in range(0, t.shape[1], ROPE_SLAB):
            wd = min(ROPE_SLAB, t.shape[1] - lo)
            ts, lanes = t[:, lo:lo + wd], slice(lo, lo + wd)
            ssq = jnp.dot((ts * ts).astype(BF16), hs_ref[:wd, :wd], preferred_element_type=F32)
            tn = (ts * lax.rsqrt(ssq * (1.0 / HEAD_DIM) + EPS) * gain[:, lanes]).astype(BF16)
            tr = jnp.dot(tn, rot_ref[:wd, :wd], preferred_element_type=F32)
            outs.append(tn.astype(F32) * c[:, lanes] + tr * s[:, lanes])
        return outs[0] if len(outs) == 1 else jnp.concatenate(outs, axis=1)

    qo = head_norm_rope(q, qg_ref[...], cos_q, sin_q)
    ko = head_norm_rope(k, kg_ref[...], cos, sin)
    q_ref[...] = qo.astype(BF16)
    k_ref[...] = jnp.concatenate([ko, pltpu.roll(ko, HEAD_DIM, axis=1)], axis=1).astype(BF16)
    v_ref[...] = jnp.concatenate([v, pltpu.roll(v, HEAD_DIM, axis=1)], axis=1).astype(BF16)
    u0 = ATTN_WIDTH + 2 * KV_WIDTH
    nchunk = u_scr.shape[1] // SSM_CHUNK
    for j in range(SSM_COLS):
        u_scr[j] = proj[:, u0 + j * LANES:u0 + (j + 1) * LANES]
    for s in range(SSM_CHUNK):
        for j in range(SSM_COLS):
            lanes = slice(s * SSM_WIDTH + j * LANES, s * SSM_WIDTH + (j + 1) * LANES)
            uc_ref[:, lanes] = u_scr[j, pl.ds(s, nchunk, stride=SSM_CHUNK), :].astype(BF16)


def _inproj(x, res, sc1, sh1, n1g, w_in, layer, qg, kg, head_sum, rot, cos, sin, seq):
    t, d = x.shape
    tm = min(TOKEN_TILE, seq)
    per_b = seq // tm
    in_width = w_in.shape[2]
    tok = lambda w: pl.BlockSpec((tm, w), lambda i: (i, 0))
    const = lambda a: pl.BlockSpec(a.shape, lambda i: (0,) * a.ndim)
    per_batch = pl.BlockSpec((None, 1, d), lambda i: (i // per_b, 0, 0))
    chunked = pl.BlockSpec((tm // SSM_CHUNK, SSM_CHUNK * SSM_WIDTH), lambda i: (i, 0))
    ins, specs = [x], [tok(d)]
    if res is not None:
        y_prev, g2_prev = res
        ins += [y_prev, g2_prev]
        specs += [tok(d), per_batch]
    ins += [sc1, sh1, n1g, w_in, qg, kg, head_sum, rot, cos, sin]
    specs += [per_batch, per_batch, const(n1g), pl.BlockSpec((None, d, in_width), lambda i: (layer, 0, 0)),
              const(qg), const(kg), const(head_sum), const(rot), tok(LANES), tok(LANES)]
    out_shape = [jax.ShapeDtypeStruct((t, ATTN_WIDTH), BF16), jax.ShapeDtypeStruct((t, 2 * KV_WIDTH), BF16),
                 jax.ShapeDtypeStruct((t, 2 * KV_WIDTH), BF16),
                 jax.ShapeDtypeStruct((t // SSM_CHUNK, SSM_CHUNK * SSM_WIDTH), BF16)]
    out_specs = [tok(ATTN_WIDTH), tok(2 * KV_WIDTH), tok(2 * KV_WIDTH), chunked]
    if res is not None:
        out_shape.append(jax.ShapeDtypeStruct((t, d), F32))
        out_specs.append(tok(d))
    assert in_width == ATTN_WIDTH + 2 * KV_WIDTH + SSM_WIDTH
    return pl.pallas_call(
        functools.partial(_inproj_kernel, has_res=res is not None),
        grid=(t // tm,),
        in_specs=specs,
        out_specs=out_specs,
        out_shape=out_shape,
        scratch_shapes=[pltpu.VMEM((SSM_COLS, tm, LANES), F32), pltpu.VMEM((d, in_width), BF16)],
        compiler_params=_params(("arbitrary",), BIG_VMEM_LIMIT),
        name="inproj",
    )(*ins)


def _attn_kernel(sink_ref, q_ref, kc_ref, kp_ref, vc_ref, vp_ref, bias_ref, g_ref, o_ref):
    nsub = q_ref.shape[0] // ATTN_BLOCK
    kk = jnp.concatenate([kp_ref[...], kc_ref[...]], axis=0)
    vv = jnp.concatenate([vp_ref[...], vc_ref[...]], axis=0)
    low = lax.broadcasted_iota(jnp.int32, (kk.shape[0], KV_WIDTH), 1) < HEAD_DIM
    zero = jnp.zeros((kk.shape[0], KV_WIDTH), BF16)

    def variants(a):
        nat, swp = a[:, :KV_WIDTH], a[:, KV_WIDTH:]
        return {(0, 0): jnp.where(low, nat, zero), (0, 1): jnp.where(low, zero, swp),
                (1, 0): jnp.where(low, swp, zero), (1, 1): jnp.where(low, zero, nat)}

    kvar, vvar = variants(kk), variants(vv)
    band = bias_ref[1]
    first = bias_ref[jnp.minimum(pl.program_id(1), 1)]
    upper = lax.broadcasted_iota(jnp.int32, (2 * ATTN_BLOCK, 1), 0) < ATTN_BLOCK
    for j in range(nsub):
        bias = first if j == 0 else band
        bias2 = jnp.concatenate([bias, bias], axis=0)
        keys = slice(j * ATTN_BLOCK, (j + 2) * ATTN_BLOCK)
        qrows = slice(j * ATTN_BLOCK, (j + 1) * ATTN_BLOCK)
        tiles = [None] * (N_Q_HEADS // 2)
        for kv in range(N_KV_HEADS):
            for half in range(2):
                pairs = (2 * kv, 2 * kv + 1)
                heads = (2 * pairs[0] + half, 2 * pairs[1] + half)
                qs = jnp.concatenate([q_ref[qrows, p * LANES:(p + 1) * LANES] for p in pairs], axis=0)
                s = lax.dot_general(qs, kvar[(kv, half)][keys], (((1,), (1,)), ((), ())),
                                    preferred_element_type=F32) + bias2
                sink = jnp.where(upper, sink_ref[heads[0]], sink_ref[heads[1]]) * LOG2_E
                m = jnp.maximum(jnp.max(s, axis=-1, keepdims=True), sink)
                p = jnp.exp2(s - m)
                denom = jnp.sum(p, axis=-1, keepdims=True) + jnp.exp2(sink - m)
                o = jnp.dot(p.astype(BF16), vvar[(kv, half)][keys], preferred_element_type=F32) * (1.0 / denom)
                for r, pr in enumerate(pairs):
                    part = o[r * ATTN_BLOCK:(r + 1) * ATTN_BLOCK]
                    tiles[pr] = part if tiles[pr] is None else tiles[pr] + part
        a = jnp.concatenate(tiles, axis=1)
        o_ref[qrows, :] = _rms(a, g_ref[...]).astype(BF16)


def _attn_bias():
    qi = np.arange(ATTN_BLOCK)[:, None]
    sj = np.arange(2 * ATTN_BLOCK)[None, :]
    diff = qi + ATTN_BLOCK - sj
    band = (diff >= 0) & (diff < ATTN_BLOCK)
    first = band & (sj >= ATTN_BLOCK)
    return jnp.asarray(np.where(np.stack([first, band]), 0.0, MASK_BIAS).astype(np.float32))


def _attention(q, kx, vx, sink, out_g, bias, batch, seq):
    t = q.shape[0]
    qb = min(ATTN_Q_TILE, seq)
    nsub = qb // ATTN_BLOCK
    nq = seq // qb
    nb = seq // ATTN_BLOCK
    cur = lambda w: pl.BlockSpec((qb, w), lambda b, n, s: (b * nq + n, 0))
    prev = lambda w: pl.BlockSpec((ATTN_BLOCK, w), lambda b, n, s: (b * nb + jnp.maximum(n * nsub - 1, 0), 0))
    grid_spec = pltpu.PrefetchScalarGridSpec(
        num_scalar_prefetch=1,
        grid=(batch, nq),
        in_specs=[cur(ATTN_WIDTH), cur(2 * KV_WIDTH), prev(2 * KV_WIDTH), cur(2 * KV_WIDTH), prev(2 * KV_WIDTH),
                  pl.BlockSpec(bias.shape, lambda b, n, s: (0, 0, 0)),
                  pl.BlockSpec((1, ATTN_WIDTH), lambda b, n, s: (0, 0))],
        out_specs=cur(ATTN_WIDTH),
    )
    return pl.pallas_call(
        _attn_kernel,
        grid_spec=grid_spec,
        out_shape=jax.ShapeDtypeStruct((t, ATTN_WIDTH), BF16),
        compiler_params=_params(("parallel", "arbitrary")),
        name="swa_attention",
    )(sink, q, kx, kx, vx, vx, bias, out_g)


def _s5_prep_kernel(lr_re_ref, lr_im_ref, ldt_ref, bt_re_ref, bt_im_ref, ct_re_ref, ct_im_ref,
                    d_ref, lcol_re_ref, lcol_im_ref, ldtcol_ref, exp_ref, exph_ref, expt_ref, expw_ref,
                    t_ref, w_ref, v_ref, la_ref, lb_ref):
    hi = lax.Precision.HIGHEST
    nl = SSM_CHUNK
    low = lax.broadcasted_iota(jnp.int32, (1, 2 * SSM_STATE), 1) < SSM_STATE
    row_low = lax.broadcasted_iota(jnp.int32, (2 * SSM_STATE, 1), 0) < SSM_STATE
    jcol = lax.broadcasted_iota(jnp.int32, (nl, 1), 0).astype(F32)
    kt_lane = lax.broadcasted_iota(jnp.int32, (SSM_GROUP_CH, nl * SSM_GROUP_CH), 1)
    kt_row = lax.broadcasted_iota(jnp.int32, (SSM_GROUP_CH, nl * SSM_GROUP_CH), 0)

    w_all, v_all, kt_all = [], [], []
    for gm in range(COL_GROUPS):
        dt = jnp.exp(ldt_ref[gm])
        lam_re, lam_im = lr_re_ref[gm], lr_im_ref[gm]
        a_r, th_r = lam_re * dt, lam_im * dt

        er = jnp.exp(jcol * a_r)
        pw_re, pw_im = er * jnp.cos(jcol * th_r), er * jnp.sin(jcol * th_r)

        nr, ni = pw_re[1:2, :] - 1.0, pw_im[1:2, :]
        den = lam_re * lam_re + lam_im * lam_im
        c_re, c_im = (nr * lam_re + ni * lam_im) / den, (ni * lam_re - nr * lam_im) / den
        bt_re, bt_im = bt_re_ref[gm], bt_im_ref[gm]
        bb_re, bb_im = c_re * bt_re - c_im * bt_im, c_re * bt_im + c_im * bt_re

        w_rows = []
        for s in range(nl):
            j = nl - 1 - s
            pr, pi = pw_re[j:j + 1, :], pw_im[j:j + 1, :]
            w_rows.append(jnp.where(low, pr * bb_re - pi * bb_im, pr * bb_im + pi * bb_re))
        w_all.append(w_rows)

        pw_re_t, pw_im_t = pw_re.T, pw_im.T
        pc, ps = _spread(pw_re_t, exp_ref[...]), _spread(pw_im_t, exp_ref[...])
        ct_re, ct_im = _spread(ct_re_ref[gm], exph_ref[...]), _spread(ct_im_ref[gm], exph_ref[...])
        a_re, a_im = ct_re * pc - ct_im * ps, ct_re * ps + ct_im * pc
        a_cat = jnp.where(row_low, a_re, -a_im)
        l1_re, l1_im = pw_re_t[:, 1:2], pw_im_t[:, 1:2]
        v_re, v_im = a_re * l1_re - a_im * l1_im, a_re * l1_im + a_im * l1_re
        v_all.append(jnp.where(row_low, v_re, -v_im))

        kt = jnp.dot(jnp.where(low, bb_re, bb_im), a_cat, precision=hi, preferred_element_type=F32)
        kt_all.append(kt + jnp.where(kt_lane == kt_row, d_ref[gm], 0.0))

    def same_group(shape, row_group, lane_group):
        r = lax.broadcasted_iota(jnp.int32, shape, 0)
        c = lax.broadcasted_iota(jnp.int32, shape, 1)
        return (row_group(r) == lane_group(c)).astype(F32)

    chan_group = lambda i: (i >> 4) & (COL_GROUPS - 1)
    state_group = lambda i: (i >> 6) & (COL_GROUPS - 1)
    over_steps = lambda m: jnp.concatenate([m] * nl, axis=1)

    kt_wide = jnp.dot(jnp.concatenate(kt_all, axis=0).astype(BF16), expt_ref[...], preferred_element_type=F32)
    bd = (kt_wide * over_steps(same_group((LANES, LANES), chan_group, chan_group))).astype(BF16)
    t_ref[0:LANES, :] = bd
    for s in range(1, nl):
        t_ref[s * LANES:(s + 1) * LANES, :] = jnp.concatenate(
            [jnp.zeros((LANES, s * LANES), BF16), bd[:, :CHUNK_LANES - s * LANES]], axis=1)

    w_stack = jnp.concatenate([w_all[gm][s] for s in range(nl) for gm in range(COL_GROUPS)], axis=0)
    w_wide = jnp.dot(w_stack.astype(BF16), expw_ref[...], preferred_element_type=F32)
    w_mask = same_group((LANES, 2 * COL_STATE), chan_group, state_group)
    w_ref[...] = (w_wide.reshape(nl, LANES, 2 * COL_STATE) * w_mask[None]).reshape(CHUNK_LANES, 2 * COL_STATE).astype(BF16)

    v_stack = jnp.concatenate([v_all[gm][half * SSM_STATE:(half + 1) * SSM_STATE, :]
                               for half in range(2) for gm in range(COL_GROUPS)], axis=0)
    v_wide = jnp.dot(v_stack.astype(BF16), expt_ref[...], preferred_element_type=F32)
    v_ref[...] = (v_wide * over_steps(same_group((2 * COL_STATE, LANES), state_group, chan_group))).astype(BF16)

    dtc = jnp.exp(ldtcol_ref[...])
    e16 = jnp.exp(nl * lcol_re_ref[...] * dtc)
    ang = nl * lcol_im_ref[...] * dtc
    la_ref[...] = e16 * jnp.cos(ang)
    lb_ref[...] = e16 * jnp.sin(ang)


def _s5_prep(lam_re, lam_im, b_re, b_im, c_re, c_im, d_skip, log_dt):
    g, p, h, nl = SSM_GROUPS, SSM_STATE, SSM_GROUP_CH, SSM_CHUNK
    cg = COL_GROUPS
    nc = lam_re.shape[0] * SSM_COLS
    col = lambda a: a.reshape((nc, cg) + a.shape[2:])
    dup_row = lambda a: col(jnp.tile(a, (1, 1, 2))[:, :, None, :])
    bt = lambda a: col(jnp.tile(jnp.swapaxes(a, 2, 3), (1, 1, 1, 2)))
    ct = lambda a: col(jnp.tile(jnp.swapaxes(a, 2, 3), (1, 1, 2, 1)))
    d_pad = col(jnp.pad(d_skip.reshape(-1, g, 1, h), ((0, 0), (0, 0), (0, 0), (0, nl * h - h))))
    wide = lambda a: a.reshape(nc, 1, cg * p)
    expand = jnp.asarray(np.tile(np.repeat(np.eye(nl, dtype=np.float32), h, axis=1), (3, 1)), BF16)
    expand_h = jnp.asarray(np.tile(np.eye(h, dtype=np.float32), (3, nl)), BF16)
    exp_t = np.zeros((nl, h, nl, cg, h), np.float32)
    exp_w = np.zeros((2, p, 2, cg, p), np.float32)
    for gm in range(cg):
        exp_t[:, :, :, gm, :] = np.eye(nl * h, dtype=np.float32).reshape(nl, h, nl, h)
        exp_w[:, :, :, gm, :] = np.eye(2 * p, dtype=np.float32).reshape(2, p, 2, p)
    exp_t = jnp.asarray(exp_t.reshape(nl * h, CHUNK_LANES), BF16)
    exp_w = jnp.asarray(exp_w.reshape(2 * p, 2 * COL_STATE), BF16)
    blk = lambda *s: pl.BlockSpec((None,) + s, lambda i: (i,) + (0,) * len(s))
    const = lambda a: pl.BlockSpec(a.shape, lambda i: (0,) * a.ndim)
    lw = nl * h
    return pl.pallas_call(
        _s5_prep_kernel,
        grid=(nc,),
        in_specs=[blk(cg, 1, 2 * p), blk(cg, 1, 2 * p), blk(cg, 1, 1),
                  blk(cg, h, 2 * p), blk(cg, h, 2 * p), blk(cg, 2 * p, h), blk(cg, 2 * p, h), blk(cg, 1, lw),
                  blk(1, cg * p), blk(1, cg * p), blk(1, cg * p), const(expand), const(expand_h), const(exp_t),
                  const(exp_w)],
        out_specs=[blk(CHUNK_LANES, CHUNK_LANES), blk(CHUNK_LANES, 2 * COL_STATE), blk(2 * COL_STATE, CHUNK_LANES),
                   blk(1, COL_STATE), blk(1, COL_STATE)],
        out_shape=[jax.ShapeDtypeStruct((nc, CHUNK_LANES, CHUNK_LANES), BF16),
                   jax.ShapeDtypeStruct((nc, CHUNK_LANES, 2 * COL_STATE), BF16),
                   jax.ShapeDtypeStruct((nc, 2 * COL_STATE, CHUNK_LANES), BF16),
                   jax.ShapeDtypeStruct((nc, 1, COL_STATE), F32), jax.ShapeDtypeStruct((nc, 1, COL_STATE), F32)],
        compiler_params=_params(("parallel",), VMEM_LIMIT),
        name="s5_prep",
    )(dup_row(lam_re), dup_row(lam_im), col(log_dt[:, :, None, None]),
      bt(b_re), bt(b_im), ct(c_re), ct(c_im), d_pad, wide(lam_re), wide(lam_im),
      wide(jnp.repeat(log_dt, p, axis=1)), expand, expand_h, exp_t, exp_w)


def _s5_kernel(*refs, nchunks, nb):
    uc_refs = refs[:SSM_CHUNK]
    t_ref, w_ref, v_ref, la_ref, lb_ref, o_ref, ucat_ref, s_ref, xp_ref = refs[SSM_CHUNK:]

    @pl.when(pl.program_id(1) == 0)
    def _():
        for s in range(SSM_CHUNK):
            ucat_ref[:, s * LANES:(s + 1) * LANES] = uc_refs[s][...]
        s_in = jnp.dot(ucat_ref[...], w_ref[...], preferred_element_type=F32)
        nblk = COL_STATE // LANES
        pitch = s_ref.shape[1] // nb
        for b in range(2 * nblk):
            for q in range(nb):
                s_ref[b, q * pitch:q * pitch + nchunks, :] = s_in[q * nchunks:(q + 1) * nchunks, b * LANES:(b + 1) * LANES]
        lr = [jnp.broadcast_to(la_ref[:, b * LANES:(b + 1) * LANES], (nb, LANES)) for b in range(nblk)]
        li = [jnp.broadcast_to(lb_ref[:, b * LANES:(b + 1) * LANES], (nb, LANES)) for b in range(nblk)]

        def step(c, carry):
            rows = pl.ds(c, nb, stride=pitch)
            out = []
            for b in range(nblk):
                re, im = carry[2 * b], carry[2 * b + 1]
                xp_ref[b, rows, :] = re
                xp_ref[nblk + b, rows, :] = im
                out.append(lr[b] * re - li[b] * im + s_ref[b, rows, :])
                out.append(lr[b] * im + li[b] * re + s_ref[nblk + b, rows, :])
            return tuple(out)

        zero = jnp.zeros((nb, LANES), F32)
        lax.fori_loop(0, nchunks, step, (zero,) * (2 * nblk), unroll=4)

    pitch = xp_ref.shape[1] // nb
    xp = jnp.concatenate(
        [jnp.concatenate([xp_ref[b, q * pitch:q * pitch + nchunks, :] for q in range(nb)], axis=0)
         for b in range(2 * COL_STATE // LANES)], axis=1).astype(BF16)
    inter = jnp.dot(xp, v_ref[...], preferred_element_type=F32)
    for kk in range(SSM_NSPLIT):
        @pl.when(pl.program_id(1) == kk)
        def _():
            live = (kk + 1) * (CHUNK_LANES // SSM_NSPLIT)
            intra = jnp.dot(ucat_ref[:, :live], t_ref[:live, :], preferred_element_type=F32)
            o_ref[...] = (intra + inter).astype(BF16)


def _s5_scan(uc, mats, layer, nchunks, nb):
    rows = uc.shape[0]
    c0 = layer * SSM_COLS
    split = CHUNK_LANES // SSM_NSPLIT
    u_spec = lambda s: pl.BlockSpec((rows, LANES), lambda j, k: (0, SSM_COLS * s + j))
    return pl.pallas_call(
        functools.partial(_s5_kernel, nchunks=nchunks, nb=nb),
        grid=(SSM_COLS, SSM_NSPLIT),
        in_specs=[u_spec(s) for s in range(SSM_CHUNK)] + [
            pl.BlockSpec((None, CHUNK_LANES, split), lambda j, k: (c0 + j, 0, k)),
            pl.BlockSpec((None, CHUNK_LANES, 2 * COL_STATE), lambda j, k: (c0 + j, 0, 0)),
            pl.BlockSpec((None, 2 * COL_STATE, split), lambda j, k: (c0 + j, 0, k)),
            pl.BlockSpec((None, 1, COL_STATE), lambda j, k: (c0 + j, 0, 0)),
            pl.BlockSpec((None, 1, COL_STATE), lambda j, k: (c0 + j, 0, 0))],
        out_specs=pl.BlockSpec((None, rows, split), lambda j, k: (j, 0, k)),
        out_shape=jax.ShapeDtypeStruct((SSM_COLS, rows, CHUNK_LANES), BF16),
        scratch_shapes=[pltpu.VMEM((rows, CHUNK_LANES), BF16),
                        pltpu.VMEM((2 * COL_STATE // LANES, nb * (nchunks + S5_ROW_PAD), LANES), F32),
                        pltpu.VMEM((2 * COL_STATE // LANES, nb * (nchunks + S5_ROW_PAD), LANES), F32)],
        compiler_params=_params(("parallel", "arbitrary"), VMEM_LIMIT),
        name="s5_scan",
    )(*([uc] * SSM_CHUNK), *mats)


def _route(logits, bias):
    m = jnp.max(logits, axis=0, keepdims=True)
    e = jnp.exp(logits - m)
    probs = e / jnp.sum(e, axis=0, keepdims=True)
    sel = probs + bias
    row = lambda a, i: a[i:i + 1, :]
    best_score, best = None, None
    for grp in range(N_EXPERT_GROUPS):
        a, b, c, d = (row(sel, EXPERTS_PER_GROUP * grp + i) for i in range(EXPERTS_PER_GROUP))
        hab, lab, hcd, lcd = jnp.maximum(a, b), jnp.minimum(a, b), jnp.maximum(c, d), jnp.minimum(c, d)
        top1 = jnp.maximum(hab, hcd)
        top2 = jnp.maximum(jnp.maximum(lab, lcd), jnp.minimum(hab, hcd))
        score = top1 + top2
        if grp == 0:
            best_score, best = score, jnp.zeros(score.shape, jnp.int32)
        else:
            better = score > best_score
            best = jnp.where(better, grp, best)
            best_score = jnp.where(better, score, best_score)

    def pick(a, i):
        out = row(a, i)
        for grp in range(1, N_EXPERT_GROUPS):
            out = jnp.where(best == grp, row(a, EXPERTS_PER_GROUP * grp + i), out)
        return out

    s_in = [pick(sel, i) for i in range(EXPERTS_PER_GROUP)]
    p_in = [pick(probs, i) for i in range(EXPERTS_PER_GROUP)]
    neg = jnp.full(s_in[0].shape, -jnp.inf, F32)

    def argmax_first(vals):
        idx, val = jnp.zeros(vals[0].shape, jnp.int32), vals[0]
        for i in range(1, len(vals)):
            better = vals[i] > val
            idx = jnp.where(better, i, idx)
            val = jnp.where(better, vals[i], val)
        return idx

    i1 = argmax_first(s_in)
    i2 = argmax_first([jnp.where(i1 == i, neg, s_in[i]) for i in range(EXPERTS_PER_GROUP)])
    zero = jnp.zeros(p_in[0].shape, F32)
    g1 = sum(jnp.where(i1 == i, p_in[i], zero) for i in range(EXPERTS_PER_GROUP))
    g2 = sum(jnp.where(i2 == i, p_in[i], zero) for i in range(EXPERTS_PER_GROUP))
    tot = g1 + g2
    w1, w2 = g1 / tot, g2 / tot
    first_low = i1 < i2
    low, high = jnp.minimum(i1, i2), jnp.maximum(i1, i2)
    w_low, w_high = jnp.where(first_low, w1, w2), jnp.where(first_low, w2, w1)
    pos = jnp.where(low == 0, high - 1, jnp.where(low == 1, jnp.where(high == 2, 4, 3), 5))
    swap = low == 2
    bucket = best * PAIRS_PER_GROUP + pos
    return jnp.concatenate([jnp.where(swap, w_high, w_low), jnp.where(swap, w_low, w_high)], axis=0), bucket


def _router_logits(w_t, h):
    w_hi = w_t.astype(BF16)
    w_r = w_t - w_hi.astype(F32)
    w_mid = w_r.astype(BF16)
    w_lo = (w_r - w_mid.astype(F32)).astype(BF16)
    h_hi = h.astype(BF16)
    h_lo = (h - h_hi.astype(F32)).astype(BF16)
    dims = (((1,), (1,)), ((), ()))
    a = lax.dot_general(jnp.concatenate([w_hi, w_mid, w_lo], axis=0), h_hi, dims, preferred_element_type=F32)
    b = lax.dot_general(jnp.concatenate([w_hi, w_mid], axis=0), h_lo, dims, preferred_element_type=F32)
    e = w_t.shape[0]
    return a[:e] + a[e:2 * e] + a[2 * e:] + b[:e] + b[e:]


def _post_kernel(x_ref, at_ref, yc_ref, wglu_ref, gs_ref, wo_ref, g1_ref, n2_ref, sc_ref, sh_ref,
                 wrt_ref, rb_ref, x1_ref, h2_ref, cw_ref, gid_ref, y_scr, wglu_s, wo_s):
    @pl.when(pl.program_id(0) == 0)
    def _():
        wglu_s[...] = wglu_ref[...].astype(BF16)
        wo_s[...] = wo_ref[...].astype(BF16)

    nchunk = y_scr.shape[1] // SSM_CHUNK
    for s in range(SSM_CHUNK):
        for j in range(SSM_COLS):
            y_scr[j, pl.ds(s, nchunk, stride=SSM_CHUNK), :] = yc_ref[j, :, s * LANES:(s + 1) * LANES].astype(F32)
    yg = jax.nn.gelu(jnp.concatenate([y_scr[j] for j in range(SSM_COLS)], axis=1))
    z = yg * jax.nn.sigmoid(jnp.dot(yg.astype(BF16), wglu_s[...], preferred_element_type=F32))
    zn = _rms(z, gs_ref[...]).astype(BF16)
    o = (jnp.dot(at_ref[...], wo_s[:ATTN_WIDTH, :], preferred_element_type=F32)
         + jnp.dot(zn, wo_s[ATTN_WIDTH:, :], preferred_element_type=F32))
    x1 = x_ref[...] + g1_ref[...] * o
    x1_ref[...] = x1
    h2 = _rms(x1, n2_ref[...] * (1.0 + sc_ref[...])) + sh_ref[...]
    h2_ref[...] = h2
    logits = _router_logits(wrt_ref[...], h2)
    cw, bucket = _route(logits, rb_ref[...])
    cw_ref[...] = cw
    gid_ref[...] = bucket


def _post(x, attn, yc, w_glu, ssm_g, w_out, layer, g1, n2g, sc2, sh2, w_router_t, router_bias, seq):
    t, d = x.shape
    tm = min(TOKEN_TILE, seq)
    per_b = seq // tm
    tok = lambda w: pl.BlockSpec((tm, w), lambda i: (i, 0))
    const = lambda a: pl.BlockSpec(a.shape, lambda i: (0,) * a.ndim)
    per_batch = pl.BlockSpec((None, 1, d), lambda i: (i // per_b, 0, 0))
    col = lambda r: pl.BlockSpec((r, tm), lambda i: (0, i))
    of_layer = lambda a: pl.BlockSpec((None,) + a.shape[1:], lambda i: (layer, 0, 0))
    chunked = pl.BlockSpec((SSM_COLS, tm // SSM_CHUNK, CHUNK_LANES), lambda i: (0, i, 0))
    return pl.pallas_call(
        _post_kernel,
        grid=(t // tm,),
        in_specs=[tok(d), tok(ATTN_WIDTH), chunked, of_layer(w_glu), const(ssm_g),
                  of_layer(w_out), per_batch, const(n2g), per_batch, per_batch, const(w_router_t), const(router_bias)],
        out_specs=[tok(d), tok(d), col(2), col(1)],
        out_shape=[jax.ShapeDtypeStruct((t, d), F32), jax.ShapeDtypeStruct((t, d), F32),
                   jax.ShapeDtypeStruct((2, t), F32), jax.ShapeDtypeStruct((1, t), jnp.int32)],
        scratch_shapes=[pltpu.VMEM((SSM_COLS, tm, LANES), F32), pltpu.VMEM(w_glu.shape[1:], BF16),
                        pltpu.VMEM(w_out.shape[1:], BF16)],
        compiler_params=_params(("arbitrary",), VMEM_LIMIT),
        name="post_mix",
    )(x, attn, yc, w_glu, ssm_g, w_out, g1, n2g, sc2, sh2, w_router_t, router_bias)


def _moe_kernel(kind_ref, rb_ref, bk_ref, pa_ref, pb_ref, first_ref, cast_ref, cpos_ref, pe_ref, offs_ref,
                x_ref, cw_ref, wg_ref, wu_ref, wd_ref, o_ref, wg_s, wu_s, wd_s):
    s = pl.program_id(0)

    @pl.when(cast_ref[s] == 1)
    def _():
        slot = cpos_ref[s]
        wg_s[slot] = wg_ref[...].astype(BF16)
        wu_s[slot] = wu_ref[...].astype(BF16)
        wd_s[slot] = wd_ref[...].astype(BF16)

    @pl.when(kind_ref[s] == STEP_ITEM)
    def _():
        bucket = bk_ref[s]
        base = rb_ref[s] * MOE_ROWS
        lo_row, hi_row = offs_ref[bucket] - base, offs_ref[bucket + 1] - base
        slots = (pa_ref[s], pb_ref[s])
        is_first = first_ref[s] == 1

        def run(r0, r1, z0, z1):
            rows = r0 + lax.broadcasted_iota(jnp.int32, (r1 - r0, 1), 0)
            cw = jnp.where((rows >= lo_row) & (rows < hi_row), cw_ref[r0:r1, :], 0.0)
            x = x_ref[r0:r1, :].astype(BF16)
            y = None
            for k in range(2):
                gate = jnp.dot(x, wg_s[slots[k]], preferred_element_type=F32)
                up = jnp.dot(x, wu_s[slots[k]], preferred_element_type=F32)
                act = (gate * jax.nn.sigmoid(gate) * up * cw[:, k:k + 1]).astype(BF16)
                yk = jnp.dot(act, wd_s[slots[k]], preferred_element_type=F32)
                y = yk if y is None else y + yk

            @pl.when(is_first)
            def _():
                o_ref[r0:r1, :] = y
                for a, b in ((z0, r0), (r1, z1)):
                    if b > a:
                        o_ref[a:b, :] = jnp.zeros((b - a, o_ref.shape[1]), F32)

            @pl.when(jnp.logical_not(is_first))
            def _():
                o_ref[r0:r1, :] += y

        for b0 in range(0, MOE_ROWS, MOE_SUB):
            b1, mid = b0 + MOE_SUB, b0 + MOE_SUB // 2
            has_rows = (lo_row < b1) & (hi_row > b0)
            needs_lower, needs_upper = has_rows & (lo_row < mid), has_rows & (hi_row > mid)
            pl.when(needs_lower & needs_upper)(lambda: run(b0, b1, b0, b1))
            pl.when(needs_lower & jnp.logical_not(needs_upper))(lambda: run(b0, mid, b0, b1))
            pl.when(jnp.logical_not(needs_lower) & needs_upper)(lambda: run(mid, b1, b0, b1))
            if MOE_ROWS > MOE_SUB:
                @pl.when(jnp.logical_not(has_rows) & is_first)
                def _():
                    o_ref[b0:b1, :] = jnp.zeros((MOE_SUB, o_ref.shape[1]), F32)


def _moe_plan_kernel(offs_ref, kind_ref, rb_ref, bk_ref, pa_ref, pb_ref, first_ref, cast_ref, cpos_ref, pe_ref,
                     irb, ibk, *, n_steps_max):
    i32 = jnp.int32
    ng, epg, ppg = N_EXPERT_GROUPS, EXPERTS_PER_GROUP, PAIRS_PER_GROUP

    shift = MOE_ROWS.bit_length() - 1

    def bucket_body(bk, cnt):
        a, b = offs_ref[bk], offs_ref[bk + 1]
        first_blk = lax.shift_right_logical(a, shift)
        n_blk = jnp.where(b > a, lax.shift_right_logical(b - 1, shift) - first_blk + 1, 0)

        def block_body(j, cnt):
            irb[cnt] = first_blk + j
            ibk[cnt] = bk
            return cnt + 1

        return lax.fori_loop(0, n_blk, block_body, cnt)

    n_items = lax.fori_loop(0, ng * ppg, bucket_body, i32(0))

    def count_body(i, m):
        g = ibk[i] // ppg
        return tuple(m[k] + (g == k).astype(i32) for k in range(ng))

    m = lax.fori_loop(0, n_items, count_body, (i32(0),) * ng)

    def next_group(g):
        nxt = i32(-1)
        for k in range(ng - 1, 0, -1):
            nxt = jnp.where((k > g) & (m[k] > 0), k, nxt)
        return nxt

    def emit(s, kind, rb, bk, pa, pb, first, cast, cpos, pe):
        kind_ref[s], rb_ref[s], bk_ref[s], pa_ref[s], pb_ref[s] = kind, rb, bk, pa, pb
        first_ref[s], cast_ref[s], cpos_ref[s], pe_ref[s] = first, cast, cpos, pe

    def item_body(i, carry):
        s, gcur, parity, q, last_pe, last_rb = carry
        rb, bk = irb[i], ibk[i]
        g, pos = bk // ppg, bk % ppg
        new = g != gcur
        started = gcur >= 0
        loaders = jnp.where(new, jnp.where(started, jnp.maximum(epg - q, 0), epg), 0)
        parity = jnp.where(new & started, 1 - parity, parity)
        q = jnp.where(new, 0, q)
        for j in range(epg):
            on = j >= epg - loaders
            emit(s, STEP_LOAD, rb, bk, 0, 0, 0, 1, parity * epg + j, epg * g + j)
            last_pe = jnp.where(on, epg * g + j, last_pe)
            s = s + on.astype(i32)
        nxt = next_group(g)
        pre = (q < epg) & (nxt >= 0)
        pe = jnp.where(pre, epg * nxt + q, last_pe)
        slot_a, slot_b = i32(PAIR_SLOTS[0][0]), i32(PAIR_SLOTS[0][1])
        for p in range(1, ppg):
            slot_a = jnp.where(pos == p, PAIR_SLOTS[p][0], slot_a)
            slot_b = jnp.where(pos == p, PAIR_SLOTS[p][1], slot_b)
        emit(s, STEP_ITEM, rb, bk, parity * epg + slot_a, parity * epg + slot_b, (rb != last_rb).astype(i32),
             pre.astype(i32), (1 - parity) * epg + q, pe)
        return s + 1, g, parity, q + 1, pe, rb

    s, _, _, _, last_pe, last_rb = lax.fori_loop(
        0, n_items, item_body, (i32(0), i32(-1), i32(0), i32(0), i32(0), i32(-1)))
    last_bk = ibk[jnp.maximum(n_items - 1, 0)]

    def pad_body(s, _):
        emit(s, STEP_PAD, last_rb, last_bk, 0, 0, 0, 0, 0, last_pe)
        return 0

    lax.fori_loop(s, n_steps_max, pad_body, 0)


def _moe_steps(bucket, t):
    i32 = jnp.int32
    nbk = N_EXPERT_GROUPS * PAIRS_PER_GROUP
    order = jnp.argsort(bucket, stable=True).astype(i32)
    counts = jnp.sum((bucket[None, :] == jnp.arange(nbk, dtype=i32)[:, None]).astype(i32), axis=1)
    offs = jnp.concatenate([jnp.zeros((1,), i32), jnp.cumsum(counts).astype(i32)])
    assert MOE_ROWS & (MOE_ROWS - 1) == 0
    n_items_max = t // MOE_ROWS + nbk - 1
    n_steps_max = n_items_max + N_EXPERTS
    smem = pl.BlockSpec(memory_space=pltpu.SMEM)
    tables = pl.pallas_call(
        functools.partial(_moe_plan_kernel, n_steps_max=n_steps_max),
        in_specs=[smem],
        out_specs=[smem] * 9,
        out_shape=[jax.ShapeDtypeStruct((n_steps_max,), i32)] * 9,
        scratch_shapes=[pltpu.SMEM((n_items_max + 1,), i32)] * 2,
        name="moe_plan",
    )(offs)
    return order, (*tables, offs), n_steps_max


def _moe(xs, cws, w_gate, w_up, w_down, layer, tables, n_steps_max):
    t, d = xs.shape
    ff = w_gate.shape[3]
    w_map = lambda s, kind, rb, bk, pa, pb, fi, ca, cp, pe, of: (layer, pe[s], 0, 0)
    row_map = lambda s, kind, rb, *_: (rb[s], 0)
    nres = 2 * EXPERTS_PER_GROUP
    grid_spec = pltpu.PrefetchScalarGridSpec(
        num_scalar_prefetch=len(tables),
        grid=(n_steps_max,),
        in_specs=[pl.BlockSpec((MOE_ROWS, d), row_map), pl.BlockSpec((MOE_ROWS, 2), row_map),
                  pl.BlockSpec((None, None, d, ff), w_map), pl.BlockSpec((None, None, d, ff), w_map),
                  pl.BlockSpec((None, None, ff, d), w_map)],
        out_specs=pl.BlockSpec((MOE_ROWS, d), row_map),
        scratch_shapes=[pltpu.VMEM((nres, d, ff), BF16), pltpu.VMEM((nres, d, ff), BF16),
                        pltpu.VMEM((nres, ff, d), BF16)],
    )
    return pl.pallas_call(
        _moe_kernel,
        grid_spec=grid_spec,
        out_shape=jax.ShapeDtypeStruct((t, d), F32),
        compiler_params=_params(("arbitrary",), BIG_VMEM_LIMIT),
        name="moe_grouped",
    )(*tables, xs, cws, w_gate, w_up, w_down)


def _take_rows(a, idx):
    return a.at[idx].get(mode="promise_in_bounds", unique_indices=True)


def _final_kernel(x_ref, y_ref, g_ref, o_ref):
    o_ref[...] = x_ref[...] + g_ref[...] * y_ref[...]


def _final(x1, y, g2, seq):
    t, d = x1.shape
    tm = min(TOKEN_TILE, seq)
    per_b = seq // tm
    tok = lambda w: pl.BlockSpec((tm, w), lambda i: (i, 0))
    return pl.pallas_call(
        _final_kernel,
        grid=(t // tm,),
        in_specs=[tok(d), tok(d), pl.BlockSpec((None, 1, d), lambda i: (i // per_b, 0, 0))],
        out_specs=tok(d),
        out_shape=jax.ShapeDtypeStruct((t, d), F32),
        compiler_params=_params(("parallel",)),
        name="final_residual",
    )(x1, y, g2)


def kernel(x, c, positions, ada_w, ada_b, norm1_g, w_in, q_norm_g, k_norm_g, attn_sink, lam_re, lam_im, ssm_b_re, ssm_b_im, ssm_c_re, ssm_c_im, ssm_d, ssm_log_dt, w_glu, attn_out_g, ssm_out_g, w_out, norm2_g, w_router, router_bias, w_exp_gate, w_exp_up, w_exp_down):
    batch, seq, d = x.shape
    depth = ada_w.shape[0]
    t = batch * seq
    assert seq % ATTN_BLOCK == 0 and seq % SSM_CHUNK == 0 and t % MOE_ROWS == 0

    mod = _adaln_mod(c, ada_w, ada_b).reshape(depth, 6, batch, 1, d)
    cos, sin = _rope_tables(positions)
    head_sum, rot = _rope_constants()
    bias = _attn_bias()
    w_router_t = w_router.T
    s5_mats = _s5_prep(lam_re, lam_im, ssm_b_re, ssm_b_im, ssm_c_re, ssm_c_im, ssm_d, ssm_log_dt)
    router_bias_col = router_bias.reshape(N_EXPERTS, 1)

    xf = x.reshape(t, d)
    res = None
    for l in range(depth):
        sh1, sc1, g1, sh2, sc2, g2 = (mod[l, j] for j in range(6))
        qg = (jnp.tile(q_norm_g[l], N_Q_HEADS) * (HEAD_DIM ** -0.5 * LOG2_E)).reshape(1, ATTN_WIDTH)
        kg = jnp.tile(k_norm_g[l], N_KV_HEADS).reshape(1, KV_WIDTH)
        outs = _inproj(xf, res, sc1, sh1, norm1_g[l].reshape(1, d), w_in, l, qg, kg, head_sum, rot, cos, sin, seq)
        if res is None:
            q, kx, vx, uc = outs
        else:
            q, kx, vx, uc, xf = outs
        attn = _attention(q, kx, vx, attn_sink[l], attn_out_g[l].reshape(1, ATTN_WIDTH), bias, batch, seq)
        yc = _s5_scan(uc, s5_mats, l, seq // SSM_CHUNK, batch)
        x1, h2, cw, gid = _post(xf, attn, yc, w_glu, ssm_out_g[l].reshape(1, SSM_WIDTH), w_out, l, g1,
                                norm2_g[l].reshape(1, d), sc2, sh2, w_router_t, router_bias_col, seq)
        order, tables, n_steps_max = _moe_steps(gid.reshape(t), t)
        y_sorted = _moe(_take_rows(h2, order), _take_rows(cw.T, order), w_exp_gate, w_exp_up, w_exp_down, l,
                        tables, n_steps_max)
        y = _take_rows(y_sorted, jnp.argsort(order).astype(jnp.int32))
        xf, res = x1, (y, g2)
    y, g2 = res
    return _final(xf, y, g2, seq).reshape(batch, seq, d)
```

```python
import functools
import math

import numpy as np
import jax
import jax.numpy as jnp
from jax import lax
from jax.experimental import pallas as pl
from jax.experimental.pallas import tpu as pltpu

F32 = jnp.float32
BF16 = jnp.bfloat16

HEAD_DIM = 64
N_Q_HEADS = 8
N_KV_HEADS = 2
Q_PER_KV = N_Q_HEADS // N_KV_HEADS
ATTN_WIDTH = N_Q_HEADS * HEAD_DIM
KV_WIDTH = N_KV_HEADS * HEAD_DIM
ATTN_BLOCK = 128
ROPE_THETA = 10000.0
ROPE_SLAB = 256
LANES = 128
SSM_GROUP_CH = 16
SSM_GROUPS = 32
SSM_WIDTH = SSM_GROUPS * SSM_GROUP_CH
SSM_STATE = 64
SSM_CHUNK = 16
SSM_COLS = SSM_WIDTH // LANES
COL_GROUPS = LANES // SSM_GROUP_CH
COL_STATE = COL_GROUPS * SSM_STATE
CHUNK_LANES = SSM_CHUNK * LANES
SSM_NSPLIT = 4
S5_ROW_PAD = 8
N_EXPERTS = 16
N_EXPERT_GROUPS = 4
EXPERTS_PER_GROUP = N_EXPERTS // N_EXPERT_GROUPS
PAIRS_PER_GROUP = EXPERTS_PER_GROUP * (EXPERTS_PER_GROUP - 1) // 2
PAIR_SLOTS = ((0, 1), (0, 2), (0, 3), (1, 3), (1, 2), (3, 2))
EPS = 1e-6
LOG2_E = math.log2(math.e)
MASK_BIAS = -1e30

TOKEN_TILE = 1024
MOE_ROWS = 512
MOE_SUB = 256
VMEM_LIMIT = 48 * 1024 * 1024
BIG_VMEM_LIMIT = 56 * 1024 * 1024
STEP_PAD, STEP_LOAD, STEP_ITEM = 0, 1, 2


def _params(sem, vmem=None):
    return pltpu.CompilerParams(dimension_semantics=sem, vmem_limit_bytes=vmem)


def _rms(x, g):
    return x * lax.rsqrt(jnp.mean(x * x, axis=-1, keepdims=True) + EPS) * g


def _mod_kernel(c_ref, w_ref, b_ref, o_ref):
    c = c_ref[...]
    s = c * jax.nn.sigmoid(c)
    o_ref[...] = jnp.dot(s.astype(BF16), w_ref[...].astype(BF16), preferred_element_type=F32) + b_ref[...]


def _adaln_mod(c, ada_w, ada_b):
    depth, d, d6 = ada_w.shape
    nb = c.shape[0]
    n6 = d6 // d
    return pl.pallas_call(
        _mod_kernel,
        grid=(depth, n6),
        in_specs=[pl.BlockSpec((nb, d), lambda l, j: (0, 0)),
                  pl.BlockSpec((None, d, d), lambda l, j: (l, 0, j)),
                  pl.BlockSpec((None, None, 1, d), lambda l, j: (l, j, 0, 0))],
        out_specs=pl.BlockSpec((None, None, nb, d), lambda l, j: (l, j, 0, 0)),
        out_shape=jax.ShapeDtypeStruct((depth, n6, nb, d), F32),
        compiler_params=_params(("arbitrary", "arbitrary"), VMEM_LIMIT),
        name="adaln_mod",
    )(c, ada_w, ada_b.reshape(depth, n6, 1, d))


def _spread(x, expander3):
    hi = x.astype(BF16)
    r1 = x - hi.astype(F32)
    mid = r1.astype(BF16)
    lo = (r1 - mid.astype(F32)).astype(BF16)
    return jnp.dot(jnp.concatenate([hi, mid, lo], axis=1), expander3, preferred_element_type=F32)


def _rope_kernel(pos_ref, freq_ref, lane_ref, quarter_ref, cos_ref, sin_ref):
    per_row = pos_ref.shape[1]
    rows = pos_ref.shape[0]
    pos = _spread(pos_ref[...].astype(F32), lane_ref[...])
    ang = pos * freq_ref[...]
    cos, sin = jnp.cos(ang), jnp.sin(ang)
    for j in range(per_row):
        cos_ref[pl.ds(j, rows, stride=per_row), :] = _spread(cos, quarter_ref[j])
        sin_ref[pl.ds(j, rows, stride=per_row), :] = _spread(sin, quarter_ref[j])


def _rope_tables(positions):
    half = HEAD_DIM // 2
    t = positions.size
    per_row = LANES // half
    rows = t // per_row
    freq = (ROPE_THETA ** (-np.arange(half, dtype=np.float64) / half)).astype(np.float32)
    freq_row = jnp.asarray(np.tile(freq, per_row)[None, :])
    to_quarter = np.repeat(np.eye(per_row, dtype=np.float32), half, axis=1)
    spread = np.zeros((per_row, LANES, LANES), np.float32)
    for j in range(per_row):
        spread[j, j * half:(j + 1) * half, :] = np.tile(np.eye(half, dtype=np.float32), (1, per_row))
    lane_sel = jnp.asarray(np.tile(to_quarter, (3, 1)), BF16)
    quarter_sel = jnp.asarray(np.tile(spread, (1, 3, 1)), BF16)
    blk = min(rows, 512)
    out = pl.BlockSpec((blk * per_row, LANES), lambda i: (i, 0))
    const = lambda a: pl.BlockSpec(a.shape, lambda i: (0,) * a.ndim)
    return pl.pallas_call(
        _rope_kernel,
        grid=(rows // blk,),
        in_specs=[pl.BlockSpec((blk, per_row), lambda i: (i, 0)), const(freq_row), const(lane_sel), const(quarter_sel)],
        out_specs=[out, out],
        out_shape=[jax.ShapeDtypeStruct((t, LANES), F32)] * 2,
        compiler_params=_params(("arbitrary",)),
        name="rope_tables",
    )(positions.reshape(rows, per_row), freq_row, lane_sel, quarter_sel)


def _rope_constants():
    lane = np.arange(ROPE_SLAB)
    head_sum = (lane[:, None] // HEAD_DIM == lane[None, :] // HEAD_DIM).astype(np.float32)
    half = HEAD_DIM // 2
    rot = np.zeros((ROPE_SLAB, ROPE_SLAB), np.float32)
    for d in range(ROPE_SLAB):
        if d % HEAD_DIM < half:
            rot[d + half, d] = -1.0
        else:
            rot[d - half, d] = 1.0
    return jnp.asarray(head_sum, BF16), jnp.asarray(rot, BF16)


def _attend(sink_ref, q_ref, kk, vv, band, first, g_ref, o_ref):
    nsub = q_ref.shape[0] // ATTN_BLOCK
    low = lax.broadcasted_iota(jnp.int32, (kk.shape[0], KV_WIDTH), 1) < HEAD_DIM
    zero = jnp.zeros((kk.shape[0], KV_WIDTH), BF16)

    def variants(a):
        nat, swp = a[:, :KV_WIDTH], a[:, KV_WIDTH:]
        return {(0, 0): jnp.where(low, nat, zero), (0, 1): jnp.where(low, zero, swp),
                (1, 0): jnp.where(low, swp, zero), (1, 1): jnp.where(low, zero, nat)}

    kvar, vvar = variants(kk), variants(vv)
    upper = lax.broadcasted_iota(jnp.int32, (2 * ATTN_BLOCK, 1), 0) < ATTN_BLOCK
    for j in range(nsub):
        bias = first if j == 0 else band
        bias2 = jnp.concatenate([bias, bias], axis=0)
        keys = slice(j * ATTN_BLOCK, (j + 2) * ATTN_BLOCK)
        qrows = slice(j * ATTN_BLOCK, (j + 1) * ATTN_BLOCK)
        tiles = [None] * (N_Q_HEADS // 2)
        for kv in range(N_KV_HEADS):
            for half in range(2):
                pairs = (2 * kv, 2 * kv + 1)
                heads = (2 * pairs[0] + half, 2 * pairs[1] + half)
                qs = jnp.concatenate([q_ref[qrows, p * LANES:(p + 1) * LANES] for p in pairs], axis=0)
                s = lax.dot_general(qs, kvar[(kv, half)][keys], (((1,), (1,)), ((), ())),
                                    preferred_element_type=F32) + bias2
                sink = jnp.where(upper, sink_ref[heads[0]], sink_ref[heads[1]]) * LOG2_E
                m = jnp.maximum(jnp.max(s, axis=-1, keepdims=True), sink)
                p = jnp.exp2(s - m)
                denom = jnp.sum(p, axis=-1, keepdims=True) + jnp.exp2(sink - m)
                o = jnp.dot(p.astype(BF16), vvar[(kv, half)][keys], preferred_element_type=F32) * (1.0 / denom)
                for r, pr in enumerate(pairs):
                    part = o[r * ATTN_BLOCK:(r + 1) * ATTN_BLOCK]
                    tiles[pr] = part if tiles[pr] is None else tiles[pr] + part
        a = jnp.concatenate(tiles, axis=1)
        o_ref[qrows, :] = _rms(a, g_ref[...]).astype(BF16)


def _inproj_kernel(sink_ref, *refs, has_res, per_seq):
    if has_res:
        (x_ref, y_ref, g2_ref, sc_ref, sh_ref, n1_ref, w_ref, qg_ref, kg_ref, hs_ref, rot_ref,
         cos_ref, sin_ref, bias_ref, og_ref, at_ref, uc_ref, xo_ref, u_scr, w_s, q_s, k_s, v_s) = refs
        x = x_ref[...] + g2_ref[...] * y_ref[...]
        xo_ref[...] = x
    else:
        (x_ref, sc_ref, sh_ref, n1_ref, w_ref, qg_ref, kg_ref, hs_ref, rot_ref,
         cos_ref, sin_ref, bias_ref, og_ref, at_ref, uc_ref, u_scr, w_s, q_s, k_s, v_s) = refs
        x = x_ref[...]

    @pl.when(pl.program_id(0) == 0)
    def _():
        w_s[...] = w_ref[...].astype(BF16)

    h = _rms(x, n1_ref[...] * (1.0 + sc_ref[...])) + sh_ref[...]
    proj = jnp.dot(h.astype(BF16), w_s[...], preferred_element_type=F32)
    q = proj[:, :ATTN_WIDTH]
    k = proj[:, ATTN_WIDTH:ATTN_WIDTH + KV_WIDTH]
    v = proj[:, ATTN_WIDTH + KV_WIDTH:ATTN_WIDTH + 2 * KV_WIDTH]
    cos = cos_ref[...]
    sin = sin_ref[...]
    reps = ATTN_WIDTH // LANES
    cos_q = jnp.concatenate([cos] * reps, axis=1)
    sin_q = jnp.concatenate([sin] * reps, axis=1)

    def head_norm_rope(t, gain, c, s):
        outs = []
        for lo in range(0, t.shape[1], ROPE_SLAB):
            wd = min(ROPE_SLAB, t.shape[1] - lo)
            ts, lanes = t[:, lo:lo + wd], slice(lo, lo + wd)
            ssq = jnp.dot((ts * ts).astype(BF16), hs_ref[:wd, :wd], preferred_element_type=F32)
            tn = (ts * lax.rsqrt(ssq * (1.0 / HEAD_DIM) + EPS) * gain[:, lanes]).astype(BF16)
            tr = jnp.dot(tn, rot_ref[:wd, :wd], preferred_element_type=F32)
            outs.append(tn.astype(F32) * c[:, lanes] + tr * s[:, lanes])
        return outs[0] if len(outs) == 1 else jnp.concatenate(outs, axis=1)

    qo = head_norm_rope(q, qg_ref[...], cos_q, sin_q)
    ko = head_norm_rope(k, kg_ref[...], cos, sin)

    tm = q_s.shape[0]
    seq_start = pl.program_id(0) % per_seq == 0

    @pl.when(seq_start)
    def _():
        k_s[0:ATTN_BLOCK, :] = jnp.zeros((ATTN_BLOCK, 2 * KV_WIDTH), BF16)
        v_s[0:ATTN_BLOCK, :] = jnp.zeros((ATTN_BLOCK, 2 * KV_WIDTH), BF16)

    q_s[...] = qo.astype(BF16)
    k_s[ATTN_BLOCK:, :] = jnp.concatenate([ko, pltpu.roll(ko, HEAD_DIM, axis=1)], axis=1).astype(BF16)
    v_s[ATTN_BLOCK:, :] = jnp.concatenate([v, pltpu.roll(v, HEAD_DIM, axis=1)], axis=1).astype(BF16)
    first = bias_ref[jnp.where(seq_start, 0, 1)]
    _attend(sink_ref, q_s, k_s[...], v_s[...], bias_ref[1], first, og_ref, at_ref)
    k_s[0:ATTN_BLOCK, :] = k_s[tm:tm + ATTN_BLOCK, :]
    v_s[0:ATTN_BLOCK, :] = v_s[tm:tm + ATTN_BLOCK, :]

    u0 = ATTN_WIDTH + 2 * KV_WIDTH
    nchunk = u_scr.shape[1] // SSM_CHUNK
    for j in range(SSM_COLS):
        u_scr[j] = proj[:, u0 + j * LANES:u0 + (j + 1) * LANES]
    for s in range(SSM_CHUNK):
        for j in range(SSM_COLS):
            lanes = slice(s * SSM_WIDTH + j * LANES, s * SSM_WIDTH + (j + 1) * LANES)
            uc_ref[:, lanes] = u_scr[j, pl.ds(s, nchunk, stride=SSM_CHUNK), :].astype(BF16)


def _attn_bias():
    qi = np.arange(ATTN_BLOCK)[:, None]
    sj = np.arange(2 * ATTN_BLOCK)[None, :]
    diff = qi + ATTN_BLOCK - sj
    band = (diff >= 0) & (diff < ATTN_BLOCK)
    first = band & (sj >= ATTN_BLOCK)
    return jnp.asarray(np.where(np.stack([first, band]), 0.0, MASK_BIAS).astype(np.float32))


def _inproj_attn(x, res, sc1, sh1, n1g, w_in, layer, qg, kg, head_sum, rot, cos, sin, sink, bias, out_g, seq):
    t, d = x.shape
    tm = min(TOKEN_TILE, seq)
    per_b = seq // tm
    in_width = w_in.shape[2]
    tok = lambda w: pl.BlockSpec((tm, w), lambda i, s: (i, 0))
    const = lambda a: pl.BlockSpec(a.shape, lambda i, s: (0,) * a.ndim)
    per_batch = pl.BlockSpec((None, 1, d), lambda i, s: (i // per_b, 0, 0))
    chunked = pl.BlockSpec((tm // SSM_CHUNK, SSM_CHUNK * SSM_WIDTH), lambda i, s: (i, 0))
    ins, specs = [x], [tok(d)]
    if res is not None:
        y_prev, g2_prev = res
        ins += [y_prev, g2_prev]
        specs += [tok(d), per_batch]
    ins += [sc1, sh1, n1g, w_in, qg, kg, head_sum, rot, cos, sin, bias, out_g]
    specs += [per_batch, per_batch, const(n1g), pl.BlockSpec((None, d, in_width), lambda i, s: (layer, 0, 0)),
              const(qg), const(kg), const(head_sum), const(rot), tok(LANES), tok(LANES), const(bias), const(out_g)]
    out_shape = [jax.ShapeDtypeStruct((t, ATTN_WIDTH), BF16),
                 jax.ShapeDtypeStruct((t // SSM_CHUNK, SSM_CHUNK * SSM_WIDTH), BF16)]
    out_specs = [tok(ATTN_WIDTH), chunked]
    if res is not None:
        out_shape.append(jax.ShapeDtypeStruct((t, d), F32))
        out_specs.append(tok(d))
    assert in_width == ATTN_WIDTH + 2 * KV_WIDTH + SSM_WIDTH and tm % ATTN_BLOCK == 0
    grid_spec = pltpu.PrefetchScalarGridSpec(
        num_scalar_prefetch=1,
        grid=(t // tm,),
        in_specs=specs,
        out_specs=out_specs,
        scratch_shapes=[pltpu.VMEM((SSM_COLS, tm, LANES), F32), pltpu.VMEM((d, in_width), BF16),
                        pltpu.VMEM((tm, ATTN_WIDTH), BF16), pltpu.VMEM((tm + ATTN_BLOCK, 2 * KV_WIDTH), BF16),
                        pltpu.VMEM((tm + ATTN_BLOCK, 2 * KV_WIDTH), BF16)],
    )
    return pl.pallas_call(
        functools.partial(_inproj_kernel, has_res=res is not None, per_seq=per_b),
        grid_spec=grid_spec,
        out_shape=out_shape,
        compiler_params=_params(("arbitrary",), BIG_VMEM_LIMIT),
        name="inproj_attn",
    )(sink, *ins)


def _s5_prep_kernel(lr_re_ref, lr_im_ref, ldt_ref, bt_re_ref, bt_im_ref, ct_re_ref, ct_im_ref,
                    d_ref, lcol_re_ref, lcol_im_ref, ldtcol_ref, exp_ref, exph_ref, expt_ref, expw_ref,
                    t_ref, w_ref, v_ref, la_ref, lb_ref):
    hi = lax.Precision.HIGHEST
    nl = SSM_CHUNK
    low = lax.broadcasted_iota(jnp.int32, (1, 2 * SSM_STATE), 1) < SSM_STATE
    row_low = lax.broadcasted_iota(jnp.int32, (2 * SSM_STATE, 1), 0) < SSM_STATE
    jcol = lax.broadcasted_iota(jnp.int32, (nl, 1), 0).astype(F32)
    kt_lane = lax.broadcasted_iota(jnp.int32, (SSM_GROUP_CH, nl * SSM_GROUP_CH), 1)
    kt_row = lax.broadcasted_iota(jnp.int32, (SSM_GROUP_CH, nl * SSM_GROUP_CH), 0)

    w_all, v_all, kt_all = [], [], []
    for gm in range(COL_GROUPS):
        dt = jnp.exp(ldt_ref[gm])
        lam_re, lam_im = lr_re_ref[gm], lr_im_ref[gm]
        a_r, th_r = lam_re * dt, lam_im * dt

        er = jnp.exp(jcol * a_r)
        pw_re, pw_im = er * jnp.cos(jcol * th_r), er * jnp.sin(jcol * th_r)

        nr, ni = pw_re[1:2, :] - 1.0, pw_im[1:2, :]
        den = lam_re * lam_re + lam_im * lam_im
        c_re, c_im = (nr * lam_re + ni * lam_im) / den, (ni * lam_re - nr * lam_im) / den
        bt_re, bt_im = bt_re_ref[gm], bt_im_ref[gm]
        bb_re, bb_im = c_re * bt_re - c_im * bt_im, c_re * bt_im + c_im * bt_re

        w_rows = []
        for s in range(nl):
            j = nl - 1 - s
            pr, pi = pw_re[j:j + 1, :], pw_im[j:j + 1, :]
            w_rows.append(jnp.where(low, pr * bb_re - pi * bb_im, pr * bb_im + pi * bb_re))
        w_all.append(w_rows)

        pw_re_t, pw_im_t = pw_re.T, pw_im.T
        pc, ps = _spread(pw_re_t, exp_ref[...]), _spread(pw_im_t, exp_ref[...])
        ct_re, ct_im = _spread(ct_re_ref[gm], exph_ref[...]), _spread(ct_im_ref[gm], exph_ref[...])
        a_re, a_im = ct_re * pc - ct_im * ps, ct_re * ps + ct_im * pc
        a_cat = jnp.where(row_low, a_re, -a_im)
        l1_re, l1_im = pw_re_t[:, 1:2], pw_im_t[:, 1:2]
        v_re, v_im = a_re * l1_re - a_im * l1_im, a_re * l1_im + a_im * l1_re
        v_all.append(jnp.where(row_low, v_re, -v_im))

        kt = jnp.dot(jnp.where(low, bb_re, bb_im), a_cat, precision=hi, preferred_element_type=F32)
        kt_all.append(kt + jnp.where(kt_lane == kt_row, d_ref[gm], 0.0))

    def same_group(shape, row_group, lane_group):
        r = lax.broadcasted_iota(jnp.int32, shape, 0)
        c = lax.broadcasted_iota(jnp.int32, shape, 1)
        return (row_group(r) == lane_group(c)).astype(F32)

    chan_group = lambda i: (i >> 4) & (COL_GROUPS - 1)
    state_group = lambda i: (i >> 6) & (COL_GROUPS - 1)
    over_steps = lambda m: jnp.concatenate([m] * nl, axis=1)

    kt_wide = jnp.dot(jnp.concatenate(kt_all, axis=0).astype(BF16), expt_ref[...], preferred_element_type=F32)
    bd = (kt_wide * over_steps(same_group((LANES, LANES), chan_group, chan_group))).astype(BF16)
    t_ref[0:LANES, :] = bd
    for s in range(1, nl):
        t_ref[s * LANES:(s + 1) * LANES, :] = jnp.concatenate(
            [jnp.zeros((LANES, s * LANES), BF16), bd[:, :CHUNK_LANES - s * LANES]], axis=1)

    w_stack = jnp.concatenate([w_all[gm][s] for s in range(nl) for gm in range(COL_GROUPS)], axis=0)
    w_wide = jnp.dot(w_stack.astype(BF16), expw_ref[...], preferred_element_type=F32)
    w_mask = same_group((LANES, 2 * COL_STATE), chan_group, state_group)
    w_ref[...] = (w_wide.reshape(nl, LANES, 2 * COL_STATE) * w_mask[None]).reshape(CHUNK_LANES, 2 * COL_STATE).astype(BF16)

    v_stack = jnp.concatenate([v_all[gm][half * SSM_STATE:(half + 1) * SSM_STATE, :]
                               for half in range(2) for gm in range(COL_GROUPS)], axis=0)
    v_wide = jnp.dot(v_stack.astype(BF16), expt_ref[...], preferred_element_type=F32)
    v_ref[...] = (v_wide * over_steps(same_group((2 * COL_STATE, LANES), state_group, chan_group))).astype(BF16)

    dtc = jnp.exp(ldtcol_ref[...])
    e16 = jnp.exp(nl * lcol_re_ref[...] * dtc)
    ang = nl * lcol_im_ref[...] * dtc
    la_ref[...] = e16 * jnp.cos(ang)
    lb_ref[...] = e16 * jnp.sin(ang)


def _s5_prep(lam_re, lam_im, b_re, b_im, c_re, c_im, d_skip, log_dt):
    g, p, h, nl = SSM_GROUPS, SSM_STATE, SSM_GROUP_CH, SSM_CHUNK
    cg = COL_GROUPS
    nc = lam_re.shape[0] * SSM_COLS
    col = lambda a: a.reshape((nc, cg) + a.shape[2:])
    dup_row = lambda a: col(jnp.tile(a, (1, 1, 2))[:, :, None, :])
    bt = lambda a: col(jnp.tile(jnp.swapaxes(a, 2, 3), (1, 1, 1, 2)))
    ct = lambda a: col(jnp.tile(jnp.swapaxes(a, 2, 3), (1, 1, 2, 1)))
    d_pad = col(jnp.pad(d_skip.reshape(-1, g, 1, h), ((0, 0), (0, 0), (0, 0), (0, nl * h - h))))
    wide = lambda a: a.reshape(nc, 1, cg * p)
    expand = jnp.asarray(np.tile(np.repeat(np.eye(nl, dtype=np.float32), h, axis=1), (3, 1)), BF16)
    expand_h = jnp.asarray(np.tile(np.eye(h, dtype=np.float32), (3, nl)), BF16)
    exp_t = np.zeros((nl, h, nl, cg, h), np.float32)
    exp_w = np.zeros((2, p, 2, cg, p), np.float32)
    for gm in range(cg):
        exp_t[:, :, :, gm, :] = np.eye(nl * h, dtype=np.float32).reshape(nl, h, nl, h)
        exp_w[:, :, :, gm, :] = np.eye(2 * p, dtype=np.float32).reshape(2, p, 2, p)
    exp_t = jnp.asarray(exp_t.reshape(nl * h, CHUNK_LANES), BF16)
    exp_w = jnp.asarray(exp_w.reshape(2 * p, 2 * COL_STATE), BF16)
    blk = lambda *s: pl.BlockSpec((None,) + s, lambda i: (i,) + (0,) * len(s))
    const = lambda a: pl.BlockSpec(a.shape, lambda i: (0,) * a.ndim)
    lw = nl * h
    return pl.pallas_call(
        _s5_prep_kernel,
        grid=(nc,),
        in_specs=[blk(cg, 1, 2 * p), blk(cg, 1, 2 * p), blk(cg, 1, 1),
                  blk(cg, h, 2 * p), blk(cg, h, 2 * p), blk(cg, 2 * p, h), blk(cg, 2 * p, h), blk(cg, 1, lw),
                  blk(1, cg * p), blk(1, cg * p), blk(1, cg * p), const(expand), const(expand_h), const(exp_t),
                  const(exp_w)],
        out_specs=[blk(CHUNK_LANES, CHUNK_LANES), blk(CHUNK_LANES, 2 * COL_STATE), blk(2 * COL_STATE, CHUNK_LANES),
                   blk(1, COL_STATE), blk(1, COL_STATE)],
        out_shape=[jax.ShapeDtypeStruct((nc, CHUNK_LANES, CHUNK_LANES), BF16),
                   jax.ShapeDtypeStruct((nc, CHUNK_LANES, 2 * COL_STATE), BF16),
                   jax.ShapeDtypeStruct((nc, 2 * COL_STATE, CHUNK_LANES), BF16),
                   jax.ShapeDtypeStruct((nc, 1, COL_STATE), F32), jax.ShapeDtypeStruct((nc, 1, COL_STATE), F32)],
        compiler_params=_params(("parallel",), VMEM_LIMIT),
        name="s5_prep",
    )(dup_row(lam_re), dup_row(lam_im), col(log_dt[:, :, None, None]),
      bt(b_re), bt(b_im), ct(c_re), ct(c_im), d_pad, wide(lam_re), wide(lam_im),
      wide(jnp.repeat(log_dt, p, axis=1)), expand, expand_h, exp_t, exp_w)


def _s5_kernel(*refs, nchunks, nb):
    uc_refs = refs[:SSM_CHUNK]
    t_ref, w_ref, v_ref, la_ref, lb_ref, o_ref, ucat_ref, s_ref, xp_ref = refs[SSM_CHUNK:]

    @pl.when(pl.program_id(1) == 0)
    def _():
        for s in range(SSM_CHUNK):
            ucat_ref[:, s * LANES:(s + 1) * LANES] = uc_refs[s][...]
        s_in = jnp.dot(ucat_ref[...], w_ref[...], preferred_element_type=F32)
        nblk = COL_STATE // LANES
        pitch = s_ref.shape[1] // nb
        for b in range(2 * nblk):
            for q in range(nb):
                s_ref[b, q * pitch:q * pitch + nchunks, :] = s_in[q * nchunks:(q + 1) * nchunks, b * LANES:(b + 1) * LANES]
        lr = [jnp.broadcast_to(la_ref[:, b * LANES:(b + 1) * LANES], (nb, LANES)) for b in range(nblk)]
        li = [jnp.broadcast_to(lb_ref[:, b * LANES:(b + 1) * LANES], (nb, LANES)) for b in range(nblk)]

        def step(c, carry):
            rows = pl.ds(c, nb, stride=pitch)
            out = []
            for b in range(nblk):
                re, im = carry[2 * b], carry[2 * b + 1]
                xp_ref[b, rows, :] = re
                xp_ref[nblk + b, rows, :] = im
                out.append(lr[b] * re - li[b] * im + s_ref[b, rows, :])
                out.append(lr[b] * im + li[b] * re + s_ref[nblk + b, rows, :])
            return tuple(out)

        zero = jnp.zeros((nb, LANES), F32)
        lax.fori_loop(0, nchunks, step, (zero,) * (2 * nblk), unroll=4)

    pitch = xp_ref.shape[1] // nb
    xp = jnp.concatenate(
        [jnp.concatenate([xp_ref[b, q * pitch:q * pitch + nchunks, :] for q in range(nb)], axis=0)
         for b in range(2 * COL_STATE // LANES)], axis=1).astype(BF16)
    inter = jnp.dot(xp, v_ref[...], preferred_element_type=F32)
    for kk in range(SSM_NSPLIT):
        @pl.when(pl.program_id(1) == kk)
        def _():
            live = (kk + 1) * (CHUNK_LANES // SSM_NSPLIT)
            intra = jnp.dot(ucat_ref[:, :live], t_ref[:live, :], preferred_element_type=F32)
            o_ref[...] = (intra + inter).astype(BF16)


def _s5_scan(uc, mats, layer, nchunks, nb):
    rows = uc.shape[0]
    c0 = layer * SSM_COLS
    split = CHUNK_LANES // SSM_NSPLIT
    u_spec = lambda s: pl.BlockSpec((rows, LANES), lambda j, k: (0, SSM_COLS * s + j))
    return pl.pallas_call(
        functools.partial(_s5_kernel, nchunks=nchunks, nb=nb),
        grid=(SSM_COLS, SSM_NSPLIT),
        in_specs=[u_spec(s) for s in range(SSM_CHUNK)] + [
            pl.BlockSpec((None, CHUNK_LANES, split), lambda j, k: (c0 + j, 0, k)),
            pl.BlockSpec((None, CHUNK_LANES, 2 * COL_STATE), lambda j, k: (c0 + j, 0, 0)),
            pl.BlockSpec((None, 2 * COL_STATE, split), lambda j, k: (c0 + j, 0, k)),
            pl.BlockSpec((None, 1, COL_STATE), lambda j, k: (c0 + j, 0, 0)),
            pl.BlockSpec((None, 1, COL_STATE), lambda j, k: (c0 + j, 0, 0))],
        out_specs=pl.BlockSpec((None, rows, split), lambda j, k: (j, 0, k)),
        out_shape=jax.ShapeDtypeStruct((SSM_COLS, rows, CHUNK_LANES), BF16),
        scratch_shapes=[pltpu.VMEM((rows, CHUNK_LANES), BF16),
                        pltpu.VMEM((2 * COL_STATE // LANES, nb * (nchunks + S5_ROW_PAD), LANES), F32),
                        pltpu.VMEM((2 * COL_STATE // LANES, nb * (nchunks + S5_ROW_PAD), LANES), F32)],
        compiler_params=_params(("parallel", "arbitrary"), VMEM_LIMIT),
        name="s5_scan",
    )(*([uc] * SSM_CHUNK), *mats)


def _route(logits, bias):
    m = jnp.max(logits, axis=0, keepdims=True)
    e = jnp.exp(logits - m)
    probs = e / jnp.sum(e, axis=0, keepdims=True)
    sel = probs + bias
    row = lambda a, i: a[i:i + 1, :]
    best_score, best = None, None
    for grp in range(N_EXPERT_GROUPS):
        a, b, c, d = (row(sel, EXPERTS_PER_GROUP * grp + i) for i in range(EXPERTS_PER_GROUP))
        hab, lab, hcd, lcd = jnp.maximum(a, b), jnp.minimum(a, b), jnp.maximum(c, d), jnp.minimum(c, d)
        top1 = jnp.maximum(hab, hcd)
        top2 = jnp.maximum(jnp.maximum(lab, lcd), jnp.minimum(hab, hcd))
        score = top1 + top2
        if grp == 0:
            best_score, best = score, jnp.zeros(score.shape, jnp.int32)
        else:
            better = score > best_score
            best = jnp.where(better, grp, best)
            best_score = jnp.where(better, score, best_score)

    def pick(a, i):
        out = row(a, i)
        for grp in range(1, N_EXPERT_GROUPS):
            out = jnp.where(best == grp, row(a, EXPERTS_PER_GROUP * grp + i), out)
        return out

    s_in = [pick(sel, i) for i in range(EXPERTS_PER_GROUP)]
    p_in = [pick(probs, i) for i in range(EXPERTS_PER_GROUP)]
    neg = jnp.full(s_in[0].shape, -jnp.inf, F32)

    def argmax_first(vals):
        idx, val = jnp.zeros(vals[0].shape, jnp.int32), vals[0]
        for i in range(1, len(vals)):
            better = vals[i] > val
            idx = jnp.where(better, i, idx)
            val = jnp.where(better, vals[i], val)
        return idx

    i1 = argmax_first(s_in)
    i2 = argmax_first([jnp.where(i1 == i, neg, s_in[i]) for i in range(EXPERTS_PER_GROUP)])
    zero = jnp.zeros(p_in[0].shape, F32)
    g1 = sum(jnp.where(i1 == i, p_in[i], zero) for i in range(EXPERTS_PER_GROUP))
    g2 = sum(jnp.where(i2 == i, p_in[i], zero) for i in range(EXPERTS_PER_GROUP))
    tot = g1 + g2
    w1, w2 = g1 / tot, g2 / tot
    first_low = i1 < i2
    low, high = jnp.minimum(i1, i2), jnp.maximum(i1, i2)
    w_low, w_high = jnp.where(first_low, w1, w2), jnp.where(first_low, w2, w1)
    pos = jnp.where(low == 0, high - 1, jnp.where(low == 1, jnp.where(high == 2, 4, 3), 5))
    swap = low == 2
    bucket = best * PAIRS_PER_GROUP + pos
    return jnp.concatenate([jnp.where(swap, w_high, w_low), jnp.where(swap, w_low, w_high)], axis=0), bucket


def _router_logits(w_t, h):
    w_hi = w_t.astype(BF16)
    w_r = w_t - w_hi.astype(F32)
    w_mid = w_r.astype(BF16)
    w_lo = (w_r - w_mid.astype(F32)).astype(BF16)
    h_hi = h.astype(BF16)
    h_lo = (h - h_hi.astype(F32)).astype(BF16)
    dims = (((1,), (1,)), ((), ()))
    a = lax.dot_general(jnp.concatenate([w_hi, w_mid, w_lo], axis=0), h_hi, dims, preferred_element_type=F32)
    b = lax.dot_general(jnp.concatenate([w_hi, w_mid], axis=0), h_lo, dims, preferred_element_type=F32)
    e = w_t.shape[0]
    return a[:e] + a[e:2 * e] + a[2 * e:] + b[:e] + b[e:]


def _post_kernel(x_ref, at_ref, yc_ref, wglu_ref, gs_ref, wo_ref, g1_ref, n2_ref, sc_ref, sh_ref,
                 wrt_ref, rb_ref, x1_ref, h2_ref, cw_ref, gid_ref, y_scr, wglu_s, wo_s):
    @pl.when(pl.program_id(0) == 0)
    def _():
        wglu_s[...] = wglu_ref[...].astype(BF16)
        wo_s[...] = wo_ref[...].astype(BF16)

    nchunk = y_scr.shape[1] // SSM_CHUNK
    for s in range(SSM_CHUNK):
        for j in range(SSM_COLS):
            y_scr[j, pl.ds(s, nchunk, stride=SSM_CHUNK), :] = yc_ref[j, :, s * LANES:(s + 1) * LANES].astype(F32)
    yg = jax.nn.gelu(jnp.concatenate([y_scr[j] for j in range(SSM_COLS)], axis=1))
    z = yg * jax.nn.sigmoid(jnp.dot(yg.astype(BF16), wglu_s[...], preferred_element_type=F32))
    zn = _rms(z, gs_ref[...]).astype(BF16)
    o = (jnp.dot(at_ref[...], wo_s[:ATTN_WIDTH, :], preferred_element_type=F32)
         + jnp.dot(zn, wo_s[ATTN_WIDTH:, :], preferred_element_type=F32))
    x1 = x_ref[...] + g1_ref[...] * o
    x1_ref[...] = x1
    h2 = _rms(x1, n2_ref[...] * (1.0 + sc_ref[...])) + sh_ref[...]
    h2_ref[...] = h2
    logits = _router_logits(wrt_ref[...], h2)
    cw, bucket = _route(logits, rb_ref[...])
    cw_ref[...] = cw
    gid_ref[...] = bucket


def _post(x, attn, yc, w_glu, ssm_g, w_out, layer, g1, n2g, sc2, sh2, w_router_t, router_bias, seq):
    t, d = x.shape
    tm = min(TOKEN_TILE, seq)
    per_b = seq // tm
    tok = lambda w: pl.BlockSpec((tm, w), lambda i: (i, 0))
    const = lambda a: pl.BlockSpec(a.shape, lambda i: (0,) * a.ndim)
    per_batch = pl.BlockSpec((None, 1, d), lambda i: (i // per_b, 0, 0))
    col = lambda r: pl.BlockSpec((r, tm), lambda i: (0, i))
    of_layer = lambda a: pl.BlockSpec((None,) + a.shape[1:], lambda i: (layer, 0, 0))
    chunked = pl.BlockSpec((SSM_COLS, tm // SSM_CHUNK, CHUNK_LANES), lambda i: (0, i, 0))
    return pl.pallas_call(
        _post_kernel,
        grid=(t // tm,),
        in_specs=[tok(d), tok(ATTN_WIDTH), chunked, of_layer(w_glu), const(ssm_g),
                  of_layer(w_out), per_batch, const(n2g), per_batch, per_batch, const(w_router_t), const(router_bias)],
        out_specs=[tok(d), tok(d), col(2), col(1)],
        out_shape=[jax.ShapeDtypeStruct((t, d), F32), jax.ShapeDtypeStruct((t, d), F32),
                   jax.ShapeDtypeStruct((2, t), F32), jax.ShapeDtypeStruct((1, t), jnp.int32)],
        scratch_shapes=[pltpu.VMEM((SSM_COLS, tm, LANES), F32), pltpu.VMEM(w_glu.shape[1:], BF16),
                        pltpu.VMEM(w_out.shape[1:], BF16)],
        compiler_params=_params(("arbitrary",), VMEM_LIMIT),
        name="post_mix",
    )(x, attn, yc, w_glu, ssm_g, w_out, g1, n2g, sc2, sh2, w_router_t, router_bias)


def _moe_kernel(kind_ref, rb_ref, bk_ref, pa_ref, pb_ref, first_ref, cast_ref, cpos_ref, pe_ref, offs_ref,
                x_ref, cw_ref, wg_ref, wu_ref, wd_ref, o_ref, wg_s, wu_s, wd_s):
    s = pl.program_id(0)

    @pl.when(cast_ref[s] == 1)
    def _():
        slot = cpos_ref[s]
        wg_s[slot] = wg_ref[...].astype(BF16)
        wu_s[slot] = wu_ref[...].astype(BF16)
        wd_s[slot] = wd_ref[...].astype(BF16)

    @pl.when(kind_ref[s] == STEP_ITEM)
    def _():
        bucket = bk_ref[s]
        base = rb_ref[s] * MOE_ROWS
        lo_row, hi_row = offs_ref[bucket] - base, offs_ref[bucket + 1] - base
        slots = (pa_ref[s], pb_ref[s])
        is_first = first_ref[s] == 1

        def run(r0, r1, z0, z1):
            rows = r0 + lax.broadcasted_iota(jnp.int32, (r1 - r0, 1), 0)
            cw = jnp.where((rows >= lo_row) & (rows < hi_row), cw_ref[r0:r1, :], 0.0)
            x = x_ref[r0:r1, :].astype(BF16)
            y = None
            for k in range(2):
                gate = jnp.dot(x, wg_s[slots[k]], preferred_element_type=F32)
                up = jnp.dot(x, wu_s[slots[k]], preferred_element_type=F32)
                act = (gate * jax.nn.sigmoid(gate) * up * cw[:, k:k + 1]).astype(BF16)
                yk = jnp.dot(act, wd_s[slots[k]], preferred_element_type=F32)
                y = yk if y is None else y + yk

            @pl.when(is_first)
            def _():
                o_ref[r0:r1, :] = y
                for a, b in ((z0, r0), (r1, z1)):
                    if b > a:
                        o_ref[a:b, :] = jnp.zeros((b - a, o_ref.shape[1]), F32)

            @pl.when(jnp.logical_not(is_first))
            def _():
                o_ref[r0:r1, :] += y

        for b0 in range(0, MOE_ROWS, MOE_SUB):
            b1, mid = b0 + MOE_SUB, b0 + MOE_SUB // 2
            has_rows = (lo_row < b1) & (hi_row > b0)
            needs_lower, needs_upper = has_rows & (lo_row < mid), has_rows & (hi_row > mid)
            pl.when(needs_lower & needs_upper)(lambda: run(b0, b1, b0, b1))
            pl.when(needs_lower & jnp.logical_not(needs_upper))(lambda: run(b0, mid, b0, b1))
            pl.when(jnp.logical_not(needs_lower) & needs_upper)(lambda: run(mid, b1, b0, b1))
            if MOE_ROWS > MOE_SUB:
                @pl.when(jnp.logical_not(has_rows) & is_first)
                def _():
                    o_ref[b0:b1, :] = jnp.zeros((MOE_SUB, o_ref.shape[1]), F32)


def _moe_plan_kernel(offs_ref, kind_ref, rb_ref, bk_ref, pa_ref, pb_ref, first_ref, cast_ref, cpos_ref, pe_ref,
                     irb, ibk, *, n_steps_max):
    i32 = jnp.int32
    ng, epg, ppg = N_EXPERT_GROUPS, EXPERTS_PER_GROUP, PAIRS_PER_GROUP

    shift = MOE_ROWS.bit_length() - 1

    def bucket_body(bk, cnt):
        a, b = offs_ref[bk], offs_ref[bk + 1]
        first_blk = lax.shift_right_logical(a, shift)
        n_blk = jnp.where(b > a, lax.shift_right_logical(b - 1, shift) - first_blk + 1, 0)

        def block_body(j, cnt):
            irb[cnt] = first_blk + j
            ibk[cnt] = bk
            return cnt + 1

        return lax.fori_loop(0, n_blk, block_body, cnt)

    n_items = lax.fori_loop(0, ng * ppg, bucket_body, i32(0))

    def count_body(i, m):
        g = ibk[i] // ppg
        return tuple(m[k] + (g == k).astype(i32) for k in range(ng))

    m = lax.fori_loop(0, n_items, count_body, (i32(0),) * ng)

    def next_group(g):
        nxt = i32(-1)
        for k in range(ng - 1, 0, -1):
            nxt = jnp.where((k > g) & (m[k] > 0), k, nxt)
        return nxt

    def emit(s, kind, rb, bk, pa, pb, first, cast, cpos, pe):
        kind_ref[s], rb_ref[s], bk_ref[s], pa_ref[s], pb_ref[s] = kind, rb, bk, pa, pb
        first_ref[s], cast_ref[s], cpos_ref[s], pe_ref[s] = first, cast, cpos, pe

    def item_body(i, carry):
        s, gcur, parity, q, last_pe, last_rb = carry
        rb, bk = irb[i], ibk[i]
        g, pos = bk // ppg, bk % ppg
        new = g != gcur
        started = gcur >= 0
        loaders = jnp.where(new, jnp.where(started, jnp.maximum(epg - q, 0), epg), 0)
        parity = jnp.where(new & started, 1 - parity, parity)
        q = jnp.where(new, 0, q)
        for j in range(epg):
            on = j >= epg - loaders
            emit(s, STEP_LOAD, rb, bk, 0, 0, 0, 1, parity * epg + j, epg * g + j)
            last_pe = jnp.where(on, epg * g + j, last_pe)
            s = s + on.astype(i32)
        nxt = next_group(g)
        pre = (q < epg) & (nxt >= 0)
        pe = jnp.where(pre, epg * nxt + q, last_pe)
        slot_a, slot_b = i32(PAIR_SLOTS[0][0]), i32(PAIR_SLOTS[0][1])
        for p in range(1, ppg):
            slot_a = jnp.where(pos == p, PAIR_SLOTS[p][0], slot_a)
            slot_b = jnp.where(pos == p, PAIR_SLOTS[p][1], slot_b)
        emit(s, STEP_ITEM, rb, bk, parity * epg + slot_a, parity * epg + slot_b, (rb != last_rb).astype(i32),
             pre.astype(i32), (1 - parity) * epg + q, pe)
        return s + 1, g, parity, q + 1, pe, rb

    s, _, _, _, last_pe, last_rb = lax.fori_loop(
        0, n_items, item_body, (i32(0), i32(-1), i32(0), i32(0), i32(0), i32(-1)))
    last_bk = ibk[jnp.maximum(n_items - 1, 0)]

    def pad_body(s, _):
        emit(s, STEP_PAD, last_rb, last_bk, 0, 0, 0, 0, 0, last_pe)
        return 0

    lax.fori_loop(s, n_steps_max, pad_body, 0)


def _moe_steps(bucket, t):
    i32 = jnp.int32
    nbk = N_EXPERT_GROUPS * PAIRS_PER_GROUP
    order = jnp.argsort(bucket, stable=True).astype(i32)
    counts = jnp.sum((bucket[None, :] == jnp.arange(nbk, dtype=i32)[:, None]).astype(i32), axis=1)
    offs = jnp.concatenate([jnp.zeros((1,), i32), jnp.cumsum(counts).astype(i32)])
    assert MOE_ROWS & (MOE_ROWS - 1) == 0
    n_items_max = t // MOE_ROWS + nbk - 1
    n_steps_max = n_items_max + N_EXPERTS
    smem = pl.BlockSpec(memory_space=pltpu.SMEM)
    tables = pl.pallas_call(
        functools.partial(_moe_plan_kernel, n_steps_max=n_steps_max),
        in_specs=[smem],
        out_specs=[smem] * 9,
        out_shape=[jax.ShapeDtypeStruct((n_steps_max,), i32)] * 9,
        scratch_shapes=[pltpu.SMEM((n_items_max + 1,), i32)] * 2,
        name="moe_plan",
    )(offs)
    return order, (*tables, offs), n_steps_max


def _moe(xs, cws, w_gate, w_up, w_down, layer, tables, n_steps_max):
    t, d = xs.shape
    ff = w_gate.shape[3]
    w_map = lambda s, kind, rb, bk, pa, pb, fi, ca, cp, pe, of: (layer, pe[s], 0, 0)
    row_map = lambda s, kind, rb, *_: (rb[s], 0)
    nres = 2 * EXPERTS_PER_GROUP
    grid_spec = pltpu.PrefetchScalarGridSpec(
        num_scalar_prefetch=len(tables),
        grid=(n_steps_max,),
        in_specs=[pl.BlockSpec((MOE_ROWS, d), row_map), pl.BlockSpec((MOE_ROWS, 2), row_map),
                  pl.BlockSpec((None, None, d, ff), w_map), pl.BlockSpec((None, None, d, ff), w_map),
                  pl.BlockSpec((None, None, ff, d), w_map)],
        out_specs=pl.BlockSpec((MOE_ROWS, d), row_map),
        scratch_shapes=[pltpu.VMEM((nres, d, ff), BF16), pltpu.VMEM((nres, d, ff), BF16),
                        pltpu.VMEM((nres, ff, d), BF16)],
    )
    return pl.pallas_call(
        _moe_kernel,
        grid_spec=grid_spec,
        out_shape=jax.ShapeDtypeStruct((t, d), F32),
        compiler_params=_params(("arbitrary",), BIG_VMEM_LIMIT),
        name="moe_grouped",
    )(*tables, xs, cws, w_gate, w_up, w_down)


def _take_rows(a, idx):
    return a.at[idx].get(mode="promise_in_bounds", unique_indices=True)


def _final_kernel(x_ref, y_ref, g_ref, o_ref):
    o_ref[...] = x_ref[...] + g_ref[...] * y_ref[...]


def _final(x1, y, g2, seq):
    t, d = x1.shape
    tm = min(TOKEN_TILE, seq)
    per_b = seq // tm
    tok = lambda w: pl.BlockSpec((tm, w), lambda i: (i, 0))
    return pl.pallas_call(
        _final_kernel,
        grid=(t // tm,),
        in_specs=[tok(d), tok(d), pl.BlockSpec((None, 1, d), lambda i: (i // per_b, 0, 0))],
        out_specs=tok(d),
        out_shape=jax.ShapeDtypeStruct((t, d), F32),
        compiler_params=_params(("parallel",)),
        name="final_residual",
    )(x1, y, g2)


def kernel(x, c, positions, ada_w, ada_b, norm1_g, w_in, q_norm_g, k_norm_g, attn_sink, lam_re, lam_im, ssm_b_re, ssm_b_im, ssm_c_re, ssm_c_im, ssm_d, ssm_log_dt, w_glu, attn_out_g, ssm_out_g, w_out, norm2_g, w_router, router_bias, w_exp_gate, w_exp_up, w_exp_down):
    batch, seq, d = x.shape
    depth = ada_w.shape[0]
    t = batch * seq
    assert seq % ATTN_BLOCK == 0 and seq % SSM_CHUNK == 0 and t % MOE_ROWS == 0

    mod = _adaln_mod(c, ada_w, ada_b).reshape(depth, 6, batch, 1, d)
    cos, sin = _rope_tables(positions)
    head_sum, rot = _rope_constants()
    bias = _attn_bias()
    w_router_t = w_router.T
    s5_mats = _s5_prep(lam_re, lam_im, ssm_b_re, ssm_b_im, ssm_c_re, ssm_c_im, ssm_d, ssm_log_dt)
    router_bias_col = router_bias.reshape(N_EXPERTS, 1)

    xf = x.reshape(t, d)
    res = None
    for l in range(depth):
        sh1, sc1, g1, sh2, sc2, g2 = (mod[l, j] for j in range(6))
        qg = (jnp.tile(q_norm_g[l], N_Q_HEADS) * (HEAD_DIM ** -0.5 * LOG2_E)).reshape(1, ATTN_WIDTH)
        kg = jnp.tile(k_norm_g[l], N_KV_HEADS).reshape(1, KV_WIDTH)
        outs = _inproj_attn(xf, res, sc1, sh1, norm1_g[l].reshape(1, d), w_in, l, qg, kg, head_sum, rot, cos, sin,
                            attn_sink[l], bias, attn_out_g[l].reshape(1, ATTN_WIDTH), seq)
        if res is None:
            attn, uc = outs
        else:
            attn, uc, xf = outs
        yc = _s5_scan(uc, s5_mats, l, seq // SSM_CHUNK, batch)
        x1, h2, cw, gid = _post(xf, attn, yc, w_glu, ssm_out_g[l].reshape(1, SSM_WIDTH), w_out, l, g1,
                                norm2_g[l].reshape(1, d), sc2, sh2, w_router_t, router_bias_col, seq)
        order, tables, n_steps_max = _moe_steps(gid.reshape(t), t)
        y_sorted = _moe(_take_rows(h2, order), _take_rows(cw.T, order), w_exp_gate, w_exp_up, w_exp_down, l,
                        tables, n_steps_max)
        y = _take_rows(y_sorted, jnp.argsort(order).astype(jnp.int32))
        xf, res = x1, (y, g2)
    y, g2 = res
    return _final(xf, y, g2, seq).reshape(batch, seq, d)
```

```python
import functools
import math

import numpy as np
import jax
import jax.numpy as jnp
from jax import lax
from jax.experimental import pallas as pl
from jax.experimental.pallas import tpu as pltpu

F32 = jnp.float32
BF16 = jnp.bfloat16

HEAD_DIM = 64
N_Q_HEADS = 8
N_KV_HEADS = 2
Q_PER_KV = N_Q_HEADS // N_KV_HEADS
ATTN_WIDTH = N_Q_HEADS * HEAD_DIM
KV_WIDTH = N_KV_HEADS * HEAD_DIM
ATTN_BLOCK = 128
ATTN_Q_TILE = 2048
ROPE_THETA = 10000.0
ROPE_SLAB = 256
LANES = 128
SSM_GROUP_CH = 16
SSM_GROUPS = 32
SSM_WIDTH = SSM_GROUPS * SSM_GROUP_CH
SSM_STATE = 64
SSM_CHUNK = 16
SSM_COLS = SSM_WIDTH // LANES
COL_GROUPS = LANES // SSM_GROUP_CH
COL_STATE = COL_GROUPS * SSM_STATE
CHUNK_LANES = SSM_CHUNK * LANES
SSM_NSPLIT = 4
S5_ROW_PAD = 8
N_EXPERTS = 16
N_EXPERT_GROUPS = 4
EXPERTS_PER_GROUP = N_EXPERTS // N_EXPERT_GROUPS
PAIRS_PER_GROUP = EXPERTS_PER_GROUP * (EXPERTS_PER_GROUP - 1) // 2
PAIR_SLOTS = ((0, 1), (0, 2), (0, 3), (1, 3), (1, 2), (3, 2))
EPS = 1e-6
LOG2_E = math.log2(math.e)
MASK_BIAS = -1e30

TOKEN_TILE = 1024
MOE_ROWS = 512
MOE_SUB = 256
VMEM_LIMIT = 48 * 1024 * 1024
BIG_VMEM_LIMIT = 56 * 1024 * 1024
STEP_PAD, STEP_LOAD, STEP_ITEM = 0, 1, 2


def _params(sem, vmem=None):
    return pltpu.CompilerParams(dimension_semantics=sem, vmem_limit_bytes=vmem)


def _rms(x, g):
    return x * lax.rsqrt(jnp.mean(x * x, axis=-1, keepdims=True) + EPS) * g


def _mod_kernel(c_ref, w_ref, b_ref, o_ref):
    c = c_ref[...]
    s = c * jax.nn.sigmoid(c)
    o_ref[...] = jnp.dot(s.astype(BF16), w_ref[...].astype(BF16), preferred_element_type=F32) + b_ref[...]


def _adaln_mod(c, ada_w, ada_b):
    depth, d, d6 = ada_w.shape
    nb = c.shape[0]
    n6 = d6 // d
    return pl.pallas_call(
        _mod_kernel,
        grid=(depth, n6),
        in_specs=[pl.BlockSpec((nb, d), lambda l, j: (0, 0)),
                  pl.BlockSpec((None, d, d), lambda l, j: (l, 0, j)),
                  pl.BlockSpec((None, None, 1, d), lambda l, j: (l, j, 0, 0))],
        out_specs=pl.BlockSpec((None, None, nb, d), lambda l, j: (l, j, 0, 0)),
        out_shape=jax.ShapeDtypeStruct((depth, n6, nb, d), F32),
        compiler_params=_params(("arbitrary", "arbitrary"), VMEM_LIMIT),
        name="adaln_mod",
    )(c, ada_w, ada_b.reshape(depth, n6, 1, d))


def _spread(x, expander3):
    hi = x.astype(BF16)
    r1 = x - hi.astype(F32)
    mid = r1.astype(BF16)
    lo = (r1 - mid.astype(F32)).astype(BF16)
    return jnp.dot(jnp.concatenate([hi, mid, lo], axis=1), expander3, preferred_element_type=F32)


def _rope_kernel(pos_ref, freq_ref, lane_ref, quarter_ref, cos_ref, sin_ref):
    per_row = pos_ref.shape[1]
    rows = pos_ref.shape[0]
    pos = _spread(pos_ref[...].astype(F32), lane_ref[...])
    ang = pos * freq_ref[...]
    cos, sin = jnp.cos(ang), jnp.sin(ang)
    for j in range(per_row):
        cos_ref[pl.ds(j, rows, stride=per_row), :] = _spread(cos, quarter_ref[j])
        sin_ref[pl.ds(j, rows, stride=per_row), :] = _spread(sin, quarter_ref[j])


def _rope_tables(positions):
    half = HEAD_DIM // 2
    t = positions.size
    per_row = LANES // half
    rows = t // per_row
    freq = (ROPE_THETA ** (-np.arange(half, dtype=np.float64) / half)).astype(np.float32)
    freq_row = jnp.asarray(np.tile(freq, per_row)[None, :])
    to_quarter = np.repeat(np.eye(per_row, dtype=np.float32), half, axis=1)
    spread = np.zeros((per_row, LANES, LANES), np.float32)
    for j in range(per_row):
        spread[j, j * half:(j + 1) * half, :] = np.tile(np.eye(half, dtype=np.float32), (1, per_row))
    lane_sel = jnp.asarray(np.tile(to_quarter, (3, 1)), BF16)
    quarter_sel = jnp.asarray(np.tile(spread, (1, 3, 1)), BF16)
    blk = min(rows, 512)
    out = pl.BlockSpec((blk * per_row, LANES), lambda i: (i, 0))
    const = lambda a: pl.BlockSpec(a.shape, lambda i: (0,) * a.ndim)
    return pl.pallas_call(
        _rope_kernel,
        grid=(rows // blk,),
        in_specs=[pl.BlockSpec((blk, per_row), lambda i: (i, 0)), const(freq_row), const(lane_sel), const(quarter_sel)],
        out_specs=[out, out],
        out_shape=[jax.ShapeDtypeStruct((t, LANES), F32)] * 2,
        compiler_params=_params(("arbitrary",)),
        name="rope_tables",
    )(positions.reshape(rows, per_row), freq_row, lane_sel, quarter_sel)


def _rope_constants():
    lane = np.arange(ROPE_SLAB)
    head_sum = (lane[:, None] // HEAD_DIM == lane[None, :] // HEAD_DIM).astype(np.float32)
    half = HEAD_DIM // 2
    rot = np.zeros((ROPE_SLAB, ROPE_SLAB), np.float32)
    for d in range(ROPE_SLAB):
        if d % HEAD_DIM < half:
            rot[d + half, d] = -1.0
        else:
            rot[d - half, d] = 1.0
    return jnp.asarray(head_sum, BF16), jnp.asarray(rot, BF16)


def _inproj_kernel(*refs, has_res):
    if has_res:
        (x_ref, y_ref, g2_ref, sc_ref, sh_ref, n1_ref, w_ref, qg_ref, kg_ref, hs_ref, rot_ref,
         cos_ref, sin_ref, q_ref, k_ref, v_ref, uc_ref, u_scr, w_s) = refs
        x = x_ref[...] + g2_ref[...] * y_ref[...]
    else:
        (x_ref, sc_ref, sh_ref, n1_ref, w_ref, qg_ref, kg_ref, hs_ref, rot_ref,
         cos_ref, sin_ref, q_ref, k_ref, v_ref, uc_ref, u_scr, w_s) = refs
        x = x_ref[...]

    @pl.when(pl.program_id(0) == 0)
    def _():
        w_s[...] = w_ref[...].astype(BF16)

    h = _rms(x, n1_ref[...] * (1.0 + sc_ref[...])) + sh_ref[...]
    proj = jnp.dot(h.astype(BF16), w_s[...], preferred_element_type=F32)
    q = proj[:, :ATTN_WIDTH]
    k = proj[:, ATTN_WIDTH:ATTN_WIDTH + KV_WIDTH]
    v = proj[:, ATTN_WIDTH + KV_WIDTH:ATTN_WIDTH + 2 * KV_WIDTH]
    cos = cos_ref[...]
    sin = sin_ref[...]
    reps = ATTN_WIDTH // LANES
    cos_q = jnp.concatenate([cos] * reps, axis=1)
    sin_q = jnp.concatenate([sin] * reps, axis=1)

    def head_norm_rope(t, gain, c, s):
        outs = []
        for lo in range(0, t.shape[1], ROPE_SLAB):
            wd = min(ROPE_SLAB, t.shape[1] - lo)
            ts, lanes = t[:, lo:lo + wd], slice(lo, lo + wd)
            ssq = jnp.dot((ts * ts).astype(BF16), hs_ref[:wd, :wd], preferred_element_type=F32)
            tn = (ts * lax.rsqrt(ssq * (1.0 / HEAD_DIM) + EPS) * gain[:, lanes]).astype(BF16)
            tr = jnp.dot(tn, rot_ref[:wd, :wd], preferred_element_type=F32)
            outs.append(tn.astype(F32) * c[:, lanes] + tr * s[:, lanes])
        return outs[0] if len(outs) == 1 else jnp.concatenate(outs, axis=1)

    qo = head_norm_rope(q, qg_ref[...], cos_q, sin_q)
    ko = head_norm_rope(k, kg_ref[...], cos, sin)
    q_ref[...] = qo.astype(BF16)
    k_ref[...] = jnp.concatenate([ko, pltpu.roll(ko, HEAD_DIM, axis=1)], axis=1).astype(BF16)
    v_ref[...] = jnp.concatenate([v, pltpu.roll(v, HEAD_DIM, axis=1)], axis=1).astype(BF16)
    u0 = ATTN_WIDTH + 2 * KV_WIDTH
    nchunk = u_scr.shape[1] // SSM_CHUNK
    for j in range(SSM_COLS):
        u_scr[j] = proj[:, u0 + j * LANES:u0 + (j + 1) * LANES]
    for s in range(SSM_CHUNK):
        for j in range(SSM_COLS):
            lanes = slice(s * SSM_WIDTH + j * LANES, s * SSM_WIDTH + (j + 1) * LANES)
            uc_ref[:, lanes] = u_scr[j, pl.ds(s, nchunk, stride=SSM_CHUNK), :].astype(BF16)


def _inproj(x, res, sc1, sh1, n1g, w_in, layer, qg, kg, head_sum, rot, cos, sin, seq):
    t, d = x.shape
    tm = min(TOKEN_TILE, seq)
    per_b = seq // tm
    in_width = w_in.shape[2]
    tok = lambda w: pl.BlockSpec((tm, w), lambda i: (i, 0))
    const = lambda a: pl.BlockSpec(a.shape, lambda i: (0,) * a.ndim)
    per_batch = pl.BlockSpec((None, 1, d), lambda i: (i // per_b, 0, 0))
    chunked = pl.BlockSpec((tm // SSM_CHUNK, SSM_CHUNK * SSM_WIDTH), lambda i: (i, 0))
    ins, specs = [x], [tok(d)]
    if res is not None:
        y_prev, g2_prev = res
        ins += [y_prev, g2_prev]
        specs += [tok(d), per_batch]
    ins += [sc1, sh1, n1g, w_in, qg, kg, head_sum, rot, cos, sin]
    specs += [per_batch, per_batch, const(n1g), pl.BlockSpec((None, d, in_width), lambda i: (layer, 0, 0)),
              const(qg), const(kg), const(head_sum), const(rot), tok(LANES), tok(LANES)]
    out_shape = [jax.ShapeDtypeStruct((t, ATTN_WIDTH), BF16), jax.ShapeDtypeStruct((t, 2 * KV_WIDTH), BF16),
                 jax.ShapeDtypeStruct((t, 2 * KV_WIDTH), BF16),
                 jax.ShapeDtypeStruct((t // SSM_CHUNK, SSM_CHUNK * SSM_WIDTH), BF16)]
    out_specs = [tok(ATTN_WIDTH), tok(2 * KV_WIDTH), tok(2 * KV_WIDTH), chunked]
    assert in_width == ATTN_WIDTH + 2 * KV_WIDTH + SSM_WIDTH
    return pl.pallas_call(
        functools.partial(_inproj_kernel, has_res=res is not None),
        grid=(t // tm,),
        in_specs=specs,
        out_specs=out_specs,
        out_shape=out_shape,
        scratch_shapes=[pltpu.VMEM((SSM_COLS, tm, LANES), F32), pltpu.VMEM((d, in_width), BF16)],
        compiler_params=_params(("arbitrary",), BIG_VMEM_LIMIT),
        name="inproj",
    )(*ins)


def _attn_kernel(sink_ref, q_ref, kc_ref, kp_ref, vc_ref, vp_ref, bias_ref, g_ref, o_ref):
    nsub = q_ref.shape[0] // ATTN_BLOCK
    kk = jnp.concatenate([kp_ref[...], kc_ref[...]], axis=0)
    vv = jnp.concatenate([vp_ref[...], vc_ref[...]], axis=0)
    low = lax.broadcasted_iota(jnp.int32, (kk.shape[0], KV_WIDTH), 1) < HEAD_DIM
    zero = jnp.zeros((kk.shape[0], KV_WIDTH), BF16)

    def variants(a):
        nat, swp = a[:, :KV_WIDTH], a[:, KV_WIDTH:]
        return {(0, 0): jnp.where(low, nat, zero), (0, 1): jnp.where(low, zero, swp),
                (1, 0): jnp.where(low, swp, zero), (1, 1): jnp.where(low, zero, nat)}

    kvar, vvar = variants(kk), variants(vv)
    band = bias_ref[1]
    first = bias_ref[jnp.minimum(pl.program_id(1), 1)]
    upper = lax.broadcasted_iota(jnp.int32, (2 * ATTN_BLOCK, 1), 0) < ATTN_BLOCK
    for j in range(nsub):
        bias = first if j == 0 else band
        bias2 = jnp.concatenate([bias, bias], axis=0)
        keys = slice(j * ATTN_BLOCK, (j + 2) * ATTN_BLOCK)
        qrows = slice(j * ATTN_BLOCK, (j + 1) * ATTN_BLOCK)
        tiles = [None] * (N_Q_HEADS // 2)
        for kv in range(N_KV_HEADS):
            for half in range(2):
                pairs = (2 * kv, 2 * kv + 1)
                heads = (2 * pairs[0] + half, 2 * pairs[1] + half)
                qs = jnp.concatenate([q_ref[qrows, p * LANES:(p + 1) * LANES] for p in pairs], axis=0)
                s = lax.dot_general(qs, kvar[(kv, half)][keys], (((1,), (1,)), ((), ())),
                                    preferred_element_type=F32) + bias2
                sink = jnp.where(upper, sink_ref[heads[0]], sink_ref[heads[1]]) * LOG2_E
                m = jnp.maximum(jnp.max(s, axis=-1, keepdims=True), sink)
                p = jnp.exp2(s - m)
                denom = jnp.sum(p, axis=-1, keepdims=True) + jnp.exp2(sink - m)
                o = jnp.dot(p.astype(BF16), vvar[(kv, half)][keys], preferred_element_type=F32) * (1.0 / denom)
                for r, pr in enumerate(pairs):
                    part = o[r * ATTN_BLOCK:(r + 1) * ATTN_BLOCK]
                    tiles[pr] = part if tiles[pr] is None else tiles[pr] + part
        a = jnp.concatenate(tiles, axis=1)
        o_ref[qrows, :] = _rms(a, g_ref[...]).astype(BF16)


def _attn_bias():
    qi = np.arange(ATTN_BLOCK)[:, None]
    sj = np.arange(2 * ATTN_BLOCK)[None, :]
    diff = qi + ATTN_BLOCK - sj
    band = (diff >= 0) & (diff < ATTN_BLOCK)
    first = band & (sj >= ATTN_BLOCK)
    return jnp.asarray(np.where(np.stack([first, band]), 0.0, MASK_BIAS).astype(np.float32))


def _attention(q, kx, vx, sink, out_g, bias, batch, seq):
    t = q.shape[0]
    qb = min(ATTN_Q_TILE, seq)
    nsub = qb // ATTN_BLOCK
    nq = seq // qb
    nb = seq // ATTN_BLOCK
    cur = lambda w: pl.BlockSpec((qb, w), lambda b, n, s: (b * nq + n, 0))
    prev = lambda w: pl.BlockSpec((ATTN_BLOCK, w), lambda b, n, s: (b * nb + jnp.maximum(n * nsub - 1, 0), 0))
    grid_spec = pltpu.PrefetchScalarGridSpec(
        num_scalar_prefetch=1,
        grid=(batch, nq),
        in_specs=[cur(ATTN_WIDTH), cur(2 * KV_WIDTH), prev(2 * KV_WIDTH), cur(2 * KV_WIDTH), prev(2 * KV_WIDTH),
                  pl.BlockSpec(bias.shape, lambda b, n, s: (0, 0, 0)),
                  pl.BlockSpec((1, ATTN_WIDTH), lambda b, n, s: (0, 0))],
        out_specs=cur(ATTN_WIDTH),
    )
    return pl.pallas_call(
        _attn_kernel,
        grid_spec=grid_spec,
        out_shape=jax.ShapeDtypeStruct((t, ATTN_WIDTH), BF16),
        compiler_params=_params(("parallel", "arbitrary")),
        name="swa_attention",
    )(sink, q, kx, kx, vx, vx, bias, out_g)


def _s5_prep_kernel(lr_re_ref, lr_im_ref, ldt_ref, bt_re_ref, bt_im_ref, ct_re_ref, ct_im_ref,
                    d_ref, lcol_re_ref, lcol_im_ref, ldtcol_ref, exp_ref, exph_ref, expt_ref, expw_ref,
                    t_ref, w_ref, v_ref, la_ref, lb_ref):
    hi = lax.Precision.HIGHEST
    nl = SSM_CHUNK
    low = lax.broadcasted_iota(jnp.int32, (1, 2 * SSM_STATE), 1) < SSM_STATE
    row_low = lax.broadcasted_iota(jnp.int32, (2 * SSM_STATE, 1), 0) < SSM_STATE
    jcol = lax.broadcasted_iota(jnp.int32, (nl, 1), 0).astype(F32)
    kt_lane = lax.broadcasted_iota(jnp.int32, (SSM_GROUP_CH, nl * SSM_GROUP_CH), 1)
    kt_row = lax.broadcasted_iota(jnp.int32, (SSM_GROUP_CH, nl * SSM_GROUP_CH), 0)

    w_all, v_all, kt_all = [], [], []
    for gm in range(COL_GROUPS):
        dt = jnp.exp(ldt_ref[gm])
        lam_re, lam_im = lr_re_ref[gm], lr_im_ref[gm]
        a_r, th_r = lam_re * dt, lam_im * dt

        er = jnp.exp(jcol * a_r)
        pw_re, pw_im = er * jnp.cos(jcol * th_r), er * jnp.sin(jcol * th_r)

        nr, ni = pw_re[1:2, :] - 1.0, pw_im[1:2, :]
        den = lam_re * lam_re + lam_im * lam_im
        c_re, c_im = (nr * lam_re + ni * lam_im) / den, (ni * lam_re - nr * lam_im) / den
        bt_re, bt_im = bt_re_ref[gm], bt_im_ref[gm]
        bb_re, bb_im = c_re * bt_re - c_im * bt_im, c_re * bt_im + c_im * bt_re

        w_rows = []
        for s in range(nl):
            j = nl - 1 - s
            pr, pi = pw_re[j:j + 1, :], pw_im[j:j + 1, :]
            w_rows.append(jnp.where(low, pr * bb_re - pi * bb_im, pr * bb_im + pi * bb_re))
        w_all.append(w_rows)

        pw_re_t, pw_im_t = pw_re.T, pw_im.T
        pc, ps = _spread(pw_re_t, exp_ref[...]), _spread(pw_im_t, exp_ref[...])
        ct_re, ct_im = _spread(ct_re_ref[gm], exph_ref[...]), _spread(ct_im_ref[gm], exph_ref[...])
        a_re, a_im = ct_re * pc - ct_im * ps, ct_re * ps + ct_im * pc
        a_cat = jnp.where(row_low, a_re, -a_im)
        l1_re, l1_im = pw_re_t[:, 1:2], pw_im_t[:, 1:2]
        v_re, v_im = a_re * l1_re - a_im * l1_im, a_re * l1_im + a_im * l1_re
        v_all.append(jnp.where(row_low, v_re, -v_im))

        kt = jnp.dot(jnp.where(low, bb_re, bb_im), a_cat, precision=hi, preferred_element_type=F32)
        kt_all.append(kt + jnp.where(kt_lane == kt_row, d_ref[gm], 0.0))

    def same_group(shape, row_group, lane_group):
        r = lax.broadcasted_iota(jnp.int32, shape, 0)
        c = lax.broadcasted_iota(jnp.int32, shape, 1)
        return (row_group(r) == lane_group(c)).astype(F32)

    chan_group = lambda i: (i >> 4) & (COL_GROUPS - 1)
    state_group = lambda i: (i >> 6) & (COL_GROUPS - 1)
    over_steps = lambda m: jnp.concatenate([m] * nl, axis=1)

    kt_wide = jnp.dot(jnp.concatenate(kt_all, axis=0).astype(BF16), expt_ref[...], preferred_element_type=F32)
    bd = (kt_wide * over_steps(same_group((LANES, LANES), chan_group, chan_group))).astype(BF16)
    t_ref[0:LANES, :] = bd
    for s in range(1, nl):
        t_ref[s * LANES:(s + 1) * LANES, :] = jnp.concatenate(
            [jnp.zeros((LANES, s * LANES), BF16), bd[:, :CHUNK_LANES - s * LANES]], axis=1)

    w_stack = jnp.concatenate([w_all[gm][s] for s in range(nl) for gm in range(COL_GROUPS)], axis=0)
    w_wide = jnp.dot(w_stack.astype(BF16), expw_ref[...], preferred_element_type=F32)
    w_mask = same_group((LANES, 2 * COL_STATE), chan_group, state_group)
    w_ref[...] = (w_wide.reshape(nl, LANES, 2 * COL_STATE) * w_mask[None]).reshape(CHUNK_LANES, 2 * COL_STATE).astype(BF16)

    v_stack = jnp.concatenate([v_all[gm][half * SSM_STATE:(half + 1) * SSM_STATE, :]
                               for half in range(2) for gm in range(COL_GROUPS)], axis=0)
    v_wide = jnp.dot(v_stack.astype(BF16), expt_ref[...], preferred_element_type=F32)
    v_ref[...] = (v_wide * over_steps(same_group((2 * COL_STATE, LANES), state_group, chan_group))).astype(BF16)

    dtc = jnp.exp(ldtcol_ref[...])
    e16 = jnp.exp(nl * lcol_re_ref[...] * dtc)
    ang = nl * lcol_im_ref[...] * dtc
    la_ref[...] = e16 * jnp.cos(ang)
    lb_ref[...] = e16 * jnp.sin(ang)


def _s5_prep(lam_re, lam_im, b_re, b_im, c_re, c_im, d_skip, log_dt):
    g, p, h, nl = SSM_GROUPS, SSM_STATE, SSM_GROUP_CH, SSM_CHUNK
    cg = COL_GROUPS
    nc = lam_re.shape[0] * SSM_COLS
    col = lambda a: a.reshape((nc, cg) + a.shape[2:])
    dup_row = lambda a: col(jnp.tile(a, (1, 1, 2))[:, :, None, :])
    bt = lambda a: col(jnp.tile(jnp.swapaxes(a, 2, 3), (1, 1, 1, 2)))
    ct = lambda a: col(jnp.tile(jnp.swapaxes(a, 2, 3), (1, 1, 2, 1)))
    d_pad = col(jnp.pad(d_skip.reshape(-1, g, 1, h), ((0, 0), (0, 0), (0, 0), (0, nl * h - h))))
    wide = lambda a: a.reshape(nc, 1, cg * p)
    expand = jnp.asarray(np.tile(np.repeat(np.eye(nl, dtype=np.float32), h, axis=1), (3, 1)), BF16)
    expand_h = jnp.asarray(np.tile(np.eye(h, dtype=np.float32), (3, nl)), BF16)
    exp_t = np.zeros((nl, h, nl, cg, h), np.float32)
    exp_w = np.zeros((2, p, 2, cg, p), np.float32)
    for gm in range(cg):
        exp_t[:, :, :, gm, :] = np.eye(nl * h, dtype=np.float32).reshape(nl, h, nl, h)
        exp_w[:, :, :, gm, :] = np.eye(2 * p, dtype=np.float32).reshape(2, p, 2, p)
    exp_t = jnp.asarray(exp_t.reshape(nl * h, CHUNK_LANES), BF16)
    exp_w = jnp.asarray(exp_w.reshape(2 * p, 2 * COL_STATE), BF16)
    blk = lambda *s: pl.BlockSpec((None,) + s, lambda i: (i,) + (0,) * len(s))
    const = lambda a: pl.BlockSpec(a.shape, lambda i: (0,) * a.ndim)
    lw = nl * h
    return pl.pallas_call(
        _s5_prep_kernel,
        grid=(nc,),
        in_specs=[blk(cg, 1, 2 * p), blk(cg, 1, 2 * p), blk(cg, 1, 1),
                  blk(cg, h, 2 * p), blk(cg, h, 2 * p), blk(cg, 2 * p, h), blk(cg, 2 * p, h), blk(cg, 1, lw),
                  blk(1, cg * p), blk(1, cg * p), blk(1, cg * p), const(expand), const(expand_h), const(exp_t),
                  const(exp_w)],
        out_specs=[blk(CHUNK_LANES, CHUNK_LANES), blk(CHUNK_LANES, 2 * COL_STATE), blk(2 * COL_STATE, CHUNK_LANES),
                   blk(1, COL_STATE), blk(1, COL_STATE)],
        out_shape=[jax.ShapeDtypeStruct((nc, CHUNK_LANES, CHUNK_LANES), BF16),
                   jax.ShapeDtypeStruct((nc, CHUNK_LANES, 2 * COL_STATE), BF16),
                   jax.ShapeDtypeStruct((nc, 2 * COL_STATE, CHUNK_LANES), BF16),
                   jax.ShapeDtypeStruct((nc, 1, COL_STATE), F32), jax.ShapeDtypeStruct((nc, 1, COL_STATE), F32)],
        compiler_params=_params(("parallel",), VMEM_LIMIT),
        name="s5_prep",
    )(dup_row(lam_re), dup_row(lam_im), col(log_dt[:, :, None, None]),
      bt(b_re), bt(b_im), ct(c_re), ct(c_im), d_pad, wide(lam_re), wide(lam_im),
      wide(jnp.repeat(log_dt, p, axis=1)), expand, expand_h, exp_t, exp_w)


def _s5_kernel(*refs, nchunks, nb):
    uc_refs = refs[:SSM_CHUNK]
    t_ref, w_ref, v_ref, la_ref, lb_ref, o_ref, ucat_ref, s_ref, xp_ref = refs[SSM_CHUNK:]

    @pl.when(pl.program_id(1) == 0)
    def _():
        for s in range(SSM_CHUNK):
            ucat_ref[:, s * LANES:(s + 1) * LANES] = uc_refs[s][...]
        s_in = jnp.dot(ucat_ref[...], w_ref[...], preferred_element_type=F32)
        nblk = COL_STATE // LANES
        pitch = s_ref.shape[1] // nb
        for b in range(2 * nblk):
            for q in range(nb):
                s_ref[b, q * pitch:q * pitch + nchunks, :] = s_in[q * nchunks:(q + 1) * nchunks, b * LANES:(b + 1) * LANES]
        lr = [jnp.broadcast_to(la_ref[:, b * LANES:(b + 1) * LANES], (nb, LANES)) for b in range(nblk)]
        li = [jnp.broadcast_to(lb_ref[:, b * LANES:(b + 1) * LANES], (nb, LANES)) for b in range(nblk)]

        def step(c, carry):
            rows = pl.ds(c, nb, stride=pitch)
            out = []
            for b in range(nblk):
                re, im = carry[2 * b], carry[2 * b + 1]
                xp_ref[b, rows, :] = re
                xp_ref[nblk + b, rows, :] = im
                out.append(lr[b] * re - li[b] * im + s_ref[b, rows, :])
                out.append(lr[b] * im + li[b] * re + s_ref[nblk + b, rows, :])
            return tuple(out)

        zero = jnp.zeros((nb, LANES), F32)
        lax.fori_loop(0, nchunks, step, (zero,) * (2 * nblk), unroll=4)

    pitch = xp_ref.shape[1] // nb
    xp = jnp.concatenate(
        [jnp.concatenate([xp_ref[b, q * pitch:q * pitch + nchunks, :] for q in range(nb)], axis=0)
         for b in range(2 * COL_STATE // LANES)], axis=1).astype(BF16)
    inter = jnp.dot(xp, v_ref[...], preferred_element_type=F32)
    for kk in range(SSM_NSPLIT):
        @pl.when(pl.program_id(1) == kk)
        def _():
            live = (kk + 1) * (CHUNK_LANES // SSM_NSPLIT)
            intra = jnp.dot(ucat_ref[:, :live], t_ref[:live, :], preferred_element_type=F32)
            o_ref[...] = (intra + inter).astype(BF16)


def _s5_scan(uc, mats, layer, nchunks, nb):
    rows = uc.shape[0]
    c0 = layer * SSM_COLS
    split = CHUNK_LANES // SSM_NSPLIT
    u_spec = lambda s: pl.BlockSpec((rows, LANES), lambda j, k: (0, SSM_COLS * s + j))
    return pl.pallas_call(
        functools.partial(_s5_kernel, nchunks=nchunks, nb=nb),
        grid=(SSM_COLS, SSM_NSPLIT),
        in_specs=[u_spec(s) for s in range(SSM_CHUNK)] + [
            pl.BlockSpec((None, CHUNK_LANES, split), lambda j, k: (c0 + j, 0, k)),
            pl.BlockSpec((None, CHUNK_LANES, 2 * COL_STATE), lambda j, k: (c0 + j, 0, 0)),
            pl.BlockSpec((None, 2 * COL_STATE, split), lambda j, k: (c0 + j, 0, k)),
            pl.BlockSpec((None, 1, COL_STATE), lambda j, k: (c0 + j, 0, 0)),
            pl.BlockSpec((None, 1, COL_STATE), lambda j, k: (c0 + j, 0, 0))],
        out_specs=pl.BlockSpec((None, rows, split), lambda j, k: (j, 0, k)),
        out_shape=jax.ShapeDtypeStruct((SSM_COLS, rows, CHUNK_LANES), BF16),
        scratch_shapes=[pltpu.VMEM((rows, CHUNK_LANES), BF16),
                        pltpu.VMEM((2 * COL_STATE // LANES, nb * (nchunks + S5_ROW_PAD), LANES), F32),
                        pltpu.VMEM((2 * COL_STATE // LANES, nb * (nchunks + S5_ROW_PAD), LANES), F32)],
        compiler_params=_params(("parallel", "arbitrary"), VMEM_LIMIT),
        name="s5_scan",
    )(*([uc] * SSM_CHUNK), *mats)


def _route(logits, bias):
    m = jnp.max(logits, axis=0, keepdims=True)
    e = jnp.exp(logits - m)
    probs = e / jnp.sum(e, axis=0, keepdims=True)
    sel = probs + bias
    row = lambda a, i: a[i:i + 1, :]
    best_score, best = None, None
    for grp in range(N_EXPERT_GROUPS):
        a, b, c, d = (row(sel, EXPERTS_PER_GROUP * grp + i) for i in range(EXPERTS_PER_GROUP))
        hab, lab, hcd, lcd = jnp.maximum(a, b), jnp.minimum(a, b), jnp.maximum(c, d), jnp.minimum(c, d)
        top1 = jnp.maximum(hab, hcd)
        top2 = jnp.maximum(jnp.maximum(lab, lcd), jnp.minimum(hab, hcd))
        score = top1 + top2
        if grp == 0:
            best_score, best = score, jnp.zeros(score.shape, jnp.int32)
        else:
            better = score > best_score
            best = jnp.where(better, grp, best)
            best_score = jnp.where(better, score, best_score)

    def pick(a, i):
        out = row(a, i)
        for grp in range(1, N_EXPERT_GROUPS):
            out = jnp.where(best == grp, row(a, EXPERTS_PER_GROUP * grp + i), out)
        return out

    s_in = [pick(sel, i) for i in range(EXPERTS_PER_GROUP)]
    p_in = [pick(probs, i) for i in range(EXPERTS_PER_GROUP)]
    neg = jnp.full(s_in[0].shape, -jnp.inf, F32)

    def argmax_first(vals):
        idx, val = jnp.zeros(vals[0].shape, jnp.int32), vals[0]
        for i in range(1, len(vals)):
            better = vals[i] > val
            idx = jnp.where(better, i, idx)
            val = jnp.where(better, vals[i], val)
        return idx

    i1 = argmax_first(s_in)
    i2 = argmax_first([jnp.where(i1 == i, neg, s_in[i]) for i in range(EXPERTS_PER_GROUP)])
    zero = jnp.zeros(p_in[0].shape, F32)
    g1 = sum(jnp.where(i1 == i, p_in[i], zero) for i in range(EXPERTS_PER_GROUP))
    g2 = sum(jnp.where(i2 == i, p_in[i], zero) for i in range(EXPERTS_PER_GROUP))
    tot = g1 + g2
    w1, w2 = g1 / tot, g2 / tot
    first_low = i1 < i2
    low, high = jnp.minimum(i1, i2), jnp.maximum(i1, i2)
    w_low, w_high = jnp.where(first_low, w1, w2), jnp.where(first_low, w2, w1)
    pos = jnp.where(low == 0, high - 1, jnp.where(low == 1, jnp.where(high == 2, 4, 3), 5))
    swap = low == 2
    bucket = best * PAIRS_PER_GROUP + pos
    return jnp.concatenate([jnp.where(swap, w_high, w_low), jnp.where(swap, w_low, w_high)], axis=0), bucket


def _router_logits(w_t, h):
    w_hi = w_t.astype(BF16)
    w_r = w_t - w_hi.astype(F32)
    w_mid = w_r.astype(BF16)
    w_lo = (w_r - w_mid.astype(F32)).astype(BF16)
    h_hi = h.astype(BF16)
    h_lo = (h - h_hi.astype(F32)).astype(BF16)
    dims = (((1,), (1,)), ((), ()))
    a = lax.dot_general(jnp.concatenate([w_hi, w_mid, w_lo], axis=0), h_hi, dims, preferred_element_type=F32)
    b = lax.dot_general(jnp.concatenate([w_hi, w_mid], axis=0), h_lo, dims, preferred_element_type=F32)
    e = w_t.shape[0]
    return a[:e] + a[e:2 * e] + a[2 * e:] + b[:e] + b[e:]


def _post_kernel(*refs, has_res):
    if has_res:
        x_ref, yp_ref, g2_ref, *refs = refs
        x = x_ref[...] + g2_ref[...] * yp_ref[...]
    else:
        x_ref, *refs = refs
        x = x_ref[...]
    (at_ref, yc_ref, wglu_ref, gs_ref, wo_ref, g1_ref, n2_ref, sc_ref, sh_ref,
     wrt_ref, rb_ref, x1_ref, h2_ref, cw_ref, gid_ref, y_scr, wglu_s, wo_s) = refs

    @pl.when(pl.program_id(0) == 0)
    def _():
        wglu_s[...] = wglu_ref[...].astype(BF16)
        wo_s[...] = wo_ref[...].astype(BF16)

    nchunk = y_scr.shape[1] // SSM_CHUNK
    for s in range(SSM_CHUNK):
        for j in range(SSM_COLS):
            y_scr[j, pl.ds(s, nchunk, stride=SSM_CHUNK), :] = yc_ref[j, :, s * LANES:(s + 1) * LANES].astype(F32)
    yg = jax.nn.gelu(jnp.concatenate([y_scr[j] for j in range(SSM_COLS)], axis=1))
    z = yg * jax.nn.sigmoid(jnp.dot(yg.astype(BF16), wglu_s[...], preferred_element_type=F32))
    zn = _rms(z, gs_ref[...]).astype(BF16)
    o = (jnp.dot(at_ref[...], wo_s[:ATTN_WIDTH, :], preferred_element_type=F32)
         + jnp.dot(zn, wo_s[ATTN_WIDTH:, :], preferred_element_type=F32))
    x1 = x + g1_ref[...] * o
    x1_ref[...] = x1
    h2 = _rms(x1, n2_ref[...] * (1.0 + sc_ref[...])) + sh_ref[...]
    h2_ref[...] = h2
    logits = _router_logits(wrt_ref[...], h2)
    cw, bucket = _route(logits, rb_ref[...])
    cw_ref[...] = cw
    gid_ref[...] = bucket


def _post(x, res, attn, yc, w_glu, ssm_g, w_out, layer, g1, n2g, sc2, sh2, w_router_t, router_bias, seq):
    t, d = x.shape
    tm = min(TOKEN_TILE, seq)
    per_b = seq // tm
    tok = lambda w: pl.BlockSpec((tm, w), lambda i: (i, 0))
    const = lambda a: pl.BlockSpec(a.shape, lambda i: (0,) * a.ndim)
    per_batch = pl.BlockSpec((None, 1, d), lambda i: (i // per_b, 0, 0))
    col = lambda r: pl.BlockSpec((r, tm), lambda i: (0, i))
    of_layer = lambda a: pl.BlockSpec((None,) + a.shape[1:], lambda i: (layer, 0, 0))
    chunked = pl.BlockSpec((SSM_COLS, tm // SSM_CHUNK, CHUNK_LANES), lambda i: (0, i, 0))
    ins, specs = [x], [tok(d)]
    if res is not None:
        y_prev, g2_prev = res
        ins += [y_prev, g2_prev]
        specs += [tok(d), per_batch]
    return pl.pallas_call(
        functools.partial(_post_kernel, has_res=res is not None),
        grid=(t // tm,),
        in_specs=specs + [tok(ATTN_WIDTH), chunked, of_layer(w_glu), const(ssm_g),
                  of_layer(w_out), per_batch, const(n2g), per_batch, per_batch, const(w_router_t), const(router_bias)],
        out_specs=[tok(d), tok(d), col(2), col(1)],
        out_shape=[jax.ShapeDtypeStruct((t, d), F32), jax.ShapeDtypeStruct((t, d), F32),
                   jax.ShapeDtypeStruct((2, t), F32), jax.ShapeDtypeStruct((1, t), jnp.int32)],
        scratch_shapes=[pltpu.VMEM((SSM_COLS, tm, LANES), F32), pltpu.VMEM(w_glu.shape[1:], BF16),
                        pltpu.VMEM(w_out.shape[1:], BF16)],
        compiler_params=_params(("arbitrary",), BIG_VMEM_LIMIT),
        name="post_mix",
    )(*ins, attn, yc, w_glu, ssm_g, w_out, g1, n2g, sc2, sh2, w_router_t, router_bias)


def _moe_kernel(kind_ref, rb_ref, bk_ref, pa_ref, pb_ref, first_ref, cast_ref, cpos_ref, pe_ref, offs_ref,
                x_ref, cw_ref, wg_ref, wu_ref, wd_ref, o_ref, wg_s, wu_s, wd_s):
    s = pl.program_id(0)

    @pl.when(cast_ref[s] == 1)
    def _():
        slot = cpos_ref[s]
        wg_s[slot] = wg_ref[...].astype(BF16)
        wu_s[slot] = wu_ref[...].astype(BF16)
        wd_s[slot] = wd_ref[...].astype(BF16)

    @pl.when(kind_ref[s] == STEP_ITEM)
    def _():
        bucket = bk_ref[s]
        base = rb_ref[s] * MOE_ROWS
        lo_row, hi_row = offs_ref[bucket] - base, offs_ref[bucket + 1] - base
        slots = (pa_ref[s], pb_ref[s])
        is_first = first_ref[s] == 1

        def run(r0, r1, z0, z1):
            rows = r0 + lax.broadcasted_iota(jnp.int32, (r1 - r0, 1), 0)
            cw = jnp.where((rows >= lo_row) & (rows < hi_row), cw_ref[r0:r1, :], 0.0)
            x = x_ref[r0:r1, :].astype(BF16)
            y = None
            for k in range(2):
                gate = jnp.dot(x, wg_s[slots[k]], preferred_element_type=F32)
                up = jnp.dot(x, wu_s[slots[k]], preferred_element_type=F32)
                act = (gate * jax.nn.sigmoid(gate) * up * cw[:, k:k + 1]).astype(BF16)
                yk = jnp.dot(act, wd_s[slots[k]], preferred_element_type=F32)
                y = yk if y is None else y + yk

            @pl.when(is_first)
            def _():
                o_ref[r0:r1, :] = y
                for a, b in ((z0, r0), (r1, z1)):
                    if b > a:
                        o_ref[a:b, :] = jnp.zeros((b - a, o_ref.shape[1]), F32)

            @pl.when(jnp.logical_not(is_first))
            def _():
                o_ref[r0:r1, :] += y

        for b0 in range(0, MOE_ROWS, MOE_SUB):
            b1, mid = b0 + MOE_SUB, b0 + MOE_SUB // 2
            has_rows = (lo_row < b1) & (hi_row > b0)
            needs_lower, needs_upper = has_rows & (lo_row < mid), has_rows & (hi_row > mid)
            pl.when(needs_lower & needs_upper)(lambda: run(b0, b1, b0, b1))
            pl.when(needs_lower & jnp.logical_not(needs_upper))(lambda: run(b0, mid, b0, b1))
            pl.when(jnp.logical_not(needs_lower) & needs_upper)(lambda: run(mid, b1, b0, b1))
            if MOE_ROWS > MOE_SUB:
                @pl.when(jnp.logical_not(has_rows) & is_first)
                def _():
                    o_ref[b0:b1, :] = jnp.zeros((MOE_SUB, o_ref.shape[1]), F32)


def _moe_plan_kernel(offs_ref, kind_ref, rb_ref, bk_ref, pa_ref, pb_ref, first_ref, cast_ref, cpos_ref, pe_ref,
                     irb, ibk, *, n_steps_max):
    i32 = jnp.int32
    ng, epg, ppg = N_EXPERT_GROUPS, EXPERTS_PER_GROUP, PAIRS_PER_GROUP

    shift = MOE_ROWS.bit_length() - 1

    def bucket_body(bk, cnt):
        a, b = offs_ref[bk], offs_ref[bk + 1]
        first_blk = lax.shift_right_logical(a, shift)
        n_blk = jnp.where(b > a, lax.shift_right_logical(b - 1, shift) - first_blk + 1, 0)

        def block_body(j, cnt):
            irb[cnt] = first_blk + j
            ibk[cnt] = bk
            return cnt + 1

        return lax.fori_loop(0, n_blk, block_body, cnt)

    n_items = lax.fori_loop(0, ng * ppg, bucket_body, i32(0))

    def count_body(i, m):
        g = ibk[i] // ppg
        return tuple(m[k] + (g == k).astype(i32) for k in range(ng))

    m = lax.fori_loop(0, n_items, count_body, (i32(0),) * ng)

    def next_group(g):
        nxt = i32(-1)
        for k in range(ng - 1, 0, -1):
            nxt = jnp.where((k > g) & (m[k] > 0), k, nxt)
        return nxt

    def emit(s, kind, rb, bk, pa, pb, first, cast, cpos, pe):
        kind_ref[s], rb_ref[s], bk_ref[s], pa_ref[s], pb_ref[s] = kind, rb, bk, pa, pb
        first_ref[s], cast_ref[s], cpos_ref[s], pe_ref[s] = first, cast, cpos, pe

    def item_body(i, carry):
        s, gcur, parity, q, last_pe, last_rb = carry
        rb, bk = irb[i], ibk[i]
        g, pos = bk // ppg, bk % ppg
        new = g != gcur
        started = gcur >= 0
        loaders = jnp.where(new, jnp.where(started, jnp.maximum(epg - q, 0), epg), 0)
        parity = jnp.where(new & started, 1 - parity, parity)
        q = jnp.where(new, 0, q)
        for j in range(epg):
            on = j >= epg - loaders
            emit(s, STEP_LOAD, rb, bk, 0, 0, 0, 1, parity * epg + j, epg * g + j)
            last_pe = jnp.where(on, epg * g + j, last_pe)
            s = s + on.astype(i32)
        nxt = next_group(g)
        pre = (q < epg) & (nxt >= 0)
        pe = jnp.where(pre, epg * nxt + q, last_pe)
        slot_a, slot_b = i32(PAIR_SLOTS[0][0]), i32(PAIR_SLOTS[0][1])
        for p in range(1, ppg):
            slot_a = jnp.where(pos == p, PAIR_SLOTS[p][0], slot_a)
            slot_b = jnp.where(pos == p, PAIR_SLOTS[p][1], slot_b)
        emit(s, STEP_ITEM, rb, bk, parity * epg + slot_a, parity * epg + slot_b, (rb != last_rb).astype(i32),
             pre.astype(i32), (1 - parity) * epg + q, pe)
        return s + 1, g, parity, q + 1, pe, rb

    s, _, _, _, last_pe, last_rb = lax.fori_loop(
        0, n_items, item_body, (i32(0), i32(-1), i32(0), i32(0), i32(0), i32(-1)))
    last_bk = ibk[jnp.maximum(n_items - 1, 0)]

    def pad_body(s, _):
        emit(s, STEP_PAD, last_rb, last_bk, 0, 0, 0, 0, 0, last_pe)
        return 0

    lax.fori_loop(s, n_steps_max, pad_body, 0)


def _moe_steps(bucket, t):
    i32 = jnp.int32
    nbk = N_EXPERT_GROUPS * PAIRS_PER_GROUP
    order = jnp.argsort(bucket, stable=True).astype(i32)
    counts = jnp.sum((bucket[None, :] == jnp.arange(nbk, dtype=i32)[:, None]).astype(i32), axis=1)
    offs = jnp.concatenate([jnp.zeros((1,), i32), jnp.cumsum(counts).astype(i32)])
    assert MOE_ROWS & (MOE_ROWS - 1) == 0
    n_items_max = t // MOE_ROWS + nbk - 1
    n_steps_max = n_items_max + N_EXPERTS
    smem = pl.BlockSpec(memory_space=pltpu.SMEM)
    tables = pl.pallas_call(
        functools.partial(_moe_plan_kernel, n_steps_max=n_steps_max),
        in_specs=[smem],
        out_specs=[smem] * 9,
        out_shape=[jax.ShapeDtypeStruct((n_steps_max,), i32)] * 9,
        scratch_shapes=[pltpu.SMEM((n_items_max + 1,), i32)] * 2,
        name="moe_plan",
    )(offs)
    return order, (*tables, offs), n_steps_max


def _moe(xs, cws, w_gate, w_up, w_down, layer, tables, n_steps_max):
    t, d = xs.shape
    ff = w_gate.shape[3]
    w_map = lambda s, kind, rb, bk, pa, pb, fi, ca, cp, pe, of: (layer, pe[s], 0, 0)
    row_map = lambda s, kind, rb, *_: (rb[s], 0)
    nres = 2 * EXPERTS_PER_GROUP
    grid_spec = pltpu.PrefetchScalarGridSpec(
        num_scalar_prefetch=len(tables),
        grid=(n_steps_max,),
        in_specs=[pl.BlockSpec((MOE_ROWS, d), row_map), pl.BlockSpec((MOE_ROWS, 2), row_map),
                  pl.BlockSpec((None, None, d, ff), w_map), pl.BlockSpec((None, None, d, ff), w_map),
                  pl.BlockSpec((None, None, ff, d), w_map)],
        out_specs=pl.BlockSpec((MOE_ROWS, d), row_map),
        scratch_shapes=[pltpu.VMEM((nres, d, ff), BF16), pltpu.VMEM((nres, d, ff), BF16),
                        pltpu.VMEM((nres, ff, d), BF16)],
    )
    return pl.pallas_call(
        _moe_kernel,
        grid_spec=grid_spec,
        out_shape=jax.ShapeDtypeStruct((t, d), F32),
        compiler_params=_params(("arbitrary",), BIG_VMEM_LIMIT),
        name="moe_grouped",
    )(*tables, xs, cws, w_gate, w_up, w_down)


def _take_rows(a, idx):
    return a.at[idx].get(mode="promise_in_bounds", unique_indices=True)


def _final_kernel(x_ref, y_ref, g_ref, o_ref):
    o_ref[...] = x_ref[...] + g_ref[...] * y_ref[...]


def _final(x1, y, g2, seq):
    t, d = x1.shape
    tm = min(TOKEN_TILE, seq)
    per_b = seq // tm
    tok = lambda w: pl.BlockSpec((tm, w), lambda i: (i, 0))
    return pl.pallas_call(
        _final_kernel,
        grid=(t // tm,),
        in_specs=[tok(d), tok(d), pl.BlockSpec((None, 1, d), lambda i: (i // per_b, 0, 0))],
        out_specs=tok(d),
        out_shape=jax.ShapeDtypeStruct((t, d), F32),
        compiler_params=_params(("parallel",)),
        name="final_residual",
    )(x1, y, g2)


def kernel(x, c, positions, ada_w, ada_b, norm1_g, w_in, q_norm_g, k_norm_g, attn_sink, lam_re, lam_im, ssm_b_re, ssm_b_im, ssm_c_re, ssm_c_im, ssm_d, ssm_log_dt, w_glu, attn_out_g, ssm_out_g, w_out, norm2_g, w_router, router_bias, w_exp_gate, w_exp_up, w_exp_down):
    batch, seq, d = x.shape
    depth = ada_w.shape[0]
    t = batch * seq
    assert seq % ATTN_BLOCK == 0 and seq % SSM_CHUNK == 0 and t % MOE_ROWS == 0

    mod = _adaln_mod(c, ada_w, ada_b).reshape(depth, 6, batch, 1, d)
    cos, sin = _rope_tables(positions)
    head_sum, rot = _rope_constants()
    bias = _attn_bias()
    w_router_t = w_router.T
    s5_mats = _s5_prep(lam_re, lam_im, ssm_b_re, ssm_b_im, ssm_c_re, ssm_c_im, ssm_d, ssm_log_dt)
    router_bias_col = router_bias.reshape(N_EXPERTS, 1)

    xf = x.reshape(t, d)
    res = None
    for l in range(depth):
        sh1, sc1, g1, sh2, sc2, g2 = (mod[l, j] for j in range(6))
        qg = (jnp.tile(q_norm_g[l], N_Q_HEADS) * (HEAD_DIM ** -0.5 * LOG2_E)).reshape(1, ATTN_WIDTH)
        kg = jnp.tile(k_norm_g[l], N_KV_HEADS).reshape(1, KV_WIDTH)
        q, kx, vx, uc = _inproj(xf, res, sc1, sh1, norm1_g[l].reshape(1, d), w_in, l, qg, kg, head_sum, rot,
                                cos, sin, seq)
        attn = _attention(q, kx, vx, attn_sink[l], attn_out_g[l].reshape(1, ATTN_WIDTH), bias, batch, seq)
        yc = _s5_scan(uc, s5_mats, l, seq // SSM_CHUNK, batch)
        x1, h2, cw, gid = _post(xf, res, attn, yc, w_glu, ssm_out_g[l].reshape(1, SSM_WIDTH), w_out, l, g1,
                                norm2_g[l].reshape(1, d), sc2, sh2, w_router_t, router_bias_col, seq)
        order, tables, n_steps_max = _moe_steps(gid.reshape(t), t)
        y_sorted = _moe(_take_rows(h2, order), _take_rows(cw.T, order), w_exp_gate, w_exp_up, w_exp_down, l,
                        tables, n_steps_max)
        y = _take_rows(y_sorted, jnp.argsort(order).astype(jnp.int32))
        xf, res = x1, (y, g2)
    y, g2 = res
    return _final(xf, y, g2, seq).reshape(batch, seq, d)
```

```python
import functools
import math

import numpy as np
import jax
import jax.numpy as jnp
from jax import lax
from jax.experimental import pallas as pl
from jax.experimental.pallas import tpu as pltpu

F32 = jnp.float32
BF16 = jnp.bfloat16

HEAD_DIM = 64
N_Q_HEADS = 8
N_KV_HEADS = 2
Q_PER_KV = N_Q_HEADS // N_KV_HEADS
ATTN_WIDTH = N_Q_HEADS * HEAD_DIM
KV_WIDTH = N_KV_HEADS * HEAD_DIM
ATTN_BLOCK = 128
ATTN_Q_TILE = 2048
ROPE_THETA = 10000.0
ROPE_SLAB = 256
LANES = 128
SSM_GROUP_CH = 16
SSM_GROUPS = 32
SSM_WIDTH = SSM_GROUPS * SSM_GROUP_CH
SSM_STATE = 64
SSM_CHUNK = 16
SSM_COLS = SSM_WIDTH // LANES
COL_GROUPS = LANES // SSM_GROUP_CH
COL_STATE = COL_GROUPS * SSM_STATE
CHUNK_LANES = SSM_CHUNK * LANES
SSM_NSPLIT = 4
S5_ROW_PAD = 8
N_EXPERTS = 16
N_EXPERT_GROUPS = 4
EXPERTS_PER_GROUP = N_EXPERTS // N_EXPERT_GROUPS
PAIRS_PER_GROUP = EXPERTS_PER_GROUP * (EXPERTS_PER_GROUP - 1) // 2
PAIR_SLOTS = ((0, 1), (0, 2), (0, 3), (1, 3), (1, 2), (3, 2))
EPS = 1e-6
LOG2_E = math.log2(math.e)
MASK_BIAS = -1e30

TOKEN_TILE = 1024
MOE_ROWS = 512
MOE_SUB = 256
VMEM_LIMIT = 48 * 1024 * 1024
BIG_VMEM_LIMIT = 56 * 1024 * 1024
STEP_PAD, STEP_LOAD, STEP_ITEM = 0, 1, 2


def _params(sem, vmem=None):
    return pltpu.CompilerParams(dimension_semantics=sem, vmem_limit_bytes=vmem)


def _rms(x, g):
    return x * lax.rsqrt(jnp.mean(x * x, axis=-1, keepdims=True) + EPS) * g


def _mod_kernel(c_ref, w_ref, b_ref, o_ref):
    c = c_ref[...]
    s = c * jax.nn.sigmoid(c)
    o_ref[...] = jnp.dot(s.astype(BF16), w_ref[...].astype(BF16), preferred_element_type=F32) + b_ref[...]


def _adaln_mod(c, ada_w, ada_b):
    depth, d, d6 = ada_w.shape
    nb = c.shape[0]
    n6 = d6 // d
    return pl.pallas_call(
        _mod_kernel,
        grid=(depth, n6),
        in_specs=[pl.BlockSpec((nb, d), lambda l, j: (0, 0)),
                  pl.BlockSpec((None, d, d), lambda l, j: (l, 0, j)),
                  pl.BlockSpec((None, None, 1, d), lambda l, j: (l, j, 0, 0))],
        out_specs=pl.BlockSpec((None, None, nb, d), lambda l, j: (l, j, 0, 0)),
        out_shape=jax.ShapeDtypeStruct((depth, n6, nb, d), F32),
        compiler_params=_params(("arbitrary", "arbitrary"), VMEM_LIMIT),
        name="adaln_mod",
    )(c, ada_w, ada_b.reshape(depth, n6, 1, d))


def _spread(x, expander3):
    hi = x.astype(BF16)
    r1 = x - hi.astype(F32)
    mid = r1.astype(BF16)
    lo = (r1 - mid.astype(F32)).astype(BF16)
    return jnp.dot(jnp.concatenate([hi, mid, lo], axis=1), expander3, preferred_element_type=F32)


def _rope_kernel(pos_ref, freq_ref, lane_ref, quarter_ref, cos_ref, sin_ref):
    per_row = pos_ref.shape[1]
    rows = pos_ref.shape[0]
    pos = _spread(pos_ref[...].astype(F32), lane_ref[...])
    ang = pos * freq_ref[...]
    cos, sin = jnp.cos(ang), jnp.sin(ang)
    for j in range(per_row):
        cos_ref[pl.ds(j, rows, stride=per_row), :] = _spread(cos, quarter_ref[j])
        sin_ref[pl.ds(j, rows, stride=per_row), :] = _spread(sin, quarter_ref[j])


def _rope_tables(positions):
    half = HEAD_DIM // 2
    t = positions.size
    per_row = LANES // half
    rows = t // per_row
    freq = (ROPE_THETA ** (-np.arange(half, dtype=np.float64) / half)).astype(np.float32)
    freq_row = jnp.asarray(np.tile(freq, per_row)[None, :])
    to_quarter = np.repeat(np.eye(per_row, dtype=np.float32), half, axis=1)
    spread = np.zeros((per_row, LANES, LANES), np.float32)
    for j in range(per_row):
        spread[j, j * half:(j + 1) * half, :] = np.tile(np.eye(half, dtype=np.float32), (1, per_row))
    lane_sel = jnp.asarray(np.tile(to_quarter, (3, 1)), BF16)
    quarter_sel = jnp.asarray(np.tile(spread, (1, 3, 1)), BF16)
    blk = min(rows, 512)
    out = pl.BlockSpec((blk * per_row, LANES), lambda i: (i, 0))
    const = lambda a: pl.BlockSpec(a.shape, lambda i: (0,) * a.ndim)
    return pl.pallas_call(
        _rope_kernel,
        grid=(rows // blk,),
        in_specs=[pl.BlockSpec((blk, per_row), lambda i: (i, 0)), const(freq_row), const(lane_sel), const(quarter_sel)],
        out_specs=[out, out],
        out_shape=[jax.ShapeDtypeStruct((t, LANES), F32)] * 2,
        compiler_params=_params(("arbitrary",)),
        name="rope_tables",
    )(positions.reshape(rows, per_row), freq_row, lane_sel, quarter_sel)


def _rope_constants():
    lane = np.arange(ROPE_SLAB)
    head_sum = (lane[:, None] // HEAD_DIM == lane[None, :] // HEAD_DIM).astype(np.float32)
    half = HEAD_DIM // 2
    rot = np.zeros((ROPE_SLAB, ROPE_SLAB), np.float32)
    for d in range(ROPE_SLAB):
        if d % HEAD_DIM < half:
            rot[d + half, d] = -1.0
        else:
            rot[d - half, d] = 1.0
    return jnp.asarray(head_sum, BF16), jnp.asarray(rot, BF16)


def _inproj_kernel(*refs, has_res):
    if has_res:
        (x_ref, y_ref, g2_ref, sc_ref, sh_ref, n1_ref, w_ref, qg_ref, kg_ref, hs_ref, rot_ref,
         cos_ref, sin_ref, q_ref, k_ref, v_ref, uc_ref, xo_ref, u_scr, w_s) = refs
        x = x_ref[...] + g2_ref[...] * y_ref[...]
        xo_ref[...] = x
    else:
        (x_ref, sc_ref, sh_ref, n1_ref, w_ref, qg_ref, kg_ref, hs_ref, rot_ref,
         cos_ref, sin_ref, q_ref, k_ref, v_ref, uc_ref, u_scr, w_s) = refs
        x = x_ref[...]

    @pl.when(pl.program_id(0) == 0)
    def _():
        w_s[...] = w_ref[...].astype(BF16)

    h = _rms(x, n1_ref[...] * (1.0 + sc_ref[...])) + sh_ref[...]
    proj = jnp.dot(h.astype(BF16), w_s[...], preferred_element_type=F32)
    q = proj[:, :ATTN_WIDTH]
    k = proj[:, ATTN_WIDTH:ATTN_WIDTH + KV_WIDTH]
    v = proj[:, ATTN_WIDTH + KV_WIDTH:ATTN_WIDTH + 2 * KV_WIDTH]
    cos = cos_ref[...]
    sin = sin_ref[...]
    reps = ATTN_WIDTH // LANES
    cos_q = jnp.concatenate([cos] * reps, axis=1)
    sin_q = jnp.concatenate([sin] * reps, axis=1)

    def head_norm_rope(t, gain, c, s):
        outs = []
        for lo in range(0, t.shape[1], ROPE_SLAB):
            wd = min(ROPE_SLAB, t.shape[1] - lo)
            ts, lanes = t[:, lo:lo + wd], slice(lo, lo + wd)
            ssq = jnp.dot((ts * ts).astype(BF16), hs_ref[:wd, :wd], preferred_element_type=F32)
            tn = (ts * lax.rsqrt(ssq * (1.0 / HEAD_DIM) + EPS) * gain[:, lanes]).astype(BF16)
            tr = jnp.dot(tn, rot_ref[:wd, :wd], preferred_element_type=F32)
            outs.append(tn.astype(F32) * c[:, lanes] + tr * s[:, lanes])
        return outs[0] if len(outs) == 1 else jnp.concatenate(outs, axis=1)

    qo = head_norm_rope(q, qg_ref[...], cos_q, sin_q)
    ko = head_norm_rope(k, kg_ref[...], cos, sin)
    q_ref[...] = qo.astype(BF16)
    k_ref[...] = jnp.concatenate([ko, pltpu.roll(ko, HEAD_DIM, axis=1)], axis=1).astype(BF16)
    v_ref[...] = jnp.concatenate([v, pltpu.roll(v, HEAD_DIM, axis=1)], axis=1).astype(BF16)
    u0 = ATTN_WIDTH + 2 * KV_WIDTH
    nchunk = u_scr.shape[1] // SSM_CHUNK
    for j in range(SSM_COLS):
        u_scr[j] = proj[:, u0 + j * LANES:u0 + (j + 1) * LANES]
    for s in range(SSM_CHUNK):
        for j in range(SSM_COLS):
            lanes = slice(s * SSM_WIDTH + j * LANES, s * SSM_WIDTH + (j + 1) * LANES)
            uc_ref[:, lanes] = u_scr[j, pl.ds(s, nchunk, stride=SSM_CHUNK), :].astype(BF16)


def _inproj(x, res, sc1, sh1, n1g, w_in, layer, qg, kg, head_sum, rot, cos, sin, seq):
    t, d = x.shape
    tm = min(TOKEN_TILE, seq)
    per_b = seq // tm
    in_width = w_in.shape[2]
    tok = lambda w: pl.BlockSpec((tm, w), lambda i: (i, 0))
    const = lambda a: pl.BlockSpec(a.shape, lambda i: (0,) * a.ndim)
    per_batch = pl.BlockSpec((None, 1, d), lambda i: (i // per_b, 0, 0))
    chunked = pl.BlockSpec((tm // SSM_CHUNK, SSM_CHUNK * SSM_WIDTH), lambda i: (i, 0))
    ins, specs = [x], [tok(d)]
    if res is not None:
        y_prev, g2_prev = res
        ins += [y_prev, g2_prev]
        specs += [tok(d), per_batch]
    ins += [sc1, sh1, n1g, w_in, qg, kg, head_sum, rot, cos, sin]
    specs += [per_batch, per_batch, const(n1g),
              pl.BlockSpec((None, d, in_width), lambda i: (layer, 0, 0), pipeline_mode=pl.Buffered(1)),
              const(qg), const(kg), const(head_sum), const(rot), tok(LANES), tok(LANES)]
    out_shape = [jax.ShapeDtypeStruct((t, ATTN_WIDTH), BF16), jax.ShapeDtypeStruct((t, 2 * KV_WIDTH), BF16),
                 jax.ShapeDtypeStruct((t, 2 * KV_WIDTH), BF16),
                 jax.ShapeDtypeStruct((t // SSM_CHUNK, SSM_CHUNK * SSM_WIDTH), BF16)]
    out_specs = [tok(ATTN_WIDTH), tok(2 * KV_WIDTH), tok(2 * KV_WIDTH), chunked]
    if res is not None:
        out_shape.append(jax.ShapeDtypeStruct((t, d), F32))
        out_specs.append(tok(d))
    assert in_width == ATTN_WIDTH + 2 * KV_WIDTH + SSM_WIDTH
    return pl.pallas_call(
        functools.partial(_inproj_kernel, has_res=res is not None),
        grid=(t // tm,),
        in_specs=specs,
        out_specs=out_specs,
        out_shape=out_shape,
        scratch_shapes=[pltpu.VMEM((SSM_COLS, tm, LANES), F32), pltpu.VMEM((d, in_width), BF16)],
        compiler_params=_params(("arbitrary",), VMEM_LIMIT),
        name="inproj",
    )(*ins)


def _attn_kernel(sink_ref, q_ref, kc_ref, kp_ref, vc_ref, vp_ref, bias_ref, g_ref, o_ref):
    nsub = q_ref.shape[0] // ATTN_BLOCK
    kk = jnp.concatenate([kp_ref[...], kc_ref[...]], axis=0)
    vv = jnp.concatenate([vp_ref[...], vc_ref[...]], axis=0)
    low = lax.broadcasted_iota(jnp.int32, (kk.shape[0], KV_WIDTH), 1) < HEAD_DIM
    zero = jnp.zeros((kk.shape[0], KV_WIDTH), BF16)

    def variants(a):
        nat, swp = a[:, :KV_WIDTH], a[:, KV_WIDTH:]
        return {(0, 0): jnp.where(low, nat, zero), (0, 1): jnp.where(low, zero, swp),
                (1, 0): jnp.where(low, swp, zero), (1, 1): jnp.where(low, zero, nat)}

    kvar, vvar = variants(kk), variants(vv)
    band = bias_ref[1]
    first = bias_ref[jnp.minimum(pl.program_id(1), 1)]
    upper = lax.broadcasted_iota(jnp.int32, (2 * ATTN_BLOCK, 1), 0) < ATTN_BLOCK
    for j in range(nsub):
        bias = first if j == 0 else band
        bias2 = jnp.concatenate([bias, bias], axis=0)
        keys = slice(j * ATTN_BLOCK, (j + 2) * ATTN_BLOCK)
        qrows = slice(j * ATTN_BLOCK, (j + 1) * ATTN_BLOCK)
        tiles = [None] * (N_Q_HEADS // 2)
        for kv in range(N_KV_HEADS):
            for half in range(2):
                pairs = (2 * kv, 2 * kv + 1)
                heads = (2 * pairs[0] + half, 2 * pairs[1] + half)
                qs = jnp.concatenate([q_ref[qrows, p * LANES:(p + 1) * LANES] for p in pairs], axis=0)
                s = lax.dot_general(qs, kvar[(kv, half)][keys], (((1,), (1,)), ((), ())),
                                    preferred_element_type=F32) + bias2
                sink = jnp.where(upper, sink_ref[heads[0]], sink_ref[heads[1]]) * LOG2_E
                m = jnp.maximum(jnp.max(s, axis=-1, keepdims=True), sink)
                p = jnp.exp2(s - m)
                denom = jnp.sum(p, axis=-1, keepdims=True) + jnp.exp2(sink - m)
                o = jnp.dot(p.astype(BF16), vvar[(kv, half)][keys], preferred_element_type=F32) * (1.0 / denom)
                for r, pr in enumerate(pairs):
                    part = o[r * ATTN_BLOCK:(r + 1) * ATTN_BLOCK]
                    tiles[pr] = part if tiles[pr] is None else tiles[pr] + part
        a = jnp.concatenate(tiles, axis=1)
        o_ref[qrows, :] = _rms(a, g_ref[...]).astype(BF16)


def _attn_bias():
    qi = np.arange(ATTN_BLOCK)[:, None]
    sj = np.arange(2 * ATTN_BLOCK)[None, :]
    diff = qi + ATTN_BLOCK - sj
    band = (diff >= 0) & (diff < ATTN_BLOCK)
    first = band & (sj >= ATTN_BLOCK)
    return jnp.asarray(np.where(np.stack([first, band]), 0.0, MASK_BIAS).astype(np.float32))


def _attention(q, kx, vx, sink, out_g, bias, batch, seq):
    t = q.shape[0]
    qb = min(ATTN_Q_TILE, seq)
    nsub = qb // ATTN_BLOCK
    nq = seq // qb
    nb = seq // ATTN_BLOCK
    cur = lambda w: pl.BlockSpec((qb, w), lambda b, n, s: (b * nq + n, 0))
    prev = lambda w: pl.BlockSpec((ATTN_BLOCK, w), lambda b, n, s: (b * nb + jnp.maximum(n * nsub - 1, 0), 0))
    grid_spec = pltpu.PrefetchScalarGridSpec(
        num_scalar_prefetch=1,
        grid=(batch, nq),
        in_specs=[cur(ATTN_WIDTH), cur(2 * KV_WIDTH), prev(2 * KV_WIDTH), cur(2 * KV_WIDTH), prev(2 * KV_WIDTH),
                  pl.BlockSpec(bias.shape, lambda b, n, s: (0, 0, 0)),
                  pl.BlockSpec((1, ATTN_WIDTH), lambda b, n, s: (0, 0))],
        out_specs=cur(ATTN_WIDTH),
    )
    return pl.pallas_call(
        _attn_kernel,
        grid_spec=grid_spec,
        out_shape=jax.ShapeDtypeStruct((t, ATTN_WIDTH), BF16),
        compiler_params=_params(("parallel", "arbitrary")),
        name="swa_attention",
    )(sink, q, kx, kx, vx, vx, bias, out_g)


def _s5_prep_kernel(lr_re_ref, lr_im_ref, ldt_ref, bt_re_ref, bt_im_ref, ct_re_ref, ct_im_ref,
                    d_ref, lcol_re_ref, lcol_im_ref, ldtcol_ref, exp_ref, exph_ref, expt_ref, expw_ref,
                    t_ref, w_ref, v_ref, la_ref, lb_ref):
    hi = lax.Precision.HIGHEST
    nl = SSM_CHUNK
    low = lax.broadcasted_iota(jnp.int32, (1, 2 * SSM_STATE), 1) < SSM_STATE
    row_low = lax.broadcasted_iota(jnp.int32, (2 * SSM_STATE, 1), 0) < SSM_STATE
    jcol = lax.broadcasted_iota(jnp.int32, (nl, 1), 0).astype(F32)
    kt_lane = lax.broadcasted_iota(jnp.int32, (SSM_GROUP_CH, nl * SSM_GROUP_CH), 1)
    kt_row = lax.broadcasted_iota(jnp.int32, (SSM_GROUP_CH, nl * SSM_GROUP_CH), 0)

    w_all, v_all, kt_all = [], [], []
    for gm in range(COL_GROUPS):
        dt = jnp.exp(ldt_ref[gm])
        lam_re, lam_im = lr_re_ref[gm], lr_im_ref[gm]
        a_r, th_r = lam_re * dt, lam_im * dt

        er = jnp.exp(jcol * a_r)
        pw_re, pw_im = er * jnp.cos(jcol * th_r), er * jnp.sin(jcol * th_r)

        nr, ni = pw_re[1:2, :] - 1.0, pw_im[1:2, :]
        den = lam_re * lam_re + lam_im * lam_im
        c_re, c_im = (nr * lam_re + ni * lam_im) / den, (ni * lam_re - nr * lam_im) / den
        bt_re, bt_im = bt_re_ref[gm], bt_im_ref[gm]
        bb_re, bb_im = c_re * bt_re - c_im * bt_im, c_re * bt_im + c_im * bt_re

        w_rows = []
        for s in range(nl):
            j = nl - 1 - s
            pr, pi = pw_re[j:j + 1, :], pw_im[j:j + 1, :]
            w_rows.append(jnp.where(low, pr * bb_re - pi * bb_im, pr * bb_im + pi * bb_re))
        w_all.append(w_rows)

        pw_re_t, pw_im_t = pw_re.T, pw_im.T
        pc, ps = _spread(pw_re_t, exp_ref[...]), _spread(pw_im_t, exp_ref[...])
        ct_re, ct_im = _spread(ct_re_ref[gm], exph_ref[...]), _spread(ct_im_ref[gm], exph_ref[...])
        a_re, a_im = ct_re * pc - ct_im * ps, ct_re * ps + ct_im * pc
        a_cat = jnp.where(row_low, a_re, -a_im)
        l1_re, l1_im = pw_re_t[:, 1:2], pw_im_t[:, 1:2]
        v_re, v_im = a_re * l1_re - a_im * l1_im, a_re * l1_im + a_im * l1_re
        v_all.append(jnp.where(row_low, v_re, -v_im))

        kt = jnp.dot(jnp.where(low, bb_re, bb_im), a_cat, precision=hi, preferred_element_type=F32)
        kt_all.append(kt + jnp.where(kt_lane == kt_row, d_ref[gm], 0.0))

    def same_group(shape, row_group, lane_group):
        r = lax.broadcasted_iota(jnp.int32, shape, 0)
        c = lax.broadcasted_iota(jnp.int32, shape, 1)
        return (row_group(r) == lane_group(c)).astype(F32)

    chan_group = lambda i: (i >> 4) & (COL_GROUPS - 1)
    state_group = lambda i: (i >> 6) & (COL_GROUPS - 1)
    over_steps = lambda m: jnp.concatenate([m] * nl, axis=1)

    kt_wide = jnp.dot(jnp.concatenate(kt_all, axis=0).astype(BF16), expt_ref[...], preferred_element_type=F32)
    bd = (kt_wide * over_steps(same_group((LANES, LANES), chan_group, chan_group))).astype(BF16)
    t_ref[0:LANES, :] = bd
    for s in range(1, nl):
        t_ref[s * LANES:(s + 1) * LANES, :] = jnp.concatenate(
            [jnp.zeros((LANES, s * LANES), BF16), bd[:, :CHUNK_LANES - s * LANES]], axis=1)

    w_stack = jnp.concatenate([w_all[gm][s] for s in range(nl) for gm in range(COL_GROUPS)], axis=0)
    w_wide = jnp.dot(w_stack.astype(BF16), expw_ref[...], preferred_element_type=F32)
    w_mask = same_group((LANES, 2 * COL_STATE), chan_group, state_group)
    w_ref[...] = (w_wide.reshape(nl, LANES, 2 * COL_STATE) * w_mask[None]).reshape(CHUNK_LANES, 2 * COL_STATE).astype(BF16)

    v_stack = jnp.concatenate([v_all[gm][half * SSM_STATE:(half + 1) * SSM_STATE, :]
                               for half in range(2) for gm in range(COL_GROUPS)], axis=0)
    v_wide = jnp.dot(v_stack.astype(BF16), expt_ref[...], preferred_element_type=F32)
    v_ref[...] = (v_wide * over_steps(same_group((2 * COL_STATE, LANES), state_group, chan_group))).astype(BF16)

    dtc = jnp.exp(ldtcol_ref[...])
    e16 = jnp.exp(nl * lcol_re_ref[...] * dtc)
    ang = nl * lcol_im_ref[...] * dtc
    la_ref[...] = e16 * jnp.cos(ang)
    lb_ref[...] = e16 * jnp.sin(ang)


def _s5_prep(lam_re, lam_im, b_re, b_im, c_re, c_im, d_skip, log_dt):
    g, p, h, nl = SSM_GROUPS, SSM_STATE, SSM_GROUP_CH, SSM_CHUNK
    cg = COL_GROUPS
    nc = lam_re.shape[0] * SSM_COLS
    col = lambda a: a.reshape((nc, cg) + a.shape[2:])
    dup_row = lambda a: col(jnp.tile(a, (1, 1, 2))[:, :, None, :])
    bt = lambda a: col(jnp.tile(jnp.swapaxes(a, 2, 3), (1, 1, 1, 2)))
    ct = lambda a: col(jnp.tile(jnp.swapaxes(a, 2, 3), (1, 1, 2, 1)))
    d_pad = col(jnp.pad(d_skip.reshape(-1, g, 1, h), ((0, 0), (0, 0), (0, 0), (0, nl * h - h))))
    wide = lambda a: a.reshape(nc, 1, cg * p)
    expand = jnp.asarray(np.tile(np.repeat(np.eye(nl, dtype=np.float32), h, axis=1), (3, 1)), BF16)
    expand_h = jnp.asarray(np.tile(np.eye(h, dtype=np.float32), (3, nl)), BF16)
    exp_t = np.zeros((nl, h, nl, cg, h), np.float32)
    exp_w = np.zeros((2, p, 2, cg, p), np.float32)
    for gm in range(cg):
        exp_t[:, :, :, gm, :] = np.eye(nl * h, dtype=np.float32).reshape(nl, h, nl, h)
        exp_w[:, :, :, gm, :] = np.eye(2 * p, dtype=np.float32).reshape(2, p, 2, p)
    exp_t = jnp.asarray(exp_t.reshape(nl * h, CHUNK_LANES), BF16)
    exp_w = jnp.asarray(exp_w.reshape(2 * p, 2 * COL_STATE), BF16)
    blk = lambda *s: pl.BlockSpec((None,) + s, lambda i: (i,) + (0,) * len(s))
    const = lambda a: pl.BlockSpec(a.shape, lambda i: (0,) * a.ndim)
    lw = nl * h
    return pl.pallas_call(
        _s5_prep_kernel,
        grid=(nc,),
        in_specs=[blk(cg, 1, 2 * p), blk(cg, 1, 2 * p), blk(cg, 1, 1),
                  blk(cg, h, 2 * p), blk(cg, h, 2 * p), blk(cg, 2 * p, h), blk(cg, 2 * p, h), blk(cg, 1, lw),
                  blk(1, cg * p), blk(1, cg * p), blk(1, cg * p), const(expand), const(expand_h), const(exp_t),
                  const(exp_w)],
        out_specs=[blk(CHUNK_LANES, CHUNK_LANES), blk(CHUNK_LANES, 2 * COL_STATE), blk(2 * COL_STATE, CHUNK_LANES),
                   blk(1, COL_STATE), blk(1, COL_STATE)],
        out_shape=[jax.ShapeDtypeStruct((nc, CHUNK_LANES, CHUNK_LANES), BF16),
                   jax.ShapeDtypeStruct((nc, CHUNK_LANES, 2 * COL_STATE), BF16),
                   jax.ShapeDtypeStruct((nc, 2 * COL_STATE, CHUNK_LANES), BF16),
                   jax.ShapeDtypeStruct((nc, 1, COL_STATE), F32), jax.ShapeDtypeStruct((nc, 1, COL_STATE), F32)],
        compiler_params=_params(("parallel",), VMEM_LIMIT),
        name="s5_prep",
    )(dup_row(lam_re), dup_row(lam_im), col(log_dt[:, :, None, None]),
      bt(b_re), bt(b_im), ct(c_re), ct(c_im), d_pad, wide(lam_re), wide(lam_im),
      wide(jnp.repeat(log_dt, p, axis=1)), expand, expand_h, exp_t, exp_w)


def _s5_kernel(*refs, nchunks, nb):
    uc_refs = refs[:SSM_CHUNK]
    t_ref, w_ref, v_ref, la_ref, lb_ref, o_ref, ucat_ref, s_ref, xp_ref = refs[SSM_CHUNK:]

    @pl.when(pl.program_id(1) == 0)
    def _():
        for s in range(SSM_CHUNK):
            ucat_ref[:, s * LANES:(s + 1) * LANES] = uc_refs[s][...]
        s_in = jnp.dot(ucat_ref[...], w_ref[...], preferred_element_type=F32)
        nblk = COL_STATE // LANES
        pitch = s_ref.shape[1] // nb
        for b in range(2 * nblk):
            for q in range(nb):
                s_ref[b, q * pitch:q * pitch + nchunks, :] = s_in[q * nchunks:(q + 1) * nchunks, b * LANES:(b + 1) * LANES]
        lr = [jnp.broadcast_to(la_ref[:, b * LANES:(b + 1) * LANES], (nb, LANES)) for b in range(nblk)]
        li = [jnp.broadcast_to(lb_ref[:, b * LANES:(b + 1) * LANES], (nb, LANES)) for b in range(nblk)]

        def step(c, carry):
            rows = pl.ds(c, nb, stride=pitch)
            out = []
            for b in range(nblk):
                re, im = carry[2 * b], carry[2 * b + 1]
                xp_ref[b, rows, :] = re
                xp_ref[nblk + b, rows, :] = im
                out.append(lr[b] * re - li[b] * im + s_ref[b, rows, :])
                out.append(lr[b] * im + li[b] * re + s_ref[nblk + b, rows, :])
            return tuple(out)

        zero = jnp.zeros((nb, LANES), F32)
        lax.fori_loop(0, nchunks, step, (zero,) * (2 * nblk), unroll=4)

    pitch = xp_ref.shape[1] // nb
    xp = jnp.concatenate(
        [jnp.concatenate([xp_ref[b, q * pitch:q * pitch + nchunks, :] for q in range(nb)], axis=0)
         for b in range(2 * COL_STATE // LANES)], axis=1).astype(BF16)
    inter = jnp.dot(xp, v_ref[...], preferred_element_type=F32)
    for kk in range(SSM_NSPLIT):
        @pl.when(pl.program_id(1) == kk)
        def _():
            live = (kk + 1) * (CHUNK_LANES // SSM_NSPLIT)
            intra = jnp.dot(ucat_ref[:, :live], t_ref[:live, :], preferred_element_type=F32)
            o_ref[...] = (intra + inter).astype(BF16)


def _s5_scan(uc, mats, layer, nchunks, nb):
    rows = uc.shape[0]
    c0 = layer * SSM_COLS
    split = CHUNK_LANES // SSM_NSPLIT
    u_spec = lambda s: pl.BlockSpec((rows, LANES), lambda j, k: (0, SSM_COLS * s + j))
    return pl.pallas_call(
        functools.partial(_s5_kernel, nchunks=nchunks, nb=nb),
        grid=(SSM_COLS, SSM_NSPLIT),
        in_specs=[u_spec(s) for s in range(SSM_CHUNK)] + [
            pl.BlockSpec((None, CHUNK_LANES, split), lambda j, k: (c0 + j, 0, k)),
            pl.BlockSpec((None, CHUNK_LANES, 2 * COL_STATE), lambda j, k: (c0 + j, 0, 0)),
            pl.BlockSpec((None, 2 * COL_STATE, split), lambda j, k: (c0 + j, 0, k)),
            pl.BlockSpec((None, 1, COL_STATE), lambda j, k: (c0 + j, 0, 0)),
            pl.BlockSpec((None, 1, COL_STATE), lambda j, k: (c0 + j, 0, 0))],
        out_specs=pl.BlockSpec((None, rows, split), lambda j, k: (j, 0, k)),
        out_shape=jax.ShapeDtypeStruct((SSM_COLS, rows, CHUNK_LANES), BF16),
        scratch_shapes=[pltpu.VMEM((rows, CHUNK_LANES), BF16),
                        pltpu.VMEM((2 * COL_STATE // LANES, nb * (nchunks + S5_ROW_PAD), LANES), F32),
                        pltpu.VMEM((2 * COL_STATE // LANES, nb * (nchunks + S5_ROW_PAD), LANES), F32)],
        compiler_params=_params(("parallel", "arbitrary"), VMEM_LIMIT),
        name="s5_scan",
    )(*([uc] * SSM_CHUNK), *mats)


def _route(logits, bias):
    m = jnp.max(logits, axis=0, keepdims=True)
    e = jnp.exp(logits - m)
    probs = e / jnp.sum(e, axis=0, keepdims=True)
    sel = probs + bias
    row = lambda a, i: a[i:i + 1, :]
    best_score, best = None, None
    for grp in range(N_EXPERT_GROUPS):
        a, b, c, d = (row(sel, EXPERTS_PER_GROUP * grp + i) for i in range(EXPERTS_PER_GROUP))
        hab, lab, hcd, lcd = jnp.maximum(a, b), jnp.minimum(a, b), jnp.maximum(c, d), jnp.minimum(c, d)
        top1 = jnp.maximum(hab, hcd)
        top2 = jnp.maximum(jnp.maximum(lab, lcd), jnp.minimum(hab, hcd))
        score = top1 + top2
        if grp == 0:
            best_score, best = score, jnp.zeros(score.shape, jnp.int32)
        else:
            better = score > best_score
            best = jnp.where(better, grp, best)
            best_score = jnp.where(better, score, best_score)

    def pick(a, i):
        out = row(a, i)
        for grp in range(1, N_EXPERT_GROUPS):
            out = jnp.where(best == grp, row(a, EXPERTS_PER_GROUP * grp + i), out)
        return out

    s_in = [pick(sel, i) for i in range(EXPERTS_PER_GROUP)]
    p_in = [pick(probs, i) for i in range(EXPERTS_PER_GROUP)]
    neg = jnp.full(s_in[0].shape, -jnp.inf, F32)

    def argmax_first(vals):
        idx, val = jnp.zeros(vals[0].shape, jnp.int32), vals[0]
        for i in range(1, len(vals)):
            better = vals[i] > val
            idx = jnp.where(better, i, idx)
            val = jnp.where(better, vals[i], val)
        return idx

    i1 = argmax_first(s_in)
    i2 = argmax_first([jnp.where(i1 == i, neg, s_in[i]) for i in range(EXPERTS_PER_GROUP)])
    zero = jnp.zeros(p_in[0].shape, F32)
    g1 = sum(jnp.where(i1 == i, p_in[i], zero) for i in range(EXPERTS_PER_GROUP))
    g2 = sum(jnp.where(i2 == i, p_in[i], zero) for i in range(EXPERTS_PER_GROUP))
    tot = g1 + g2
    w1, w2 = g1 / tot, g2 / tot
    first_low = i1 < i2
    low, high = jnp.minimum(i1, i2), jnp.maximum(i1, i2)
    w_low, w_high = jnp.where(first_low, w1, w2), jnp.where(first_low, w2, w1)
    pos = jnp.where(low == 0, high - 1, jnp.where(low == 1, jnp.where(high == 2, 4, 3), 5))
    swap = low == 2
    bucket = best * PAIRS_PER_GROUP + pos
    return jnp.concatenate([jnp.where(swap, w_high, w_low), jnp.where(swap, w_low, w_high)], axis=0), bucket


def _router_logits(w_t, h):
    w_hi = w_t.astype(BF16)
    w_r = w_t - w_hi.astype(F32)
    w_mid = w_r.astype(BF16)
    w_lo = (w_r - w_mid.astype(F32)).astype(BF16)
    h_hi = h.astype(BF16)
    h_lo = (h - h_hi.astype(F32)).astype(BF16)
    dims = (((1,), (1,)), ((), ()))
    a = lax.dot_general(jnp.concatenate([w_hi, w_mid, w_lo], axis=0), h_hi, dims, preferred_element_type=F32)
    b = lax.dot_general(jnp.concatenate([w_hi, w_mid], axis=0), h_lo, dims, preferred_element_type=F32)
    e = w_t.shape[0]
    return a[:e] + a[e:2 * e] + a[2 * e:] + b[:e] + b[e:]


def _post_kernel(x_ref, at_ref, yc_ref, wglu_ref, gs_ref, wo_ref, g1_ref, n2_ref, sc_ref, sh_ref,
                 wrt_ref, rb_ref, x1_ref, h2_ref, cw_ref, gid_ref, y_scr, wglu_s, wo_s):
    @pl.when(pl.program_id(0) == 0)
    def _():
        wglu_s[...] = wglu_ref[...].astype(BF16)
        wo_s[...] = wo_ref[...].astype(BF16)

    nchunk = y_scr.shape[1] // SSM_CHUNK
    for s in range(SSM_CHUNK):
        for j in range(SSM_COLS):
            y_scr[j, pl.ds(s, nchunk, stride=SSM_CHUNK), :] = yc_ref[j, :, s * LANES:(s + 1) * LANES].astype(F32)
    yg = jax.nn.gelu(jnp.concatenate([y_scr[j] for j in range(SSM_COLS)], axis=1))
    z = yg * jax.nn.sigmoid(jnp.dot(yg.astype(BF16), wglu_s[...], preferred_element_type=F32))
    zn = _rms(z, gs_ref[...]).astype(BF16)
    o = (jnp.dot(at_ref[...], wo_s[:ATTN_WIDTH, :], preferred_element_type=F32)
         + jnp.dot(zn, wo_s[ATTN_WIDTH:, :], preferred_element_type=F32))
    x1 = x_ref[...] + g1_ref[...] * o
    x1_ref[...] = x1
    h2 = _rms(x1, n2_ref[...] * (1.0 + sc_ref[...])) + sh_ref[...]
    h2_ref[...] = h2
    logits = _router_logits(wrt_ref[...], h2)
    cw, bucket = _route(logits, rb_ref[...])
    cw_ref[...] = cw
    gid_ref[...] = bucket


def _post(x, attn, yc, w_glu, ssm_g, w_out, layer, g1, n2g, sc2, sh2, w_router_t, router_bias, seq):
    t, d = x.shape
    tm = min(TOKEN_TILE, seq)
    per_b = seq // tm
    tok = lambda w: pl.BlockSpec((tm, w), lambda i: (i, 0))
    const = lambda a: pl.BlockSpec(a.shape, lambda i: (0,) * a.ndim)
    per_batch = pl.BlockSpec((None, 1, d), lambda i: (i // per_b, 0, 0))
    col = lambda r: pl.BlockSpec((r, tm), lambda i: (0, i))
    of_layer = lambda a: pl.BlockSpec((None,) + a.shape[1:], lambda i: (layer, 0, 0), pipeline_mode=pl.Buffered(1))
    chunked = pl.BlockSpec((SSM_COLS, tm // SSM_CHUNK, CHUNK_LANES), lambda i: (0, i, 0))
    return pl.pallas_call(
        _post_kernel,
        grid=(t // tm,),
        in_specs=[tok(d), tok(ATTN_WIDTH), chunked, of_layer(w_glu), const(ssm_g),
                  of_layer(w_out), per_batch, const(n2g), per_batch, per_batch, const(w_router_t), const(router_bias)],
        out_specs=[tok(d), tok(d), col(2), col(1)],
        out_shape=[jax.ShapeDtypeStruct((t, d), F32), jax.ShapeDtypeStruct((t, d), F32),
                   jax.ShapeDtypeStruct((2, t), F32), jax.ShapeDtypeStruct((1, t), jnp.int32)],
        scratch_shapes=[pltpu.VMEM((SSM_COLS, tm, LANES), F32), pltpu.VMEM(w_glu.shape[1:], BF16),
                        pltpu.VMEM(w_out.shape[1:], BF16)],
        compiler_params=_params(("arbitrary",), VMEM_LIMIT),
        name="post_mix",
    )(x, attn, yc, w_glu, ssm_g, w_out, g1, n2g, sc2, sh2, w_router_t, router_bias)


def _moe_kernel(kind_ref, rb_ref, bk_ref, pa_ref, pb_ref, first_ref, cast_ref, cpos_ref, pe_ref, offs_ref,
                x_ref, cw_ref, wg_ref, wu_ref, wd_ref, o_ref, wg_s, wu_s, wd_s):
    s = pl.program_id(0)

    @pl.when(cast_ref[s] == 1)
    def _():
        slot = cpos_ref[s]
        wg_s[slot] = wg_ref[...].astype(BF16)
        wu_s[slot] = wu_ref[...].astype(BF16)
        wd_s[slot] = wd_ref[...].astype(BF16)

    @pl.when(kind_ref[s] == STEP_ITEM)
    def _():
        bucket = bk_ref[s]
        base = rb_ref[s] * MOE_ROWS
        lo_row, hi_row = offs_ref[bucket] - base, offs_ref[bucket + 1] - base
        slots = (pa_ref[s], pb_ref[s])
        is_first = first_ref[s] == 1

        def run(r0, r1, z0, z1):
            rows = r0 + lax.broadcasted_iota(jnp.int32, (r1 - r0, 1), 0)
            cw = jnp.where((rows >= lo_row) & (rows < hi_row), cw_ref[r0:r1, :], 0.0)
            x = x_ref[r0:r1, :].astype(BF16)
            y = None
            for k in range(2):
                gate = jnp.dot(x, wg_s[slots[k]], preferred_element_type=F32)
                up = jnp.dot(x, wu_s[slots[k]], preferred_element_type=F32)
                act = (gate * jax.nn.sigmoid(gate) * up * cw[:, k:k + 1]).astype(BF16)
                yk = jnp.dot(act, wd_s[slots[k]], preferred_element_type=F32)
                y = yk if y is None else y + yk

            @pl.when(is_first)
            def _():
                o_ref[r0:r1, :] = y
                for a, b in ((z0, r0), (r1, z1)):
                    if b > a:
                        o_ref[a:b, :] = jnp.zeros((b - a, o_ref.shape[1]), F32)

            @pl.when(jnp.logical_not(is_first))
            def _():
                o_ref[r0:r1, :] += y

        for b0 in range(0, MOE_ROWS, MOE_SUB):
            b1, mid = b0 + MOE_SUB, b0 + MOE_SUB // 2
            has_rows = (lo_row < b1) & (hi_row > b0)
            needs_lower, needs_upper = has_rows & (lo_row < mid), has_rows & (hi_row > mid)
            pl.when(needs_lower & needs_upper)(lambda: run(b0, b1, b0, b1))
            pl.when(needs_lower & jnp.logical_not(needs_upper))(lambda: run(b0, mid, b0, b1))
            pl.when(jnp.logical_not(needs_lower) & needs_upper)(lambda: run(mid, b1, b0, b1))
            if MOE_ROWS > MOE_SUB:
                @pl.when(jnp.logical_not(has_rows) & is_first)
                def _():
                    o_ref[b0:b1, :] = jnp.zeros((MOE_SUB, o_ref.shape[1]), F32)


def _moe_plan_kernel(offs_ref, kind_ref, rb_ref, bk_ref, pa_ref, pb_ref, first_ref, cast_ref, cpos_ref, pe_ref,
                     irb, ibk, *, n_steps_max):
    i32 = jnp.int32
    ng, epg, ppg = N_EXPERT_GROUPS, EXPERTS_PER_GROUP, PAIRS_PER_GROUP

    shift = MOE_ROWS.bit_length() - 1

    def bucket_body(bk, cnt):
        a, b = offs_ref[bk], offs_ref[bk + 1]
        first_blk = lax.shift_right_logical(a, shift)
        n_blk = jnp.where(b > a, lax.shift_right_logical(b - 1, shift) - first_blk + 1, 0)

        def block_body(j, cnt):
            irb[cnt] = first_blk + j
            ibk[cnt] = bk
            return cnt + 1

        return lax.fori_loop(0, n_blk, block_body, cnt)

    n_items = lax.fori_loop(0, ng * ppg, bucket_body, i32(0))

    def count_body(i, m):
        g = ibk[i] // ppg
        return tuple(m[k] + (g == k).astype(i32) for k in range(ng))

    m = lax.fori_loop(0, n_items, count_body, (i32(0),) * ng)

    def next_group(g):
        nxt = i32(-1)
        for k in range(ng - 1, 0, -1):
            nxt = jnp.where((k > g) & (m[k] > 0), k, nxt)
        return nxt

    def emit(s, kind, rb, bk, pa, pb, first, cast, cpos, pe):
        kind_ref[s], rb_ref[s], bk_ref[s], pa_ref[s], pb_ref[s] = kind, rb, bk, pa, pb
        first_ref[s], cast_ref[s], cpos_ref[s], pe_ref[s] = first, cast, cpos, pe

    def item_body(i, carry):
        s, gcur, parity, q, last_pe, last_rb = carry
        rb, bk = irb[i], ibk[i]
        g, pos = bk // ppg, bk % ppg
        new = g != gcur
        started = gcur >= 0
        loaders = jnp.where(new, jnp.where(started, jnp.maximum(epg - q, 0), epg), 0)
        parity = jnp.where(new & started, 1 - parity, parity)
        q = jnp.where(new, 0, q)
        for j in range(epg):
            on = j >= epg - loaders
            emit(s, STEP_LOAD, rb, bk, 0, 0, 0, 1, parity * epg + j, epg * g + j)
            last_pe = jnp.where(on, epg * g + j, last_pe)
            s = s + on.astype(i32)
        nxt = next_group(g)
        pre = (q < epg) & (nxt >= 0)
        pe = jnp.where(pre, epg * nxt + q, last_pe)
        slot_a, slot_b = i32(PAIR_SLOTS[0][0]), i32(PAIR_SLOTS[0][1])
        for p in range(1, ppg):
            slot_a = jnp.where(pos == p, PAIR_SLOTS[p][0], slot_a)
            slot_b = jnp.where(pos == p, PAIR_SLOTS[p][1], slot_b)
        emit(s, STEP_ITEM, rb, bk, parity * epg + slot_a, parity * epg + slot_b, (rb != last_rb).astype(i32),
             pre.astype(i32), (1 - parity) * epg + q, pe)
        return s + 1, g, parity, q + 1, pe, rb

    s, _, _, _, last_pe, last_rb = lax.fori_loop(
        0, n_items, item_body, (i32(0), i32(-1), i32(0), i32(0), i32(0), i32(-1)))
    last_bk = ibk[jnp.maximum(n_items - 1, 0)]

    def pad_body(s, _):
        emit(s, STEP_PAD, last_rb, last_bk, 0, 0, 0, 0, 0, last_pe)
        return 0

    lax.fori_loop(s, n_steps_max, pad_body, 0)


def _moe_steps(bucket, t):
    i32 = jnp.int32
    nbk = N_EXPERT_GROUPS * PAIRS_PER_GROUP
    order = jnp.argsort(bucket, stable=True).astype(i32)
    counts = jnp.sum((bucket[None, :] == jnp.arange(nbk, dtype=i32)[:, None]).astype(i32), axis=1)
    offs = jnp.concatenate([jnp.zeros((1,), i32), jnp.cumsum(counts).astype(i32)])
    assert MOE_ROWS & (MOE_ROWS - 1) == 0
    n_items_max = t // MOE_ROWS + nbk - 1
    n_steps_max = n_items_max + N_EXPERTS
    smem = pl.BlockSpec(memory_space=pltpu.SMEM)
    tables = pl.pallas_call(
        functools.partial(_moe_plan_kernel, n_steps_max=n_steps_max),
        in_specs=[smem],
        out_specs=[smem] * 9,
        out_shape=[jax.ShapeDtypeStruct((n_steps_max,), i32)] * 9,
        scratch_shapes=[pltpu.SMEM((n_items_max + 1,), i32)] * 2,
        name="moe_plan",
    )(offs)
    return order, (*tables, offs), n_steps_max


def _moe(xs, cws, w_gate, w_up, w_down, layer, tables, n_steps_max):
    t, d = xs.shape
    ff = w_gate.shape[3]
    w_map = lambda s, kind, rb, bk, pa, pb, fi, ca, cp, pe, of: (layer, pe[s], 0, 0)
    row_map = lambda s, kind, rb, *_: (rb[s], 0)
    nres = 2 * EXPERTS_PER_GROUP
    grid_spec = pltpu.PrefetchScalarGridSpec(
        num_scalar_prefetch=len(tables),
        grid=(n_steps_max,),
        in_specs=[pl.BlockSpec((MOE_ROWS, d), row_map), pl.BlockSpec((MOE_ROWS, 2), row_map),
                  pl.BlockSpec((None, None, d, ff), w_map), pl.BlockSpec((None, None, d, ff), w_map),
                  pl.BlockSpec((None, None, ff, d), w_map)],
        out_specs=pl.BlockSpec((MOE_ROWS, d), row_map),
        scratch_shapes=[pltpu.VMEM((nres, d, ff), BF16), pltpu.VMEM((nres, d, ff), BF16),
                        pltpu.VMEM((nres, ff, d), BF16)],
    )
    return pl.pallas_call(
        _moe_kernel,
        grid_spec=grid_spec,
        out_shape=jax.ShapeDtypeStruct((t, d), F32),
        compiler_params=_params(("arbitrary",), BIG_VMEM_LIMIT),
        name="moe_grouped",
    )(*tables, xs, cws, w_gate, w_up, w_down)


def _take_rows(a, idx):
    return a.at[idx].get(mode="promise_in_bounds", unique_indices=True)


def _final_kernel(x_ref, y_ref, g_ref, o_ref):
    o_ref[...] = x_ref[...] + g_ref[...] * y_ref[...]


def _final(x1, y, g2, seq):
    t, d = x1.shape
    tm = min(TOKEN_TILE, seq)
    per_b = seq // tm
    tok = lambda w: pl.BlockSpec((tm, w), lambda i: (i, 0))
    return pl.pallas_call(
        _final_kernel,
        grid=(t // tm,),
        in_specs=[tok(d), tok(d), pl.BlockSpec((None, 1, d), lambda i: (i // per_b, 0, 0))],
        out_specs=tok(d),
        out_shape=jax.ShapeDtypeStruct((t, d), F32),
        compiler_params=_params(("parallel",)),
        name="final_residual",
    )(x1, y, g2)


def kernel(x, c, positions, ada_w, ada_b, norm1_g, w_in, q_norm_g, k_norm_g, attn_sink, lam_re, lam_im, ssm_b_re, ssm_b_im, ssm_c_re, ssm_c_im, ssm_d, ssm_log_dt, w_glu, attn_out_g, ssm_out_g, w_out, norm2_g, w_router, router_bias, w_exp_gate, w_exp_up, w_exp_down):
    batch, seq, d = x.shape
    depth = ada_w.shape[0]
    t = batch * seq
    assert seq % ATTN_BLOCK == 0 and seq % SSM_CHUNK == 0 and t % MOE_ROWS == 0

    mod = _adaln_mod(c, ada_w, ada_b).reshape(depth, 6, batch, 1, d)
    cos, sin = _rope_tables(positions)
    head_sum, rot = _rope_constants()
    bias = _attn_bias()
    w_router_t = w_router.T
    s5_mats = _s5_prep(lam_re, lam_im, ssm_b_re, ssm_b_im, ssm_c_re, ssm_c_im, ssm_d, ssm_log_dt)
    router_bias_col = router_bias.reshape(N_EXPERTS, 1)

    xf = x.reshape(t, d)
    res = None
    for l in range(depth):
        sh1, sc1, g1, sh2, sc2, g2 = (mod[l, j] for j in range(6))
        qg = (jnp.tile(q_norm_g[l], N_Q_HEADS) * (HEAD_DIM ** -0.5 * LOG2_E)).reshape(1, ATTN_WIDTH)
        kg = jnp.tile(k_norm_g[l], N_KV_HEADS).reshape(1, KV_WIDTH)
        outs = _inproj(xf, res, sc1, sh1, norm1_g[l].reshape(1, d), w_in, l, qg, kg, head_sum, rot, cos, sin, seq)
        if res is None:
            q, kx, vx, uc = outs
        else:
            q, kx, vx, uc, xf = outs
        attn = _attention(q, kx, vx, attn_sink[l], attn_out_g[l].reshape(1, ATTN_WIDTH), bias, batch, seq)
        yc = _s5_scan(uc, s5_mats, l, seq // SSM_CHUNK, batch)
        x1, h2, cw, gid = _post(xf, attn, yc, w_glu, ssm_out_g[l].reshape(1, SSM_WIDTH), w_out, l, g1,
                                norm2_g[l].reshape(1, d), sc2, sh2, w_router_t, router_bias_col, seq)
        order, tables, n_steps_max = _moe_steps(gid.reshape(t), t)
        y_sorted = _moe(_take_rows(h2, order), _take_rows(cw.T, order), w_exp_gate, w_exp_up, w_exp_down, l,
                        tables, n_steps_max)
        y = _take_rows(y_sorted, jnp.argsort(order).astype(jnp.int32))
        xf, res = x1, (y, g2)
    y, g2 = res
    return _final(xf, y, g2, seq).reshape(batch, seq, d)
```

```python
import functools
import math

import numpy as np
import jax
import jax.numpy as jnp
from jax import lax
from jax.experimental import pallas as pl
from jax.experimental.pallas import tpu as pltpu

F32 = jnp.float32
BF16 = jnp.bfloat16

HEAD_DIM = 64
N_Q_HEADS = 8
N_KV_HEADS = 2
Q_PER_KV = N_Q_HEADS // N_KV_HEADS
ATTN_WIDTH = N_Q_HEADS * HEAD_DIM
KV_WIDTH = N_KV_HEADS * HEAD_DIM
ATTN_BLOCK = 128
ATTN_Q_TILE = 2048
ROPE_THETA = 10000.0
ROPE_SLAB = 256
LANES = 128
SSM_GROUP_CH = 16
SSM_GROUPS = 32
SSM_WIDTH = SSM_GROUPS * SSM_GROUP_CH
SSM_STATE = 64
SSM_CHUNK = 16
SSM_COLS = SSM_WIDTH // LANES
COL_GROUPS = LANES // SSM_GROUP_CH
COL_STATE = COL_GROUPS * SSM_STATE
CHUNK_LANES = SSM_CHUNK * LANES
SSM_NSPLIT = 4
S5_ROW_PAD = 8
N_EXPERTS = 16
N_EXPERT_GROUPS = 4
EXPERTS_PER_GROUP = N_EXPERTS // N_EXPERT_GROUPS
PAIRS_PER_GROUP = EXPERTS_PER_GROUP * (EXPERTS_PER_GROUP - 1) // 2
PAIR_SLOTS = ((0, 1), (0, 2), (0, 3), (1, 3), (1, 2), (3, 2))
EPS = 1e-6
LOG2_E = math.log2(math.e)
MASK_BIAS = -1e30

TOKEN_TILE = 1024
MOE_ROWS = 1024
MOE_SUB = 256
VMEM_LIMIT = 48 * 1024 * 1024
BIG_VMEM_LIMIT = 56 * 1024 * 1024
STEP_PAD, STEP_LOAD, STEP_ITEM = 0, 1, 2


def _params(sem, vmem=None):
    return pltpu.CompilerParams(dimension_semantics=sem, vmem_limit_bytes=vmem)


def _rms(x, g):
    return x * lax.rsqrt(jnp.mean(x * x, axis=-1, keepdims=True) + EPS) * g


def _mod_kernel(c_ref, w_ref, b_ref, o_ref):
    c = c_ref[...]
    s = c * jax.nn.sigmoid(c)
    o_ref[...] = jnp.dot(s.astype(BF16), w_ref[...].astype(BF16), preferred_element_type=F32) + b_ref[...]


def _adaln_mod(c, ada_w, ada_b):
    depth, d, d6 = ada_w.shape
    nb = c.shape[0]
    n6 = d6 // d
    return pl.pallas_call(
        _mod_kernel,
        grid=(depth, n6),
        in_specs=[pl.BlockSpec((nb, d), lambda l, j: (0, 0)),
                  pl.BlockSpec((None, d, d), lambda l, j: (l, 0, j)),
                  pl.BlockSpec((None, None, 1, d), lambda l, j: (l, j, 0, 0))],
        out_specs=pl.BlockSpec((None, None, nb, d), lambda l, j: (l, j, 0, 0)),
        out_shape=jax.ShapeDtypeStruct((depth, n6, nb, d), F32),
        compiler_params=_params(("arbitrary", "arbitrary"), VMEM_LIMIT),
        name="adaln_mod",
    )(c, ada_w, ada_b.reshape(depth, n6, 1, d))


def _spread(x, expander3):
    hi = x.astype(BF16)
    r1 = x - hi.astype(F32)
    mid = r1.astype(BF16)
    lo = (r1 - mid.astype(F32)).astype(BF16)
    return jnp.dot(jnp.concatenate([hi, mid, lo], axis=1), expander3, preferred_element_type=F32)


def _rope_kernel(pos_ref, freq_ref, lane_ref, quarter_ref, cos_ref, sin_ref):
    per_row = pos_ref.shape[1]
    rows = pos_ref.shape[0]
    pos = _spread(pos_ref[...].astype(F32), lane_ref[...])
    ang = pos * freq_ref[...]
    cos, sin = jnp.cos(ang), jnp.sin(ang)
    for j in range(per_row):
        cos_ref[pl.ds(j, rows, stride=per_row), :] = _spread(cos, quarter_ref[j])
        sin_ref[pl.ds(j, rows, stride=per_row), :] = _spread(sin, quarter_ref[j])


def _rope_tables(positions):
    half = HEAD_DIM // 2
    t = positions.size
    per_row = LANES // half
    rows = t // per_row
    freq = (ROPE_THETA ** (-np.arange(half, dtype=np.float64) / half)).astype(np.float32)
    freq_row = jnp.asarray(np.tile(freq, per_row)[None, :])
    to_quarter = np.repeat(np.eye(per_row, dtype=np.float32), half, axis=1)
    spread = np.zeros((per_row, LANES, LANES), np.float32)
    for j in range(per_row):
        spread[j, j * half:(j + 1) * half, :] = np.tile(np.eye(half, dtype=np.float32), (1, per_row))
    lane_sel = jnp.asarray(np.tile(to_quarter, (3, 1)), BF16)
    quarter_sel = jnp.asarray(np.tile(spread, (1, 3, 1)), BF16)
    blk = min(rows, 512)
    out = pl.BlockSpec((blk * per_row, LANES), lambda i: (i, 0))
    const = lambda a: pl.BlockSpec(a.shape, lambda i: (0,) * a.ndim)
    return pl.pallas_call(
        _rope_kernel,
        grid=(rows // blk,),
        in_specs=[pl.BlockSpec((blk, per_row), lambda i: (i, 0)), const(freq_row), const(lane_sel), const(quarter_sel)],
        out_specs=[out, out],
        out_shape=[jax.ShapeDtypeStruct((t, LANES), F32)] * 2,
        compiler_params=_params(("arbitrary",)),
        name="rope_tables",
    )(positions.reshape(rows, per_row), freq_row, lane_sel, quarter_sel)


def _rope_constants():
    lane = np.arange(ROPE_SLAB)
    head_sum = (lane[:, None] // HEAD_DIM == lane[None, :] // HEAD_DIM).astype(np.float32)
    half = HEAD_DIM // 2
    rot = np.zeros((ROPE_SLAB, ROPE_SLAB), np.float32)
    for d in range(ROPE_SLAB):
        if d % HEAD_DIM < half:
            rot[d + half, d] = -1.0
        else:
            rot[d - half, d] = 1.0
    return jnp.asarray(head_sum, BF16), jnp.asarray(rot, BF16)


def _inproj_kernel(*refs, has_res):
    if has_res:
        (x_ref, y_ref, g2_ref, sc_ref, sh_ref, n1_ref, w_ref, qg_ref, kg_ref, hs_ref, rot_ref,
         cos_ref, sin_ref, q_ref, k_ref, v_ref, uc_ref, xo_ref, u_scr, w_s) = refs
        x = x_ref[...] + g2_ref[...] * y_ref[...]
        xo_ref[...] = x
    else:
        (x_ref, sc_ref, sh_ref, n1_ref, w_ref, qg_ref, kg_ref, hs_ref, rot_ref,
         cos_ref, sin_ref, q_ref, k_ref, v_ref, uc_ref, u_scr, w_s) = refs
        x = x_ref[...]

    @pl.when(pl.program_id(0) == 0)
    def _():
        w_s[...] = w_ref[...].astype(BF16)

    h = _rms(x, n1_ref[...] * (1.0 + sc_ref[...])) + sh_ref[...]
    proj = jnp.dot(h.astype(BF16), w_s[...], preferred_element_type=F32)
    q = proj[:, :ATTN_WIDTH]
    k = proj[:, ATTN_WIDTH:ATTN_WIDTH + KV_WIDTH]
    v = proj[:, ATTN_WIDTH + KV_WIDTH:ATTN_WIDTH + 2 * KV_WIDTH]
    cos = cos_ref[...]
    sin = sin_ref[...]
    reps = ATTN_WIDTH // LANES
    cos_q = jnp.concatenate([cos] * reps, axis=1)
    sin_q = jnp.concatenate([sin] * reps, axis=1)

    def head_norm_rope(t, gain, c, s):
        outs = []
        for lo in range(0, t.shape[1], ROPE_SLAB):
            wd = min(ROPE_SLAB, t.shape[1] - lo)
            ts, lanes = t[:, lo:lo + wd], slice(lo, lo + wd)
            ssq = jnp.dot((ts * ts).astype(BF16), hs_ref[:wd, :wd], preferred_element_type=F32)
            tn = (ts * lax.rsqrt(ssq * (1.0 / HEAD_DIM) + EPS) * gain[:, lanes]).astype(BF16)
            tr = jnp.dot(tn, rot_ref[:wd, :wd], preferred_element_type=F32)
            outs.append(tn.astype(F32) * c[:, lanes] + tr * s[:, lanes])
        return outs[0] if len(outs) == 1 else jnp.concatenate(outs, axis=1)

    qo = head_norm_rope(q, qg_ref[...], cos_q, sin_q)
    ko = head_norm_rope(k, kg_ref[...], cos, sin)
    q_ref[...] = qo.astype(BF16)
    k_ref[...] = jnp.concatenate([ko, pltpu.roll(ko, HEAD_DIM, axis=1)], axis=1).astype(BF16)
    v_ref[...] = jnp.concatenate([v, pltpu.roll(v, HEAD_DIM, axis=1)], axis=1).astype(BF16)
    u0 = ATTN_WIDTH + 2 * KV_WIDTH
    nchunk = u_scr.shape[1] // SSM_CHUNK
    for j in range(SSM_COLS):
        u_scr[j] = proj[:, u0 + j * LANES:u0 + (j + 1) * LANES]
    for s in range(SSM_CHUNK):
        for j in range(SSM_COLS):
            lanes = slice(s * SSM_WIDTH + j * LANES, s * SSM_WIDTH + (j + 1) * LANES)
            uc_ref[:, lanes] = u_scr[j, pl.ds(s, nchunk, stride=SSM_CHUNK), :].astype(BF16)


def _inproj(x, res, sc1, sh1, n1g, w_in, layer, qg, kg, head_sum, rot, cos, sin, seq):
    t, d = x.shape
    tm = min(TOKEN_TILE, seq)
    per_b = seq // tm
    in_width = w_in.shape[2]
    tok = lambda w: pl.BlockSpec((tm, w), lambda i: (i, 0))
    const = lambda a: pl.BlockSpec(a.shape, lambda i: (0,) * a.ndim)
    per_batch = pl.BlockSpec((None, 1, d), lambda i: (i // per_b, 0, 0))
    chunked = pl.BlockSpec((tm // SSM_CHUNK, SSM_CHUNK * SSM_WIDTH), lambda i: (i, 0))
    ins, specs = [x], [tok(d)]
    if res is not None:
        y_prev, g2_prev = res
        ins += [y_prev, g2_prev]
        specs += [tok(d), per_batch]
    ins += [sc1, sh1, n1g, w_in, qg, kg, head_sum, rot, cos, sin]
    specs += [per_batch, per_batch, const(n1g), pl.BlockSpec((None, d, in_width), lambda i: (layer, 0, 0)),
              const(qg), const(kg), const(head_sum), const(rot), tok(LANES), tok(LANES)]
    out_shape = [jax.ShapeDtypeStruct((t, ATTN_WIDTH), BF16), jax.ShapeDtypeStruct((t, 2 * KV_WIDTH), BF16),
                 jax.ShapeDtypeStruct((t, 2 * KV_WIDTH), BF16),
                 jax.ShapeDtypeStruct((t // SSM_CHUNK, SSM_CHUNK * SSM_WIDTH), BF16)]
    out_specs = [tok(ATTN_WIDTH), tok(2 * KV_WIDTH), tok(2 * KV_WIDTH), chunked]
    if res is not None:
        out_shape.append(jax.ShapeDtypeStruct((t, d), F32))
        out_specs.append(tok(d))
    assert in_width == ATTN_WIDTH + 2 * KV_WIDTH + SSM_WIDTH
    return pl.pallas_call(
        functools.partial(_inproj_kernel, has_res=res is not None),
        grid=(t // tm,),
        in_specs=specs,
        out_specs=out_specs,
        out_shape=out_shape,
        scratch_shapes=[pltpu.VMEM((SSM_COLS, tm, LANES), F32), pltpu.VMEM((d, in_width), BF16)],
        compiler_params=_params(("arbitrary",), BIG_VMEM_LIMIT),
        name="inproj",
    )(*ins)


def _attn_kernel(sink_ref, q_ref, kc_ref, kp_ref, vc_ref, vp_ref, bias_ref, g_ref, o_ref):
    nsub = q_ref.shape[0] // ATTN_BLOCK
    kk = jnp.concatenate([kp_ref[...], kc_ref[...]], axis=0)
    vv = jnp.concatenate([vp_ref[...], vc_ref[...]], axis=0)
    low = lax.broadcasted_iota(jnp.int32, (kk.shape[0], KV_WIDTH), 1) < HEAD_DIM
    zero = jnp.zeros((kk.shape[0], KV_WIDTH), BF16)

    def variants(a):
        nat, swp = a[:, :KV_WIDTH], a[:, KV_WIDTH:]
        return {(0, 0): jnp.where(low, nat, zero), (0, 1): jnp.where(low, zero, swp),
                (1, 0): jnp.where(low, swp, zero), (1, 1): jnp.where(low, zero, nat)}

    kvar, vvar = variants(kk), variants(vv)
    band = bias_ref[1]
    first = bias_ref[jnp.minimum(pl.program_id(1), 1)]
    upper = lax.broadcasted_iota(jnp.int32, (2 * ATTN_BLOCK, 1), 0) < ATTN_BLOCK
    for j in range(nsub):
        bias = first if j == 0 else band
        bias2 = jnp.concatenate([bias, bias], axis=0)
        keys = slice(j * ATTN_BLOCK, (j + 2) * ATTN_BLOCK)
        qrows = slice(j * ATTN_BLOCK, (j + 1) * ATTN_BLOCK)
        tiles = [None] * (N_Q_HEADS // 2)
        for kv in range(N_KV_HEADS):
            for half in range(2):
                pairs = (2 * kv, 2 * kv + 1)
                heads = (2 * pairs[0] + half, 2 * pairs[1] + half)
                qs = jnp.concatenate([q_ref[qrows, p * LANES:(p + 1) * LANES] for p in pairs], axis=0)
                s = lax.dot_general(qs, kvar[(kv, half)][keys], (((1,), (1,)), ((), ())),
                                    preferred_element_type=F32) + bias2
                sink = jnp.where(upper, sink_ref[heads[0]], sink_ref[heads[1]]) * LOG2_E
                m = jnp.maximum(jnp.max(s, axis=-1, keepdims=True), sink)
                p = jnp.exp2(s - m)
                denom = jnp.sum(p, axis=-1, keepdims=True) + jnp.exp2(sink - m)
                o = jnp.dot(p.astype(BF16), vvar[(kv, half)][keys], preferred_element_type=F32) * (1.0 / denom)
                for r, pr in enumerate(pairs):
                    part = o[r * ATTN_BLOCK:(r + 1) * ATTN_BLOCK]
                    tiles[pr] = part if tiles[pr] is None else tiles[pr] + part
        a = jnp.concatenate(tiles, axis=1)
        o_ref[qrows, :] = _rms(a, g_ref[...]).astype(BF16)


def _attn_bias():
    qi = np.arange(ATTN_BLOCK)[:, None]
    sj = np.arange(2 * ATTN_BLOCK)[None, :]
    diff = qi + ATTN_BLOCK - sj
    band = (diff >= 0) & (diff < ATTN_BLOCK)
    first = band & (sj >= ATTN_BLOCK)
    return jnp.asarray(np.where(np.stack([first, band]), 0.0, MASK_BIAS).astype(np.float32))


def _attention(q, kx, vx, sink, out_g, bias, batch, seq):
    t = q.shape[0]
    qb = min(ATTN_Q_TILE, seq)
    nsub = qb // ATTN_BLOCK
    nq = seq // qb
    nb = seq // ATTN_BLOCK
    cur = lambda w: pl.BlockSpec((qb, w), lambda b, n, s: (b * nq + n, 0))
    prev = lambda w: pl.BlockSpec((ATTN_BLOCK, w), lambda b, n, s: (b * nb + jnp.maximum(n * nsub - 1, 0), 0))
    grid_spec = pltpu.PrefetchScalarGridSpec(
        num_scalar_prefetch=1,
        grid=(batch, nq),
        in_specs=[cur(ATTN_WIDTH), cur(2 * KV_WIDTH), prev(2 * KV_WIDTH), cur(2 * KV_WIDTH), prev(2 * KV_WIDTH),
                  pl.BlockSpec(bias.shape, lambda b, n, s: (0, 0, 0)),
                  pl.BlockSpec((1, ATTN_WIDTH), lambda b, n, s: (0, 0))],
        out_specs=cur(ATTN_WIDTH),
    )
    return pl.pallas_call(
        _attn_kernel,
        grid_spec=grid_spec,
        out_shape=jax.ShapeDtypeStruct((t, ATTN_WIDTH), BF16),
        compiler_params=_params(("parallel", "arbitrary")),
        name="swa_attention",
    )(sink, q, kx, kx, vx, vx, bias, out_g)


def _s5_prep_kernel(lr_re_ref, lr_im_ref, ldt_ref, bt_re_ref, bt_im_ref, ct_re_ref, ct_im_ref,
                    d_ref, lcol_re_ref, lcol_im_ref, ldtcol_ref, exp_ref, exph_ref, expt_ref, expw_ref,
                    t_ref, w_ref, v_ref, la_ref, lb_ref):
    hi = lax.Precision.HIGHEST
    nl = SSM_CHUNK
    low = lax.broadcasted_iota(jnp.int32, (1, 2 * SSM_STATE), 1) < SSM_STATE
    row_low = lax.broadcasted_iota(jnp.int32, (2 * SSM_STATE, 1), 0) < SSM_STATE
    jcol = lax.broadcasted_iota(jnp.int32, (nl, 1), 0).astype(F32)
    kt_lane = lax.broadcasted_iota(jnp.int32, (SSM_GROUP_CH, nl * SSM_GROUP_CH), 1)
    kt_row = lax.broadcasted_iota(jnp.int32, (SSM_GROUP_CH, nl * SSM_GROUP_CH), 0)

    w_all, v_all, kt_all = [], [], []
    for gm in range(COL_GROUPS):
        dt = jnp.exp(ldt_ref[gm])
        lam_re, lam_im = lr_re_ref[gm], lr_im_ref[gm]
        a_r, th_r = lam_re * dt, lam_im * dt

        er = jnp.exp(jcol * a_r)
        pw_re, pw_im = er * jnp.cos(jcol * th_r), er * jnp.sin(jcol * th_r)

        nr, ni = pw_re[1:2, :] - 1.0, pw_im[1:2, :]
        den = lam_re * lam_re + lam_im * lam_im
        c_re, c_im = (nr * lam_re + ni * lam_im) / den, (ni * lam_re - nr * lam_im) / den
        bt_re, bt_im = bt_re_ref[gm], bt_im_ref[gm]
        bb_re, bb_im = c_re * bt_re - c_im * bt_im, c_re * bt_im + c_im * bt_re

        w_rows = []
        for s in range(nl):
            j = nl - 1 - s
            pr, pi = pw_re[j:j + 1, :], pw_im[j:j + 1, :]
            w_rows.append(jnp.where(low, pr * bb_re - pi * bb_im, pr * bb_im + pi * bb_re))
        w_all.append(w_rows)

        pw_re_t, pw_im_t = pw_re.T, pw_im.T
        pc, ps = _spread(pw_re_t, exp_ref[...]), _spread(pw_im_t, exp_ref[...])
        ct_re, ct_im = _spread(ct_re_ref[gm], exph_ref[...]), _spread(ct_im_ref[gm], exph_ref[...])
        a_re, a_im = ct_re * pc - ct_im * ps, ct_re * ps + ct_im * pc
        a_cat = jnp.where(row_low, a_re, -a_im)
        l1_re, l1_im = pw_re_t[:, 1:2], pw_im_t[:, 1:2]
        v_re, v_im = a_re * l1_re - a_im * l1_im, a_re * l1_im + a_im * l1_re
        v_all.append(jnp.where(row_low, v_re, -v_im))

        kt = jnp.dot(jnp.where(low, bb_re, bb_im), a_cat, precision=hi, preferred_element_type=F32)
        kt_all.append(kt + jnp.where(kt_lane == kt_row, d_ref[gm], 0.0))

    def same_group(shape, row_group, lane_group):
        r = lax.broadcasted_iota(jnp.int32, shape, 0)
        c = lax.broadcasted_iota(jnp.int32, shape, 1)
        return (row_group(r) == lane_group(c)).astype(F32)

    chan_group = lambda i: (i >> 4) & (COL_GROUPS - 1)
    state_group = lambda i: (i >> 6) & (COL_GROUPS - 1)
    over_steps = lambda m: jnp.concatenate([m] * nl, axis=1)

    kt_wide = jnp.dot(jnp.concatenate(kt_all, axis=0).astype(BF16), expt_ref[...], preferred_element_type=F32)
    bd = (kt_wide * over_steps(same_group((LANES, LANES), chan_group, chan_group))).astype(BF16)
    t_ref[0:LANES, :] = bd
    for s in range(1, nl):
        t_ref[s * LANES:(s + 1) * LANES, :] = jnp.concatenate(
            [jnp.zeros((LANES, s * LANES), BF16), bd[:, :CHUNK_LANES - s * LANES]], axis=1)

    w_stack = jnp.concatenate([w_all[gm][s] for s in range(nl) for gm in range(COL_GROUPS)], axis=0)
    w_wide = jnp.dot(w_stack.astype(BF16), expw_ref[...], preferred_element_type=F32)
    w_mask = same_group((LANES, 2 * COL_STATE), chan_group, state_group)
    w_ref[...] = (w_wide.reshape(nl, LANES, 2 * COL_STATE) * w_mask[None]).reshape(CHUNK_LANES, 2 * COL_STATE).astype(BF16)

    v_stack = jnp.concatenate([v_all[gm][half * SSM_STATE:(half + 1) * SSM_STATE, :]
                               for half in range(2) for gm in range(COL_GROUPS)], axis=0)
    v_wide = jnp.dot(v_stack.astype(BF16), expt_ref[...], preferred_element_type=F32)
    v_ref[...] = (v_wide * over_steps(same_group((2 * COL_STATE, LANES), state_group, chan_group))).astype(BF16)

    dtc = jnp.exp(ldtcol_ref[...])
    e16 = jnp.exp(nl * lcol_re_ref[...] * dtc)
    ang = nl * lcol_im_ref[...] * dtc
    la_ref[...] = e16 * jnp.cos(ang)
    lb_ref[...] = e16 * jnp.sin(ang)


def _s5_prep(lam_re, lam_im, b_re, b_im, c_re, c_im, d_skip, log_dt):
    g, p, h, nl = SSM_GROUPS, SSM_STATE, SSM_GROUP_CH, SSM_CHUNK
    cg = COL_GROUPS
    nc = lam_re.shape[0] * SSM_COLS
    col = lambda a: a.reshape((nc, cg) + a.shape[2:])
    dup_row = lambda a: col(jnp.tile(a, (1, 1, 2))[:, :, None, :])
    bt = lambda a: col(jnp.tile(jnp.swapaxes(a, 2, 3), (1, 1, 1, 2)))
    ct = lambda a: col(jnp.tile(jnp.swapaxes(a, 2, 3), (1, 1, 2, 1)))
    d_pad = col(jnp.pad(d_skip.reshape(-1, g, 1, h), ((0, 0), (0, 0), (0, 0), (0, nl * h - h))))
    wide = lambda a: a.reshape(nc, 1, cg * p)
    expand = jnp.asarray(np.tile(np.repeat(np.eye(nl, dtype=np.float32), h, axis=1), (3, 1)), BF16)
    expand_h = jnp.asarray(np.tile(np.eye(h, dtype=np.float32), (3, nl)), BF16)
    exp_t = np.zeros((nl, h, nl, cg, h), np.float32)
    exp_w = np.zeros((2, p, 2, cg, p), np.float32)
    for gm in range(cg):
        exp_t[:, :, :, gm, :] = np.eye(nl * h, dtype=np.float32).reshape(nl, h, nl, h)
        exp_w[:, :, :, gm, :] = np.eye(2 * p, dtype=np.float32).reshape(2, p, 2, p)
    exp_t = jnp.asarray(exp_t.reshape(nl * h, CHUNK_LANES), BF16)
    exp_w = jnp.asarray(exp_w.reshape(2 * p, 2 * COL_STATE), BF16)
    blk = lambda *s: pl.BlockSpec((None,) + s, lambda i: (i,) + (0,) * len(s))
    const = lambda a: pl.BlockSpec(a.shape, lambda i: (0,) * a.ndim)
    lw = nl * h
    return pl.pallas_call(
        _s5_prep_kernel,
        grid=(nc,),
        in_specs=[blk(cg, 1, 2 * p), blk(cg, 1, 2 * p), blk(cg, 1, 1),
                  blk(cg, h, 2 * p), blk(cg, h, 2 * p), blk(cg, 2 * p, h), blk(cg, 2 * p, h), blk(cg, 1, lw),
                  blk(1, cg * p), blk(1, cg * p), blk(1, cg * p), const(expand), const(expand_h), const(exp_t),
                  const(exp_w)],
        out_specs=[blk(CHUNK_LANES, CHUNK_LANES), blk(CHUNK_LANES, 2 * COL_STATE), blk(2 * COL_STATE, CHUNK_LANES),
                   blk(1, COL_STATE), blk(1, COL_STATE)],
        out_shape=[jax.ShapeDtypeStruct((nc, CHUNK_LANES, CHUNK_LANES), BF16),
                   jax.ShapeDtypeStruct((nc, CHUNK_LANES, 2 * COL_STATE), BF16),
                   jax.ShapeDtypeStruct((nc, 2 * COL_STATE, CHUNK_LANES), BF16),
                   jax.ShapeDtypeStruct((nc, 1, COL_STATE), F32), jax.ShapeDtypeStruct((nc, 1, COL_STATE), F32)],
        compiler_params=_params(("parallel",), VMEM_LIMIT),
        name="s5_prep",
    )(dup_row(lam_re), dup_row(lam_im), col(log_dt[:, :, None, None]),
      bt(b_re), bt(b_im), ct(c_re), ct(c_im), d_pad, wide(lam_re), wide(lam_im),
      wide(jnp.repeat(log_dt, p, axis=1)), expand, expand_h, exp_t, exp_w)


def _s5_kernel(*refs, nchunks, nb):
    uc_refs = refs[:SSM_CHUNK]
    t_ref, w_ref, v_ref, la_ref, lb_ref, o_ref, ucat_ref, s_ref, xp_ref = refs[SSM_CHUNK:]

    @pl.when(pl.program_id(1) == 0)
    def _():
        for s in range(SSM_CHUNK):
            ucat_ref[:, s * LANES:(s + 1) * LANES] = uc_refs[s][...]
        s_in = jnp.dot(ucat_ref[...], w_ref[...], preferred_element_type=F32)
        nblk = COL_STATE // LANES
        pitch = s_ref.shape[1] // nb
        for b in range(2 * nblk):
            for q in range(nb):
                s_ref[b, q * pitch:q * pitch + nchunks, :] = s_in[q * nchunks:(q + 1) * nchunks, b * LANES:(b + 1) * LANES]
        lr = [jnp.broadcast_to(la_ref[:, b * LANES:(b + 1) * LANES], (nb, LANES)) for b in range(nblk)]
        li = [jnp.broadcast_to(lb_ref[:, b * LANES:(b + 1) * LANES], (nb, LANES)) for b in range(nblk)]

        def step(c, carry):
            rows = pl.ds(c, nb, stride=pitch)
            out = []
            for b in range(nblk):
                re, im = carry[2 * b], carry[2 * b + 1]
                xp_ref[b, rows, :] = re
                xp_ref[nblk + b, rows, :] = im
                out.append(lr[b] * re - li[b] * im + s_ref[b, rows, :])
                out.append(lr[b] * im + li[b] * re + s_ref[nblk + b, rows, :])
            return tuple(out)

        zero = jnp.zeros((nb, LANES), F32)
        lax.fori_loop(0, nchunks, step, (zero,) * (2 * nblk), unroll=4)

    pitch = xp_ref.shape[1] // nb
    xp = jnp.concatenate(
        [jnp.concatenate([xp_ref[b, q * pitch:q * pitch + nchunks, :] for q in range(nb)], axis=0)
         for b in range(2 * COL_STATE // LANES)], axis=1).astype(BF16)
    inter = jnp.dot(xp, v_ref[...], preferred_element_type=F32)
    for kk in range(SSM_NSPLIT):
        @pl.when(pl.program_id(1) == kk)
        def _():
            live = (kk + 1) * (CHUNK_LANES // SSM_NSPLIT)
            intra = jnp.dot(ucat_ref[:, :live], t_ref[:live, :], preferred_element_type=F32)
            o_ref[...] = (intra + inter).astype(BF16)


def _s5_scan(uc, mats, layer, nchunks, nb):
    rows = uc.shape[0]
    c0 = layer * SSM_COLS
    split = CHUNK_LANES // SSM_NSPLIT
    u_spec = lambda s: pl.BlockSpec((rows, LANES), lambda j, k: (0, SSM_COLS * s + j))
    return pl.pallas_call(
        functools.partial(_s5_kernel, nchunks=nchunks, nb=nb),
        grid=(SSM_COLS, SSM_NSPLIT),
        in_specs=[u_spec(s) for s in range(SSM_CHUNK)] + [
            pl.BlockSpec((None, CHUNK_LANES, split), lambda j, k: (c0 + j, 0, k)),
            pl.BlockSpec((None, CHUNK_LANES, 2 * COL_STATE), lambda j, k: (c0 + j, 0, 0)),
            pl.BlockSpec((None, 2 * COL_STATE, split), lambda j, k: (c0 + j, 0, k)),
            pl.BlockSpec((None, 1, COL_STATE), lambda j, k: (c0 + j, 0, 0)),
            pl.BlockSpec((None, 1, COL_STATE), lambda j, k: (c0 + j, 0, 0))],
        out_specs=pl.BlockSpec((None, rows, split), lambda j, k: (j, 0, k)),
        out_shape=jax.ShapeDtypeStruct((SSM_COLS, rows, CHUNK_LANES), BF16),
        scratch_shapes=[pltpu.VMEM((rows, CHUNK_LANES), BF16),
                        pltpu.VMEM((2 * COL_STATE // LANES, nb * (nchunks + S5_ROW_PAD), LANES), F32),
                        pltpu.VMEM((2 * COL_STATE // LANES, nb * (nchunks + S5_ROW_PAD), LANES), F32)],
        compiler_params=_params(("parallel", "arbitrary"), VMEM_LIMIT),
        name="s5_scan",
    )(*([uc] * SSM_CHUNK), *mats)


def _route(logits, bias):
    m = jnp.max(logits, axis=0, keepdims=True)
    e = jnp.exp(logits - m)
    probs = e / jnp.sum(e, axis=0, keepdims=True)
    sel = probs + bias
    row = lambda a, i: a[i:i + 1, :]
    best_score, best = None, None
    for grp in range(N_EXPERT_GROUPS):
        a, b, c, d = (row(sel, EXPERTS_PER_GROUP * grp + i) for i in range(EXPERTS_PER_GROUP))
        hab, lab, hcd, lcd = jnp.maximum(a, b), jnp.minimum(a, b), jnp.maximum(c, d), jnp.minimum(c, d)
        top1 = jnp.maximum(hab, hcd)
        top2 = jnp.maximum(jnp.maximum(lab, lcd), jnp.minimum(hab, hcd))
        score = top1 + top2
        if grp == 0:
            best_score, best = score, jnp.zeros(score.shape, jnp.int32)
        else:
            better = score > best_score
            best = jnp.where(better, grp, best)
            best_score = jnp.where(better, score, best_score)

    def pick(a, i):
        out = row(a, i)
        for grp in range(1, N_EXPERT_GROUPS):
            out = jnp.where(best == grp, row(a, EXPERTS_PER_GROUP * grp + i), out)
        return out

    s_in = [pick(sel, i) for i in range(EXPERTS_PER_GROUP)]
    p_in = [pick(probs, i) for i in range(EXPERTS_PER_GROUP)]
    neg = jnp.full(s_in[0].shape, -jnp.inf, F32)

    def argmax_first(vals):
        idx, val = jnp.zeros(vals[0].shape, jnp.int32), vals[0]
        for i in range(1, len(vals)):
            better = vals[i] > val
            idx = jnp.where(better, i, idx)
            val = jnp.where(better, vals[i], val)
        return idx

    i1 = argmax_first(s_in)
    i2 = argmax_first([jnp.where(i1 == i, neg, s_in[i]) for i in range(EXPERTS_PER_GROUP)])
    zero = jnp.zeros(p_in[0].shape, F32)
    g1 = sum(jnp.where(i1 == i, p_in[i], zero) for i in range(EXPERTS_PER_GROUP))
    g2 = sum(jnp.where(i2 == i, p_in[i], zero) for i in range(EXPERTS_PER_GROUP))
    tot = g1 + g2
    w1, w2 = g1 / tot, g2 / tot
    first_low = i1 < i2
    low, high = jnp.minimum(i1, i2), jnp.maximum(i1, i2)
    w_low, w_high = jnp.where(first_low, w1, w2), jnp.where(first_low, w2, w1)
    pos = jnp.where(low == 0, high - 1, jnp.where(low == 1, jnp.where(high == 2, 4, 3), 5))
    swap = low == 2
    bucket = best * PAIRS_PER_GROUP + pos
    return jnp.concatenate([jnp.where(swap, w_high, w_low), jnp.where(swap, w_low, w_high)], axis=0), bucket


def _router_logits(w_t, h):
    w_hi = w_t.astype(BF16)
    w_r = w_t - w_hi.astype(F32)
    w_mid = w_r.astype(BF16)
    w_lo = (w_r - w_mid.astype(F32)).astype(BF16)
    h_hi = h.astype(BF16)
    h_lo = (h - h_hi.astype(F32)).astype(BF16)
    dims = (((1,), (1,)), ((), ()))
    a = lax.dot_general(jnp.concatenate([w_hi, w_mid, w_lo], axis=0), h_hi, dims, preferred_element_type=F32)
    b = lax.dot_general(jnp.concatenate([w_hi, w_mid], axis=0), h_lo, dims, preferred_element_type=F32)
    e = w_t.shape[0]
    return a[:e] + a[e:2 * e] + a[2 * e:] + b[:e] + b[e:]


def _post_kernel(x_ref, at_ref, yc_ref, wglu_ref, gs_ref, wo_ref, g1_ref, n2_ref, sc_ref, sh_ref,
                 wrt_ref, rb_ref, x1_ref, h2_ref, cw_ref, gid_ref, y_scr, wglu_s, wo_s):
    @pl.when(pl.program_id(0) == 0)
    def _():
        wglu_s[...] = wglu_ref[...].astype(BF16)
        wo_s[...] = wo_ref[...].astype(BF16)

    nchunk = y_scr.shape[1] // SSM_CHUNK
    for s in range(SSM_CHUNK):
        for j in range(SSM_COLS):
            y_scr[j, pl.ds(s, nchunk, stride=SSM_CHUNK), :] = yc_ref[j, :, s * LANES:(s + 1) * LANES].astype(F32)
    yg = jax.nn.gelu(jnp.concatenate([y_scr[j] for j in range(SSM_COLS)], axis=1))
    z = yg * jax.nn.sigmoid(jnp.dot(yg.astype(BF16), wglu_s[...], preferred_element_type=F32))
    zn = _rms(z, gs_ref[...]).astype(BF16)
    o = (jnp.dot(at_ref[...], wo_s[:ATTN_WIDTH, :], preferred_element_type=F32)
         + jnp.dot(zn, wo_s[ATTN_WIDTH:, :], preferred_element_type=F32))
    x1 = x_ref[...] + g1_ref[...] * o
    x1_ref[...] = x1
    h2 = _rms(x1, n2_ref[...] * (1.0 + sc_ref[...])) + sh_ref[...]
    h2_ref[...] = h2
    logits = _router_logits(wrt_ref[...], h2)
    cw, bucket = _route(logits, rb_ref[...])
    cw_ref[...] = cw
    gid_ref[...] = bucket


def _post(x, attn, yc, w_glu, ssm_g, w_out, layer, g1, n2g, sc2, sh2, w_router_t, router_bias, seq):
    t, d = x.shape
    tm = min(TOKEN_TILE, seq)
    per_b = seq // tm
    tok = lambda w: pl.BlockSpec((tm, w), lambda i: (i, 0))
    const = lambda a: pl.BlockSpec(a.shape, lambda i: (0,) * a.ndim)
    per_batch = pl.BlockSpec((None, 1, d), lambda i: (i // per_b, 0, 0))
    col = lambda r: pl.BlockSpec((r, tm), lambda i: (0, i))
    of_layer = lambda a: pl.BlockSpec((None,) + a.shape[1:], lambda i: (layer, 0, 0))
    chunked = pl.BlockSpec((SSM_COLS, tm // SSM_CHUNK, CHUNK_LANES), lambda i: (0, i, 0))
    return pl.pallas_call(
        _post_kernel,
        grid=(t // tm,),
        in_specs=[tok(d), tok(ATTN_WIDTH), chunked, of_layer(w_glu), const(ssm_g),
                  of_layer(w_out), per_batch, const(n2g), per_batch, per_batch, const(w_router_t), const(router_bias)],
        out_specs=[tok(d), tok(d), col(2), col(1)],
        out_shape=[jax.ShapeDtypeStruct((t, d), F32), jax.ShapeDtypeStruct((t, d), F32),
                   jax.ShapeDtypeStruct((2, t), F32), jax.ShapeDtypeStruct((1, t), jnp.int32)],
        scratch_shapes=[pltpu.VMEM((SSM_COLS, tm, LANES), F32), pltpu.VMEM(w_glu.shape[1:], BF16),
                        pltpu.VMEM(w_out.shape[1:], BF16)],
        compiler_params=_params(("arbitrary",), VMEM_LIMIT),
        name="post_mix",
    )(x, attn, yc, w_glu, ssm_g, w_out, g1, n2g, sc2, sh2, w_router_t, router_bias)


def _moe_kernel(kind_ref, rb_ref, bk_ref, pa_ref, pb_ref, first_ref, cast_ref, cpos_ref, pe_ref, offs_ref,
                x_ref, cw_ref, wg_ref, wu_ref, wd_ref, o_ref, wg_s, wu_s, wd_s):
    s = pl.program_id(0)

    @pl.when(cast_ref[s] == 1)
    def _():
        slot = cpos_ref[s]
        wg_s[slot] = wg_ref[...].astype(BF16)
        wu_s[slot] = wu_ref[...].astype(BF16)
        wd_s[slot] = wd_ref[...].astype(BF16)

    @pl.when(kind_ref[s] == STEP_ITEM)
    def _():
        bucket = bk_ref[s]
        base = rb_ref[s] * MOE_ROWS
        lo_row, hi_row = offs_ref[bucket] - base, offs_ref[bucket + 1] - base
        slots = (pa_ref[s], pb_ref[s])
        is_first = first_ref[s] == 1

        def run(r0, r1, z0, z1):
            rows = r0 + lax.broadcasted_iota(jnp.int32, (r1 - r0, 1), 0)
            cw = jnp.where((rows >= lo_row) & (rows < hi_row), cw_ref[r0:r1, :], 0.0)
            x = x_ref[r0:r1, :].astype(BF16)
            y = None
            for k in range(2):
                gate = jnp.dot(x, wg_s[slots[k]], preferred_element_type=F32)
                up = jnp.dot(x, wu_s[slots[k]], preferred_element_type=F32)
                act = (gate * jax.nn.sigmoid(gate) * up * cw[:, k:k + 1]).astype(BF16)
                yk = jnp.dot(act, wd_s[slots[k]], preferred_element_type=F32)
                y = yk if y is None else y + yk

            @pl.when(is_first)
            def _():
                o_ref[r0:r1, :] = y
                for a, b in ((z0, r0), (r1, z1)):
                    if b > a:
                        o_ref[a:b, :] = jnp.zeros((b - a, o_ref.shape[1]), F32)

            @pl.when(jnp.logical_not(is_first))
            def _():
                o_ref[r0:r1, :] += y

        for b0 in range(0, MOE_ROWS, MOE_SUB):
            b1, mid = b0 + MOE_SUB, b0 + MOE_SUB // 2
            has_rows = (lo_row < b1) & (hi_row > b0)
            needs_lower, needs_upper = has_rows & (lo_row < mid), has_rows & (hi_row > mid)
            pl.when(needs_lower & needs_upper)(lambda: run(b0, b1, b0, b1))
            pl.when(needs_lower & jnp.logical_not(needs_upper))(lambda: run(b0, mid, b0, b1))
            pl.when(jnp.logical_not(needs_lower) & needs_upper)(lambda: run(mid, b1, b0, b1))
            if MOE_ROWS > MOE_SUB:
                @pl.when(jnp.logical_not(has_rows) & is_first)
                def _():
                    o_ref[b0:b1, :] = jnp.zeros((MOE_SUB, o_ref.shape[1]), F32)


def _moe_plan_kernel(offs_ref, kind_ref, rb_ref, bk_ref, pa_ref, pb_ref, first_ref, cast_ref, cpos_ref, pe_ref,
                     irb, ibk, *, n_steps_max):
    i32 = jnp.int32
    ng, epg, ppg = N_EXPERT_GROUPS, EXPERTS_PER_GROUP, PAIRS_PER_GROUP

    shift = MOE_ROWS.bit_length() - 1

    def bucket_body(bk, cnt):
        a, b = offs_ref[bk], offs_ref[bk + 1]
        first_blk = lax.shift_right_logical(a, shift)
        n_blk = jnp.where(b > a, lax.shift_right_logical(b - 1, shift) - first_blk + 1, 0)

        def block_body(j, cnt):
            irb[cnt] = first_blk + j
            ibk[cnt] = bk
            return cnt + 1

        return lax.fori_loop(0, n_blk, block_body, cnt)

    n_items = lax.fori_loop(0, ng * ppg, bucket_body, i32(0))

    def count_body(i, m):
        g = ibk[i] // ppg
        return tuple(m[k] + (g == k).astype(i32) for k in range(ng))

    m = lax.fori_loop(0, n_items, count_body, (i32(0),) * ng)

    def next_group(g):
        nxt = i32(-1)
        for k in range(ng - 1, 0, -1):
            nxt = jnp.where((k > g) & (m[k] > 0), k, nxt)
        return nxt

    def emit(s, kind, rb, bk, pa, pb, first, cast, cpos, pe):
        kind_ref[s], rb_ref[s], bk_ref[s], pa_ref[s], pb_ref[s] = kind, rb, bk, pa, pb
        first_ref[s], cast_ref[s], cpos_ref[s], pe_ref[s] = first, cast, cpos, pe

    def item_body(i, carry):
        s, gcur, parity, q, last_pe, last_rb = carry
        rb, bk = irb[i], ibk[i]
        g, pos = bk // ppg, bk % ppg
        new = g != gcur
        started = gcur >= 0
        loaders = jnp.where(new, jnp.where(started, jnp.maximum(epg - q, 0), epg), 0)
        parity = jnp.where(new & started, 1 - parity, parity)
        q = jnp.where(new, 0, q)
        for j in range(epg):
            on = j >= epg - loaders
            emit(s, STEP_LOAD, rb, bk, 0, 0, 0, 1, parity * epg + j, epg * g + j)
            last_pe = jnp.where(on, epg * g + j, last_pe)
            s = s + on.astype(i32)
        nxt = next_group(g)
        pre = (q < epg) & (nxt >= 0)
        pe = jnp.where(pre, epg * nxt + q, last_pe)
        slot_a, slot_b = i32(PAIR_SLOTS[0][0]), i32(PAIR_SLOTS[0][1])
        for p in range(1, ppg):
            slot_a = jnp.where(pos == p, PAIR_SLOTS[p][0], slot_a)
            slot_b = jnp.where(pos == p, PAIR_SLOTS[p][1], slot_b)
        emit(s, STEP_ITEM, rb, bk, parity * epg + slot_a, parity * epg + slot_b, (rb != last_rb).astype(i32),
             pre.astype(i32), (1 - parity) * epg + q, pe)
        return s + 1, g, parity, q + 1, pe, rb

    s, _, _, _, last_pe, last_rb = lax.fori_loop(
        0, n_items, item_body, (i32(0), i32(-1), i32(0), i32(0), i32(0), i32(-1)))
    last_bk = ibk[jnp.maximum(n_items - 1, 0)]

    def pad_body(s, _):
        emit(s, STEP_PAD, last_rb, last_bk, 0, 0, 0, 0, 0, last_pe)
        return 0

    lax.fori_loop(s, n_steps_max, pad_body, 0)


def _moe_steps(bucket, t):
    i32 = jnp.int32
    nbk = N_EXPERT_GROUPS * PAIRS_PER_GROUP
    order = jnp.argsort(bucket, stable=True).astype(i32)
    counts = jnp.sum((bucket[None, :] == jnp.arange(nbk, dtype=i32)[:, None]).astype(i32), axis=1)
    offs = jnp.concatenate([jnp.zeros((1,), i32), jnp.cumsum(counts).astype(i32)])
    assert MOE_ROWS & (MOE_ROWS - 1) == 0
    n_items_max = t // MOE_ROWS + nbk - 1
    n_steps_max = n_items_max + N_EXPERTS
    smem = pl.BlockSpec(memory_space=pltpu.SMEM)
    tables = pl.pallas_call(
        functools.partial(_moe_plan_kernel, n_steps_max=n_steps_max),
        in_specs=[smem],
        out_specs=[smem] * 9,
        out_shape=[jax.ShapeDtypeStruct((n_steps_max,), i32)] * 9,
        scratch_shapes=[pltpu.SMEM((n_items_max + 1,), i32)] * 2,
        name="moe_plan",
    )(offs)
    return order, (*tables, offs), n_steps_max


def _moe(xs, cws, w_gate, w_up, w_down, layer, tables, n_steps_max):
    t, d = xs.shape
    ff = w_gate.shape[3]
    w_map = lambda s, kind, rb, bk, pa, pb, fi, ca, cp, pe, of: (layer, pe[s], 0, 0)
    row_map = lambda s, kind, rb, *_: (rb[s], 0)
    nres = 2 * EXPERTS_PER_GROUP
    grid_spec = pltpu.PrefetchScalarGridSpec(
        num_scalar_prefetch=len(tables),
        grid=(n_steps_max,),
        in_specs=[pl.BlockSpec((MOE_ROWS, d), row_map), pl.BlockSpec((MOE_ROWS, 2), row_map),
                  pl.BlockSpec((None, None, d, ff), w_map), pl.BlockSpec((None, None, d, ff), w_map),
                  pl.BlockSpec((None, None, ff, d), w_map)],
        out_specs=pl.BlockSpec((MOE_ROWS, d), row_map),
        scratch_shapes=[pltpu.VMEM((nres, d, ff), BF16), pltpu.VMEM((nres, d, ff), BF16),
                        pltpu.VMEM((nres, ff, d), BF16)],
    )
    return pl.pallas_call(
        _moe_kernel,
        grid_spec=grid_spec,
        out_shape=jax.ShapeDtypeStruct((t, d), F32),
        compiler_params=_params(("arbitrary",), BIG_VMEM_LIMIT),
        name="moe_grouped",
    )(*tables, xs, cws, w_gate, w_up, w_down)


def _take_rows(a, idx):
    return a.at[idx].get(mode="promise_in_bounds", unique_indices=True)


def _final_kernel(x_ref, y_ref, g_ref, o_ref):
    o_ref[...] = x_ref[...] + g_ref[...] * y_ref[...]


def _final(x1, y, g2, seq):
    t, d = x1.shape
    tm = min(TOKEN_TILE, seq)
    per_b = seq // tm
    tok = lambda w: pl.BlockSpec((tm, w), lambda i: (i, 0))
    return pl.pallas_call(
        _final_kernel,
        grid=(t // tm,),
        in_specs=[tok(d), tok(d), pl.BlockSpec((None, 1, d), lambda i: (i // per_b, 0, 0))],
        out_specs=tok(d),
        out_shape=jax.ShapeDtypeStruct((t, d), F32),
        compiler_params=_params(("parallel",)),
        name="final_residual",
    )(x1, y, g2)


def kernel(x, c, positions, ada_w, ada_b, norm1_g, w_in, q_norm_g, k_norm_g, attn_sink, lam_re, lam_im, ssm_b_re, ssm_b_im, ssm_c_re, ssm_c_im, ssm_d, ssm_log_dt, w_glu, attn_out_g, ssm_out_g, w_out, norm2_g, w_router, router_bias, w_exp_gate, w_exp_up, w_exp_down):
    batch, seq, d = x.shape
    depth = ada_w.shape[0]
    t = batch * seq
    assert seq % ATTN_BLOCK == 0 and seq % SSM_CHUNK == 0 and t % MOE_ROWS == 0

    mod = _adaln_mod(c, ada_w, ada_b).reshape(depth, 6, batch, 1, d)
    cos, sin = _rope_tables(positions)
    head_sum, rot = _rope_constants()
    bias = _attn_bias()
    w_router_t = w_router.T
    s5_mats = _s5_prep(lam_re, lam_im, ssm_b_re, ssm_b_im, ssm_c_re, ssm_c_im, ssm_d, ssm_log_dt)
    router_bias_col = router_bias.reshape(N_EXPERTS, 1)

    xf = x.reshape(t, d)
    res = None
    for l in range(depth):
        sh1, sc1, g1, sh2, sc2, g2 = (mod[l, j] for j in range(6))
        qg = (jnp.tile(q_norm_g[l], N_Q_HEADS) * (HEAD_DIM ** -0.5 * LOG2_E)).reshape(1, ATTN_WIDTH)
        kg = jnp.tile(k_norm_g[l], N_KV_HEADS).reshape(1, KV_WIDTH)
        outs = _inproj(xf, res, sc1, sh1, norm1_g[l].reshape(1, d), w_in, l, qg, kg, head_sum, rot, cos, sin, seq)
        if res is None:
            q, kx, vx, uc = outs
        else:
            q, kx, vx, uc, xf = outs
        attn = _attention(q, kx, vx, attn_sink[l], attn_out_g[l].reshape(1, ATTN_WIDTH), bias, batch, seq)
        yc = _s5_scan(uc, s5_mats, l, seq // SSM_CHUNK, batch)
        x1, h2, cw, gid = _post(xf, attn, yc, w_glu, ssm_out_g[l].reshape(1, SSM_WIDTH), w_out, l, g1,
                                norm2_g[l].reshape(1, d), sc2, sh2, w_router_t, router_bias_col, seq)
        order, tables, n_steps_max = _moe_steps(gid.reshape(t), t)
        y_sorted = _moe(_take_rows(h2, order), _take_rows(cw.T, order), w_exp_gate, w_exp_up, w_exp_down, l,
                        tables, n_steps_max)
        y = _take_rows(y_sorted, jnp.argsort(order).astype(jnp.int32))
        xf, res = x1, (y, g2)
    y, g2 = res
    return _final(xf, y, g2, seq).reshape(batch, seq, d)
```

```python
import functools
import math

import numpy as np
import jax
import jax.numpy as jnp
from jax import lax
from jax.experimental import pallas as pl
from jax.experimental.pallas import tpu as pltpu

F32 = jnp.float32
BF16 = jnp.bfloat16

HEAD_DIM = 64
N_Q_HEADS = 8
N_KV_HEADS = 2
Q_PER_KV = N_Q_HEADS // N_KV_HEADS
ATTN_WIDTH = N_Q_HEADS * HEAD_DIM
KV_WIDTH = N_KV_HEADS * HEAD_DIM
ATTN_BLOCK = 128
ATTN_Q_TILE = 2048
ROPE_THETA = 10000.0
ROPE_SLAB = 256
LANES = 128
SSM_GROUP_CH = 16
SSM_GROUPS = 32
SSM_WIDTH = SSM_GROUPS * SSM_GROUP_CH
SSM_STATE = 64
SSM_CHUNK = 16
SSM_COLS = SSM_WIDTH // LANES
COL_GROUPS = LANES // SSM_GROUP_CH
COL_STATE = COL_GROUPS * SSM_STATE
CHUNK_LANES = SSM_CHUNK * LANES
SSM_NSPLIT = 4
S5_ROW_PAD = 8
N_EXPERTS = 16
N_EXPERT_GROUPS = 4
EXPERTS_PER_GROUP = N_EXPERTS // N_EXPERT_GROUPS
PAIRS_PER_GROUP = EXPERTS_PER_GROUP * (EXPERTS_PER_GROUP - 1) // 2
PAIR_SLOTS = ((0, 1), (0, 2), (0, 3), (1, 3), (1, 2), (3, 2))
EPS = 1e-6
LOG2_E = math.log2(math.e)
MASK_BIAS = -1e30

TOKEN_TILE = 1024
MOE_ROWS = 512
MOE_SUB = 256
VMEM_LIMIT = 48 * 1024 * 1024
BIG_VMEM_LIMIT = 56 * 1024 * 1024
STEP_PAD, STEP_LOAD, STEP_ITEM = 0, 1, 2


def _params(sem, vmem=None):
    return pltpu.CompilerParams(dimension_semantics=sem, vmem_limit_bytes=vmem)


def _rms(x, g):
    return x * lax.rsqrt(jnp.mean(x * x, axis=-1, keepdims=True) + EPS) * g


def _mod_kernel(c_ref, w_ref, b_ref, o_ref):
    c = c_ref[...]
    s = c * jax.nn.sigmoid(c)
    o_ref[...] = jnp.dot(s.astype(BF16), w_ref[...].astype(BF16), preferred_element_type=F32) + b_ref[...]


def _adaln_mod(c, ada_w, ada_b):
    depth, d, d6 = ada_w.shape
    nb = c.shape[0]
    n6 = d6 // d
    return pl.pallas_call(
        _mod_kernel,
        grid=(depth, n6),
        in_specs=[pl.BlockSpec((nb, d), lambda l, j: (0, 0)),
                  pl.BlockSpec((None, d, d), lambda l, j: (l, 0, j)),
                  pl.BlockSpec((None, None, 1, d), lambda l, j: (l, j, 0, 0))],
        out_specs=pl.BlockSpec((None, None, nb, d), lambda l, j: (l, j, 0, 0)),
        out_shape=jax.ShapeDtypeStruct((depth, n6, nb, d), F32),
        compiler_params=_params(("arbitrary", "arbitrary"), VMEM_LIMIT),
        name="adaln_mod",
    )(c, ada_w, ada_b.reshape(depth, n6, 1, d))


def _spread(x, expander3):
    hi = x.astype(BF16)
    r1 = x - hi.astype(F32)
    mid = r1.astype(BF16)
    lo = (r1 - mid.astype(F32)).astype(BF16)
    return jnp.dot(jnp.concatenate([hi, mid, lo], axis=1), expander3, preferred_element_type=F32)


def _rope_kernel(pos_ref, freq_ref, lane_ref, quarter_ref, cos_ref, sin_ref):
    per_row = pos_ref.shape[1]
    rows = pos_ref.shape[0]
    pos = _spread(pos_ref[...].astype(F32), lane_ref[...])
    ang = pos * freq_ref[...]
    cos, sin = jnp.cos(ang), jnp.sin(ang)
    for j in range(per_row):
        cos_ref[pl.ds(j, rows, stride=per_row), :] = _spread(cos, quarter_ref[j])
        sin_ref[pl.ds(j, rows, stride=per_row), :] = _spread(sin, quarter_ref[j])


def _rope_tables(positions):
    half = HEAD_DIM // 2
    t = positions.size
    per_row = LANES // half
    rows = t // per_row
    freq = (ROPE_THETA ** (-np.arange(half, dtype=np.float64) / half)).astype(np.float32)
    freq_row = jnp.asarray(np.tile(freq, per_row)[None, :])
    to_quarter = np.repeat(np.eye(per_row, dtype=np.float32), half, axis=1)
    spread = np.zeros((per_row, LANES, LANES), np.float32)
    for j in range(per_row):
        spread[j, j * half:(j + 1) * half, :] = np.tile(np.eye(half, dtype=np.float32), (1, per_row))
    lane_sel = jnp.asarray(np.tile(to_quarter, (3, 1)), BF16)
    quarter_sel = jnp.asarray(np.tile(spread, (1, 3, 1)), BF16)
    blk = min(rows, 512)
    out = pl.BlockSpec((blk * per_row, LANES), lambda i: (i, 0))
    const = lambda a: pl.BlockSpec(a.shape, lambda i: (0,) * a.ndim)
    return pl.pallas_call(
        _rope_kernel,
        grid=(rows // blk,),
        in_specs=[pl.BlockSpec((blk, per_row), lambda i: (i, 0)), const(freq_row), const(lane_sel), const(quarter_sel)],
        out_specs=[out, out],
        out_shape=[jax.ShapeDtypeStruct((t, LANES), F32)] * 2,
        compiler_params=_params(("arbitrary",)),
        name="rope_tables",
    )(positions.reshape(rows, per_row), freq_row, lane_sel, quarter_sel)


def _rope_constants():
    lane = np.arange(ROPE_SLAB)
    head_sum = (lane[:, None] // HEAD_DIM == lane[None, :] // HEAD_DIM).astype(np.float32)
    half = HEAD_DIM // 2
    rot = np.zeros((ROPE_SLAB, ROPE_SLAB), np.float32)
    for d in range(ROPE_SLAB):
        if d % HEAD_DIM < half:
            rot[d + half, d] = -1.0
        else:
            rot[d - half, d] = 1.0
    return jnp.asarray(head_sum, BF16), jnp.asarray(rot, BF16)


RING_SLOTS = 3


def _ring_tile(x_hbm, ring, sem, n_tiles):
    i = pl.program_id(0)
    tm = ring.shape[1]

    def copy(j):
        slot = j % RING_SLOTS
        return pltpu.make_async_copy(x_hbm.at[pl.ds(j * tm, tm), :], ring.at[slot], sem.at[slot])

    @pl.when(i == 0)
    def _():
        for j in range(min(RING_SLOTS - 1, n_tiles)):
            copy(j).start()

    @pl.when(i + (RING_SLOTS - 1) < n_tiles)
    def _():
        copy(i + (RING_SLOTS - 1)).start()

    copy(i).wait()
    return ring[i % RING_SLOTS]


def _inproj_kernel(*refs, has_res, n_tiles):
    if has_res:
        (x_hbm, y_ref, g2_ref, sc_ref, sh_ref, n1_ref, w_ref, qg_ref, kg_ref, hs_ref, rot_ref,
         cos_ref, sin_ref, q_ref, k_ref, v_ref, uc_ref, xo_ref, u_scr, w_s, x_ring, x_sem) = refs
        x = _ring_tile(x_hbm, x_ring, x_sem, n_tiles) + g2_ref[...] * y_ref[...]
        xo_ref[...] = x
    else:
        (x_hbm, sc_ref, sh_ref, n1_ref, w_ref, qg_ref, kg_ref, hs_ref, rot_ref,
         cos_ref, sin_ref, q_ref, k_ref, v_ref, uc_ref, u_scr, w_s, x_ring, x_sem) = refs
        x = _ring_tile(x_hbm, x_ring, x_sem, n_tiles)

    @pl.when(pl.program_id(0) == 0)
    def _():
        w_s[...] = w_ref[...].astype(BF16)

    h = _rms(x, n1_ref[...] * (1.0 + sc_ref[...])) + sh_ref[...]
    proj = jnp.dot(h.astype(BF16), w_s[...], preferred_element_type=F32)
    q = proj[:, :ATTN_WIDTH]
    k = proj[:, ATTN_WIDTH:ATTN_WIDTH + KV_WIDTH]
    v = proj[:, ATTN_WIDTH + KV_WIDTH:ATTN_WIDTH + 2 * KV_WIDTH]
    cos = cos_ref[...]
    sin = sin_ref[...]
    reps = ATTN_WIDTH // LANES
    cos_q = jnp.concatenate([cos] * reps, axis=1)
    sin_q = jnp.concatenate([sin] * reps, axis=1)

    def head_norm_rope(t, gain, c, s):
        outs = []
        for lo in range(0, t.shape[1], ROPE_SLAB):
            wd = min(ROPE_SLAB, t.shape[1] - lo)
            ts, lanes = t[:, lo:lo + wd], slice(lo, lo + wd)
            ssq = jnp.dot((ts * ts).astype(BF16), hs_ref[:wd, :wd], preferred_element_type=F32)
            tn = (ts * lax.rsqrt(ssq * (1.0 / HEAD_DIM) + EPS) * gain[:, lanes]).astype(BF16)
            tr = jnp.dot(tn, rot_ref[:wd, :wd], preferred_element_type=F32)
            outs.append(tn.astype(F32) * c[:, lanes] + tr * s[:, lanes])
        return outs[0] if len(outs) == 1 else jnp.concatenate(outs, axis=1)

    qo = head_norm_rope(q, qg_ref[...], cos_q, sin_q)
    ko = head_norm_rope(k, kg_ref[...], cos, sin)
    q_ref[...] = qo.astype(BF16)
    k_ref[...] = jnp.concatenate([ko, pltpu.roll(ko, HEAD_DIM, axis=1)], axis=1).astype(BF16)
    v_ref[...] = jnp.concatenate([v, pltpu.roll(v, HEAD_DIM, axis=1)], axis=1).astype(BF16)
    u0 = ATTN_WIDTH + 2 * KV_WIDTH
    nchunk = u_scr.shape[1] // SSM_CHUNK
    for j in range(SSM_COLS):
        u_scr[j] = proj[:, u0 + j * LANES:u0 + (j + 1) * LANES]
    for s in range(SSM_CHUNK):
        for j in range(SSM_COLS):
            lanes = slice(s * SSM_WIDTH + j * LANES, s * SSM_WIDTH + (j + 1) * LANES)
            uc_ref[:, lanes] = u_scr[j, pl.ds(s, nchunk, stride=SSM_CHUNK), :].astype(BF16)


def _inproj(x, res, sc1, sh1, n1g, w_in, layer, qg, kg, head_sum, rot, cos, sin, seq):
    t, d = x.shape
    tm = min(TOKEN_TILE, seq)
    per_b = seq // tm
    in_width = w_in.shape[2]
    tok = lambda w: pl.BlockSpec((tm, w), lambda i: (i, 0))
    const = lambda a: pl.BlockSpec(a.shape, lambda i: (0,) * a.ndim)
    per_batch = pl.BlockSpec((None, 1, d), lambda i: (i // per_b, 0, 0))
    chunked = pl.BlockSpec((tm // SSM_CHUNK, SSM_CHUNK * SSM_WIDTH), lambda i: (i, 0))
    ins, specs = [x], [pl.BlockSpec(memory_space=pl.ANY)]
    if res is not None:
        y_prev, g2_prev = res
        ins += [y_prev, g2_prev]
        specs += [tok(d), per_batch]
    ins += [sc1, sh1, n1g, w_in, qg, kg, head_sum, rot, cos, sin]
    specs += [per_batch, per_batch, const(n1g),
              pl.BlockSpec((None, d, in_width), lambda i: (layer, 0, 0), pipeline_mode=pl.Buffered(1)),
              const(qg), const(kg), const(head_sum), const(rot), tok(LANES), tok(LANES)]
    out_shape = [jax.ShapeDtypeStruct((t, ATTN_WIDTH), BF16), jax.ShapeDtypeStruct((t, 2 * KV_WIDTH), BF16),
                 jax.ShapeDtypeStruct((t, 2 * KV_WIDTH), BF16),
                 jax.ShapeDtypeStruct((t // SSM_CHUNK, SSM_CHUNK * SSM_WIDTH), BF16)]
    out_specs = [tok(ATTN_WIDTH), tok(2 * KV_WIDTH), tok(2 * KV_WIDTH), chunked]
    if res is not None:
        out_shape.append(jax.ShapeDtypeStruct((t, d), F32))
        out_specs.append(tok(d))
    assert in_width == ATTN_WIDTH + 2 * KV_WIDTH + SSM_WIDTH
    return pl.pallas_call(
        functools.partial(_inproj_kernel, has_res=res is not None, n_tiles=t // tm),
        grid=(t // tm,),
        in_specs=specs,
        out_specs=out_specs,
        out_shape=out_shape,
        scratch_shapes=[pltpu.VMEM((SSM_COLS, tm, LANES), F32), pltpu.VMEM((d, in_width), BF16),
                        pltpu.VMEM((RING_SLOTS, tm, d), F32), pltpu.SemaphoreType.DMA((RING_SLOTS,))],
        compiler_params=_params(("arbitrary",), BIG_VMEM_LIMIT),
        name="inproj",
    )(*ins)


def _attn_kernel(sink_ref, q_ref, kc_ref, kp_ref, vc_ref, vp_ref, bias_ref, g_ref, o_ref):
    nsub = q_ref.shape[0] // ATTN_BLOCK
    kk = jnp.concatenate([kp_ref[...], kc_ref[...]], axis=0)
    vv = jnp.concatenate([vp_ref[...], vc_ref[...]], axis=0)
    low = lax.broadcasted_iota(jnp.int32, (kk.shape[0], KV_WIDTH), 1) < HEAD_DIM
    zero = jnp.zeros((kk.shape[0], KV_WIDTH), BF16)

    def variants(a):
        nat, swp = a[:, :KV_WIDTH], a[:, KV_WIDTH:]
        return {(0, 0): jnp.where(low, nat, zero), (0, 1): jnp.where(low, zero, swp),
                (1, 0): jnp.where(low, swp, zero), (1, 1): jnp.where(low, zero, nat)}

    kvar, vvar = variants(kk), variants(vv)
    band = bias_ref[1]
    first = bias_ref[jnp.minimum(pl.program_id(1), 1)]
    upper = lax.broadcasted_iota(jnp.int32, (2 * ATTN_BLOCK, 1), 0) < ATTN_BLOCK
    for j in range(nsub):
        bias = first if j == 0 else band
        bias2 = jnp.concatenate([bias, bias], axis=0)
        keys = slice(j * ATTN_BLOCK, (j + 2) * ATTN_BLOCK)
        qrows = slice(j * ATTN_BLOCK, (j + 1) * ATTN_BLOCK)
        tiles = [None] * (N_Q_HEADS // 2)
        for kv in range(N_KV_HEADS):
            for half in range(2):
                pairs = (2 * kv, 2 * kv + 1)
                heads = (2 * pairs[0] + half, 2 * pairs[1] + half)
                qs = jnp.concatenate([q_ref[qrows, p * LANES:(p + 1) * LANES] for p in pairs], axis=0)
                s = lax.dot_general(qs, kvar[(kv, half)][keys], (((1,), (1,)), ((), ())),
                                    preferred_element_type=F32) + bias2
                sink = jnp.where(upper, sink_ref[heads[0]], sink_ref[heads[1]]) * LOG2_E
                m = jnp.maximum(jnp.max(s, axis=-1, keepdims=True), sink)
                p = jnp.exp2(s - m)
                denom = jnp.sum(p, axis=-1, keepdims=True) + jnp.exp2(sink - m)
                o = jnp.dot(p.astype(BF16), vvar[(kv, half)][keys], preferred_element_type=F32) * (1.0 / denom)
                for r, pr in enumerate(pairs):
                    part = o[r * ATTN_BLOCK:(r + 1) * ATTN_BLOCK]
                    tiles[pr] = part if tiles[pr] is None else tiles[pr] + part
        a = jnp.concatenate(tiles, axis=1)
        o_ref[qrows, :] = _rms(a, g_ref[...]).astype(BF16)


def _attn_bias():
    qi = np.arange(ATTN_BLOCK)[:, None]
    sj = np.arange(2 * ATTN_BLOCK)[None, :]
    diff = qi + ATTN_BLOCK - sj
    band = (diff >= 0) & (diff < ATTN_BLOCK)
    first = band & (sj >= ATTN_BLOCK)
    return jnp.asarray(np.where(np.stack([first, band]), 0.0, MASK_BIAS).astype(np.float32))


def _attention(q, kx, vx, sink, out_g, bias, batch, seq):
    t = q.shape[0]
    qb = min(ATTN_Q_TILE, seq)
    nsub = qb // ATTN_BLOCK
    nq = seq // qb
    nb = seq // ATTN_BLOCK
    cur = lambda w: pl.BlockSpec((qb, w), lambda b, n, s: (b * nq + n, 0))
    prev = lambda w: pl.BlockSpec((ATTN_BLOCK, w), lambda b, n, s: (b * nb + jnp.maximum(n * nsub - 1, 0), 0))
    grid_spec = pltpu.PrefetchScalarGridSpec(
        num_scalar_prefetch=1,
        grid=(batch, nq),
        in_specs=[cur(ATTN_WIDTH), cur(2 * KV_WIDTH), prev(2 * KV_WIDTH), cur(2 * KV_WIDTH), prev(2 * KV_WIDTH),
                  pl.BlockSpec(bias.shape, lambda b, n, s: (0, 0, 0)),
                  pl.BlockSpec((1, ATTN_WIDTH), lambda b, n, s: (0, 0))],
        out_specs=cur(ATTN_WIDTH),
    )
    return pl.pallas_call(
        _attn_kernel,
        grid_spec=grid_spec,
        out_shape=jax.ShapeDtypeStruct((t, ATTN_WIDTH), BF16),
        compiler_params=_params(("parallel", "arbitrary")),
        name="swa_attention",
    )(sink, q, kx, kx, vx, vx, bias, out_g)


def _s5_prep_kernel(lr_re_ref, lr_im_ref, ldt_ref, bt_re_ref, bt_im_ref, ct_re_ref, ct_im_ref,
                    d_ref, lcol_re_ref, lcol_im_ref, ldtcol_ref, exp_ref, exph_ref, expt_ref, expw_ref,
                    t_ref, w_ref, v_ref, la_ref, lb_ref):
    hi = lax.Precision.HIGHEST
    nl = SSM_CHUNK
    low = lax.broadcasted_iota(jnp.int32, (1, 2 * SSM_STATE), 1) < SSM_STATE
    row_low = lax.broadcasted_iota(jnp.int32, (2 * SSM_STATE, 1), 0) < SSM_STATE
    jcol = lax.broadcasted_iota(jnp.int32, (nl, 1), 0).astype(F32)
    kt_lane = lax.broadcasted_iota(jnp.int32, (SSM_GROUP_CH, nl * SSM_GROUP_CH), 1)
    kt_row = lax.broadcasted_iota(jnp.int32, (SSM_GROUP_CH, nl * SSM_GROUP_CH), 0)

    w_all, v_all, kt_all = [], [], []
    for gm in range(COL_GROUPS):
        dt = jnp.exp(ldt_ref[gm])
        lam_re, lam_im = lr_re_ref[gm], lr_im_ref[gm]
        a_r, th_r = lam_re * dt, lam_im * dt

        er = jnp.exp(jcol * a_r)
        pw_re, pw_im = er * jnp.cos(jcol * th_r), er * jnp.sin(jcol * th_r)

        nr, ni = pw_re[1:2, :] - 1.0, pw_im[1:2, :]
        den = lam_re * lam_re + lam_im * lam_im
        c_re, c_im = (nr * lam_re + ni * lam_im) / den, (ni * lam_re - nr * lam_im) / den
        bt_re, bt_im = bt_re_ref[gm], bt_im_ref[gm]
        bb_re, bb_im = c_re * bt_re - c_im * bt_im, c_re * bt_im + c_im * bt_re

        w_rows = []
        for s in range(nl):
            j = nl - 1 - s
            pr, pi = pw_re[j:j + 1, :], pw_im[j:j + 1, :]
            w_rows.append(jnp.where(low, pr * bb_re - pi * bb_im, pr * bb_im + pi * bb_re))
        w_all.append(w_rows)

        pw_re_t, pw_im_t = pw_re.T, pw_im.T
        pc, ps = _spread(pw_re_t, exp_ref[...]), _spread(pw_im_t, exp_ref[...])
        ct_re, ct_im = _spread(ct_re_ref[gm], exph_ref[...]), _spread(ct_im_ref[gm], exph_ref[...])
        a_re, a_im = ct_re * pc - ct_im * ps, ct_re * ps + ct_im * pc
        a_cat = jnp.where(row_low, a_re, -a_im)
        l1_re, l1_im = pw_re_t[:, 1:2], pw_im_t[:, 1:2]
        v_re, v_im = a_re * l1_re - a_im * l1_im, a_re * l1_im + a_im * l1_re
        v_all.append(jnp.where(row_low, v_re, -v_im))

        kt = jnp.dot(jnp.where(low, bb_re, bb_im), a_cat, precision=hi, preferred_element_type=F32)
        kt_all.append(kt + jnp.where(kt_lane == kt_row, d_ref[gm], 0.0))

    def same_group(shape, row_group, lane_group):
        r = lax.broadcasted_iota(jnp.int32, shape, 0)
        c = lax.broadcasted_iota(jnp.int32, shape, 1)
        return (row_group(r) == lane_group(c)).astype(F32)

    chan_group = lambda i: (i >> 4) & (COL_GROUPS - 1)
    state_group = lambda i: (i >> 6) & (COL_GROUPS - 1)
    over_steps = lambda m: jnp.concatenate([m] * nl, axis=1)

    kt_wide = jnp.dot(jnp.concatenate(kt_all, axis=0).astype(BF16), expt_ref[...], preferred_element_type=F32)
    bd = (kt_wide * over_steps(same_group((LANES, LANES), chan_group, chan_group))).astype(BF16)
    t_ref[0:LANES, :] = bd
    for s in range(1, nl):
        t_ref[s * LANES:(s + 1) * LANES, :] = jnp.concatenate(
            [jnp.zeros((LANES, s * LANES), BF16), bd[:, :CHUNK_LANES - s * LANES]], axis=1)

    w_stack = jnp.concatenate([w_all[gm][s] for s in range(nl) for gm in range(COL_GROUPS)], axis=0)
    w_wide = jnp.dot(w_stack.astype(BF16), expw_ref[...], preferred_element_type=F32)
    w_mask = same_group((LANES, 2 * COL_STATE), chan_group, state_group)
    w_ref[...] = (w_wide.reshape(nl, LANES, 2 * COL_STATE) * w_mask[None]).reshape(CHUNK_LANES, 2 * COL_STATE).astype(BF16)

    v_stack = jnp.concatenate([v_all[gm][half * SSM_STATE:(half + 1) * SSM_STATE, :]
                               for half in range(2) for gm in range(COL_GROUPS)], axis=0)
    v_wide = jnp.dot(v_stack.astype(BF16), expt_ref[...], preferred_element_type=F32)
    v_ref[...] = (v_wide * over_steps(same_group((2 * COL_STATE, LANES), state_group, chan_group))).astype(BF16)

    dtc = jnp.exp(ldtcol_ref[...])
    e16 = jnp.exp(nl * lcol_re_ref[...] * dtc)
    ang = nl * lcol_im_ref[...] * dtc
    la_ref[...] = e16 * jnp.cos(ang)
    lb_ref[...] = e16 * jnp.sin(ang)


def _s5_prep(lam_re, lam_im, b_re, b_im, c_re, c_im, d_skip, log_dt):
    g, p, h, nl = SSM_GROUPS, SSM_STATE, SSM_GROUP_CH, SSM_CHUNK
    cg = COL_GROUPS
    nc = lam_re.shape[0] * SSM_COLS
    col = lambda a: a.reshape((nc, cg) + a.shape[2:])
    dup_row = lambda a: col(jnp.tile(a, (1, 1, 2))[:, :, None, :])
    bt = lambda a: col(jnp.tile(jnp.swapaxes(a, 2, 3), (1, 1, 1, 2)))
    ct = lambda a: col(jnp.tile(jnp.swapaxes(a, 2, 3), (1, 1, 2, 1)))
    d_pad = col(jnp.pad(d_skip.reshape(-1, g, 1, h), ((0, 0), (0, 0), (0, 0), (0, nl * h - h))))
    wide = lambda a: a.reshape(nc, 1, cg * p)
    expand = jnp.asarray(np.tile(np.repeat(np.eye(nl, dtype=np.float32), h, axis=1), (3, 1)), BF16)
    expand_h = jnp.asarray(np.tile(np.eye(h, dtype=np.float32), (3, nl)), BF16)
    exp_t = np.zeros((nl, h, nl, cg, h), np.float32)
    exp_w = np.zeros((2, p, 2, cg, p), np.float32)
    for gm in range(cg):
        exp_t[:, :, :, gm, :] = np.eye(nl * h, dtype=np.float32).reshape(nl, h, nl, h)
        exp_w[:, :, :, gm, :] = np.eye(2 * p, dtype=np.float32).reshape(2, p, 2, p)
    exp_t = jnp.asarray(exp_t.reshape(nl * h, CHUNK_LANES), BF16)
    exp_w = jnp.asarray(exp_w.reshape(2 * p, 2 * COL_STATE), BF16)
    blk = lambda *s: pl.BlockSpec((None,) + s, lambda i: (i,) + (0,) * len(s))
    const = lambda a: pl.BlockSpec(a.shape, lambda i: (0,) * a.ndim)
    lw = nl * h
    return pl.pallas_call(
        _s5_prep_kernel,
        grid=(nc,),
        in_specs=[blk(cg, 1, 2 * p), blk(cg, 1, 2 * p), blk(cg, 1, 1),
                  blk(cg, h, 2 * p), blk(cg, h, 2 * p), blk(cg, 2 * p, h), blk(cg, 2 * p, h), blk(cg, 1, lw),
                  blk(1, cg * p), blk(1, cg * p), blk(1, cg * p), const(expand), const(expand_h), const(exp_t),
                  const(exp_w)],
        out_specs=[blk(CHUNK_LANES, CHUNK_LANES), blk(CHUNK_LANES, 2 * COL_STATE), blk(2 * COL_STATE, CHUNK_LANES),
                   blk(1, COL_STATE), blk(1, COL_STATE)],
        out_shape=[jax.ShapeDtypeStruct((nc, CHUNK_LANES, CHUNK_LANES), BF16),
                   jax.ShapeDtypeStruct((nc, CHUNK_LANES, 2 * COL_STATE), BF16),
                   jax.ShapeDtypeStruct((nc, 2 * COL_STATE, CHUNK_LANES), BF16),
                   jax.ShapeDtypeStruct((nc, 1, COL_STATE), F32), jax.ShapeDtypeStruct((nc, 1, COL_STATE), F32)],
        compiler_params=_params(("parallel",), VMEM_LIMIT),
        name="s5_prep",
    )(dup_row(lam_re), dup_row(lam_im), col(log_dt[:, :, None, None]),
      bt(b_re), bt(b_im), ct(c_re), ct(c_im), d_pad, wide(lam_re), wide(lam_im),
      wide(jnp.repeat(log_dt, p, axis=1)), expand, expand_h, exp_t, exp_w)


def _s5_kernel(*refs, nchunks, nb):
    uc_refs = refs[:SSM_CHUNK]
    t_ref, w_ref, v_ref, la_ref, lb_ref, o_ref, ucat_ref, s_ref, xp_ref = refs[SSM_CHUNK:]

    @pl.when(pl.program_id(1) == 0)
    def _():
        for s in range(SSM_CHUNK):
            ucat_ref[:, s * LANES:(s + 1) * LANES] = uc_refs[s][...]
        s_in = jnp.dot(ucat_ref[...], w_ref[...], preferred_element_type=F32)
        nblk = COL_STATE // LANES
        pitch = s_ref.shape[1] // nb
        for b in range(2 * nblk):
            for q in range(nb):
                s_ref[b, q * pitch:q * pitch + nchunks, :] = s_in[q * nchunks:(q + 1) * nchunks, b * LANES:(b + 1) * LANES]
        lr = [jnp.broadcast_to(la_ref[:, b * LANES:(b + 1) * LANES], (nb, LANES)) for b in range(nblk)]
        li = [jnp.broadcast_to(lb_ref[:, b * LANES:(b + 1) * LANES], (nb, LANES)) for b in range(nblk)]

        def step(c, carry):
            rows = pl.ds(c, nb, stride=pitch)
            out = []
            for b in range(nblk):
                re, im = carry[2 * b], carry[2 * b + 1]
                xp_ref[b, rows, :] = re
                xp_ref[nblk + b, rows, :] = im
                out.append(lr[b] * re - li[b] * im + s_ref[b, rows, :])
                out.append(lr[b] * im + li[b] * re + s_ref[nblk + b, rows, :])
            return tuple(out)

        zero = jnp.zeros((nb, LANES), F32)
        lax.fori_loop(0, nchunks, step, (zero,) * (2 * nblk), unroll=4)

    pitch = xp_ref.shape[1] // nb
    xp = jnp.concatenate(
        [jnp.concatenate([xp_ref[b, q * pitch:q * pitch + nchunks, :] for q in range(nb)], axis=0)
         for b in range(2 * COL_STATE // LANES)], axis=1).astype(BF16)
    inter = jnp.dot(xp, v_ref[...], preferred_element_type=F32)
    for kk in range(SSM_NSPLIT):
        @pl.when(pl.program_id(1) == kk)
        def _():
            live = (kk + 1) * (CHUNK_LANES // SSM_NSPLIT)
            intra = jnp.dot(ucat_ref[:, :live], t_ref[:live, :], preferred_element_type=F32)
            o_ref[...] = (intra + inter).astype(BF16)


def _s5_scan(uc, mats, layer, nchunks, nb):
    rows = uc.shape[0]
    c0 = layer * SSM_COLS
    split = CHUNK_LANES // SSM_NSPLIT
    u_spec = lambda s: pl.BlockSpec((rows, LANES), lambda j, k: (0, SSM_COLS * s + j))
    return pl.pallas_call(
        functools.partial(_s5_kernel, nchunks=nchunks, nb=nb),
        grid=(SSM_COLS, SSM_NSPLIT),
        in_specs=[u_spec(s) for s in range(SSM_CHUNK)] + [
            pl.BlockSpec((None, CHUNK_LANES, split), lambda j, k: (c0 + j, 0, k)),
            pl.BlockSpec((None, CHUNK_LANES, 2 * COL_STATE), lambda j, k: (c0 + j, 0, 0)),
            pl.BlockSpec((None, 2 * COL_STATE, split), lambda j, k: (c0 + j, 0, k)),
            pl.BlockSpec((None, 1, COL_STATE), lambda j, k: (c0 + j, 0, 0)),
            pl.BlockSpec((None, 1, COL_STATE), lambda j, k: (c0 + j, 0, 0))],
        out_specs=pl.BlockSpec((None, rows, split), lambda j, k: (j, 0, k)),
        out_shape=jax.ShapeDtypeStruct((SSM_COLS, rows, CHUNK_LANES), BF16),
        scratch_shapes=[pltpu.VMEM((rows, CHUNK_LANES), BF16),
                        pltpu.VMEM((2 * COL_STATE // LANES, nb * (nchunks + S5_ROW_PAD), LANES), F32),
                        pltpu.VMEM((2 * COL_STATE // LANES, nb * (nchunks + S5_ROW_PAD), LANES), F32)],
        compiler_params=_params(("parallel", "arbitrary"), VMEM_LIMIT),
        name="s5_scan",
    )(*([uc] * SSM_CHUNK), *mats)


def _route(logits, bias):
    m = jnp.max(logits, axis=0, keepdims=True)
    e = jnp.exp(logits - m)
    probs = e / jnp.sum(e, axis=0, keepdims=True)
    sel = probs + bias
    row = lambda a, i: a[i:i + 1, :]
    best_score, best = None, None
    for grp in range(N_EXPERT_GROUPS):
        a, b, c, d = (row(sel, EXPERTS_PER_GROUP * grp + i) for i in range(EXPERTS_PER_GROUP))
        hab, lab, hcd, lcd = jnp.maximum(a, b), jnp.minimum(a, b), jnp.maximum(c, d), jnp.minimum(c, d)
        top1 = jnp.maximum(hab, hcd)
        top2 = jnp.maximum(jnp.maximum(lab, lcd), jnp.minimum(hab, hcd))
        score = top1 + top2
        if grp == 0:
            best_score, best = score, jnp.zeros(score.shape, jnp.int32)
        else:
            better = score > best_score
            best = jnp.where(better, grp, best)
            best_score = jnp.where(better, score, best_score)

    def pick(a, i):
        out = row(a, i)
        for grp in range(1, N_EXPERT_GROUPS):
            out = jnp.where(best == grp, row(a, EXPERTS_PER_GROUP * grp + i), out)
        return out

    s_in = [pick(sel, i) for i in range(EXPERTS_PER_GROUP)]
    p_in = [pick(probs, i) for i in range(EXPERTS_PER_GROUP)]
    neg = jnp.full(s_in[0].shape, -jnp.inf, F32)

    def argmax_first(vals):
        idx, val = jnp.zeros(vals[0].shape, jnp.int32), vals[0]
        for i in range(1, len(vals)):
            better = vals[i] > val
            idx = jnp.where(better, i, idx)
            val = jnp.where(better, vals[i], val)
        return idx

    i1 = argmax_first(s_in)
    i2 = argmax_first([jnp.where(i1 == i, neg, s_in[i]) for i in range(EXPERTS_PER_GROUP)])
    zero = jnp.zeros(p_in[0].shape, F32)
    g1 = sum(jnp.where(i1 == i, p_in[i], zero) for i in range(EXPERTS_PER_GROUP))
    g2 = sum(jnp.where(i2 == i, p_in[i], zero) for i in range(EXPERTS_PER_GROUP))
    tot = g1 + g2
    w1, w2 = g1 / tot, g2 / tot
    first_low = i1 < i2
    low, high = jnp.minimum(i1, i2), jnp.maximum(i1, i2)
    w_low, w_high = jnp.where(first_low, w1, w2), jnp.where(first_low, w2, w1)
    pos = jnp.where(low == 0, high - 1, jnp.where(low == 1, jnp.where(high == 2, 4, 3), 5))
    swap = low == 2
    bucket = best * PAIRS_PER_GROUP + pos
    return jnp.concatenate([jnp.where(swap, w_high, w_low), jnp.where(swap, w_low, w_high)], axis=0), bucket


def _router_logits(w_t, h):
    w_hi = w_t.astype(BF16)
    w_r = w_t - w_hi.astype(F32)
    w_mid = w_r.astype(BF16)
    w_lo = (w_r - w_mid.astype(F32)).astype(BF16)
    h_hi = h.astype(BF16)
    h_lo = (h - h_hi.astype(F32)).astype(BF16)
    dims = (((1,), (1,)), ((), ()))
    a = lax.dot_general(jnp.concatenate([w_hi, w_mid, w_lo], axis=0), h_hi, dims, preferred_element_type=F32)
    b = lax.dot_general(jnp.concatenate([w_hi, w_mid], axis=0), h_lo, dims, preferred_element_type=F32)
    e = w_t.shape[0]
    return a[:e] + a[e:2 * e] + a[2 * e:] + b[:e] + b[e:]


def _post_kernel(x_hbm, at_ref, yc_ref, wglu_ref, gs_ref, wo_ref, g1_ref, n2_ref, sc_ref, sh_ref,
                 wrt_ref, rb_ref, x1_ref, h2_ref, cw_ref, gid_ref, y_scr, wglu_s, wo_s, x_ring, x_sem, *, n_tiles):
    x = _ring_tile(x_hbm, x_ring, x_sem, n_tiles)

    @pl.when(pl.program_id(0) == 0)
    def _():
        wglu_s[...] = wglu_ref[...].astype(BF16)
        wo_s[...] = wo_ref[...].astype(BF16)

    nchunk = y_scr.shape[1] // SSM_CHUNK
    for s in range(SSM_CHUNK):
        for j in range(SSM_COLS):
            y_scr[j, pl.ds(s, nchunk, stride=SSM_CHUNK), :] = yc_ref[j, :, s * LANES:(s + 1) * LANES].astype(F32)
    yg = jax.nn.gelu(jnp.concatenate([y_scr[j] for j in range(SSM_COLS)], axis=1))
    z = yg * jax.nn.sigmoid(jnp.dot(yg.astype(BF16), wglu_s[...], preferred_element_type=F32))
    zn = _rms(z, gs_ref[...]).astype(BF16)
    o = (jnp.dot(at_ref[...], wo_s[:ATTN_WIDTH, :], preferred_element_type=F32)
         + jnp.dot(zn, wo_s[ATTN_WIDTH:, :], preferred_element_type=F32))
    x1 = x + g1_ref[...] * o
    x1_ref[...] = x1
    h2 = _rms(x1, n2_ref[...] * (1.0 + sc_ref[...])) + sh_ref[...]
    h2_ref[...] = h2
    logits = _router_logits(wrt_ref[...], h2)
    cw, bucket = _route(logits, rb_ref[...])
    cw_ref[...] = cw
    gid_ref[...] = bucket


def _post(x, attn, yc, w_glu, ssm_g, w_out, layer, g1, n2g, sc2, sh2, w_router_t, router_bias, seq):
    t, d = x.shape
    tm = min(TOKEN_TILE, seq)
    per_b = seq // tm
    tok = lambda w: pl.BlockSpec((tm, w), lambda i: (i, 0))
    const = lambda a: pl.BlockSpec(a.shape, lambda i: (0,) * a.ndim)
    per_batch = pl.BlockSpec((None, 1, d), lambda i: (i // per_b, 0, 0))
    col = lambda r: pl.BlockSpec((r, tm), lambda i: (0, i))
    of_layer = lambda a: pl.BlockSpec((None,) + a.shape[1:], lambda i: (layer, 0, 0), pipeline_mode=pl.Buffered(1))
    chunked = pl.BlockSpec((SSM_COLS, tm // SSM_CHUNK, CHUNK_LANES), lambda i: (0, i, 0))
    return pl.pallas_call(
        functools.partial(_post_kernel, n_tiles=t // tm),
        grid=(t // tm,),
        in_specs=[pl.BlockSpec(memory_space=pl.ANY), tok(ATTN_WIDTH), chunked, of_layer(w_glu), const(ssm_g),
                  of_layer(w_out), per_batch, const(n2g), per_batch, per_batch, const(w_router_t), const(router_bias)],
        out_specs=[tok(d), tok(d), col(2), col(1)],
        out_shape=[jax.ShapeDtypeStruct((t, d), F32), jax.ShapeDtypeStruct((t, d), F32),
                   jax.ShapeDtypeStruct((2, t), F32), jax.ShapeDtypeStruct((1, t), jnp.int32)],
        scratch_shapes=[pltpu.VMEM((SSM_COLS, tm, LANES), F32), pltpu.VMEM(w_glu.shape[1:], BF16),
                        pltpu.VMEM(w_out.shape[1:], BF16),
                        pltpu.VMEM((RING_SLOTS, tm, d), F32), pltpu.SemaphoreType.DMA((RING_SLOTS,))],
        compiler_params=_params(("arbitrary",), BIG_VMEM_LIMIT),
        name="post_mix",
    )(x, attn, yc, w_glu, ssm_g, w_out, g1, n2g, sc2, sh2, w_router_t, router_bias)


def _moe_kernel(kind_ref, rb_ref, bk_ref, pa_ref, pb_ref, first_ref, cast_ref, cpos_ref, pe_ref, offs_ref,
                x_ref, cw_ref, wg_ref, wu_ref, wd_ref, o_ref, wg_s, wu_s, wd_s):
    s = pl.program_id(0)

    @pl.when(cast_ref[s] == 1)
    def _():
        slot = cpos_ref[s]
        wg_s[slot] = wg_ref[...].astype(BF16)
        wu_s[slot] = wu_ref[...].astype(BF16)
        wd_s[slot] = wd_ref[...].astype(BF16)

    @pl.when(kind_ref[s] == STEP_ITEM)
    def _():
        bucket = bk_ref[s]
        base = rb_ref[s] * MOE_ROWS
        lo_row, hi_row = offs_ref[bucket] - base, offs_ref[bucket + 1] - base
        slots = (pa_ref[s], pb_ref[s])
        is_first = first_ref[s] == 1

        def run(r0, r1, z0, z1):
            rows = r0 + lax.broadcasted_iota(jnp.int32, (r1 - r0, 1), 0)
            cw = jnp.where((rows >= lo_row) & (rows < hi_row), cw_ref[r0:r1, :], 0.0)
            x = x_ref[r0:r1, :].astype(BF16)
            y = None
            for k in range(2):
                gate = jnp.dot(x, wg_s[slots[k]], preferred_element_type=F32)
                up = jnp.dot(x, wu_s[slots[k]], preferred_element_type=F32)
                act = (gate * jax.nn.sigmoid(gate) * up * cw[:, k:k + 1]).astype(BF16)
                yk = jnp.dot(act, wd_s[slots[k]], preferred_element_type=F32)
                y = yk if y is None else y + yk

            @pl.when(is_first)
            def _():
                o_ref[r0:r1, :] = y
                for a, b in ((z0, r0), (r1, z1)):
                    if b > a:
                        o_ref[a:b, :] = jnp.zeros((b - a, o_ref.shape[1]), F32)

            @pl.when(jnp.logical_not(is_first))
            def _():
                o_ref[r0:r1, :] += y

        for b0 in range(0, MOE_ROWS, MOE_SUB):
            b1, mid = b0 + MOE_SUB, b0 + MOE_SUB // 2
            has_rows = (lo_row < b1) & (hi_row > b0)
            needs_lower, needs_upper = has_rows & (lo_row < mid), has_rows & (hi_row > mid)
            pl.when(needs_lower & needs_upper)(lambda: run(b0, b1, b0, b1))
            pl.when(needs_lower & jnp.logical_not(needs_upper))(lambda: run(b0, mid, b0, b1))
            pl.when(jnp.logical_not(needs_lower) & needs_upper)(lambda: run(mid, b1, b0, b1))
            if MOE_ROWS > MOE_SUB:
                @pl.when(jnp.logical_not(has_rows) & is_first)
                def _():
                    o_ref[b0:b1, :] = jnp.zeros((MOE_SUB, o_ref.shape[1]), F32)


def _moe_plan_kernel(offs_ref, kind_ref, rb_ref, bk_ref, pa_ref, pb_ref, first_ref, cast_ref, cpos_ref, pe_ref,
                     irb, ibk, *, n_steps_max):
    i32 = jnp.int32
    ng, epg, ppg = N_EXPERT_GROUPS, EXPERTS_PER_GROUP, PAIRS_PER_GROUP

    shift = MOE_ROWS.bit_length() - 1

    def bucket_body(bk, cnt):
        a, b = offs_ref[bk], offs_ref[bk + 1]
        first_blk = lax.shift_right_logical(a, shift)
        n_blk = jnp.where(b > a, lax.shift_right_logical(b - 1, shift) - first_blk + 1, 0)

        def block_body(j, cnt):
            irb[cnt] = first_blk + j
            ibk[cnt] = bk
            return cnt + 1

        return lax.fori_loop(0, n_blk, block_body, cnt)

    n_items = lax.fori_loop(0, ng * ppg, bucket_body, i32(0))

    def count_body(i, m):
        g = ibk[i] // ppg
        return tuple(m[k] + (g == k).astype(i32) for k in range(ng))

    m = lax.fori_loop(0, n_items, count_body, (i32(0),) * ng)

    def next_group(g):
        nxt = i32(-1)
        for k in range(ng - 1, 0, -1):
            nxt = jnp.where((k > g) & (m[k] > 0), k, nxt)
        return nxt

    def emit(s, kind, rb, bk, pa, pb, first, cast, cpos, pe):
        kind_ref[s], rb_ref[s], bk_ref[s], pa_ref[s], pb_ref[s] = kind, rb, bk, pa, pb
        first_ref[s], cast_ref[s], cpos_ref[s], pe_ref[s] = first, cast, cpos, pe

    def item_body(i, carry):
        s, gcur, parity, q, last_pe, last_rb = carry
        rb, bk = irb[i], ibk[i]
        g, pos = bk // ppg, bk % ppg
        new = g != gcur
        started = gcur >= 0
        loaders = jnp.where(new, jnp.where(started, jnp.maximum(epg - q, 0), epg), 0)
        parity = jnp.where(new & started, 1 - parity, parity)
        q = jnp.where(new, 0, q)
        for j in range(epg):
            on = j >= epg - loaders
            emit(s, STEP_LOAD, rb, bk, 0, 0, 0, 1, parity * epg + j, epg * g + j)
            last_pe = jnp.where(on, epg * g + j, last_pe)
            s = s + on.astype(i32)
        nxt = next_group(g)
        pre = (q < epg) & (nxt >= 0)
        pe = jnp.where(pre, epg * nxt + q, last_pe)
        slot_a, slot_b = i32(PAIR_SLOTS[0][0]), i32(PAIR_SLOTS[0][1])
        for p in range(1, ppg):
            slot_a = jnp.where(pos == p, PAIR_SLOTS[p][0], slot_a)
            slot_b = jnp.where(pos == p, PAIR_SLOTS[p][1], slot_b)
        emit(s, STEP_ITEM, rb, bk, parity * epg + slot_a, parity * epg + slot_b, (rb != last_rb).astype(i32),
             pre.astype(i32), (1 - parity) * epg + q, pe)
        return s + 1, g, parity, q + 1, pe, rb

    s, _, _, _, last_pe, last_rb = lax.fori_loop(
        0, n_items, item_body, (i32(0), i32(-1), i32(0), i32(0), i32(0), i32(-1)))
    last_bk = ibk[jnp.maximum(n_items - 1, 0)]

    def pad_body(s, _):
        emit(s, STEP_PAD, last_rb, last_bk, 0, 0, 0, 0, 0, last_pe)
        return 0

    lax.fori_loop(s, n_steps_max, pad_body, 0)


def _moe_steps(bucket, t):
    i32 = jnp.int32
    nbk = N_EXPERT_GROUPS * PAIRS_PER_GROUP
    order = jnp.argsort(bucket, stable=True).astype(i32)
    counts = jnp.sum((bucket[None, :] == jnp.arange(nbk, dtype=i32)[:, None]).astype(i32), axis=1)
    offs = jnp.concatenate([jnp.zeros((1,), i32), jnp.cumsum(counts).astype(i32)])
    assert MOE_ROWS & (MOE_ROWS - 1) == 0
    n_items_max = t // MOE_ROWS + nbk - 1
    n_steps_max = n_items_max + N_EXPERTS
    smem = pl.BlockSpec(memory_space=pltpu.SMEM)
    tables = pl.pallas_call(
        functools.partial(_moe_plan_kernel, n_steps_max=n_steps_max),
        in_specs=[smem],
        out_specs=[smem] * 9,
        out_shape=[jax.ShapeDtypeStruct((n_steps_max,), i32)] * 9,
        scratch_shapes=[pltpu.SMEM((n_items_max + 1,), i32)] * 2,
        name="moe_plan",
    )(offs)
    return order, (*tables, offs), n_steps_max


def _moe(xs, cws, w_gate, w_up, w_down, layer, tables, n_steps_max):
    t, d = xs.shape
    ff = w_gate.shape[3]
    w_map = lambda s, kind, rb, bk, pa, pb, fi, ca, cp, pe, of: (layer, pe[s], 0, 0)
    row_map = lambda s, kind, rb, *_: (rb[s], 0)
    nres = 2 * EXPERTS_PER_GROUP
    grid_spec = pltpu.PrefetchScalarGridSpec(
        num_scalar_prefetch=len(tables),
        grid=(n_steps_max,),
        in_specs=[pl.BlockSpec((MOE_ROWS, d), row_map), pl.BlockSpec((MOE_ROWS, 2), row_map),
                  pl.BlockSpec((None, None, d, ff), w_map), pl.BlockSpec((None, None, d, ff), w_map),
                  pl.BlockSpec((None, None, ff, d), w_map)],
        out_specs=pl.BlockSpec((MOE_ROWS, d), row_map),
        scratch_shapes=[pltpu.VMEM((nres, d, ff), BF16), pltpu.VMEM((nres, d, ff), BF16),
                        pltpu.VMEM((nres, ff, d), BF16)],
    )
    return pl.pallas_call(
        _moe_kernel,
        grid_spec=grid_spec,
        out_shape=jax.ShapeDtypeStruct((t, d), F32),
        compiler_params=_params(("arbitrary",), BIG_VMEM_LIMIT),
        name="moe_grouped",
    )(*tables, xs, cws, w_gate, w_up, w_down)


def _take_rows(a, idx):
    return a.at[idx].get(mode="promise_in_bounds", unique_indices=True)


def _final_kernel(x_ref, y_ref, g_ref, o_ref):
    o_ref[...] = x_ref[...] + g_ref[...] * y_ref[...]


def _final(x1, y, g2, seq):
    t, d = x1.shape
    tm = min(TOKEN_TILE, seq)
    per_b = seq // tm
    tok = lambda w: pl.BlockSpec((tm, w), lambda i: (i, 0))
    return pl.pallas_call(
        _final_kernel,
        grid=(t // tm,),
        in_specs=[tok(d), tok(d), pl.BlockSpec((None, 1, d), lambda i: (i // per_b, 0, 0))],
        out_specs=tok(d),
        out_shape=jax.ShapeDtypeStruct((t, d), F32),
        compiler_params=_params(("parallel",)),
        name="final_residual",
    )(x1, y, g2)


def kernel(x, c, positions, ada_w, ada_b, norm1_g, w_in, q_norm_g, k_norm_g, attn_sink, lam_re, lam_im, ssm_b_re, ssm_b_im, ssm_c_re, ssm_c_im, ssm_d, ssm_log_dt, w_glu, attn_out_g, ssm_out_g, w_out, norm2_g, w_router, router_bias, w_exp_gate, w_exp_up, w_exp_down):
    batch, seq, d = x.shape
    depth = ada_w.shape[0]
    t = batch * seq
    assert seq % ATTN_BLOCK == 0 and seq % SSM_CHUNK == 0 and t % MOE_ROWS == 0

    mod = _adaln_mod(c, ada_w, ada_b).reshape(depth, 6, batch, 1, d)
    cos, sin = _rope_tables(positions)
    head_sum, rot = _rope_constants()
    bias = _attn_bias()
    w_router_t = w_router.T
    s5_mats = _s5_prep(lam_re, lam_im, ssm_b_re, ssm_b_im, ssm_c_re, ssm_c_im, ssm_d, ssm_log_dt)
    router_bias_col = router_bias.reshape(N_EXPERTS, 1)

    xf = x.reshape(t, d)
    res = None
    for l in range(depth):
        sh1, sc1, g1, sh2, sc2, g2 = (mod[l, j] for j in range(6))
        qg = (jnp.tile(q_norm_g[l], N_Q_HEADS) * (HEAD_DIM ** -0.5 * LOG2_E)).reshape(1, ATTN_WIDTH)
        kg = jnp.tile(k_norm_g[l], N_KV_HEADS).reshape(1, KV_WIDTH)
        outs = _inproj(xf, res, sc1, sh1, norm1_g[l].reshape(1, d), w_in, l, qg, kg, head_sum, rot, cos, sin, seq)
        if res is None:
            q, kx, vx, uc = outs
        else:
            q, kx, vx, uc, xf = outs
        attn = _attention(q, kx, vx, attn_sink[l], attn_out_g[l].reshape(1, ATTN_WIDTH), bias, batch, seq)
        yc = _s5_scan(uc, s5_mats, l, seq // SSM_CHUNK, batch)
        x1, h2, cw, gid = _post(xf, attn, yc, w_glu, ssm_out_g[l].reshape(1, SSM_WIDTH), w_out, l, g1,
                                norm2_g[l].reshape(1, d), sc2, sh2, w_router_t, router_bias_col, seq)
        order, tables, n_steps_max = _moe_steps(gid.reshape(t), t)
        y_sorted = _moe(_take_rows(h2, order), _take_rows(cw.T, order), w_exp_gate, w_exp_up, w_exp_down, l,
                        tables, n_steps_max)
        y = _take_rows(y_sorted, jnp.argsort(order).astype(jnp.int32))
        xf, res = x1, (y, g2)
    y, g2 = res
    return _final(xf, y, g2, seq).reshape(batch, seq, d)
```

```python
import functools
import math

import numpy as np
import jax
import jax.numpy as jnp
from jax import lax
from jax.experimental import pallas as pl
from jax.experimental.pallas import tpu as pltpu

F32 = jnp.float32
BF16 = jnp.bfloat16

HEAD_DIM = 64
N_Q_HEADS = 8
N_KV_HEADS = 2
Q_PER_KV = N_Q_HEADS // N_KV_HEADS
ATTN_WIDTH = N_Q_HEADS * HEAD_DIM
KV_WIDTH = N_KV_HEADS * HEAD_DIM
ATTN_BLOCK = 128
ATTN_Q_TILE = 2048
ROPE_THETA = 10000.0
ROPE_SLAB = 256
LANES = 128
SSM_GROUP_CH = 16
SSM_GROUPS = 32
SSM_WIDTH = SSM_GROUPS * SSM_GROUP_CH
SSM_STATE = 64
SSM_CHUNK = 16
SSM_COLS = SSM_WIDTH // LANES
COL_GROUPS = LANES // SSM_GROUP_CH
COL_STATE = COL_GROUPS * SSM_STATE
CHUNK_LANES = SSM_CHUNK * LANES
SSM_NSPLIT = 4
S5_ROW_PAD = 8
N_EXPERTS = 16
N_EXPERT_GROUPS = 4
EXPERTS_PER_GROUP = N_EXPERTS // N_EXPERT_GROUPS
PAIRS_PER_GROUP = EXPERTS_PER_GROUP * (EXPERTS_PER_GROUP - 1) // 2
PAIR_SLOTS = ((0, 1), (0, 2), (0, 3), (1, 3), (1, 2), (3, 2))
EPS = 1e-6
LOG2_E = math.log2(math.e)
MASK_BIAS = -1e30

TOKEN_TILE = 1024
MOE_ROWS = 512
MOE_SUB = 256
VMEM_LIMIT = 48 * 1024 * 1024
BIG_VMEM_LIMIT = 56 * 1024 * 1024
STEP_PAD, STEP_LOAD, STEP_ITEM = 0, 1, 2


def _params(sem, vmem=None):
    return pltpu.CompilerParams(dimension_semantics=sem, vmem_limit_bytes=vmem)


def _rms(x, g):
    return x * lax.rsqrt(jnp.mean(x * x, axis=-1, keepdims=True) + EPS) * g


def _mod_kernel(c_ref, w_ref, b_ref, o_ref):
    c = c_ref[...]
    s = c * jax.nn.sigmoid(c)
    o_ref[...] = jnp.dot(s.astype(BF16), w_ref[...].astype(BF16), preferred_element_type=F32) + b_ref[...]


def _adaln_mod(c, ada_w, ada_b):
    depth, d, d6 = ada_w.shape
    nb = c.shape[0]
    n6 = d6 // d
    return pl.pallas_call(
        _mod_kernel,
        grid=(depth, n6),
        in_specs=[pl.BlockSpec((nb, d), lambda l, j: (0, 0)),
                  pl.BlockSpec((None, d, d), lambda l, j: (l, 0, j)),
                  pl.BlockSpec((None, None, 1, d), lambda l, j: (l, j, 0, 0))],
        out_specs=pl.BlockSpec((None, None, nb, d), lambda l, j: (l, j, 0, 0)),
        out_shape=jax.ShapeDtypeStruct((depth, n6, nb, d), F32),
        compiler_params=_params(("arbitrary", "arbitrary"), VMEM_LIMIT),
        name="adaln_mod",
    )(c, ada_w, ada_b.reshape(depth, n6, 1, d))


def _spread(x, expander3):
    hi = x.astype(BF16)
    r1 = x - hi.astype(F32)
    mid = r1.astype(BF16)
    lo = (r1 - mid.astype(F32)).astype(BF16)
    return jnp.dot(jnp.concatenate([hi, mid, lo], axis=1), expander3, preferred_element_type=F32)


def _rope_kernel(pos_ref, freq_ref, lane_ref, quarter_ref, cos_ref, sin_ref):
    per_row = pos_ref.shape[1]
    rows = pos_ref.shape[0]
    pos = _spread(pos_ref[...].astype(F32), lane_ref[...])
    ang = pos * freq_ref[...]
    cos, sin = jnp.cos(ang), jnp.sin(ang)
    for j in range(per_row):
        cos_ref[pl.ds(j, rows, stride=per_row), :] = _spread(cos, quarter_ref[j])
        sin_ref[pl.ds(j, rows, stride=per_row), :] = _spread(sin, quarter_ref[j])


def _rope_tables(positions):
    half = HEAD_DIM // 2
    t = positions.size
    per_row = LANES // half
    rows = t // per_row
    freq = (ROPE_THETA ** (-np.arange(half, dtype=np.float64) / half)).astype(np.float32)
    freq_row = jnp.asarray(np.tile(freq, per_row)[None, :])
    to_quarter = np.repeat(np.eye(per_row, dtype=np.float32), half, axis=1)
    spread = np.zeros((per_row, LANES, LANES), np.float32)
    for j in range(per_row):
        spread[j, j * half:(j + 1) * half, :] = np.tile(np.eye(half, dtype=np.float32), (1, per_row))
    lane_sel = jnp.asarray(np.tile(to_quarter, (3, 1)), BF16)
    quarter_sel = jnp.asarray(np.tile(spread, (1, 3, 1)), BF16)
    blk = min(rows, 512)
    out = pl.BlockSpec((blk * per_row, LANES), lambda i: (i, 0))
    const = lambda a: pl.BlockSpec(a.shape, lambda i: (0,) * a.ndim)
    return pl.pallas_call(
        _rope_kernel,
        grid=(rows // blk,),
        in_specs=[pl.BlockSpec((blk, per_row), lambda i: (i, 0)), const(freq_row), const(lane_sel), const(quarter_sel)],
        out_specs=[out, out],
        out_shape=[jax.ShapeDtypeStruct((t, LANES), F32)] * 2,
        compiler_params=_params(("arbitrary",)),
        name="rope_tables",
    )(positions.reshape(rows, per_row), freq_row, lane_sel, quarter_sel)


def _rope_constants():
    lane = np.arange(ROPE_SLAB)
    head_sum = (lane[:, None] // HEAD_DIM == lane[None, :] // HEAD_DIM).astype(np.float32)
    half = HEAD_DIM // 2
    rot = np.zeros((ROPE_SLAB, ROPE_SLAB), np.float32)
    for d in range(ROPE_SLAB):
        if d % HEAD_DIM < half:
            rot[d + half, d] = -1.0
        else:
            rot[d - half, d] = 1.0
    return jnp.asarray(head_sum, BF16), jnp.asarray(rot, BF16)


RING_SLOTS = 3


def _ring_tile(x_hbm, ring, sem, n_tiles):
    i = pl.program_id(0)
    tm = ring.shape[1]

    def copy(j):
        slot = j % RING_SLOTS
        return pltpu.make_async_copy(x_hbm.at[pl.ds(j * tm, tm), :], ring.at[slot], sem.at[slot])

    @pl.when(i == 0)
    def _():
        for j in range(min(RING_SLOTS - 1, n_tiles)):
            copy(j).start()

    @pl.when(i + (RING_SLOTS - 1) < n_tiles)
    def _():
        copy(i + (RING_SLOTS - 1)).start()

    copy(i).wait()
    return ring[i % RING_SLOTS]


def _inproj_kernel(*refs, has_res, n_tiles):
    if has_res:
        (x_hbm, y_ref, g2_ref, sc_ref, sh_ref, n1_ref, w_ref, qg_ref, kg_ref, hs_ref, rot_ref,
         cos_ref, sin_ref, q_ref, k_ref, v_ref, uc_ref, xo_ref, u_scr, w_s, x_ring, x_sem) = refs
        x = _ring_tile(x_hbm, x_ring, x_sem, n_tiles) + g2_ref[...] * y_ref[...]
        xo_ref[...] = x
    else:
        (x_hbm, sc_ref, sh_ref, n1_ref, w_ref, qg_ref, kg_ref, hs_ref, rot_ref,
         cos_ref, sin_ref, q_ref, k_ref, v_ref, uc_ref, u_scr, w_s, x_ring, x_sem) = refs
        x = _ring_tile(x_hbm, x_ring, x_sem, n_tiles)

    @pl.when(pl.program_id(0) == 0)
    def _():
        w_s[...] = w_ref[...].astype(BF16)

    h = _rms(x, n1_ref[...] * (1.0 + sc_ref[...])) + sh_ref[...]
    proj = jnp.dot(h.astype(BF16), w_s[...], preferred_element_type=F32)
    q = proj[:, :ATTN_WIDTH]
    k = proj[:, ATTN_WIDTH:ATTN_WIDTH + KV_WIDTH]
    v = proj[:, ATTN_WIDTH + KV_WIDTH:ATTN_WIDTH + 2 * KV_WIDTH]
    cos = cos_ref[...]
    sin = sin_ref[...]
    reps = ATTN_WIDTH // LANES
    cos_q = jnp.concatenate([cos] * reps, axis=1)
    sin_q = jnp.concatenate([sin] * reps, axis=1)

    def head_norm_rope(t, gain, c, s):
        outs = []
        for lo in range(0, t.shape[1], ROPE_SLAB):
            wd = min(ROPE_SLAB, t.shape[1] - lo)
            ts, lanes = t[:, lo:lo + wd], slice(lo, lo + wd)
            ssq = jnp.dot((ts * ts).astype(BF16), hs_ref[:wd, :wd], preferred_element_type=F32)
            tn = (ts * lax.rsqrt(ssq * (1.0 / HEAD_DIM) + EPS) * gain[:, lanes]).astype(BF16)
            tr = jnp.dot(tn, rot_ref[:wd, :wd], preferred_element_type=F32)
            outs.append(tn.astype(F32) * c[:, lanes] + tr * s[:, lanes])
        return outs[0] if len(outs) == 1 else jnp.concatenate(outs, axis=1)

    qo = head_norm_rope(q, qg_ref[...], cos_q, sin_q)
    ko = head_norm_rope(k, kg_ref[...], cos, sin)
    q_ref[...] = qo.astype(BF16)
    k_ref[...] = jnp.concatenate([ko, pltpu.roll(ko, HEAD_DIM, axis=1)], axis=1).astype(BF16)
    v_ref[...] = jnp.concatenate([v, pltpu.roll(v, HEAD_DIM, axis=1)], axis=1).astype(BF16)
    u0 = ATTN_WIDTH + 2 * KV_WIDTH
    nchunk = u_scr.shape[1] // SSM_CHUNK
    for j in range(SSM_COLS):
        u_scr[j] = proj[:, u0 + j * LANES:u0 + (j + 1) * LANES]
    for s in range(SSM_CHUNK):
        for j in range(SSM_COLS):
            lanes = slice(s * SSM_WIDTH + j * LANES, s * SSM_WIDTH + (j + 1) * LANES)
            uc_ref[:, lanes] = u_scr[j, pl.ds(s, nchunk, stride=SSM_CHUNK), :].astype(BF16)


def _inproj(x, res, sc1, sh1, n1g, w_in, layer, qg, kg, head_sum, rot, cos, sin, seq):
    t, d = x.shape
    tm = min(TOKEN_TILE, seq)
    per_b = seq // tm
    in_width = w_in.shape[2]
    tok = lambda w: pl.BlockSpec((tm, w), lambda i: (i, 0))
    const = lambda a: pl.BlockSpec(a.shape, lambda i: (0,) * a.ndim)
    per_batch = pl.BlockSpec((None, 1, d), lambda i: (i // per_b, 0, 0))
    chunked = pl.BlockSpec((tm // SSM_CHUNK, SSM_CHUNK * SSM_WIDTH), lambda i: (i, 0))
    ins, specs = [x], [pl.BlockSpec(memory_space=pl.ANY)]
    if res is not None:
        y_prev, g2_prev = res
        ins += [y_prev, g2_prev]
        specs += [tok(d), per_batch]
    ins += [sc1, sh1, n1g, w_in, qg, kg, head_sum, rot, cos, sin]
    specs += [per_batch, per_batch, const(n1g),
              pl.BlockSpec((None, d, in_width), lambda i: (layer, 0, 0), pipeline_mode=pl.Buffered(1)),
              const(qg), const(kg), const(head_sum), const(rot), tok(LANES), tok(LANES)]
    out_shape = [jax.ShapeDtypeStruct((t, ATTN_WIDTH), BF16), jax.ShapeDtypeStruct((t, 2 * KV_WIDTH), BF16),
                 jax.ShapeDtypeStruct((t, 2 * KV_WIDTH), BF16),
                 jax.ShapeDtypeStruct((t // SSM_CHUNK, SSM_CHUNK * SSM_WIDTH), BF16)]
    out_specs = [tok(ATTN_WIDTH), tok(2 * KV_WIDTH), tok(2 * KV_WIDTH), chunked]
    if res is not None:
        out_shape.append(jax.ShapeDtypeStruct((t, d), F32))
        out_specs.append(tok(d))
    assert in_width == ATTN_WIDTH + 2 * KV_WIDTH + SSM_WIDTH
    return pl.pallas_call(
        functools.partial(_inproj_kernel, has_res=res is not None, n_tiles=t // tm),
        grid=(t // tm,),
        in_specs=specs,
        out_specs=out_specs,
        out_shape=out_shape,
        scratch_shapes=[pltpu.VMEM((SSM_COLS, tm, LANES), F32), pltpu.VMEM((d, in_width), BF16),
                        pltpu.VMEM((RING_SLOTS, tm, d), F32), pltpu.SemaphoreType.DMA((RING_SLOTS,))],
        compiler_params=_params(("arbitrary",), BIG_VMEM_LIMIT),
        name="inproj",
    )(*ins)


def _attn_kernel(sink_ref, q_ref, kc_ref, kp_ref, vc_ref, vp_ref, bias_ref, g_ref, o_ref):
    nsub = q_ref.shape[0] // ATTN_BLOCK
    kk = jnp.concatenate([kp_ref[...], kc_ref[...]], axis=0)
    vv = jnp.concatenate([vp_ref[...], vc_ref[...]], axis=0)
    low = lax.broadcasted_iota(jnp.int32, (kk.shape[0], KV_WIDTH), 1) < HEAD_DIM
    zero = jnp.zeros((kk.shape[0], KV_WIDTH), BF16)

    def variants(a):
        nat, swp = a[:, :KV_WIDTH], a[:, KV_WIDTH:]
        return {(0, 0): jnp.where(low, nat, zero), (0, 1): jnp.where(low, zero, swp),
                (1, 0): jnp.where(low, swp, zero), (1, 1): jnp.where(low, zero, nat)}

    kvar, vvar = variants(kk), variants(vv)
    band = bias_ref[1]
    first = bias_ref[jnp.minimum(pl.program_id(1), 1)]
    upper = lax.broadcasted_iota(jnp.int32, (2 * ATTN_BLOCK, 1), 0) < ATTN_BLOCK
    for j in range(nsub):
        bias = first if j == 0 else band
        bias2 = jnp.concatenate([bias, bias], axis=0)
        keys = slice(j * ATTN_BLOCK, (j + 2) * ATTN_BLOCK)
        qrows = slice(j * ATTN_BLOCK, (j + 1) * ATTN_BLOCK)
        tiles = [None] * (N_Q_HEADS // 2)
        for kv in range(N_KV_HEADS):
            for half in range(2):
                pairs = (2 * kv, 2 * kv + 1)
                heads = (2 * pairs[0] + half, 2 * pairs[1] + half)
                qs = jnp.concatenate([q_ref[qrows, p * LANES:(p + 1) * LANES] for p in pairs], axis=0)
                s = lax.dot_general(qs, kvar[(kv, half)][keys], (((1,), (1,)), ((), ())),
                                    preferred_element_type=F32) + bias2
                sink = jnp.where(upper, sink_ref[heads[0]], sink_ref[heads[1]]) * LOG2_E
                m = jnp.maximum(jnp.max(s, axis=-1, keepdims=True), sink)
                p = jnp.exp2(s - m)
                denom = jnp.sum(p, axis=-1, keepdims=True) + jnp.exp2(sink - m)
                o = jnp.dot(p.astype(BF16), vvar[(kv, half)][keys], preferred_element_type=F32) * (1.0 / denom)
                for r, pr in enumerate(pairs):
                    part = o[r * ATTN_BLOCK:(r + 1) * ATTN_BLOCK]
                    tiles[pr] = part if tiles[pr] is None else tiles[pr] + part
        a = jnp.concatenate(tiles, axis=1)
        o_ref[qrows, :] = _rms(a, g_ref[...]).astype(BF16)


def _attn_bias():
    qi = np.arange(ATTN_BLOCK)[:, None]
    sj = np.arange(2 * ATTN_BLOCK)[None, :]
    diff = qi + ATTN_BLOCK - sj
    band = (diff >= 0) & (diff < ATTN_BLOCK)
    first = band & (sj >= ATTN_BLOCK)
    return jnp.asarray(np.where(np.stack([first, band]), 0.0, MASK_BIAS).astype(np.float32))


def _attention(q, kx, vx, sink, out_g, bias, batch, seq):
    t = q.shape[0]
    qb = min(ATTN_Q_TILE, seq)
    nsub = qb // ATTN_BLOCK
    nq = seq // qb
    nb = seq // ATTN_BLOCK
    cur = lambda w: pl.BlockSpec((qb, w), lambda b, n, s: (b * nq + n, 0))
    prev = lambda w: pl.BlockSpec((ATTN_BLOCK, w), lambda b, n, s: (b * nb + jnp.maximum(n * nsub - 1, 0), 0))
    grid_spec = pltpu.PrefetchScalarGridSpec(
        num_scalar_prefetch=1,
        grid=(batch, nq),
        in_specs=[cur(ATTN_WIDTH), cur(2 * KV_WIDTH), prev(2 * KV_WIDTH), cur(2 * KV_WIDTH), prev(2 * KV_WIDTH),
                  pl.BlockSpec(bias.shape, lambda b, n, s: (0, 0, 0)),
                  pl.BlockSpec((1, ATTN_WIDTH), lambda b, n, s: (0, 0))],
        out_specs=cur(ATTN_WIDTH),
    )
    return pl.pallas_call(
        _attn_kernel,
        grid_spec=grid_spec,
        out_shape=jax.ShapeDtypeStruct((t, ATTN_WIDTH), BF16),
        compiler_params=_params(("parallel", "arbitrary")),
        name="swa_attention",
    )(sink, q, kx, kx, vx, vx, bias, out_g)


def _s5_prep_kernel(lr_re_ref, lr_im_ref, ldt_ref, bt_re_ref, bt_im_ref, ct_re_ref, ct_im_ref,
                    d_ref, lcol_re_ref, lcol_im_ref, ldtcol_ref, exp_ref, exph_ref, expt_ref, expw_ref,
                    t_ref, w_ref, v_ref, la_ref, lb_ref):
    hi = lax.Precision.HIGHEST
    nl = SSM_CHUNK
    low = lax.broadcasted_iota(jnp.int32, (1, 2 * SSM_STATE), 1) < SSM_STATE
    row_low = lax.broadcasted_iota(jnp.int32, (2 * SSM_STATE, 1), 0) < SSM_STATE
    jcol = lax.broadcasted_iota(jnp.int32, (nl, 1), 0).astype(F32)
    kt_lane = lax.broadcasted_iota(jnp.int32, (SSM_GROUP_CH, nl * SSM_GROUP_CH), 1)
    kt_row = lax.broadcasted_iota(jnp.int32, (SSM_GROUP_CH, nl * SSM_GROUP_CH), 0)

    w_all, v_all, kt_all = [], [], []
    for gm in range(COL_GROUPS):
        dt = jnp.exp(ldt_ref[gm])
        lam_re, lam_im = lr_re_ref[gm], lr_im_ref[gm]
        a_r, th_r = lam_re * dt, lam_im * dt

        er = jnp.exp(jcol * a_r)
        pw_re, pw_im = er * jnp.cos(jcol * th_r), er * jnp.sin(jcol * th_r)

        nr, ni = pw_re[1:2, :] - 1.0, pw_im[1:2, :]
        den = lam_re * lam_re + lam_im * lam_im
        c_re, c_im = (nr * lam_re + ni * lam_im) / den, (ni * lam_re - nr * lam_im) / den
        bt_re, bt_im = bt_re_ref[gm], bt_im_ref[gm]
        bb_re, bb_im = c_re * bt_re - c_im * bt_im, c_re * bt_im + c_im * bt_re

        w_rows = []
        for s in range(nl):
            j = nl - 1 - s
            pr, pi = pw_re[j:j + 1, :], pw_im[j:j + 1, :]
            w_rows.append(jnp.where(low, pr * bb_re - pi * bb_im, pr * bb_im + pi * bb_re))
        w_all.append(w_rows)

        pw_re_t, pw_im_t = pw_re.T, pw_im.T
        pc, ps = _spread(pw_re_t, exp_ref[...]), _spread(pw_im_t, exp_ref[...])
        ct_re, ct_im = _spread(ct_re_ref[gm], exph_ref[...]), _spread(ct_im_ref[gm], exph_ref[...])
        a_re, a_im = ct_re * pc - ct_im * ps, ct_re * ps + ct_im * pc
        a_cat = jnp.where(row_low, a_re, -a_im)
        l1_re, l1_im = pw_re_t[:, 1:2], pw_im_t[:, 1:2]
        v_re, v_im = a_re * l1_re - a_im * l1_im, a_re * l1_im + a_im * l1_re
        v_all.append(jnp.where(row_low, v_re, -v_im))

        kt = jnp.dot(jnp.where(low, bb_re, bb_im), a_cat, precision=hi, preferred_element_type=F32)
        kt_all.append(kt + jnp.where(kt_lane == kt_row, d_ref[gm], 0.0))

    def same_group(shape, row_group, lane_group):
        r = lax.broadcasted_iota(jnp.int32, shape, 0)
        c = lax.broadcasted_iota(jnp.int32, shape, 1)
        return (row_group(r) == lane_group(c)).astype(F32)

    chan_group = lambda i: (i >> 4) & (COL_GROUPS - 1)
    state_group = lambda i: (i >> 6) & (COL_GROUPS - 1)
    over_steps = lambda m: jnp.concatenate([m] * nl, axis=1)

    kt_wide = jnp.dot(jnp.concatenate(kt_all, axis=0).astype(BF16), expt_ref[...], preferred_element_type=F32)
    bd = (kt_wide * over_steps(same_group((LANES, LANES), chan_group, chan_group))).astype(BF16)
    t_ref[0:LANES, :] = bd
    for s in range(1, nl):
        t_ref[s * LANES:(s + 1) * LANES, :] = jnp.concatenate(
            [jnp.zeros((LANES, s * LANES), BF16), bd[:, :CHUNK_LANES - s * LANES]], axis=1)

    w_stack = jnp.concatenate([w_all[gm][s] for s in range(nl) for gm in range(COL_GROUPS)], axis=0)
    w_wide = jnp.dot(w_stack.astype(BF16), expw_ref[...], preferred_element_type=F32)
    w_mask = same_group((LANES, 2 * COL_STATE), chan_group, state_group)
    w_ref[...] = (w_wide.reshape(nl, LANES, 2 * COL_STATE) * w_mask[None]).reshape(CHUNK_LANES, 2 * COL_STATE).astype(BF16)

    v_stack = jnp.concatenate([v_all[gm][half * SSM_STATE:(half + 1) * SSM_STATE, :]
                               for half in range(2) for gm in range(COL_GROUPS)], axis=0)
    v_wide = jnp.dot(v_stack.astype(BF16), expt_ref[...], preferred_element_type=F32)
    v_ref[...] = (v_wide * over_steps(same_group((2 * COL_STATE, LANES), state_group, chan_group))).astype(BF16)

    dtc = jnp.exp(ldtcol_ref[...])
    e16 = jnp.exp(nl * lcol_re_ref[...] * dtc)
    ang = nl * lcol_im_ref[...] * dtc
    la_ref[...] = e16 * jnp.cos(ang)
    lb_ref[...] = e16 * jnp.sin(ang)


def _s5_prep(lam_re, lam_im, b_re, b_im, c_re, c_im, d_skip, log_dt):
    g, p, h, nl = SSM_GROUPS, SSM_STATE, SSM_GROUP_CH, SSM_CHUNK
    cg = COL_GROUPS
    nc = lam_re.shape[0] * SSM_COLS
    col = lambda a: a.reshape((nc, cg) + a.shape[2:])
    dup_row = lambda a: col(jnp.tile(a, (1, 1, 2))[:, :, None, :])
    bt = lambda a: col(jnp.tile(jnp.swapaxes(a, 2, 3), (1, 1, 1, 2)))
    ct = lambda a: col(jnp.tile(jnp.swapaxes(a, 2, 3), (1, 1, 2, 1)))
    d_pad = col(jnp.pad(d_skip.reshape(-1, g, 1, h), ((0, 0), (0, 0), (0, 0), (0, nl * h - h))))
    wide = lambda a: a.reshape(nc, 1, cg * p)
    expand = jnp.asarray(np.tile(np.repeat(np.eye(nl, dtype=np.float32), h, axis=1), (3, 1)), BF16)
    expand_h = jnp.asarray(np.tile(np.eye(h, dtype=np.float32), (3, nl)), BF16)
    exp_t = np.zeros((nl, h, nl, cg, h), np.float32)
    exp_w = np.zeros((2, p, 2, cg, p), np.float32)
    for gm in range(cg):
        exp_t[:, :, :, gm, :] = np.eye(nl * h, dtype=np.float32).reshape(nl, h, nl, h)
        exp_w[:, :, :, gm, :] = np.eye(2 * p, dtype=np.float32).reshape(2, p, 2, p)
    exp_t = jnp.asarray(exp_t.reshape(nl * h, CHUNK_LANES), BF16)
    exp_w = jnp.asarray(exp_w.reshape(2 * p, 2 * COL_STATE), BF16)
    blk = lambda *s: pl.BlockSpec((None,) + s, lambda i: (i,) + (0,) * len(s))
    const = lambda a: pl.BlockSpec(a.shape, lambda i: (0,) * a.ndim)
    lw = nl * h
    return pl.pallas_call(
        _s5_prep_kernel,
        grid=(nc,),
        in_specs=[blk(cg, 1, 2 * p), blk(cg, 1, 2 * p), blk(cg, 1, 1),
                  blk(cg, h, 2 * p), blk(cg, h, 2 * p), blk(cg, 2 * p, h), blk(cg, 2 * p, h), blk(cg, 1, lw),
                  blk(1, cg * p), blk(1, cg * p), blk(1, cg * p), const(expand), const(expand_h), const(exp_t),
                  const(exp_w)],
        out_specs=[blk(CHUNK_LANES, CHUNK_LANES), blk(CHUNK_LANES, 2 * COL_STATE), blk(2 * COL_STATE, CHUNK_LANES),
                   blk(1, COL_STATE), blk(1, COL_STATE)],
        out_shape=[jax.ShapeDtypeStruct((nc, CHUNK_LANES, CHUNK_LANES), BF16),
                   jax.ShapeDtypeStruct((nc, CHUNK_LANES, 2 * COL_STATE), BF16),
                   jax.ShapeDtypeStruct((nc, 2 * COL_STATE, CHUNK_LANES), BF16),
                   jax.ShapeDtypeStruct((nc, 1, COL_STATE), F32), jax.ShapeDtypeStruct((nc, 1, COL_STATE), F32)],
        compiler_params=_params(("parallel",), VMEM_LIMIT),
        name="s5_prep",
    )(dup_row(lam_re), dup_row(lam_im), col(log_dt[:, :, None, None]),
      bt(b_re), bt(b_im), ct(c_re), ct(c_im), d_pad, wide(lam_re), wide(lam_im),
      wide(jnp.repeat(log_dt, p, axis=1)), expand, expand_h, exp_t, exp_w)


def _s5_kernel(*refs, nchunks, nb):
    uc_refs = refs[:SSM_CHUNK]
    t_ref, w_ref, v_ref, la_ref, lb_ref, o_ref, ucat_ref, s_ref, xp_ref = refs[SSM_CHUNK:]

    @pl.when(pl.program_id(1) == 0)
    def _():
        for s in range(SSM_CHUNK):
            ucat_ref[:, s * LANES:(s + 1) * LANES] = uc_refs[s][...]
        s_in = jnp.dot(ucat_ref[...], w_ref[...], preferred_element_type=F32)
        nblk = COL_STATE // LANES
        pitch = s_ref.shape[1] // nb
        for b in range(2 * nblk):
            for q in range(nb):
                s_ref[b, q * pitch:q * pitch + nchunks, :] = s_in[q * nchunks:(q + 1) * nchunks, b * LANES:(b + 1) * LANES]
        lr = [jnp.broadcast_to(la_ref[:, b * LANES:(b + 1) * LANES], (nb, LANES)) for b in range(nblk)]
        li = [jnp.broadcast_to(lb_ref[:, b * LANES:(b + 1) * LANES], (nb, LANES)) for b in range(nblk)]

        def step(c, carry):
            rows = pl.ds(c, nb, stride=pitch)
            out = []
            for b in range(nblk):
                re, im = carry[2 * b], carry[2 * b + 1]
                xp_ref[b, rows, :] = re
                xp_ref[nblk + b, rows, :] = im
                out.append(lr[b] * re - li[b] * im + s_ref[b, rows, :])
                out.append(lr[b] * im + li[b] * re + s_ref[nblk + b, rows, :])
            return tuple(out)

        zero = jnp.zeros((nb, LANES), F32)
        lax.fori_loop(0, nchunks, step, (zero,) * (2 * nblk), unroll=4)

    pitch = xp_ref.shape[1] // nb
    xp = jnp.concatenate(
        [jnp.concatenate([xp_ref[b, q * pitch:q * pitch + nchunks, :] for q in range(nb)], axis=0)
         for b in range(2 * COL_STATE // LANES)], axis=1).astype(BF16)
    inter = jnp.dot(xp, v_ref[...], preferred_element_type=F32)
    for kk in range(SSM_NSPLIT):
        @pl.when(pl.program_id(1) == kk)
        def _():
            live = (kk + 1) * (CHUNK_LANES // SSM_NSPLIT)
            intra = jnp.dot(ucat_ref[:, :live], t_ref[:live, :], preferred_element_type=F32)
            o_ref[...] = (intra + inter).astype(BF16)


def _s5_scan(uc, mats, layer, nchunks, nb):
    rows = uc.shape[0]
    c0 = layer * SSM_COLS
    split = CHUNK_LANES // SSM_NSPLIT
    u_spec = lambda s: pl.BlockSpec((rows, LANES), lambda j, k: (0, SSM_COLS * s + j))
    return pl.pallas_call(
        functools.partial(_s5_kernel, nchunks=nchunks, nb=nb),
        grid=(SSM_COLS, SSM_NSPLIT),
        in_specs=[u_spec(s) for s in range(SSM_CHUNK)] + [
            pl.BlockSpec((None, CHUNK_LANES, split), lambda j, k: (c0 + j, 0, k)),
            pl.BlockSpec((None, CHUNK_LANES, 2 * COL_STATE), lambda j, k: (c0 + j, 0, 0)),
            pl.BlockSpec((None, 2 * COL_STATE, split), lambda j, k: (c0 + j, 0, k)),
            pl.BlockSpec((None, 1, COL_STATE), lambda j, k: (c0 + j, 0, 0)),
            pl.BlockSpec((None, 1, COL_STATE), lambda j, k: (c0 + j, 0, 0))],
        out_specs=pl.BlockSpec((None, rows, split), lambda j, k: (j, 0, k)),
        out_shape=jax.ShapeDtypeStruct((SSM_COLS, rows, CHUNK_LANES), BF16),
        scratch_shapes=[pltpu.VMEM((rows, CHUNK_LANES), BF16),
                        pltpu.VMEM((2 * COL_STATE // LANES, nb * (nchunks + S5_ROW_PAD), LANES), F32),
                        pltpu.VMEM((2 * COL_STATE // LANES, nb * (nchunks + S5_ROW_PAD), LANES), F32)],
        compiler_params=_params(("parallel", "arbitrary"), VMEM_LIMIT),
        name="s5_scan",
    )(*([uc] * SSM_CHUNK), *mats)


def _route(logits, bias):
    m = jnp.max(logits, axis=0, keepdims=True)
    e = jnp.exp(logits - m)
    probs = e / jnp.sum(e, axis=0, keepdims=True)
    sel = probs + bias
    row = lambda a, i: a[i:i + 1, :]
    best_score, best = None, None
    for grp in range(N_EXPERT_GROUPS):
        a, b, c, d = (row(sel, EXPERTS_PER_GROUP * grp + i) for i in range(EXPERTS_PER_GROUP))
        hab, lab, hcd, lcd = jnp.maximum(a, b), jnp.minimum(a, b), jnp.maximum(c, d), jnp.minimum(c, d)
        top1 = jnp.maximum(hab, hcd)
        top2 = jnp.maximum(jnp.maximum(lab, lcd), jnp.minimum(hab, hcd))
        score = top1 + top2
        if grp == 0:
            best_score, best = score, jnp.zeros(score.shape, jnp.int32)
        else:
            better = score > best_score
            best = jnp.where(better, grp, best)
            best_score = jnp.where(better, score, best_score)

    def pick(a, i):
        out = row(a, i)
        for grp in range(1, N_EXPERT_GROUPS):
            out = jnp.where(best == grp, row(a, EXPERTS_PER_GROUP * grp + i), out)
        return out

    s_in = [pick(sel, i) for i in range(EXPERTS_PER_GROUP)]
    p_in = [pick(probs, i) for i in range(EXPERTS_PER_GROUP)]
    neg = jnp.full(s_in[0].shape, -jnp.inf, F32)

    def argmax_first(vals):
        idx, val = jnp.zeros(vals[0].shape, jnp.int32), vals[0]
        for i in range(1, len(vals)):
            better = vals[i] > val
            idx = jnp.where(better, i, idx)
            val = jnp.where(better, vals[i], val)
        return idx

    i1 = argmax_first(s_in)
    i2 = argmax_first([jnp.where(i1 == i, neg, s_in[i]) for i in range(EXPERTS_PER_GROUP)])
    zero = jnp.zeros(p_in[0].shape, F32)
    g1 = sum(jnp.where(i1 == i, p_in[i], zero) for i in range(EXPERTS_PER_GROUP))
    g2 = sum(jnp.where(i2 == i, p_in[i], zero) for i in range(EXPERTS_PER_GROUP))
    tot = g1 + g2
    w1, w2 = g1 / tot, g2 / tot
    first_low = i1 < i2
    low, high = jnp.minimum(i1, i2), jnp.maximum(i1, i2)
    w_low, w_high = jnp.where(first_low, w1, w2), jnp.where(first_low, w2, w1)
    pos = jnp.where(low == 0, high - 1, jnp.where(low == 1, jnp.where(high == 2, 4, 3), 5))
    swap = low == 2
    bucket = best * PAIRS_PER_GROUP + pos
    return jnp.concatenate([jnp.where(swap, w_high, w_low), jnp.where(swap, w_low, w_high)], axis=0), bucket


def _router_logits(w_t, h):
    w_hi = w_t.astype(BF16)
    w_r = w_t - w_hi.astype(F32)
    w_mid = w_r.astype(BF16)
    w_lo = (w_r - w_mid.astype(F32)).astype(BF16)
    h_hi = h.astype(BF16)
    h_lo = (h - h_hi.astype(F32)).astype(BF16)
    dims = (((1,), (1,)), ((), ()))
    a = lax.dot_general(jnp.concatenate([w_hi, w_mid, w_lo], axis=0), h_hi, dims, preferred_element_type=F32)
    b = lax.dot_general(jnp.concatenate([w_hi, w_mid], axis=0), h_lo, dims, preferred_element_type=F32)
    e = w_t.shape[0]
    return a[:e] + a[e:2 * e] + a[2 * e:] + b[:e] + b[e:]


def _post_kernel(x_ref, at_ref, yc_ref, wglu_ref, gs_ref, wo_ref, g1_ref, n2_ref, sc_ref, sh_ref,
                 wrt_ref, rb_ref, x1_ref, h2_ref, cw_ref, gid_ref, y_scr, wglu_s, wo_s):
    @pl.when(pl.program_id(0) == 0)
    def _():
        wglu_s[...] = wglu_ref[...].astype(BF16)
        wo_s[...] = wo_ref[...].astype(BF16)

    nchunk = y_scr.shape[1] // SSM_CHUNK
    for s in range(SSM_CHUNK):
        for j in range(SSM_COLS):
            y_scr[j, pl.ds(s, nchunk, stride=SSM_CHUNK), :] = yc_ref[j, :, s * LANES:(s + 1) * LANES].astype(F32)
    yg = jax.nn.gelu(jnp.concatenate([y_scr[j] for j in range(SSM_COLS)], axis=1))
    z = yg * jax.nn.sigmoid(jnp.dot(yg.astype(BF16), wglu_s[...], preferred_element_type=F32))
    zn = _rms(z, gs_ref[...]).astype(BF16)
    o = (jnp.dot(at_ref[...], wo_s[:ATTN_WIDTH, :], preferred_element_type=F32)
         + jnp.dot(zn, wo_s[ATTN_WIDTH:, :], preferred_element_type=F32))
    x1 = x_ref[...] + g1_ref[...] * o
    x1_ref[...] = x1
    h2 = _rms(x1, n2_ref[...] * (1.0 + sc_ref[...])) + sh_ref[...]
    h2_ref[...] = h2
    logits = _router_logits(wrt_ref[...], h2)
    cw, bucket = _route(logits, rb_ref[...])
    cw_ref[...] = cw
    gid_ref[...] = bucket


def _post(x, attn, yc, w_glu, ssm_g, w_out, layer, g1, n2g, sc2, sh2, w_router_t, router_bias, seq):
    t, d = x.shape
    tm = min(TOKEN_TILE, seq)
    per_b = seq // tm
    tok = lambda w: pl.BlockSpec((tm, w), lambda i: (i, 0))
    const = lambda a: pl.BlockSpec(a.shape, lambda i: (0,) * a.ndim)
    per_batch = pl.BlockSpec((None, 1, d), lambda i: (i // per_b, 0, 0))
    col = lambda r: pl.BlockSpec((r, tm), lambda i: (0, i))
    of_layer = lambda a: pl.BlockSpec((None,) + a.shape[1:], lambda i: (layer, 0, 0))
    chunked = pl.BlockSpec((SSM_COLS, tm // SSM_CHUNK, CHUNK_LANES), lambda i: (0, i, 0))
    return pl.pallas_call(
        _post_kernel,
        grid=(t // tm,),
        in_specs=[tok(d), tok(ATTN_WIDTH), chunked, of_layer(w_glu), const(ssm_g),
                  of_layer(w_out), per_batch, const(n2g), per_batch, per_batch, const(w_router_t), const(router_bias)],
        out_specs=[tok(d), tok(d), col(2), col(1)],
        out_shape=[jax.ShapeDtypeStruct((t, d), F32), jax.ShapeDtypeStruct((t, d), F32),
                   jax.ShapeDtypeStruct((2, t), F32), jax.ShapeDtypeStruct((1, t), jnp.int32)],
        scratch_shapes=[pltpu.VMEM((SSM_COLS, tm, LANES), F32), pltpu.VMEM(w_glu.shape[1:], BF16),
                        pltpu.VMEM(w_out.shape[1:], BF16)],
        compiler_params=_params(("arbitrary",), VMEM_LIMIT),
        name="post_mix",
    )(x, attn, yc, w_glu, ssm_g, w_out, g1, n2g, sc2, sh2, w_router_t, router_bias)


def _moe_kernel(kind_ref, rb_ref, bk_ref, pa_ref, pb_ref, first_ref, cast_ref, cpos_ref, pe_ref, offs_ref,
                x_ref, cw_ref, wg_ref, wu_ref, wd_ref, o_ref, wg_s, wu_s, wd_s):
    s = pl.program_id(0)

    @pl.when(cast_ref[s] == 1)
    def _():
        slot = cpos_ref[s]
        wg_s[slot] = wg_ref[...].astype(BF16)
        wu_s[slot] = wu_ref[...].astype(BF16)
        wd_s[slot] = wd_ref[...].astype(BF16)

    @pl.when(kind_ref[s] == STEP_ITEM)
    def _():
        bucket = bk_ref[s]
        base = rb_ref[s] * MOE_ROWS
        lo_row, hi_row = offs_ref[bucket] - base, offs_ref[bucket + 1] - base
        slots = (pa_ref[s], pb_ref[s])
        is_first = first_ref[s] == 1

        def run(r0, r1, z0, z1):
            rows = r0 + lax.broadcasted_iota(jnp.int32, (r1 - r0, 1), 0)
            cw = jnp.where((rows >= lo_row) & (rows < hi_row), cw_ref[r0:r1, :], 0.0)
            x = x_ref[r0:r1, :].astype(BF16)
            y = None
            for k in range(2):
                gate = jnp.dot(x, wg_s[slots[k]], preferred_element_type=F32)
                up = jnp.dot(x, wu_s[slots[k]], preferred_element_type=F32)
                act = (gate * jax.nn.sigmoid(gate) * up * cw[:, k:k + 1]).astype(BF16)
                yk = jnp.dot(act, wd_s[slots[k]], preferred_element_type=F32)
                y = yk if y is None else y + yk

            @pl.when(is_first)
            def _():
                o_ref[r0:r1, :] = y
                for a, b in ((z0, r0), (r1, z1)):
                    if b > a:
                        o_ref[a:b, :] = jnp.zeros((b - a, o_ref.shape[1]), F32)

            @pl.when(jnp.logical_not(is_first))
            def _():
                o_ref[r0:r1, :] += y

        for b0 in range(0, MOE_ROWS, MOE_SUB):
            b1, mid = b0 + MOE_SUB, b0 + MOE_SUB // 2
            has_rows = (lo_row < b1) & (hi_row > b0)
            needs_lower, needs_upper = has_rows & (lo_row < mid), has_rows & (hi_row > mid)
            pl.when(needs_lower & needs_upper)(lambda: run(b0, b1, b0, b1))
            pl.when(needs_lower & jnp.logical_not(needs_upper))(lambda: run(b0, mid, b0, b1))
            pl.when(jnp.logical_not(needs_lower) & needs_upper)(lambda: run(mid, b1, b0, b1))
            if MOE_ROWS > MOE_SUB:
                @pl.when(jnp.logical_not(has_rows) & is_first)
                def _():
                    o_ref[b0:b1, :] = jnp.zeros((MOE_SUB, o_ref.shape[1]), F32)


def _moe_plan_kernel(offs_ref, kind_ref, rb_ref, bk_ref, pa_ref, pb_ref, first_ref, cast_ref, cpos_ref, pe_ref,
                     irb, ibk, *, n_steps_max):
    i32 = jnp.int32
    ng, epg, ppg = N_EXPERT_GROUPS, EXPERTS_PER_GROUP, PAIRS_PER_GROUP

    shift = MOE_ROWS.bit_length() - 1

    def bucket_body(bk, cnt):
        a, b = offs_ref[bk], offs_ref[bk + 1]
        first_blk = lax.shift_right_logical(a, shift)
        n_blk = jnp.where(b > a, lax.shift_right_logical(b - 1, shift) - first_blk + 1, 0)

        def block_body(j, cnt):
            irb[cnt] = first_blk + j
            ibk[cnt] = bk
            return cnt + 1

        return lax.fori_loop(0, n_blk, block_body, cnt)

    n_items = lax.fori_loop(0, ng * ppg, bucket_body, i32(0))

    def count_body(i, m):
        g = ibk[i] // ppg
        return tuple(m[k] + (g == k).astype(i32) for k in range(ng))

    m = lax.fori_loop(0, n_items, count_body, (i32(0),) * ng)

    def next_group(g):
        nxt = i32(-1)
        for k in range(ng - 1, 0, -1):
            nxt = jnp.where((k > g) & (m[k] > 0), k, nxt)
        return nxt

    def emit(s, kind, rb, bk, pa, pb, first, cast, cpos, pe):
        kind_ref[s], rb_ref[s], bk_ref[s], pa_ref[s], pb_ref[s] = kind, rb, bk, pa, pb
        first_ref[s], cast_ref[s], cpos_ref[s], pe_ref[s] = first, cast, cpos, pe

    def item_body(i, carry):
        s, gcur, parity, q, last_pe, last_rb = carry
        rb, bk = irb[i], ibk[i]
        g, pos = bk // ppg, bk % ppg
        new = g != gcur
        started = gcur >= 0
        loaders = jnp.where(new, jnp.where(started, jnp.maximum(epg - q, 0), epg), 0)
        parity = jnp.where(new & started, 1 - parity, parity)
        q = jnp.where(new, 0, q)
        for j in range(epg):
            on = j >= epg - loaders
            emit(s, STEP_LOAD, rb, bk, 0, 0, 0, 1, parity * epg + j, epg * g + j)
            last_pe = jnp.where(on, epg * g + j, last_pe)
            s = s + on.astype(i32)
        nxt = next_group(g)
        pre = (q < epg) & (nxt >= 0)
        pe = jnp.where(pre, epg * nxt + q, last_pe)
        slot_a, slot_b = i32(PAIR_SLOTS[0][0]), i32(PAIR_SLOTS[0][1])
        for p in range(1, ppg):
            slot_a = jnp.where(pos == p, PAIR_SLOTS[p][0], slot_a)
            slot_b = jnp.where(pos == p, PAIR_SLOTS[p][1], slot_b)
        emit(s, STEP_ITEM, rb, bk, parity * epg + slot_a, parity * epg + slot_b, (rb != last_rb).astype(i32),
             pre.astype(i32), (1 - parity) * epg + q, pe)
        return s + 1, g, parity, q + 1, pe, rb

    s, _, _, _, last_pe, last_rb = lax.fori_loop(
        0, n_items, item_body, (i32(0), i32(-1), i32(0), i32(0), i32(0), i32(-1)))
    last_bk = ibk[jnp.maximum(n_items - 1, 0)]

    def pad_body(s, _):
        emit(s, STEP_PAD, last_rb, last_bk, 0, 0, 0, 0, 0, last_pe)
        return 0

    lax.fori_loop(s, n_steps_max, pad_body, 0)


def _moe_steps(bucket, t):
    i32 = jnp.int32
    nbk = N_EXPERT_GROUPS * PAIRS_PER_GROUP
    order = jnp.argsort(bucket, stable=True).astype(i32)
    counts = jnp.sum((bucket[None, :] == jnp.arange(nbk, dtype=i32)[:, None]).astype(i32), axis=1)
    offs = jnp.concatenate([jnp.zeros((1,), i32), jnp.cumsum(counts).astype(i32)])
    assert MOE_ROWS & (MOE_ROWS - 1) == 0
    n_items_max = t // MOE_ROWS + nbk - 1
    n_steps_max = n_items_max + N_EXPERTS
    smem = pl.BlockSpec(memory_space=pltpu.SMEM)
    tables = pl.pallas_call(
        functools.partial(_moe_plan_kernel, n_steps_max=n_steps_max),
        in_specs=[smem],
        out_specs=[smem] * 9,
        out_shape=[jax.ShapeDtypeStruct((n_steps_max,), i32)] * 9,
        scratch_shapes=[pltpu.SMEM((n_items_max + 1,), i32)] * 2,
        name="moe_plan",
    )(offs)
    return order, (*tables, offs), n_steps_max


def _moe(xs, cws, w_gate, w_up, w_down, layer, tables, n_steps_max):
    t, d = xs.shape
    ff = w_gate.shape[3]
    w_map = lambda s, kind, rb, bk, pa, pb, fi, ca, cp, pe, of: (layer, pe[s], 0, 0)
    row_map = lambda s, kind, rb, *_: (rb[s], 0)
    nres = 2 * EXPERTS_PER_GROUP
    grid_spec = pltpu.PrefetchScalarGridSpec(
        num_scalar_prefetch=len(tables),
        grid=(n_steps_max,),
        in_specs=[pl.BlockSpec((MOE_ROWS, d), row_map), pl.BlockSpec((MOE_ROWS, 2), row_map),
                  pl.BlockSpec((None, None, d, ff), w_map), pl.BlockSpec((None, None, d, ff), w_map),
                  pl.BlockSpec((None, None, ff, d), w_map)],
        out_specs=pl.BlockSpec((MOE_ROWS, d), row_map),
        scratch_shapes=[pltpu.VMEM((nres, d, ff), BF16), pltpu.VMEM((nres, d, ff), BF16),
                        pltpu.VMEM((nres, ff, d), BF16)],
    )
    return pl.pallas_call(
        _moe_kernel,
        grid_spec=grid_spec,
        out_shape=jax.ShapeDtypeStruct((t, d), F32),
        compiler_params=_params(("arbitrary",), BIG_VMEM_LIMIT),
        name="moe_grouped",
    )(*tables, xs, cws, w_gate, w_up, w_down)


def _take_rows(a, idx):
    return a.at[idx].get(mode="promise_in_bounds", unique_indices=True)


def _final_kernel(x_ref, y_ref, g_ref, o_ref):
    o_ref[...] = x_ref[...] + g_ref[...] * y_ref[...]


def _final(x1, y, g2, seq):
    t, d = x1.shape
    tm = min(TOKEN_TILE, seq)
    per_b = seq // tm
    tok = lambda w: pl.BlockSpec((tm, w), lambda i: (i, 0))
    return pl.pallas_call(
        _final_kernel,
        grid=(t // tm,),
        in_specs=[tok(d), tok(d), pl.BlockSpec((None, 1, d), lambda i: (i // per_b, 0, 0))],
        out_specs=tok(d),
        out_shape=jax.ShapeDtypeStruct((t, d), F32),
        compiler_params=_params(("parallel",)),
        name="final_residual",
    )(x1, y, g2)


def kernel(x, c, positions, ada_w, ada_b, norm1_g, w_in, q_norm_g, k_norm_g, attn_sink, lam_re, lam_im, ssm_b_re, ssm_b_im, ssm_c_re, ssm_c_im, ssm_d, ssm_log_dt, w_glu, attn_out_g, ssm_out_g, w_out, norm2_g, w_router, router_bias, w_exp_gate, w_exp_up, w_exp_down):
    batch, seq, d = x.shape
    depth = ada_w.shape[0]
    t = batch * seq
    assert seq % ATTN_BLOCK == 0 and seq % SSM_CHUNK == 0 and t % MOE_ROWS == 0

    mod = _adaln_mod(c, ada_w, ada_b).reshape(depth, 6, batch, 1, d)
    cos, sin = _rope_tables(positions)
    head_sum, rot = _rope_constants()
    bias = _attn_bias()
    w_router_t = w_router.T
    s5_mats = _s5_prep(lam_re, lam_im, ssm_b_re, ssm_b_im, ssm_c_re, ssm_c_im, ssm_d, ssm_log_dt)
    router_bias_col = router_bias.reshape(N_EXPERTS, 1)

    xf = x.reshape(t, d)
    res = None
    for l in range(depth):
        sh1, sc1, g1, sh2, sc2, g2 = (mod[l, j] for j in range(6))
        qg = (jnp.tile(q_norm_g[l], N_Q_HEADS) * (HEAD_DIM ** -0.5 * LOG2_E)).reshape(1, ATTN_WIDTH)
        kg = jnp.tile(k_norm_g[l], N_KV_HEADS).reshape(1, KV_WIDTH)
        outs = _inproj(xf, res, sc1, sh1, norm1_g[l].reshape(1, d), w_in, l, qg, kg, head_sum, rot, cos, sin, seq)
        if res is None:
            q, kx, vx, uc = outs
        else:
            q, kx, vx, uc, xf = outs
        attn = _attention(q, kx, vx, attn_sink[l], attn_out_g[l].reshape(1, ATTN_WIDTH), bias, batch, seq)
        yc = _s5_scan(uc, s5_mats, l, seq // SSM_CHUNK, batch)
        x1, h2, cw, gid = _post(xf, attn, yc, w_glu, ssm_out_g[l].reshape(1, SSM_WIDTH), w_out, l, g1,
                                norm2_g[l].reshape(1, d), sc2, sh2, w_router_t, router_bias_col, seq)
        order, tables, n_steps_max = _moe_steps(gid.reshape(t), t)
        y_sorted = _moe(_take_rows(h2, order), _take_rows(cw.T, order), w_exp_gate, w_exp_up, w_exp_down, l,
                        tables, n_steps_max)
        y = _take_rows(y_sorted, jnp.argsort(order).astype(jnp.int32))
        xf, res = x1, (y, g2)
    y, g2 = res
    return _final(xf, y, g2, seq).reshape(batch, seq, d)
```
